```python
import math
import jax
import jax.numpy as jnp
from jax import lax
import numpy as np

D_MODEL = 1024
BATCH = 32
SEQ = 2048
DEPTH = 2

GRID_W = 64
CTX_LEN = 256
N_MIXERS = 2
N_A_LAYERS = (DEPTH + N_MIXERS - 1) // N_MIXERS
N_B_LAYERS = DEPTH // N_MIXERS
N_DIR = 2
N_MOD = 6
NORM_EPS = 1e-6

S5_GROUP = 16
S5_GROUPS = D_MODEL // S5_GROUP
S5_STATE = 64
S5_DT_MIN = 1e-3
S5_DT_MAX = 1e-1
S5_LAM_RE_MAX = -1e-4

HGRN_HEAD_K = 128
HGRN_HEADS = D_MODEL // HGRN_HEAD_K
HGRN_HEAD_V = D_MODEL // HGRN_HEADS
HGRN_CHUNK = 32
HGRN_N_PROJ = 5

D_FF = ((8 * D_MODEL // 3 + 127) // 128) * 128
CONV_WIDTH = 3

kernel_name = 'hybrid_s5_hgrn2_prefix_flow_block'


def rms_norm(x, w):
    x32 = x.astype(jnp.float32)
    y = x32 * lax.rsqrt(jnp.mean(x32 * x32, axis=-1, keepdims=True) + NORM_EPS)
    return (y * w.astype(jnp.float32)).astype(x.dtype)


def modulate(h, shift, scale):
    return h * (1.0 + scale[:, None, :]) + shift[:, None, :]


def _dwconv(u, w, b, axis):
    n = u.shape[axis]
    half = CONV_WIDTH // 2
    pad = [(0, 0)] * u.ndim
    pad[axis] = (half, half)
    up = jnp.pad(u, pad)
    out = b
    for j in range(CONV_WIDTH):
        out = out + lax.slice_in_dim(up, j, j + n, axis=axis) * w[j]
    return out


def conv_ffn(h, w_up, conv_w, conv_b, w_down, rows):
    u = h @ w_up
    bn, length, f2 = u.shape
    if rows is None:
        u = _dwconv(u, conv_w, conv_b, axis=1)
    else:
        u = _dwconv(u.reshape(bn, rows, GRID_W, f2), conv_w, conv_b, axis=2).reshape(bn, length, f2)
    a, g = jnp.split(u, 2, axis=-1)
    return (jax.nn.silu(a) * g) @ w_down


def _cplx_combine(e1, e2):
    a1r, a1i, b1r, b1i = e1
    a2r, a2i, b2r, b2i = e2
    return (a2r * a1r - a2i * a1i,
            a2r * a1i + a2i * a1r,
            a2r * b1r - a2i * b1i + b2r,
            a2r * b1i + a2i * b1r + b2i)


def _s5_discretise(lam_re, lam_im, log_step, b_re, b_im):
    lr = jnp.minimum(lam_re.astype(jnp.float32), S5_LAM_RE_MAX)
    li = lam_im.astype(jnp.float32)
    dt = jnp.exp(log_step.astype(jnp.float32))[:, None]
    mag = jnp.exp(lr * dt)
    abar_r = mag * jnp.cos(li * dt)
    abar_i = mag * jnp.sin(li * dt)
    den = lr * lr + li * li
    nr = abar_r - 1.0
    coef_r = (nr * lr + abar_i * li) / den
    coef_i = (abar_i * lr - nr * li) / den
    br = b_re.astype(jnp.float32)
    bi = b_im.astype(jnp.float32)
    bbar_r = coef_r[..., None] * br - coef_i[..., None] * bi
    bbar_i = coef_r[..., None] * bi + coef_i[..., None] * br
    return abar_r, abar_i, bbar_r, bbar_i


def _s5_scan(u, s0_r, s0_i, abar_r, abar_i, bbar_r, bbar_i, reverse):
    length = u.shape[1]
    bu_r = jnp.einsum('blgh,gph->blgp', u, bbar_r)
    bu_i = jnp.einsum('blgh,gph->blgp', u, bbar_i)
    a_r = jnp.broadcast_to(abar_r[None, None], (1, length) + abar_r.shape)
    a_i = jnp.broadcast_to(abar_i[None, None], (1, length) + abar_i.shape)
    acr, aci, sr, si = lax.associative_scan(_cplx_combine, (a_r, a_i, bu_r, bu_i), reverse=reverse, axis=1)
    s0r = s0_r[:, None]
    s0i = s0_i[:, None]
    sr = sr + acr * s0r - aci * s0i
    si = si + acr * s0i + aci * s0r
    end = 0 if reverse else length - 1
    return sr, si, sr[:, end], si[:, end]


def _s5_readout(sr, si, c_re, c_im):
    return (jnp.einsum('blgp,ghp->blgh', sr, c_re.astype(jnp.float32))
            - jnp.einsum('blgp,ghp->blgh', si, c_im.astype(jnp.float32)))


def s5_mixer(h_ctx, h_lat, w_in, lam_re, lam_im, log_step, b_re, b_im, c_re, c_im, d_skip, w_glu, w_out, need_ctx_out):
    dtype = h_lat.dtype
    bn = h_lat.shape[0]

    def inputs(h):
        return (h @ w_in).astype(jnp.float32).reshape(h.shape[0], h.shape[1], S5_GROUPS, S5_GROUP)

    def glu_out(y):
        z = jax.nn.gelu(y.reshape(y.shape[0], y.shape[1], D_MODEL)).astype(dtype)
        z = z * jax.nn.sigmoid(z @ w_glu)
        return z @ w_out

    u_ctx = inputs(h_ctx)
    u_lat = inputs(h_lat)
    dsk = d_skip.astype(jnp.float32).reshape(S5_GROUPS, S5_GROUP)
    zeros = jnp.zeros((bn, S5_GROUPS, S5_STATE), jnp.float32)
    y_lat = dsk * u_lat
    y_ctx = dsk * u_ctx if need_ctx_out else None
    for d in range(N_DIR):
        rev = d == 1
        disc = _s5_discretise(lam_re[d], lam_im[d], log_step[d], b_re[d], b_im[d])
        sr, si, fr, fi = _s5_scan(u_ctx, zeros, zeros, *disc, reverse=rev)
        if need_ctx_out:
            y_ctx = y_ctx + _s5_readout(sr, si, c_re[d], c_im[d])
        sr, si, _, _ = _s5_scan(u_lat, fr, fi, *disc, reverse=rev)
        y_lat = y_lat + _s5_readout(sr, si, c_re[d], c_im[d])
    out_ctx = glu_out(y_ctx) if need_ctx_out else None
    return out_ctx, glu_out(y_lat)


def _hgrn2_scan(q, k, v, logf, s0, reverse):
    if reverse:
        q, k, v, logf = (jnp.flip(t, axis=1) for t in (q, k, v, logf))
    bn, length, hh, _ = q.shape
    n_chunks = length // HGRN_CHUNK

    def chunks(t):
        return jnp.moveaxis(t.reshape(bn, n_chunks, HGRN_CHUNK, hh, t.shape[-1]), 1, 0)

    tri = jnp.tril(jnp.ones((HGRN_CHUNK, HGRN_CHUNK), dtype=bool))

    def step(state, xs):
        qc, kc, vc, fc = xs
        b = jnp.cumsum(fc, axis=1)
        b_end = b[:, -1]
        q_dec = qc * jnp.exp(b)
        att = jnp.einsum('bthk,bshk->bhts', q_dec, kc * jnp.exp(-b))
        att = jnp.where(tri, att, 0.0)
        o = jnp.einsum('bhts,bshv->bthv', att, vc) + jnp.einsum('bthk,bhkv->bthv', q_dec, state)
        k_end = kc * jnp.exp(b_end[:, None] - b)
        state = jnp.exp(b_end)[..., None] * state + jnp.einsum('bshk,bshv->bhkv', k_end, vc)
        return state, o

    s_fin, o = lax.scan(step, s0, (chunks(q), chunks(k), chunks(v), chunks(logf)))
    o = jnp.moveaxis(o, 0, 1).reshape(bn, length, hh, v.shape[-1])
    if reverse:
        o = jnp.flip(o, axis=1)
    return o, s_fin


def hgrn2_mixer(h_ctx, h_lat, w_in, lb, gnorm_w, w_out, need_ctx_out):
    dtype = h_lat.dtype
    bn = h_lat.shape[0]

    def project(h):
        z = (h @ w_in).astype(jnp.float32)
        q, i, f_fwd, f_bwd, g = jnp.split(z, HGRN_N_PROJ, axis=-1)
        heads = lambda t: t.reshape(h.shape[0], h.shape[1], HGRN_HEADS, -1)
        return heads(q), heads(i), (heads(f_fwd), heads(f_bwd)), heads(g)

    def gates(fraw, lb_d):
        lbh = lb_d.reshape(HGRN_HEADS, HGRN_HEAD_K)
        logf = jnp.logaddexp(jnp.log(lbh), jnp.log1p(-lbh) + jax.nn.log_sigmoid(fraw))
        return logf, (1.0 - lbh) * jax.nn.sigmoid(-fraw)

    def readout(o, g):
        o = o * lax.rsqrt(jnp.mean(o * o, axis=-1, keepdims=True) + NORM_EPS) * gnorm_w.astype(jnp.float32)
        o = o * jax.nn.sigmoid(g)
        return o.reshape(o.shape[0], o.shape[1], D_MODEL).astype(dtype) @ w_out

    q_c, v_c, f_c, g_c = project(h_ctx)
    q_l, v_l, f_l, g_l = project(h_lat)
    zeros = jnp.zeros((bn, HGRN_HEADS, HGRN_HEAD_K, HGRN_HEAD_V), jnp.float32)
    o_ctx = None
    o_lat = None
    for d in range(N_DIR):
        rev = d == 1
        logf_c, k_c = gates(f_c[d], lb[d])
        logf_l, k_l = gates(f_l[d], lb[d])
        oc, s_ctx = _hgrn2_scan(q_c, k_c, v_c, logf_c, zeros, rev)
        ol, _ = _hgrn2_scan(q_l, k_l, v_l, logf_l, s_ctx, rev)
        o_lat = ol if o_lat is None else o_lat + ol
        if need_ctx_out:
            o_ctx = oc if o_ctx is None else o_ctx + oc
    out_ctx = readout(o_ctx, g_c) if need_ctx_out else None
    return out_ctx, readout(o_lat, g_l)


def _fwd_setup_inputs(seed: int = 0) -> dict:
    key = jax.random.key(seed)
    ks = jax.random.split(key, 28)
    f32 = jnp.float32
    D = D_MODEL

    def nrm(k, shape, s):
        return s * jax.random.normal(k, shape, f32)

    s5_shape = (N_A_LAYERS, N_DIR, S5_GROUPS, S5_STATE)
    lam_im0 = jnp.pi * jnp.arange(S5_STATE, dtype=f32)
    return {
        'x': nrm(ks[0], (BATCH, SEQ, D), 1.0),
        'c': nrm(ks[1], (BATCH, D), 1.0),
        'ctx': nrm(ks[2], (BATCH, CTX_LEN, D), 1.0),
        'c_ctx': nrm(ks[3], (D,), 1.0),
        'w_mod': nrm(ks[4], (DEPTH, D, N_MOD * D), 0.02),
        'b_mod': nrm(ks[5], (DEPTH, N_MOD * D), 0.02),
        'norm1_w': 1.0 + nrm(ks[6], (DEPTH, D), 0.01),
        'norm2_w': 1.0 + nrm(ks[7], (DEPTH, D), 0.01),
        'final_norm_w': 1.0 + nrm(ks[8], (D,), 0.01),
        's5_w_in': nrm(ks[9], (N_A_LAYERS, D, D), D ** -0.5),
        's5_lam_re': -0.5 + nrm(ks[10], s5_shape, 0.01),
        's5_lam_im': lam_im0 + nrm(ks[11], s5_shape, 0.01),
        's5_log_step': jax.random.uniform(ks[12], (N_A_LAYERS, N_DIR, S5_GROUPS), f32, minval=math.log(S5_DT_MIN), maxval=math.log(S5_DT_MAX)),
        's5_b_re': nrm(ks[13], s5_shape + (S5_GROUP,), (0.5 / S5_GROUP) ** 0.5),
        's5_b_im': nrm(ks[14], s5_shape + (S5_GROUP,), (0.5 / S5_GROUP) ** 0.5),
        's5_c_re': nrm(ks[15], (N_A_LAYERS, N_DIR, S5_GROUPS, S5_GROUP, S5_STATE), (1.0 / S5_STATE) ** 0.5),
        's5_c_im': nrm(ks[16], (N_A_LAYERS, N_DIR, S5_GROUPS, S5_GROUP, S5_STATE), (1.0 / S5_STATE) ** 0.5),
        's5_d': nrm(ks[17], (N_A_LAYERS, D), 1.0),
        's5_w_glu': nrm(ks[18], (N_A_LAYERS, D, D), D ** -0.5),
        's5_w_out': nrm(ks[19], (N_A_LAYERS, D, D), D ** -0.5),
        'hg_w_in': nrm(ks[20], (N_B_LAYERS, D, HGRN_N_PROJ * D), D ** -0.5),
        'hg_lower_bounds': nrm(ks[21], (N_DIR, DEPTH, D), 0.1),
        'hg_gnorm_w': 1.0 + nrm(ks[22], (N_B_LAYERS, HGRN_HEAD_V), 0.01),
        'hg_w_out': nrm(ks[23], (N_B_LAYERS, D, D), D ** -0.5),
        'ffn_w_up': nrm(ks[24], (DEPTH, D, 2 * D_FF), D ** -0.5),
        'ffn_conv_w': nrm(ks[25], (DEPTH, CONV_WIDTH, 2 * D_FF), 0.5),
        'ffn_conv_b': nrm(ks[26], (DEPTH, 2 * D_FF), 0.02),
        'ffn_w_down': nrm(ks[27], (DEPTH, D_FF, D), D_FF ** -0.5),
    }


def _fwd_reference(x, c, ctx, c_ctx, w_mod, b_mod, norm1_w, norm2_w, final_norm_w,
              s5_w_in, s5_lam_re, s5_lam_im, s5_log_step, s5_b_re, s5_b_im, s5_c_re, s5_c_im, s5_d, s5_w_glu, s5_w_out,
              hg_w_in, hg_lower_bounds, hg_gnorm_w, hg_w_out,
              ffn_w_up, ffn_conv_w, ffn_conv_b, ffn_w_down):
    rows = x.shape[1] // GRID_W
    lb_all = jax.nn.softmax(hg_lower_bounds.astype(jnp.float32), axis=1)
    lb_all = jnp.cumsum(lb_all, axis=1) - lb_all[:, :1]
    sc_lat = jax.nn.silu(c)
    sc_ctx = jax.nn.silu(c_ctx)[None, :]
    for layer in range(DEPTH):
        last = layer == DEPTH - 1
        m_lat = jnp.split(sc_lat @ w_mod[layer] + b_mod[layer], N_MOD, axis=-1)
        m_ctx = jnp.split(sc_ctx @ w_mod[layer] + b_mod[layer], N_MOD, axis=-1)
        h_lat = modulate(rms_norm(x, norm1_w[layer]), m_lat[0], m_lat[1])
        h_ctx = modulate(rms_norm(ctx, norm1_w[layer]), m_ctx[0], m_ctx[1])
        j = layer // N_MIXERS
        if layer % N_MIXERS == 0:
            y_ctx, y_lat = s5_mixer(h_ctx, h_lat, s5_w_in[j], s5_lam_re[j], s5_lam_im[j], s5_log_step[j],
                                    s5_b_re[j], s5_b_im[j], s5_c_re[j], s5_c_im[j], s5_d[j],
                                    s5_w_glu[j], s5_w_out[j], not last)
        else:
            y_ctx, y_lat = hgrn2_mixer(h_ctx, h_lat, hg_w_in[j], lb_all[:, layer], hg_gnorm_w[j],
                                       hg_w_out[j], not last)
        x = x + m_lat[2][:, None, :] * y_lat
        f_lat = conv_ffn(modulate(rms_norm(x, norm2_w[layer]), m_lat[3], m_lat[4]),
                         ffn_w_up[layer], ffn_conv_w[layer], ffn_conv_b[layer], ffn_w_down[layer], rows)
        x = x + m_lat[5][:, None, :] * f_lat
        if not last:
            ctx = ctx + m_ctx[2][:, None, :] * y_ctx
            f_ctx = conv_ffn(modulate(rms_norm(ctx, norm2_w[layer]), m_ctx[3], m_ctx[4]),
                             ffn_w_up[layer], ffn_conv_w[layer], ffn_conv_b[layer], ffn_w_down[layer], None)
            ctx = ctx + m_ctx[5][:, None, :] * f_ctx
    return rms_norm(x, final_norm_w)


import jax as _jax
import jax.numpy as _jnp

TWIN_FORMAT = 'train_step'
FWD_PARAMS = ['x', 'c', 'ctx', 'c_ctx', 'w_mod', 'b_mod', 'norm1_w', 'norm2_w', 'final_norm_w', 's5_w_in', 's5_lam_re', 's5_lam_im', 's5_log_step', 's5_b_re', 's5_b_im', 's5_c_re', 's5_c_im', 's5_d', 's5_w_glu', 's5_w_out', 'hg_w_in', 'hg_lower_bounds', 'hg_gnorm_w', 'hg_w_out', 'ffn_w_up', 'ffn_conv_w', 'ffn_conv_b', 'ffn_w_down']
TWIN_WEIGHTS = ['c_ctx', 'w_mod', 'b_mod', 'norm1_w', 'norm2_w', 'final_norm_w', 's5_w_in', 's5_lam_re', 's5_lam_im', 's5_log_step', 's5_b_re', 's5_b_im', 's5_c_re', 's5_c_im', 's5_d', 's5_w_glu', 's5_w_out', 'hg_w_in', 'hg_lower_bounds', 'hg_gnorm_w', 'hg_w_out', 'ffn_w_up', 'ffn_conv_w', 'ffn_conv_b', 'ffn_w_down']
TWIN_DIFF_INPUT = 'x'
TWIN_INPUTS = ['x', 'c', 'ctx', 'c_ctx', 'w_mod', 'b_mod', 'norm1_w', 'norm2_w', 'final_norm_w', 's5_w_in', 's5_lam_re', 's5_lam_im', 's5_log_step', 's5_b_re', 's5_b_im', 's5_c_re', 's5_c_im', 's5_d', 's5_w_glu', 's5_w_out', 'hg_w_in', 'hg_lower_bounds', 'hg_gnorm_w', 'hg_w_out', 'ffn_w_up', 'ffn_conv_w', 'ffn_conv_b', 'ffn_w_down', 'loss_target', 'm_c_ctx', 'm_w_mod', 'm_b_mod', 'm_norm1_w', 'm_norm2_w', 'm_final_norm_w', 'm_s5_w_in', 'm_s5_lam_re', 'm_s5_lam_im', 'm_s5_log_step', 'm_s5_b_re', 'm_s5_b_im', 'm_s5_c_re', 'm_s5_c_im', 'm_s5_d', 'm_s5_w_glu', 'm_s5_w_out', 'm_hg_w_in', 'm_hg_lower_bounds', 'm_hg_gnorm_w', 'm_hg_w_out', 'm_ffn_w_up', 'm_ffn_conv_w', 'm_ffn_conv_b', 'm_ffn_w_down', 'v_c_ctx', 'v_w_mod', 'v_b_mod', 'v_norm1_w', 'v_norm2_w', 'v_final_norm_w', 'v_s5_w_in', 'v_s5_lam_re', 'v_s5_lam_im', 'v_s5_log_step', 'v_s5_b_re', 'v_s5_b_im', 'v_s5_c_re', 'v_s5_c_im', 'v_s5_d', 'v_s5_w_glu', 'v_s5_w_out', 'v_hg_w_in', 'v_hg_lower_bounds', 'v_hg_gnorm_w', 'v_hg_w_out', 'v_ffn_w_up', 'v_ffn_conv_w', 'v_ffn_conv_b', 'v_ffn_w_down']
TWIN_OUTPUTS = ['loss', 'grad_x', 'grad_c_ctx', 'grad_w_mod', 'grad_b_mod', 'grad_norm1_w', 'grad_norm2_w', 'grad_final_norm_w', 'grad_s5_w_in', 'grad_s5_lam_re', 'grad_s5_lam_im', 'grad_s5_log_step', 'grad_s5_b_re', 'grad_s5_b_im', 'grad_s5_c_re', 'grad_s5_c_im', 'grad_s5_d', 'grad_s5_w_glu', 'grad_s5_w_out', 'grad_hg_w_in', 'grad_hg_lower_bounds', 'grad_hg_gnorm_w', 'grad_hg_w_out', 'grad_ffn_w_up', 'grad_ffn_conv_w', 'grad_ffn_conv_b', 'grad_ffn_w_down', 'delta_c_ctx', 'delta_w_mod', 'delta_b_mod', 'delta_norm1_w', 'delta_norm2_w', 'delta_final_norm_w', 'delta_s5_w_in', 'delta_s5_lam_re', 'delta_s5_lam_im', 'delta_s5_log_step', 'delta_s5_b_re', 'delta_s5_b_im', 'delta_s5_c_re', 'delta_s5_c_im', 'delta_s5_d', 'delta_s5_w_glu', 'delta_s5_w_out', 'delta_hg_w_in', 'delta_hg_lower_bounds', 'delta_hg_gnorm_w', 'delta_hg_w_out', 'delta_ffn_w_up', 'delta_ffn_conv_w', 'delta_ffn_conv_b', 'delta_ffn_w_down', 'new_m_c_ctx', 'new_m_w_mod', 'new_m_b_mod', 'new_m_norm1_w', 'new_m_norm2_w', 'new_m_final_norm_w', 'new_m_s5_w_in', 'new_m_s5_lam_re', 'new_m_s5_lam_im', 'new_m_s5_log_step', 'new_m_s5_b_re', 'new_m_s5_b_im', 'new_m_s5_c_re', 'new_m_s5_c_im', 'new_m_s5_d', 'new_m_s5_w_glu', 'new_m_s5_w_out', 'new_m_hg_w_in', 'new_m_hg_lower_bounds', 'new_m_hg_gnorm_w', 'new_m_hg_w_out', 'new_m_ffn_w_up', 'new_m_ffn_conv_w', 'new_m_ffn_conv_b', 'new_m_ffn_w_down', 'new_v_c_ctx', 'new_v_w_mod', 'new_v_b_mod', 'new_v_norm1_w', 'new_v_norm2_w', 'new_v_final_norm_w', 'new_v_s5_w_in', 'new_v_s5_lam_re', 'new_v_s5_lam_im', 'new_v_s5_log_step', 'new_v_s5_b_re', 'new_v_s5_b_im', 'new_v_s5_c_re', 'new_v_s5_c_im', 'new_v_s5_d', 'new_v_s5_w_glu', 'new_v_s5_w_out', 'new_v_hg_w_in', 'new_v_hg_lower_bounds', 'new_v_hg_gnorm_w', 'new_v_hg_w_out', 'new_v_ffn_w_up', 'new_v_ffn_conv_w', 'new_v_ffn_conv_b', 'new_v_ffn_w_down']
TWIN_LEAF_KINDS = {'loss': 'loss', 'grad_x': 'grad_x', 'grad_c_ctx': 'grad_w', 'grad_w_mod': 'grad_w', 'grad_b_mod': 'grad_w', 'grad_norm1_w': 'grad_w', 'grad_norm2_w': 'grad_w', 'grad_final_norm_w': 'grad_w', 'grad_s5_w_in': 'grad_w', 'grad_s5_lam_re': 'grad_w', 'grad_s5_lam_im': 'grad_w', 'grad_s5_log_step': 'grad_w', 'grad_s5_b_re': 'grad_w', 'grad_s5_b_im': 'grad_w', 'grad_s5_c_re': 'grad_w', 'grad_s5_c_im': 'grad_w', 'grad_s5_d': 'grad_w', 'grad_s5_w_glu': 'grad_w', 'grad_s5_w_out': 'grad_w', 'grad_hg_w_in': 'grad_w', 'grad_hg_lower_bounds': 'grad_w', 'grad_hg_gnorm_w': 'grad_w', 'grad_hg_w_out': 'grad_w', 'grad_ffn_w_up': 'grad_w', 'grad_ffn_conv_w': 'grad_w', 'grad_ffn_conv_b': 'grad_w', 'grad_ffn_w_down': 'grad_w', 'delta_c_ctx': 'delta_w', 'delta_w_mod': 'delta_w', 'delta_b_mod': 'delta_w', 'delta_norm1_w': 'delta_w', 'delta_norm2_w': 'delta_w', 'delta_final_norm_w': 'delta_w', 'delta_s5_w_in': 'delta_w', 'delta_s5_lam_re': 'delta_w', 'delta_s5_lam_im': 'delta_w', 'delta_s5_log_step': 'delta_w', 'delta_s5_b_re': 'delta_w', 'delta_s5_b_im': 'delta_w', 'delta_s5_c_re': 'delta_w', 'delta_s5_c_im': 'delta_w', 'delta_s5_d': 'delta_w', 'delta_s5_w_glu': 'delta_w', 'delta_s5_w_out': 'delta_w', 'delta_hg_w_in': 'delta_w', 'delta_hg_lower_bounds': 'delta_w', 'delta_hg_gnorm_w': 'delta_w', 'delta_hg_w_out': 'delta_w', 'delta_ffn_w_up': 'delta_w', 'delta_ffn_conv_w': 'delta_w', 'delta_ffn_conv_b': 'delta_w', 'delta_ffn_w_down': 'delta_w', 'new_m_c_ctx': 'new_m', 'new_m_w_mod': 'new_m', 'new_m_b_mod': 'new_m', 'new_m_norm1_w': 'new_m', 'new_m_norm2_w': 'new_m', 'new_m_final_norm_w': 'new_m', 'new_m_s5_w_in': 'new_m', 'new_m_s5_lam_re': 'new_m', 'new_m_s5_lam_im': 'new_m', 'new_m_s5_log_step': 'new_m', 'new_m_s5_b_re': 'new_m', 'new_m_s5_b_im': 'new_m', 'new_m_s5_c_re': 'new_m', 'new_m_s5_c_im': 'new_m', 'new_m_s5_d': 'new_m', 'new_m_s5_w_glu': 'new_m', 'new_m_s5_w_out': 'new_m', 'new_m_hg_w_in': 'new_m', 'new_m_hg_lower_bounds': 'new_m', 'new_m_hg_gnorm_w': 'new_m', 'new_m_hg_w_out': 'new_m', 'new_m_ffn_w_up': 'new_m', 'new_m_ffn_conv_w': 'new_m', 'new_m_ffn_conv_b': 'new_m', 'new_m_ffn_w_down': 'new_m', 'new_v_c_ctx': 'new_v', 'new_v_w_mod': 'new_v', 'new_v_b_mod': 'new_v', 'new_v_norm1_w': 'new_v', 'new_v_norm2_w': 'new_v', 'new_v_final_norm_w': 'new_v', 'new_v_s5_w_in': 'new_v', 'new_v_s5_lam_re': 'new_v', 'new_v_s5_lam_im': 'new_v', 'new_v_s5_log_step': 'new_v', 'new_v_s5_b_re': 'new_v', 'new_v_s5_b_im': 'new_v', 'new_v_s5_c_re': 'new_v', 'new_v_s5_c_im': 'new_v', 'new_v_s5_d': 'new_v', 'new_v_s5_w_glu': 'new_v', 'new_v_s5_w_out': 'new_v', 'new_v_hg_w_in': 'new_v', 'new_v_hg_lower_bounds': 'new_v', 'new_v_hg_gnorm_w': 'new_v', 'new_v_hg_w_out': 'new_v', 'new_v_ffn_w_up': 'new_v', 'new_v_ffn_conv_w': 'new_v', 'new_v_ffn_conv_b': 'new_v', 'new_v_ffn_w_down': 'new_v'}


def _forward(args):
    return _fwd_reference(*[args[k] for k in FWD_PARAMS])


def _output_shape():
    out = _jax.eval_shape(lambda: _forward(_fwd_setup_inputs(0)))
    return out.shape, out.dtype

N_MICROBATCH = 1
ADAM_LR = 0.001
ADAM_B1 = 0.9
ADAM_B2 = 0.999
ADAM_EPS = 1e-08
ADAM_WD = 0.01
ADAM_STEP = 10
PER_EXAMPLE_BATCH_AXIS = {'x': 0, 'c': 0, 'ctx': 0, 'loss_target': 0}
SHARED_INPUTS = []
_WEIGHT_DTYPES = {'c_ctx': _jnp.float32, 'w_mod': _jnp.float32, 'b_mod': _jnp.float32, 'norm1_w': _jnp.float32, 'norm2_w': _jnp.float32, 'final_norm_w': _jnp.float32, 's5_w_in': _jnp.float32, 's5_lam_re': _jnp.float32, 's5_lam_im': _jnp.float32, 's5_log_step': _jnp.float32, 's5_b_re': _jnp.float32, 's5_b_im': _jnp.float32, 's5_c_re': _jnp.float32, 's5_c_im': _jnp.float32, 's5_d': _jnp.float32, 's5_w_glu': _jnp.float32, 's5_w_out': _jnp.float32, 'hg_w_in': _jnp.float32, 'hg_lower_bounds': _jnp.float32, 'hg_gnorm_w': _jnp.float32, 'hg_w_out': _jnp.float32, 'ffn_w_up': _jnp.float32, 'ffn_conv_w': _jnp.float32, 'ffn_conv_b': _jnp.float32, 'ffn_w_down': _jnp.float32}
MOMENT_SCALE = {'c_ctx': 3.515733e-03, 'w_mod': 6.173879e-02, 'b_mod': 1.018899e-01, 'norm1_w': 8.197667e-02, 'norm2_w': 7.763938e-02, 'final_norm_w': 6.404253e+01, 's5_w_in': 4.770562e-02, 's5_lam_re': 4.928936e-03, 's5_lam_im': 5.442293e-03, 's5_log_step': 1.766036e+00, 's5_b_re': 2.836411e-03, 's5_b_im': 3.058190e-03, 's5_c_re': 4.097842e-03, 's5_c_im': 4.118233e-03, 's5_d': 4.965563e-02, 's5_w_glu': 1.454534e-02, 's5_w_out': 4.404927e-02, 'hg_w_in': 4.998301e-02, 'hg_lower_bounds': 2.036584e-02, 'hg_gnorm_w': 1.369444e-01, 'hg_w_out': 5.195235e-02, 'ffn_w_up': 3.560964e-02, 'ffn_conv_w': 4.132832e-02, 'ffn_conv_b': 3.561375e-02, 'ffn_w_down': 5.811871e-02}


def _to_microbatches(a, axis):
    t = _jnp.moveaxis(a, axis, 0)
    t = t.reshape((N_MICROBATCH, t.shape[0] // N_MICROBATCH) + t.shape[1:])
    return _jnp.moveaxis(t, 1, axis + 1)


def setup_inputs(seed: int = 0) -> dict:
    inp = _fwd_setup_inputs(seed)
    key = _jax.random.fold_in(_jax.random.key(seed), 7919)
    shape, _ = _output_shape()
    out = dict(inp)
    out["loss_target"] = _jax.random.normal(_jax.random.fold_in(key, 0), shape, _jnp.float32)
    for i, name in enumerate(TWIN_WEIGHTS):
        w = inp[name].astype(_jnp.float32)
        if MOMENT_SCALE is None:
            s = _jnp.sqrt(_jnp.mean(_jnp.square(w)) + 1e-30)
        else:
            s = MOMENT_SCALE[name]
        km, kv = _jax.random.split(_jax.random.fold_in(key, i + 1))
        out[name] = w
        out["m_" + name] = s * _jax.random.normal(km, w.shape, _jnp.float32)
        out["v_" + name] = (s * s) * _jax.random.uniform(kv, w.shape, _jnp.float32, 0.5, 1.5)
    if N_MICROBATCH > 1:
        for name, axis in PER_EXAMPLE_BATCH_AXIS.items():
            out[name] = _to_microbatches(out[name], axis)
    return {'x': out['x'], 'c': out['c'], 'ctx': out['ctx'], 'c_ctx': out['c_ctx'], 'w_mod': out['w_mod'], 'b_mod': out['b_mod'], 'norm1_w': out['norm1_w'], 'norm2_w': out['norm2_w'], 'final_norm_w': out['final_norm_w'], 's5_w_in': out['s5_w_in'], 's5_lam_re': out['s5_lam_re'], 's5_lam_im': out['s5_lam_im'], 's5_log_step': out['s5_log_step'], 's5_b_re': out['s5_b_re'], 's5_b_im': out['s5_b_im'], 's5_c_re': out['s5_c_re'], 's5_c_im': out['s5_c_im'], 's5_d': out['s5_d'], 's5_w_glu': out['s5_w_glu'], 's5_w_out': out['s5_w_out'], 'hg_w_in': out['hg_w_in'], 'hg_lower_bounds': out['hg_lower_bounds'], 'hg_gnorm_w': out['hg_gnorm_w'], 'hg_w_out': out['hg_w_out'], 'ffn_w_up': out['ffn_w_up'], 'ffn_conv_w': out['ffn_conv_w'], 'ffn_conv_b': out['ffn_conv_b'], 'ffn_w_down': out['ffn_w_down'], 'loss_target': out['loss_target'], 'm_c_ctx': out['m_c_ctx'], 'm_w_mod': out['m_w_mod'], 'm_b_mod': out['m_b_mod'], 'm_norm1_w': out['m_norm1_w'], 'm_norm2_w': out['m_norm2_w'], 'm_final_norm_w': out['m_final_norm_w'], 'm_s5_w_in': out['m_s5_w_in'], 'm_s5_lam_re': out['m_s5_lam_re'], 'm_s5_lam_im': out['m_s5_lam_im'], 'm_s5_log_step': out['m_s5_log_step'], 'm_s5_b_re': out['m_s5_b_re'], 'm_s5_b_im': out['m_s5_b_im'], 'm_s5_c_re': out['m_s5_c_re'], 'm_s5_c_im': out['m_s5_c_im'], 'm_s5_d': out['m_s5_d'], 'm_s5_w_glu': out['m_s5_w_glu'], 'm_s5_w_out': out['m_s5_w_out'], 'm_hg_w_in': out['m_hg_w_in'], 'm_hg_lower_bounds': out['m_hg_lower_bounds'], 'm_hg_gnorm_w': out['m_hg_gnorm_w'], 'm_hg_w_out': out['m_hg_w_out'], 'm_ffn_w_up': out['m_ffn_w_up'], 'm_ffn_conv_w': out['m_ffn_conv_w'], 'm_ffn_conv_b': out['m_ffn_conv_b'], 'm_ffn_w_down': out['m_ffn_w_down'], 'v_c_ctx': out['v_c_ctx'], 'v_w_mod': out['v_w_mod'], 'v_b_mod': out['v_b_mod'], 'v_norm1_w': out['v_norm1_w'], 'v_norm2_w': out['v_norm2_w'], 'v_final_norm_w': out['v_final_norm_w'], 'v_s5_w_in': out['v_s5_w_in'], 'v_s5_lam_re': out['v_s5_lam_re'], 'v_s5_lam_im': out['v_s5_lam_im'], 'v_s5_log_step': out['v_s5_log_step'], 'v_s5_b_re': out['v_s5_b_re'], 'v_s5_b_im': out['v_s5_b_im'], 'v_s5_c_re': out['v_s5_c_re'], 'v_s5_c_im': out['v_s5_c_im'], 'v_s5_d': out['v_s5_d'], 'v_s5_w_glu': out['v_s5_w_glu'], 'v_s5_w_out': out['v_s5_w_out'], 'v_hg_w_in': out['v_hg_w_in'], 'v_hg_lower_bounds': out['v_hg_lower_bounds'], 'v_hg_gnorm_w': out['v_hg_gnorm_w'], 'v_hg_w_out': out['v_hg_w_out'], 'v_ffn_w_up': out['v_ffn_w_up'], 'v_ffn_conv_w': out['v_ffn_conv_w'], 'v_ffn_conv_b': out['v_ffn_conv_b'], 'v_ffn_w_down': out['v_ffn_w_down']}


def _loss(weights, diff, rest, loss_target):
    with _jax.named_scope("forward"):
        args = {**rest, TWIN_DIFF_INPUT: diff, **{k: w.astype(_WEIGHT_DTYPES[k]) for k, w in weights.items()}}
        y = _forward(args)
    with _jax.named_scope("loss_head"):
        err = _jnp.square(y.astype(_jnp.float32) - loss_target)
        return 0.5 * _jnp.sum(_jnp.mean(err, axis=-1)) if err.ndim else 0.5 * err


def _adamw(w, g, m, v):
    m = ADAM_B1 * m + (1.0 - ADAM_B1) * g
    v = ADAM_B2 * v + (1.0 - ADAM_B2) * _jnp.square(g)
    m_hat = m / (1.0 - ADAM_B1 ** ADAM_STEP)
    v_hat = v / (1.0 - ADAM_B2 ** ADAM_STEP)
    delta = -ADAM_LR * (m_hat / (_jnp.sqrt(v_hat) + ADAM_EPS) + ADAM_WD * w)
    return delta, m, v


def reference(x, c, ctx, c_ctx, w_mod, b_mod, norm1_w, norm2_w, final_norm_w, s5_w_in, s5_lam_re, s5_lam_im, s5_log_step, s5_b_re, s5_b_im, s5_c_re, s5_c_im, s5_d, s5_w_glu, s5_w_out, hg_w_in, hg_lower_bounds, hg_gnorm_w, hg_w_out, ffn_w_up, ffn_conv_w, ffn_conv_b, ffn_w_down, loss_target, m_c_ctx, m_w_mod, m_b_mod, m_norm1_w, m_norm2_w, m_final_norm_w, m_s5_w_in, m_s5_lam_re, m_s5_lam_im, m_s5_log_step, m_s5_b_re, m_s5_b_im, m_s5_c_re, m_s5_c_im, m_s5_d, m_s5_w_glu, m_s5_w_out, m_hg_w_in, m_hg_lower_bounds, m_hg_gnorm_w, m_hg_w_out, m_ffn_w_up, m_ffn_conv_w, m_ffn_conv_b, m_ffn_w_down, v_c_ctx, v_w_mod, v_b_mod, v_norm1_w, v_norm2_w, v_final_norm_w, v_s5_w_in, v_s5_lam_re, v_s5_lam_im, v_s5_log_step, v_s5_b_re, v_s5_b_im, v_s5_c_re, v_s5_c_im, v_s5_d, v_s5_w_glu, v_s5_w_out, v_hg_w_in, v_hg_lower_bounds, v_hg_gnorm_w, v_hg_w_out, v_ffn_w_up, v_ffn_conv_w, v_ffn_conv_b, v_ffn_w_down):
    given = dict(x=x, c=c, ctx=ctx, c_ctx=c_ctx, w_mod=w_mod, b_mod=b_mod, norm1_w=norm1_w, norm2_w=norm2_w, final_norm_w=final_norm_w, s5_w_in=s5_w_in, s5_lam_re=s5_lam_re, s5_lam_im=s5_lam_im, s5_log_step=s5_log_step, s5_b_re=s5_b_re, s5_b_im=s5_b_im, s5_c_re=s5_c_re, s5_c_im=s5_c_im, s5_d=s5_d, s5_w_glu=s5_w_glu, s5_w_out=s5_w_out, hg_w_in=hg_w_in, hg_lower_bounds=hg_lower_bounds, hg_gnorm_w=hg_gnorm_w, hg_w_out=hg_w_out, ffn_w_up=ffn_w_up, ffn_conv_w=ffn_conv_w, ffn_conv_b=ffn_conv_b, ffn_w_down=ffn_w_down, loss_target=loss_target, m_c_ctx=m_c_ctx, m_w_mod=m_w_mod, m_b_mod=m_b_mod, m_norm1_w=m_norm1_w, m_norm2_w=m_norm2_w, m_final_norm_w=m_final_norm_w, m_s5_w_in=m_s5_w_in, m_s5_lam_re=m_s5_lam_re, m_s5_lam_im=m_s5_lam_im, m_s5_log_step=m_s5_log_step, m_s5_b_re=m_s5_b_re, m_s5_b_im=m_s5_b_im, m_s5_c_re=m_s5_c_re, m_s5_c_im=m_s5_c_im, m_s5_d=m_s5_d, m_s5_w_glu=m_s5_w_glu, m_s5_w_out=m_s5_w_out, m_hg_w_in=m_hg_w_in, m_hg_lower_bounds=m_hg_lower_bounds, m_hg_gnorm_w=m_hg_gnorm_w, m_hg_w_out=m_hg_w_out, m_ffn_w_up=m_ffn_w_up, m_ffn_conv_w=m_ffn_conv_w, m_ffn_conv_b=m_ffn_conv_b, m_ffn_w_down=m_ffn_w_down, v_c_ctx=v_c_ctx, v_w_mod=v_w_mod, v_b_mod=v_b_mod, v_norm1_w=v_norm1_w, v_norm2_w=v_norm2_w, v_final_norm_w=v_final_norm_w, v_s5_w_in=v_s5_w_in, v_s5_lam_re=v_s5_lam_re, v_s5_lam_im=v_s5_lam_im, v_s5_log_step=v_s5_log_step, v_s5_b_re=v_s5_b_re, v_s5_b_im=v_s5_b_im, v_s5_c_re=v_s5_c_re, v_s5_c_im=v_s5_c_im, v_s5_d=v_s5_d, v_s5_w_glu=v_s5_w_glu, v_s5_w_out=v_s5_w_out, v_hg_w_in=v_hg_w_in, v_hg_lower_bounds=v_hg_lower_bounds, v_hg_gnorm_w=v_hg_gnorm_w, v_hg_w_out=v_hg_w_out, v_ffn_w_up=v_ffn_w_up, v_ffn_conv_w=v_ffn_conv_w, v_ffn_conv_b=v_ffn_conv_b, v_ffn_w_down=v_ffn_w_down)
    weights = {n: given[n] for n in TWIN_WEIGHTS}
    shared = {n: given[n] for n in SHARED_INPUTS}
    per_example = {n: given[n] for n in ['x', 'c', 'ctx']}
    grad_fn = _jax.value_and_grad(_loss, argnums=(0, 1))

    def one_microbatch(ex, loss_target):
        ex = dict(ex)
        diff = ex.pop(TWIN_DIFF_INPUT)
        return grad_fn(weights, diff, {**shared, **ex}, loss_target)

    if N_MICROBATCH == 1:
        loss, (grad_w, grad_x) = one_microbatch(per_example, given["loss_target"])
    else:
        def body(carry, xs):
            loss_sum, grad_sum = carry
            l_k, (gw_k, gx_k) = one_microbatch(xs[0], xs[1])
            with _jax.named_scope("update"):
                return (loss_sum + l_k, _jax.tree.map(_jnp.add, grad_sum, gw_k)), gx_k

        init = (_jnp.zeros((), _jnp.float32), _jax.tree.map(_jnp.zeros_like, weights))
        (loss, grad_w), grad_x = _jax.lax.scan(body, init, (per_example, given["loss_target"]))
    with _jax.named_scope("update"):
        delta_w, new_m, new_v = {}, {}, {}
        for n in TWIN_WEIGHTS:
            delta_w[n], new_m[n], new_v[n] = _adamw(weights[n], grad_w[n], given["m_" + n], given["v_" + n])
    return (loss, grad_x, *[grad_w[n] for n in TWIN_WEIGHTS], *[delta_w[n] for n in TWIN_WEIGHTS],
            *[new_m[n] for n in TWIN_WEIGHTS], *[new_v[n] for n in TWIN_WEIGHTS])
```

```python
import functools

import jax
import jax.numpy as jnp
from jax import lax
from jax.experimental import pallas as pl
from jax.experimental.pallas import tpu as pltpu

F32 = jnp.float32
BF16 = jnp.bfloat16
NDEV = 8
LOCAL_B = 4
NORM_EPS = 1e-6
N_MOD = 6
S5_GROUP = 16
S5_STATE = 64
S5_LAM_RE_MAX = -1e-4
HG_HEAD = 128
HG_ROWS = 128
GRID_W = 64
ADAM_LR, ADAM_B1, ADAM_B2, ADAM_EPS, ADAM_WD, ADAM_STEP = 0.001, 0.9, 0.999, 1e-08, 0.01, 10
VMEM_BYTES_V7X = 64 * 1024 * 1024
LANES = 128
SUBLANES = 8

NN = (((1,), (0,)), ((), ()))
NT = (((1,), (1,)), ((), ()))
TN = (((0,), (0,)), ((), ()))
MESH = pl.DeviceIdType.MESH


def _params(sem=None, vmem=None):
    kw = {}
    if sem is not None:
        kw["dimension_semantics"] = sem
    if vmem is not None:
        kw["vmem_limit_bytes"] = int(min(vmem, VMEM_BYTES_V7X - (4 << 20)))
    return pltpu.CompilerParams(**kw)


def _nbytes(shape, dtype):
    n = 1
    for s in shape:
        n *= 1 if s is None else s
    return n * jnp.dtype(dtype).itemsize


def _dot(a, b, dims=NN, precision=None):
    return lax.dot_general(a, b, dims, preferred_element_type=F32, precision=precision)


def _sigmoid(x):
    return 1.0 / (1.0 + jnp.exp(-x))


def _exchange(arrs, *, a2a, name):
    n = len(arrs)
    out_shape = [jax.ShapeDtypeStruct(a.shape if a2a else (NDEV,) + a.shape, a.dtype) for a in arrs]

    def body(*refs):
        ins, outs = refs[:n], refs[n:2 * n]
        send_sems, recv_sems, loc_sems = refs[2 * n:]
        x, y, c = lax.axis_index("x"), lax.axis_index("y"), lax.axis_index("c")
        me = 4 * x + 2 * y + c
        local, sends, recvs = [], [], []
        for a in range(n):
            src = ins[a].at[me] if a2a else ins[a]
            lc = pltpu.make_async_copy(src, outs[a].at[me], loc_sems.at[a])
            lc.start()
            local.append(lc)
        for a in range(n):
            for k in range(1, NDEV):
                px = (1 - x) if (k >> 2) & 1 else x
                py = (1 - y) if (k >> 1) & 1 else y
                pc = (1 - c) if k & 1 else c
                p = 4 * px + 2 * py + pc
                s = a * (NDEV - 1) + k - 1
                src = ins[a].at[p] if a2a else ins[a]
                cp = pltpu.make_async_remote_copy(src_ref=src, dst_ref=outs[a].at[me], send_sem=send_sems.at[s],
                                                  recv_sem=recv_sems.at[s], device_id=(px, py, pc), device_id_type=MESH)
                cp.start()
                sends.append(cp)
                recvs.append(pltpu.make_async_remote_copy(src_ref=src, dst_ref=outs[a].at[p], send_sem=send_sems.at[s],
                                                          recv_sem=recv_sems.at[s], device_id=(px, py, pc),
                                                          device_id_type=MESH))
        for cp in sends:
            cp.wait_send()
        for cp in recvs:
            cp.wait_recv()
        for lc in local:
            lc.wait()

    res = pl.pallas_call(
        body, name=name, out_shape=out_shape,
        in_specs=[pl.BlockSpec(memory_space=pl.ANY)] * n,
        out_specs=[pl.BlockSpec(memory_space=pl.ANY)] * n,
        scratch_shapes=[pltpu.SemaphoreType.DMA((n * (NDEV - 1),)), pltpu.SemaphoreType.DMA((n * (NDEV - 1),)),
                        pltpu.SemaphoreType.DMA((n,))],
    )(*arrs)
    return list(res)


def _mm(a, b, *, name, grid, a_spec, b_spec, o_spec, o_shape, o_dtype, dims):
    nk = grid[2]
    o_block = tuple(s for s in o_spec.block_shape if s is not None)

    def body(a_ref, b_ref, o_ref, *scr):
        r = _dot(a_ref[...].astype(BF16), b_ref[...].astype(BF16), dims)
        if nk == 1:
            o_ref[...] = r.astype(o_dtype)
        else:
            acc = scr[0]
            k = pl.program_id(2)

            @pl.when(k == 0)
            def _():
                acc[...] = r

            @pl.when(k > 0)
            def _():
                acc[...] += r

            @pl.when(k == nk - 1)
            def _():
                o_ref[...] = acc[...].astype(o_dtype)

    blocks = (_nbytes(a_spec.block_shape, a.dtype) + _nbytes(b_spec.block_shape, b.dtype) + _nbytes(o_block, o_dtype))
    scratch = [pltpu.VMEM(o_block, F32)] if nk > 1 else []
    vmem = 2 * blocks + 3 * _nbytes(o_block, F32) + (8 << 20)
    return pl.pallas_call(
        body, name=name, out_shape=jax.ShapeDtypeStruct(o_shape, o_dtype), grid=grid,
        in_specs=[a_spec, b_spec], out_specs=o_spec, scratch_shapes=scratch,
        compiler_params=_params(("parallel", "parallel", "arbitrary"), vmem),
    )(a, b)


def _mm_fwd(a, w3, *, name, tm, shard_out=False, o_dtype=F32):
    n, kk = a.shape
    s, _, ns = w3.shape
    if shard_out:
        o_shape, o_spec = (s, n, ns), pl.BlockSpec((None, tm, ns), lambda i, j, k: (j, i, 0))
    else:
        o_shape, o_spec = (n, s * ns), pl.BlockSpec((tm, ns), lambda i, j, k: (i, j))
    return _mm(a, w3, name=name, grid=(n // tm, s, 1), dims=NN, o_shape=o_shape, o_dtype=o_dtype, o_spec=o_spec,
               a_spec=pl.BlockSpec((tm, kk), lambda i, j, k: (i, 0)),
               b_spec=pl.BlockSpec((None, kk, ns), lambda i, j, k: (j, 0, 0)))


def _mm_bwd_in(dy, w3, *, name, tm, shard_in=False, o_dtype=F32):
    s, kk, ns = w3.shape
    if shard_in:
        n = dy.shape[1]
        a_spec = pl.BlockSpec((None, tm, ns), lambda i, j, k: (k, i, 0))
    else:
        n = dy.shape[0]
        a_spec = pl.BlockSpec((tm, ns), lambda i, j, k: (i, k))
    return _mm(dy, w3, name=name, grid=(n // tm, 1, s), dims=NT, o_shape=(n, kk), o_dtype=o_dtype,
               o_spec=pl.BlockSpec((tm, kk), lambda i, j, k: (i, 0)), a_spec=a_spec,
               b_spec=pl.BlockSpec((None, kk, ns), lambda i, j, k: (k, 0, 0)))


def _mm_bwd_w(a, dy, *, name, tm, s, ns, shard_in=False):
    n, kk = a.shape
    if shard_in:
        b_spec = pl.BlockSpec((None, tm, ns), lambda i, j, k: (j, k, 0))
    else:
        b_spec = pl.BlockSpec((tm, ns), lambda i, j, k: (k, j))
    return _mm(a, dy, name=name, grid=(1, s, n // tm), dims=TN, o_shape=(s, kk, ns), o_dtype=F32,
               o_spec=pl.BlockSpec((None, kk, ns), lambda i, j, k: (j, 0, 0)),
               a_spec=pl.BlockSpec((tm, kk), lambda i, j, k: (k, 0)), b_spec=b_spec)


def _mm_down_fwd(hm, wd, *, name, tm):
    s, n, ks = hm.shape
    d = wd.shape[2]
    return _mm(hm, wd, name=name, grid=(n // tm, 1, s), dims=NN, o_shape=(n, d), o_dtype=F32,
               o_spec=pl.BlockSpec((tm, d), lambda i, j, k: (i, 0)),
               a_spec=pl.BlockSpec((None, tm, ks), lambda i, j, k: (k, i, 0)),
               b_spec=pl.BlockSpec((None, ks, d), lambda i, j, k: (k, 0, 0)))


def _mm_down_bwd_in(df, wd, *, name, tm):
    n, d = df.shape
    s, ks, _ = wd.shape
    return _mm(df, wd, name=name, grid=(n // tm, s, 1), dims=NT, o_shape=(s, n, ks), o_dtype=BF16,
               o_spec=pl.BlockSpec((None, tm, ks), lambda i, j, k: (j, i, 0)),
               a_spec=pl.BlockSpec((tm, d), lambda i, j, k: (i, 0)),
               b_spec=pl.BlockSpec((None, ks, d), lambda i, j, k: (j, 0, 0)))


def _mm_down_bwd_w(hm, df, *, name, tm):
    s, n, ks = hm.shape
    d = df.shape[1]
    return _mm(hm, df, name=name, grid=(1, s, n // tm), dims=TN, o_shape=(s, ks, d), o_dtype=F32,
               o_spec=pl.BlockSpec((None, ks, d), lambda i, j, k: (j, 0, 0)),
               a_spec=pl.BlockSpec((None, tm, ks), lambda i, j, k: (j, k, 0)),
               b_spec=pl.BlockSpec((tm, d), lambda i, j, k: (k, 0)))


def _rowk(fn, *, name, n, tm, nctx, rows=(), pats=(), consts=(), out_rows=(), out_seg=(), out_acc=()):
    nb, ncb = n // tm, nctx // tm
    nr, npat, ncst = len(rows), len(pats), len(consts)
    no, nseg, nacc = len(out_rows), len(out_seg), len(out_acc)
    in_specs, blocks = [], 0
    for arr, w, cb, off in rows:
        in_specs.append(pl.BlockSpec((tm, w), lambda i, cb=cb, off=off: (jnp.maximum(i - off, 0), cb)))
        blocks += _nbytes((tm, w), arr.dtype)
    for p in pats:
        in_specs.append(pl.BlockSpec((None, SUBLANES, p.shape[2]), lambda i: (jnp.where(i >= ncb, 1, 0), 0, 0)))
    for cst in consts:
        in_specs.append(pl.BlockSpec(cst.shape, lambda i: (0, 0)))
    out_shape, out_specs = [], []
    for wt, dt, w, cb in out_rows:
        out_shape.append(jax.ShapeDtypeStruct((n, wt), dt))
        out_specs.append(pl.BlockSpec((tm, w), lambda i, cb=cb: (i, cb)))
        blocks += _nbytes((tm, w), dt)
    for w in out_seg:
        out_shape.append(jax.ShapeDtypeStruct((SUBLANES, w), F32))
        out_specs.append(pl.BlockSpec((SUBLANES, w), lambda i: (0, 0)))
    for w in out_acc:
        out_shape.append(jax.ShapeDtypeStruct((1, w), F32))
        out_specs.append(pl.BlockSpec((1, w), lambda i: (0, 0)))
    scratch = [pltpu.VMEM((2, SUBLANES, w), F32) for w in out_seg] + [pltpu.VMEM((SUBLANES, w), F32) for w in out_acc]

    def body(*refs):
        r_in = refs[:nr]
        p_in = refs[nr:nr + npat]
        c_in = refs[nr + npat:nr + npat + ncst]
        base = nr + npat + ncst
        o_rows = refs[base:base + no]
        o_seg = refs[base + no:base + no + nseg]
        o_acc = refs[base + no + nseg:base + no + nseg + nacc]
        s_seg = refs[base + no + nseg + nacc:base + no + nseg + nacc + nseg]
        s_acc = refs[base + no + nseg + nacc + nseg:]
        i = pl.program_id(0)
        rv = [r[...].astype(F32).reshape(tm // SUBLANES, SUBLANES, r.shape[1]) for r in r_in]
        pv = [p[...] for p in p_in]
        cv = [c[...] for c in c_in]
        is_lat = (i >= ncb).astype(F32)
        ro, so, ao = fn(rv, pv, cv, is_lat)
        for ref, val in zip(o_rows, ro):
            ref[...] = val.reshape(tm, ref.shape[1]).astype(ref.dtype)
        if nseg or nacc:
            @pl.when(i == 0)
            def _():
                for s in list(s_seg) + list(s_acc):
                    s[...] = jnp.zeros(s.shape, F32)

            seg = jnp.where(i >= ncb, 1, 0)
            for s, val in zip(s_seg, so):
                s[seg] = s[seg] + val
            for s, val in zip(s_acc, ao):
                s[...] = s[...] + val

            @pl.when(i == nb - 1)
            def _():
                for o, s in zip(o_seg, s_seg):
                    lat, ctx = s[1], s[0]
                    row = lax.broadcasted_iota(jnp.int32, lat.shape, 0)
                    lat = lat + pltpu.roll(lat, 4, 0)
                    ctx = jnp.broadcast_to(jnp.sum(ctx, axis=0, keepdims=True), lat.shape)
                    o[...] = jnp.where(row < 4, lat, jnp.where(row == 4, ctx, 0.0))
                for o, s in zip(o_acc, s_acc):
                    o[...] = jnp.sum(s[...], axis=0, keepdims=True)

    vmem = 2 * blocks + 8 * tm * 1024 * 4 + (8 << 20)
    res = pl.pallas_call(
        body, name=name, out_shape=out_shape, grid=(nb,), in_specs=in_specs, out_specs=out_specs,
        scratch_shapes=scratch, compiler_params=_params(("arbitrary",), vmem),
    )(*[r[0] for r in rows], *pats, *consts)
    return list(res)


def _rms(z):
    return lax.rsqrt(jnp.mean(z * z, axis=-1, keepdims=True) + NORM_EPS)


def _norm_mod_fwd(z, w, sh, sc, *, name, dims, res=None):
    n, d = z.shape

    def fn(rv, pv, cv, is_lat):
        zz = rv[0]
        if res is not None:
            zz = zz + pv[2][None] * rv[1]
        h = (zz * _rms(zz) * cv[0]) * (1.0 + pv[1][None]) + pv[0][None]
        return ([zz, h] if res is not None else [h]), [], []

    rows = [(z, d, 0, 0)] + ([(res[0], d, 0, 0)] if res is not None else [])
    pats = [sh, sc] + ([res[1]] if res is not None else [])
    outs = ([(d, F32, d, 0)] if res is not None else []) + [(d, BF16, d, 0)]
    out = _rowk(fn, name=name, n=n, tm=dims["tm_row"], nctx=dims["nctx"], rows=rows, pats=pats, consts=[w],
                out_rows=outs)
    return (out[0], out[1]) if res is not None else (None, out[0])


def _norm_core_bwd(zin, dh, w, sc):
    r = _rms(zin)
    xh = zin * r
    dsh = jnp.sum(dh, axis=0)
    dsc = jnp.sum(dh * (xh * w), axis=0)
    dyv = dh * (1.0 + sc[None])
    dw = jnp.sum(dyv * xh, axis=0)
    dxh = dyv * w
    dx = r * (dxh - xh * jnp.mean(dxh * xh, axis=-1, keepdims=True))
    return dx, dsh, dsc, dw


def _norm_mod_bwd(dh, zin, dz_up, w, sc, *, name, dims, res=None):
    n, d = zin.shape

    def fn(rv, pv, cv, is_lat):
        dx, dsh, dsc, dw = _norm_core_bwd(rv[1], rv[0], cv[0], pv[0])
        dz = rv[2] + dx
        if res is None:
            return [dz], [dsh, dsc], [dw]
        return [dz, dz * pv[1][None]], [dsh, dsc, jnp.sum(dz * rv[3], axis=0)], [dw]

    rows = [(dh, d, 0, 0), (zin, d, 0, 0), (dz_up, d, 0, 0)] + ([(res[0], d, 0, 0)] if res is not None else [])
    pats = [sc] + ([res[1]] if res is not None else [])
    outs = [(d, F32, d, 0)] + ([(d, BF16, d, 0)] if res is not None else [])
    return _rowk(fn, name=name, n=n, tm=dims["tm_row"], nctx=dims["nctx"], rows=rows, pats=pats, consts=[w],
                 out_rows=outs, out_seg=[d] * (3 if res is not None else 2), out_acc=[d])


def _loss_bwd(z1, f, gate, tgt, w, *, name, dims):
    n, d = z1.shape

    def fn(rv, pv, cv, is_lat):
        z2 = rv[0] + pv[0][None] * rv[1]
        r = _rms(z2)
        xh = z2 * r
        err = (xh * cv[0] - rv[2]) * is_lat
        dout = err * (1.0 / d)
        dxh = dout * cv[0]
        dz = r * (dxh - xh * jnp.mean(dxh * xh, axis=-1, keepdims=True))
        return ([dz, dz * pv[0][None]], [jnp.sum(dz * rv[1], axis=0)],
                [jnp.sum(0.5 * err * err * (1.0 / d), axis=0), jnp.sum(dout * xh, axis=0)])

    tm = dims["tm_row"]
    rows = [(z1, d, 0, 0), (f, d, 0, 0), (tgt, d, 0, dims["nctx"] // tm)]
    return _rowk(fn, name=name, n=n, tm=tm, nctx=dims["nctx"], rows=rows, pats=[gate], consts=[w],
                 out_rows=[(d, F32, d, 0), (d, BF16, d, 0)], out_seg=[d], out_acc=[d, d])


def _gelu(y):
    return jax.nn.gelu(y, approximate=True)


def _conv_masks(tb, i):
    tok = lax.broadcasted_iota(jnp.int32, (tb, 1), 0) >> 2
    last = jnp.where(i == 0, tb // LOCAL_B - 1, GRID_W - 1)
    wpos = tok & last
    return wpos == 0, wpos == last


def _convffn_fwd(u, cw, cb, *, name, dims):
    _, sh, n, ns = u.shape
    tb = dims["nctx"]

    def body(u_ref, cw_ref, cb_ref, o_ref):
        no_left, no_right = _conv_masks(tb, pl.program_id(1))

        def conv(s):
            uu = u_ref[s]
            ul = jnp.where(no_left, 0.0, pltpu.roll(uu, LOCAL_B, 0))
            ur = jnp.where(no_right, 0.0, pltpu.roll(uu, tb - LOCAL_B, 0))
            return (cb_ref[s] + ul * cw_ref[s, pl.ds(0, 1), :] + uu * cw_ref[s, pl.ds(1, 1), :]
                    + ur * cw_ref[s, pl.ds(2, 1), :])

        a, g = conv(0), conv(1)
        o_ref[...] = (a * _sigmoid(a) * g).astype(BF16)

    vmem = 2 * (2 * tb * ns * 4 + tb * ns * 2) + 10 * tb * ns * 4 + (8 << 20)
    return pl.pallas_call(
        body, name=name, out_shape=jax.ShapeDtypeStruct((sh, n, ns), BF16), grid=(sh, n // tb),
        in_specs=[pl.BlockSpec((2, None, tb, ns), lambda j, i: (0, j, i, 0)),
                  pl.BlockSpec((2, None, 3, ns), lambda j, i: (0, j, 0, 0)),
                  pl.BlockSpec((2, None, 1, ns), lambda j, i: (0, j, 0, 0))],
        out_specs=pl.BlockSpec((None, tb, ns), lambda j, i: (j, i, 0)),
        compiler_params=_params(("parallel", "arbitrary"), vmem),
    )(u, cw, cb)


def _convffn_bwd(u, dhm, cw, cb, *, name, dims):
    _, sh, n, ns = u.shape
    tb = dims["nctx"]

    def body(u_ref, dh_ref, cw_ref, cb_ref, du_ref, dcw_ref, dcb_ref):
        i = pl.program_id(1)
        no_left, no_right = _conv_masks(tb, i)

        @pl.when(i == 0)
        def _():
            dcw_ref[...] = jnp.zeros(dcw_ref.shape, F32)
            dcb_ref[...] = jnp.zeros(dcb_ref.shape, F32)

        def taps(s):
            uu = u_ref[s]
            ul = jnp.where(no_left, 0.0, pltpu.roll(uu, LOCAL_B, 0))
            ur = jnp.where(no_right, 0.0, pltpu.roll(uu, tb - LOCAL_B, 0))
            val = (cb_ref[s] + ul * cw_ref[s, pl.ds(0, 1), :] + uu * cw_ref[s, pl.ds(1, 1), :]
                   + ur * cw_ref[s, pl.ds(2, 1), :])
            return val, ul, uu, ur

        a, al, ac, ar = taps(0)
        g, gl, gc, gr = taps(1)
        dh = dh_ref[...].astype(F32)
        sa = _sigmoid(a)
        dg = dh * (a * sa)
        da = dh * g * (sa * (1.0 + a * (1.0 - sa)))
        for s, dc, (tl, tc, tr) in ((0, da, (al, ac, ar)), (1, dg, (gl, gc, gr))):
            dcb_ref[s] += jnp.sum(dc, axis=0, keepdims=True)
            dcw_ref[s, pl.ds(0, 1), :] += jnp.sum(dc * tl, axis=0, keepdims=True)
            dcw_ref[s, pl.ds(1, 1), :] += jnp.sum(dc * tc, axis=0, keepdims=True)
            dcw_ref[s, pl.ds(2, 1), :] += jnp.sum(dc * tr, axis=0, keepdims=True)
            du = (dc * cw_ref[s, pl.ds(1, 1), :]
                  + pltpu.roll(jnp.where(no_left, 0.0, dc) * cw_ref[s, pl.ds(0, 1), :], tb - LOCAL_B, 0)
                  + pltpu.roll(jnp.where(no_right, 0.0, dc) * cw_ref[s, pl.ds(2, 1), :], LOCAL_B, 0))
            du_ref[s] = du.astype(BF16)

    vmem = 2 * (2 * tb * ns * 4 + tb * ns * 2 + 2 * tb * ns * 2) + 16 * tb * ns * 4 + (8 << 20)
    return pl.pallas_call(
        body, name=name,
        out_shape=[jax.ShapeDtypeStruct((2, sh, n, ns), BF16), jax.ShapeDtypeStruct((2, sh, 3, ns), F32),
                   jax.ShapeDtypeStruct((2, sh, 1, ns), F32)],
        grid=(sh, n // tb),
        in_specs=[pl.BlockSpec((2, None, tb, ns), lambda j, i: (0, j, i, 0)),
                  pl.BlockSpec((None, tb, ns), lambda j, i: (j, i, 0)),
                  pl.BlockSpec((2, None, 3, ns), lambda j, i: (0, j, 0, 0)),
                  pl.BlockSpec((2, None, 1, ns), lambda j, i: (0, j, 0, 0))],
        out_specs=[pl.BlockSpec((2, None, tb, ns), lambda j, i: (0, j, i, 0)),
                   pl.BlockSpec((2, None, 3, ns), lambda j, i: (0, j, 0, 0)),
                   pl.BlockSpec((2, None, 1, ns), lambda j, i: (0, j, 0, 0))],
        compiler_params=_params(("parallel", "arbitrary"), vmem),
    )(u, dhm, cw, cb)


def _s5_disc(lr, li, ls, brt, bit):
    lr = jnp.minimum(lr, S5_LAM_RE_MAX)
    dt = jnp.exp(ls)
    mag = jnp.exp(lr * dt)
    ar = mag * jnp.cos(li * dt)
    ai = mag * jnp.sin(li * dt)
    den = lr * lr + li * li
    nr = ar - 1.0
    cr = (nr * lr + ai * li) / den
    ci = (ai * lr - nr * li) / den
    return ar, ai, cr * brt - ci * bit, cr * bit + ci * brt


def _s5_disc_fwd(lr, li, ls, brt, bit):
    def body(lr_ref, li_ref, ls_ref, br_ref, bi_ref, ar_ref, ai_ref, bbr_ref, bbi_ref):
        ar, ai, bbr, bbi = _s5_disc(lr_ref[...], li_ref[...], ls_ref[...], br_ref[...], bi_ref[...])
        ar_ref[...] = ar
        ai_ref[...] = ai
        bbr_ref[...] = bbr
        bbi_ref[...] = bbi

    sd = jax.ShapeDtypeStruct
    return pl.pallas_call(body, name="s5_disc_fwd",
                          out_shape=[sd(lr.shape, F32), sd(lr.shape, F32), sd(brt.shape, F32), sd(brt.shape, F32)],
                          compiler_params=_params(None, 32 << 20))(lr, li, ls, brt, bit)


def _s5_disc_bwd(lr, li, ls, brt, bit, dar, dai, dbbr, dbbi):
    def body(lr_ref, li_ref, ls_ref, br_ref, bi_ref, dar_ref, dai_ref, dbbr_ref, dbbi_ref,
             dlr_ref, dli_ref, dls_ref, dbr_ref, dbi_ref):
        _, vjp = jax.vjp(_s5_disc, lr_ref[...], li_ref[...], ls_ref[...], br_ref[...], bi_ref[...])
        dlr, dli, dls, dbr, dbi = vjp((dar_ref[...], dai_ref[...], dbbr_ref[...], dbbi_ref[...]))
        dlr_ref[...] = dlr
        dli_ref[...] = dli
        dls_ref[...] = dls
        dbr_ref[...] = dbr
        dbi_ref[...] = dbi

    sd = jax.ShapeDtypeStruct
    return pl.pallas_call(body, name="s5_disc_bwd",
                          out_shape=[sd(lr.shape, F32), sd(lr.shape, F32), sd(ls.shape, F32), sd(brt.shape, F32),
                                     sd(brt.shape, F32)],
                          compiler_params=_params(None, 48 << 20))(lr, li, ls, brt, bit, dar, dai, dbbr, dbbi)


def _cmul(ar, ai, xr, xi):
    return ar * xr - ai * xi, ar * xi + ai * xr


def _scan_consts(a_r, a_i, rev):
    row = lax.broadcasted_iota(jnp.int32, a_r.shape, 0)
    second = (row < 4) if rev else (row >= 4)
    a2r, a2i = _cmul(a_r, a_i, a_r, a_i)
    a1r, a1i = jnp.where(second, a_r, 0.0), jnp.where(second, a_i, 0.0)
    apr, api = jnp.where(second, a2r, a_r), jnp.where(second, a2i, a_i)
    return second, a1r, a1i, apr, api


def _scan_tile(xr, xi, pr, pi, consts):
    second, a1r, a1i, apr, api = consts
    sr, si = pltpu.roll(xr, 4, 0), pltpu.roll(xi, 4, 0)
    t1r, t1i = _cmul(a1r, a1i, sr, si)
    t2r, t2i = _cmul(apr, api, pr, pi)
    yr, yi = xr + t1r + t2r, xi + t1i + t2i
    npr = jnp.where(second, yr, pltpu.roll(yr, 4, 0))
    npi = jnp.where(second, yi, pltpu.roll(yi, 4, 0))
    return yr, yi, npr, npi


def _s5_scan(xr_ref, xi_ref, row0, nrows, a_r_ref, a_i_ref, cr_ref, ci_ref, *, rev, conj, lane_block, extra=None):
    width = xr_ref.shape[1]
    nt = nrows // SUBLANES
    for lb in range(width // lane_block):
        lanes = pl.ds(lb * lane_block, lane_block)
        a_r = a_r_ref[:, lanes]
        a_i = a_i_ref[:, lanes]
        if conj:
            a_i = -a_i
        consts = _scan_consts(a_r, a_i, rev)

        def step(t, carry):
            pr, pi = carry[0], carry[1]
            j = (nt - 1 - t) if rev else t
            rows = pl.ds(pl.multiple_of(row0 + j * SUBLANES, SUBLANES), SUBLANES)
            yr, yi, pr, pi = _scan_tile(xr_ref[rows, lanes], xi_ref[rows, lanes], pr, pi, consts)
            xr_ref[rows, lanes] = yr
            xi_ref[rows, lanes] = yi
            if extra is None:
                return pr, pi
            return (pr, pi) + tuple(extra(j, lanes, yr, yi, carry[2:]))

        init = (cr_ref[:, lanes], ci_ref[:, lanes])
        if extra is not None:
            init = init + tuple(extra.init(lanes))
        out = lax.fori_loop(0, nt, step, init)
        cr_ref[:, lanes] = out[0]
        ci_ref[:, lanes] = out[1]
        if extra is not None:
            extra.done(lanes, out[2:])


def _s5_chunk_of(step, ncc, nc, rev):
    if not rev:
        return step
    return jnp.where(step < ncc, ncc - 1 - step, nc - 1 - (step - ncc))


def _s5_fwd_dir(u, base, a8r, a8i, bbr, bbi, ccr, cci, dsk, *, d, name, dims):
    n, dm = u.shape
    nk, swk = bbr.shape[0], bbr.shape[2]
    rr, sw = dims["s5_rows"], nk * swk
    nc, ncc = n // rr, dims["nctx"] // rr
    rev = d == 1
    cmap = lambda i: (_s5_chunk_of(i, ncc, nc, rev), 0)

    def body(*refs):
        if d == 0:
            u_ref, a8r_ref, a8i_ref, bbr_ref, bbi_ref, ccr_ref, cci_ref, dsk_ref = refs[:8]
            rest = refs[8:]
        else:
            u_ref, base_ref, a8r_ref, a8i_ref, bbr_ref, bbi_ref, ccr_ref, cci_ref = refs[:8]
            rest = refs[8:]
        y_ref, str_ref, sti_ref, sr, si, cr, ci = rest
        i = pl.program_id(0)

        @pl.when(i == 0)
        def _():
            cr[...] = jnp.zeros(cr.shape, F32)
            ci[...] = jnp.zeros(ci.shape, F32)

        str_ref[...] = cr[...]
        sti_ref[...] = ci[...]
        ub = u_ref[...].astype(BF16)
        for k in range(nk):
            uk = ub[:, k * LANES:(k + 1) * LANES]
            sr[:, k * swk:(k + 1) * swk] = _dot(uk, bbr_ref[k])
            si[:, k * swk:(k + 1) * swk] = _dot(uk, bbi_ref[k])
        _s5_scan(sr, si, 0, rr, a8r_ref, a8i_ref, cr, ci, rev=rev, conj=False, lane_block=dims["s5_lane_block"])
        for k in range(nk):
            cols = slice(k * LANES, (k + 1) * LANES)
            yk = (_dot(sr[:, k * swk:(k + 1) * swk].astype(BF16), ccr_ref[k])
                  - _dot(si[:, k * swk:(k + 1) * swk].astype(BF16), cci_ref[k]))
            if d == 0:
                y_ref[:, cols] = yk + dsk_ref[:, cols] * u_ref[:, cols]
            else:
                y_ref[:, cols] = yk + base_ref[:, cols]

    row_spec = pl.BlockSpec((rr, dm), cmap)
    full = lambda a: pl.BlockSpec(a.shape, lambda i: (0,) * a.ndim)
    ins = [u] + ([] if d == 0 else [base]) + [a8r, a8i, bbr, bbi, ccr, cci] + ([dsk] if d == 0 else [])
    in_specs = [row_spec] + ([] if d == 0 else [row_spec]) + [full(a) for a in (a8r, a8i, bbr, bbi, ccr, cci)]
    in_specs += [full(dsk)] if d == 0 else []
    st_spec = pl.BlockSpec((None, SUBLANES, sw), lambda i: (_s5_chunk_of(i, ncc, nc, rev), 0, 0))
    vmem = 2 * rr * sw * 4 + 6 * rr * dm * 4 + 8 * nk * LANES * swk * 2 + (12 << 20)
    return pl.pallas_call(
        body, name=name,
        out_shape=[jax.ShapeDtypeStruct((n, dm), F32), jax.ShapeDtypeStruct((nc, SUBLANES, sw), F32),
                   jax.ShapeDtypeStruct((nc, SUBLANES, sw), F32)],
        grid=(nc,), in_specs=in_specs, out_specs=[row_spec, st_spec, st_spec],
        scratch_shapes=[pltpu.VMEM((rr, sw), F32), pltpu.VMEM((rr, sw), F32), pltpu.VMEM((SUBLANES, sw), F32),
                        pltpu.VMEM((SUBLANES, sw), F32)],
        compiler_params=_params(("arbitrary",), vmem),
    )(*ins)


class _DaHook:
    def __init__(self, sr, si, accr, acci, rev_fwd):
        self.sr, self.si, self.accr, self.acci, self.rev_fwd = sr, si, accr, acci, rev_fwd

    def init(self, lanes):
        return self.accr[:, lanes], self.acci[:, lanes]

    def done(self, lanes, acc):
        self.accr[:, lanes] = acc[0]
        self.acci[:, lanes] = acc[1]

    def __call__(self, j, lanes, lr, li, acc):
        base = pl.multiple_of(SUBLANES + j * SUBLANES, SUBLANES)
        cur = pl.ds(base, SUBLANES)
        row = lax.broadcasted_iota(jnp.int32, lr.shape, 0)
        if self.rev_fwd:
            oth = pl.ds(pl.multiple_of(base + SUBLANES, SUBLANES), SUBLANES)
            spr = pltpu.roll(jnp.where(row >= 4, self.sr[cur, lanes], self.sr[oth, lanes]), 4, 0)
            spi = pltpu.roll(jnp.where(row >= 4, self.si[cur, lanes], self.si[oth, lanes]), 4, 0)
        else:
            oth = pl.ds(pl.multiple_of(base - SUBLANES, SUBLANES), SUBLANES)
            spr = pltpu.roll(jnp.where(row >= 4, self.sr[oth, lanes], self.sr[cur, lanes]), 4, 0)
            spi = pltpu.roll(jnp.where(row >= 4, self.si[oth, lanes], self.si[cur, lanes]), 4, 0)
        return acc[0] + spr * lr + spi * li, acc[1] + spr * li - spi * lr


def _s5_bwd_dir(u, dy, du_prev, a8r, a8i, bbr, bbi, ccr, cci, dsk, st_r, st_i, *, d, name, dims):
    n, dm = u.shape
    rr = dims["s5_rows"]
    nk, swk = bbr.shape[0], bbr.shape[2]
    sw = nk * swk
    nc, ncc = n // rr, dims["nctx"] // rr
    rev = d == 1
    chunk = lambda i: _s5_chunk_of(nc - 1 - i, ncc, nc, rev)

    def body(*refs):
        u_ref, dy_ref = refs[0], refs[1]
        pos = 2
        dup_ref = None
        if d == 1:
            dup_ref = refs[pos]
            pos += 1
        a8r_ref, a8i_ref, bbr_ref, bbi_ref, ccr_ref, cci_ref = refs[pos:pos + 6]
        pos += 6
        dsk_ref = None
        if d == 0:
            dsk_ref = refs[pos]
            pos += 1
        str_ref, sti_ref = refs[pos:pos + 2]
        pos += 2
        du_ref, dbbr_ref, dbbi_ref, dccr_ref, dcci_ref, dar_ref, dai_ref = refs[pos:pos + 7]
        pos += 7
        dd_ref = None
        if d == 0:
            dd_ref = refs[pos]
            pos += 1
        sr, si, lr, li, cr, ci, lcr, lci, accr, acci, dda = refs[pos:]
        i = pl.program_id(0)

        @pl.when(i == 0)
        def _():
            for ref in (lcr, lci, accr, acci, dda, dbbr_ref, dbbi_ref, dccr_ref, dcci_ref):
                ref[...] = jnp.zeros(ref.shape, F32)

        cr[...] = str_ref[...]
        ci[...] = sti_ref[...]
        spare = pl.ds(rr + SUBLANES, SUBLANES) if rev else pl.ds(0, SUBLANES)
        sr[spare, :] = str_ref[...]
        si[spare, :] = sti_ref[...]
        ub = u_ref[...].astype(BF16)
        dyb = dy_ref[...].astype(BF16)
        for k in range(nk):
            uk = ub[:, k * LANES:(k + 1) * LANES]
            sr[pl.ds(SUBLANES, rr), k * swk:(k + 1) * swk] = _dot(uk, bbr_ref[k])
            si[pl.ds(SUBLANES, rr), k * swk:(k + 1) * swk] = _dot(uk, bbi_ref[k])
        _s5_scan(sr, si, SUBLANES, rr, a8r_ref, a8i_ref, cr, ci, rev=rev, conj=False,
                 lane_block=dims["s5_lane_block"])
        for k in range(nk):
            dyk = dyb[:, k * LANES:(k + 1) * LANES]
            sl = slice(k * swk, (k + 1) * swk)
            lr[:, sl] = _dot(dyk, ccr_ref[k], NT)
            li[:, sl] = -_dot(dyk, cci_ref[k], NT)
            dccr_ref[k] += _dot(sr[pl.ds(SUBLANES, rr), sl].astype(BF16), dyk, TN)
            dcci_ref[k] -= _dot(si[pl.ds(SUBLANES, rr), sl].astype(BF16), dyk, TN)
        _s5_scan(lr, li, 0, rr, a8r_ref, a8i_ref, lcr, lci, rev=not rev, conj=True,
                 lane_block=dims["s5_lane_block"], extra=_DaHook(sr, si, accr, acci, rev))
        for k in range(nk):
            cols = slice(k * LANES, (k + 1) * LANES)
            sl = slice(k * swk, (k + 1) * swk)
            uk = ub[:, cols]
            lrk, lik = lr[:, sl].astype(BF16), li[:, sl].astype(BF16)
            dbbr_ref[k] += _dot(uk, lrk, TN)
            dbbi_ref[k] += _dot(uk, lik, TN)
            duk = _dot(lrk, bbr_ref[k], NT) + _dot(lik, bbi_ref[k], NT)
            if d == 0:
                du_ref[:, cols] = duk + dsk_ref[:, cols] * dy_ref[:, cols]
            else:
                du_ref[:, cols] = duk + dup_ref[:, cols]
        if d == 0:
            prod = (dy_ref[...] * u_ref[...]).reshape(rr // SUBLANES, SUBLANES, dm)
            dda[...] += jnp.sum(prod, axis=0)

        @pl.when(i == nc - 1)
        def _():
            dar_ref[...] = jnp.sum(accr[...], axis=0, keepdims=True)
            dai_ref[...] = jnp.sum(acci[...], axis=0, keepdims=True)
            if d == 0:
                dd_ref[...] = jnp.sum(dda[...], axis=0, keepdims=True)

    row_spec = pl.BlockSpec((rr, dm), lambda i: (chunk(i), 0))
    full = lambda a: pl.BlockSpec(a.shape, lambda i: (0,) * a.ndim)
    st_spec = pl.BlockSpec((None, SUBLANES, sw), lambda i: (chunk(i), 0, 0))
    ins = [u, dy] + ([du_prev] if d == 1 else []) + [a8r, a8i, bbr, bbi, ccr, cci] + ([dsk] if d == 0 else [])
    ins += [st_r, st_i]
    in_specs = [row_spec, row_spec] + ([row_spec] if d == 1 else []) + [full(a) for a in (a8r, a8i, bbr, bbi, ccr, cci)]
    in_specs += ([full(dsk)] if d == 0 else []) + [st_spec, st_spec]
    sd = jax.ShapeDtypeStruct
    out_shape = [sd((n, dm), F32), sd(bbr.shape, F32), sd(bbr.shape, F32), sd(ccr.shape, F32),
                 sd(ccr.shape, F32), sd((1, sw), F32), sd((1, sw), F32)] + ([sd((1, dm), F32)] if d == 0 else [])
    out_specs = [row_spec] + [pl.BlockSpec(s.shape, lambda i, nd=len(s.shape): (0,) * nd) for s in out_shape[1:]]
    scratch = [pltpu.VMEM((rr + 2 * SUBLANES, sw), F32), pltpu.VMEM((rr + 2 * SUBLANES, sw), F32),
               pltpu.VMEM((rr, sw), F32), pltpu.VMEM((rr, sw), F32)]
    scratch += [pltpu.VMEM((SUBLANES, sw), F32)] * 6 + [pltpu.VMEM((SUBLANES, dm), F32)]
    vmem = 4 * (rr + 16) * sw * 4 + 10 * rr * dm * 4 + 24 * nk * LANES * swk * 4 + (12 << 20)
    return pl.pallas_call(body, name=name, out_shape=out_shape, grid=(nc,), in_specs=in_specs, out_specs=out_specs,
                          scratch_shapes=scratch, compiler_params=_params(("arbitrary",), vmem))(*ins)


def _hg_chunk(q, v, fraw, l0, l1, st, *, rev):
    lb = _sigmoid(l1 - l0)
    logf = jnp.logaddexp(jnp.log(lb), jnp.log1p(-lb) + jax.nn.log_sigmoid(fraw))
    kk = (1.0 - lb) * _sigmoid(fraw * -1.0)
    r = lax.broadcasted_iota(jnp.int32, (HG_ROWS, HG_ROWS), 0)
    c = lax.broadcasted_iota(jnp.int32, (HG_ROWS, HG_ROWS), 1)
    same = (r & 3) == (c & 3)
    tri = jnp.logical_and(same, ((c >> 2) >= (r >> 2)) if rev else ((c >> 2) <= (r >> 2)))
    hi = lax.Precision.HIGHEST
    bcum = _dot(tri.astype(F32), logf, NN, hi)
    bend = _dot(same.astype(F32), logf, NN, hi)
    qd = q * jnp.exp(bcum)
    kd = kk * jnp.exp(-bcum)
    att = jnp.where(tri, _dot(qd.astype(BF16), kd.astype(BF16), NT), 0.0)
    o = _dot(att.astype(BF16), v.astype(BF16))
    ke = (kk * jnp.exp(bend - bcum)).astype(BF16)
    rowb = lax.broadcasted_iota(jnp.int32, (HG_ROWS, 1), 0) & 3
    r8 = lax.broadcasted_iota(jnp.int32, (SUBLANES, HG_ROWS), 0)
    c8 = lax.broadcasted_iota(jnp.int32, (SUBLANES, HG_ROWS), 1)
    bend8 = _dot(((c8 & 3) == r8).astype(F32), logf, NN, hi)
    new = []
    for b in range(LOCAL_B):
        mb = (rowb == b).astype(F32)
        o = o + mb * _dot(qd.astype(BF16), st[b].astype(BF16), NT)
        dec = jnp.exp(jnp.sum(jnp.where(r8 == b, bend8, 0.0), axis=0, keepdims=True))
        new.append(st[b] * dec + _dot((v * mb).astype(BF16), ke, TN))
    return o, jnp.stack(new)


def _hg_chunk_of(step, ncc, nc, rev):
    return _s5_chunk_of(step, ncc, nc, rev)


def _hg_fwd_dir(zz, lb2, *, d, name, dims):
    n = zz.shape[0]
    dm = zz.shape[1] // 5
    nh = dm // HG_HEAD
    nc, ncc = n // HG_ROWS, dims["nctx"] // HG_ROWS
    rev = d == 1
    ch = lambda i: _hg_chunk_of(i, ncc, nc, rev)

    def body(q_ref, v_ref, f_ref, l0_ref, l1_ref, o_ref, st_ref, st):
        @pl.when(pl.program_id(1) == 0)
        def _():
            st[...] = jnp.zeros(st.shape, F32)

        st_ref[...] = st[...]
        o, new = _hg_chunk(q_ref[...], v_ref[...], f_ref[...], l0_ref[...], l1_ref[...], st[...], rev=rev)
        o_ref[...] = o
        st[...] = new

    blk = lambda off: pl.BlockSpec((HG_ROWS, HG_HEAD), lambda h, i, off=off: (ch(i), off + h))
    lspec = lambda layer: pl.BlockSpec((None, None, 1, HG_HEAD), lambda h, i, layer=layer: (d, layer, 0, h))
    return pl.pallas_call(
        body, name=name,
        out_shape=[jax.ShapeDtypeStruct((n, dm), F32),
                   jax.ShapeDtypeStruct((nh, nc, LOCAL_B, HG_HEAD, HG_HEAD), F32)],
        grid=(nh, nc),
        in_specs=[blk(0), blk(nh), blk((2 + d) * nh), lspec(0), lspec(1)],
        out_specs=[pl.BlockSpec((HG_ROWS, HG_HEAD), lambda h, i: (ch(i), h)),
                   pl.BlockSpec((None, None, LOCAL_B, HG_HEAD, HG_HEAD), lambda h, i: (h, ch(i), 0, 0, 0))],
        scratch_shapes=[pltpu.VMEM((LOCAL_B, HG_HEAD, HG_HEAD), F32)],
        compiler_params=_params(("parallel", "arbitrary"), 32 << 20),
    )(zz, zz, zz, lb2, lb2)


def _hg_bwd_dir(zz, lb2, do, sts, dqv_prev, *, d, name, dims):
    n = zz.shape[0]
    dm = zz.shape[1] // 5
    nh = dm // HG_HEAD
    nc, ncc = n // HG_ROWS, dims["nctx"] // HG_ROWS
    rev = d == 1
    ch = lambda i: _hg_chunk_of(nc - 1 - i, ncc, nc, rev)
    qv_dtype = F32 if d == 0 else BF16

    def body(*refs):
        q_ref, v_ref, f_ref, l0_ref, l1_ref, do_ref, st_ref = refs[:7]
        pos = 7
        if d == 1:
            dqp_ref, dvp_ref = refs[7:9]
            pos = 9
        dq_ref, dv_ref, df_ref, dl_ref, dst = refs[pos:]
        i = pl.program_id(1)

        @pl.when(i == 0)
        def _():
            dst[...] = jnp.zeros(dst.shape, F32)
            dl_ref[...] = jnp.zeros(dl_ref.shape, F32)

        _, vjp = jax.vjp(functools.partial(_hg_chunk, rev=rev), q_ref[...], v_ref[...], f_ref[...], l0_ref[...],
                         l1_ref[...], st_ref[...])
        dq, dv, df, dl0, dl1, dstn = vjp((do_ref[...], dst[...]))
        dst[...] = dstn
        if d == 1:
            dq = dq + dqp_ref[...]
            dv = dv + dvp_ref[...]
        dq_ref[...] = dq.astype(qv_dtype)
        dv_ref[...] = dv.astype(qv_dtype)
        df_ref[...] = df.astype(BF16)
        dl_ref[0, pl.ds(0, 1), :] += dl0
        dl_ref[1, pl.ds(0, 1), :] += dl1

    blk = lambda off: pl.BlockSpec((HG_ROWS, HG_HEAD), lambda h, i, off=off: (ch(i), off + h))
    oblk = pl.BlockSpec((HG_ROWS, HG_HEAD), lambda h, i: (ch(i), h))
    lspec = lambda layer: pl.BlockSpec((None, None, 1, HG_HEAD), lambda h, i, layer=layer: (d, layer, 0, h))
    ins = [zz, zz, zz, lb2, lb2, do, sts] + (list(dqv_prev) if d == 1 else [])
    in_specs = [blk(0), blk(nh), blk((2 + d) * nh), lspec(0), lspec(1), oblk,
                pl.BlockSpec((None, None, LOCAL_B, HG_HEAD, HG_HEAD), lambda h, i: (h, ch(i), 0, 0, 0))]
    in_specs += [oblk, oblk] if d == 1 else []
    sd = jax.ShapeDtypeStruct
    return pl.pallas_call(
        body, name=name,
        out_shape=[sd((n, dm), qv_dtype), sd((n, dm), qv_dtype), sd((n, dm), BF16), sd((2, SUBLANES, dm), F32)],
        grid=(nh, nc), in_specs=in_specs,
        out_specs=[oblk, oblk, oblk, pl.BlockSpec((2, SUBLANES, HG_HEAD), lambda h, i: (0, 0, h))],
        scratch_shapes=[pltpu.VMEM((LOCAL_B, HG_HEAD, HG_HEAD), F32)],
        compiler_params=_params(("parallel", "arbitrary"), 48 << 20),
    )(*ins)


def _hg_readout(o, g, w):
    outs = []
    for h in range(o.shape[-1] // HG_HEAD):
        sl = slice(h * HG_HEAD, (h + 1) * HG_HEAD)
        oh = o[..., sl]
        outs.append(oh * _rms(oh) * w * _sigmoid(g[..., sl]))
    return jnp.concatenate(outs, axis=-1)


def _silu(x):
    return x * _sigmoid(x)


def _mod_fwd(craw, w, b):
    def body(c_ref, w_ref, b_ref, o_ref):
        s = _silu(c_ref[...]).astype(BF16)
        for layer in range(w.shape[0]):
            o_ref[layer] = _dot(s, w_ref[layer].astype(BF16)) + b_ref[layer]

    return pl.pallas_call(body, name="mod_fwd",
                          out_shape=jax.ShapeDtypeStruct((w.shape[0], craw.shape[0], w.shape[2]), F32),
                          compiler_params=_params(None, 40 << 20))(craw, w, b)


def _mod_bwd(craw, w, dlat_sh, dctx_sh, dlat_full, dctx_full):
    nl, dm, ns = w.shape
    nb = dlat_sh.shape[1]

    def body(c_ref, w_ref, dl_ref, dc_ref, dlf_ref, dcf_ref, dw_ref, db_ref, dcc_ref):
        craw_v = c_ref[...]
        s = _silu(craw_v)
        s_lat = s[:nb].astype(BF16)
        s_ctx = s[nb:].astype(BF16)
        row = lax.broadcasted_iota(jnp.int32, (SUBLANES, ns), 0)
        dsc = jnp.zeros((SUBLANES, dm), F32)
        for layer in range(nl):
            tot = dc_ref[0, pl.ds(layer, 1), :]
            totf = dcf_ref[0, pl.ds(layer, 1), :]
            for i in range(1, NDEV):
                tot = tot + dc_ref[i, pl.ds(layer, 1), :]
                totf = totf + dcf_ref[i, pl.ds(layer, 1), :]
            dc8 = jnp.where(row == 0, jnp.broadcast_to(tot, (SUBLANES, ns)), 0.0).astype(BF16)
            dw_ref[layer] = _dot(s_lat, dl_ref[layer].astype(BF16), TN) + _dot(s_ctx, dc8, TN)
            db_ref[layer] = jnp.sum(dlf_ref[layer], axis=0, keepdims=True) + totf
            dsc = dsc + _dot(dc8, w_ref[layer].astype(BF16), NT)
        cc = craw_v[nb:]
        sg = _sigmoid(cc)
        dcc_ref[...] = dsc * (sg * (1.0 + cc * (1.0 - sg)))

    sd = jax.ShapeDtypeStruct
    return pl.pallas_call(body, name="mod_bwd",
                          out_shape=[sd((nl, dm, ns), F32), sd((nl, 1, dlat_full.shape[2]), F32), sd((SUBLANES, dm), F32)],
                          compiler_params=_params(None, 48 << 20))(craw, w, dlat_sh, dctx_sh, dlat_full, dctx_full)


def _adam_rows(r):
    best = None
    for t in range(SUBLANES, min(r, 128) + 1, SUBLANES):
        if r % t == 0:
            best = t
    return best if best is not None else r


def _adamw(parts, w, m, v, *, name):
    npart, r, c = parts.shape
    tr = _adam_rows(r)

    def body(p_ref, w_ref, m_ref, v_ref, g_ref, d_ref, nm_ref, nv_ref):
        g = p_ref[0]
        for i in range(1, npart):
            g = g + p_ref[i]
        nm = ADAM_B1 * m_ref[...] + (1.0 - ADAM_B1) * g
        nv = ADAM_B2 * v_ref[...] + (1.0 - ADAM_B2) * (g * g)
        m_hat = nm / (1.0 - ADAM_B1 ** ADAM_STEP)
        v_hat = nv / (1.0 - ADAM_B2 ** ADAM_STEP)
        g_ref[...] = g
        d_ref[...] = -ADAM_LR * (m_hat / (jnp.sqrt(v_hat) + ADAM_EPS) + ADAM_WD * w_ref[...])
        nm_ref[...] = nm
        nv_ref[...] = nv

    spec = pl.BlockSpec((tr, c), lambda i: (i, 0))
    vmem = 2 * (npart + 7) * tr * c * 4 + (8 << 20)
    return pl.pallas_call(
        body, name=name, out_shape=[jax.ShapeDtypeStruct((r, c), F32)] * 4, grid=(r // tr,),
        in_specs=[pl.BlockSpec((npart, tr, c), lambda i: (0, i, 0)), spec, spec, spec], out_specs=[spec] * 4,
        compiler_params=_params(("parallel",), vmem),
    )(parts, w, m, v)


def _pack(arrs, width):
    rows, sizes = [], []
    for a in arrs:
        f = a.reshape(-1).astype(F32)
        nr = -(-f.shape[0] // width)
        rows.append(jnp.pad(f, (0, nr * width - f.shape[0])).reshape(nr, width))
        sizes.append(nr)
    tot = sum(sizes)
    pad = -tot % SUBLANES
    if pad:
        rows.append(jnp.zeros((pad, width), F32))
    return jnp.concatenate(rows, axis=0), sizes


def _unpack(buf, sizes, shapes):
    out, r0 = [], 0
    for nr, shp in zip(sizes, shapes):
        cnt = 1
        for s in shp:
            cnt *= s
        out.append(buf[r0:r0 + nr].reshape(-1)[:cnt].reshape(shp))
        r0 += nr
    return out


def _to_tm(a):
    return jnp.transpose(a, (1, 0, 2)).reshape(a.shape[1] * a.shape[0], a.shape[2])


def _pattern(mod_lat, mod_ctx, m, dm):
    lat = mod_lat[:, m * dm:(m + 1) * dm]
    ctx = jnp.broadcast_to(mod_ctx[None, m * dm:(m + 1) * dm], (SUBLANES, dm))
    return jnp.stack([ctx, jnp.concatenate([lat, lat], axis=0)])


def _blockdiag_b(bt, nk):
    g, h, p = bt.shape
    t = bt.reshape(nk, 8, h, p)
    return jnp.einsum("kghp,gj->kghjp", t, jnp.eye(8, dtype=bt.dtype)).reshape(nk, 8 * h, 8 * p)


def _blockdiag_c(ct, nk):
    g, h, p = ct.shape
    t = ct.reshape(nk, 8, h, p)
    return jnp.einsum("kghp,gj->kgpjh", t, jnp.eye(8, dtype=ct.dtype)).reshape(nk, 8 * p, 8 * h)


def _diag_b(dbb, h, p):
    nk = dbb.shape[0]
    return jnp.einsum("kghgp->kghp", dbb.reshape(nk, 8, h, 8, p)).reshape(nk * 8, h, p)


def _diag_c(dcc, h, p):
    nk = dcc.shape[0]
    return jnp.einsum("kgpgh->kghp", dcc.reshape(nk, 8, p, 8, h)).reshape(nk * 8, h, p)


def kernel(x, c, ctx, c_ctx, w_mod, b_mod, norm1_w, norm2_w, final_norm_w, s5_w_in, s5_lam_re, s5_lam_im, s5_log_step, s5_b_re, s5_b_im, s5_c_re, s5_c_im, s5_d, s5_w_glu, s5_w_out, hg_w_in, hg_lower_bounds, hg_gnorm_w, hg_w_out, ffn_w_up, ffn_conv_w, ffn_conv_b, ffn_w_down, loss_target, m_c_ctx, m_w_mod, m_b_mod, m_norm1_w, m_norm2_w, m_final_norm_w, m_s5_w_in, m_s5_lam_re, m_s5_lam_im, m_s5_log_step, m_s5_b_re, m_s5_b_im, m_s5_c_re, m_s5_c_im, m_s5_d, m_s5_w_glu, m_s5_w_out, m_hg_w_in, m_hg_lower_bounds, m_hg_gnorm_w, m_hg_w_out, m_ffn_w_up, m_ffn_conv_w, m_ffn_conv_b, m_ffn_w_down, v_c_ctx, v_w_mod, v_b_mod, v_norm1_w, v_norm2_w, v_final_norm_w, v_s5_w_in, v_s5_lam_re, v_s5_lam_im, v_s5_log_step, v_s5_b_re, v_s5_b_im, v_s5_c_re, v_s5_c_im, v_s5_d, v_s5_w_glu, v_s5_w_out, v_hg_w_in, v_hg_lower_bounds, v_hg_gnorm_w, v_hg_w_out, v_ffn_w_up, v_ffn_conv_w, v_ffn_conv_b, v_ffn_w_down):
    given = dict(locals())
    bsz, lx, dm = x.shape
    lc = ctx.shape[1]
    assert bsz == LOCAL_B and w_mod.shape[0] == 2 and dm % LANES == 0
    n, nctx = (lc + lx) * bsz, lc * bsz
    ngrp, nstate, hgrp = dm // S5_GROUP, S5_STATE, S5_GROUP
    nk = dm // LANES
    dims = dict(nctx=nctx, tm=min(512, nctx), tm_row=min(256, nctx), s5_rows=min(256, nctx),
                s5_lane_block=min(512, 8 * nstate))
    tm = dims["tm"]
    assert nctx % HG_ROWS == 0 and (lx * bsz) % nctx == 0 and lc % GRID_W == 0 and lc & (lc - 1) == 0
    me = 4 * lax.axis_index("x") + 2 * lax.axis_index("y") + lax.axis_index("c")

    wnames = ["s5_w_in", "s5_w_glu", "s5_w_out", "hg_w_in", "hg_w_out", "ffn_w_up", "ffn_w_down"]
    gath = _exchange([given[k].astype(BF16) for k in wnames] + [c, hg_lower_bounds, ffn_conv_w], a2a=False,
                     name="gather_weights")
    gw = dict(zip(wnames, gath[:7]))
    c_all, lb_all, cw_all = gath[7:]
    w_s5in = gw["s5_w_in"].reshape(1, dm, dm)
    w_glu = gw["s5_w_glu"].reshape(1, dm, dm)
    w_s5out = gw["s5_w_out"].reshape(1, dm, dm)
    w_hgin = gw["hg_w_in"][:, 0]
    w_hgout = gw["hg_w_out"].reshape(1, dm, dm)
    ns_up = ffn_w_up.shape[2]
    w_up = [gw["ffn_w_up"][:, layer] for layer in range(2)]
    w_dn = [gw["ffn_w_down"][:, layer].reshape(NDEV // 2, -1, dm) for layer in range(2)]
    cw = [cw_all[:, layer].reshape(2, NDEV // 2, 3, ns_up) for layer in range(2)]
    cb = [ffn_conv_b[layer].reshape(2, NDEV // 2, 1, ns_up) for layer in range(2)]
    lb2 = jnp.transpose(lb_all, (1, 2, 0, 3)).reshape(2, 2, 1, dm)

    nsm = w_mod.shape[2]
    craw = jnp.concatenate([c_all.reshape(NDEV * bsz, dm), c_ctx[None], jnp.zeros((SUBLANES - 1, dm), F32)], axis=0)
    b_sh = lax.dynamic_slice(b_mod, (0, me * nsm), (2, nsm)).reshape(2, 1, nsm)
    mod_sh = _mod_fwd(craw, w_mod, b_sh)
    (mod_g,) = _exchange([mod_sh], a2a=False, name="gather_mod")
    mod_full = jnp.transpose(mod_g, (1, 2, 0, 3)).reshape(2, craw.shape[0], NDEV * nsm)
    pat = []
    for layer in range(2):
        mlat = lax.dynamic_slice(mod_full[layer], (me * bsz, 0), (bsz, N_MOD * dm))
        mctx = mod_full[layer, NDEV * bsz]
        pat.append([_pattern(mlat, mctx, m, dm) for m in range(N_MOD)])

    lr4 = s5_lam_re[0].reshape(2, ngrp, 1, nstate)
    li4 = s5_lam_im[0].reshape(2, ngrp, 1, nstate)
    ls4 = s5_log_step[0].reshape(2, ngrp, 1, 1)
    brt = jnp.transpose(s5_b_re[0], (0, 1, 3, 2))
    bit = jnp.transpose(s5_b_im[0], (0, 1, 3, 2))
    abar_r, abar_i, bbar_r, bbar_i = _s5_disc_fwd(lr4, li4, ls4, brt, bit)
    sw = ngrp * nstate
    a8r = [jnp.broadcast_to(abar_r[d].reshape(1, sw), (SUBLANES, sw)) for d in range(2)]
    a8i = [jnp.broadcast_to(abar_i[d].reshape(1, sw), (SUBLANES, sw)) for d in range(2)]
    bbr = [_blockdiag_b(bbar_r[d], nk).astype(BF16) for d in range(2)]
    bbi = [_blockdiag_b(bbar_i[d], nk).astype(BF16) for d in range(2)]
    ccr = [_blockdiag_c(s5_c_re[0, d], nk).astype(BF16) for d in range(2)]
    cci = [_blockdiag_c(s5_c_im[0, d], nk).astype(BF16) for d in range(2)]
    dsk = s5_d.reshape(1, dm)

    z0 = jnp.concatenate([_to_tm(ctx), _to_tm(x)], axis=0)
    tgt = _to_tm(loss_target)
    n1w = [norm1_w[layer].reshape(1, dm) for layer in range(2)]
    n2w = [norm2_w[layer].reshape(1, dm) for layer in range(2)]

    def ffn_fwd(layer, h2):
        u = _mm_fwd(h2, w_up[layer], name=f"ffn_up{layer}", tm=tm, shard_out=True)
        u4 = u.reshape(2, NDEV // 2, n, ns_up)
        hm = _convffn_fwd(u4, cw[layer], cb[layer], name=f"convffn_fwd{layer}", dims=dims)
        f = _mm_down_fwd(hm, w_dn[layer], name=f"ffn_down{layer}", tm=tm)
        return u4, hm, f

    _, h0 = _norm_mod_fwd(z0, n1w[0], pat[0][0], pat[0][1], name="norm1_l0", dims=dims)
    u_s5 = _mm_fwd(h0, w_s5in, name="s5_in", tm=tm)
    y_a, st0r, st0i = _s5_fwd_dir(u_s5, None, a8r[0], a8i[0], bbr[0], bbi[0], ccr[0], cci[0], dsk, d=0,
                                  name="s5_fwd_d0", dims=dims)
    y_s5, st1r, st1i = _s5_fwd_dir(u_s5, y_a, a8r[1], a8i[1], bbr[1], bbi[1], ccr[1], cci[1], dsk, d=1,
                                   name="s5_fwd_d1", dims=dims)
    (zg,) = _rowk(lambda rv, pv, cv, il: ([_gelu(rv[0])], [], []), name="s5_gelu", n=n, tm=dims["tm_row"], nctx=nctx,
                  rows=[(y_s5, dm, 0, 0)], out_rows=[(dm, BF16, dm, 0)])
    t_glu = _mm_fwd(zg, w_glu, name="s5_glu", tm=tm)
    (z2g,) = _rowk(lambda rv, pv, cv, il: ([rv[0] * _sigmoid(rv[1])], [], []), name="s5_gate", n=n,
                   tm=dims["tm_row"], nctx=nctx, rows=[(zg, dm, 0, 0), (t_glu, dm, 0, 0)],
                   out_rows=[(dm, BF16, dm, 0)])
    ymix0 = _mm_fwd(z2g, w_s5out, name="s5_out", tm=tm)
    z1_l0, h2_l0 = _norm_mod_fwd(z0, n2w[0], pat[0][3], pat[0][4], name="norm2_l0", dims=dims,
                                 res=(ymix0, pat[0][2]))
    u4_l0, hm_l0, f_l0 = ffn_fwd(0, h2_l0)

    z2_l0, h1 = _norm_mod_fwd(z1_l0, n1w[1], pat[1][0], pat[1][1], name="norm1_l1", dims=dims,
                              res=(f_l0, pat[0][5]))
    zz = _mm_fwd(h1, w_hgin, name="hg_in", tm=tm)
    o_f, sts_f = _hg_fwd_dir(zz, lb2, d=0, name="hg_fwd_d0", dims=dims)
    o_b, sts_b = _hg_fwd_dir(zz, lb2, d=1, name="hg_fwd_d1", dims=dims)
    gnw = hg_gnorm_w.reshape(1, HG_HEAD)
    (og,) = _rowk(lambda rv, pv, cv, il: ([_hg_readout(rv[0] + rv[1], rv[2], cv[0])], [], []), name="hg_readout",
                  n=n, tm=dims["tm_row"], nctx=nctx, rows=[(o_f, dm, 0, 0), (o_b, dm, 0, 0), (zz, dm, 4, 0)],
                  consts=[gnw], out_rows=[(dm, BF16, dm, 0)])
    ymix1 = _mm_fwd(og, w_hgout, name="hg_out", tm=tm)
    z1_l1, h2_l1 = _norm_mod_fwd(z2_l0, n2w[1], pat[1][3], pat[1][4], name="norm2_l1", dims=dims,
                                 res=(ymix1, pat[1][2]))
    u4_l1, hm_l1, f_l1 = ffn_fwd(1, h2_l1)

    dz, df, dgate2_l1, loss_part, dfinal_w = _loss_bwd(z1_l1, f_l1, pat[1][5], tgt, final_norm_w.reshape(1, dm),
                                                        name="loss_bwd", dims=dims)

    def ffn_bwd(layer, df_, u4, hm, h2):
        dhm = _mm_down_bwd_in(df_, w_dn[layer], name=f"ffn_down_bwd_in{layer}", tm=tm)
        dwd = _mm_down_bwd_w(hm, df_, name=f"ffn_down_bwd_w{layer}", tm=tm)
        du4, dcw, dcb = _convffn_bwd(u4, dhm, cw[layer], cb[layer], name=f"convffn_bwd{layer}", dims=dims)
        du = du4.reshape(NDEV, n, ns_up)
        dh2 = _mm_bwd_in(du, w_up[layer], name=f"ffn_up_bwd_in{layer}", tm=tm, shard_in=True)
        dwu = _mm_bwd_w(h2, du, name=f"ffn_up_bwd_w{layer}", tm=tm, s=NDEV, ns=ns_up, shard_in=True)
        return dh2, dwu, dwd, dcw, dcb

    dh2, dwu_l1, dwd_l1, dcw_l1, dcb_l1 = ffn_bwd(1, df, u4_l1, hm_l1, h2_l1)
    dz, dymix, dsh2_l1, dsc2_l1, dgate1_l1, dn2w_l1 = _norm_mod_bwd(dh2, z1_l1, dz, n2w[1], pat[1][4], name="norm2_bwd_l1",
                                                                    dims=dims, res=(ymix1, pat[1][2]))
    dog = _mm_bwd_in(dymix, w_hgout, name="hg_out_bwd_in", tm=tm)
    dw_hgout = _mm_bwd_w(og, dymix, name="hg_out_bwd_w", tm=tm, s=1, ns=dm)

    def readout_bwd(rv, pv, cv, il):
        _, vjp = jax.vjp(_hg_readout, rv[0] + rv[1], rv[2], cv[0])
        do, dg, dw = vjp(rv[3])
        return [do, dg], [], [jnp.broadcast_to(dw, (SUBLANES, HG_HEAD)) * (1.0 / SUBLANES)]

    do, dg, dgnw = _rowk(readout_bwd, name="hg_readout_bwd", n=n, tm=dims["tm_row"], nctx=nctx,
                         rows=[(o_f, dm, 0, 0), (o_b, dm, 0, 0), (zz, dm, 4, 0), (dog, dm, 0, 0)], consts=[gnw],
                         out_rows=[(dm, F32, dm, 0), (dm, BF16, dm, 0)], out_acc=[HG_HEAD])
    dq0, dv0, dff, dl_f = _hg_bwd_dir(zz, lb2, do, sts_f, None, d=0, name="hg_bwd_d0", dims=dims)
    dq, dv, dfb, dl_b = _hg_bwd_dir(zz, lb2, do, sts_b, (dq0, dv0), d=1, name="hg_bwd_d1", dims=dims)
    dzz = jnp.concatenate([dq, dv, dff, dfb, dg], axis=1)
    dh1 = _mm_bwd_in(dzz, w_hgin, name="hg_in_bwd_in", tm=tm)
    dw_hgin = _mm_bwd_w(h1, dzz, name="hg_in_bwd_w", tm=tm, s=NDEV, ns=w_hgin.shape[2])
    dz, df0, dsh1_l1, dsc1_l1, dgate2_l0, dn1w_l1 = _norm_mod_bwd(dh1, z2_l0, dz, n1w[1], pat[1][1], name="norm1_bwd_l1",
                                                                  dims=dims, res=(f_l0, pat[0][5]))
    dh2, dwu_l0, dwd_l0, dcw_l0, dcb_l0 = ffn_bwd(0, df0, u4_l0, hm_l0, h2_l0)
    dz, dymix, dsh2_l0, dsc2_l0, dgate1_l0, dn2w_l0 = _norm_mod_bwd(dh2, z1_l0, dz, n2w[0], pat[0][4], name="norm2_bwd_l0",
                                                                    dims=dims, res=(ymix0, pat[0][2]))
    dz2g = _mm_bwd_in(dymix, w_s5out, name="s5_out_bwd_in", tm=tm)
    dw_s5out = _mm_bwd_w(z2g, dymix, name="s5_out_bwd_w", tm=tm, s=1, ns=dm)

    def gate_bwd(rv, pv, cv, il):
        sg = _sigmoid(rv[1])
        return [rv[2] * rv[0] * sg * (1.0 - sg), rv[2] * sg], [], []

    dt_glu, dzg_a = _rowk(gate_bwd, name="s5_gate_bwd", n=n, tm=dims["tm_row"], nctx=nctx,
                          rows=[(zg, dm, 0, 0), (t_glu, dm, 0, 0), (dz2g, dm, 0, 0)],
                          out_rows=[(dm, BF16, dm, 0), (dm, F32, dm, 0)])
    dzg_b = _mm_bwd_in(dt_glu, w_glu, name="s5_glu_bwd_in", tm=tm)
    dw_glu = _mm_bwd_w(zg, dt_glu, name="s5_glu_bwd_w", tm=tm, s=1, ns=dm)

    def gelu_bwd(rv, pv, cv, il):
        _, vjp = jax.vjp(_gelu, rv[0])
        return [vjp(rv[1] + rv[2])[0]], [], []

    (dy_s5,) = _rowk(gelu_bwd, name="s5_gelu_bwd", n=n, tm=dims["tm_row"], nctx=nctx,
                     rows=[(y_s5, dm, 0, 0), (dzg_a, dm, 0, 0), (dzg_b, dm, 0, 0)], out_rows=[(dm, F32, dm, 0)])
    du_a, dbbr0, dbbi0, dccr0, dcci0, dar0, dai0, ddsk = _s5_bwd_dir(
        u_s5, dy_s5, None, a8r[0], a8i[0], bbr[0], bbi[0], ccr[0], cci[0], dsk, st0r, st0i, d=0, name="s5_bwd_d0",
        dims=dims)
    du_s5, dbbr1, dbbi1, dccr1, dcci1, dar1, dai1 = _s5_bwd_dir(
        u_s5, dy_s5, du_a, a8r[1], a8i[1], bbr[1], bbi[1], ccr[1], cci[1], dsk, st1r, st1i, d=1, name="s5_bwd_d1",
        dims=dims)
    dh0 = _mm_bwd_in(du_s5, w_s5in, name="s5_in_bwd_in", tm=tm)
    dw_s5in = _mm_bwd_w(h0, du_s5, name="s5_in_bwd_w", tm=tm, s=1, ns=dm)
    dz0, dsh1_l0, dsc1_l0, dn1w_l0 = _norm_mod_bwd(dh0, z0, dz, n1w[0], pat[0][1], name="norm1_bwd_l0", dims=dims)

    dar = jnp.stack([dar0, dar1]).reshape(2, ngrp, 1, nstate)
    dai = jnp.stack([dai0, dai1]).reshape(2, ngrp, 1, nstate)
    dbbr = jnp.stack([_diag_b(dbbr0, hgrp, nstate), _diag_b(dbbr1, hgrp, nstate)])
    dbbi = jnp.stack([_diag_b(dbbi0, hgrp, nstate), _diag_b(dbbi1, hgrp, nstate)])
    dlr, dli, dls, dbrt, dbit = _s5_disc_bwd(lr4, li4, ls4, brt, bit, dar, dai, dbbr, dbbi)
    g_c_re = jnp.stack([_diag_c(dccr0, hgrp, nstate), _diag_c(dccr1, hgrp, nstate)])
    g_c_im = jnp.stack([_diag_c(dcci0, hgrp, nstate), _diag_c(dcci1, hgrp, nstate)])

    zeros8 = jnp.zeros((SUBLANES, dm), F32)
    dmod = jnp.stack([
        jnp.concatenate([dsh1_l0, dsc1_l0, dgate1_l0, dsh2_l0, dsc2_l0, dgate2_l0], axis=1),
        jnp.concatenate([dsh1_l1, dsc1_l1, dgate1_l1, dsh2_l1, dsc2_l1, dgate2_l1], axis=1)])
    del zeros8
    (dmod_g,) = _exchange([dmod], a2a=False, name="gather_dmod")
    dlat_full = jnp.transpose(dmod_g[:, :, :bsz], (1, 0, 2, 3)).reshape(2, NDEV * bsz, N_MOD * dm)
    dctx_full = dmod_g[:, :, bsz]
    dlat_sh = lax.dynamic_slice(dlat_full, (0, 0, me * nsm), (2, NDEV * bsz, nsm))
    dctx_sh = lax.dynamic_slice(dctx_full, (0, 0, me * nsm), (NDEV, 2, nsm))
    g_w_mod, g_b_mod, dcctx8 = _mod_bwd(craw, w_mod, dlat_sh, dctx_sh, dlat_full, dctx_full)

    big = [dw_s5in.reshape(NDEV, -1, dm), dw_glu.reshape(NDEV, -1, dm), dw_s5out.reshape(NDEV, -1, dm), dw_hgin,
           dw_hgout.reshape(NDEV, -1, dm), dwu_l0, dwu_l1, dwd_l0.reshape(NDEV, -1, dm), dwd_l1.reshape(NDEV, -1, dm),
           jnp.stack([dcw_l0.reshape(NDEV, 3, ns_up), dcw_l1.reshape(NDEV, 3, ns_up)], axis=1)]
    parts = _exchange(big, a2a=True, name="scatter_grads")
    (p_s5in, p_glu, p_s5out, p_hgin, p_hgout, p_up0, p_up1, p_dn0, p_dn1, p_cw) = parts

    dl_hg = jnp.stack([dl_f[:, 0], dl_b[:, 0]])
    small_names = ["c_ctx", "norm1_w", "norm2_w", "final_norm_w", "s5_lam_re", "s5_lam_im", "s5_log_step", "s5_b_re",
                   "s5_b_im", "s5_c_re", "s5_c_im", "s5_d", "hg_gnorm_w", "ffn_conv_b"]
    small_grads = [dcctx8[0], jnp.concatenate([dn1w_l0, dn1w_l1]), jnp.concatenate([dn2w_l0, dn2w_l1]), dfinal_w,
                   dlr, dli, dls, jnp.transpose(dbrt, (0, 1, 3, 2)), jnp.transpose(dbit, (0, 1, 3, 2)), g_c_re, g_c_im,
                   ddsk, dgnw, jnp.stack([dcb_l0.reshape(-1), dcb_l1.reshape(-1)])]
    width = 8 * LANES
    gbuf, sizes = _pack(small_grads + [dl_hg, loss_part], width)
    (gbuf_g,) = _exchange([gbuf], a2a=False, name="gather_small")
    nsm_rows = gbuf.shape[0]
    wbuf, _ = _pack([given[k] for k in small_names] + [jnp.zeros_like(dl_hg), jnp.zeros_like(loss_part)], width)
    mbuf, _ = _pack([given["m_" + k] for k in small_names] + [jnp.zeros_like(dl_hg), jnp.zeros_like(loss_part)], width)
    vbuf, _ = _pack([given["v_" + k] for k in small_names] + [jnp.zeros_like(dl_hg), jnp.zeros_like(loss_part)], width)
    sg, sdl, snm, snv = _adamw(gbuf_g, wbuf, mbuf, vbuf, name="adamw_small")
    shapes = [given[k].shape for k in small_names] + [dl_hg.shape, loss_part.shape]
    sg_l, sdl_l, snm_l, snv_l = (_unpack(b, sizes, shapes) for b in (sg, sdl, snm, snv))
    res = {k: (sg_l[i], sdl_l[i], snm_l[i], snv_l[i]) for i, k in enumerate(small_names)}
    loss = jnp.sum(sg_l[-1])
    dl_tot = sg_l[-2]
    nlb = hg_lower_bounds.shape[2]
    g_lb = lax.dynamic_slice(dl_tot, (0, 0, me * nlb), (2, 2, nlb))

    def adam_local(name, g, shape2):
        w, m, v = given[name], given["m_" + name], given["v_" + name]
        out = _adamw(g.reshape((1,) + shape2), w.reshape(shape2), m.reshape(shape2), v.reshape(shape2),
                     name="adamw_" + name)
        return tuple(o.reshape(w.shape) for o in out)

    def adam_parts(name, p):
        w, m, v = given[name], given["m_" + name], given["v_" + name]
        shape2 = (p.shape[0], -1, w.shape[-1])
        p3 = p.reshape(shape2)
        s2 = p3.shape[1:]
        out = _adamw(p3, w.reshape(s2), m.reshape(s2), v.reshape(s2), name="adamw_" + name)
        return tuple(o.reshape(w.shape) for o in out)

    res["hg_lower_bounds"] = adam_local("hg_lower_bounds", g_lb, (2 * 2, nlb))
    res["w_mod"] = adam_local("w_mod", g_w_mod, (2 * dm, nsm))
    res["b_mod"] = adam_local("b_mod", g_b_mod, (2, N_MOD * dm))
    res["s5_w_in"] = adam_parts("s5_w_in", p_s5in)
    res["s5_w_glu"] = adam_parts("s5_w_glu", p_glu)
    res["s5_w_out"] = adam_parts("s5_w_out", p_s5out)
    res["hg_w_in"] = adam_parts("hg_w_in", p_hgin)
    res["hg_w_out"] = adam_parts("hg_w_out", p_hgout)
    res["ffn_w_up"] = adam_parts("ffn_w_up", jnp.stack([p_up0, p_up1], axis=1))
    res["ffn_w_down"] = adam_parts("ffn_w_down", jnp.stack([p_dn0, p_dn1], axis=1))
    res["ffn_conv_w"] = adam_parts("ffn_conv_w", p_cw)

    grad_x = jnp.transpose(dz0[nctx:].reshape(lx, bsz, dm), (1, 0, 2))
    order = ["c_ctx", "w_mod", "b_mod", "norm1_w", "norm2_w", "final_norm_w", "s5_w_in", "s5_lam_re", "s5_lam_im",
             "s5_log_step", "s5_b_re", "s5_b_im", "s5_c_re", "s5_c_im", "s5_d", "s5_w_glu", "s5_w_out", "hg_w_in",
             "hg_lower_bounds", "hg_gnorm_w", "hg_w_out", "ffn_w_up", "ffn_conv_w", "ffn_conv_b", "ffn_w_down"]
    outs = [loss, grad_x]
    for j in range(4):
        outs += [res[k][j].reshape(given[k].shape) for k in order]
    return tuple(outs)
```

```python
import functools

import jax
import jax.numpy as jnp
from jax import lax
from jax.experimental import pallas as pl
from jax.experimental.pallas import tpu as pltpu

F32 = jnp.float32
BF16 = jnp.bfloat16
NDEV = 8
LOCAL_B = 4
NORM_EPS = 1e-6
N_MOD = 6
S5_GROUP = 16
S5_STATE = 64
S5_LAM_RE_MAX = -1e-4
HG_HEAD = 128
HG_ROWS = 128
GRID_W = 64
ADAM_LR, ADAM_B1, ADAM_B2, ADAM_EPS, ADAM_WD, ADAM_STEP = 0.001, 0.9, 0.999, 1e-08, 0.01, 10
VMEM_BYTES_V7X = 64 * 1024 * 1024
LANES = 128
SUBLANES = 8

NN = (((1,), (0,)), ((), ()))
NT = (((1,), (1,)), ((), ()))
TN = (((0,), (0,)), ((), ()))
MESH = pl.DeviceIdType.MESH


def _params(sem=None, vmem=None):
    kw = {}
    if sem is not None:
        kw["dimension_semantics"] = sem
    if vmem is not None:
        kw["vmem_limit_bytes"] = int(min(vmem, VMEM_BYTES_V7X - (4 << 20)))
    return pltpu.CompilerParams(**kw)


def _nbytes(shape, dtype):
    n = 1
    for s in shape:
        n *= 1 if s is None else s
    return n * jnp.dtype(dtype).itemsize


def _dot(a, b, dims=NN, precision=None):
    return lax.dot_general(a, b, dims, preferred_element_type=F32, precision=precision)


def _sigmoid(x):
    return 1.0 / (1.0 + jnp.exp(-x))


def _exchange(arrs, *, a2a, name):
    n = len(arrs)
    out_shape = [jax.ShapeDtypeStruct(a.shape if a2a else (NDEV,) + a.shape, a.dtype) for a in arrs]

    def body(*refs):
        ins, outs = refs[:n], refs[n:2 * n]
        send_sems, recv_sems, loc_sems = refs[2 * n:]
        x, y, c = lax.axis_index("x"), lax.axis_index("y"), lax.axis_index("c")
        me = 4 * x + 2 * y + c
        local, sends, recvs = [], [], []
        for a in range(n):
            src = ins[a].at[me] if a2a else ins[a]
            lc = pltpu.make_async_copy(src, outs[a].at[me], loc_sems.at[a])
            lc.start()
            local.append(lc)
        for a in range(n):
            for k in range(1, NDEV):
                px = (1 - x) if (k >> 2) & 1 else x
                py = (1 - y) if (k >> 1) & 1 else y
                pc = (1 - c) if k & 1 else c
                p = 4 * px + 2 * py + pc
                s = a * (NDEV - 1) + k - 1
                src = ins[a].at[p] if a2a else ins[a]
                cp = pltpu.make_async_remote_copy(src_ref=src, dst_ref=outs[a].at[me], send_sem=send_sems.at[s],
                                                  recv_sem=recv_sems.at[s], device_id=(px, py, pc), device_id_type=MESH)
                cp.start()
                sends.append(cp)
                recvs.append(pltpu.make_async_remote_copy(src_ref=src, dst_ref=outs[a].at[p], send_sem=send_sems.at[s],
                                                          recv_sem=recv_sems.at[s], device_id=(px, py, pc),
                                                          device_id_type=MESH))
        for cp in sends:
            cp.wait_send()
        for cp in recvs:
            cp.wait_recv()
        for lc in local:
            lc.wait()

    res = pl.pallas_call(
        body, name=name, out_shape=out_shape,
        in_specs=[pl.BlockSpec(memory_space=pl.ANY)] * n,
        out_specs=[pl.BlockSpec(memory_space=pl.ANY)] * n,
        scratch_shapes=[pltpu.SemaphoreType.DMA((n * (NDEV - 1),)), pltpu.SemaphoreType.DMA((n * (NDEV - 1),)),
                        pltpu.SemaphoreType.DMA((n,))],
    )(*arrs)
    return list(res)


def _mm(a, b, *, name, grid, a_spec, b_spec, o_spec, o_shape, o_dtype, dims):
    nk = grid[2]
    o_block = tuple(s for s in o_spec.block_shape if s is not None)

    def body(a_ref, b_ref, o_ref, *scr):
        r = _dot(a_ref[...].astype(BF16), b_ref[...].astype(BF16), dims)
        if nk == 1:
            o_ref[...] = r.astype(o_dtype)
        else:
            acc = scr[0]
            k = pl.program_id(2)

            @pl.when(k == 0)
            def _():
                acc[...] = r

            @pl.when(k > 0)
            def _():
                acc[...] += r

            @pl.when(k == nk - 1)
            def _():
                o_ref[...] = acc[...].astype(o_dtype)

    blocks = (_nbytes(a_spec.block_shape, a.dtype) + _nbytes(b_spec.block_shape, b.dtype) + _nbytes(o_block, o_dtype))
    scratch = [pltpu.VMEM(o_block, F32)] if nk > 1 else []
    vmem = 2 * blocks + 3 * _nbytes(o_block, F32) + (8 << 20)
    return pl.pallas_call(
        body, name=name, out_shape=jax.ShapeDtypeStruct(o_shape, o_dtype), grid=grid,
        in_specs=[a_spec, b_spec], out_specs=o_spec, scratch_shapes=scratch,
        compiler_params=_params(("parallel", "parallel", "arbitrary"), vmem),
    )(a, b)


def _mm_fwd(a, w3, *, name, tm, shard_out=False, o_dtype=F32):
    n, kk = a.shape
    s, _, ns = w3.shape
    if shard_out:
        o_shape, o_spec = (s, n, ns), pl.BlockSpec((None, tm, ns), lambda i, j, k: (j, i, 0))
    else:
        o_shape, o_spec = (n, s * ns), pl.BlockSpec((tm, ns), lambda i, j, k: (i, j))
    return _mm(a, w3, name=name, grid=(n // tm, s, 1), dims=NN, o_shape=o_shape, o_dtype=o_dtype, o_spec=o_spec,
               a_spec=pl.BlockSpec((tm, kk), lambda i, j, k: (i, 0)),
               b_spec=pl.BlockSpec((None, kk, ns), lambda i, j, k: (j, 0, 0)))


def _mm_bwd_in(dy, w3, *, name, tm, shard_in=False, o_dtype=F32):
    s, kk, ns = w3.shape
    if shard_in:
        n = dy.shape[1]
        a_spec = pl.BlockSpec((None, tm, ns), lambda i, j, k: (k, i, 0))
    else:
        n = dy.shape[0]
        a_spec = pl.BlockSpec((tm, ns), lambda i, j, k: (i, k))
    return _mm(dy, w3, name=name, grid=(n // tm, 1, s), dims=NT, o_shape=(n, kk), o_dtype=o_dtype,
               o_spec=pl.BlockSpec((tm, kk), lambda i, j, k: (i, 0)), a_spec=a_spec,
               b_spec=pl.BlockSpec((None, kk, ns), lambda i, j, k: (k, 0, 0)))


def _mm_bwd_w(a, dy, *, name, tm, s, ns, shard_in=False):
    n, kk = a.shape
    if shard_in:
        b_spec = pl.BlockSpec((None, tm, ns), lambda i, j, k: (j, k, 0))
    else:
        b_spec = pl.BlockSpec((tm, ns), lambda i, j, k: (k, j))
    return _mm(a, dy, name=name, grid=(1, s, n // tm), dims=TN, o_shape=(s, kk, ns), o_dtype=BF16,
               o_spec=pl.BlockSpec((None, kk, ns), lambda i, j, k: (j, 0, 0)),
               a_spec=pl.BlockSpec((tm, kk), lambda i, j, k: (k, 0)), b_spec=b_spec)


def _mm_down_fwd(hm, wd, *, name, tm):
    s, n, ks = hm.shape
    d = wd.shape[2]
    return _mm(hm, wd, name=name, grid=(n // tm, 1, s), dims=NN, o_shape=(n, d), o_dtype=F32,
               o_spec=pl.BlockSpec((tm, d), lambda i, j, k: (i, 0)),
               a_spec=pl.BlockSpec((None, tm, ks), lambda i, j, k: (k, i, 0)),
               b_spec=pl.BlockSpec((None, ks, d), lambda i, j, k: (k, 0, 0)))


def _mm_down_bwd_in(df, wd, *, name, tm):
    n, d = df.shape
    s, ks, _ = wd.shape
    return _mm(df, wd, name=name, grid=(n // tm, s, 1), dims=NT, o_shape=(s, n, ks), o_dtype=BF16,
               o_spec=pl.BlockSpec((None, tm, ks), lambda i, j, k: (j, i, 0)),
               a_spec=pl.BlockSpec((tm, d), lambda i, j, k: (i, 0)),
               b_spec=pl.BlockSpec((None, ks, d), lambda i, j, k: (j, 0, 0)))


def _mm_down_bwd_w(hm, df, *, name, tm):
    s, n, ks = hm.shape
    d = df.shape[1]
    return _mm(hm, df, name=name, grid=(1, s, n // tm), dims=TN, o_shape=(s, ks, d), o_dtype=BF16,
               o_spec=pl.BlockSpec((None, ks, d), lambda i, j, k: (j, 0, 0)),
               a_spec=pl.BlockSpec((None, tm, ks), lambda i, j, k: (j, k, 0)),
               b_spec=pl.BlockSpec((tm, d), lambda i, j, k: (k, 0)))


def _rowk(fn, *, name, n, tm, nctx, rows=(), pats=(), consts=(), out_rows=(), out_seg=(), out_acc=()):
    nb, ncb = n // tm, nctx // tm
    nr, npat, ncst = len(rows), len(pats), len(consts)
    no, nseg, nacc = len(out_rows), len(out_seg), len(out_acc)
    in_specs, blocks = [], 0
    for arr, w, cb, off in rows:
        in_specs.append(pl.BlockSpec((tm, w), lambda i, cb=cb, off=off: (jnp.maximum(i - off, 0), cb)))
        blocks += _nbytes((tm, w), arr.dtype)
    for p in pats:
        in_specs.append(pl.BlockSpec((None, SUBLANES, p.shape[2]), lambda i: (jnp.where(i >= ncb, 1, 0), 0, 0)))
    for cst in consts:
        in_specs.append(pl.BlockSpec(cst.shape, lambda i: (0, 0)))
    out_shape, out_specs = [], []
    for wt, dt, w, cb in out_rows:
        out_shape.append(jax.ShapeDtypeStruct((n, wt), dt))
        out_specs.append(pl.BlockSpec((tm, w), lambda i, cb=cb: (i, cb)))
        blocks += _nbytes((tm, w), dt)
    for w in out_seg:
        out_shape.append(jax.ShapeDtypeStruct((SUBLANES, w), F32))
        out_specs.append(pl.BlockSpec((SUBLANES, w), lambda i: (0, 0)))
    for w in out_acc:
        out_shape.append(jax.ShapeDtypeStruct((1, w), F32))
        out_specs.append(pl.BlockSpec((1, w), lambda i: (0, 0)))
    scratch = [pltpu.VMEM((2, SUBLANES, w), F32) for w in out_seg] + [pltpu.VMEM((SUBLANES, w), F32) for w in out_acc]

    def body(*refs):
        r_in = refs[:nr]
        p_in = refs[nr:nr + npat]
        c_in = refs[nr + npat:nr + npat + ncst]
        base = nr + npat + ncst
        o_rows = refs[base:base + no]
        o_seg = refs[base + no:base + no + nseg]
        o_acc = refs[base + no + nseg:base + no + nseg + nacc]
        s_seg = refs[base + no + nseg + nacc:base + no + nseg + nacc + nseg]
        s_acc = refs[base + no + nseg + nacc + nseg:]
        i = pl.program_id(0)
        rv = [r[...].astype(F32).reshape(tm // SUBLANES, SUBLANES, r.shape[1]) for r in r_in]
        pv = [p[...] for p in p_in]
        cv = [c[...] for c in c_in]
        is_lat = (i >= ncb).astype(F32)
        ro, so, ao = fn(rv, pv, cv, is_lat)
        for ref, val in zip(o_rows, ro):
            ref[...] = val.reshape(tm, ref.shape[1]).astype(ref.dtype)
        if nseg or nacc:
            @pl.when(i == 0)
            def _():
                for s in list(s_seg) + list(s_acc):
                    s[...] = jnp.zeros(s.shape, F32)

            seg = jnp.where(i >= ncb, 1, 0)
            for s, val in zip(s_seg, so):
                s[seg] = s[seg] + val
            for s, val in zip(s_acc, ao):
                s[...] = s[...] + val

            @pl.when(i == nb - 1)
            def _():
                for o, s in zip(o_seg, s_seg):
                    lat, ctx = s[1], s[0]
                    row = lax.broadcasted_iota(jnp.int32, lat.shape, 0)
                    lat = lat + pltpu.roll(lat, 4, 0)
                    ctx = jnp.broadcast_to(jnp.sum(ctx, axis=0, keepdims=True), lat.shape)
                    o[...] = jnp.where(row < 4, lat, jnp.where(row == 4, ctx, 0.0))
                for o, s in zip(o_acc, s_acc):
                    o[...] = jnp.sum(s[...], axis=0, keepdims=True)

    vmem = 2 * blocks + 8 * tm * 1024 * 4 + (8 << 20)
    res = pl.pallas_call(
        body, name=name, out_shape=out_shape, grid=(nb,), in_specs=in_specs, out_specs=out_specs,
        scratch_shapes=scratch, compiler_params=_params(("arbitrary",), vmem),
    )(*[r[0] for r in rows], *pats, *consts)
    return list(res)


def _rms(z):
    return lax.rsqrt(jnp.mean(z * z, axis=-1, keepdims=True) + NORM_EPS)


def _norm_mod_fwd(z, w, sh, sc, *, name, dims, res=None):
    n, d = z.shape

    def fn(rv, pv, cv, is_lat):
        zz = rv[0]
        if res is not None:
            zz = zz + pv[2][None] * rv[1]
        h = (zz * _rms(zz) * cv[0]) * (1.0 + pv[1][None]) + pv[0][None]
        return ([zz, h] if res is not None else [h]), [], []

    rows = [(z, d, 0, 0)] + ([(res[0], d, 0, 0)] if res is not None else [])
    pats = [sh, sc] + ([res[1]] if res is not None else [])
    outs = ([(d, F32, d, 0)] if res is not None else []) + [(d, BF16, d, 0)]
    out = _rowk(fn, name=name, n=n, tm=dims["tm_row"], nctx=dims["nctx"], rows=rows, pats=pats, consts=[w],
                out_rows=outs)
    return (out[0], out[1]) if res is not None else (None, out[0])


def _norm_core_bwd(zin, dh, w, sc):
    r = _rms(zin)
    xh = zin * r
    dsh = jnp.sum(dh, axis=0)
    dsc = jnp.sum(dh * (xh * w), axis=0)
    dyv = dh * (1.0 + sc[None])
    dw = jnp.sum(dyv * xh, axis=0)
    dxh = dyv * w
    dx = r * (dxh - xh * jnp.mean(dxh * xh, axis=-1, keepdims=True))
    return dx, dsh, dsc, dw


def _norm_mod_bwd(dh, zin, dz_up, w, sc, *, name, dims, res=None):
    n, d = zin.shape

    def fn(rv, pv, cv, is_lat):
        dx, dsh, dsc, dw = _norm_core_bwd(rv[1], rv[0], cv[0], pv[0])
        dz = rv[2] + dx
        if res is None:
            return [dz], [dsh, dsc], [dw]
        return [dz, dz * pv[1][None]], [dsh, dsc, jnp.sum(dz * rv[3], axis=0)], [dw]

    rows = [(dh, d, 0, 0), (zin, d, 0, 0), (dz_up, d, 0, 0)] + ([(res[0], d, 0, 0)] if res is not None else [])
    pats = [sc] + ([res[1]] if res is not None else [])
    outs = [(d, F32, d, 0)] + ([(d, BF16, d, 0)] if res is not None else [])
    return _rowk(fn, name=name, n=n, tm=dims["tm_row"], nctx=dims["nctx"], rows=rows, pats=pats, consts=[w],
                 out_rows=outs, out_seg=[d] * (3 if res is not None else 2), out_acc=[d])


def _loss_bwd(z1, f, gate, tgt, w, *, name, dims):
    n, d = z1.shape

    def fn(rv, pv, cv, is_lat):
        z2 = rv[0] + pv[0][None] * rv[1]
        r = _rms(z2)
        xh = z2 * r
        err = (xh * cv[0] - rv[2]) * is_lat
        dout = err * (1.0 / d)
        dxh = dout * cv[0]
        dz = r * (dxh - xh * jnp.mean(dxh * xh, axis=-1, keepdims=True))
        return ([dz, dz * pv[0][None]], [jnp.sum(dz * rv[1], axis=0)],
                [jnp.sum(0.5 * err * err * (1.0 / d), axis=0), jnp.sum(dout * xh, axis=0)])

    tm = dims["tm_row"]
    rows = [(z1, d, 0, 0), (f, d, 0, 0), (tgt, d, 0, dims["nctx"] // tm)]
    return _rowk(fn, name=name, n=n, tm=tm, nctx=dims["nctx"], rows=rows, pats=[gate], consts=[w],
                 out_rows=[(d, F32, d, 0), (d, BF16, d, 0)], out_seg=[d], out_acc=[d, d])


def _gelu(y):
    return jax.nn.gelu(y, approximate=True)


def _conv_masks(tb, i):
    tok = lax.broadcasted_iota(jnp.int32, (tb, 1), 0) >> 2
    last = jnp.where(i == 0, tb // LOCAL_B - 1, GRID_W - 1)
    wpos = tok & last
    return wpos == 0, wpos == last


def _convffn_fwd(u, cw, cb, *, name, dims):
    _, sh, n, ns = u.shape
    tb = dims["nctx"]

    def body(u_ref, cw_ref, cb_ref, o_ref):
        no_left, no_right = _conv_masks(tb, pl.program_id(1))

        def conv(s):
            uu = u_ref[s]
            ul = jnp.where(no_left, 0.0, pltpu.roll(uu, LOCAL_B, 0))
            ur = jnp.where(no_right, 0.0, pltpu.roll(uu, tb - LOCAL_B, 0))
            return (cb_ref[s] + ul * cw_ref[s, pl.ds(0, 1), :] + uu * cw_ref[s, pl.ds(1, 1), :]
                    + ur * cw_ref[s, pl.ds(2, 1), :])

        a, g = conv(0), conv(1)
        o_ref[...] = (a * _sigmoid(a) * g).astype(BF16)

    vmem = 2 * (2 * tb * ns * 4 + tb * ns * 2) + 10 * tb * ns * 4 + (8 << 20)
    return pl.pallas_call(
        body, name=name, out_shape=jax.ShapeDtypeStruct((sh, n, ns), BF16), grid=(sh, n // tb),
        in_specs=[pl.BlockSpec((2, None, tb, ns), lambda j, i: (0, j, i, 0)),
                  pl.BlockSpec((2, None, 3, ns), lambda j, i: (0, j, 0, 0)),
                  pl.BlockSpec((2, None, 1, ns), lambda j, i: (0, j, 0, 0))],
        out_specs=pl.BlockSpec((None, tb, ns), lambda j, i: (j, i, 0)),
        compiler_params=_params(("parallel", "arbitrary"), vmem),
    )(u, cw, cb)


def _convffn_bwd(u, dhm, cw, cb, *, name, dims):
    _, sh, n, ns = u.shape
    tb = dims["nctx"]

    def body(u_ref, dh_ref, cw_ref, cb_ref, du_ref, dcw_ref, dcb_ref):
        i = pl.program_id(1)
        no_left, no_right = _conv_masks(tb, i)

        @pl.when(i == 0)
        def _():
            dcw_ref[...] = jnp.zeros(dcw_ref.shape, F32)
            dcb_ref[...] = jnp.zeros(dcb_ref.shape, F32)

        def taps(s):
            uu = u_ref[s]
            ul = jnp.where(no_left, 0.0, pltpu.roll(uu, LOCAL_B, 0))
            ur = jnp.where(no_right, 0.0, pltpu.roll(uu, tb - LOCAL_B, 0))
            val = (cb_ref[s] + ul * cw_ref[s, pl.ds(0, 1), :] + uu * cw_ref[s, pl.ds(1, 1), :]
                   + ur * cw_ref[s, pl.ds(2, 1), :])
            return val, ul, uu, ur

        a, al, ac, ar = taps(0)
        g, gl, gc, gr = taps(1)
        dh = dh_ref[...].astype(F32)
        sa = _sigmoid(a)
        dg = dh * (a * sa)
        da = dh * g * (sa * (1.0 + a * (1.0 - sa)))
        for s, dc, (tl, tc, tr) in ((0, da, (al, ac, ar)), (1, dg, (gl, gc, gr))):
            dcb_ref[s] += jnp.sum(dc, axis=0, keepdims=True)
            dcw_ref[s, pl.ds(0, 1), :] += jnp.sum(dc * tl, axis=0, keepdims=True)
            dcw_ref[s, pl.ds(1, 1), :] += jnp.sum(dc * tc, axis=0, keepdims=True)
            dcw_ref[s, pl.ds(2, 1), :] += jnp.sum(dc * tr, axis=0, keepdims=True)
            du = (dc * cw_ref[s, pl.ds(1, 1), :]
                  + pltpu.roll(jnp.where(no_left, 0.0, dc) * cw_ref[s, pl.ds(0, 1), :], tb - LOCAL_B, 0)
                  + pltpu.roll(jnp.where(no_right, 0.0, dc) * cw_ref[s, pl.ds(2, 1), :], LOCAL_B, 0))
            du_ref[s] = du.astype(BF16)

    vmem = 2 * (2 * tb * ns * 4 + tb * ns * 2 + 2 * tb * ns * 2) + 16 * tb * ns * 4 + (8 << 20)
    return pl.pallas_call(
        body, name=name,
        out_shape=[jax.ShapeDtypeStruct((2, sh, n, ns), BF16), jax.ShapeDtypeStruct((2, sh, 3, ns), F32),
                   jax.ShapeDtypeStruct((2, sh, 1, ns), F32)],
        grid=(sh, n // tb),
        in_specs=[pl.BlockSpec((2, None, tb, ns), lambda j, i: (0, j, i, 0)),
                  pl.BlockSpec((None, tb, ns), lambda j, i: (j, i, 0)),
                  pl.BlockSpec((2, None, 3, ns), lambda j, i: (0, j, 0, 0)),
                  pl.BlockSpec((2, None, 1, ns), lambda j, i: (0, j, 0, 0))],
        out_specs=[pl.BlockSpec((2, None, tb, ns), lambda j, i: (0, j, i, 0)),
                   pl.BlockSpec((2, None, 3, ns), lambda j, i: (0, j, 0, 0)),
                   pl.BlockSpec((2, None, 1, ns), lambda j, i: (0, j, 0, 0))],
        compiler_params=_params(("parallel", "arbitrary"), vmem),
    )(u, dhm, cw, cb)


def _s5_disc(lr, li, ls, brt, bit):
    lr = jnp.minimum(lr, S5_LAM_RE_MAX)
    dt = jnp.exp(ls)
    mag = jnp.exp(lr * dt)
    ar = mag * jnp.cos(li * dt)
    ai = mag * jnp.sin(li * dt)
    den = lr * lr + li * li
    nr = ar - 1.0
    cr = (nr * lr + ai * li) / den
    ci = (ai * lr - nr * li) / den
    return ar, ai, cr * brt - ci * bit, cr * bit + ci * brt


def _s5_disc_fwd(lr, li, ls, brt, bit):
    def body(lr_ref, li_ref, ls_ref, br_ref, bi_ref, ar_ref, ai_ref, bbr_ref, bbi_ref):
        ar, ai, bbr, bbi = _s5_disc(lr_ref[...], li_ref[...], ls_ref[...], br_ref[...], bi_ref[...])
        ar_ref[...] = ar
        ai_ref[...] = ai
        bbr_ref[...] = bbr
        bbi_ref[...] = bbi

    sd = jax.ShapeDtypeStruct
    return pl.pallas_call(body, name="s5_disc_fwd",
                          out_shape=[sd(lr.shape, F32), sd(lr.shape, F32), sd(brt.shape, F32), sd(brt.shape, F32)],
                          compiler_params=_params(None, 32 << 20))(lr, li, ls, brt, bit)


def _s5_disc_bwd(lr, li, ls, brt, bit, dar, dai, dbbr, dbbi):
    def body(lr_ref, li_ref, ls_ref, br_ref, bi_ref, dar_ref, dai_ref, dbbr_ref, dbbi_ref,
             dlr_ref, dli_ref, dls_ref, dbr_ref, dbi_ref):
        _, vjp = jax.vjp(_s5_disc, lr_ref[...], li_ref[...], ls_ref[...], br_ref[...], bi_ref[...])
        dlr, dli, dls, dbr, dbi = vjp((dar_ref[...], dai_ref[...], dbbr_ref[...], dbbi_ref[...]))
        dlr_ref[...] = dlr
        dli_ref[...] = dli
        dls_ref[...] = dls
        dbr_ref[...] = dbr
        dbi_ref[...] = dbi

    sd = jax.ShapeDtypeStruct
    return pl.pallas_call(body, name="s5_disc_bwd",
                          out_shape=[sd(lr.shape, F32), sd(lr.shape, F32), sd(ls.shape, F32), sd(brt.shape, F32),
                                     sd(brt.shape, F32)],
                          compiler_params=_params(None, 48 << 20))(lr, li, ls, brt, bit, dar, dai, dbbr, dbbi)


def _cmul(ar, ai, xr, xi):
    return ar * xr - ai * xi, ar * xi + ai * xr


def _scan_consts(a_r, a_i, rev):
    row = lax.broadcasted_iota(jnp.int32, a_r.shape, 0)
    second = (row < 4) if rev else (row >= 4)
    a2r, a2i = _cmul(a_r, a_i, a_r, a_i)
    a1r, a1i = jnp.where(second, a_r, 0.0), jnp.where(second, a_i, 0.0)
    apr, api = jnp.where(second, a2r, a_r), jnp.where(second, a2i, a_i)
    return second, a1r, a1i, apr, api


def _scan_tile(xr, xi, pr, pi, consts):
    second, a1r, a1i, apr, api = consts
    sr, si = pltpu.roll(xr, 4, 0), pltpu.roll(xi, 4, 0)
    t1r, t1i = _cmul(a1r, a1i, sr, si)
    t2r, t2i = _cmul(apr, api, pr, pi)
    yr, yi = xr + t1r + t2r, xi + t1i + t2i
    npr = jnp.where(second, yr, pltpu.roll(yr, 4, 0))
    npi = jnp.where(second, yi, pltpu.roll(yi, 4, 0))
    return yr, yi, npr, npi


def _s5_scan(xr_ref, xi_ref, row0, nrows, a_r_ref, a_i_ref, cr_ref, ci_ref, *, rev, conj, lane_block, extra=None):
    width = xr_ref.shape[1]
    nt = nrows // SUBLANES
    for lb in range(width // lane_block):
        lanes = pl.ds(lb * lane_block, lane_block)
        a_r = a_r_ref[:, lanes]
        a_i = a_i_ref[:, lanes]
        if conj:
            a_i = -a_i
        consts = _scan_consts(a_r, a_i, rev)

        def step(t, carry):
            pr, pi = carry[0], carry[1]
            j = (nt - 1 - t) if rev else t
            rows = pl.ds(pl.multiple_of(row0 + j * SUBLANES, SUBLANES), SUBLANES)
            yr, yi, pr, pi = _scan_tile(xr_ref[rows, lanes], xi_ref[rows, lanes], pr, pi, consts)
            xr_ref[rows, lanes] = yr
            xi_ref[rows, lanes] = yi
            if extra is None:
                return pr, pi
            return (pr, pi) + tuple(extra(j, lanes, yr, yi, carry[2:]))

        init = (cr_ref[:, lanes], ci_ref[:, lanes])
        if extra is not None:
            init = init + tuple(extra.init(lanes))
        out = lax.fori_loop(0, nt, step, init)
        cr_ref[:, lanes] = out[0]
        ci_ref[:, lanes] = out[1]
        if extra is not None:
            extra.done(lanes, out[2:])


def _s5_chunk_of(step, ncc, nc, rev):
    if not rev:
        return step
    return jnp.where(step < ncc, ncc - 1 - step, nc - 1 - (step - ncc))


def _s5_fwd_dir(u, base, a8r, a8i, bbr, bbi, ccr, cci, dsk, *, d, name, dims):
    n, dm = u.shape
    nk, swk = bbr.shape[0], bbr.shape[2]
    rr, sw = dims["s5_rows"], nk * swk
    nc, ncc = n // rr, dims["nctx"] // rr
    rev = d == 1
    cmap = lambda i: (_s5_chunk_of(i, ncc, nc, rev), 0)

    def body(*refs):
        if d == 0:
            u_ref, a8r_ref, a8i_ref, bbr_ref, bbi_ref, ccr_ref, cci_ref, dsk_ref = refs[:8]
            rest = refs[8:]
        else:
            u_ref, base_ref, a8r_ref, a8i_ref, bbr_ref, bbi_ref, ccr_ref, cci_ref = refs[:8]
            rest = refs[8:]
        y_ref, str_ref, sti_ref, sr, si, cr, ci = rest
        i = pl.program_id(0)

        @pl.when(i == 0)
        def _():
            cr[...] = jnp.zeros(cr.shape, F32)
            ci[...] = jnp.zeros(ci.shape, F32)

        str_ref[...] = cr[...]
        sti_ref[...] = ci[...]
        ub = u_ref[...].astype(BF16)
        for k in range(nk):
            uk = ub[:, k * LANES:(k + 1) * LANES]
            sr[:, k * swk:(k + 1) * swk] = _dot(uk, bbr_ref[k])
            si[:, k * swk:(k + 1) * swk] = _dot(uk, bbi_ref[k])
        _s5_scan(sr, si, 0, rr, a8r_ref, a8i_ref, cr, ci, rev=rev, conj=False, lane_block=dims["s5_lane_block"])
        for k in range(nk):
            cols = slice(k * LANES, (k + 1) * LANES)
            yk = (_dot(sr[:, k * swk:(k + 1) * swk].astype(BF16), ccr_ref[k])
                  - _dot(si[:, k * swk:(k + 1) * swk].astype(BF16), cci_ref[k]))
            if d == 0:
                y_ref[:, cols] = yk + dsk_ref[:, cols] * u_ref[:, cols]
            else:
                y_ref[:, cols] = yk + base_ref[:, cols]

    row_spec = pl.BlockSpec((rr, dm), cmap)
    full = lambda a: pl.BlockSpec(a.shape, lambda i: (0,) * a.ndim)
    ins = [u] + ([] if d == 0 else [base]) + [a8r, a8i, bbr, bbi, ccr, cci] + ([dsk] if d == 0 else [])
    in_specs = [row_spec] + ([] if d == 0 else [row_spec]) + [full(a) for a in (a8r, a8i, bbr, bbi, ccr, cci)]
    in_specs += [full(dsk)] if d == 0 else []
    st_spec = pl.BlockSpec((None, SUBLANES, sw), lambda i: (_s5_chunk_of(i, ncc, nc, rev), 0, 0))
    vmem = 2 * rr * sw * 4 + 6 * rr * dm * 4 + 8 * nk * LANES * swk * 2 + (12 << 20)
    return pl.pallas_call(
        body, name=name,
        out_shape=[jax.ShapeDtypeStruct((n, dm), F32), jax.ShapeDtypeStruct((nc, SUBLANES, sw), F32),
                   jax.ShapeDtypeStruct((nc, SUBLANES, sw), F32)],
        grid=(nc,), in_specs=in_specs, out_specs=[row_spec, st_spec, st_spec],
        scratch_shapes=[pltpu.VMEM((rr, sw), F32), pltpu.VMEM((rr, sw), F32), pltpu.VMEM((SUBLANES, sw), F32),
                        pltpu.VMEM((SUBLANES, sw), F32)],
        compiler_params=_params(("arbitrary",), vmem),
    )(*ins)


class _DaHook:
    def __init__(self, sr, si, accr, acci, rev_fwd):
        self.sr, self.si, self.accr, self.acci, self.rev_fwd = sr, si, accr, acci, rev_fwd

    def init(self, lanes):
        return self.accr[:, lanes], self.acci[:, lanes]

    def done(self, lanes, acc):
        self.accr[:, lanes] = acc[0]
        self.acci[:, lanes] = acc[1]

    def __call__(self, j, lanes, lr, li, acc):
        base = pl.multiple_of(SUBLANES + j * SUBLANES, SUBLANES)
        cur = pl.ds(base, SUBLANES)
        row = lax.broadcasted_iota(jnp.int32, lr.shape, 0)
        if self.rev_fwd:
            oth = pl.ds(pl.multiple_of(base + SUBLANES, SUBLANES), SUBLANES)
            spr = pltpu.roll(jnp.where(row >= 4, self.sr[cur, lanes], self.sr[oth, lanes]), 4, 0)
            spi = pltpu.roll(jnp.where(row >= 4, self.si[cur, lanes], self.si[oth, lanes]), 4, 0)
        else:
            oth = pl.ds(pl.multiple_of(base - SUBLANES, SUBLANES), SUBLANES)
            spr = pltpu.roll(jnp.where(row >= 4, self.sr[oth, lanes], self.sr[cur, lanes]), 4, 0)
            spi = pltpu.roll(jnp.where(row >= 4, self.si[oth, lanes], self.si[cur, lanes]), 4, 0)
        return acc[0] + spr * lr + spi * li, acc[1] + spr * li - spi * lr


def _s5_bwd_dir(u, dy, du_prev, a8r, a8i, bbr, bbi, ccr, cci, dsk, st_r, st_i, *, d, name, dims):
    n, dm = u.shape
    rr = dims["s5_rows"]
    nk, swk = bbr.shape[0], bbr.shape[2]
    sw = nk * swk
    nc, ncc = n // rr, dims["nctx"] // rr
    rev = d == 1
    chunk = lambda i: _s5_chunk_of(nc - 1 - i, ncc, nc, rev)

    def body(*refs):
        u_ref, dy_ref = refs[0], refs[1]
        pos = 2
        dup_ref = None
        if d == 1:
            dup_ref = refs[pos]
            pos += 1
        a8r_ref, a8i_ref, bbr_ref, bbi_ref, ccr_ref, cci_ref = refs[pos:pos + 6]
        pos += 6
        dsk_ref = None
        if d == 0:
            dsk_ref = refs[pos]
            pos += 1
        str_ref, sti_ref = refs[pos:pos + 2]
        pos += 2
        du_ref, dbbr_ref, dbbi_ref, dccr_ref, dcci_ref, dar_ref, dai_ref = refs[pos:pos + 7]
        pos += 7
        dd_ref = None
        if d == 0:
            dd_ref = refs[pos]
            pos += 1
        sr, si, lr, li, cr, ci, lcr, lci, accr, acci, dda = refs[pos:]
        i = pl.program_id(0)

        @pl.when(i == 0)
        def _():
            for ref in (lcr, lci, accr, acci, dda, dbbr_ref, dbbi_ref, dccr_ref, dcci_ref):
                ref[...] = jnp.zeros(ref.shape, F32)

        cr[...] = str_ref[...]
        ci[...] = sti_ref[...]
        spare = pl.ds(rr + SUBLANES, SUBLANES) if rev else pl.ds(0, SUBLANES)
        sr[spare, :] = str_ref[...]
        si[spare, :] = sti_ref[...]
        ub = u_ref[...].astype(BF16)
        dyb = dy_ref[...].astype(BF16)
        for k in range(nk):
            uk = ub[:, k * LANES:(k + 1) * LANES]
            sr[pl.ds(SUBLANES, rr), k * swk:(k + 1) * swk] = _dot(uk, bbr_ref[k])
            si[pl.ds(SUBLANES, rr), k * swk:(k + 1) * swk] = _dot(uk, bbi_ref[k])
        _s5_scan(sr, si, SUBLANES, rr, a8r_ref, a8i_ref, cr, ci, rev=rev, conj=False,
                 lane_block=dims["s5_lane_block"])
        for k in range(nk):
            dyk = dyb[:, k * LANES:(k + 1) * LANES]
            sl = slice(k * swk, (k + 1) * swk)
            lr[:, sl] = _dot(dyk, ccr_ref[k], NT)
            li[:, sl] = -_dot(dyk, cci_ref[k], NT)
            dccr_ref[k] += _dot(sr[pl.ds(SUBLANES, rr), sl].astype(BF16), dyk, TN)
            dcci_ref[k] -= _dot(si[pl.ds(SUBLANES, rr), sl].astype(BF16), dyk, TN)
        _s5_scan(lr, li, 0, rr, a8r_ref, a8i_ref, lcr, lci, rev=not rev, conj=True,
                 lane_block=dims["s5_lane_block"], extra=_DaHook(sr, si, accr, acci, rev))
        for k in range(nk):
            cols = slice(k * LANES, (k + 1) * LANES)
            sl = slice(k * swk, (k + 1) * swk)
            uk = ub[:, cols]
            lrk, lik = lr[:, sl].astype(BF16), li[:, sl].astype(BF16)
            dbbr_ref[k] += _dot(uk, lrk, TN)
            dbbi_ref[k] += _dot(uk, lik, TN)
            duk = _dot(lrk, bbr_ref[k], NT) + _dot(lik, bbi_ref[k], NT)
            if d == 0:
                du_ref[:, cols] = duk + dsk_ref[:, cols] * dy_ref[:, cols]
            else:
                du_ref[:, cols] = duk + dup_ref[:, cols]
        if d == 0:
            prod = (dy_ref[...] * u_ref[...]).reshape(rr // SUBLANES, SUBLANES, dm)
            dda[...] += jnp.sum(prod, axis=0)

        @pl.when(i == nc - 1)
        def _():
            dar_ref[...] = jnp.sum(accr[...], axis=0, keepdims=True)
            dai_ref[...] = jnp.sum(acci[...], axis=0, keepdims=True)
            if d == 0:
                dd_ref[...] = jnp.sum(dda[...], axis=0, keepdims=True)

    row_spec = pl.BlockSpec((rr, dm), lambda i: (chunk(i), 0))
    full = lambda a: pl.BlockSpec(a.shape, lambda i: (0,) * a.ndim)
    st_spec = pl.BlockSpec((None, SUBLANES, sw), lambda i: (chunk(i), 0, 0))
    ins = [u, dy] + ([du_prev] if d == 1 else []) + [a8r, a8i, bbr, bbi, ccr, cci] + ([dsk] if d == 0 else [])
    ins += [st_r, st_i]
    in_specs = [row_spec, row_spec] + ([row_spec] if d == 1 else []) + [full(a) for a in (a8r, a8i, bbr, bbi, ccr, cci)]
    in_specs += ([full(dsk)] if d == 0 else []) + [st_spec, st_spec]
    sd = jax.ShapeDtypeStruct
    out_shape = [sd((n, dm), F32), sd(bbr.shape, F32), sd(bbr.shape, F32), sd(ccr.shape, F32),
                 sd(ccr.shape, F32), sd((1, sw), F32), sd((1, sw), F32)] + ([sd((1, dm), F32)] if d == 0 else [])
    out_specs = [row_spec] + [pl.BlockSpec(s.shape, lambda i, nd=len(s.shape): (0,) * nd) for s in out_shape[1:]]
    scratch = [pltpu.VMEM((rr + 2 * SUBLANES, sw), F32), pltpu.VMEM((rr + 2 * SUBLANES, sw), F32),
               pltpu.VMEM((rr, sw), F32), pltpu.VMEM((rr, sw), F32)]
    scratch += [pltpu.VMEM((SUBLANES, sw), F32)] * 6 + [pltpu.VMEM((SUBLANES, dm), F32)]
    vmem = 4 * (rr + 16) * sw * 4 + 10 * rr * dm * 4 + 24 * nk * LANES * swk * 4 + (12 << 20)
    return pl.pallas_call(body, name=name, out_shape=out_shape, grid=(nc,), in_specs=in_specs, out_specs=out_specs,
                          scratch_shapes=scratch, compiler_params=_params(("arbitrary",), vmem))(*ins)


def _hg_chunk(q, v, fraw, l0, l1, st, *, rev):
    nh = q.shape[1] // HG_HEAD
    lb = _sigmoid(l1 - l0)
    logf = jnp.logaddexp(jnp.log(lb), jnp.log1p(-lb) + jax.nn.log_sigmoid(fraw))
    kk = (1.0 - lb) * _sigmoid(fraw * -1.0)
    r = lax.broadcasted_iota(jnp.int32, (HG_ROWS, HG_ROWS), 0)
    c = lax.broadcasted_iota(jnp.int32, (HG_ROWS, HG_ROWS), 1)
    same = (r & 3) == (c & 3)
    tri = jnp.logical_and(same, ((c >> 2) >= (r >> 2)) if rev else ((c >> 2) <= (r >> 2)))
    hi = lax.Precision.HIGHEST
    bcum = _dot(tri.astype(F32), logf, NN, hi)
    bend = _dot(same.astype(F32), logf, NN, hi)
    r8 = lax.broadcasted_iota(jnp.int32, (SUBLANES, HG_ROWS), 0)
    c8 = lax.broadcasted_iota(jnp.int32, (SUBLANES, HG_ROWS), 1)
    bend8 = _dot(((c8 & 3) == r8).astype(F32), logf, NN, hi)
    r8d = lax.broadcasted_iota(jnp.int32, bend8.shape, 0)
    decs = [jnp.exp(jnp.sum(jnp.where(r8d == b, bend8, 0.0), axis=0, keepdims=True)) for b in range(LOCAL_B)]
    qd = (q * jnp.exp(bcum)).astype(BF16)
    kd = (kk * jnp.exp(-bcum)).astype(BF16)
    ke = (kk * jnp.exp(bend - bcum)).astype(BF16)
    rowb = lax.broadcasted_iota(jnp.int32, (HG_ROWS, 1), 0) & 3
    masks = [(rowb == b).astype(F32) for b in range(LOCAL_B)]
    outs, new = [], []
    for h in range(nh):
        sl = slice(h * HG_HEAD, (h + 1) * HG_HEAD)
        vh = v[:, sl]
        att = jnp.where(tri, _dot(qd[:, sl], kd[:, sl], NT), 0.0)
        o = _dot(att.astype(BF16), vh.astype(BF16))
        for b in range(LOCAL_B):
            sb = st[h * LOCAL_B + b]
            o = o + masks[b] * _dot(qd[:, sl], sb.astype(BF16), NT)
            new.append(sb * decs[b][:, sl] + _dot((vh * masks[b]).astype(BF16), ke[:, sl], TN))
        outs.append(o)
    return jnp.concatenate(outs, axis=1), tuple(new)


def _hg_chunk_of(step, ncc, nc, rev):
    return _s5_chunk_of(step, ncc, nc, rev)


def _hg_fwd_dir(zz, lb2, *, d, name, dims):
    n = zz.shape[0]
    dm = zz.shape[1] // 5
    ns = (dm // HG_HEAD) * LOCAL_B
    nc, ncc = n // HG_ROWS, dims["nctx"] // HG_ROWS
    rev = d == 1
    ch = lambda i: _hg_chunk_of(i, ncc, nc, rev)

    def body(q_ref, v_ref, f_ref, l0_ref, l1_ref, o_ref, st_ref, st):
        @pl.when(pl.program_id(0) == 0)
        def _():
            st[...] = jnp.zeros(st.shape, F32)

        st_ref[...] = st[...]
        o, new = _hg_chunk(q_ref[...], v_ref[...], f_ref[...], l0_ref[...], l1_ref[...],
                           tuple(st[j] for j in range(ns)), rev=rev)
        o_ref[...] = o
        for j in range(ns):
            st[j] = new[j]

    blk = lambda off: pl.BlockSpec((HG_ROWS, dm), lambda i, off=off: (ch(i), off))
    lspec = lambda layer: pl.BlockSpec((None, None, 1, dm), lambda i, layer=layer: (d, layer, 0, 0))
    return pl.pallas_call(
        body, name=name,
        out_shape=[jax.ShapeDtypeStruct((n, dm), F32), jax.ShapeDtypeStruct((nc, ns, HG_HEAD, HG_HEAD), F32)],
        grid=(nc,),
        in_specs=[blk(0), blk(1), blk(2 + d), lspec(0), lspec(1)],
        out_specs=[pl.BlockSpec((HG_ROWS, dm), lambda i: (ch(i), 0)),
                   pl.BlockSpec((None, ns, HG_HEAD, HG_HEAD), lambda i: (ch(i), 0, 0, 0))],
        scratch_shapes=[pltpu.VMEM((ns, HG_HEAD, HG_HEAD), F32)],
        compiler_params=_params(("arbitrary",), 48 << 20),
    )(zz, zz, zz, lb2, lb2)


def _hg_bwd_dir(zz, lb2, do, sts, dqv_prev, *, d, name, dims):
    n = zz.shape[0]
    dm = zz.shape[1] // 5
    ns = (dm // HG_HEAD) * LOCAL_B
    nc, ncc = n // HG_ROWS, dims["nctx"] // HG_ROWS
    rev = d == 1
    ch = lambda i: _hg_chunk_of(nc - 1 - i, ncc, nc, rev)
    qv_dtype = F32 if d == 0 else BF16

    def body(*refs):
        q_ref, v_ref, f_ref, l0_ref, l1_ref, do_ref, st_ref = refs[:7]
        pos = 7
        if d == 1:
            dqp_ref, dvp_ref = refs[7:9]
            pos = 9
        dq_ref, dv_ref, df_ref, dl_ref, dst = refs[pos:]
        i = pl.program_id(0)

        @pl.when(i == 0)
        def _():
            dst[...] = jnp.zeros(dst.shape, F32)
            dl_ref[...] = jnp.zeros(dl_ref.shape, F32)

        _, vjp = jax.vjp(functools.partial(_hg_chunk, rev=rev), q_ref[...], v_ref[...], f_ref[...], l0_ref[...],
                         l1_ref[...], tuple(st_ref[j] for j in range(ns)))
        dq, dv, df, dl0, dl1, dstn = vjp((do_ref[...], tuple(dst[j] for j in range(ns))))
        for j in range(ns):
            dst[j] = dstn[j]
        if d == 1:
            dq = dq + dqp_ref[...]
            dv = dv + dvp_ref[...]
        dq_ref[...] = dq.astype(qv_dtype)
        dv_ref[...] = dv.astype(qv_dtype)
        df_ref[...] = df.astype(BF16)
        dl_ref[0] += dl0
        dl_ref[1] += dl1

    blk = lambda off: pl.BlockSpec((HG_ROWS, dm), lambda i, off=off: (ch(i), off))
    oblk = pl.BlockSpec((HG_ROWS, dm), lambda i: (ch(i), 0))
    lspec = lambda layer: pl.BlockSpec((None, None, 1, dm), lambda i, layer=layer: (d, layer, 0, 0))
    ins = [zz, zz, zz, lb2, lb2, do, sts] + (list(dqv_prev) if d == 1 else [])
    in_specs = [blk(0), blk(1), blk(2 + d), lspec(0), lspec(1), oblk,
                pl.BlockSpec((None, ns, HG_HEAD, HG_HEAD), lambda i: (ch(i), 0, 0, 0))]
    in_specs += [oblk, oblk] if d == 1 else []
    sd = jax.ShapeDtypeStruct
    return pl.pallas_call(
        body, name=name,
        out_shape=[sd((n, dm), qv_dtype), sd((n, dm), qv_dtype), sd((n, dm), BF16), sd((2, 1, dm), F32)],
        grid=(nc,), in_specs=in_specs,
        out_specs=[oblk, oblk, oblk, pl.BlockSpec((2, 1, dm), lambda i: (0, 0, 0))],
        scratch_shapes=[pltpu.VMEM((ns, HG_HEAD, HG_HEAD), F32)],
        compiler_params=_params(("arbitrary",), 56 << 20),
    )(*ins)


def _hg_readout(o, g, w):
    outs = []
    for h in range(o.shape[-1] // HG_HEAD):
        sl = slice(h * HG_HEAD, (h + 1) * HG_HEAD)
        oh = o[..., sl]
        outs.append(oh * _rms(oh) * w * _sigmoid(g[..., sl]))
    return jnp.concatenate(outs, axis=-1)


def _silu(x):
    return x * _sigmoid(x)


def _mod_fwd(craw, w, b):
    def body(c_ref, w_ref, b_ref, o_ref):
        s = _silu(c_ref[...]).astype(BF16)
        for layer in range(w.shape[0]):
            o_ref[layer] = _dot(s, w_ref[layer].astype(BF16)) + b_ref[layer]

    return pl.pallas_call(body, name="mod_fwd",
                          out_shape=jax.ShapeDtypeStruct((w.shape[0], craw.shape[0], w.shape[2]), F32),
                          compiler_params=_params(None, 40 << 20))(craw, w, b)


def _mod_bwd(craw, w, dlat_sh, dctx_sh, dlat_full, dctx_full):
    nl, dm, ns = w.shape
    nb = dlat_sh.shape[1]

    def body(c_ref, w_ref, dl_ref, dc_ref, dlf_ref, dcf_ref, dw_ref, db_ref, dcc_ref):
        craw_v = c_ref[...]
        s = _silu(craw_v)
        s_lat = s[:nb].astype(BF16)
        s_ctx = s[nb:].astype(BF16)
        row = lax.broadcasted_iota(jnp.int32, (SUBLANES, ns), 0)
        dsc = jnp.zeros((SUBLANES, dm), F32)
        for layer in range(nl):
            tot = dc_ref[0, pl.ds(layer, 1), :]
            totf = dcf_ref[0, pl.ds(layer, 1), :]
            for i in range(1, NDEV):
                tot = tot + dc_ref[i, pl.ds(layer, 1), :]
                totf = totf + dcf_ref[i, pl.ds(layer, 1), :]
            dc8 = jnp.where(row == 0, jnp.broadcast_to(tot, (SUBLANES, ns)), 0.0).astype(BF16)
            dw_ref[layer] = _dot(s_lat, dl_ref[layer].astype(BF16), TN) + _dot(s_ctx, dc8, TN)
            db_ref[layer] = jnp.sum(dlf_ref[layer], axis=0, keepdims=True) + totf
            dsc = dsc + _dot(dc8, w_ref[layer].astype(BF16), NT)
        cc = craw_v[nb:]
        sg = _sigmoid(cc)
        dcc_ref[...] = dsc * (sg * (1.0 + cc * (1.0 - sg)))

    sd = jax.ShapeDtypeStruct
    return pl.pallas_call(body, name="mod_bwd",
                          out_shape=[sd((nl, dm, ns), F32), sd((nl, 1, dlat_full.shape[2]), F32), sd((SUBLANES, dm), F32)],
                          compiler_params=_params(None, 48 << 20))(craw, w, dlat_sh, dctx_sh, dlat_full, dctx_full)


def _adam_rows(r):
    best = None
    for t in range(2 * SUBLANES, min(r, 128) + 1, 2 * SUBLANES):
        if r % t == 0:
            best = t
    return best if best is not None else r


def _adamw(parts, w, m, v, *, name):
    npart, r, c = parts.shape
    tr = _adam_rows(r)

    def body(p_ref, w_ref, m_ref, v_ref, g_ref, d_ref, nm_ref, nv_ref):
        g = p_ref[0].astype(F32)
        for i in range(1, npart):
            g = g + p_ref[i].astype(F32)
        nm = ADAM_B1 * m_ref[...] + (1.0 - ADAM_B1) * g
        nv = ADAM_B2 * v_ref[...] + (1.0 - ADAM_B2) * (g * g)
        m_hat = nm / (1.0 - ADAM_B1 ** ADAM_STEP)
        v_hat = nv / (1.0 - ADAM_B2 ** ADAM_STEP)
        g_ref[...] = g
        d_ref[...] = -ADAM_LR * (m_hat / (jnp.sqrt(v_hat) + ADAM_EPS) + ADAM_WD * w_ref[...])
        nm_ref[...] = nm
        nv_ref[...] = nv

    spec = pl.BlockSpec((tr, c), lambda i: (i, 0))
    vmem = 2 * (npart + 7) * tr * c * 4 + (8 << 20)
    return pl.pallas_call(
        body, name=name, out_shape=[jax.ShapeDtypeStruct((r, c), F32)] * 4, grid=(r // tr,),
        in_specs=[pl.BlockSpec((npart, tr, c), lambda i: (0, i, 0)), spec, spec, spec], out_specs=[spec] * 4,
        compiler_params=_params(("parallel",), vmem),
    )(parts, w, m, v)


def _pack(arrs, width):
    rows, sizes = [], []
    for a in arrs:
        f = a.reshape(-1).astype(F32)
        nr = -(-f.shape[0] // width)
        rows.append(jnp.pad(f, (0, nr * width - f.shape[0])).reshape(nr, width))
        sizes.append(nr)
    tot = sum(sizes)
    pad = -tot % (2 * SUBLANES)
    if pad:
        rows.append(jnp.zeros((pad, width), F32))
    return jnp.concatenate(rows, axis=0), sizes


def _unpack(buf, sizes, shapes):
    out, r0 = [], 0
    for nr, shp in zip(sizes, shapes):
        cnt = 1
        for s in shp:
            cnt *= s
        out.append(buf[r0:r0 + nr].reshape(-1)[:cnt].reshape(shp))
        r0 += nr
    return out


def _to_tm(a):
    return jnp.transpose(a, (1, 0, 2)).reshape(a.shape[1] * a.shape[0], a.shape[2])


def _pattern(mod_lat, mod_ctx, m, dm):
    lat = mod_lat[:, m * dm:(m + 1) * dm]
    ctx = jnp.broadcast_to(mod_ctx[None, m * dm:(m + 1) * dm], (SUBLANES, dm))
    return jnp.stack([ctx, jnp.concatenate([lat, lat], axis=0)])


def _blockdiag_b(bt, nk):
    g, h, p = bt.shape
    t = bt.reshape(nk, 8, h, p)
    return jnp.einsum("kghp,gj->kghjp", t, jnp.eye(8, dtype=bt.dtype)).reshape(nk, 8 * h, 8 * p)


def _blockdiag_c(ct, nk):
    g, h, p = ct.shape
    t = ct.reshape(nk, 8, h, p)
    return jnp.einsum("kghp,gj->kgpjh", t, jnp.eye(8, dtype=ct.dtype)).reshape(nk, 8 * p, 8 * h)


def _diag_b(dbb, h, p):
    nk = dbb.shape[0]
    return jnp.einsum("kghgp->kghp", dbb.reshape(nk, 8, h, 8, p)).reshape(nk * 8, h, p)


def _diag_c(dcc, h, p):
    nk = dcc.shape[0]
    return jnp.einsum("kgpgh->kghp", dcc.reshape(nk, 8, p, 8, h)).reshape(nk * 8, h, p)


def kernel(x, c, ctx, c_ctx, w_mod, b_mod, norm1_w, norm2_w, final_norm_w, s5_w_in, s5_lam_re, s5_lam_im, s5_log_step, s5_b_re, s5_b_im, s5_c_re, s5_c_im, s5_d, s5_w_glu, s5_w_out, hg_w_in, hg_lower_bounds, hg_gnorm_w, hg_w_out, ffn_w_up, ffn_conv_w, ffn_conv_b, ffn_w_down, loss_target, m_c_ctx, m_w_mod, m_b_mod, m_norm1_w, m_norm2_w, m_final_norm_w, m_s5_w_in, m_s5_lam_re, m_s5_lam_im, m_s5_log_step, m_s5_b_re, m_s5_b_im, m_s5_c_re, m_s5_c_im, m_s5_d, m_s5_w_glu, m_s5_w_out, m_hg_w_in, m_hg_lower_bounds, m_hg_gnorm_w, m_hg_w_out, m_ffn_w_up, m_ffn_conv_w, m_ffn_conv_b, m_ffn_w_down, v_c_ctx, v_w_mod, v_b_mod, v_norm1_w, v_norm2_w, v_final_norm_w, v_s5_w_in, v_s5_lam_re, v_s5_lam_im, v_s5_log_step, v_s5_b_re, v_s5_b_im, v_s5_c_re, v_s5_c_im, v_s5_d, v_s5_w_glu, v_s5_w_out, v_hg_w_in, v_hg_lower_bounds, v_hg_gnorm_w, v_hg_w_out, v_ffn_w_up, v_ffn_conv_w, v_ffn_conv_b, v_ffn_w_down):
    given = dict(locals())
    bsz, lx, dm = x.shape
    lc = ctx.shape[1]
    assert bsz == LOCAL_B and w_mod.shape[0] == 2 and dm % LANES == 0
    n, nctx = (lc + lx) * bsz, lc * bsz
    ngrp, nstate, hgrp = dm // S5_GROUP, S5_STATE, S5_GROUP
    nk = dm // LANES
    dims = dict(nctx=nctx, tm=min(512, nctx), tm_row=min(256, nctx), s5_rows=min(256, nctx),
                s5_lane_block=min(512, 8 * nstate))
    tm = dims["tm"]
    assert nctx % HG_ROWS == 0 and (lx * bsz) % nctx == 0 and lc % GRID_W == 0 and lc & (lc - 1) == 0
    me = 4 * lax.axis_index("x") + 2 * lax.axis_index("y") + lax.axis_index("c")

    wnames = ["s5_w_in", "s5_w_glu", "s5_w_out", "hg_w_in", "hg_w_out", "ffn_w_up", "ffn_w_down"]
    gath = _exchange([given[k].astype(BF16) for k in wnames] + [c, hg_lower_bounds, ffn_conv_w], a2a=False,
                     name="gather_weights")
    gw = dict(zip(wnames, gath[:7]))
    c_all, lb_all, cw_all = gath[7:]
    w_s5in = gw["s5_w_in"].reshape(1, dm, dm)
    w_glu = gw["s5_w_glu"].reshape(1, dm, dm)
    w_s5out = gw["s5_w_out"].reshape(1, dm, dm)
    w_hgin = gw["hg_w_in"][:, 0]
    w_hgout = gw["hg_w_out"].reshape(1, dm, dm)
    ns_up = ffn_w_up.shape[2]
    w_up = [gw["ffn_w_up"][:, layer] for layer in range(2)]
    w_dn = [gw["ffn_w_down"][:, layer].reshape(NDEV // 2, -1, dm) for layer in range(2)]
    cw = [cw_all[:, layer].reshape(2, NDEV // 2, 3, ns_up) for layer in range(2)]
    cb = [ffn_conv_b[layer].reshape(2, NDEV // 2, 1, ns_up) for layer in range(2)]
    lb2 = jnp.transpose(lb_all, (1, 2, 0, 3)).reshape(2, 2, 1, dm)

    nsm = w_mod.shape[2]
    craw = jnp.concatenate([c_all.reshape(NDEV * bsz, dm), c_ctx[None], jnp.zeros((SUBLANES - 1, dm), F32)], axis=0)
    b_sh = lax.dynamic_slice(b_mod, (0, me * nsm), (2, nsm)).reshape(2, 1, nsm)
    mod_sh = _mod_fwd(craw, w_mod, b_sh)
    (mod_g,) = _exchange([mod_sh], a2a=False, name="gather_mod")
    mod_full = jnp.transpose(mod_g, (1, 2, 0, 3)).reshape(2, craw.shape[0], NDEV * nsm)
    pat = []
    for layer in range(2):
        mlat = lax.dynamic_slice(mod_full[layer], (me * bsz, 0), (bsz, N_MOD * dm))
        mctx = mod_full[layer, NDEV * bsz]
        pat.append([_pattern(mlat, mctx, m, dm) for m in range(N_MOD)])

    lr4 = s5_lam_re[0].reshape(2, ngrp, 1, nstate)
    li4 = s5_lam_im[0].reshape(2, ngrp, 1, nstate)
    ls4 = s5_log_step[0].reshape(2, ngrp, 1, 1)
    brt = jnp.transpose(s5_b_re[0], (0, 1, 3, 2))
    bit = jnp.transpose(s5_b_im[0], (0, 1, 3, 2))
    abar_r, abar_i, bbar_r, bbar_i = _s5_disc_fwd(lr4, li4, ls4, brt, bit)
    sw = ngrp * nstate
    a8r = [jnp.broadcast_to(abar_r[d].reshape(1, sw), (SUBLANES, sw)) for d in range(2)]
    a8i = [jnp.broadcast_to(abar_i[d].reshape(1, sw), (SUBLANES, sw)) for d in range(2)]
    bbr = [_blockdiag_b(bbar_r[d], nk).astype(BF16) for d in range(2)]
    bbi = [_blockdiag_b(bbar_i[d], nk).astype(BF16) for d in range(2)]
    ccr = [_blockdiag_c(s5_c_re[0, d], nk).astype(BF16) for d in range(2)]
    cci = [_blockdiag_c(s5_c_im[0, d], nk).astype(BF16) for d in range(2)]
    dsk = s5_d.reshape(1, dm)

    z0 = jnp.concatenate([_to_tm(ctx), _to_tm(x)], axis=0)
    tgt = _to_tm(loss_target)
    n1w = [norm1_w[layer].reshape(1, dm) for layer in range(2)]
    n2w = [norm2_w[layer].reshape(1, dm) for layer in range(2)]

    def ffn_fwd(layer, h2):
        u = _mm_fwd(h2, w_up[layer], name=f"ffn_up{layer}", tm=tm, shard_out=True)
        u4 = u.reshape(2, NDEV // 2, n, ns_up)
        hm = _convffn_fwd(u4, cw[layer], cb[layer], name=f"convffn_fwd{layer}", dims=dims)
        f = _mm_down_fwd(hm, w_dn[layer], name=f"ffn_down{layer}", tm=tm)
        return u4, hm, f

    _, h0 = _norm_mod_fwd(z0, n1w[0], pat[0][0], pat[0][1], name="norm1_l0", dims=dims)
    u_s5 = _mm_fwd(h0, w_s5in, name="s5_in", tm=tm)
    y_a, st0r, st0i = _s5_fwd_dir(u_s5, None, a8r[0], a8i[0], bbr[0], bbi[0], ccr[0], cci[0], dsk, d=0,
                                  name="s5_fwd_d0", dims=dims)
    y_s5, st1r, st1i = _s5_fwd_dir(u_s5, y_a, a8r[1], a8i[1], bbr[1], bbi[1], ccr[1], cci[1], dsk, d=1,
                                   name="s5_fwd_d1", dims=dims)
    (zg,) = _rowk(lambda rv, pv, cv, il: ([_gelu(rv[0])], [], []), name="s5_gelu", n=n, tm=dims["tm_row"], nctx=nctx,
                  rows=[(y_s5, dm, 0, 0)], out_rows=[(dm, BF16, dm, 0)])
    t_glu = _mm_fwd(zg, w_glu, name="s5_glu", tm=tm)
    (z2g,) = _rowk(lambda rv, pv, cv, il: ([rv[0] * _sigmoid(rv[1])], [], []), name="s5_gate", n=n,
                   tm=dims["tm_row"], nctx=nctx, rows=[(zg, dm, 0, 0), (t_glu, dm, 0, 0)],
                   out_rows=[(dm, BF16, dm, 0)])
    ymix0 = _mm_fwd(z2g, w_s5out, name="s5_out", tm=tm)
    z1_l0, h2_l0 = _norm_mod_fwd(z0, n2w[0], pat[0][3], pat[0][4], name="norm2_l0", dims=dims,
                                 res=(ymix0, pat[0][2]))
    u4_l0, hm_l0, f_l0 = ffn_fwd(0, h2_l0)

    z2_l0, h1 = _norm_mod_fwd(z1_l0, n1w[1], pat[1][0], pat[1][1], name="norm1_l1", dims=dims,
                              res=(f_l0, pat[0][5]))
    zz = _mm_fwd(h1, w_hgin, name="hg_in", tm=tm)
    o_f, sts_f = _hg_fwd_dir(zz, lb2, d=0, name="hg_fwd_d0", dims=dims)
    o_b, sts_b = _hg_fwd_dir(zz, lb2, d=1, name="hg_fwd_d1", dims=dims)
    gnw = hg_gnorm_w.reshape(1, HG_HEAD)
    (og,) = _rowk(lambda rv, pv, cv, il: ([_hg_readout(rv[0] + rv[1], rv[2], cv[0])], [], []), name="hg_readout",
                  n=n, tm=dims["tm_row"], nctx=nctx, rows=[(o_f, dm, 0, 0), (o_b, dm, 0, 0), (zz, dm, 4, 0)],
                  consts=[gnw], out_rows=[(dm, BF16, dm, 0)])
    ymix1 = _mm_fwd(og, w_hgout, name="hg_out", tm=tm)
    z1_l1, h2_l1 = _norm_mod_fwd(z2_l0, n2w[1], pat[1][3], pat[1][4], name="norm2_l1", dims=dims,
                                 res=(ymix1, pat[1][2]))
    u4_l1, hm_l1, f_l1 = ffn_fwd(1, h2_l1)

    dz, df, dgate2_l1, loss_part, dfinal_w = _loss_bwd(z1_l1, f_l1, pat[1][5], tgt, final_norm_w.reshape(1, dm),
                                                        name="loss_bwd", dims=dims)

    def ffn_bwd(layer, df_, u4, hm, h2):
        dhm = _mm_down_bwd_in(df_, w_dn[layer], name=f"ffn_down_bwd_in{layer}", tm=tm)
        dwd = _mm_down_bwd_w(hm, df_, name=f"ffn_down_bwd_w{layer}", tm=tm)
        du4, dcw, dcb = _convffn_bwd(u4, dhm, cw[layer], cb[layer], name=f"convffn_bwd{layer}", dims=dims)
        du = du4.reshape(NDEV, n, ns_up)
        dh2 = _mm_bwd_in(du, w_up[layer], name=f"ffn_up_bwd_in{layer}", tm=tm, shard_in=True)
        dwu = _mm_bwd_w(h2, du, name=f"ffn_up_bwd_w{layer}", tm=tm, s=NDEV, ns=ns_up, shard_in=True)
        return dh2, dwu, dwd, dcw, dcb

    dh2, dwu_l1, dwd_l1, dcw_l1, dcb_l1 = ffn_bwd(1, df, u4_l1, hm_l1, h2_l1)
    dz, dymix, dsh2_l1, dsc2_l1, dgate1_l1, dn2w_l1 = _norm_mod_bwd(dh2, z1_l1, dz, n2w[1], pat[1][4], name="norm2_bwd_l1",
                                                                    dims=dims, res=(ymix1, pat[1][2]))
    dog = _mm_bwd_in(dymix, w_hgout, name="hg_out_bwd_in", tm=tm)
    dw_hgout = _mm_bwd_w(og, dymix, name="hg_out_bwd_w", tm=tm, s=1, ns=dm)

    def readout_bwd(rv, pv, cv, il):
        _, vjp = jax.vjp(_hg_readout, rv[0] + rv[1], rv[2], cv[0])
        do, dg, dw = vjp(rv[3])
        return [do, dg], [], [jnp.broadcast_to(dw, (SUBLANES, HG_HEAD)) * (1.0 / SUBLANES)]

    do, dg, dgnw = _rowk(readout_bwd, name="hg_readout_bwd", n=n, tm=dims["tm_row"], nctx=nctx,
                         rows=[(o_f, dm, 0, 0), (o_b, dm, 0, 0), (zz, dm, 4, 0), (dog, dm, 0, 0)], consts=[gnw],
                         out_rows=[(dm, F32, dm, 0), (dm, BF16, dm, 0)], out_acc=[HG_HEAD])
    dq0, dv0, dff, dl_f = _hg_bwd_dir(zz, lb2, do, sts_f, None, d=0, name="hg_bwd_d0", dims=dims)
    dq, dv, dfb, dl_b = _hg_bwd_dir(zz, lb2, do, sts_b, (dq0, dv0), d=1, name="hg_bwd_d1", dims=dims)
    dzz = jnp.concatenate([dq, dv, dff, dfb, dg], axis=1)
    dh1 = _mm_bwd_in(dzz, w_hgin, name="hg_in_bwd_in", tm=tm)
    dw_hgin = _mm_bwd_w(h1, dzz, name="hg_in_bwd_w", tm=tm, s=NDEV, ns=w_hgin.shape[2])
    dz, df0, dsh1_l1, dsc1_l1, dgate2_l0, dn1w_l1 = _norm_mod_bwd(dh1, z2_l0, dz, n1w[1], pat[1][1], name="norm1_bwd_l1",
                                                                  dims=dims, res=(f_l0, pat[0][5]))
    dh2, dwu_l0, dwd_l0, dcw_l0, dcb_l0 = ffn_bwd(0, df0, u4_l0, hm_l0, h2_l0)
    dz, dymix, dsh2_l0, dsc2_l0, dgate1_l0, dn2w_l0 = _norm_mod_bwd(dh2, z1_l0, dz, n2w[0], pat[0][4], name="norm2_bwd_l0",
                                                                    dims=dims, res=(ymix0, pat[0][2]))
    dz2g = _mm_bwd_in(dymix, w_s5out, name="s5_out_bwd_in", tm=tm)
    dw_s5out = _mm_bwd_w(z2g, dymix, name="s5_out_bwd_w", tm=tm, s=1, ns=dm)

    def gate_bwd(rv, pv, cv, il):
        sg = _sigmoid(rv[1])
        return [rv[2] * rv[0] * sg * (1.0 - sg), rv[2] * sg], [], []

    dt_glu, dzg_a = _rowk(gate_bwd, name="s5_gate_bwd", n=n, tm=dims["tm_row"], nctx=nctx,
                          rows=[(zg, dm, 0, 0), (t_glu, dm, 0, 0), (dz2g, dm, 0, 0)],
                          out_rows=[(dm, BF16, dm, 0), (dm, F32, dm, 0)])
    dzg_b = _mm_bwd_in(dt_glu, w_glu, name="s5_glu_bwd_in", tm=tm)
    dw_glu = _mm_bwd_w(zg, dt_glu, name="s5_glu_bwd_w", tm=tm, s=1, ns=dm)

    def gelu_bwd(rv, pv, cv, il):
        _, vjp = jax.vjp(_gelu, rv[0])
        return [vjp(rv[1] + rv[2])[0]], [], []

    (dy_s5,) = _rowk(gelu_bwd, name="s5_gelu_bwd", n=n, tm=dims["tm_row"], nctx=nctx,
                     rows=[(y_s5, dm, 0, 0), (dzg_a, dm, 0, 0), (dzg_b, dm, 0, 0)], out_rows=[(dm, F32, dm, 0)])
    du_a, dbbr0, dbbi0, dccr0, dcci0, dar0, dai0, ddsk = _s5_bwd_dir(
        u_s5, dy_s5, None, a8r[0], a8i[0], bbr[0], bbi[0], ccr[0], cci[0], dsk, st0r, st0i, d=0, name="s5_bwd_d0",
        dims=dims)
    du_s5, dbbr1, dbbi1, dccr1, dcci1, dar1, dai1 = _s5_bwd_dir(
        u_s5, dy_s5, du_a, a8r[1], a8i[1], bbr[1], bbi[1], ccr[1], cci[1], dsk, st1r, st1i, d=1, name="s5_bwd_d1",
        dims=dims)
    dh0 = _mm_bwd_in(du_s5, w_s5in, name="s5_in_bwd_in", tm=tm)
    dw_s5in = _mm_bwd_w(h0, du_s5, name="s5_in_bwd_w", tm=tm, s=1, ns=dm)
    dz0, dsh1_l0, dsc1_l0, dn1w_l0 = _norm_mod_bwd(dh0, z0, dz, n1w[0], pat[0][1], name="norm1_bwd_l0", dims=dims)

    dar = jnp.stack([dar0, dar1]).reshape(2, ngrp, 1, nstate)
    dai = jnp.stack([dai0, dai1]).reshape(2, ngrp, 1, nstate)
    dbbr = jnp.stack([_diag_b(dbbr0, hgrp, nstate), _diag_b(dbbr1, hgrp, nstate)])
    dbbi = jnp.stack([_diag_b(dbbi0, hgrp, nstate), _diag_b(dbbi1, hgrp, nstate)])
    dlr, dli, dls, dbrt, dbit = _s5_disc_bwd(lr4, li4, ls4, brt, bit, dar, dai, dbbr, dbbi)
    g_c_re = jnp.stack([_diag_c(dccr0, hgrp, nstate), _diag_c(dccr1, hgrp, nstate)])
    g_c_im = jnp.stack([_diag_c(dcci0, hgrp, nstate), _diag_c(dcci1, hgrp, nstate)])

    zeros8 = jnp.zeros((SUBLANES, dm), F32)
    dmod = jnp.stack([
        jnp.concatenate([dsh1_l0, dsc1_l0, dgate1_l0, dsh2_l0, dsc2_l0, dgate2_l0], axis=1),
        jnp.concatenate([dsh1_l1, dsc1_l1, dgate1_l1, dsh2_l1, dsc2_l1, dgate2_l1], axis=1)])
    del zeros8
    (dmod_g,) = _exchange([dmod], a2a=False, name="gather_dmod")
    dlat_full = jnp.transpose(dmod_g[:, :, :bsz], (1, 0, 2, 3)).reshape(2, NDEV * bsz, N_MOD * dm)
    dctx_full = dmod_g[:, :, bsz]
    dlat_sh = lax.dynamic_slice(dlat_full, (0, 0, me * nsm), (2, NDEV * bsz, nsm))
    dctx_sh = lax.dynamic_slice(dctx_full, (0, 0, me * nsm), (NDEV, 2, nsm))
    g_w_mod, g_b_mod, dcctx8 = _mod_bwd(craw, w_mod, dlat_sh, dctx_sh, dlat_full, dctx_full)

    big = [dw_s5in.reshape(NDEV, -1, dm), dw_glu.reshape(NDEV, -1, dm), dw_s5out.reshape(NDEV, -1, dm), dw_hgin,
           dw_hgout.reshape(NDEV, -1, dm), dwu_l0, dwu_l1, dwd_l0.reshape(NDEV, -1, dm), dwd_l1.reshape(NDEV, -1, dm),
           jnp.stack([dcw_l0.reshape(NDEV, 3, ns_up), dcw_l1.reshape(NDEV, 3, ns_up)], axis=1)]
    parts = _exchange(big, a2a=True, name="scatter_grads")
    (p_s5in, p_glu, p_s5out, p_hgin, p_hgout, p_up0, p_up1, p_dn0, p_dn1, p_cw) = parts

    dl_hg = jnp.stack([dl_f[:, 0], dl_b[:, 0]])
    small_names = ["c_ctx", "norm1_w", "norm2_w", "final_norm_w", "s5_lam_re", "s5_lam_im", "s5_log_step", "s5_b_re",
                   "s5_b_im", "s5_c_re", "s5_c_im", "s5_d", "hg_gnorm_w", "ffn_conv_b"]
    small_grads = [dcctx8[0], jnp.concatenate([dn1w_l0, dn1w_l1]), jnp.concatenate([dn2w_l0, dn2w_l1]), dfinal_w,
                   dlr, dli, dls, jnp.transpose(dbrt, (0, 1, 3, 2)), jnp.transpose(dbit, (0, 1, 3, 2)), g_c_re, g_c_im,
                   ddsk, dgnw, jnp.stack([dcb_l0.reshape(-1), dcb_l1.reshape(-1)])]
    width = 8 * LANES
    gbuf, sizes = _pack(small_grads + [dl_hg, loss_part], width)
    (gbuf_g,) = _exchange([gbuf], a2a=False, name="gather_small")
    nsm_rows = gbuf.shape[0]
    wbuf, _ = _pack([given[k] for k in small_names] + [jnp.zeros_like(dl_hg), jnp.zeros_like(loss_part)], width)
    mbuf, _ = _pack([given["m_" + k] for k in small_names] + [jnp.zeros_like(dl_hg), jnp.zeros_like(loss_part)], width)
    vbuf, _ = _pack([given["v_" + k] for k in small_names] + [jnp.zeros_like(dl_hg), jnp.zeros_like(loss_part)], width)
    sg, sdl, snm, snv = _adamw(gbuf_g, wbuf, mbuf, vbuf, name="adamw_small")
    shapes = [given[k].shape for k in small_names] + [dl_hg.shape, loss_part.shape]
    sg_l, sdl_l, snm_l, snv_l = (_unpack(b, sizes, shapes) for b in (sg, sdl, snm, snv))
    res = {k: (sg_l[i], sdl_l[i], snm_l[i], snv_l[i]) for i, k in enumerate(small_names)}
    loss = jnp.sum(sg_l[-1])
    dl_tot = sg_l[-2]
    nlb = hg_lower_bounds.shape[2]
    g_lb = lax.dynamic_slice(dl_tot, (0, 0, me * nlb), (2, 2, nlb))

    def adam_local(name, g, shape2):
        w, m, v = given[name], given["m_" + name], given["v_" + name]
        out = _adamw(g.reshape((1,) + shape2), w.reshape(shape2), m.reshape(shape2), v.reshape(shape2),
                     name="adamw_" + name)
        return tuple(o.reshape(w.shape) for o in out)

    def adam_parts(name, p):
        w, m, v = given[name], given["m_" + name], given["v_" + name]
        shape2 = (p.shape[0], -1, w.shape[-1])
        p3 = p.reshape(shape2)
        s2 = p3.shape[1:]
        out = _adamw(p3, w.reshape(s2), m.reshape(s2), v.reshape(s2), name="adamw_" + name)
        return tuple(o.reshape(w.shape) for o in out)

    res["hg_lower_bounds"] = adam_local("hg_lower_bounds", g_lb, (2 * 2, nlb))
    res["w_mod"] = adam_local("w_mod", g_w_mod, (2 * dm, nsm))
    res["b_mod"] = adam_local("b_mod", g_b_mod, (2, N_MOD * dm))
    res["s5_w_in"] = adam_parts("s5_w_in", p_s5in)
    res["s5_w_glu"] = adam_parts("s5_w_glu", p_glu)
    res["s5_w_out"] = adam_parts("s5_w_out", p_s5out)
    res["hg_w_in"] = adam_parts("hg_w_in", p_hgin)
    res["hg_w_out"] = adam_parts("hg_w_out", p_hgout)
    res["ffn_w_up"] = adam_parts("ffn_w_up", jnp.stack([p_up0, p_up1], axis=1))
    res["ffn_w_down"] = adam_parts("ffn_w_down", jnp.stack([p_dn0, p_dn1], axis=1))
    res["ffn_conv_w"] = adam_parts("ffn_conv_w", p_cw)

    grad_x = jnp.transpose(dz0[nctx:].reshape(lx, bsz, dm), (1, 0, 2))
    order = ["c_ctx", "w_mod", "b_mod", "norm1_w", "norm2_w", "final_norm_w", "s5_w_in", "s5_lam_re", "s5_lam_im",
             "s5_log_step", "s5_b_re", "s5_b_im", "s5_c_re", "s5_c_im", "s5_d", "s5_w_glu", "s5_w_out", "hg_w_in",
             "hg_lower_bounds", "hg_gnorm_w", "hg_w_out", "ffn_w_up", "ffn_conv_w", "ffn_conv_b", "ffn_w_down"]
    outs = [loss, grad_x]
    for j in range(4):
        outs += [res[k][j].reshape(given[k].shape) for k in order]
    return tuple(outs)
```

```python
import functools

import jax
import jax.numpy as jnp
from jax import lax
from jax.experimental import pallas as pl
from jax.experimental.pallas import tpu as pltpu

F32 = jnp.float32
BF16 = jnp.bfloat16
NDEV = 8
LOCAL_B = 4
NORM_EPS = 1e-6
N_MOD = 6
S5_GROUP = 16
S5_STATE = 64
S5_LAM_RE_MAX = -1e-4
HG_HEAD = 128
HG_ROWS = 128
GRID_W = 64
ADAM_LR, ADAM_B1, ADAM_B2, ADAM_EPS, ADAM_WD, ADAM_STEP = 0.001, 0.9, 0.999, 1e-08, 0.01, 10
VMEM_BYTES_V7X = 64 * 1024 * 1024
LANES = 128
SUBLANES = 8

NN = (((1,), (0,)), ((), ()))
NT = (((1,), (1,)), ((), ()))
TN = (((0,), (0,)), ((), ()))
MESH = pl.DeviceIdType.MESH


def _params(sem=None, vmem=None):
    kw = {}
    if sem is not None:
        kw["dimension_semantics"] = sem
    if vmem is not None:
        kw["vmem_limit_bytes"] = int(min(vmem, VMEM_BYTES_V7X - (4 << 20)))
    return pltpu.CompilerParams(**kw)


def _nbytes(shape, dtype):
    n = 1
    for s in shape:
        n *= 1 if s is None else s
    return n * jnp.dtype(dtype).itemsize


def _dot(a, b, dims=NN, precision=None):
    return lax.dot_general(a, b, dims, preferred_element_type=F32, precision=precision)


def _sigmoid(x):
    return 1.0 / (1.0 + jnp.exp(-x))


class _Xchg:
    def __init__(self, arrs, a2a):
        self.arrs, self.a2a, self.n = list(arrs), list(a2a), len(arrs)

    def out_shape(self):
        return [jax.ShapeDtypeStruct(a.shape if f else (NDEV,) + a.shape, a.dtype) for a, f in zip(self.arrs, self.a2a)]

    def scratch(self):
        return [pltpu.SemaphoreType.DMA((self.n * (NDEV - 1),)), pltpu.SemaphoreType.DMA((self.n * (NDEV - 1),)),
                pltpu.SemaphoreType.DMA((self.n,))]

    def _copies(self, ins, outs, sems, with_recvs):
        send_sems, recv_sems, loc_sems = sems
        x, y, c = lax.axis_index("x"), lax.axis_index("y"), lax.axis_index("c")
        me = 4 * x + 2 * y + c
        local, sends, recvs = [], [], []
        for a in range(self.n):
            src = ins[a].at[me] if self.a2a[a] else ins[a]
            local.append(pltpu.make_async_copy(src, outs[a].at[me], loc_sems.at[a]))
            for k in range(1, NDEV):
                px = (1 - x) if (k >> 2) & 1 else x
                py = (1 - y) if (k >> 1) & 1 else y
                pc = (1 - c) if k & 1 else c
                p = 4 * px + 2 * py + pc
                s = a * (NDEV - 1) + k - 1
                src = ins[a].at[p] if self.a2a[a] else ins[a]
                kw = dict(src_ref=src, send_sem=send_sems.at[s], recv_sem=recv_sems.at[s], device_id=(px, py, pc),
                          device_id_type=MESH)
                sends.append(pltpu.make_async_remote_copy(dst_ref=outs[a].at[me], **kw))
                if with_recvs:
                    recvs.append(pltpu.make_async_remote_copy(dst_ref=outs[a].at[p], **kw))
        return local, sends, recvs

    def start(self, ins, outs, sems):
        local, sends, _ = self._copies(ins, outs, sems, False)
        for cp in local + sends:
            cp.start()

    def wait(self, ins, outs, sems):
        local, sends, recvs = self._copies(ins, outs, sems, True)
        for cp in sends:
            cp.wait_send()
        for cp in recvs:
            cp.wait_recv()
        for cp in local:
            cp.wait()


def _exchange(arrs, *, a2a, name):
    xch = _Xchg(arrs, a2a if isinstance(a2a, (list, tuple)) else [a2a] * len(arrs))
    n = xch.n

    def body(*refs):
        xch.start(refs[:n], refs[n:2 * n], refs[2 * n:])
        xch.wait(refs[:n], refs[n:2 * n], refs[2 * n:])

    res = pl.pallas_call(
        body, name=name, out_shape=xch.out_shape(),
        in_specs=[pl.BlockSpec(memory_space=pl.ANY)] * n, out_specs=[pl.BlockSpec(memory_space=pl.ANY)] * n,
        scratch_shapes=xch.scratch(),
    )(*arrs)
    return list(res)


def _call(body, *, name, out_shape, grid, in_specs, out_specs, scratch, params, args, xch=None):
    in_specs, out_specs, out_shape, scratch, args = list(in_specs), list(out_specs), list(out_shape), list(scratch), list(args)
    n_in, n_out, n_scr = len(in_specs), len(out_shape), len(scratch)
    if xch is not None:
        k = xch.n
        inner = body

        def body(*refs):
            ins, xin = refs[:n_in], refs[n_in:n_in + k]
            outs, xout = refs[n_in + k:n_in + k + n_out], refs[n_in + k + n_out:n_in + 2 * k + n_out]
            scr = refs[n_in + 2 * k + n_out:n_in + 2 * k + n_out + n_scr]
            sems = refs[n_in + 2 * k + n_out + n_scr:]
            first = pl.program_id(0) == 0
            last = pl.program_id(0) == grid[0] - 1
            for ax in range(1, len(grid)):
                first = jnp.logical_and(first, pl.program_id(ax) == 0)
                last = jnp.logical_and(last, pl.program_id(ax) == grid[ax] - 1)

            @pl.when(first)
            def _():
                xch.start(xin, xout, sems)

            inner(*ins, *outs, *scr)

            @pl.when(last)
            def _():
                xch.wait(xin, xout, sems)

        anyspec = pl.BlockSpec(memory_space=pl.ANY)
        in_specs += [anyspec] * k
        out_specs += [anyspec] * k
        out_shape += xch.out_shape()
        scratch += xch.scratch()
        args += xch.arrs
    res = pl.pallas_call(body, name=name, out_shape=out_shape, grid=grid, in_specs=in_specs, out_specs=out_specs,
                         scratch_shapes=scratch, compiler_params=params)(*args)
    return list(res[:n_out]), list(res[n_out:])


def _mm(a, b, *, name, grid, a_spec, b_spec, o_spec, o_shape, o_dtype, dims):
    nk = grid[2]
    o_block = tuple(s for s in o_spec.block_shape if s is not None)

    def body(a_ref, b_ref, o_ref, *scr):
        r = _dot(a_ref[...].astype(BF16), b_ref[...].astype(BF16), dims)
        if nk == 1:
            o_ref[...] = r.astype(o_dtype)
        else:
            acc = scr[0]
            k = pl.program_id(2)

            @pl.when(k == 0)
            def _():
                acc[...] = r

            @pl.when(k > 0)
            def _():
                acc[...] += r

            @pl.when(k == nk - 1)
            def _():
                o_ref[...] = acc[...].astype(o_dtype)

    blocks = (_nbytes(a_spec.block_shape, a.dtype) + _nbytes(b_spec.block_shape, b.dtype) + _nbytes(o_block, o_dtype))
    scratch = [pltpu.VMEM(o_block, F32)] if nk > 1 else []
    vmem = 2 * blocks + 3 * _nbytes(o_block, F32) + (8 << 20)
    return pl.pallas_call(
        body, name=name, out_shape=jax.ShapeDtypeStruct(o_shape, o_dtype), grid=grid,
        in_specs=[a_spec, b_spec], out_specs=o_spec, scratch_shapes=scratch,
        compiler_params=_params(("parallel", "parallel", "arbitrary"), vmem),
    )(a, b)


def _mm_fwd(a, w3, *, name, tm, shard_out=False, o_dtype=F32):
    n, kk = a.shape
    s, _, ns = w3.shape
    if shard_out:
        o_shape, o_spec = (s, n, ns), pl.BlockSpec((None, tm, ns), lambda i, j, k: (j, i, 0))
    else:
        o_shape, o_spec = (n, s * ns), pl.BlockSpec((tm, ns), lambda i, j, k: (i, j))
    return _mm(a, w3, name=name, grid=(n // tm, s, 1), dims=NN, o_shape=o_shape, o_dtype=o_dtype, o_spec=o_spec,
               a_spec=pl.BlockSpec((tm, kk), lambda i, j, k: (i, 0)),
               b_spec=pl.BlockSpec((None, kk, ns), lambda i, j, k: (j, 0, 0)))


def _mm_bwd_in(dy, w3, *, name, tm, shard_in=False, o_dtype=F32):
    s, kk, ns = w3.shape
    if shard_in:
        n = dy.shape[1]
        a_spec = pl.BlockSpec((None, tm, ns), lambda i, j, k: (k, i, 0))
    else:
        n = dy.shape[0]
        a_spec = pl.BlockSpec((tm, ns), lambda i, j, k: (i, k))
    return _mm(dy, w3, name=name, grid=(n // tm, 1, s), dims=NT, o_shape=(n, kk), o_dtype=o_dtype,
               o_spec=pl.BlockSpec((tm, kk), lambda i, j, k: (i, 0)), a_spec=a_spec,
               b_spec=pl.BlockSpec((None, kk, ns), lambda i, j, k: (k, 0, 0)))


def _mm_bwd_w(a, dy, *, name, tm, s, ns, shard_in=False):
    n, kk = a.shape
    if shard_in:
        b_spec = pl.BlockSpec((None, tm, ns), lambda i, j, k: (j, k, 0))
    else:
        b_spec = pl.BlockSpec((tm, ns), lambda i, j, k: (k, j))
    return _mm(a, dy, name=name, grid=(1, s, n // tm), dims=TN, o_shape=(s, kk, ns), o_dtype=BF16,
               o_spec=pl.BlockSpec((None, kk, ns), lambda i, j, k: (j, 0, 0)),
               a_spec=pl.BlockSpec((tm, kk), lambda i, j, k: (k, 0)), b_spec=b_spec)


def _mm_down_fwd(hm, wd, *, name, tm):
    s, n, ks = hm.shape
    d = wd.shape[2]
    return _mm(hm, wd, name=name, grid=(n // tm, 1, s), dims=NN, o_shape=(n, d), o_dtype=F32,
               o_spec=pl.BlockSpec((tm, d), lambda i, j, k: (i, 0)),
               a_spec=pl.BlockSpec((None, tm, ks), lambda i, j, k: (k, i, 0)),
               b_spec=pl.BlockSpec((None, ks, d), lambda i, j, k: (k, 0, 0)))


def _mm_down_bwd_in(df, wd, *, name, tm):
    n, d = df.shape
    s, ks, _ = wd.shape
    return _mm(df, wd, name=name, grid=(n // tm, s, 1), dims=NT, o_shape=(s, n, ks), o_dtype=BF16,
               o_spec=pl.BlockSpec((None, tm, ks), lambda i, j, k: (j, i, 0)),
               a_spec=pl.BlockSpec((tm, d), lambda i, j, k: (i, 0)),
               b_spec=pl.BlockSpec((None, ks, d), lambda i, j, k: (j, 0, 0)))


def _mm_down_bwd_w(hm, df, *, name, tm):
    s, n, ks = hm.shape
    d = df.shape[1]
    return _mm(hm, df, name=name, grid=(1, s, n // tm), dims=TN, o_shape=(s, ks, d), o_dtype=BF16,
               o_spec=pl.BlockSpec((None, ks, d), lambda i, j, k: (j, 0, 0)),
               a_spec=pl.BlockSpec((None, tm, ks), lambda i, j, k: (j, k, 0)),
               b_spec=pl.BlockSpec((tm, d), lambda i, j, k: (k, 0)))


def _rowk(fn, *, name, n, tm, nctx, rows=(), pats=(), consts=(), out_rows=(), out_seg=(), out_acc=()):
    nb, ncb = n // tm, nctx // tm
    nr, npat, ncst = len(rows), len(pats), len(consts)
    no, nseg, nacc = len(out_rows), len(out_seg), len(out_acc)
    in_specs, blocks = [], 0
    for arr, w, cb, off in rows:
        in_specs.append(pl.BlockSpec((tm, w), lambda i, cb=cb, off=off: (jnp.maximum(i - off, 0), cb)))
        blocks += _nbytes((tm, w), arr.dtype)
    for p in pats:
        in_specs.append(pl.BlockSpec((None, SUBLANES, p.shape[2]), lambda i: (jnp.where(i >= ncb, 1, 0), 0, 0)))
    for cst in consts:
        in_specs.append(pl.BlockSpec(cst.shape, lambda i: (0, 0)))
    out_shape, out_specs = [], []
    for wt, dt, w, cb in out_rows:
        out_shape.append(jax.ShapeDtypeStruct((n, wt), dt))
        out_specs.append(pl.BlockSpec((tm, w), lambda i, cb=cb: (i, cb)))
        blocks += _nbytes((tm, w), dt)
    for w in out_seg:
        out_shape.append(jax.ShapeDtypeStruct((SUBLANES, w), F32))
        out_specs.append(pl.BlockSpec((SUBLANES, w), lambda i: (0, 0)))
    for w in out_acc:
        out_shape.append(jax.ShapeDtypeStruct((1, w), F32))
        out_specs.append(pl.BlockSpec((1, w), lambda i: (0, 0)))
    scratch = [pltpu.VMEM((2, SUBLANES, w), F32) for w in out_seg] + [pltpu.VMEM((SUBLANES, w), F32) for w in out_acc]

    def body(*refs):
        r_in = refs[:nr]
        p_in = refs[nr:nr + npat]
        c_in = refs[nr + npat:nr + npat + ncst]
        base = nr + npat + ncst
        o_rows = refs[base:base + no]
        o_seg = refs[base + no:base + no + nseg]
        o_acc = refs[base + no + nseg:base + no + nseg + nacc]
        s_seg = refs[base + no + nseg + nacc:base + no + nseg + nacc + nseg]
        s_acc = refs[base + no + nseg + nacc + nseg:]
        i = pl.program_id(0)
        rv = [r[...].astype(F32).reshape(tm // SUBLANES, SUBLANES, r.shape[1]) for r in r_in]
        pv = [p[...] for p in p_in]
        cv = [c[...] for c in c_in]
        is_lat = (i >= ncb).astype(F32)
        ro, so, ao = fn(rv, pv, cv, is_lat)
        for ref, val in zip(o_rows, ro):
            ref[...] = val.reshape(tm, ref.shape[1]).astype(ref.dtype)
        if nseg or nacc:
            @pl.when(i == 0)
            def _():
                for s in list(s_seg) + list(s_acc):
                    s[...] = jnp.zeros(s.shape, F32)

            seg = jnp.where(i >= ncb, 1, 0)
            for s, val in zip(s_seg, so):
                s[seg] = s[seg] + val
            for s, val in zip(s_acc, ao):
                s[...] = s[...] + val

            @pl.when(i == nb - 1)
            def _():
                for o, s in zip(o_seg, s_seg):
                    lat, ctx = s[1], s[0]
                    row = lax.broadcasted_iota(jnp.int32, lat.shape, 0)
                    lat = lat + pltpu.roll(lat, 4, 0)
                    ctx = jnp.broadcast_to(jnp.sum(ctx, axis=0, keepdims=True), lat.shape)
                    o[...] = jnp.where(row < 4, lat, jnp.where(row == 4, ctx, 0.0))
                for o, s in zip(o_acc, s_acc):
                    o[...] = jnp.sum(s[...], axis=0, keepdims=True)

    vmem = 2 * blocks + 8 * tm * 1024 * 4 + (8 << 20)
    res = pl.pallas_call(
        body, name=name, out_shape=out_shape, grid=(nb,), in_specs=in_specs, out_specs=out_specs,
        scratch_shapes=scratch, compiler_params=_params(("arbitrary",), vmem),
    )(*[r[0] for r in rows], *pats, *consts)
    return list(res)


def _rms(z):
    return lax.rsqrt(jnp.mean(z * z, axis=-1, keepdims=True) + NORM_EPS)


def _norm_mod_fwd(z, w, sh, sc, *, name, dims, res=None):
    n, d = z.shape

    def fn(rv, pv, cv, is_lat):
        zz = rv[0]
        if res is not None:
            zz = zz + pv[2][None] * rv[1]
        h = (zz * _rms(zz) * cv[0]) * (1.0 + pv[1][None]) + pv[0][None]
        return ([zz, h] if res is not None else [h]), [], []

    rows = [(z, d, 0, 0)] + ([(res[0], d, 0, 0)] if res is not None else [])
    pats = [sh, sc] + ([res[1]] if res is not None else [])
    outs = ([(d, F32, d, 0)] if res is not None else []) + [(d, BF16, d, 0)]
    out = _rowk(fn, name=name, n=n, tm=dims["tm_row"], nctx=dims["nctx"], rows=rows, pats=pats, consts=[w],
                out_rows=outs)
    return (out[0], out[1]) if res is not None else (None, out[0])


def _norm_core_bwd(zin, dh, w, sc):
    r = _rms(zin)
    xh = zin * r
    dsh = jnp.sum(dh, axis=0)
    dsc = jnp.sum(dh * (xh * w), axis=0)
    dyv = dh * (1.0 + sc[None])
    dw = jnp.sum(dyv * xh, axis=0)
    dxh = dyv * w
    dx = r * (dxh - xh * jnp.mean(dxh * xh, axis=-1, keepdims=True))
    return dx, dsh, dsc, dw


def _norm_mod_bwd(dh, zin, dz_up, w, sc, *, name, dims, res=None):
    n, d = zin.shape

    def fn(rv, pv, cv, is_lat):
        dx, dsh, dsc, dw = _norm_core_bwd(rv[1], rv[0], cv[0], pv[0])
        dz = rv[2] + dx
        if res is None:
            return [dz], [dsh, dsc], [dw]
        return [dz, dz * pv[1][None]], [dsh, dsc, jnp.sum(dz * rv[3], axis=0)], [dw]

    rows = [(dh, d, 0, 0), (zin, d, 0, 0), (dz_up, d, 0, 0)] + ([(res[0], d, 0, 0)] if res is not None else [])
    pats = [sc] + ([res[1]] if res is not None else [])
    outs = [(d, F32, d, 0)] + ([(d, BF16, d, 0)] if res is not None else [])
    return _rowk(fn, name=name, n=n, tm=dims["tm_row"], nctx=dims["nctx"], rows=rows, pats=pats, consts=[w],
                 out_rows=outs, out_seg=[d] * (3 if res is not None else 2), out_acc=[d])


def _loss_bwd(z1, f, gate, tgt, w, *, name, dims):
    n, d = z1.shape

    def fn(rv, pv, cv, is_lat):
        z2 = rv[0] + pv[0][None] * rv[1]
        r = _rms(z2)
        xh = z2 * r
        err = (xh * cv[0] - rv[2]) * is_lat
        dout = err * (1.0 / d)
        dxh = dout * cv[0]
        dz = r * (dxh - xh * jnp.mean(dxh * xh, axis=-1, keepdims=True))
        return ([dz, dz * pv[0][None]], [jnp.sum(dz * rv[1], axis=0)],
                [jnp.sum(0.5 * err * err * (1.0 / d), axis=0), jnp.sum(dout * xh, axis=0)])

    tm = dims["tm_row"]
    rows = [(z1, d, 0, 0), (f, d, 0, 0), (tgt, d, 0, dims["nctx"] // tm)]
    return _rowk(fn, name=name, n=n, tm=tm, nctx=dims["nctx"], rows=rows, pats=[gate], consts=[w],
                 out_rows=[(d, F32, d, 0), (d, BF16, d, 0)], out_seg=[d], out_acc=[d, d])


def _gelu(y):
    return jax.nn.gelu(y, approximate=True)


def _conv_masks(tb, i):
    tok = lax.broadcasted_iota(jnp.int32, (tb, 1), 0) >> 2
    last = jnp.where(i == 0, tb // LOCAL_B - 1, GRID_W - 1)
    wpos = tok & last
    return wpos == 0, wpos == last


def _convffn_fwd(u, cw, cb, *, name, dims):
    _, sh, n, ns = u.shape
    tb = dims["nctx"]

    def body(u_ref, cw_ref, cb_ref, o_ref):
        no_left, no_right = _conv_masks(tb, pl.program_id(1))

        def conv(s):
            uu = u_ref[s]
            ul = jnp.where(no_left, 0.0, pltpu.roll(uu, LOCAL_B, 0))
            ur = jnp.where(no_right, 0.0, pltpu.roll(uu, tb - LOCAL_B, 0))
            return (cb_ref[s] + ul * cw_ref[s, pl.ds(0, 1), :] + uu * cw_ref[s, pl.ds(1, 1), :]
                    + ur * cw_ref[s, pl.ds(2, 1), :])

        a, g = conv(0), conv(1)
        o_ref[...] = (a * _sigmoid(a) * g).astype(BF16)

    vmem = 2 * (2 * tb * ns * 4 + tb * ns * 2) + 10 * tb * ns * 4 + (8 << 20)
    return pl.pallas_call(
        body, name=name, out_shape=jax.ShapeDtypeStruct((sh, n, ns), BF16), grid=(sh, n // tb),
        in_specs=[pl.BlockSpec((2, None, tb, ns), lambda j, i: (0, j, i, 0)),
                  pl.BlockSpec((2, None, 3, ns), lambda j, i: (0, j, 0, 0)),
                  pl.BlockSpec((2, None, 1, ns), lambda j, i: (0, j, 0, 0))],
        out_specs=pl.BlockSpec((None, tb, ns), lambda j, i: (j, i, 0)),
        compiler_params=_params(("parallel", "arbitrary"), vmem),
    )(u, cw, cb)


def _convffn_bwd(u, dhm, cw, cb, *, name, dims, xch=None):
    _, sh, n, ns = u.shape
    tb = dims["nctx"]

    def body(u_ref, dh_ref, cw_ref, cb_ref, du_ref, dcw_ref, dcb_ref):
        i = pl.program_id(1)
        no_left, no_right = _conv_masks(tb, i)

        @pl.when(i == 0)
        def _():
            dcw_ref[...] = jnp.zeros(dcw_ref.shape, F32)
            dcb_ref[...] = jnp.zeros(dcb_ref.shape, F32)

        def taps(s):
            uu = u_ref[s]
            ul = jnp.where(no_left, 0.0, pltpu.roll(uu, LOCAL_B, 0))
            ur = jnp.where(no_right, 0.0, pltpu.roll(uu, tb - LOCAL_B, 0))
            val = (cb_ref[s] + ul * cw_ref[s, pl.ds(0, 1), :] + uu * cw_ref[s, pl.ds(1, 1), :]
                   + ur * cw_ref[s, pl.ds(2, 1), :])
            return val, ul, uu, ur

        a, al, ac, ar = taps(0)
        g, gl, gc, gr = taps(1)
        dh = dh_ref[...].astype(F32)
        sa = _sigmoid(a)
        dg = dh * (a * sa)
        da = dh * g * (sa * (1.0 + a * (1.0 - sa)))
        for s, dc, (tl, tc, tr) in ((0, da, (al, ac, ar)), (1, dg, (gl, gc, gr))):
            dcb_ref[s] += jnp.sum(dc, axis=0, keepdims=True)
            dcw_ref[s, pl.ds(0, 1), :] += jnp.sum(dc * tl, axis=0, keepdims=True)
            dcw_ref[s, pl.ds(1, 1), :] += jnp.sum(dc * tc, axis=0, keepdims=True)
            dcw_ref[s, pl.ds(2, 1), :] += jnp.sum(dc * tr, axis=0, keepdims=True)
            du = (dc * cw_ref[s, pl.ds(1, 1), :]
                  + pltpu.roll(jnp.where(no_left, 0.0, dc) * cw_ref[s, pl.ds(0, 1), :], tb - LOCAL_B, 0)
                  + pltpu.roll(jnp.where(no_right, 0.0, dc) * cw_ref[s, pl.ds(2, 1), :], LOCAL_B, 0))
            du_ref[s] = du.astype(BF16)

    vmem = 2 * (2 * tb * ns * 4 + tb * ns * 2 + 2 * tb * ns * 2) + 16 * tb * ns * 4 + (8 << 20)
    return _call(
        body, name=name, xch=xch, args=[u, dhm, cw, cb], scratch=[],
        out_shape=[jax.ShapeDtypeStruct((2, sh, n, ns), BF16), jax.ShapeDtypeStruct((2, sh, 3, ns), F32),
                   jax.ShapeDtypeStruct((2, sh, 1, ns), F32)],
        grid=(sh, n // tb),
        in_specs=[pl.BlockSpec((2, None, tb, ns), lambda j, i: (0, j, i, 0)),
                  pl.BlockSpec((None, tb, ns), lambda j, i: (j, i, 0)),
                  pl.BlockSpec((2, None, 3, ns), lambda j, i: (0, j, 0, 0)),
                  pl.BlockSpec((2, None, 1, ns), lambda j, i: (0, j, 0, 0))],
        out_specs=[pl.BlockSpec((2, None, tb, ns), lambda j, i: (0, j, i, 0)),
                   pl.BlockSpec((2, None, 3, ns), lambda j, i: (0, j, 0, 0)),
                   pl.BlockSpec((2, None, 1, ns), lambda j, i: (0, j, 0, 0))],
        params=_params(("arbitrary", "arbitrary"), vmem))


def _s5_disc(lr, li, ls, brt, bit):
    lr = jnp.minimum(lr, S5_LAM_RE_MAX)
    dt = jnp.exp(ls)
    mag = jnp.exp(lr * dt)
    ar = mag * jnp.cos(li * dt)
    ai = mag * jnp.sin(li * dt)
    den = lr * lr + li * li
    nr = ar - 1.0
    cr = (nr * lr + ai * li) / den
    ci = (ai * lr - nr * li) / den
    return ar, ai, cr * brt - ci * bit, cr * bit + ci * brt


def _s5_disc_fwd(lr, li, ls, brt, bit):
    def body(lr_ref, li_ref, ls_ref, br_ref, bi_ref, ar_ref, ai_ref, bbr_ref, bbi_ref):
        ar, ai, bbr, bbi = _s5_disc(lr_ref[...], li_ref[...], ls_ref[...], br_ref[...], bi_ref[...])
        ar_ref[...] = ar
        ai_ref[...] = ai
        bbr_ref[...] = bbr
        bbi_ref[...] = bbi

    sd = jax.ShapeDtypeStruct
    return pl.pallas_call(body, name="s5_disc_fwd",
                          out_shape=[sd(lr.shape, F32), sd(lr.shape, F32), sd(brt.shape, F32), sd(brt.shape, F32)],
                          compiler_params=_params(None, 32 << 20))(lr, li, ls, brt, bit)


def _s5_disc_bwd(lr, li, ls, brt, bit, dar, dai, dbbr, dbbi):
    def body(lr_ref, li_ref, ls_ref, br_ref, bi_ref, dar_ref, dai_ref, dbbr_ref, dbbi_ref,
             dlr_ref, dli_ref, dls_ref, dbr_ref, dbi_ref):
        _, vjp = jax.vjp(_s5_disc, lr_ref[...], li_ref[...], ls_ref[...], br_ref[...], bi_ref[...])
        dlr, dli, dls, dbr, dbi = vjp((dar_ref[...], dai_ref[...], dbbr_ref[...], dbbi_ref[...]))
        dlr_ref[...] = dlr
        dli_ref[...] = dli
        dls_ref[...] = dls
        dbr_ref[...] = dbr
        dbi_ref[...] = dbi

    sd = jax.ShapeDtypeStruct
    return pl.pallas_call(body, name="s5_disc_bwd",
                          out_shape=[sd(lr.shape, F32), sd(lr.shape, F32), sd(ls.shape, F32), sd(brt.shape, F32),
                                     sd(brt.shape, F32)],
                          compiler_params=_params(None, 48 << 20))(lr, li, ls, brt, bit, dar, dai, dbbr, dbbi)


def _cmul(ar, ai, xr, xi):
    return ar * xr - ai * xi, ar * xi + ai * xr


def _scan_consts(a_r, a_i, rev):
    row = lax.broadcasted_iota(jnp.int32, a_r.shape, 0)
    second = (row < 4) if rev else (row >= 4)
    a2r, a2i = _cmul(a_r, a_i, a_r, a_i)
    a1r, a1i = jnp.where(second, a_r, 0.0), jnp.where(second, a_i, 0.0)
    apr, api = jnp.where(second, a2r, a_r), jnp.where(second, a2i, a_i)
    return second, a1r, a1i, apr, api


def _scan_tile(xr, xi, pr, pi, consts):
    second, a1r, a1i, apr, api = consts
    sr, si = pltpu.roll(xr, 4, 0), pltpu.roll(xi, 4, 0)
    t1r, t1i = _cmul(a1r, a1i, sr, si)
    t2r, t2i = _cmul(apr, api, pr, pi)
    yr, yi = xr + t1r + t2r, xi + t1i + t2i
    npr = jnp.where(second, yr, pltpu.roll(yr, 4, 0))
    npi = jnp.where(second, yi, pltpu.roll(yi, 4, 0))
    return yr, yi, npr, npi


def _s5_scan(xr_ref, xi_ref, row0, nrows, a_r_ref, a_i_ref, cr_ref, ci_ref, *, rev, conj, lane_block, extra=None):
    width = xr_ref.shape[1]
    nt = nrows // SUBLANES
    for lb in range(width // lane_block):
        lanes = pl.ds(lb * lane_block, lane_block)
        a_r = a_r_ref[:, lanes]
        a_i = a_i_ref[:, lanes]
        if conj:
            a_i = -a_i
        consts = _scan_consts(a_r, a_i, rev)

        def step(t, carry):
            pr, pi = carry[0], carry[1]
            j = (nt - 1 - t) if rev else t
            rows = pl.ds(pl.multiple_of(row0 + j * SUBLANES, SUBLANES), SUBLANES)
            yr, yi, pr, pi = _scan_tile(xr_ref[rows, lanes], xi_ref[rows, lanes], pr, pi, consts)
            xr_ref[rows, lanes] = yr
            xi_ref[rows, lanes] = yi
            if extra is None:
                return pr, pi
            return (pr, pi) + tuple(extra(j, lanes, yr, yi, carry[2:]))

        init = (cr_ref[:, lanes], ci_ref[:, lanes])
        if extra is not None:
            init = init + tuple(extra.init(lanes))
        out = lax.fori_loop(0, nt, step, init)
        cr_ref[:, lanes] = out[0]
        ci_ref[:, lanes] = out[1]
        if extra is not None:
            extra.done(lanes, out[2:])


def _s5_chunk_of(step, ncc, nc, rev):
    if not rev:
        return step
    return jnp.where(step < ncc, ncc - 1 - step, nc - 1 - (step - ncc))


def _s5_fwd_dir(u, base, a8r, a8i, bbr, bbi, ccr, cci, dsk, *, d, name, dims, xch=None):
    n, dm = u.shape
    nk, swk = bbr.shape[0], bbr.shape[2]
    rr, sw = dims["s5_rows"], nk * swk
    nc, ncc = n // rr, dims["nctx"] // rr
    rev = d == 1
    cmap = lambda i: (_s5_chunk_of(i, ncc, nc, rev), 0)

    def body(*refs):
        if d == 0:
            u_ref, a8r_ref, a8i_ref, bbr_ref, bbi_ref, ccr_ref, cci_ref, dsk_ref = refs[:8]
            rest = refs[8:]
        else:
            u_ref, base_ref, a8r_ref, a8i_ref, bbr_ref, bbi_ref, ccr_ref, cci_ref = refs[:8]
            rest = refs[8:]
        y_ref, str_ref, sti_ref, sr, si, cr, ci = rest
        i = pl.program_id(0)

        @pl.when(i == 0)
        def _():
            cr[...] = jnp.zeros(cr.shape, F32)
            ci[...] = jnp.zeros(ci.shape, F32)

        str_ref[...] = cr[...]
        sti_ref[...] = ci[...]
        ub = u_ref[...].astype(BF16)
        for k in range(nk):
            uk = ub[:, k * LANES:(k + 1) * LANES]
            sr[:, k * swk:(k + 1) * swk] = _dot(uk, bbr_ref[k])
            si[:, k * swk:(k + 1) * swk] = _dot(uk, bbi_ref[k])
        _s5_scan(sr, si, 0, rr, a8r_ref, a8i_ref, cr, ci, rev=rev, conj=False, lane_block=dims["s5_lane_block"])
        for k in range(nk):
            cols = slice(k * LANES, (k + 1) * LANES)
            yk = (_dot(sr[:, k * swk:(k + 1) * swk].astype(BF16), ccr_ref[k])
                  - _dot(si[:, k * swk:(k + 1) * swk].astype(BF16), cci_ref[k]))
            if d == 0:
                y_ref[:, cols] = yk + dsk_ref[:, cols] * u_ref[:, cols]
            else:
                y_ref[:, cols] = yk + base_ref[:, cols]

    row_spec = pl.BlockSpec((rr, dm), cmap)
    full = lambda a: pl.BlockSpec(a.shape, lambda i: (0,) * a.ndim)
    ins = [u] + ([] if d == 0 else [base]) + [a8r, a8i, bbr, bbi, ccr, cci] + ([dsk] if d == 0 else [])
    in_specs = [row_spec] + ([] if d == 0 else [row_spec]) + [full(a) for a in (a8r, a8i, bbr, bbi, ccr, cci)]
    in_specs += [full(dsk)] if d == 0 else []
    st_spec = pl.BlockSpec((None, SUBLANES, sw), lambda i: (_s5_chunk_of(i, ncc, nc, rev), 0, 0))
    vmem = 2 * rr * sw * 4 + 6 * rr * dm * 4 + 8 * nk * LANES * swk * 2 + (12 << 20)
    return _call(
        body, name=name, xch=xch, args=ins,
        out_shape=[jax.ShapeDtypeStruct((n, dm), F32), jax.ShapeDtypeStruct((nc, SUBLANES, sw), F32),
                   jax.ShapeDtypeStruct((nc, SUBLANES, sw), F32)],
        grid=(nc,), in_specs=in_specs, out_specs=[row_spec, st_spec, st_spec],
        scratch=[pltpu.VMEM((rr, sw), F32), pltpu.VMEM((rr, sw), F32), pltpu.VMEM((SUBLANES, sw), F32),
                 pltpu.VMEM((SUBLANES, sw), F32)],
        params=_params(("arbitrary",), vmem))


class _DaHook:
    def __init__(self, sr, si, accr, acci, rev_fwd):
        self.sr, self.si, self.accr, self.acci, self.rev_fwd = sr, si, accr, acci, rev_fwd

    def init(self, lanes):
        return self.accr[:, lanes], self.acci[:, lanes]

    def done(self, lanes, acc):
        self.accr[:, lanes] = acc[0]
        self.acci[:, lanes] = acc[1]

    def __call__(self, j, lanes, lr, li, acc):
        base = pl.multiple_of(SUBLANES + j * SUBLANES, SUBLANES)
        cur = pl.ds(base, SUBLANES)
        row = lax.broadcasted_iota(jnp.int32, lr.shape, 0)
        if self.rev_fwd:
            oth = pl.ds(pl.multiple_of(base + SUBLANES, SUBLANES), SUBLANES)
            spr = pltpu.roll(jnp.where(row >= 4, self.sr[cur, lanes], self.sr[oth, lanes]), 4, 0)
            spi = pltpu.roll(jnp.where(row >= 4, self.si[cur, lanes], self.si[oth, lanes]), 4, 0)
        else:
            oth = pl.ds(pl.multiple_of(base - SUBLANES, SUBLANES), SUBLANES)
            spr = pltpu.roll(jnp.where(row >= 4, self.sr[oth, lanes], self.sr[cur, lanes]), 4, 0)
            spi = pltpu.roll(jnp.where(row >= 4, self.si[oth, lanes], self.si[cur, lanes]), 4, 0)
        return acc[0] + spr * lr + spi * li, acc[1] + spr * li - spi * lr


def _s5_bwd_dir(u, dy, du_prev, a8r, a8i, bbr, bbi, ccr, cci, dsk, st_r, st_i, *, d, name, dims, xch=None):
    n, dm = u.shape
    rr = dims["s5_rows"]
    nk, swk = bbr.shape[0], bbr.shape[2]
    sw = nk * swk
    nc, ncc = n // rr, dims["nctx"] // rr
    rev = d == 1
    chunk = lambda i: _s5_chunk_of(nc - 1 - i, ncc, nc, rev)

    def body(*refs):
        u_ref, dy_ref = refs[0], refs[1]
        pos = 2
        dup_ref = None
        if d == 1:
            dup_ref = refs[pos]
            pos += 1
        a8r_ref, a8i_ref, bbr_ref, bbi_ref, ccr_ref, cci_ref = refs[pos:pos + 6]
        pos += 6
        dsk_ref = None
        if d == 0:
            dsk_ref = refs[pos]
            pos += 1
        str_ref, sti_ref = refs[pos:pos + 2]
        pos += 2
        du_ref, dbbr_ref, dbbi_ref, dccr_ref, dcci_ref, dar_ref, dai_ref = refs[pos:pos + 7]
        pos += 7
        dd_ref = None
        if d == 0:
            dd_ref = refs[pos]
            pos += 1
        sr, si, lr, li, cr, ci, lcr, lci, accr, acci, dda = refs[pos:]
        i = pl.program_id(0)

        @pl.when(i == 0)
        def _():
            for ref in (lcr, lci, accr, acci, dda, dbbr_ref, dbbi_ref, dccr_ref, dcci_ref):
                ref[...] = jnp.zeros(ref.shape, F32)

        cr[...] = str_ref[...]
        ci[...] = sti_ref[...]
        spare = pl.ds(rr + SUBLANES, SUBLANES) if rev else pl.ds(0, SUBLANES)
        sr[spare, :] = str_ref[...]
        si[spare, :] = sti_ref[...]
        ub = u_ref[...].astype(BF16)
        dyb = dy_ref[...].astype(BF16)
        for k in range(nk):
            uk = ub[:, k * LANES:(k + 1) * LANES]
            sr[pl.ds(SUBLANES, rr), k * swk:(k + 1) * swk] = _dot(uk, bbr_ref[k])
            si[pl.ds(SUBLANES, rr), k * swk:(k + 1) * swk] = _dot(uk, bbi_ref[k])
        _s5_scan(sr, si, SUBLANES, rr, a8r_ref, a8i_ref, cr, ci, rev=rev, conj=False,
                 lane_block=dims["s5_lane_block"])
        for k in range(nk):
            dyk = dyb[:, k * LANES:(k + 1) * LANES]
            sl = slice(k * swk, (k + 1) * swk)
            lr[:, sl] = _dot(dyk, ccr_ref[k], NT)
            li[:, sl] = -_dot(dyk, cci_ref[k], NT)
            dccr_ref[k] += _dot(sr[pl.ds(SUBLANES, rr), sl].astype(BF16), dyk, TN)
            dcci_ref[k] -= _dot(si[pl.ds(SUBLANES, rr), sl].astype(BF16), dyk, TN)
        _s5_scan(lr, li, 0, rr, a8r_ref, a8i_ref, lcr, lci, rev=not rev, conj=True,
                 lane_block=dims["s5_lane_block"], extra=_DaHook(sr, si, accr, acci, rev))
        for k in range(nk):
            cols = slice(k * LANES, (k + 1) * LANES)
            sl = slice(k * swk, (k + 1) * swk)
            uk = ub[:, cols]
            lrk, lik = lr[:, sl].astype(BF16), li[:, sl].astype(BF16)
            dbbr_ref[k] += _dot(uk, lrk, TN)
            dbbi_ref[k] += _dot(uk, lik, TN)
            duk = _dot(lrk, bbr_ref[k], NT) + _dot(lik, bbi_ref[k], NT)
            if d == 0:
                du_ref[:, cols] = duk + dsk_ref[:, cols] * dy_ref[:, cols]
            else:
                du_ref[:, cols] = duk + dup_ref[:, cols]
        if d == 0:
            prod = (dy_ref[...] * u_ref[...]).reshape(rr // SUBLANES, SUBLANES, dm)
            dda[...] += jnp.sum(prod, axis=0)

        @pl.when(i == nc - 1)
        def _():
            dar_ref[...] = jnp.sum(accr[...], axis=0, keepdims=True)
            dai_ref[...] = jnp.sum(acci[...], axis=0, keepdims=True)
            if d == 0:
                dd_ref[...] = jnp.sum(dda[...], axis=0, keepdims=True)

    row_spec = pl.BlockSpec((rr, dm), lambda i: (chunk(i), 0))
    full = lambda a: pl.BlockSpec(a.shape, lambda i: (0,) * a.ndim)
    st_spec = pl.BlockSpec((None, SUBLANES, sw), lambda i: (chunk(i), 0, 0))
    ins = [u, dy] + ([du_prev] if d == 1 else []) + [a8r, a8i, bbr, bbi, ccr, cci] + ([dsk] if d == 0 else [])
    ins += [st_r, st_i]
    in_specs = [row_spec, row_spec] + ([row_spec] if d == 1 else []) + [full(a) for a in (a8r, a8i, bbr, bbi, ccr, cci)]
    in_specs += ([full(dsk)] if d == 0 else []) + [st_spec, st_spec]
    sd = jax.ShapeDtypeStruct
    out_shape = [sd((n, dm), F32), sd(bbr.shape, F32), sd(bbr.shape, F32), sd(ccr.shape, F32),
                 sd(ccr.shape, F32), sd((1, sw), F32), sd((1, sw), F32)] + ([sd((1, dm), F32)] if d == 0 else [])
    out_specs = [row_spec] + [pl.BlockSpec(s.shape, lambda i, nd=len(s.shape): (0,) * nd) for s in out_shape[1:]]
    scratch = [pltpu.VMEM((rr + 2 * SUBLANES, sw), F32), pltpu.VMEM((rr + 2 * SUBLANES, sw), F32),
               pltpu.VMEM((rr, sw), F32), pltpu.VMEM((rr, sw), F32)]
    scratch += [pltpu.VMEM((SUBLANES, sw), F32)] * 6 + [pltpu.VMEM((SUBLANES, dm), F32)]
    vmem = 4 * (rr + 16) * sw * 4 + 10 * rr * dm * 4 + 24 * nk * LANES * swk * 4 + (12 << 20)
    return _call(body, name=name, xch=xch, args=ins, out_shape=out_shape, grid=(nc,), in_specs=in_specs,
                 out_specs=out_specs, scratch=scratch, params=_params(("arbitrary",), vmem))


def _hg_chunk(q, v, fraw, l0, l1, st, *, rev):
    nh = q.shape[1] // HG_HEAD
    lb = _sigmoid(l1 - l0)
    logf = jnp.logaddexp(jnp.log(lb), jnp.log1p(-lb) + jax.nn.log_sigmoid(fraw))
    kk = (1.0 - lb) * _sigmoid(fraw * -1.0)
    r = lax.broadcasted_iota(jnp.int32, (HG_ROWS, HG_ROWS), 0)
    c = lax.broadcasted_iota(jnp.int32, (HG_ROWS, HG_ROWS), 1)
    same = (r & 3) == (c & 3)
    tri = jnp.logical_and(same, ((c >> 2) >= (r >> 2)) if rev else ((c >> 2) <= (r >> 2)))
    hi = lax.Precision.HIGHEST
    bcum = _dot(tri.astype(F32), logf, NN, hi)
    bend = _dot(same.astype(F32), logf, NN, hi)
    r8 = lax.broadcasted_iota(jnp.int32, (SUBLANES, HG_ROWS), 0)
    c8 = lax.broadcasted_iota(jnp.int32, (SUBLANES, HG_ROWS), 1)
    bend8 = _dot(((c8 & 3) == r8).astype(F32), logf, NN, hi)
    r8d = lax.broadcasted_iota(jnp.int32, bend8.shape, 0)
    decs = [jnp.exp(jnp.sum(jnp.where(r8d == b, bend8, 0.0), axis=0, keepdims=True)) for b in range(LOCAL_B)]
    qd = (q * jnp.exp(bcum)).astype(BF16)
    kd = (kk * jnp.exp(-bcum)).astype(BF16)
    ke = (kk * jnp.exp(bend - bcum)).astype(BF16)
    rowb = lax.broadcasted_iota(jnp.int32, (HG_ROWS, 1), 0) & 3
    masks = [(rowb == b).astype(F32) for b in range(LOCAL_B)]
    outs, new = [], []
    for h in range(nh):
        sl = slice(h * HG_HEAD, (h + 1) * HG_HEAD)
        vh = v[:, sl]
        att = jnp.where(tri, _dot(qd[:, sl], kd[:, sl], NT), 0.0)
        o = _dot(att.astype(BF16), vh.astype(BF16))
        for b in range(LOCAL_B):
            sb = st[h * LOCAL_B + b]
            o = o + masks[b] * _dot(qd[:, sl], sb.astype(BF16), NT)
            new.append(sb * decs[b][:, sl] + _dot((vh * masks[b]).astype(BF16), ke[:, sl], TN))
        outs.append(o)
    return jnp.concatenate(outs, axis=1), tuple(new)


def _hg_chunk_of(step, ncc, nc, rev):
    return _s5_chunk_of(step, ncc, nc, rev)


def _hg_fwd_dir(zz, lb2, *, d, name, dims, xch=None):
    n = zz.shape[0]
    dm = zz.shape[1] // 5
    ns = (dm // HG_HEAD) * LOCAL_B
    nc, ncc = n // HG_ROWS, dims["nctx"] // HG_ROWS
    rev = d == 1
    ch = lambda i: _hg_chunk_of(i, ncc, nc, rev)

    def body(q_ref, v_ref, f_ref, l0_ref, l1_ref, o_ref, st_ref, st):
        @pl.when(pl.program_id(0) == 0)
        def _():
            st[...] = jnp.zeros(st.shape, F32)

        st_ref[...] = st[...]
        o, new = _hg_chunk(q_ref[...], v_ref[...], f_ref[...], l0_ref[...], l1_ref[...],
                           tuple(st[j] for j in range(ns)), rev=rev)
        o_ref[...] = o
        for j in range(ns):
            st[j] = new[j]

    blk = lambda off: pl.BlockSpec((HG_ROWS, dm), lambda i, off=off: (ch(i), off))
    lspec = lambda layer: pl.BlockSpec((None, None, 1, dm), lambda i, layer=layer: (d, layer, 0, 0))
    return _call(
        body, name=name, xch=xch, args=[zz, zz, zz, lb2, lb2],
        out_shape=[jax.ShapeDtypeStruct((n, dm), F32), jax.ShapeDtypeStruct((nc, ns, HG_HEAD, HG_HEAD), F32)],
        grid=(nc,),
        in_specs=[blk(0), blk(1), blk(2 + d), lspec(0), lspec(1)],
        out_specs=[pl.BlockSpec((HG_ROWS, dm), lambda i: (ch(i), 0)),
                   pl.BlockSpec((None, ns, HG_HEAD, HG_HEAD), lambda i: (ch(i), 0, 0, 0))],
        scratch=[pltpu.VMEM((ns, HG_HEAD, HG_HEAD), F32)],
        params=_params(("arbitrary",), 48 << 20))


def _hg_bwd_dir(zz, lb2, do, sts, dqv_prev, *, d, name, dims, xch=None):
    n = zz.shape[0]
    dm = zz.shape[1] // 5
    ns = (dm // HG_HEAD) * LOCAL_B
    nc, ncc = n // HG_ROWS, dims["nctx"] // HG_ROWS
    rev = d == 1
    ch = lambda i: _hg_chunk_of(nc - 1 - i, ncc, nc, rev)
    qv_dtype = F32 if d == 0 else BF16

    def body(*refs):
        q_ref, v_ref, f_ref, l0_ref, l1_ref, do_ref, st_ref = refs[:7]
        pos = 7
        if d == 1:
            dqp_ref, dvp_ref = refs[7:9]
            pos = 9
        dq_ref, dv_ref, df_ref, dl_ref, dst = refs[pos:]
        i = pl.program_id(0)

        @pl.when(i == 0)
        def _():
            dst[...] = jnp.zeros(dst.shape, F32)
            dl_ref[...] = jnp.zeros(dl_ref.shape, F32)

        _, vjp = jax.vjp(functools.partial(_hg_chunk, rev=rev), q_ref[...], v_ref[...], f_ref[...], l0_ref[...],
                         l1_ref[...], tuple(st_ref[j] for j in range(ns)))
        dq, dv, df, dl0, dl1, dstn = vjp((do_ref[...], tuple(dst[j] for j in range(ns))))
        for j in range(ns):
            dst[j] = dstn[j]
        if d == 1:
            dq = dq + dqp_ref[...]
            dv = dv + dvp_ref[...]
        dq_ref[...] = dq.astype(qv_dtype)
        dv_ref[...] = dv.astype(qv_dtype)
        df_ref[...] = df.astype(BF16)
        dl_ref[0] += dl0
        dl_ref[1] += dl1

    blk = lambda off: pl.BlockSpec((HG_ROWS, dm), lambda i, off=off: (ch(i), off))
    oblk = pl.BlockSpec((HG_ROWS, dm), lambda i: (ch(i), 0))
    lspec = lambda layer: pl.BlockSpec((None, None, 1, dm), lambda i, layer=layer: (d, layer, 0, 0))
    ins = [zz, zz, zz, lb2, lb2, do, sts] + (list(dqv_prev) if d == 1 else [])
    in_specs = [blk(0), blk(1), blk(2 + d), lspec(0), lspec(1), oblk,
                pl.BlockSpec((None, ns, HG_HEAD, HG_HEAD), lambda i: (ch(i), 0, 0, 0))]
    in_specs += [oblk, oblk] if d == 1 else []
    sd = jax.ShapeDtypeStruct
    return _call(
        body, name=name, xch=xch, args=ins,
        out_shape=[sd((n, dm), qv_dtype), sd((n, dm), qv_dtype), sd((n, dm), BF16), sd((2, 1, dm), F32)],
        grid=(nc,), in_specs=in_specs,
        out_specs=[oblk, oblk, oblk, pl.BlockSpec((2, 1, dm), lambda i: (0, 0, 0))],
        scratch=[pltpu.VMEM((ns, HG_HEAD, HG_HEAD), F32)],
        params=_params(("arbitrary",), 56 << 20))


def _hg_readout(o, g, w):
    outs = []
    for h in range(o.shape[-1] // HG_HEAD):
        sl = slice(h * HG_HEAD, (h + 1) * HG_HEAD)
        oh = o[..., sl]
        outs.append(oh * _rms(oh) * w * _sigmoid(g[..., sl]))
    return jnp.concatenate(outs, axis=-1)


def _silu(x):
    return x * _sigmoid(x)


def _mod_fwd(craw, w, b):
    def body(c_ref, w_ref, b_ref, o_ref):
        s = _silu(c_ref[...]).astype(BF16)
        for layer in range(w.shape[0]):
            o_ref[layer] = _dot(s, w_ref[layer].astype(BF16)) + b_ref[layer]

    return pl.pallas_call(body, name="mod_fwd",
                          out_shape=jax.ShapeDtypeStruct((w.shape[0], craw.shape[0], w.shape[2]), F32),
                          compiler_params=_params(None, 40 << 20))(craw, w, b)


def _mod_bwd(craw, w, dlat_sh, dctx_sh, dlat_full, dctx_full):
    nl, dm, ns = w.shape
    nb = dlat_sh.shape[1]

    def body(c_ref, w_ref, dl_ref, dc_ref, dlf_ref, dcf_ref, dw_ref, db_ref, dcc_ref):
        craw_v = c_ref[...]
        s = _silu(craw_v)
        s_lat = s[:nb].astype(BF16)
        s_ctx = s[nb:].astype(BF16)
        row = lax.broadcasted_iota(jnp.int32, (SUBLANES, ns), 0)
        dsc = jnp.zeros((SUBLANES, dm), F32)
        for layer in range(nl):
            tot = dc_ref[0, pl.ds(layer, 1), :]
            totf = dcf_ref[0, pl.ds(layer, 1), :]
            for i in range(1, NDEV):
                tot = tot + dc_ref[i, pl.ds(layer, 1), :]
                totf = totf + dcf_ref[i, pl.ds(layer, 1), :]
            dc8 = jnp.where(row == 0, jnp.broadcast_to(tot, (SUBLANES, ns)), 0.0).astype(BF16)
            dw_ref[layer] = _dot(s_lat, dl_ref[layer].astype(BF16), TN) + _dot(s_ctx, dc8, TN)
            db_ref[layer] = jnp.sum(dlf_ref[layer], axis=0, keepdims=True) + totf
            dsc = dsc + _dot(dc8, w_ref[layer].astype(BF16), NT)
        cc = craw_v[nb:]
        sg = _sigmoid(cc)
        dcc_ref[...] = dsc * (sg * (1.0 + cc * (1.0 - sg)))

    sd = jax.ShapeDtypeStruct
    return pl.pallas_call(body, name="mod_bwd",
                          out_shape=[sd((nl, dm, ns), F32), sd((nl, 1, dlat_full.shape[2]), F32), sd((SUBLANES, dm), F32)],
                          compiler_params=_params(None, 48 << 20))(craw, w, dlat_sh, dctx_sh, dlat_full, dctx_full)


def _adam_rows(r):
    best = None
    for t in range(2 * SUBLANES, min(r, 128) + 1, 2 * SUBLANES):
        if r % t == 0:
            best = t
    return best if best is not None else r


def _adamw(parts, w, m, v, *, name):
    npart, r, c = parts.shape
    tr = _adam_rows(r)

    def body(p_ref, w_ref, m_ref, v_ref, g_ref, d_ref, nm_ref, nv_ref):
        g = p_ref[0].astype(F32)
        for i in range(1, npart):
            g = g + p_ref[i].astype(F32)
        nm = ADAM_B1 * m_ref[...] + (1.0 - ADAM_B1) * g
        nv = ADAM_B2 * v_ref[...] + (1.0 - ADAM_B2) * (g * g)
        m_hat = nm / (1.0 - ADAM_B1 ** ADAM_STEP)
        v_hat = nv / (1.0 - ADAM_B2 ** ADAM_STEP)
        g_ref[...] = g
        d_ref[...] = -ADAM_LR * (m_hat / (jnp.sqrt(v_hat) + ADAM_EPS) + ADAM_WD * w_ref[...])
        nm_ref[...] = nm
        nv_ref[...] = nv

    spec = pl.BlockSpec((tr, c), lambda i: (i, 0))
    vmem = 2 * (npart + 7) * tr * c * 4 + (8 << 20)
    return pl.pallas_call(
        body, name=name, out_shape=[jax.ShapeDtypeStruct((r, c), F32)] * 4, grid=(r // tr,),
        in_specs=[pl.BlockSpec((npart, tr, c), lambda i: (0, i, 0)), spec, spec, spec], out_specs=[spec] * 4,
        compiler_params=_params(("parallel",), vmem),
    )(parts, w, m, v)


def _pack(arrs, width):
    rows, sizes = [], []
    for a in arrs:
        f = a.reshape(-1).astype(F32)
        nr = -(-f.shape[0] // width)
        rows.append(jnp.pad(f, (0, nr * width - f.shape[0])).reshape(nr, width))
        sizes.append(nr)
    tot = sum(sizes)
    pad = -tot % (2 * SUBLANES)
    if pad:
        rows.append(jnp.zeros((pad, width), F32))
    return jnp.concatenate(rows, axis=0), sizes


def _unpack(buf, sizes, shapes):
    out, r0 = [], 0
    for nr, shp in zip(sizes, shapes):
        cnt = 1
        for s in shp:
            cnt *= s
        out.append(buf[r0:r0 + nr].reshape(-1)[:cnt].reshape(shp))
        r0 += nr
    return out


def _to_tm(a):
    return jnp.transpose(a, (1, 0, 2)).reshape(a.shape[1] * a.shape[0], a.shape[2])


def _pattern(mod_lat, mod_ctx, m, dm):
    lat = mod_lat[:, m * dm:(m + 1) * dm]
    ctx = jnp.broadcast_to(mod_ctx[None, m * dm:(m + 1) * dm], (SUBLANES, dm))
    return jnp.stack([ctx, jnp.concatenate([lat, lat], axis=0)])


def _blockdiag_b(bt, nk):
    g, h, p = bt.shape
    t = bt.reshape(nk, 8, h, p)
    return jnp.einsum("kghp,gj->kghjp", t, jnp.eye(8, dtype=bt.dtype)).reshape(nk, 8 * h, 8 * p)


def _blockdiag_c(ct, nk):
    g, h, p = ct.shape
    t = ct.reshape(nk, 8, h, p)
    return jnp.einsum("kghp,gj->kgpjh", t, jnp.eye(8, dtype=ct.dtype)).reshape(nk, 8 * p, 8 * h)


def _diag_b(dbb, h, p):
    nk = dbb.shape[0]
    return jnp.einsum("kghgp->kghp", dbb.reshape(nk, 8, h, 8, p)).reshape(nk * 8, h, p)


def _diag_c(dcc, h, p):
    nk = dcc.shape[0]
    return jnp.einsum("kgpgh->kghp", dcc.reshape(nk, 8, p, 8, h)).reshape(nk * 8, h, p)


def kernel(x, c, ctx, c_ctx, w_mod, b_mod, norm1_w, norm2_w, final_norm_w, s5_w_in, s5_lam_re, s5_lam_im, s5_log_step, s5_b_re, s5_b_im, s5_c_re, s5_c_im, s5_d, s5_w_glu, s5_w_out, hg_w_in, hg_lower_bounds, hg_gnorm_w, hg_w_out, ffn_w_up, ffn_conv_w, ffn_conv_b, ffn_w_down, loss_target, m_c_ctx, m_w_mod, m_b_mod, m_norm1_w, m_norm2_w, m_final_norm_w, m_s5_w_in, m_s5_lam_re, m_s5_lam_im, m_s5_log_step, m_s5_b_re, m_s5_b_im, m_s5_c_re, m_s5_c_im, m_s5_d, m_s5_w_glu, m_s5_w_out, m_hg_w_in, m_hg_lower_bounds, m_hg_gnorm_w, m_hg_w_out, m_ffn_w_up, m_ffn_conv_w, m_ffn_conv_b, m_ffn_w_down, v_c_ctx, v_w_mod, v_b_mod, v_norm1_w, v_norm2_w, v_final_norm_w, v_s5_w_in, v_s5_lam_re, v_s5_lam_im, v_s5_log_step, v_s5_b_re, v_s5_b_im, v_s5_c_re, v_s5_c_im, v_s5_d, v_s5_w_glu, v_s5_w_out, v_hg_w_in, v_hg_lower_bounds, v_hg_gnorm_w, v_hg_w_out, v_ffn_w_up, v_ffn_conv_w, v_ffn_conv_b, v_ffn_w_down):
    given = dict(locals())
    bsz, lx, dm = x.shape
    lc = ctx.shape[1]
    assert bsz == LOCAL_B and w_mod.shape[0] == 2 and dm % LANES == 0
    n, nctx = (lc + lx) * bsz, lc * bsz
    ngrp, nstate, hgrp = dm // S5_GROUP, S5_STATE, S5_GROUP
    nk = dm // LANES
    dims = dict(nctx=nctx, tm=min(512, nctx), tm_row=min(256, nctx), s5_rows=min(256, nctx),
                s5_lane_block=min(512, 8 * nstate))
    tm = dims["tm"]
    assert nctx % HG_ROWS == 0 and (lx * bsz) % nctx == 0 and lc % GRID_W == 0 and lc & (lc - 1) == 0
    me = 4 * lax.axis_index("x") + 2 * lax.axis_index("y") + lax.axis_index("c")

    gath = _exchange([given[k].astype(BF16) for k in ("s5_w_in", "s5_w_glu", "s5_w_out")]
                     + [c, hg_lower_bounds, ffn_conv_w], a2a=False, name="gather_weights")
    w_s5in, w_glu, w_s5out = (g.reshape(1, dm, dm) for g in gath[:3])
    c_all, lb_all, cw_all = gath[3:]
    ns_up = ffn_w_up.shape[2]
    w_up, w_dn = [None, None], [None, None]
    gather = lambda arrs: _Xchg([a.astype(BF16) for a in arrs], [False] * len(arrs))
    scatter = lambda arrs: _Xchg(arrs, [True] * len(arrs))
    cw =[cw_all[:, layer].reshape(2, NDEV // 2, 3, ns_up) for layer in range(2)]
    cb = [ffn_conv_b[layer].reshape(2, NDEV // 2, 1, ns_up) for layer in range(2)]
    lb2 = jnp.transpose(lb_all, (1, 2, 0, 3)).reshape(2, 2, 1, dm)

    nsm = w_mod.shape[2]
    craw = jnp.concatenate([c_all.reshape(NDEV * bsz, dm), c_ctx[None], jnp.zeros((SUBLANES - 1, dm), F32)], axis=0)
    b_sh = lax.dynamic_slice(b_mod, (0, me * nsm), (2, nsm)).reshape(2, 1, nsm)
    mod_sh = _mod_fwd(craw, w_mod, b_sh)
    (mod_g,) = _exchange([mod_sh], a2a=False, name="gather_mod")
    mod_full = jnp.transpose(mod_g, (1, 2, 0, 3)).reshape(2, craw.shape[0], NDEV * nsm)
    pat = []
    for layer in range(2):
        mlat = lax.dynamic_slice(mod_full[layer], (me * bsz, 0), (bsz, N_MOD * dm))
        mctx = mod_full[layer, NDEV * bsz]
        pat.append([_pattern(mlat, mctx, m, dm) for m in range(N_MOD)])

    lr4 = s5_lam_re[0].reshape(2, ngrp, 1, nstate)
    li4 = s5_lam_im[0].reshape(2, ngrp, 1, nstate)
    ls4 = s5_log_step[0].reshape(2, ngrp, 1, 1)
    brt = jnp.transpose(s5_b_re[0], (0, 1, 3, 2))
    bit = jnp.transpose(s5_b_im[0], (0, 1, 3, 2))
    abar_r, abar_i, bbar_r, bbar_i = _s5_disc_fwd(lr4, li4, ls4, brt, bit)
    sw = ngrp * nstate
    a8r = [jnp.broadcast_to(abar_r[d].reshape(1, sw), (SUBLANES, sw)) for d in range(2)]
    a8i = [jnp.broadcast_to(abar_i[d].reshape(1, sw), (SUBLANES, sw)) for d in range(2)]
    bbr = [_blockdiag_b(bbar_r[d], nk).astype(BF16) for d in range(2)]
    bbi = [_blockdiag_b(bbar_i[d], nk).astype(BF16) for d in range(2)]
    ccr = [_blockdiag_c(s5_c_re[0, d], nk).astype(BF16) for d in range(2)]
    cci = [_blockdiag_c(s5_c_im[0, d], nk).astype(BF16) for d in range(2)]
    dsk = s5_d.reshape(1, dm)

    z0 = jnp.concatenate([_to_tm(ctx), _to_tm(x)], axis=0)
    tgt = _to_tm(loss_target)
    n1w = [norm1_w[layer].reshape(1, dm) for layer in range(2)]
    n2w = [norm2_w[layer].reshape(1, dm) for layer in range(2)]

    def ffn_fwd(layer, h2):
        u = _mm_fwd(h2, w_up[layer], name=f"ffn_up{layer}", tm=tm, shard_out=True)
        u4 = u.reshape(2, NDEV // 2, n, ns_up)
        hm = _convffn_fwd(u4, cw[layer], cb[layer], name=f"convffn_fwd{layer}", dims=dims)
        f = _mm_down_fwd(hm, w_dn[layer], name=f"ffn_down{layer}", tm=tm)
        return u4, hm, f

    _, h0 = _norm_mod_fwd(z0, n1w[0], pat[0][0], pat[0][1], name="norm1_l0", dims=dims)
    u_s5 = _mm_fwd(h0, w_s5in, name="s5_in", tm=tm)
    (y_a, st0r, st0i), (g_up0, g_dn0) = _s5_fwd_dir(
        u_s5, None, a8r[0], a8i[0], bbr[0], bbi[0], ccr[0], cci[0], dsk, d=0, name="s5_fwd_d0", dims=dims,
        xch=gather([ffn_w_up[0], ffn_w_down[0]]))
    w_up[0], w_dn[0] = g_up0, g_dn0.reshape(NDEV // 2, -1, dm)
    (y_s5, st1r, st1i), (g_hgin, g_hgout, g_dn1) = _s5_fwd_dir(
        u_s5, y_a, a8r[1], a8i[1], bbr[1], bbi[1], ccr[1], cci[1], dsk, d=1, name="s5_fwd_d1", dims=dims,
        xch=gather([hg_w_in[0], hg_w_out[0], ffn_w_down[1]]))
    w_hgin, w_hgout, w_dn[1] = g_hgin, g_hgout.reshape(1, dm, dm), g_dn1.reshape(NDEV // 2, -1, dm)
    (zg,) = _rowk(lambda rv, pv, cv, il: ([_gelu(rv[0])], [], []), name="s5_gelu", n=n, tm=dims["tm_row"], nctx=nctx,
                  rows=[(y_s5, dm, 0, 0)], out_rows=[(dm, BF16, dm, 0)])
    t_glu = _mm_fwd(zg, w_glu, name="s5_glu", tm=tm)
    (z2g,) = _rowk(lambda rv, pv, cv, il: ([rv[0] * _sigmoid(rv[1])], [], []), name="s5_gate", n=n,
                   tm=dims["tm_row"], nctx=nctx, rows=[(zg, dm, 0, 0), (t_glu, dm, 0, 0)],
                   out_rows=[(dm, BF16, dm, 0)])
    ymix0 = _mm_fwd(z2g, w_s5out, name="s5_out", tm=tm)
    z1_l0, h2_l0 = _norm_mod_fwd(z0, n2w[0], pat[0][3], pat[0][4], name="norm2_l0", dims=dims,
                                 res=(ymix0, pat[0][2]))
    u4_l0, hm_l0, f_l0 = ffn_fwd(0, h2_l0)

    z2_l0, h1 = _norm_mod_fwd(z1_l0, n1w[1], pat[1][0], pat[1][1], name="norm1_l1", dims=dims,
                              res=(f_l0, pat[0][5]))
    zz = _mm_fwd(h1, w_hgin, name="hg_in", tm=tm)
    (o_f, sts_f), (w_up[1],) = _hg_fwd_dir(zz, lb2, d=0, name="hg_fwd_d0", dims=dims, xch=gather([ffn_w_up[1]]))
    (o_b, sts_b), _ = _hg_fwd_dir(zz, lb2, d=1, name="hg_fwd_d1", dims=dims)
    gnw = hg_gnorm_w.reshape(1, HG_HEAD)
    (og,) = _rowk(lambda rv, pv, cv, il: ([_hg_readout(rv[0] + rv[1], rv[2], cv[0])], [], []), name="hg_readout",
                  n=n, tm=dims["tm_row"], nctx=nctx, rows=[(o_f, dm, 0, 0), (o_b, dm, 0, 0), (zz, dm, 4, 0)],
                  consts=[gnw], out_rows=[(dm, BF16, dm, 0)])
    ymix1 = _mm_fwd(og, w_hgout, name="hg_out", tm=tm)
    z1_l1, h2_l1 = _norm_mod_fwd(z2_l0, n2w[1], pat[1][3], pat[1][4], name="norm2_l1", dims=dims,
                                 res=(ymix1, pat[1][2]))
    u4_l1, hm_l1, f_l1 = ffn_fwd(1, h2_l1)

    dz, df, dgate2_l1, loss_part, dfinal_w = _loss_bwd(z1_l1, f_l1, pat[1][5], tgt, final_norm_w.reshape(1, dm),
                                                        name="loss_bwd", dims=dims)

    def ffn_bwd(layer, df_, u4, hm, h2, xch=None):
        dhm = _mm_down_bwd_in(df_, w_dn[layer], name=f"ffn_down_bwd_in{layer}", tm=tm)
        dwd = _mm_down_bwd_w(hm, df_, name=f"ffn_down_bwd_w{layer}", tm=tm)
        (du4, dcw, dcb), got = _convffn_bwd(u4, dhm, cw[layer], cb[layer], name=f"convffn_bwd{layer}", dims=dims,
                                            xch=xch)
        du = du4.reshape(NDEV, n, ns_up)
        dh2 = _mm_bwd_in(du, w_up[layer], name=f"ffn_up_bwd_in{layer}", tm=tm, shard_in=True)
        dwu = _mm_bwd_w(h2, du, name=f"ffn_up_bwd_w{layer}", tm=tm, s=NDEV, ns=ns_up, shard_in=True)
        return dh2, dwu, dwd, dcw, dcb, got

    dh2, dwu_l1, dwd_l1, dcw_l1, dcb_l1, _ = ffn_bwd(1, df, u4_l1, hm_l1, h2_l1)
    dz, dymix, dsh2_l1, dsc2_l1, dgate1_l1, dn2w_l1 = _norm_mod_bwd(dh2, z1_l1, dz, n2w[1], pat[1][4], name="norm2_bwd_l1",
                                                                    dims=dims, res=(ymix1, pat[1][2]))
    dog = _mm_bwd_in(dymix, w_hgout, name="hg_out_bwd_in", tm=tm)
    dw_hgout = _mm_bwd_w(og, dymix, name="hg_out_bwd_w", tm=tm, s=1, ns=dm)

    def readout_bwd(rv, pv, cv, il):
        _, vjp = jax.vjp(_hg_readout, rv[0] + rv[1], rv[2], cv[0])
        do, dg, dw = vjp(rv[3])
        return [do, dg], [], [jnp.broadcast_to(dw, (SUBLANES, HG_HEAD)) * (1.0 / SUBLANES)]

    do, dg, dgnw = _rowk(readout_bwd, name="hg_readout_bwd", n=n, tm=dims["tm_row"], nctx=nctx,
                         rows=[(o_f, dm, 0, 0), (o_b, dm, 0, 0), (zz, dm, 4, 0), (dog, dm, 0, 0)], consts=[gnw],
                         out_rows=[(dm, F32, dm, 0), (dm, BF16, dm, 0)], out_acc=[HG_HEAD])
    (dq0, dv0, dff, dl_f), (p_up1, p_dn1) = _hg_bwd_dir(
        zz, lb2, do, sts_f, None, d=0, name="hg_bwd_d0", dims=dims,
        xch=scatter([dwu_l1, dwd_l1.reshape(NDEV, -1, dm)]))
    (dq, dv, dfb, dl_b), (p_hgout,) = _hg_bwd_dir(
        zz, lb2, do, sts_b, (dq0, dv0), d=1, name="hg_bwd_d1", dims=dims,
        xch=scatter([dw_hgout.reshape(NDEV, -1, dm)]))
    dzz = jnp.concatenate([dq, dv, dff, dfb, dg], axis=1)
    dh1 = _mm_bwd_in(dzz, w_hgin, name="hg_in_bwd_in", tm=tm)
    dw_hgin = _mm_bwd_w(h1, dzz, name="hg_in_bwd_w", tm=tm, s=NDEV, ns=w_hgin.shape[2])
    dz, df0, dsh1_l1, dsc1_l1, dgate2_l0, dn1w_l1 = _norm_mod_bwd(dh1, z2_l0, dz, n1w[1], pat[1][1], name="norm1_bwd_l1",
                                                                  dims=dims, res=(f_l0, pat[0][5]))
    dh2, dwu_l0, dwd_l0, dcw_l0, dcb_l0, (p_hgin,) = ffn_bwd(0, df0, u4_l0, hm_l0, h2_l0, xch=scatter([dw_hgin]))
    dz, dymix, dsh2_l0, dsc2_l0, dgate1_l0, dn2w_l0 = _norm_mod_bwd(dh2, z1_l0, dz, n2w[0], pat[0][4], name="norm2_bwd_l0",
                                                                    dims=dims, res=(ymix0, pat[0][2]))
    dz2g = _mm_bwd_in(dymix, w_s5out, name="s5_out_bwd_in", tm=tm)
    dw_s5out = _mm_bwd_w(z2g, dymix, name="s5_out_bwd_w", tm=tm, s=1, ns=dm)

    def gate_bwd(rv, pv, cv, il):
        sg = _sigmoid(rv[1])
        return [rv[2] * rv[0] * sg * (1.0 - sg), rv[2] * sg], [], []

    dt_glu, dzg_a = _rowk(gate_bwd, name="s5_gate_bwd", n=n, tm=dims["tm_row"], nctx=nctx,
                          rows=[(zg, dm, 0, 0), (t_glu, dm, 0, 0), (dz2g, dm, 0, 0)],
                          out_rows=[(dm, BF16, dm, 0), (dm, F32, dm, 0)])
    dzg_b = _mm_bwd_in(dt_glu, w_glu, name="s5_glu_bwd_in", tm=tm)
    dw_glu = _mm_bwd_w(zg, dt_glu, name="s5_glu_bwd_w", tm=tm, s=1, ns=dm)

    def gelu_bwd(rv, pv, cv, il):
        _, vjp = jax.vjp(_gelu, rv[0])
        return [vjp(rv[1] + rv[2])[0]], [], []

    (dy_s5,) = _rowk(gelu_bwd, name="s5_gelu_bwd", n=n, tm=dims["tm_row"], nctx=nctx,
                     rows=[(y_s5, dm, 0, 0), (dzg_a, dm, 0, 0), (dzg_b, dm, 0, 0)], out_rows=[(dm, F32, dm, 0)])
    dcw_both = jnp.stack([dcw_l0.reshape(NDEV, 3, ns_up), dcw_l1.reshape(NDEV, 3, ns_up)], axis=1)
    (du_a, dbbr0, dbbi0, dccr0, dcci0, dar0, dai0, ddsk), (p_up0, p_dn0, p_cw) = _s5_bwd_dir(
        u_s5, dy_s5, None, a8r[0], a8i[0], bbr[0], bbi[0], ccr[0], cci[0], dsk, st0r, st0i, d=0, name="s5_bwd_d0",
        dims=dims, xch=scatter([dwu_l0, dwd_l0.reshape(NDEV, -1, dm), dcw_both]))
    (du_s5, dbbr1, dbbi1, dccr1, dcci1, dar1, dai1), (p_s5out, p_glu) = _s5_bwd_dir(
        u_s5, dy_s5, du_a, a8r[1], a8i[1], bbr[1], bbi[1], ccr[1], cci[1], dsk, st1r, st1i, d=1, name="s5_bwd_d1",
        dims=dims, xch=scatter([dw_s5out.reshape(NDEV, -1, dm), dw_glu.reshape(NDEV, -1, dm)]))
    dh0 = _mm_bwd_in(du_s5, w_s5in, name="s5_in_bwd_in", tm=tm)
    dw_s5in = _mm_bwd_w(h0, du_s5, name="s5_in_bwd_w", tm=tm, s=1, ns=dm)
    dz0, dsh1_l0, dsc1_l0, dn1w_l0 = _norm_mod_bwd(dh0, z0, dz, n1w[0], pat[0][1], name="norm1_bwd_l0", dims=dims)

    dar = jnp.stack([dar0, dar1]).reshape(2, ngrp, 1, nstate)
    dai = jnp.stack([dai0, dai1]).reshape(2, ngrp, 1, nstate)
    dbbr = jnp.stack([_diag_b(dbbr0, hgrp, nstate), _diag_b(dbbr1, hgrp, nstate)])
    dbbi = jnp.stack([_diag_b(dbbi0, hgrp, nstate), _diag_b(dbbi1, hgrp, nstate)])
    dlr, dli, dls, dbrt, dbit = _s5_disc_bwd(lr4, li4, ls4, brt, bit, dar, dai, dbbr, dbbi)
    g_c_re = jnp.stack([_diag_c(dccr0, hgrp, nstate), _diag_c(dccr1, hgrp, nstate)])
    g_c_im = jnp.stack([_diag_c(dcci0, hgrp, nstate), _diag_c(dcci1, hgrp, nstate)])

    dmod = jnp.stack([
        jnp.concatenate([dsh1_l0, dsc1_l0, dgate1_l0, dsh2_l0, dsc2_l0, dgate2_l0], axis=1),
        jnp.concatenate([dsh1_l1, dsc1_l1, dgate1_l1, dsh2_l1, dsc2_l1, dgate2_l1], axis=1)])
    dmod_g, p_s5in = _exchange([dmod, dw_s5in.reshape(NDEV, -1, dm)], a2a=[False, True], name="gather_dmod")
    dlat_full = jnp.transpose(dmod_g[:, :, :bsz], (1, 0, 2, 3)).reshape(2, NDEV * bsz, N_MOD * dm)
    dctx_full = dmod_g[:, :, bsz]
    dlat_sh = lax.dynamic_slice(dlat_full, (0, 0, me * nsm), (2, NDEV * bsz, nsm))
    dctx_sh = lax.dynamic_slice(dctx_full, (0, 0, me * nsm), (NDEV, 2, nsm))
    g_w_mod, g_b_mod, dcctx8 = _mod_bwd(craw, w_mod, dlat_sh, dctx_sh, dlat_full, dctx_full)

    dl_hg = jnp.stack([dl_f[:, 0], dl_b[:, 0]])
    small_names = ["c_ctx", "norm1_w", "norm2_w", "final_norm_w", "s5_lam_re", "s5_lam_im", "s5_log_step", "s5_b_re",
                   "s5_b_im", "s5_c_re", "s5_c_im", "s5_d", "hg_gnorm_w", "ffn_conv_b"]
    small_grads = [dcctx8[0], jnp.concatenate([dn1w_l0, dn1w_l1]), jnp.concatenate([dn2w_l0, dn2w_l1]), dfinal_w,
                   dlr, dli, dls, jnp.transpose(dbrt, (0, 1, 3, 2)), jnp.transpose(dbit, (0, 1, 3, 2)), g_c_re, g_c_im,
                   ddsk, dgnw, jnp.stack([dcb_l0.reshape(-1), dcb_l1.reshape(-1)])]
    width = 8 * LANES
    gbuf, sizes = _pack(small_grads + [dl_hg, loss_part], width)
    (gbuf_g,) = _exchange([gbuf], a2a=False, name="gather_small")
    nsm_rows = gbuf.shape[0]
    wbuf, _ = _pack([given[k] for k in small_names] + [jnp.zeros_like(dl_hg), jnp.zeros_like(loss_part)], width)
    mbuf, _ = _pack([given["m_" + k] for k in small_names] + [jnp.zeros_like(dl_hg), jnp.zeros_like(loss_part)], width)
    vbuf, _ = _pack([given["v_" + k] for k in small_names] + [jnp.zeros_like(dl_hg), jnp.zeros_like(loss_part)], width)
    sg, sdl, snm, snv = _adamw(gbuf_g, wbuf, mbuf, vbuf, name="adamw_small")
    shapes = [given[k].shape for k in small_names] + [dl_hg.shape, loss_part.shape]
    sg_l, sdl_l, snm_l, snv_l = (_unpack(b, sizes, shapes) for b in (sg, sdl, snm, snv))
    res = {k: (sg_l[i], sdl_l[i], snm_l[i], snv_l[i]) for i, k in enumerate(small_names)}
    loss = jnp.sum(sg_l[-1])
    dl_tot = sg_l[-2]
    nlb = hg_lower_bounds.shape[2]
    g_lb = lax.dynamic_slice(dl_tot, (0, 0, me * nlb), (2, 2, nlb))

    def adam_local(name, g, shape2):
        w, m, v = given[name], given["m_" + name], given["v_" + name]
        out = _adamw(g.reshape((1,) + shape2), w.reshape(shape2), m.reshape(shape2), v.reshape(shape2),
                     name="adamw_" + name)
        return tuple(o.reshape(w.shape) for o in out)

    def adam_parts(name, p):
        w, m, v = given[name], given["m_" + name], given["v_" + name]
        shape2 = (p.shape[0], -1, w.shape[-1])
        p3 = p.reshape(shape2)
        s2 = p3.shape[1:]
        out = _adamw(p3, w.reshape(s2), m.reshape(s2), v.reshape(s2), name="adamw_" + name)
        return tuple(o.reshape(w.shape) for o in out)

    res["hg_lower_bounds"] = adam_local("hg_lower_bounds", g_lb, (2 * 2, nlb))
    res["w_mod"] = adam_local("w_mod", g_w_mod, (2 * dm, nsm))
    res["b_mod"] = adam_local("b_mod", g_b_mod, (2, N_MOD * dm))
    res["s5_w_in"] = adam_parts("s5_w_in", p_s5in)
    res["s5_w_glu"] = adam_parts("s5_w_glu", p_glu)
    res["s5_w_out"] = adam_parts("s5_w_out", p_s5out)
    res["hg_w_in"] = adam_parts("hg_w_in", p_hgin)
    res["hg_w_out"] = adam_parts("hg_w_out", p_hgout)
    res["ffn_w_up"] = adam_parts("ffn_w_up", jnp.stack([p_up0, p_up1], axis=1))
    res["ffn_w_down"] = adam_parts("ffn_w_down", jnp.stack([p_dn0, p_dn1], axis=1))
    res["ffn_conv_w"] = adam_parts("ffn_conv_w", p_cw)

    grad_x = jnp.transpose(dz0[nctx:].reshape(lx, bsz, dm), (1, 0, 2))
    order = ["c_ctx", "w_mod", "b_mod", "norm1_w", "norm2_w", "final_norm_w", "s5_w_in", "s5_lam_re", "s5_lam_im",
             "s5_log_step", "s5_b_re", "s5_b_im", "s5_c_re", "s5_c_im", "s5_d", "s5_w_glu", "s5_w_out", "hg_w_in",
             "hg_lower_bounds", "hg_gnorm_w", "hg_w_out", "ffn_w_up", "ffn_conv_w", "ffn_conv_b", "ffn_w_down"]
    outs = [loss, grad_x]
    for j in range(4):
        outs += [res[k][j].reshape(given[k].shape) for k in order]
    return tuple(outs)
```

```python
import functools

import jax
import jax.numpy as jnp
from jax import lax
from jax.experimental import pallas as pl
from jax.experimental.pallas import tpu as pltpu

F32 = jnp.float32
BF16 = jnp.bfloat16
NDEV = 8
LOCAL_B = 4
NORM_EPS = 1e-6
N_MOD = 6
S5_GROUP = 16
S5_STATE = 64
S5_LAM_RE_MAX = -1e-4
HG_HEAD = 128
HG_ROWS = 128
GRID_W = 64
ADAM_LR, ADAM_B1, ADAM_B2, ADAM_EPS, ADAM_WD, ADAM_STEP = 0.001, 0.9, 0.999, 1e-08, 0.01, 10
VMEM_BYTES_V7X = 64 * 1024 * 1024
LANES = 128
SUBLANES = 8

NN = (((1,), (0,)), ((), ()))
NT = (((1,), (1,)), ((), ()))
TN = (((0,), (0,)), ((), ()))
MESH = pl.DeviceIdType.MESH


def _params(sem=None, vmem=None):
    kw = {}
    if sem is not None:
        kw["dimension_semantics"] = sem
    if vmem is not None:
        kw["vmem_limit_bytes"] = int(min(vmem, VMEM_BYTES_V7X - (4 << 20)))
    return pltpu.CompilerParams(**kw)


def _nbytes(shape, dtype):
    n = 1
    for s in shape:
        n *= 1 if s is None else s
    return n * jnp.dtype(dtype).itemsize


def _dot(a, b, dims=NN, precision=None):
    return lax.dot_general(a, b, dims, preferred_element_type=F32, precision=precision)


def _sigmoid(x):
    return 1.0 / (1.0 + jnp.exp(-x))


class _Xchg:
    def __init__(self, arrs, a2a):
        self.arrs, self.a2a, self.n = list(arrs), list(a2a), len(arrs)

    def out_shape(self):
        return [jax.ShapeDtypeStruct(a.shape if f else (NDEV,) + a.shape, a.dtype) for a, f in zip(self.arrs, self.a2a)]

    def scratch(self):
        return [pltpu.SemaphoreType.DMA((self.n * (NDEV - 1),)), pltpu.SemaphoreType.DMA((self.n * (NDEV - 1),)),
                pltpu.SemaphoreType.DMA((self.n,))]

    def _copies(self, ins, outs, sems, with_recvs):
        send_sems, recv_sems, loc_sems = sems
        x, y, c = lax.axis_index("x"), lax.axis_index("y"), lax.axis_index("c")
        me = 4 * x + 2 * y + c
        local, sends, recvs = [], [], []
        for a in range(self.n):
            src = ins[a].at[me] if self.a2a[a] else ins[a]
            local.append(pltpu.make_async_copy(src, outs[a].at[me], loc_sems.at[a]))
            for k in range(1, NDEV):
                px = (1 - x) if (k >> 2) & 1 else x
                py = (1 - y) if (k >> 1) & 1 else y
                pc = (1 - c) if k & 1 else c
                p = 4 * px + 2 * py + pc
                s = a * (NDEV - 1) + k - 1
                src = ins[a].at[p] if self.a2a[a] else ins[a]
                kw = dict(src_ref=src, send_sem=send_sems.at[s], recv_sem=recv_sems.at[s], device_id=(px, py, pc),
                          device_id_type=MESH)
                sends.append(pltpu.make_async_remote_copy(dst_ref=outs[a].at[me], **kw))
                if with_recvs:
                    recvs.append(pltpu.make_async_remote_copy(dst_ref=outs[a].at[p], **kw))
        return local, sends, recvs

    def start(self, ins, outs, sems):
        local, sends, _ = self._copies(ins, outs, sems, False)
        for cp in local + sends:
            cp.start()

    def wait(self, ins, outs, sems):
        local, sends, recvs = self._copies(ins, outs, sems, True)
        for cp in sends:
            cp.wait_send()
        for cp in recvs:
            cp.wait_recv()
        for cp in local:
            cp.wait()


def _exchange(arrs, *, a2a, name):
    xch = _Xchg(arrs, a2a if isinstance(a2a, (list, tuple)) else [a2a] * len(arrs))
    n = xch.n

    def body(*refs):
        xch.start(refs[:n], refs[n:2 * n], refs[2 * n:])
        xch.wait(refs[:n], refs[n:2 * n], refs[2 * n:])

    res = pl.pallas_call(
        body, name=name, out_shape=xch.out_shape(),
        in_specs=[pl.BlockSpec(memory_space=pl.ANY)] * n, out_specs=[pl.BlockSpec(memory_space=pl.ANY)] * n,
        scratch_shapes=xch.scratch(),
    )(*arrs)
    return list(res)


def _call(body, *, name, out_shape, grid, in_specs, out_specs, scratch, params, args, xch=None):
    in_specs, out_specs, out_shape, scratch, args = list(in_specs), list(out_specs), list(out_shape), list(scratch), list(args)
    n_in, n_out, n_scr = len(in_specs), len(out_shape), len(scratch)
    if xch is not None:
        k = xch.n
        inner = body

        def body(*refs):
            ins, xin = refs[:n_in], refs[n_in:n_in + k]
            outs, xout = refs[n_in + k:n_in + k + n_out], refs[n_in + k + n_out:n_in + 2 * k + n_out]
            scr = refs[n_in + 2 * k + n_out:n_in + 2 * k + n_out + n_scr]
            sems = refs[n_in + 2 * k + n_out + n_scr:]
            first = pl.program_id(0) == 0
            last = pl.program_id(0) == grid[0] - 1
            for ax in range(1, len(grid)):
                first = jnp.logical_and(first, pl.program_id(ax) == 0)
                last = jnp.logical_and(last, pl.program_id(ax) == grid[ax] - 1)

            @pl.when(first)
            def _():
                xch.start(xin, xout, sems)

            inner(*ins, *outs, *scr)

            @pl.when(last)
            def _():
                xch.wait(xin, xout, sems)

        anyspec = pl.BlockSpec(memory_space=pl.ANY)
        in_specs += [anyspec] * k
        out_specs += [anyspec] * k
        out_shape += xch.out_shape()
        scratch += xch.scratch()
        args += xch.arrs
    res = pl.pallas_call(body, name=name, out_shape=out_shape, grid=grid, in_specs=in_specs, out_specs=out_specs,
                         scratch_shapes=scratch, compiler_params=params)(*args)
    return list(res[:n_out]), list(res[n_out:])


def _mm(a, b, *, name, grid, a_spec, b_spec, o_spec, o_shape, o_dtype, dims):
    nk = grid[2]
    o_block = tuple(s for s in o_spec.block_shape if s is not None)

    def body(a_ref, b_ref, o_ref, *scr):
        r = _dot(a_ref[...].astype(BF16), b_ref[...].astype(BF16), dims)
        if nk == 1:
            o_ref[...] = r.astype(o_dtype)
        else:
            acc = scr[0]
            k = pl.program_id(2)

            @pl.when(k == 0)
            def _():
                acc[...] = r

            @pl.when(k > 0)
            def _():
                acc[...] += r

            @pl.when(k == nk - 1)
            def _():
                o_ref[...] = acc[...].astype(o_dtype)

    blocks = (_nbytes(a_spec.block_shape, a.dtype) + _nbytes(b_spec.block_shape, b.dtype) + _nbytes(o_block, o_dtype))
    scratch = [pltpu.VMEM(o_block, F32)] if nk > 1 else []
    vmem = 2 * blocks + 3 * _nbytes(o_block, F32) + (8 << 20)
    return pl.pallas_call(
        body, name=name, out_shape=jax.ShapeDtypeStruct(o_shape, o_dtype), grid=grid,
        in_specs=[a_spec, b_spec], out_specs=o_spec, scratch_shapes=scratch,
        compiler_params=_params(("parallel", "parallel", "arbitrary"), vmem),
    )(a, b)


def _mm_fwd(a, w3, *, name, tm, shard_out=False, o_dtype=F32):
    n, kk = a.shape
    s, _, ns = w3.shape
    if shard_out:
        o_shape, o_spec = (s, n, ns), pl.BlockSpec((None, tm, ns), lambda i, j, k: (j, i, 0))
    else:
        o_shape, o_spec = (n, s * ns), pl.BlockSpec((tm, ns), lambda i, j, k: (i, j))
    return _mm(a, w3, name=name, grid=(n // tm, s, 1), dims=NN, o_shape=o_shape, o_dtype=o_dtype, o_spec=o_spec,
               a_spec=pl.BlockSpec((tm, kk), lambda i, j, k: (i, 0)),
               b_spec=pl.BlockSpec((None, kk, ns), lambda i, j, k: (j, 0, 0)))


def _mm_bwd_in(dy, w3, *, name, tm, shard_in=False, o_dtype=F32):
    s, kk, ns = w3.shape
    if shard_in:
        n = dy.shape[1]
        a_spec = pl.BlockSpec((None, tm, ns), lambda i, j, k: (k, i, 0))
    else:
        n = dy.shape[0]
        a_spec = pl.BlockSpec((tm, ns), lambda i, j, k: (i, k))
    return _mm(dy, w3, name=name, grid=(n // tm, 1, s), dims=NT, o_shape=(n, kk), o_dtype=o_dtype,
               o_spec=pl.BlockSpec((tm, kk), lambda i, j, k: (i, 0)), a_spec=a_spec,
               b_spec=pl.BlockSpec((None, kk, ns), lambda i, j, k: (k, 0, 0)))


def _mm_bwd_w(a, dy, *, name, tm, s, ns, shard_in=False):
    n, kk = a.shape
    if shard_in:
        b_spec = pl.BlockSpec((None, tm, ns), lambda i, j, k: (j, k, 0))
    else:
        b_spec = pl.BlockSpec((tm, ns), lambda i, j, k: (k, j))
    return _mm(a, dy, name=name, grid=(1, s, n // tm), dims=TN, o_shape=(s, kk, ns), o_dtype=BF16,
               o_spec=pl.BlockSpec((None, kk, ns), lambda i, j, k: (j, 0, 0)),
               a_spec=pl.BlockSpec((tm, kk), lambda i, j, k: (k, 0)), b_spec=b_spec)


def _mm_down_fwd(hm, wd, *, name, tm):
    s, n, ks = hm.shape
    d = wd.shape[2]
    return _mm(hm, wd, name=name, grid=(n // tm, 1, s), dims=NN, o_shape=(n, d), o_dtype=F32,
               o_spec=pl.BlockSpec((tm, d), lambda i, j, k: (i, 0)),
               a_spec=pl.BlockSpec((None, tm, ks), lambda i, j, k: (k, i, 0)),
               b_spec=pl.BlockSpec((None, ks, d), lambda i, j, k: (k, 0, 0)))


def _mm_down_bwd_in(df, wd, *, name, tm):
    n, d = df.shape
    s, ks, _ = wd.shape
    return _mm(df, wd, name=name, grid=(n // tm, s, 1), dims=NT, o_shape=(s, n, ks), o_dtype=BF16,
               o_spec=pl.BlockSpec((None, tm, ks), lambda i, j, k: (j, i, 0)),
               a_spec=pl.BlockSpec((tm, d), lambda i, j, k: (i, 0)),
               b_spec=pl.BlockSpec((None, ks, d), lambda i, j, k: (j, 0, 0)))


def _mm_down_bwd_w(hm, df, *, name, tm):
    s, n, ks = hm.shape
    d = df.shape[1]
    return _mm(hm, df, name=name, grid=(1, s, n // tm), dims=TN, o_shape=(s, ks, d), o_dtype=BF16,
               o_spec=pl.BlockSpec((None, ks, d), lambda i, j, k: (j, 0, 0)),
               a_spec=pl.BlockSpec((None, tm, ks), lambda i, j, k: (j, k, 0)),
               b_spec=pl.BlockSpec((tm, d), lambda i, j, k: (k, 0)))


def _rowk(fn, *, name, n, tm, nctx, rows=(), pats=(), consts=(), out_rows=(), out_seg=(), out_acc=()):
    nb, ncb = n // tm, nctx // tm
    nr, npat, ncst = len(rows), len(pats), len(consts)
    no, nseg, nacc = len(out_rows), len(out_seg), len(out_acc)
    in_specs, blocks = [], 0
    for arr, w, cb, off in rows:
        in_specs.append(pl.BlockSpec((tm, w), lambda i, cb=cb, off=off: (jnp.maximum(i - off, 0), cb)))
        blocks += _nbytes((tm, w), arr.dtype)
    for p in pats:
        in_specs.append(pl.BlockSpec((None, SUBLANES, p.shape[2]), lambda i: (jnp.where(i >= ncb, 1, 0), 0, 0)))
    for cst in consts:
        in_specs.append(pl.BlockSpec(cst.shape, lambda i: (0, 0)))
    out_shape, out_specs = [], []
    for wt, dt, w, cb in out_rows:
        out_shape.append(jax.ShapeDtypeStruct((n, wt), dt))
        out_specs.append(pl.BlockSpec((tm, w), lambda i, cb=cb: (i, cb)))
        blocks += _nbytes((tm, w), dt)
    for w in out_seg:
        out_shape.append(jax.ShapeDtypeStruct((SUBLANES, w), F32))
        out_specs.append(pl.BlockSpec((SUBLANES, w), lambda i: (0, 0)))
    for w in out_acc:
        out_shape.append(jax.ShapeDtypeStruct((1, w), F32))
        out_specs.append(pl.BlockSpec((1, w), lambda i: (0, 0)))
    scratch = [pltpu.VMEM((2, SUBLANES, w), F32) for w in out_seg] + [pltpu.VMEM((SUBLANES, w), F32) for w in out_acc]

    def body(*refs):
        r_in = refs[:nr]
        p_in = refs[nr:nr + npat]
        c_in = refs[nr + npat:nr + npat + ncst]
        base = nr + npat + ncst
        o_rows = refs[base:base + no]
        o_seg = refs[base + no:base + no + nseg]
        o_acc = refs[base + no + nseg:base + no + nseg + nacc]
        s_seg = refs[base + no + nseg + nacc:base + no + nseg + nacc + nseg]
        s_acc = refs[base + no + nseg + nacc + nseg:]
        i = pl.program_id(0)
        rv = [r[...].astype(F32).reshape(tm // SUBLANES, SUBLANES, r.shape[1]) for r in r_in]
        pv = [p[...] for p in p_in]
        cv = [c[...] for c in c_in]
        is_lat = (i >= ncb).astype(F32)
        ro, so, ao = fn(rv, pv, cv, is_lat)
        for ref, val in zip(o_rows, ro):
            ref[...] = val.reshape(tm, ref.shape[1]).astype(ref.dtype)
        if nseg or nacc:
            @pl.when(i == 0)
            def _():
                for s in list(s_seg) + list(s_acc):
                    s[...] = jnp.zeros(s.shape, F32)

            seg = jnp.where(i >= ncb, 1, 0)
            for s, val in zip(s_seg, so):
                s[seg] = s[seg] + val
            for s, val in zip(s_acc, ao):
                s[...] = s[...] + val

            @pl.when(i == nb - 1)
            def _():
                for o, s in zip(o_seg, s_seg):
                    lat, ctx = s[1], s[0]
                    row = lax.broadcasted_iota(jnp.int32, lat.shape, 0)
                    lat = lat + pltpu.roll(lat, 4, 0)
                    ctx = jnp.broadcast_to(jnp.sum(ctx, axis=0, keepdims=True), lat.shape)
                    o[...] = jnp.where(row < 4, lat, jnp.where(row == 4, ctx, 0.0))
                for o, s in zip(o_acc, s_acc):
                    o[...] = jnp.sum(s[...], axis=0, keepdims=True)

    vmem = 2 * blocks + 8 * tm * 1024 * 4 + (8 << 20)
    res = pl.pallas_call(
        body, name=name, out_shape=out_shape, grid=(nb,), in_specs=in_specs, out_specs=out_specs,
        scratch_shapes=scratch, compiler_params=_params(("arbitrary",), vmem),
    )(*[r[0] for r in rows], *pats, *consts)
    return list(res)


def _rms(z):
    return lax.rsqrt(jnp.mean(z * z, axis=-1, keepdims=True) + NORM_EPS)


def _norm_mod_fwd(z, w, sh, sc, *, name, dims, res=None):
    n, d = z.shape

    def fn(rv, pv, cv, is_lat):
        zz = rv[0]
        if res is not None:
            zz = zz + pv[2][None] * rv[1]
        h = (zz * _rms(zz) * cv[0]) * (1.0 + pv[1][None]) + pv[0][None]
        return ([zz, h] if res is not None else [h]), [], []

    rows = [(z, d, 0, 0)] + ([(res[0], d, 0, 0)] if res is not None else [])
    pats = [sh, sc] + ([res[1]] if res is not None else [])
    outs = ([(d, F32, d, 0)] if res is not None else []) + [(d, BF16, d, 0)]
    out = _rowk(fn, name=name, n=n, tm=dims["tm_row"], nctx=dims["nctx"], rows=rows, pats=pats, consts=[w],
                out_rows=outs)
    return (out[0], out[1]) if res is not None else (None, out[0])


def _norm_core_bwd(zin, dh, w, sc):
    r = _rms(zin)
    xh = zin * r
    dsh = jnp.sum(dh, axis=0)
    dsc = jnp.sum(dh * (xh * w), axis=0)
    dyv = dh * (1.0 + sc[None])
    dw = jnp.sum(dyv * xh, axis=0)
    dxh = dyv * w
    dx = r * (dxh - xh * jnp.mean(dxh * xh, axis=-1, keepdims=True))
    return dx, dsh, dsc, dw


def _norm_mod_bwd(dh, zin, dz_up, w, sc, *, name, dims, res=None):
    n, d = zin.shape

    def fn(rv, pv, cv, is_lat):
        dx, dsh, dsc, dw = _norm_core_bwd(rv[1], rv[0], cv[0], pv[0])
        dz = rv[2] + dx
        if res is None:
            return [dz], [dsh, dsc], [dw]
        return [dz, dz * pv[1][None]], [dsh, dsc, jnp.sum(dz * rv[3], axis=0)], [dw]

    rows = [(dh, d, 0, 0), (zin, d, 0, 0), (dz_up, d, 0, 0)] + ([(res[0], d, 0, 0)] if res is not None else [])
    pats = [sc] + ([res[1]] if res is not None else [])
    outs = [(d, F32, d, 0)] + ([(d, BF16, d, 0)] if res is not None else [])
    return _rowk(fn, name=name, n=n, tm=dims["tm_row"], nctx=dims["nctx"], rows=rows, pats=pats, consts=[w],
                 out_rows=outs, out_seg=[d] * (3 if res is not None else 2), out_acc=[d])


def _loss_bwd(z1, f, gate, tgt, w, *, name, dims):
    n, d = z1.shape

    def fn(rv, pv, cv, is_lat):
        z2 = rv[0] + pv[0][None] * rv[1]
        r = _rms(z2)
        xh = z2 * r
        err = (xh * cv[0] - rv[2]) * is_lat
        dout = err * (1.0 / d)
        dxh = dout * cv[0]
        dz = r * (dxh - xh * jnp.mean(dxh * xh, axis=-1, keepdims=True))
        return ([dz, dz * pv[0][None]], [jnp.sum(dz * rv[1], axis=0)],
                [jnp.sum(0.5 * err * err * (1.0 / d), axis=0), jnp.sum(dout * xh, axis=0)])

    tm = dims["tm_row"]
    rows = [(z1, d, 0, 0), (f, d, 0, 0), (tgt, d, 0, dims["nctx"] // tm)]
    return _rowk(fn, name=name, n=n, tm=tm, nctx=dims["nctx"], rows=rows, pats=[gate], consts=[w],
                 out_rows=[(d, F32, d, 0), (d, BF16, d, 0)], out_seg=[d], out_acc=[d, d])


def _gelu(y):
    return jax.nn.gelu(y, approximate=True)


def _conv_masks(tb, i):
    tok = lax.broadcasted_iota(jnp.int32, (tb, 1), 0) >> 2
    last = jnp.where(i == 0, tb // LOCAL_B - 1, GRID_W - 1)
    wpos = tok & last
    return wpos == 0, wpos == last


def _convffn_fwd(u, cw, cb, *, name, dims):
    _, sh, n, ns = u.shape
    tb = dims["nctx"]

    def body(u_ref, cw_ref, cb_ref, o_ref):
        no_left, no_right = _conv_masks(tb, pl.program_id(1))

        def conv(s):
            uu = u_ref[s]
            ul = jnp.where(no_left, 0.0, pltpu.roll(uu, LOCAL_B, 0))
            ur = jnp.where(no_right, 0.0, pltpu.roll(uu, tb - LOCAL_B, 0))
            return (cb_ref[s] + ul * cw_ref[s, pl.ds(0, 1), :] + uu * cw_ref[s, pl.ds(1, 1), :]
                    + ur * cw_ref[s, pl.ds(2, 1), :])

        a, g = conv(0), conv(1)
        o_ref[...] = (a * _sigmoid(a) * g).astype(BF16)

    vmem = 2 * (2 * tb * ns * 4 + tb * ns * 2) + 10 * tb * ns * 4 + (8 << 20)
    return pl.pallas_call(
        body, name=name, out_shape=jax.ShapeDtypeStruct((sh, n, ns), BF16), grid=(sh, n // tb),
        in_specs=[pl.BlockSpec((2, None, tb, ns), lambda j, i: (0, j, i, 0)),
                  pl.BlockSpec((2, None, 3, ns), lambda j, i: (0, j, 0, 0)),
                  pl.BlockSpec((2, None, 1, ns), lambda j, i: (0, j, 0, 0))],
        out_specs=pl.BlockSpec((None, tb, ns), lambda j, i: (j, i, 0)),
        compiler_params=_params(("parallel", "arbitrary"), vmem),
    )(u, cw, cb)


def _convffn_bwd(u, dhm, cw, cb, *, name, dims, xch=None):
    _, sh, n, ns = u.shape
    tb = dims["nctx"]

    def body(u_ref, dh_ref, cw_ref, cb_ref, du_ref, dcw_ref, dcb_ref):
        i = pl.program_id(1)
        no_left, no_right = _conv_masks(tb, i)

        @pl.when(i == 0)
        def _():
            dcw_ref[...] = jnp.zeros(dcw_ref.shape, F32)
            dcb_ref[...] = jnp.zeros(dcb_ref.shape, F32)

        def taps(s):
            uu = u_ref[s]
            ul = jnp.where(no_left, 0.0, pltpu.roll(uu, LOCAL_B, 0))
            ur = jnp.where(no_right, 0.0, pltpu.roll(uu, tb - LOCAL_B, 0))
            val = (cb_ref[s] + ul * cw_ref[s, pl.ds(0, 1), :] + uu * cw_ref[s, pl.ds(1, 1), :]
                   + ur * cw_ref[s, pl.ds(2, 1), :])
            return val, ul, uu, ur

        a, al, ac, ar = taps(0)
        g, gl, gc, gr = taps(1)
        dh = dh_ref[...].astype(F32)
        sa = _sigmoid(a)
        dg = dh * (a * sa)
        da = dh * g * (sa * (1.0 + a * (1.0 - sa)))
        for s, dc, (tl, tc, tr) in ((0, da, (al, ac, ar)), (1, dg, (gl, gc, gr))):
            dcb_ref[s] += jnp.sum(dc, axis=0, keepdims=True)
            dcw_ref[s, pl.ds(0, 1), :] += jnp.sum(dc * tl, axis=0, keepdims=True)
            dcw_ref[s, pl.ds(1, 1), :] += jnp.sum(dc * tc, axis=0, keepdims=True)
            dcw_ref[s, pl.ds(2, 1), :] += jnp.sum(dc * tr, axis=0, keepdims=True)
            du = (dc * cw_ref[s, pl.ds(1, 1), :]
                  + pltpu.roll(jnp.where(no_left, 0.0, dc) * cw_ref[s, pl.ds(0, 1), :], tb - LOCAL_B, 0)
                  + pltpu.roll(jnp.where(no_right, 0.0, dc) * cw_ref[s, pl.ds(2, 1), :], LOCAL_B, 0))
            du_ref[s] = du.astype(BF16)

    vmem = 2 * (2 * tb * ns * 4 + tb * ns * 2 + 2 * tb * ns * 2) + 16 * tb * ns * 4 + (8 << 20)
    return _call(
        body, name=name, xch=xch, args=[u, dhm, cw, cb], scratch=[],
        out_shape=[jax.ShapeDtypeStruct((2, sh, n, ns), BF16), jax.ShapeDtypeStruct((2, sh, 3, ns), F32),
                   jax.ShapeDtypeStruct((2, sh, 1, ns), F32)],
        grid=(sh, n // tb),
        in_specs=[pl.BlockSpec((2, None, tb, ns), lambda j, i: (0, j, i, 0)),
                  pl.BlockSpec((None, tb, ns), lambda j, i: (j, i, 0)),
                  pl.BlockSpec((2, None, 3, ns), lambda j, i: (0, j, 0, 0)),
                  pl.BlockSpec((2, None, 1, ns), lambda j, i: (0, j, 0, 0))],
        out_specs=[pl.BlockSpec((2, None, tb, ns), lambda j, i: (0, j, i, 0)),
                   pl.BlockSpec((2, None, 3, ns), lambda j, i: (0, j, 0, 0)),
                   pl.BlockSpec((2, None, 1, ns), lambda j, i: (0, j, 0, 0))],
        params=_params(("arbitrary", "arbitrary"), vmem))


def _s5_disc(lr, li, ls, brt, bit):
    lr = jnp.minimum(lr, S5_LAM_RE_MAX)
    dt = jnp.exp(ls)
    mag = jnp.exp(lr * dt)
    ar = mag * jnp.cos(li * dt)
    ai = mag * jnp.sin(li * dt)
    den = lr * lr + li * li
    nr = ar - 1.0
    cr = (nr * lr + ai * li) / den
    ci = (ai * lr - nr * li) / den
    return ar, ai, cr * brt - ci * bit, cr * bit + ci * brt


def _s5_disc_fwd(lr, li, ls, brt, bit):
    def body(lr_ref, li_ref, ls_ref, br_ref, bi_ref, ar_ref, ai_ref, bbr_ref, bbi_ref):
        ar, ai, bbr, bbi = _s5_disc(lr_ref[...], li_ref[...], ls_ref[...], br_ref[...], bi_ref[...])
        ar_ref[...] = ar
        ai_ref[...] = ai
        bbr_ref[...] = bbr
        bbi_ref[...] = bbi

    sd = jax.ShapeDtypeStruct
    return pl.pallas_call(body, name="s5_disc_fwd",
                          out_shape=[sd(lr.shape, F32), sd(lr.shape, F32), sd(brt.shape, F32), sd(brt.shape, F32)],
                          compiler_params=_params(None, 32 << 20))(lr, li, ls, brt, bit)


def _s5_disc_bwd(lr, li, ls, brt, bit, dar, dai, dbbr, dbbi):
    def body(lr_ref, li_ref, ls_ref, br_ref, bi_ref, dar_ref, dai_ref, dbbr_ref, dbbi_ref,
             dlr_ref, dli_ref, dls_ref, dbr_ref, dbi_ref):
        _, vjp = jax.vjp(_s5_disc, lr_ref[...], li_ref[...], ls_ref[...], br_ref[...], bi_ref[...])
        dlr, dli, dls, dbr, dbi = vjp((dar_ref[...], dai_ref[...], dbbr_ref[...], dbbi_ref[...]))
        dlr_ref[...] = dlr
        dli_ref[...] = dli
        dls_ref[...] = dls
        dbr_ref[...] = dbr
        dbi_ref[...] = dbi

    sd = jax.ShapeDtypeStruct
    return pl.pallas_call(body, name="s5_disc_bwd",
                          out_shape=[sd(lr.shape, F32), sd(lr.shape, F32), sd(ls.shape, F32), sd(brt.shape, F32),
                                     sd(brt.shape, F32)],
                          compiler_params=_params(None, 48 << 20))(lr, li, ls, brt, bit, dar, dai, dbbr, dbbi)


def _cmul(ar, ai, xr, xi):
    return ar * xr - ai * xi, ar * xi + ai * xr


def _scan_consts(a_r, a_i, rev):
    row = lax.broadcasted_iota(jnp.int32, a_r.shape, 0)
    second = (row < 4) if rev else (row >= 4)
    a2r, a2i = _cmul(a_r, a_i, a_r, a_i)
    a1r, a1i = jnp.where(second, a_r, 0.0), jnp.where(second, a_i, 0.0)
    apr, api = jnp.where(second, a2r, a_r), jnp.where(second, a2i, a_i)
    return second, a1r, a1i, apr, api


def _scan_tile(xr, xi, pr, pi, consts):
    second, a1r, a1i, apr, api = consts
    sr, si = pltpu.roll(xr, 4, 0), pltpu.roll(xi, 4, 0)
    t1r, t1i = _cmul(a1r, a1i, sr, si)
    t2r, t2i = _cmul(apr, api, pr, pi)
    yr, yi = xr + t1r + t2r, xi + t1i + t2i
    npr = jnp.where(second, yr, pltpu.roll(yr, 4, 0))
    npi = jnp.where(second, yi, pltpu.roll(yi, 4, 0))
    return yr, yi, npr, npi


def _s5_scan(xr_ref, xi_ref, row0, nrows, a_r_ref, a_i_ref, cr_ref, ci_ref, *, rev, conj, lane_block, extra=None):
    width = xr_ref.shape[1]
    nt = nrows // SUBLANES
    for lb in range(width // lane_block):
        lanes = pl.ds(lb * lane_block, lane_block)
        a_r = a_r_ref[:, lanes]
        a_i = a_i_ref[:, lanes]
        if conj:
            a_i = -a_i
        consts = _scan_consts(a_r, a_i, rev)

        def step(t, carry):
            pr, pi = carry[0], carry[1]
            j = (nt - 1 - t) if rev else t
            rows = pl.ds(pl.multiple_of(row0 + j * SUBLANES, SUBLANES), SUBLANES)
            yr, yi, pr, pi = _scan_tile(xr_ref[rows, lanes], xi_ref[rows, lanes], pr, pi, consts)
            xr_ref[rows, lanes] = yr
            xi_ref[rows, lanes] = yi
            if extra is None:
                return pr, pi
            return (pr, pi) + tuple(extra(j, lanes, yr, yi, carry[2:]))

        init = (cr_ref[:, lanes], ci_ref[:, lanes])
        if extra is not None:
            init = init + tuple(extra.init(lanes))
        out = lax.fori_loop(0, nt, step, init)
        cr_ref[:, lanes] = out[0]
        ci_ref[:, lanes] = out[1]
        if extra is not None:
            extra.done(lanes, out[2:])


def _s5_chunk_of(step, ncc, nc, rev):
    if not rev:
        return step
    return jnp.where(step < ncc, ncc - 1 - step, nc - 1 - (step - ncc))


def _s5_fwd_dir(u, base, a8r, a8i, bbr, bbi, ccr, cci, dsk, *, d, name, dims, xch=None):
    n, dm = u.shape
    nk, swk = bbr.shape[0], bbr.shape[2]
    rr, sw = dims["s5_rows"], nk * swk
    nc, ncc = n // rr, dims["nctx"] // rr
    rev = d == 1
    cmap = lambda i: (_s5_chunk_of(i, ncc, nc, rev), 0)

    def body(*refs):
        if d == 0:
            u_ref, a8r_ref, a8i_ref, bbr_ref, bbi_ref, ccr_ref, cci_ref, dsk_ref = refs[:8]
            rest = refs[8:]
        else:
            u_ref, base_ref, a8r_ref, a8i_ref, bbr_ref, bbi_ref, ccr_ref, cci_ref = refs[:8]
            rest = refs[8:]
        y_ref, str_ref, sti_ref, sr, si, cr, ci = rest
        i = pl.program_id(0)

        @pl.when(i == 0)
        def _():
            cr[...] = jnp.zeros(cr.shape, F32)
            ci[...] = jnp.zeros(ci.shape, F32)

        str_ref[...] = cr[...]
        sti_ref[...] = ci[...]
        ub = u_ref[...].astype(BF16)
        for k in range(nk):
            uk = ub[:, k * LANES:(k + 1) * LANES]
            sr[:, k * swk:(k + 1) * swk] = _dot(uk, bbr_ref[k])
            si[:, k * swk:(k + 1) * swk] = _dot(uk, bbi_ref[k])
        _s5_scan(sr, si, 0, rr, a8r_ref, a8i_ref, cr, ci, rev=rev, conj=False, lane_block=dims["s5_lane_block"])
        for k in range(nk):
            cols = slice(k * LANES, (k + 1) * LANES)
            yk = (_dot(sr[:, k * swk:(k + 1) * swk].astype(BF16), ccr_ref[k])
                  - _dot(si[:, k * swk:(k + 1) * swk].astype(BF16), cci_ref[k]))
            if d == 0:
                y_ref[:, cols] = yk + dsk_ref[:, cols] * u_ref[:, cols]
            else:
                y_ref[:, cols] = yk + base_ref[:, cols]

    row_spec = pl.BlockSpec((rr, dm), cmap)
    full = lambda a: pl.BlockSpec(a.shape, lambda i: (0,) * a.ndim)
    ins = [u] + ([] if d == 0 else [base]) + [a8r, a8i, bbr, bbi, ccr, cci] + ([dsk] if d == 0 else [])
    in_specs = [row_spec] + ([] if d == 0 else [row_spec]) + [full(a) for a in (a8r, a8i, bbr, bbi, ccr, cci)]
    in_specs += [full(dsk)] if d == 0 else []
    st_spec = pl.BlockSpec((None, SUBLANES, sw), lambda i: (_s5_chunk_of(i, ncc, nc, rev), 0, 0))
    vmem = 2 * rr * sw * 4 + 6 * rr * dm * 4 + 8 * nk * LANES * swk * 2 + (12 << 20)
    return _call(
        body, name=name, xch=xch, args=ins,
        out_shape=[jax.ShapeDtypeStruct((n, dm), F32), jax.ShapeDtypeStruct((nc, SUBLANES, sw), F32),
                   jax.ShapeDtypeStruct((nc, SUBLANES, sw), F32)],
        grid=(nc,), in_specs=in_specs, out_specs=[row_spec, st_spec, st_spec],
        scratch=[pltpu.VMEM((rr, sw), F32), pltpu.VMEM((rr, sw), F32), pltpu.VMEM((SUBLANES, sw), F32),
                 pltpu.VMEM((SUBLANES, sw), F32)],
        params=_params(("arbitrary",), vmem))


class _DaHook:
    def __init__(self, sr, si, accr, acci, rev_fwd):
        self.sr, self.si, self.accr, self.acci, self.rev_fwd = sr, si, accr, acci, rev_fwd

    def init(self, lanes):
        return self.accr[:, lanes], self.acci[:, lanes]

    def done(self, lanes, acc):
        self.accr[:, lanes] = acc[0]
        self.acci[:, lanes] = acc[1]

    def __call__(self, j, lanes, lr, li, acc):
        base = pl.multiple_of(SUBLANES + j * SUBLANES, SUBLANES)
        cur = pl.ds(base, SUBLANES)
        row = lax.broadcasted_iota(jnp.int32, lr.shape, 0)
        if self.rev_fwd:
            oth = pl.ds(pl.multiple_of(base + SUBLANES, SUBLANES), SUBLANES)
            spr = pltpu.roll(jnp.where(row >= 4, self.sr[cur, lanes], self.sr[oth, lanes]), 4, 0)
            spi = pltpu.roll(jnp.where(row >= 4, self.si[cur, lanes], self.si[oth, lanes]), 4, 0)
        else:
            oth = pl.ds(pl.multiple_of(base - SUBLANES, SUBLANES), SUBLANES)
            spr = pltpu.roll(jnp.where(row >= 4, self.sr[oth, lanes], self.sr[cur, lanes]), 4, 0)
            spi = pltpu.roll(jnp.where(row >= 4, self.si[oth, lanes], self.si[cur, lanes]), 4, 0)
        return acc[0] + spr * lr + spi * li, acc[1] + spr * li - spi * lr


def _s5_bwd_dir(u, dy, du_prev, a8r, a8i, bbr, bbi, ccr, cci, dsk, st_r, st_i, *, d, name, dims, xch=None):
    n, dm = u.shape
    rr = dims["s5_rows"]
    nk, swk = bbr.shape[0], bbr.shape[2]
    sw = nk * swk
    nc, ncc = n // rr, dims["nctx"] // rr
    rev = d == 1
    chunk = lambda i: _s5_chunk_of(nc - 1 - i, ncc, nc, rev)

    def body(*refs):
        u_ref, dy_ref = refs[0], refs[1]
        pos = 2
        dup_ref = None
        if d == 1:
            dup_ref = refs[pos]
            pos += 1
        a8r_ref, a8i_ref, bbr_ref, bbi_ref, ccr_ref, cci_ref = refs[pos:pos + 6]
        pos += 6
        dsk_ref = None
        if d == 0:
            dsk_ref = refs[pos]
            pos += 1
        str_ref, sti_ref = refs[pos:pos + 2]
        pos += 2
        du_ref, dbbr_ref, dbbi_ref, dccr_ref, dcci_ref, dar_ref, dai_ref = refs[pos:pos + 7]
        pos += 7
        dd_ref = None
        if d == 0:
            dd_ref = refs[pos]
            pos += 1
        sr, si, lr, li, cr, ci, lcr, lci, accr, acci, dda = refs[pos:]
        i = pl.program_id(0)

        @pl.when(i == 0)
        def _():
            for ref in (lcr, lci, accr, acci, dda, dbbr_ref, dbbi_ref, dccr_ref, dcci_ref):
                ref[...] = jnp.zeros(ref.shape, F32)

        cr[...] = str_ref[...]
        ci[...] = sti_ref[...]
        spare = pl.ds(rr + SUBLANES, SUBLANES) if rev else pl.ds(0, SUBLANES)
        sr[spare, :] = str_ref[...]
        si[spare, :] = sti_ref[...]
        ub = u_ref[...].astype(BF16)
        dyb = dy_ref[...].astype(BF16)
        for k in range(nk):
            uk = ub[:, k * LANES:(k + 1) * LANES]
            sr[pl.ds(SUBLANES, rr), k * swk:(k + 1) * swk] = _dot(uk, bbr_ref[k])
            si[pl.ds(SUBLANES, rr), k * swk:(k + 1) * swk] = _dot(uk, bbi_ref[k])
        _s5_scan(sr, si, SUBLANES, rr, a8r_ref, a8i_ref, cr, ci, rev=rev, conj=False,
                 lane_block=dims["s5_lane_block"])
        for k in range(nk):
            dyk = dyb[:, k * LANES:(k + 1) * LANES]
            sl = slice(k * swk, (k + 1) * swk)
            lr[:, sl] = _dot(dyk, ccr_ref[k], NT)
            li[:, sl] = -_dot(dyk, cci_ref[k], NT)
            dccr_ref[k] += _dot(sr[pl.ds(SUBLANES, rr), sl].astype(BF16), dyk, TN)
            dcci_ref[k] -= _dot(si[pl.ds(SUBLANES, rr), sl].astype(BF16), dyk, TN)
        _s5_scan(lr, li, 0, rr, a8r_ref, a8i_ref, lcr, lci, rev=not rev, conj=True,
                 lane_block=dims["s5_lane_block"], extra=_DaHook(sr, si, accr, acci, rev))
        for k in range(nk):
            cols = slice(k * LANES, (k + 1) * LANES)
            sl = slice(k * swk, (k + 1) * swk)
            uk = ub[:, cols]
            lrk, lik = lr[:, sl].astype(BF16), li[:, sl].astype(BF16)
            dbbr_ref[k] += _dot(uk, lrk, TN)
            dbbi_ref[k] += _dot(uk, lik, TN)
            duk = _dot(lrk, bbr_ref[k], NT) + _dot(lik, bbi_ref[k], NT)
            if d == 0:
                du_ref[:, cols] = duk + dsk_ref[:, cols] * dy_ref[:, cols]
            else:
                du_ref[:, cols] = duk + dup_ref[:, cols]
        if d == 0:
            prod = (dy_ref[...] * u_ref[...]).reshape(rr // SUBLANES, SUBLANES, dm)
            dda[...] += jnp.sum(prod, axis=0)

        @pl.when(i == nc - 1)
        def _():
            dar_ref[...] = jnp.sum(accr[...], axis=0, keepdims=True)
            dai_ref[...] = jnp.sum(acci[...], axis=0, keepdims=True)
            if d == 0:
                dd_ref[...] = jnp.sum(dda[...], axis=0, keepdims=True)

    row_spec = pl.BlockSpec((rr, dm), lambda i: (chunk(i), 0))
    full = lambda a: pl.BlockSpec(a.shape, lambda i: (0,) * a.ndim)
    st_spec = pl.BlockSpec((None, SUBLANES, sw), lambda i: (chunk(i), 0, 0))
    ins = [u, dy] + ([du_prev] if d == 1 else []) + [a8r, a8i, bbr, bbi, ccr, cci] + ([dsk] if d == 0 else [])
    ins += [st_r, st_i]
    in_specs = [row_spec, row_spec] + ([row_spec] if d == 1 else []) + [full(a) for a in (a8r, a8i, bbr, bbi, ccr, cci)]
    in_specs += ([full(dsk)] if d == 0 else []) + [st_spec, st_spec]
    sd = jax.ShapeDtypeStruct
    out_shape = [sd((n, dm), F32), sd(bbr.shape, F32), sd(bbr.shape, F32), sd(ccr.shape, F32),
                 sd(ccr.shape, F32), sd((1, sw), F32), sd((1, sw), F32)] + ([sd((1, dm), F32)] if d == 0 else [])
    out_specs = [row_spec] + [pl.BlockSpec(s.shape, lambda i, nd=len(s.shape): (0,) * nd) for s in out_shape[1:]]
    scratch = [pltpu.VMEM((rr + 2 * SUBLANES, sw), F32), pltpu.VMEM((rr + 2 * SUBLANES, sw), F32),
               pltpu.VMEM((rr, sw), F32), pltpu.VMEM((rr, sw), F32)]
    scratch += [pltpu.VMEM((SUBLANES, sw), F32)] * 6 + [pltpu.VMEM((SUBLANES, dm), F32)]
    vmem = 4 * (rr + 16) * sw * 4 + 10 * rr * dm * 4 + 24 * nk * LANES * swk * 4 + (12 << 20)
    return _call(body, name=name, xch=xch, args=ins, out_shape=out_shape, grid=(nc,), in_specs=in_specs,
                 out_specs=out_specs, scratch=scratch, params=_params(("arbitrary",), vmem))


def _hg_mask(kind, rev):
    if kind == "tot":
        r = lax.broadcasted_iota(jnp.int32, (SUBLANES, HG_ROWS), 0)
        c = lax.broadcasted_iota(jnp.int32, (SUBLANES, HG_ROWS), 1)
        return (c & 3) == r
    r = lax.broadcasted_iota(jnp.int32, (HG_ROWS, HG_ROWS), 0)
    c = lax.broadcasted_iota(jnp.int32, (HG_ROWS, HG_ROWS), 1)
    same = (r & 3) == (c & 3)
    before = ((c >> 2) >= (r >> 2)) if rev else ((c >> 2) <= (r >> 2))
    return jnp.logical_and(same, before if kind == "cum" else jnp.logical_not(before))


def _split2(x):
    hi = x.astype(BF16)
    return hi, (x - hi.astype(F32)).astype(BF16)


@functools.partial(jax.custom_vjp, nondiff_argnums=(1, 2))
def _mask_sum(x, kind, rev):
    m = _hg_mask(kind, rev).astype(BF16)
    hi, lo = _split2(x)
    return _dot(m, hi) + _dot(m, lo)


def _mask_sum_fwd(x, kind, rev):
    return _mask_sum(x, kind, rev), None


def _mask_sum_bwd(kind, rev, _, g):
    m = _hg_mask(kind, rev).astype(BF16)
    hi, lo = _split2(g)
    return (_dot(m, hi, TN) + _dot(m, lo, TN),)


_mask_sum.defvjp(_mask_sum_fwd, _mask_sum_bwd)


def _hg_chunk(q, v, fraw, l0, l1, st, *, rev):
    nh = q.shape[1] // HG_HEAD
    lb = _sigmoid(l1 - l0)
    logf = jnp.logaddexp(jnp.log(lb), jnp.log1p(-lb) + jax.nn.log_sigmoid(fraw))
    kk = (1.0 - lb) * _sigmoid(fraw * -1.0)
    tri = _hg_mask("cum", rev)
    bcum = _mask_sum(logf, "cum", rev)
    brem = _mask_sum(logf, "rem", rev)
    bend8 = _mask_sum(logf, "tot", rev)
    r8d = lax.broadcasted_iota(jnp.int32, bend8.shape, 0)
    decs = [jnp.exp(jnp.sum(jnp.where(r8d == b, bend8, 0.0), axis=0, keepdims=True)) for b in range(LOCAL_B)]
    qd = (q * jnp.exp(bcum)).astype(BF16)
    kd = (kk * jnp.exp(-bcum)).astype(BF16)
    ke = (kk * jnp.exp(brem)).astype(BF16)
    rowb = lax.broadcasted_iota(jnp.int32, (HG_ROWS, 1), 0) & 3
    masks = [(rowb == b).astype(F32) for b in range(LOCAL_B)]
    outs, new = [], []
    for h in range(nh):
        sl = slice(h * HG_HEAD, (h + 1) * HG_HEAD)
        vh = v[:, sl]
        att = jnp.where(tri, _dot(qd[:, sl], kd[:, sl], NT), 0.0)
        o = _dot(att.astype(BF16), vh.astype(BF16))
        for b in range(LOCAL_B):
            sb = st[h * LOCAL_B + b]
            o = o + masks[b] * _dot(qd[:, sl], sb.astype(BF16), NT)
            new.append(sb * decs[b][:, sl] + _dot((vh * masks[b]).astype(BF16), ke[:, sl], TN))
        outs.append(o)
    return jnp.concatenate(outs, axis=1), tuple(new)


def _hg_chunk_of(step, ncc, nc, rev):
    return _s5_chunk_of(step, ncc, nc, rev)


def _hg_fwd_dir(zz, lb2, *, d, name, dims, xch=None):
    n = zz.shape[0]
    dm = zz.shape[1] // 5
    ns = (dm // HG_HEAD) * LOCAL_B
    nc, ncc = n // HG_ROWS, dims["nctx"] // HG_ROWS
    rev = d == 1
    ch = lambda i: _hg_chunk_of(i, ncc, nc, rev)

    def body(q_ref, v_ref, f_ref, l0_ref, l1_ref, o_ref, st_ref, st):
        @pl.when(pl.program_id(0) == 0)
        def _():
            st[...] = jnp.zeros(st.shape, F32)

        st_ref[...] = st[...]
        o, new = _hg_chunk(q_ref[...], v_ref[...], f_ref[...], l0_ref[...], l1_ref[...],
                           tuple(st[j] for j in range(ns)), rev=rev)
        o_ref[...] = o
        for j in range(ns):
            st[j] = new[j]

    blk = lambda off: pl.BlockSpec((HG_ROWS, dm), lambda i, off=off: (ch(i), off))
    lspec = lambda layer: pl.BlockSpec((None, None, 1, dm), lambda i, layer=layer: (d, layer, 0, 0))
    return _call(
        body, name=name, xch=xch, args=[zz, zz, zz, lb2, lb2],
        out_shape=[jax.ShapeDtypeStruct((n, dm), F32), jax.ShapeDtypeStruct((nc, ns, HG_HEAD, HG_HEAD), F32)],
        grid=(nc,),
        in_specs=[blk(0), blk(1), blk(2 + d), lspec(0), lspec(1)],
        out_specs=[pl.BlockSpec((HG_ROWS, dm), lambda i: (ch(i), 0)),
                   pl.BlockSpec((None, ns, HG_HEAD, HG_HEAD), lambda i: (ch(i), 0, 0, 0))],
        scratch=[pltpu.VMEM((ns, HG_HEAD, HG_HEAD), F32)],
        params=_params(("arbitrary",), 48 << 20))


def _hg_bwd_dir(zz, lb2, do, sts, dqv_prev, *, d, name, dims, xch=None):
    n = zz.shape[0]
    dm = zz.shape[1] // 5
    ns = (dm // HG_HEAD) * LOCAL_B
    nc, ncc = n // HG_ROWS, dims["nctx"] // HG_ROWS
    rev = d == 1
    ch = lambda i: _hg_chunk_of(nc - 1 - i, ncc, nc, rev)
    qv_dtype = F32 if d == 0 else BF16

    def body(*refs):
        q_ref, v_ref, f_ref, l0_ref, l1_ref, do_ref, st_ref = refs[:7]
        pos = 7
        if d == 1:
            dqp_ref, dvp_ref = refs[7:9]
            pos = 9
        dq_ref, dv_ref, df_ref, dl_ref, dst = refs[pos:]
        i = pl.program_id(0)

        @pl.when(i == 0)
        def _():
            dst[...] = jnp.zeros(dst.shape, F32)
            dl_ref[...] = jnp.zeros(dl_ref.shape, F32)

        _, vjp = jax.vjp(functools.partial(_hg_chunk, rev=rev), q_ref[...], v_ref[...], f_ref[...], l0_ref[...],
                         l1_ref[...], tuple(st_ref[j] for j in range(ns)))
        dq, dv, df, dl0, dl1, dstn = vjp((do_ref[...], tuple(dst[j] for j in range(ns))))
        for j in range(ns):
            dst[j] = dstn[j]
        if d == 1:
            dq = dq + dqp_ref[...]
            dv = dv + dvp_ref[...]
        dq_ref[...] = dq.astype(qv_dtype)
        dv_ref[...] = dv.astype(qv_dtype)
        df_ref[...] = df.astype(BF16)
        dl_ref[0] += dl0
        dl_ref[1] += dl1

    blk = lambda off: pl.BlockSpec((HG_ROWS, dm), lambda i, off=off: (ch(i), off))
    oblk = pl.BlockSpec((HG_ROWS, dm), lambda i: (ch(i), 0))
    lspec = lambda layer: pl.BlockSpec((None, None, 1, dm), lambda i, layer=layer: (d, layer, 0, 0))
    ins = [zz, zz, zz, lb2, lb2, do, sts] + (list(dqv_prev) if d == 1 else [])
    in_specs = [blk(0), blk(1), blk(2 + d), lspec(0), lspec(1), oblk,
                pl.BlockSpec((None, ns, HG_HEAD, HG_HEAD), lambda i: (ch(i), 0, 0, 0))]
    in_specs += [oblk, oblk] if d == 1 else []
    sd = jax.ShapeDtypeStruct
    return _call(
        body, name=name, xch=xch, args=ins,
        out_shape=[sd((n, dm), qv_dtype), sd((n, dm), qv_dtype), sd((n, dm), BF16), sd((2, 1, dm), F32)],
        grid=(nc,), in_specs=in_specs,
        out_specs=[oblk, oblk, oblk, pl.BlockSpec((2, 1, dm), lambda i: (0, 0, 0))],
        scratch=[pltpu.VMEM((ns, HG_HEAD, HG_HEAD), F32)],
        params=_params(("arbitrary",), 56 << 20))


def _hg_readout(o, g, w):
    outs = []
    for h in range(o.shape[-1] // HG_HEAD):
        sl = slice(h * HG_HEAD, (h + 1) * HG_HEAD)
        oh = o[..., sl]
        outs.append(oh * _rms(oh) * w * _sigmoid(g[..., sl]))
    return jnp.concatenate(outs, axis=-1)


def _silu(x):
    return x * _sigmoid(x)


def _mod_fwd(craw, w, b):
    def body(c_ref, w_ref, b_ref, o_ref):
        s = _silu(c_ref[...]).astype(BF16)
        for layer in range(w.shape[0]):
            o_ref[layer] = _dot(s, w_ref[layer].astype(BF16)) + b_ref[layer]

    return pl.pallas_call(body, name="mod_fwd",
                          out_shape=jax.ShapeDtypeStruct((w.shape[0], craw.shape[0], w.shape[2]), F32),
                          compiler_params=_params(None, 40 << 20))(craw, w, b)


def _mod_bwd(craw, w, dlat_sh, dctx_sh, dlat_full, dctx_full):
    nl, dm, ns = w.shape
    nb = dlat_sh.shape[1]

    def body(c_ref, w_ref, dl_ref, dc_ref, dlf_ref, dcf_ref, dw_ref, db_ref, dcc_ref):
        craw_v = c_ref[...]
        s = _silu(craw_v)
        s_lat = s[:nb].astype(BF16)
        s_ctx = s[nb:].astype(BF16)
        row = lax.broadcasted_iota(jnp.int32, (SUBLANES, ns), 0)
        dsc = jnp.zeros((SUBLANES, dm), F32)
        for layer in range(nl):
            tot = dc_ref[0, pl.ds(layer, 1), :]
            totf = dcf_ref[0, pl.ds(layer, 1), :]
            for i in range(1, NDEV):
                tot = tot + dc_ref[i, pl.ds(layer, 1), :]
                totf = totf + dcf_ref[i, pl.ds(layer, 1), :]
            dc8 = jnp.where(row == 0, jnp.broadcast_to(tot, (SUBLANES, ns)), 0.0).astype(BF16)
            dw_ref[layer] = _dot(s_lat, dl_ref[layer].astype(BF16), TN) + _dot(s_ctx, dc8, TN)
            db_ref[layer] = jnp.sum(dlf_ref[layer], axis=0, keepdims=True) + totf
            dsc = dsc + _dot(dc8, w_ref[layer].astype(BF16), NT)
        cc = craw_v[nb:]
        sg = _sigmoid(cc)
        dcc_ref[...] = dsc * (sg * (1.0 + cc * (1.0 - sg)))

    sd = jax.ShapeDtypeStruct
    return pl.pallas_call(body, name="mod_bwd",
                          out_shape=[sd((nl, dm, ns), F32), sd((nl, 1, dlat_full.shape[2]), F32), sd((SUBLANES, dm), F32)],
                          compiler_params=_params(None, 48 << 20))(craw, w, dlat_sh, dctx_sh, dlat_full, dctx_full)


def _adam_rows(r):
    best = None
    for t in range(2 * SUBLANES, min(r, 128) + 1, 2 * SUBLANES):
        if r % t == 0:
            best = t
    return best if best is not None else r


def _adamw(parts, w, m, v, *, name):
    npart, r, c = parts.shape
    tr = _adam_rows(r)

    def body(p_ref, w_ref, m_ref, v_ref, g_ref, d_ref, nm_ref, nv_ref):
        g = p_ref[0].astype(F32)
        for i in range(1, npart):
            g = g + p_ref[i].astype(F32)
        nm = ADAM_B1 * m_ref[...] + (1.0 - ADAM_B1) * g
        nv = ADAM_B2 * v_ref[...] + (1.0 - ADAM_B2) * (g * g)
        m_hat = nm / (1.0 - ADAM_B1 ** ADAM_STEP)
        v_hat = nv / (1.0 - ADAM_B2 ** ADAM_STEP)
        g_ref[...] = g
        d_ref[...] = -ADAM_LR * (m_hat / (jnp.sqrt(v_hat) + ADAM_EPS) + ADAM_WD * w_ref[...])
        nm_ref[...] = nm
        nv_ref[...] = nv

    spec = pl.BlockSpec((tr, c), lambda i: (i, 0))
    vmem = 2 * (npart + 7) * tr * c * 4 + (8 << 20)
    return pl.pallas_call(
        body, name=name, out_shape=[jax.ShapeDtypeStruct((r, c), F32)] * 4, grid=(r // tr,),
        in_specs=[pl.BlockSpec((npart, tr, c), lambda i: (0, i, 0)), spec, spec, spec], out_specs=[spec] * 4,
        compiler_params=_params(("parallel",), vmem),
    )(parts, w, m, v)


def _to_tm(a):
    return jnp.transpose(a, (1, 0, 2)).reshape(a.shape[1] * a.shape[0], a.shape[2])


def _pattern(mod_lat, mod_ctx, m, dm):
    lat = mod_lat[:, m * dm:(m + 1) * dm]
    ctx = jnp.broadcast_to(mod_ctx[None, m * dm:(m + 1) * dm], (SUBLANES, dm))
    return jnp.stack([ctx, jnp.concatenate([lat, lat], axis=0)])


def _blockdiag_b(bt, nk):
    g, h, p = bt.shape
    t = bt.reshape(nk, 8, h, p)
    return jnp.einsum("kghp,gj->kghjp", t, jnp.eye(8, dtype=bt.dtype)).reshape(nk, 8 * h, 8 * p)


def _blockdiag_c(ct, nk):
    g, h, p = ct.shape
    t = ct.reshape(nk, 8, h, p)
    return jnp.einsum("kghp,gj->kgpjh", t, jnp.eye(8, dtype=ct.dtype)).reshape(nk, 8 * p, 8 * h)


def _diag_b(dbb, h, p):
    nk = dbb.shape[0]
    return jnp.einsum("kghgp->kghp", dbb.reshape(nk, 8, h, 8, p)).reshape(nk * 8, h, p)


def _diag_c(dcc, h, p):
    nk = dcc.shape[0]
    return jnp.einsum("kgpgh->kghp", dcc.reshape(nk, 8, p, 8, h)).reshape(nk * 8, h, p)


def kernel(x, c, ctx, c_ctx, w_mod, b_mod, norm1_w, norm2_w, final_norm_w, s5_w_in, s5_lam_re, s5_lam_im, s5_log_step, s5_b_re, s5_b_im, s5_c_re, s5_c_im, s5_d, s5_w_glu, s5_w_out, hg_w_in, hg_lower_bounds, hg_gnorm_w, hg_w_out, ffn_w_up, ffn_conv_w, ffn_conv_b, ffn_w_down, loss_target, m_c_ctx, m_w_mod, m_b_mod, m_norm1_w, m_norm2_w, m_final_norm_w, m_s5_w_in, m_s5_lam_re, m_s5_lam_im, m_s5_log_step, m_s5_b_re, m_s5_b_im, m_s5_c_re, m_s5_c_im, m_s5_d, m_s5_w_glu, m_s5_w_out, m_hg_w_in, m_hg_lower_bounds, m_hg_gnorm_w, m_hg_w_out, m_ffn_w_up, m_ffn_conv_w, m_ffn_conv_b, m_ffn_w_down, v_c_ctx, v_w_mod, v_b_mod, v_norm1_w, v_norm2_w, v_final_norm_w, v_s5_w_in, v_s5_lam_re, v_s5_lam_im, v_s5_log_step, v_s5_b_re, v_s5_b_im, v_s5_c_re, v_s5_c_im, v_s5_d, v_s5_w_glu, v_s5_w_out, v_hg_w_in, v_hg_lower_bounds, v_hg_gnorm_w, v_hg_w_out, v_ffn_w_up, v_ffn_conv_w, v_ffn_conv_b, v_ffn_w_down):
    given = dict(locals())
    bsz, lx, dm = x.shape
    lc = ctx.shape[1]
    assert bsz == LOCAL_B and w_mod.shape[0] == 2 and dm % LANES == 0
    n, nctx = (lc + lx) * bsz, lc * bsz
    ngrp, nstate, hgrp = dm // S5_GROUP, S5_STATE, S5_GROUP
    nk = dm // LANES
    dims = dict(nctx=nctx, tm=min(512, nctx), tm_row=min(256, nctx), s5_rows=min(256, nctx),
                s5_lane_block=min(512, 8 * nstate))
    tm = dims["tm"]
    assert nctx % HG_ROWS == 0 and (lx * bsz) % nctx == 0 and lc % GRID_W == 0 and lc & (lc - 1) == 0
    me = 4 * lax.axis_index("x") + 2 * lax.axis_index("y") + lax.axis_index("c")

    gath = _exchange([given[k].astype(BF16) for k in ("s5_w_in", "s5_w_glu", "s5_w_out")]
                     + [c, hg_lower_bounds, ffn_conv_w], a2a=False, name="gather_weights")
    w_s5in, w_glu, w_s5out = (g.reshape(1, dm, dm) for g in gath[:3])
    c_all, lb_all, cw_all = gath[3:]
    ns_up = ffn_w_up.shape[2]
    w_up, w_dn = [None, None], [None, None]
    gather = lambda arrs: _Xchg([a.astype(BF16) for a in arrs], [False] * len(arrs))
    scatter = lambda arrs: _Xchg(arrs, [True] * len(arrs))
    cw = [cw_all[:, layer].reshape(2, NDEV // 2, 3, ns_up) for layer in range(2)]
    cb = [ffn_conv_b[layer].reshape(2, NDEV // 2, 1, ns_up) for layer in range(2)]
    lb2 = jnp.transpose(lb_all, (1, 2, 0, 3)).reshape(2, 2, 1, dm)

    nsm = w_mod.shape[2]
    craw = jnp.concatenate([c_all.reshape(NDEV * bsz, dm), c_ctx[None], jnp.zeros((SUBLANES - 1, dm), F32)], axis=0)
    b_sh = lax.dynamic_slice(b_mod, (0, me * nsm), (2, nsm)).reshape(2, 1, nsm)
    mod_sh = _mod_fwd(craw, w_mod, b_sh)
    (mod_g,) = _exchange([mod_sh], a2a=False, name="gather_mod")
    mod_full = jnp.transpose(mod_g, (1, 2, 0, 3)).reshape(2, craw.shape[0], NDEV * nsm)
    pat = []
    for layer in range(2):
        mlat = lax.dynamic_slice(mod_full[layer], (me * bsz, 0), (bsz, N_MOD * dm))
        mctx = mod_full[layer, NDEV * bsz]
        pat.append([_pattern(mlat, mctx, m, dm) for m in range(N_MOD)])

    lr4 = s5_lam_re[0].reshape(2, ngrp, 1, nstate)
    li4 = s5_lam_im[0].reshape(2, ngrp, 1, nstate)
    ls4 = s5_log_step[0].reshape(2, ngrp, 1, 1)
    brt = jnp.transpose(s5_b_re[0], (0, 1, 3, 2))
    bit = jnp.transpose(s5_b_im[0], (0, 1, 3, 2))
    abar_r, abar_i, bbar_r, bbar_i = _s5_disc_fwd(lr4, li4, ls4, brt, bit)
    sw = ngrp * nstate
    a8r = [jnp.broadcast_to(abar_r[d].reshape(1, sw), (SUBLANES, sw)) for d in range(2)]
    a8i = [jnp.broadcast_to(abar_i[d].reshape(1, sw), (SUBLANES, sw)) for d in range(2)]
    bbr = [_blockdiag_b(bbar_r[d], nk).astype(BF16) for d in range(2)]
    bbi = [_blockdiag_b(bbar_i[d], nk).astype(BF16) for d in range(2)]
    ccr = [_blockdiag_c(s5_c_re[0, d], nk).astype(BF16) for d in range(2)]
    cci = [_blockdiag_c(s5_c_im[0, d], nk).astype(BF16) for d in range(2)]
    dsk = s5_d.reshape(1, dm)

    z0 = jnp.concatenate([_to_tm(ctx), _to_tm(x)], axis=0)
    tgt = _to_tm(loss_target)
    n1w = [norm1_w[layer].reshape(1, dm) for layer in range(2)]
    n2w = [norm2_w[layer].reshape(1, dm) for layer in range(2)]

    def ffn_fwd(layer, h2):
        u = _mm_fwd(h2, w_up[layer], name=f"ffn_up{layer}", tm=tm, shard_out=True)
        u4 = u.reshape(2, NDEV // 2, n, ns_up)
        hm = _convffn_fwd(u4, cw[layer], cb[layer], name=f"convffn_fwd{layer}", dims=dims)
        f = _mm_down_fwd(hm, w_dn[layer], name=f"ffn_down{layer}", tm=tm)
        return u4, hm, f

    _, h0 = _norm_mod_fwd(z0, n1w[0], pat[0][0], pat[0][1], name="norm1_l0", dims=dims)
    u_s5 = _mm_fwd(h0, w_s5in, name="s5_in", tm=tm)
    (y_a, st0r, st0i), (g_up0, g_dn0) = _s5_fwd_dir(
        u_s5, None, a8r[0], a8i[0], bbr[0], bbi[0], ccr[0], cci[0], dsk, d=0, name="s5_fwd_d0", dims=dims,
        xch=gather([ffn_w_up[0], ffn_w_down[0]]))
    w_up[0], w_dn[0] = g_up0, g_dn0.reshape(NDEV // 2, -1, dm)
    (y_s5, st1r, st1i), (g_hgin, g_hgout, g_dn1) = _s5_fwd_dir(
        u_s5, y_a, a8r[1], a8i[1], bbr[1], bbi[1], ccr[1], cci[1], dsk, d=1, name="s5_fwd_d1", dims=dims,
        xch=gather([hg_w_in[0], hg_w_out[0], ffn_w_down[1]]))
    w_hgin, w_hgout, w_dn[1] = g_hgin, g_hgout.reshape(1, dm, dm), g_dn1.reshape(NDEV // 2, -1, dm)
    (zg,) = _rowk(lambda rv, pv, cv, il: ([_gelu(rv[0])], [], []), name="s5_gelu", n=n, tm=dims["tm_row"], nctx=nctx,
                  rows=[(y_s5, dm, 0, 0)], out_rows=[(dm, BF16, dm, 0)])
    t_glu = _mm_fwd(zg, w_glu, name="s5_glu", tm=tm)
    (z2g,) = _rowk(lambda rv, pv, cv, il: ([rv[0] * _sigmoid(rv[1])], [], []), name="s5_gate", n=n,
                   tm=dims["tm_row"], nctx=nctx, rows=[(zg, dm, 0, 0), (t_glu, dm, 0, 0)],
                   out_rows=[(dm, BF16, dm, 0)])
    ymix0 = _mm_fwd(z2g, w_s5out, name="s5_out", tm=tm)
    z1_l0, h2_l0 = _norm_mod_fwd(z0, n2w[0], pat[0][3], pat[0][4], name="norm2_l0", dims=dims,
                                 res=(ymix0, pat[0][2]))
    u4_l0, hm_l0, f_l0 = ffn_fwd(0, h2_l0)

    z2_l0, h1 = _norm_mod_fwd(z1_l0, n1w[1], pat[1][0], pat[1][1], name="norm1_l1", dims=dims,
                              res=(f_l0, pat[0][5]))
    zz = _mm_fwd(h1, w_hgin, name="hg_in", tm=tm)
    (o_f, sts_f), (w_up[1],) = _hg_fwd_dir(zz, lb2, d=0, name="hg_fwd_d0", dims=dims, xch=gather([ffn_w_up[1]]))
    (o_b, sts_b), _ = _hg_fwd_dir(zz, lb2, d=1, name="hg_fwd_d1", dims=dims)
    gnw = hg_gnorm_w.reshape(1, HG_HEAD)
    (og,) = _rowk(lambda rv, pv, cv, il: ([_hg_readout(rv[0] + rv[1], rv[2], cv[0])], [], []), name="hg_readout",
                  n=n, tm=dims["tm_row"], nctx=nctx, rows=[(o_f, dm, 0, 0), (o_b, dm, 0, 0), (zz, dm, 4, 0)],
                  consts=[gnw], out_rows=[(dm, BF16, dm, 0)])
    ymix1 = _mm_fwd(og, w_hgout, name="hg_out", tm=tm)
    z1_l1, h2_l1 = _norm_mod_fwd(z2_l0, n2w[1], pat[1][3], pat[1][4], name="norm2_l1", dims=dims,
                                 res=(ymix1, pat[1][2]))
    u4_l1, hm_l1, f_l1 = ffn_fwd(1, h2_l1)

    dz, df, dgate2_l1, loss_part, dfinal_w = _loss_bwd(z1_l1, f_l1, pat[1][5], tgt, final_norm_w.reshape(1, dm),
                                                        name="loss_bwd", dims=dims)

    def ffn_bwd(layer, df_, u4, hm, h2, xch=None):
        dhm = _mm_down_bwd_in(df_, w_dn[layer], name=f"ffn_down_bwd_in{layer}", tm=tm)
        dwd = _mm_down_bwd_w(hm, df_, name=f"ffn_down_bwd_w{layer}", tm=tm)
        (du4, dcw, dcb), got = _convffn_bwd(u4, dhm, cw[layer], cb[layer], name=f"convffn_bwd{layer}", dims=dims,
                                            xch=xch)
        du = du4.reshape(NDEV, n, ns_up)
        dh2 = _mm_bwd_in(du, w_up[layer], name=f"ffn_up_bwd_in{layer}", tm=tm, shard_in=True)
        dwu = _mm_bwd_w(h2, du, name=f"ffn_up_bwd_w{layer}", tm=tm, s=NDEV, ns=ns_up, shard_in=True)
        return dh2, dwu, dwd, dcw, dcb, got

    dh2, dwu_l1, dwd_l1, dcw_l1, dcb_l1, _ = ffn_bwd(1, df, u4_l1, hm_l1, h2_l1)
    dz, dymix, dsh2_l1, dsc2_l1, dgate1_l1, dn2w_l1 = _norm_mod_bwd(dh2, z1_l1, dz, n2w[1], pat[1][4], name="norm2_bwd_l1",
                                                                    dims=dims, res=(ymix1, pat[1][2]))
    dog = _mm_bwd_in(dymix, w_hgout, name="hg_out_bwd_in", tm=tm)
    dw_hgout = _mm_bwd_w(og, dymix, name="hg_out_bwd_w", tm=tm, s=1, ns=dm)

    def readout_bwd(rv, pv, cv, il):
        _, vjp = jax.vjp(_hg_readout, rv[0] + rv[1], rv[2], cv[0])
        do, dg, dw = vjp(rv[3])
        return [do, dg], [], [jnp.broadcast_to(dw, (SUBLANES, HG_HEAD)) * (1.0 / SUBLANES)]

    do, dg, dgnw = _rowk(readout_bwd, name="hg_readout_bwd", n=n, tm=dims["tm_row"], nctx=nctx,
                         rows=[(o_f, dm, 0, 0), (o_b, dm, 0, 0), (zz, dm, 4, 0), (dog, dm, 0, 0)], consts=[gnw],
                         out_rows=[(dm, F32, dm, 0), (dm, BF16, dm, 0)], out_acc=[HG_HEAD])
    (dq0, dv0, dff, dl_f), (p_up1, p_dn1) = _hg_bwd_dir(
        zz, lb2, do, sts_f, None, d=0, name="hg_bwd_d0", dims=dims,
        xch=scatter([dwu_l1, dwd_l1.reshape(NDEV, -1, dm)]))
    (dq, dv, dfb, dl_b), (p_hgout,) = _hg_bwd_dir(
        zz, lb2, do, sts_b, (dq0, dv0), d=1, name="hg_bwd_d1", dims=dims,
        xch=scatter([dw_hgout.reshape(NDEV, -1, dm)]))
    dzz = jnp.concatenate([dq, dv, dff, dfb, dg], axis=1)
    dh1 = _mm_bwd_in(dzz, w_hgin, name="hg_in_bwd_in", tm=tm)
    dw_hgin = _mm_bwd_w(h1, dzz, name="hg_in_bwd_w", tm=tm, s=NDEV, ns=w_hgin.shape[2])
    dz, df0, dsh1_l1, dsc1_l1, dgate2_l0, dn1w_l1 = _norm_mod_bwd(dh1, z2_l0, dz, n1w[1], pat[1][1], name="norm1_bwd_l1",
                                                                  dims=dims, res=(f_l0, pat[0][5]))
    dh2, dwu_l0, dwd_l0, dcw_l0, dcb_l0, (p_hgin,) = ffn_bwd(0, df0, u4_l0, hm_l0, h2_l0, xch=scatter([dw_hgin]))
    dz, dymix, dsh2_l0, dsc2_l0, dgate1_l0, dn2w_l0 = _norm_mod_bwd(dh2, z1_l0, dz, n2w[0], pat[0][4], name="norm2_bwd_l0",
                                                                    dims=dims, res=(ymix0, pat[0][2]))
    dz2g = _mm_bwd_in(dymix, w_s5out, name="s5_out_bwd_in", tm=tm)
    dw_s5out = _mm_bwd_w(z2g, dymix, name="s5_out_bwd_w", tm=tm, s=1, ns=dm)

    def gate_bwd(rv, pv, cv, il):
        sg = _sigmoid(rv[1])
        return [rv[2] * rv[0] * sg * (1.0 - sg), rv[2] * sg], [], []

    dt_glu, dzg_a = _rowk(gate_bwd, name="s5_gate_bwd", n=n, tm=dims["tm_row"], nctx=nctx,
                          rows=[(zg, dm, 0, 0), (t_glu, dm, 0, 0), (dz2g, dm, 0, 0)],
                          out_rows=[(dm, BF16, dm, 0), (dm, F32, dm, 0)])
    dzg_b = _mm_bwd_in(dt_glu, w_glu, name="s5_glu_bwd_in", tm=tm)
    dw_glu = _mm_bwd_w(zg, dt_glu, name="s5_glu_bwd_w", tm=tm, s=1, ns=dm)

    def gelu_bwd(rv, pv, cv, il):
        _, vjp = jax.vjp(_gelu, rv[0])
        return [vjp(rv[1] + rv[2])[0]], [], []

    (dy_s5,) = _rowk(gelu_bwd, name="s5_gelu_bwd", n=n, tm=dims["tm_row"], nctx=nctx,
                     rows=[(y_s5, dm, 0, 0), (dzg_a, dm, 0, 0), (dzg_b, dm, 0, 0)], out_rows=[(dm, F32, dm, 0)])
    dcw_both = jnp.stack([dcw_l0.reshape(NDEV, 3, ns_up), dcw_l1.reshape(NDEV, 3, ns_up)], axis=1)
    (du_a, dbbr0, dbbi0, dccr0, dcci0, dar0, dai0, ddsk), (p_up0, p_dn0, p_cw) = _s5_bwd_dir(
        u_s5, dy_s5, None, a8r[0], a8i[0], bbr[0], bbi[0], ccr[0], cci[0], dsk, st0r, st0i, d=0, name="s5_bwd_d0",
        dims=dims, xch=scatter([dwu_l0, dwd_l0.reshape(NDEV, -1, dm), dcw_both]))
    (du_s5, dbbr1, dbbi1, dccr1, dcci1, dar1, dai1), (p_s5out, p_glu) = _s5_bwd_dir(
        u_s5, dy_s5, du_a, a8r[1], a8i[1], bbr[1], bbi[1], ccr[1], cci[1], dsk, st1r, st1i, d=1, name="s5_bwd_d1",
        dims=dims, xch=scatter([dw_s5out.reshape(NDEV, -1, dm), dw_glu.reshape(NDEV, -1, dm)]))
    dh0 = _mm_bwd_in(du_s5, w_s5in, name="s5_in_bwd_in", tm=tm)
    dw_s5in = _mm_bwd_w(h0, du_s5, name="s5_in_bwd_w", tm=tm, s=1, ns=dm)
    dz0, dsh1_l0, dsc1_l0, dn1w_l0 = _norm_mod_bwd(dh0, z0, dz, n1w[0], pat[0][1], name="norm1_bwd_l0", dims=dims)

    dar = jnp.stack([dar0, dar1]).reshape(2, ngrp, 1, nstate)
    dai = jnp.stack([dai0, dai1]).reshape(2, ngrp, 1, nstate)
    dbbr = jnp.stack([_diag_b(dbbr0, hgrp, nstate), _diag_b(dbbr1, hgrp, nstate)])
    dbbi = jnp.stack([_diag_b(dbbi0, hgrp, nstate), _diag_b(dbbi1, hgrp, nstate)])
    dlr, dli, dls, dbrt, dbit = _s5_disc_bwd(lr4, li4, ls4, brt, bit, dar, dai, dbbr, dbbi)
    g_c_re = jnp.stack([_diag_c(dccr0, hgrp, nstate), _diag_c(dccr1, hgrp, nstate)])
    g_c_im = jnp.stack([_diag_c(dcci0, hgrp, nstate), _diag_c(dcci1, hgrp, nstate)])

    dmod = jnp.stack([
        jnp.concatenate([dsh1_l0, dsc1_l0, dgate1_l0, dsh2_l0, dsc2_l0, dgate2_l0], axis=1),
        jnp.concatenate([dsh1_l1, dsc1_l1, dgate1_l1, dsh2_l1, dsc2_l1, dgate2_l1], axis=1)])
    dmod_g, p_s5in = _exchange([dmod, dw_s5in.reshape(NDEV, -1, dm)], a2a=[False, True], name="gather_dmod")
    dlat_full = jnp.transpose(dmod_g[:, :, :bsz], (1, 0, 2, 3)).reshape(2, NDEV * bsz, N_MOD * dm)
    dctx_full = dmod_g[:, :, bsz]
    dlat_sh = lax.dynamic_slice(dlat_full, (0, 0, me * nsm), (2, NDEV * bsz, nsm))
    dctx_sh = lax.dynamic_slice(dctx_full, (0, 0, me * nsm), (NDEV, 2, nsm))
    g_w_mod, g_b_mod, dcctx8 = _mod_bwd(craw, w_mod, dlat_sh, dctx_sh, dlat_full, dctx_full)

    dl_hg = jnp.stack([dl_f[:, 0], dl_b[:, 0]])
    wide = lambda g: g.reshape(-1, dm)
    small = [("c_ctx", wide(dcctx8[:1])), ("norm1_w", jnp.concatenate([dn1w_l0, dn1w_l1])),
             ("norm2_w", jnp.concatenate([dn2w_l0, dn2w_l1])), ("final_norm_w", dfinal_w),
             ("s5_lam_re", dlr.reshape(-1, nstate)), ("s5_lam_im", dli.reshape(-1, nstate)),
             ("s5_log_step", dls.reshape(2, ngrp)),
             ("s5_b_re", wide(jnp.transpose(dbrt, (0, 1, 3, 2)).astype(BF16))),
             ("s5_b_im", wide(jnp.transpose(dbit, (0, 1, 3, 2)).astype(BF16))),
             ("s5_c_re", wide(g_c_re.astype(BF16))), ("s5_c_im", wide(g_c_im.astype(BF16))), ("s5_d", ddsk),
             ("hg_gnorm_w", dgnw), ("ffn_conv_b", jnp.stack([dcb_l0.reshape(-1), dcb_l1.reshape(-1)]))]
    gathered = _exchange([g for _, g in small] + [wide(dl_hg), loss_part], a2a=False, name="gather_small")
    res = {}
    for (k, g), parts in zip(small, gathered):
        w2, m2, v2 = (given[p + k].reshape(g.shape) for p in ("", "m_", "v_"))
        res[k] = tuple(o.reshape(given[k].shape) for o in _adamw(parts, w2, m2, v2, name="adamw_" + k))

    def total(parts, name):
        z = jnp.zeros(parts.shape[1:], F32)
        return _adamw(parts, z, z, z, name=name)[0]

    loss = jnp.sum(total(gathered[-1], "sum_loss"))
    dl_tot = total(gathered[-2], "sum_dlb").reshape(dl_hg.shape)
    nlb = hg_lower_bounds.shape[2]
    g_lb = lax.dynamic_slice(dl_tot, (0, 0, me * nlb), (2, 2, nlb))

    def adam_local(name, g, shape2):
        w, m, v = given[name], given["m_" + name], given["v_" + name]
        out = _adamw(g.reshape((1,) + shape2), w.reshape(shape2), m.reshape(shape2), v.reshape(shape2),
                     name="adamw_" + name)
        return tuple(o.reshape(w.shape) for o in out)

    def adam_parts(name, p):
        w, m, v = given[name], given["m_" + name], given["v_" + name]
        shape2 = (p.shape[0], -1, w.shape[-1])
        p3 = p.reshape(shape2)
        s2 = p3.shape[1:]
        out = _adamw(p3, w.reshape(s2), m.reshape(s2), v.reshape(s2), name="adamw_" + name)
        return tuple(o.reshape(w.shape) for o in out)

    res["hg_lower_bounds"] = adam_local("hg_lower_bounds", g_lb, (2 * 2, nlb))
    res["w_mod"] = adam_local("w_mod", g_w_mod, (2 * dm, nsm))
    res["b_mod"] = adam_local("b_mod", g_b_mod, (2, N_MOD * dm))
    res["s5_w_in"] = adam_parts("s5_w_in", p_s5in)
    res["s5_w_glu"] = adam_parts("s5_w_glu", p_glu)
    res["s5_w_out"] = adam_parts("s5_w_out", p_s5out)
    res["hg_w_in"] = adam_parts("hg_w_in", p_hgin)
    res["hg_w_out"] = adam_parts("hg_w_out", p_hgout)
    res["ffn_w_up"] = adam_parts("ffn_w_up", jnp.stack([p_up0, p_up1], axis=1))
    res["ffn_w_down"] = adam_parts("ffn_w_down", jnp.stack([p_dn0, p_dn1], axis=1))
    res["ffn_conv_w"] = adam_parts("ffn_conv_w", p_cw)

    grad_x = jnp.transpose(dz0[nctx:].reshape(lx, bsz, dm), (1, 0, 2))
    order = ["c_ctx", "w_mod", "b_mod", "norm1_w", "norm2_w", "final_norm_w", "s5_w_in", "s5_lam_re", "s5_lam_im",
             "s5_log_step", "s5_b_re", "s5_b_im", "s5_c_re", "s5_c_im", "s5_d", "s5_w_glu", "s5_w_out", "hg_w_in",
             "hg_lower_bounds", "hg_gnorm_w", "hg_w_out", "ffn_w_up", "ffn_conv_w", "ffn_conv_b", "ffn_w_down"]
    outs = [loss, grad_x]
    for j in range(4):
        outs += [res[k][j].reshape(given[k].shape) for k in order]
    return tuple(outs)
```

```python
import functools

import jax
import jax.numpy as jnp
from jax import lax
from jax.experimental import pallas as pl
from jax.experimental.pallas import tpu as pltpu

F32 = jnp.float32
BF16 = jnp.bfloat16
NDEV = 8
LOCAL_B = 4
NORM_EPS = 1e-6
N_MOD = 6
S5_GROUP = 16
S5_STATE = 64
S5_LAM_RE_MAX = -1e-4
HG_HEAD = 128
HG_ROWS = 128
GRID_W = 64
ADAM_LR, ADAM_B1, ADAM_B2, ADAM_EPS, ADAM_WD, ADAM_STEP = 0.001, 0.9, 0.999, 1e-08, 0.01, 10
VMEM_BYTES_V7X = 64 * 1024 * 1024
LANES = 128
SUBLANES = 8

NN = (((1,), (0,)), ((), ()))
NT = (((1,), (1,)), ((), ()))
TN = (((0,), (0,)), ((), ()))
MESH = pl.DeviceIdType.MESH


def _params(sem=None, vmem=None):
    kw = {}
    if sem is not None:
        kw["dimension_semantics"] = sem
    if vmem is not None:
        kw["vmem_limit_bytes"] = int(min(vmem, VMEM_BYTES_V7X - (4 << 20)))
    return pltpu.CompilerParams(**kw)


def _nbytes(shape, dtype):
    n = 1
    for s in shape:
        n *= 1 if s is None else s
    return n * jnp.dtype(dtype).itemsize


def _dot(a, b, dims=NN, precision=None):
    return lax.dot_general(a, b, dims, preferred_element_type=F32, precision=precision)


def _sigmoid(x):
    return 1.0 / (1.0 + jnp.exp(-x))


class _Xchg:
    def __init__(self, arrs, a2a):
        self.arrs, self.a2a, self.n = list(arrs), list(a2a), len(arrs)

    def out_shape(self):
        return [jax.ShapeDtypeStruct(a.shape if f else (NDEV,) + a.shape, a.dtype) for a, f in zip(self.arrs, self.a2a)]

    def scratch(self):
        return [pltpu.SemaphoreType.DMA((self.n * (NDEV - 1),)), pltpu.SemaphoreType.DMA((self.n * (NDEV - 1),)),
                pltpu.SemaphoreType.DMA((self.n,))]

    def _copies(self, ins, outs, sems, with_recvs):
        send_sems, recv_sems, loc_sems = sems
        x, y, c = lax.axis_index("x"), lax.axis_index("y"), lax.axis_index("c")
        me = 4 * x + 2 * y + c
        local, sends, recvs = [], [], []
        for a in range(self.n):
            src = ins[a].at[me] if self.a2a[a] else ins[a]
            local.append(pltpu.make_async_copy(src, outs[a].at[me], loc_sems.at[a]))
            for k in range(1, NDEV):
                px = (1 - x) if (k >> 2) & 1 else x
                py = (1 - y) if (k >> 1) & 1 else y
                pc = (1 - c) if k & 1 else c
                p = 4 * px + 2 * py + pc
                s = a * (NDEV - 1) + k - 1
                src = ins[a].at[p] if self.a2a[a] else ins[a]
                kw = dict(src_ref=src, send_sem=send_sems.at[s], recv_sem=recv_sems.at[s], device_id=(px, py, pc),
                          device_id_type=MESH)
                sends.append(pltpu.make_async_remote_copy(dst_ref=outs[a].at[me], **kw))
                if with_recvs:
                    recvs.append(pltpu.make_async_remote_copy(dst_ref=outs[a].at[p], **kw))
        return local, sends, recvs

    def start(self, ins, outs, sems):
        local, sends, _ = self._copies(ins, outs, sems, False)
        for cp in local + sends:
            cp.start()

    def wait(self, ins, outs, sems):
        local, sends, recvs = self._copies(ins, outs, sems, True)
        for cp in sends:
            cp.wait_send()
        for cp in recvs:
            cp.wait_recv()
        for cp in local:
            cp.wait()


def _exchange(arrs, *, a2a, name):
    xch = _Xchg(arrs, a2a if isinstance(a2a, (list, tuple)) else [a2a] * len(arrs))
    n = xch.n

    def body(*refs):
        xch.start(refs[:n], refs[n:2 * n], refs[2 * n:])
        xch.wait(refs[:n], refs[n:2 * n], refs[2 * n:])

    res = pl.pallas_call(
        body, name=name, out_shape=xch.out_shape(),
        in_specs=[pl.BlockSpec(memory_space=pl.ANY)] * n, out_specs=[pl.BlockSpec(memory_space=pl.ANY)] * n,
        scratch_shapes=xch.scratch(),
    )(*arrs)
    return list(res)


def _call(body, *, name, out_shape, grid, in_specs, out_specs, scratch, params, args, xch=None):
    in_specs, out_specs, out_shape, scratch, args = list(in_specs), list(out_specs), list(out_shape), list(scratch), list(args)
    n_in, n_out, n_scr = len(in_specs), len(out_shape), len(scratch)
    if xch is not None:
        k = xch.n
        inner = body

        def body(*refs):
            ins, xin = refs[:n_in], refs[n_in:n_in + k]
            outs, xout = refs[n_in + k:n_in + k + n_out], refs[n_in + k + n_out:n_in + 2 * k + n_out]
            scr = refs[n_in + 2 * k + n_out:n_in + 2 * k + n_out + n_scr]
            sems = refs[n_in + 2 * k + n_out + n_scr:]
            first = pl.program_id(0) == 0
            last = pl.program_id(0) == grid[0] - 1
            for ax in range(1, len(grid)):
                first = jnp.logical_and(first, pl.program_id(ax) == 0)
                last = jnp.logical_and(last, pl.program_id(ax) == grid[ax] - 1)

            @pl.when(first)
            def _():
                xch.start(xin, xout, sems)

            inner(*ins, *outs, *scr)

            @pl.when(last)
            def _():
                xch.wait(xin, xout, sems)

        anyspec = pl.BlockSpec(memory_space=pl.ANY)
        in_specs += [anyspec] * k
        out_specs += [anyspec] * k
        out_shape += xch.out_shape()
        scratch += xch.scratch()
        args += xch.arrs
    res = pl.pallas_call(body, name=name, out_shape=out_shape, grid=grid, in_specs=in_specs, out_specs=out_specs,
                         scratch_shapes=scratch, compiler_params=params)(*args)
    return list(res[:n_out]), list(res[n_out:])


def _mm(a, b, *, name, grid, a_spec, b_spec, o_spec, o_shape, o_dtype, dims, base=None):
    nk = grid[2]
    o_block = tuple(s for s in o_spec.block_shape if s is not None)

    def body(a_ref, b_ref, *rest):
        base_ref = rest[0] if base is not None else None
        o_ref, scr = rest[1 if base is not None else 0], rest[2 if base is not None else 1:]
        r = _dot(a_ref[...].astype(BF16), b_ref[...].astype(BF16), dims)
        if nk == 1:
            if base is not None:
                r = r + base_ref[...].astype(F32)
            o_ref[...] = r.astype(o_dtype)
        else:
            acc = scr[0]
            k = pl.program_id(2)

            @pl.when(k == 0)
            def _():
                acc[...] = r

            @pl.when(k > 0)
            def _():
                acc[...] += r

            @pl.when(k == nk - 1)
            def _():
                tot = acc[...] if base is None else acc[...] + base_ref[...].astype(F32)
                o_ref[...] = tot.astype(o_dtype)

    blocks = (_nbytes(a_spec.block_shape, a.dtype) + _nbytes(b_spec.block_shape, b.dtype) + _nbytes(o_block, o_dtype)
              + (_nbytes(o_block, base.dtype) if base is not None else 0))
    scratch = [pltpu.VMEM(o_block, F32)] if nk > 1 else []
    vmem = 2 * blocks + 3 * _nbytes(o_block, F32) + (8 << 20)
    return pl.pallas_call(
        body, name=name, out_shape=jax.ShapeDtypeStruct(o_shape, o_dtype), grid=grid,
        in_specs=[a_spec, b_spec] + ([o_spec] if base is not None else []), out_specs=o_spec, scratch_shapes=scratch,
        compiler_params=_params(("parallel", "parallel", "arbitrary"), vmem),
    )(a, b, *([base] if base is not None else []))


def _lin(a, w, *, name, trans_w=False, tm, tn, o_dtype=F32, kblk=0, base=None):
    m, kk = a.shape
    nout = w.shape[0] if trans_w else w.shape[1]
    if trans_w:
        b_spec = pl.BlockSpec((tn, kk), lambda j, i, k: (j, kblk))
    else:
        b_spec = pl.BlockSpec((kk, tn), lambda j, i, k: (0, j))
    return _mm(a, w, name=name, grid=(nout // tn, m // tm, 1), dims=NT if trans_w else NN, o_shape=(m, nout),
               o_dtype=o_dtype, o_spec=pl.BlockSpec((tm, tn), lambda j, i, k: (i, j)),
               a_spec=pl.BlockSpec((tm, kk), lambda j, i, k: (i, 0)), b_spec=b_spec, base=base)


def _lin_w(a, dy, *, name, ta, tn, tkr):
    m, ka = a.shape
    nout = dy.shape[1]
    return _mm(a, dy, name=name, grid=(ka // ta, nout // tn, m // tkr), dims=TN, o_shape=(ka, nout), o_dtype=BF16,
               o_spec=pl.BlockSpec((ta, tn), lambda i, j, k: (i, j)),
               a_spec=pl.BlockSpec((tkr, ta), lambda i, j, k: (k, i)),
               b_spec=pl.BlockSpec((tkr, tn), lambda i, j, k: (k, j)))


def _rowk(fn, *, name, n, tm, nctx, rows=(), pats=(), consts=(), out_rows=(), out_seg=(), out_acc=()):
    nb, ncb = n // tm, nctx // tm
    nr, npat, ncst = len(rows), len(pats), len(consts)
    no, nseg, nacc = len(out_rows), len(out_seg), len(out_acc)
    in_specs, blocks = [], 0
    for arr, w, cb, off in rows:
        in_specs.append(pl.BlockSpec((tm, w), lambda i, cb=cb, off=off: (jnp.maximum(i - off, 0), cb)))
        blocks += _nbytes((tm, w), arr.dtype)
    for p in pats:
        in_specs.append(pl.BlockSpec((None, SUBLANES, p.shape[2]), lambda i: (jnp.where(i >= ncb, 1, 0), 0, 0)))
    for cst in consts:
        in_specs.append(pl.BlockSpec(cst.shape, lambda i: (0, 0)))
    out_shape, out_specs = [], []
    for wt, dt, w, cb in out_rows:
        out_shape.append(jax.ShapeDtypeStruct((n, wt), dt))
        out_specs.append(pl.BlockSpec((tm, w), lambda i, cb=cb: (i, cb)))
        blocks += _nbytes((tm, w), dt)
    for w in out_seg:
        out_shape.append(jax.ShapeDtypeStruct((SUBLANES, w), F32))
        out_specs.append(pl.BlockSpec((SUBLANES, w), lambda i: (0, 0)))
    for w in out_acc:
        out_shape.append(jax.ShapeDtypeStruct((1, w), F32))
        out_specs.append(pl.BlockSpec((1, w), lambda i: (0, 0)))
    scratch = [pltpu.VMEM((2, SUBLANES, w), F32) for w in out_seg] + [pltpu.VMEM((SUBLANES, w), F32) for w in out_acc]

    def body(*refs):
        r_in = refs[:nr]
        p_in = refs[nr:nr + npat]
        c_in = refs[nr + npat:nr + npat + ncst]
        base = nr + npat + ncst
        o_rows = refs[base:base + no]
        o_seg = refs[base + no:base + no + nseg]
        o_acc = refs[base + no + nseg:base + no + nseg + nacc]
        s_seg = refs[base + no + nseg + nacc:base + no + nseg + nacc + nseg]
        s_acc = refs[base + no + nseg + nacc + nseg:]
        i = pl.program_id(0)
        rv = [r[...].astype(F32).reshape(tm // SUBLANES, SUBLANES, r.shape[1]) for r in r_in]
        pv = [p[...] for p in p_in]
        cv = [c[...] for c in c_in]
        is_lat = (i >= ncb).astype(F32)
        ro, so, ao = fn(rv, pv, cv, is_lat)
        for ref, val in zip(o_rows, ro):
            ref[...] = val.reshape(tm, ref.shape[1]).astype(ref.dtype)
        if nseg or nacc:
            @pl.when(i == 0)
            def _():
                for s in list(s_seg) + list(s_acc):
                    s[...] = jnp.zeros(s.shape, F32)

            seg = jnp.where(i >= ncb, 1, 0)
            for s, val in zip(s_seg, so):
                s[seg] = s[seg] + val
            for s, val in zip(s_acc, ao):
                s[...] = s[...] + val

            @pl.when(i == nb - 1)
            def _():
                for o, s in zip(o_seg, s_seg):
                    lat, ctx = s[1], s[0]
                    row = lax.broadcasted_iota(jnp.int32, lat.shape, 0)
                    lat = lat + pltpu.roll(lat, 4, 0)
                    ctx = jnp.broadcast_to(jnp.sum(ctx, axis=0, keepdims=True), lat.shape)
                    o[...] = jnp.where(row < 4, lat, jnp.where(row == 4, ctx, 0.0))
                for o, s in zip(o_acc, s_acc):
                    o[...] = jnp.sum(s[...], axis=0, keepdims=True)

    vmem = 2 * blocks + 8 * tm * 1024 * 4 + (8 << 20)
    res = pl.pallas_call(
        body, name=name, out_shape=out_shape, grid=(nb,), in_specs=in_specs, out_specs=out_specs,
        scratch_shapes=scratch, compiler_params=_params(("arbitrary",), vmem),
    )(*[r[0] for r in rows], *pats, *consts)
    return list(res)


def _rms(z):
    return lax.rsqrt(jnp.mean(z * z, axis=-1, keepdims=True) + NORM_EPS)


def _norm_mod_fwd(z, w, sh, sc, *, name, dims, res=None):
    n, d = z.shape

    def fn(rv, pv, cv, is_lat):
        zz = rv[0]
        if res is not None:
            zz = zz + pv[2][None] * rv[1]
        h = (zz * _rms(zz) * cv[0]) * (1.0 + pv[1][None]) + pv[0][None]
        return ([zz, h] if res is not None else [h]), [], []

    rows = [(z, d, 0, 0)] + ([(res[0], d, 0, 0)] if res is not None else [])
    pats = [sh, sc] + ([res[1]] if res is not None else [])
    outs = ([(d, F32, d, 0)] if res is not None else []) + [(d, BF16, d, 0)]
    out = _rowk(fn, name=name, n=n, tm=dims["tm_row"], nctx=dims["nctx"], rows=rows, pats=pats, consts=[w],
                out_rows=outs)
    return (out[0], out[1]) if res is not None else (None, out[0])


def _norm_core_bwd(zin, dh, w, sc):
    r = _rms(zin)
    xh = zin * r
    dsh = jnp.sum(dh, axis=0)
    dsc = jnp.sum(dh * (xh * w), axis=0)
    dyv = dh * (1.0 + sc[None])
    dw = jnp.sum(dyv * xh, axis=0)
    dxh = dyv * w
    dx = r * (dxh - xh * jnp.mean(dxh * xh, axis=-1, keepdims=True))
    return dx, dsh, dsc, dw


def _norm_mod_bwd(dh, zin, dz_up, w, sc, *, name, dims, res=None):
    n, d = zin.shape

    def fn(rv, pv, cv, is_lat):
        dx, dsh, dsc, dw = _norm_core_bwd(rv[1], rv[0], cv[0], pv[0])
        dz = rv[2] + dx
        if res is None:
            return [dz], [dsh, dsc], [dw]
        return [dz, dz * pv[1][None]], [dsh, dsc, jnp.sum(dz * rv[3], axis=0)], [dw]

    rows = [(dh, d, 0, 0), (zin, d, 0, 0), (dz_up, d, 0, 0)] + ([(res[0], d, 0, 0)] if res is not None else [])
    pats = [sc] + ([res[1]] if res is not None else [])
    outs = [(d, F32, d, 0)] + ([(d, BF16, d, 0)] if res is not None else [])
    return _rowk(fn, name=name, n=n, tm=dims["tm_row"], nctx=dims["nctx"], rows=rows, pats=pats, consts=[w],
                 out_rows=outs, out_seg=[d] * (3 if res is not None else 2), out_acc=[d])


def _loss_bwd(z1, f, gate, tgt, w, *, name, dims):
    n, d = z1.shape

    def fn(rv, pv, cv, is_lat):
        z2 = rv[0] + pv[0][None] * rv[1]
        r = _rms(z2)
        xh = z2 * r
        err = (xh * cv[0] - rv[2]) * is_lat
        dout = err * (1.0 / d)
        dxh = dout * cv[0]
        dz = r * (dxh - xh * jnp.mean(dxh * xh, axis=-1, keepdims=True))
        return ([dz, dz * pv[0][None]], [jnp.sum(dz * rv[1], axis=0)],
                [jnp.sum(0.5 * err * err * (1.0 / d), axis=0), jnp.sum(dout * xh, axis=0)])

    tm = dims["tm_row"]
    rows = [(z1, d, 0, 0), (f, d, 0, 0), (tgt, d, 0, dims["nctx"] // tm)]
    return _rowk(fn, name=name, n=n, tm=tm, nctx=dims["nctx"], rows=rows, pats=[gate], consts=[w],
                 out_rows=[(d, F32, d, 0), (d, BF16, d, 0)], out_seg=[d], out_acc=[d, d])


def _gelu(y):
    return jax.nn.gelu(y, approximate=True)


def _conv_masks(tb, i):
    tok = lax.broadcasted_iota(jnp.int32, (tb, 1), 0) >> 2
    last = jnp.where(i == 0, tb // LOCAL_B - 1, GRID_W - 1)
    wpos = tok & last
    return wpos == 0, wpos == last


CONV_LANES = 2 * LANES


def _conv_taps(u_ref, cw_ref, cb_ref, no_left, no_right, tb):
    uu = u_ref[...]
    ul = jnp.where(no_left, 0.0, pltpu.roll(uu, LOCAL_B, 0))
    ur = jnp.where(no_right, 0.0, pltpu.roll(uu, tb - LOCAL_B, 0))
    val = cb_ref[...] + ul * cw_ref[pl.ds(0, 1), :] + uu * cw_ref[pl.ds(1, 1), :] + ur * cw_ref[pl.ds(2, 1), :]
    return val, ul, uu, ur


def _convffn_specs(tb, nj):
    cl = CONV_LANES
    return [pl.BlockSpec((tb, cl), lambda j, i: (i, j)), pl.BlockSpec((tb, cl), lambda j, i: (i, nj + j)),
            pl.BlockSpec((3, cl), lambda j, i: (0, j)), pl.BlockSpec((3, cl), lambda j, i: (0, nj + j)),
            pl.BlockSpec((1, cl), lambda j, i: (0, j)), pl.BlockSpec((1, cl), lambda j, i: (0, nj + j))]


def _convffn_fwd(u, cw, cb, *, name, dims):
    n, f2 = u.shape
    tb, nj = dims["nctx"], f2 // 2 // CONV_LANES

    def body(ua_ref, ug_ref, cwa_ref, cwg_ref, cba_ref, cbg_ref, o_ref):
        no_left, no_right = _conv_masks(tb, pl.program_id(1))
        a = _conv_taps(ua_ref, cwa_ref, cba_ref, no_left, no_right, tb)[0]
        g = _conv_taps(ug_ref, cwg_ref, cbg_ref, no_left, no_right, tb)[0]
        o_ref[...] = (a * _sigmoid(a) * g).astype(BF16)

    vmem = 16 * tb * CONV_LANES * 4 + (8 << 20)
    return pl.pallas_call(
        body, name=name, out_shape=jax.ShapeDtypeStruct((n, f2 // 2), BF16), grid=(nj, n // tb),
        in_specs=_convffn_specs(tb, nj), out_specs=pl.BlockSpec((tb, CONV_LANES), lambda j, i: (i, j)),
        compiler_params=_params(("parallel", "arbitrary"), vmem),
    )(u, u, cw, cw, cb, cb)


def _convffn_bwd(u, dhm, cw, cb, *, name, dims, xch=None):
    n, f2 = u.shape
    tb, nj = dims["nctx"], f2 // 2 // CONV_LANES

    def body(ua_ref, ug_ref, cwa_ref, cwg_ref, cba_ref, cbg_ref, dh_ref, dua_ref, dug_ref, dcwa_ref, dcwg_ref, dcba_ref,
             dcbg_ref):
        i = pl.program_id(1)
        no_left, no_right = _conv_masks(tb, i)

        @pl.when(i == 0)
        def _():
            for ref in (dcwa_ref, dcwg_ref, dcba_ref, dcbg_ref):
                ref[...] = jnp.zeros(ref.shape, F32)

        a, al, ac, ar = _conv_taps(ua_ref, cwa_ref, cba_ref, no_left, no_right, tb)
        g, gl, gc, gr = _conv_taps(ug_ref, cwg_ref, cbg_ref, no_left, no_right, tb)
        dh = dh_ref[...].astype(F32)
        sa = _sigmoid(a)
        dg = dh * (a * sa)
        da = dh * g * (sa * (1.0 + a * (1.0 - sa)))
        for dc, (tl, tc, tr), cw_ref, du_ref, dcw_ref, dcb_ref in (
                (da, (al, ac, ar), cwa_ref, dua_ref, dcwa_ref, dcba_ref),
                (dg, (gl, gc, gr), cwg_ref, dug_ref, dcwg_ref, dcbg_ref)):
            dcb_ref[...] += jnp.sum(dc, axis=0, keepdims=True)
            dcw_ref[pl.ds(0, 1), :] += jnp.sum(dc * tl, axis=0, keepdims=True)
            dcw_ref[pl.ds(1, 1), :] += jnp.sum(dc * tc, axis=0, keepdims=True)
            dcw_ref[pl.ds(2, 1), :] += jnp.sum(dc * tr, axis=0, keepdims=True)
            du = (dc * cw_ref[pl.ds(1, 1), :]
                  + pltpu.roll(jnp.where(no_left, 0.0, dc) * cw_ref[pl.ds(0, 1), :], tb - LOCAL_B, 0)
                  + pltpu.roll(jnp.where(no_right, 0.0, dc) * cw_ref[pl.ds(2, 1), :], LOCAL_B, 0))
            du_ref[...] = du.astype(BF16)

    cl, f = CONV_LANES, f2 // 2
    sd = jax.ShapeDtypeStruct
    row = pl.BlockSpec((tb, cl), lambda j, i: (i, j))
    vmem = 24 * tb * cl * 4 + (8 << 20)
    return _call(
        body, name=name, xch=xch, args=[u, u, cw, cw, cb, cb, dhm], scratch=[],
        out_shape=[sd((n, f), BF16), sd((n, f), BF16), sd((3, f), F32), sd((3, f), F32), sd((1, f), F32), sd((1, f), F32)],
        grid=(nj, n // tb), in_specs=_convffn_specs(tb, nj) + [row],
        out_specs=[row, row, pl.BlockSpec((3, cl), lambda j, i: (0, j)), pl.BlockSpec((3, cl), lambda j, i: (0, j)),
                   pl.BlockSpec((1, cl), lambda j, i: (0, j)), pl.BlockSpec((1, cl), lambda j, i: (0, j))],
        params=_params(("arbitrary", "arbitrary"), vmem))


def _s5_disc(lr, li, ls, brt, bit):
    lr = jnp.minimum(lr, S5_LAM_RE_MAX)
    dt = jnp.exp(ls)
    mag = jnp.exp(lr * dt)
    ar = mag * jnp.cos(li * dt)
    ai = mag * jnp.sin(li * dt)
    den = lr * lr + li * li
    nr = ar - 1.0
    cr = (nr * lr + ai * li) / den
    ci = (ai * lr - nr * li) / den
    return ar, ai, cr * brt - ci * bit, cr * bit + ci * brt


def _s5_disc_fwd(lr, li, ls, brt, bit):
    def body(lr_ref, li_ref, ls_ref, br_ref, bi_ref, ar_ref, ai_ref, bbr_ref, bbi_ref):
        ar, ai, bbr, bbi = _s5_disc(lr_ref[...], li_ref[...], ls_ref[...], br_ref[...], bi_ref[...])
        ar_ref[...] = ar
        ai_ref[...] = ai
        bbr_ref[...] = bbr
        bbi_ref[...] = bbi

    sd = jax.ShapeDtypeStruct
    return pl.pallas_call(body, name="s5_disc_fwd",
                          out_shape=[sd(lr.shape, F32), sd(lr.shape, F32), sd(brt.shape, F32), sd(brt.shape, F32)],
                          compiler_params=_params(None, 32 << 20))(lr, li, ls, brt, bit)


def _s5_disc_bwd(lr, li, ls, brt, bit, dar, dai, dbbr, dbbi):
    def body(lr_ref, li_ref, ls_ref, br_ref, bi_ref, dar_ref, dai_ref, dbbr_ref, dbbi_ref,
             dlr_ref, dli_ref, dls_ref, dbr_ref, dbi_ref):
        _, vjp = jax.vjp(_s5_disc, lr_ref[...], li_ref[...], ls_ref[...], br_ref[...], bi_ref[...])
        dlr, dli, dls, dbr, dbi = vjp((dar_ref[...], dai_ref[...], dbbr_ref[...], dbbi_ref[...]))
        dlr_ref[...] = dlr
        dli_ref[...] = dli
        dls_ref[...] = dls
        dbr_ref[...] = dbr
        dbi_ref[...] = dbi

    sd = jax.ShapeDtypeStruct
    return pl.pallas_call(body, name="s5_disc_bwd",
                          out_shape=[sd(lr.shape, F32), sd(lr.shape, F32), sd(ls.shape, F32), sd(brt.shape, F32),
                                     sd(brt.shape, F32)],
                          compiler_params=_params(None, 48 << 20))(lr, li, ls, brt, bit, dar, dai, dbbr, dbbi)


def _cmul(ar, ai, xr, xi):
    return ar * xr - ai * xi, ar * xi + ai * xr


def _scan_consts(a_r, a_i, rev):
    row = lax.broadcasted_iota(jnp.int32, a_r.shape, 0)
    second = (row < 4) if rev else (row >= 4)
    a2r, a2i = _cmul(a_r, a_i, a_r, a_i)
    a1r, a1i = jnp.where(second, a_r, 0.0), jnp.where(second, a_i, 0.0)
    apr, api = jnp.where(second, a2r, a_r), jnp.where(second, a2i, a_i)
    return second, a1r, a1i, apr, api


def _scan_tile(xr, xi, pr, pi, consts):
    second, a1r, a1i, apr, api = consts
    sr, si = pltpu.roll(xr, 4, 0), pltpu.roll(xi, 4, 0)
    t1r, t1i = _cmul(a1r, a1i, sr, si)
    t2r, t2i = _cmul(apr, api, pr, pi)
    yr, yi = xr + t1r + t2r, xi + t1i + t2i
    npr = jnp.where(second, yr, pltpu.roll(yr, 4, 0))
    npi = jnp.where(second, yi, pltpu.roll(yi, 4, 0))
    return yr, yi, npr, npi


def _s5_scan(xr_ref, xi_ref, row0, nrows, a_r_ref, a_i_ref, cr_ref, ci_ref, *, rev, conj, lane_block, extra=None):
    width = xr_ref.shape[1]
    nt = nrows // SUBLANES
    for lb in range(width // lane_block):
        lanes = pl.ds(lb * lane_block, lane_block)
        a_r = a_r_ref[:, lanes]
        a_i = a_i_ref[:, lanes]
        if conj:
            a_i = -a_i
        consts = _scan_consts(a_r, a_i, rev)

        def step(t, carry):
            pr, pi = carry[0], carry[1]
            j = (nt - 1 - t) if rev else t
            rows = pl.ds(pl.multiple_of(row0 + j * SUBLANES, SUBLANES), SUBLANES)
            yr, yi, pr, pi = _scan_tile(xr_ref[rows, lanes], xi_ref[rows, lanes], pr, pi, consts)
            xr_ref[rows, lanes] = yr
            xi_ref[rows, lanes] = yi
            if extra is None:
                return pr, pi
            return (pr, pi) + tuple(extra(j, lanes, yr, yi, carry[2:]))

        init = (cr_ref[:, lanes], ci_ref[:, lanes])
        if extra is not None:
            init = init + tuple(extra.init(lanes))
        out = lax.fori_loop(0, nt, step, init)
        cr_ref[:, lanes] = out[0]
        ci_ref[:, lanes] = out[1]
        if extra is not None:
            extra.done(lanes, out[2:])


def _s5_chunk_of(step, ncc, nc, rev):
    if not rev:
        return step
    return jnp.where(step < ncc, ncc - 1 - step, nc - 1 - (step - ncc))


def _s5_fwd_dir(u, base, a8r, a8i, bbr, bbi, ccr, cci, dsk, *, d, name, dims, xch=None):
    n, dm = u.shape
    nk, swk = bbr.shape[0], bbr.shape[2]
    rr, sw = dims["s5_rows"], nk * swk
    nc, ncc = n // rr, dims["nctx"] // rr
    rev = d == 1
    cmap = lambda i: (_s5_chunk_of(i, ncc, nc, rev), 0)

    def body(*refs):
        if d == 0:
            u_ref, a8r_ref, a8i_ref, bbr_ref, bbi_ref, ccr_ref, cci_ref, dsk_ref = refs[:8]
            rest = refs[8:]
        else:
            u_ref, base_ref, a8r_ref, a8i_ref, bbr_ref, bbi_ref, ccr_ref, cci_ref = refs[:8]
            rest = refs[8:]
        y_ref, str_ref, sti_ref, sr, si, cr, ci = rest
        i = pl.program_id(0)

        @pl.when(i == 0)
        def _():
            cr[...] = jnp.zeros(cr.shape, F32)
            ci[...] = jnp.zeros(ci.shape, F32)

        str_ref[...] = cr[...]
        sti_ref[...] = ci[...]
        ub = u_ref[...].astype(BF16)
        for k in range(nk):
            uk = ub[:, k * LANES:(k + 1) * LANES]
            sr[:, k * swk:(k + 1) * swk] = _dot(uk, bbr_ref[k])
            si[:, k * swk:(k + 1) * swk] = _dot(uk, bbi_ref[k])
        _s5_scan(sr, si, 0, rr, a8r_ref, a8i_ref, cr, ci, rev=rev, conj=False, lane_block=dims["s5_lane_block"])
        for k in range(nk):
            cols = slice(k * LANES, (k + 1) * LANES)
            yk = (_dot(sr[:, k * swk:(k + 1) * swk].astype(BF16), ccr_ref[k])
                  - _dot(si[:, k * swk:(k + 1) * swk].astype(BF16), cci_ref[k]))
            if d == 0:
                y_ref[:, cols] = yk + dsk_ref[:, cols] * u_ref[:, cols]
            else:
                y_ref[:, cols] = yk + base_ref[:, cols]

    row_spec = pl.BlockSpec((rr, dm), cmap)
    full = lambda a: pl.BlockSpec(a.shape, lambda i: (0,) * a.ndim)
    ins = [u] + ([] if d == 0 else [base]) + [a8r, a8i, bbr, bbi, ccr, cci] + ([dsk] if d == 0 else [])
    in_specs = [row_spec] + ([] if d == 0 else [row_spec]) + [full(a) for a in (a8r, a8i, bbr, bbi, ccr, cci)]
    in_specs += [full(dsk)] if d == 0 else []
    st_spec = pl.BlockSpec((None, SUBLANES, sw), lambda i: (_s5_chunk_of(i, ncc, nc, rev), 0, 0))
    vmem = 2 * rr * sw * 4 + 6 * rr * dm * 4 + 8 * nk * LANES * swk * 2 + (12 << 20)
    return _call(
        body, name=name, xch=xch, args=ins,
        out_shape=[jax.ShapeDtypeStruct((n, dm), F32), jax.ShapeDtypeStruct((nc, SUBLANES, sw), F32),
                   jax.ShapeDtypeStruct((nc, SUBLANES, sw), F32)],
        grid=(nc,), in_specs=in_specs, out_specs=[row_spec, st_spec, st_spec],
        scratch=[pltpu.VMEM((rr, sw), F32), pltpu.VMEM((rr, sw), F32), pltpu.VMEM((SUBLANES, sw), F32),
                 pltpu.VMEM((SUBLANES, sw), F32)],
        params=_params(("arbitrary",), vmem))


class _DaHook:
    def __init__(self, sr, si, accr, acci, rev_fwd):
        self.sr, self.si, self.accr, self.acci, self.rev_fwd = sr, si, accr, acci, rev_fwd

    def init(self, lanes):
        return self.accr[:, lanes], self.acci[:, lanes]

    def done(self, lanes, acc):
        self.accr[:, lanes] = acc[0]
        self.acci[:, lanes] = acc[1]

    def __call__(self, j, lanes, lr, li, acc):
        base = pl.multiple_of(SUBLANES + j * SUBLANES, SUBLANES)
        cur = pl.ds(base, SUBLANES)
        row = lax.broadcasted_iota(jnp.int32, lr.shape, 0)
        if self.rev_fwd:
            oth = pl.ds(pl.multiple_of(base + SUBLANES, SUBLANES), SUBLANES)
            spr = pltpu.roll(jnp.where(row >= 4, self.sr[cur, lanes], self.sr[oth, lanes]), 4, 0)
            spi = pltpu.roll(jnp.where(row >= 4, self.si[cur, lanes], self.si[oth, lanes]), 4, 0)
        else:
            oth = pl.ds(pl.multiple_of(base - SUBLANES, SUBLANES), SUBLANES)
            spr = pltpu.roll(jnp.where(row >= 4, self.sr[oth, lanes], self.sr[cur, lanes]), 4, 0)
            spi = pltpu.roll(jnp.where(row >= 4, self.si[oth, lanes], self.si[cur, lanes]), 4, 0)
        return acc[0] + spr * lr + spi * li, acc[1] + spr * li - spi * lr


def _s5_bwd_dir(u, dy, du_prev, a8r, a8i, bbr, bbi, ccr, cci, dsk, st_r, st_i, *, d, name, dims, xch=None):
    n, dm = u.shape
    rr = dims["s5_rows"]
    nk, swk = bbr.shape[0], bbr.shape[2]
    sw = nk * swk
    nc, ncc = n // rr, dims["nctx"] // rr
    rev = d == 1
    chunk = lambda i: _s5_chunk_of(nc - 1 - i, ncc, nc, rev)

    def body(*refs):
        u_ref, dy_ref = refs[0], refs[1]
        pos = 2
        dup_ref = None
        if d == 1:
            dup_ref = refs[pos]
            pos += 1
        a8r_ref, a8i_ref, bbr_ref, bbi_ref, ccr_ref, cci_ref = refs[pos:pos + 6]
        pos += 6
        dsk_ref = None
        if d == 0:
            dsk_ref = refs[pos]
            pos += 1
        str_ref, sti_ref = refs[pos:pos + 2]
        pos += 2
        du_ref, dbbr_ref, dbbi_ref, dccr_ref, dcci_ref, dar_ref, dai_ref = refs[pos:pos + 7]
        pos += 7
        dd_ref = None
        if d == 0:
            dd_ref = refs[pos]
            pos += 1
        sr, si, lr, li, cr, ci, lcr, lci, accr, acci, dda = refs[pos:]
        i = pl.program_id(0)

        @pl.when(i == 0)
        def _():
            for ref in (lcr, lci, accr, acci, dda, dbbr_ref, dbbi_ref, dccr_ref, dcci_ref):
                ref[...] = jnp.zeros(ref.shape, F32)

        cr[...] = str_ref[...]
        ci[...] = sti_ref[...]
        spare = pl.ds(rr + SUBLANES, SUBLANES) if rev else pl.ds(0, SUBLANES)
        sr[spare, :] = str_ref[...]
        si[spare, :] = sti_ref[...]
        ub = u_ref[...].astype(BF16)
        dyb = dy_ref[...].astype(BF16)
        for k in range(nk):
            uk = ub[:, k * LANES:(k + 1) * LANES]
            sr[pl.ds(SUBLANES, rr), k * swk:(k + 1) * swk] = _dot(uk, bbr_ref[k])
            si[pl.ds(SUBLANES, rr), k * swk:(k + 1) * swk] = _dot(uk, bbi_ref[k])
        _s5_scan(sr, si, SUBLANES, rr, a8r_ref, a8i_ref, cr, ci, rev=rev, conj=False,
                 lane_block=dims["s5_lane_block"])
        for k in range(nk):
            dyk = dyb[:, k * LANES:(k + 1) * LANES]
            sl = slice(k * swk, (k + 1) * swk)
            lr[:, sl] = _dot(dyk, ccr_ref[k], NT)
            li[:, sl] = -_dot(dyk, cci_ref[k], NT)
            dccr_ref[k] += _dot(sr[pl.ds(SUBLANES, rr), sl].astype(BF16), dyk, TN)
            dcci_ref[k] -= _dot(si[pl.ds(SUBLANES, rr), sl].astype(BF16), dyk, TN)
        _s5_scan(lr, li, 0, rr, a8r_ref, a8i_ref, lcr, lci, rev=not rev, conj=True,
                 lane_block=dims["s5_lane_block"], extra=_DaHook(sr, si, accr, acci, rev))
        for k in range(nk):
            cols = slice(k * LANES, (k + 1) * LANES)
            sl = slice(k * swk, (k + 1) * swk)
            uk = ub[:, cols]
            lrk, lik = lr[:, sl].astype(BF16), li[:, sl].astype(BF16)
            dbbr_ref[k] += _dot(uk, lrk, TN)
            dbbi_ref[k] += _dot(uk, lik, TN)
            duk = _dot(lrk, bbr_ref[k], NT) + _dot(lik, bbi_ref[k], NT)
            if d == 0:
                du_ref[:, cols] = duk + dsk_ref[:, cols] * dy_ref[:, cols]
            else:
                du_ref[:, cols] = duk + dup_ref[:, cols]
        if d == 0:
            prod = (dy_ref[...] * u_ref[...]).reshape(rr // SUBLANES, SUBLANES, dm)
            dda[...] += jnp.sum(prod, axis=0)

        @pl.when(i == nc - 1)
        def _():
            dar_ref[...] = jnp.sum(accr[...], axis=0, keepdims=True)
            dai_ref[...] = jnp.sum(acci[...], axis=0, keepdims=True)
            if d == 0:
                dd_ref[...] = jnp.sum(dda[...], axis=0, keepdims=True)

    row_spec = pl.BlockSpec((rr, dm), lambda i: (chunk(i), 0))
    full = lambda a: pl.BlockSpec(a.shape, lambda i: (0,) * a.ndim)
    st_spec = pl.BlockSpec((None, SUBLANES, sw), lambda i: (chunk(i), 0, 0))
    ins = [u, dy] + ([du_prev] if d == 1 else []) + [a8r, a8i, bbr, bbi, ccr, cci] + ([dsk] if d == 0 else [])
    ins += [st_r, st_i]
    in_specs = [row_spec, row_spec] + ([row_spec] if d == 1 else []) + [full(a) for a in (a8r, a8i, bbr, bbi, ccr, cci)]
    in_specs += ([full(dsk)] if d == 0 else []) + [st_spec, st_spec]
    sd = jax.ShapeDtypeStruct
    out_shape = [sd((n, dm), F32), sd(bbr.shape, F32), sd(bbr.shape, F32), sd(ccr.shape, F32),
                 sd(ccr.shape, F32), sd((1, sw), F32), sd((1, sw), F32)] + ([sd((1, dm), F32)] if d == 0 else [])
    out_specs = [row_spec] + [pl.BlockSpec(s.shape, lambda i, nd=len(s.shape): (0,) * nd) for s in out_shape[1:]]
    scratch = [pltpu.VMEM((rr + 2 * SUBLANES, sw), F32), pltpu.VMEM((rr + 2 * SUBLANES, sw), F32),
               pltpu.VMEM((rr, sw), F32), pltpu.VMEM((rr, sw), F32)]
    scratch += [pltpu.VMEM((SUBLANES, sw), F32)] * 6 + [pltpu.VMEM((SUBLANES, dm), F32)]
    vmem = 4 * (rr + 16) * sw * 4 + 10 * rr * dm * 4 + 24 * nk * LANES * swk * 4 + (12 << 20)
    return _call(body, name=name, xch=xch, args=ins, out_shape=out_shape, grid=(nc,), in_specs=in_specs,
                 out_specs=out_specs, scratch=scratch, params=_params(("arbitrary",), vmem))


def _hg_mask(kind, rev):
    if kind == "tot":
        r = lax.broadcasted_iota(jnp.int32, (SUBLANES, HG_ROWS), 0)
        c = lax.broadcasted_iota(jnp.int32, (SUBLANES, HG_ROWS), 1)
        return (c & 3) == r
    r = lax.broadcasted_iota(jnp.int32, (HG_ROWS, HG_ROWS), 0)
    c = lax.broadcasted_iota(jnp.int32, (HG_ROWS, HG_ROWS), 1)
    same = (r & 3) == (c & 3)
    before = ((c >> 2) >= (r >> 2)) if rev else ((c >> 2) <= (r >> 2))
    return jnp.logical_and(same, before if kind == "cum" else jnp.logical_not(before))


def _split2(x):
    hi = x.astype(BF16)
    return hi, (x - hi.astype(F32)).astype(BF16)


@functools.partial(jax.custom_vjp, nondiff_argnums=(1, 2))
def _mask_sum(x, kind, rev):
    m = _hg_mask(kind, rev).astype(BF16)
    hi, lo = _split2(x)
    return _dot(m, hi) + _dot(m, lo)


def _mask_sum_fwd(x, kind, rev):
    return _mask_sum(x, kind, rev), None


def _mask_sum_bwd(kind, rev, _, g):
    m = _hg_mask(kind, rev).astype(BF16)
    hi, lo = _split2(g)
    return (_dot(m, hi, TN) + _dot(m, lo, TN),)


_mask_sum.defvjp(_mask_sum_fwd, _mask_sum_bwd)


def _hg_chunk(q, v, fraw, l0, l1, st, *, rev):
    nh = q.shape[1] // HG_HEAD
    lb = _sigmoid(l1 - l0)
    logf = jnp.logaddexp(jnp.log(lb), jnp.log1p(-lb) + jax.nn.log_sigmoid(fraw))
    kk = (1.0 - lb) * _sigmoid(fraw * -1.0)
    tri = _hg_mask("cum", rev)
    bcum = _mask_sum(logf, "cum", rev)
    brem = _mask_sum(logf, "rem", rev)
    bend8 = _mask_sum(logf, "tot", rev)
    r8d = lax.broadcasted_iota(jnp.int32, bend8.shape, 0)
    decs = [jnp.exp(jnp.sum(jnp.where(r8d == b, bend8, 0.0), axis=0, keepdims=True)) for b in range(LOCAL_B)]
    qd = (q * jnp.exp(bcum)).astype(BF16)
    kd = (kk * jnp.exp(-bcum)).astype(BF16)
    ke = (kk * jnp.exp(brem)).astype(BF16)
    rowb = lax.broadcasted_iota(jnp.int32, (HG_ROWS, 1), 0) & 3
    masks = [(rowb == b).astype(F32) for b in range(LOCAL_B)]
    outs, new = [], []
    for h in range(nh):
        sl = slice(h * HG_HEAD, (h + 1) * HG_HEAD)
        vh = v[:, sl]
        att = jnp.where(tri, _dot(qd[:, sl], kd[:, sl], NT), 0.0)
        o = _dot(att.astype(BF16), vh.astype(BF16))
        for b in range(LOCAL_B):
            sb = st[h * LOCAL_B + b]
            o = o + masks[b] * _dot(qd[:, sl], sb.astype(BF16), NT)
            new.append(sb * decs[b][:, sl] + _dot((vh * masks[b]).astype(BF16), ke[:, sl], TN))
        outs.append(o)
    return jnp.concatenate(outs, axis=1), tuple(new)


def _hg_chunk_of(step, ncc, nc, rev):
    return _s5_chunk_of(step, ncc, nc, rev)


def _hg_fwd_dir(zz, lb2, *, d, name, dims, xch=None):
    n = zz.shape[0]
    dm = zz.shape[1] // 5
    ns = (dm // HG_HEAD) * LOCAL_B
    nc, ncc = n // HG_ROWS, dims["nctx"] // HG_ROWS
    rev = d == 1
    ch = lambda i: _hg_chunk_of(i, ncc, nc, rev)

    def body(q_ref, v_ref, f_ref, l0_ref, l1_ref, o_ref, st_ref, st):
        @pl.when(pl.program_id(0) == 0)
        def _():
            st[...] = jnp.zeros(st.shape, F32)

        st_ref[...] = st[...]
        o, new = _hg_chunk(q_ref[...], v_ref[...], f_ref[...], l0_ref[...], l1_ref[...],
                           tuple(st[j] for j in range(ns)), rev=rev)
        o_ref[...] = o
        for j in range(ns):
            st[j] = new[j]

    blk = lambda off: pl.BlockSpec((HG_ROWS, dm), lambda i, off=off: (ch(i), off))
    lspec = lambda layer: pl.BlockSpec((None, None, 1, dm), lambda i, layer=layer: (d, layer, 0, 0))
    return _call(
        body, name=name, xch=xch, args=[zz, zz, zz, lb2, lb2],
        out_shape=[jax.ShapeDtypeStruct((n, dm), F32), jax.ShapeDtypeStruct((nc, ns, HG_HEAD, HG_HEAD), F32)],
        grid=(nc,),
        in_specs=[blk(0), blk(1), blk(2 + d), lspec(0), lspec(1)],
        out_specs=[pl.BlockSpec((HG_ROWS, dm), lambda i: (ch(i), 0)),
                   pl.BlockSpec((None, ns, HG_HEAD, HG_HEAD), lambda i: (ch(i), 0, 0, 0))],
        scratch=[pltpu.VMEM((ns, HG_HEAD, HG_HEAD), F32)],
        params=_params(("arbitrary",), 48 << 20))


def _hg_bwd_dir(zz, lb2, do, sts, dqv_prev, *, d, name, dims, xch=None):
    n = zz.shape[0]
    dm = zz.shape[1] // 5
    ns = (dm // HG_HEAD) * LOCAL_B
    nc, ncc = n // HG_ROWS, dims["nctx"] // HG_ROWS
    rev = d == 1
    ch = lambda i: _hg_chunk_of(nc - 1 - i, ncc, nc, rev)
    qv_dtype = F32 if d == 0 else BF16

    def body(*refs):
        q_ref, v_ref, f_ref, l0_ref, l1_ref, do_ref, st_ref = refs[:7]
        pos = 7
        if d == 1:
            dqp_ref, dvp_ref = refs[7:9]
            pos = 9
        dq_ref, dv_ref, df_ref, dl_ref, dst = refs[pos:]
        i = pl.program_id(0)

        @pl.when(i == 0)
        def _():
            dst[...] = jnp.zeros(dst.shape, F32)
            dl_ref[...] = jnp.zeros(dl_ref.shape, F32)

        _, vjp = jax.vjp(functools.partial(_hg_chunk, rev=rev), q_ref[...], v_ref[...], f_ref[...], l0_ref[...],
                         l1_ref[...], tuple(st_ref[j] for j in range(ns)))
        dq, dv, df, dl0, dl1, dstn = vjp((do_ref[...], tuple(dst[j] for j in range(ns))))
        for j in range(ns):
            dst[j] = dstn[j]
        if d == 1:
            dq = dq + dqp_ref[...]
            dv = dv + dvp_ref[...]
        dq_ref[...] = dq.astype(qv_dtype)
        dv_ref[...] = dv.astype(qv_dtype)
        df_ref[...] = df.astype(BF16)
        dl_ref[0] += dl0
        dl_ref[1] += dl1

    blk = lambda off: pl.BlockSpec((HG_ROWS, dm), lambda i, off=off: (ch(i), off))
    oblk = pl.BlockSpec((HG_ROWS, dm), lambda i: (ch(i), 0))
    lspec = lambda layer: pl.BlockSpec((None, None, 1, dm), lambda i, layer=layer: (d, layer, 0, 0))
    ins = [zz, zz, zz, lb2, lb2, do, sts] + (list(dqv_prev) if d == 1 else [])
    in_specs = [blk(0), blk(1), blk(2 + d), lspec(0), lspec(1), oblk,
                pl.BlockSpec((None, ns, HG_HEAD, HG_HEAD), lambda i: (ch(i), 0, 0, 0))]
    in_specs += [oblk, oblk] if d == 1 else []
    sd = jax.ShapeDtypeStruct
    return _call(
        body, name=name, xch=xch, args=ins,
        out_shape=[sd((n, dm), qv_dtype), sd((n, dm), qv_dtype), sd((n, dm), BF16), sd((2, 1, dm), F32)],
        grid=(nc,), in_specs=in_specs,
        out_specs=[oblk, oblk, oblk, pl.BlockSpec((2, 1, dm), lambda i: (0, 0, 0))],
        scratch=[pltpu.VMEM((ns, HG_HEAD, HG_HEAD), F32)],
        params=_params(("arbitrary",), 56 << 20))


def _hg_readout(o, g, w):
    outs = []
    for h in range(o.shape[-1] // HG_HEAD):
        sl = slice(h * HG_HEAD, (h + 1) * HG_HEAD)
        oh = o[..., sl]
        outs.append(oh * _rms(oh) * w * _sigmoid(g[..., sl]))
    return jnp.concatenate(outs, axis=-1)


def _silu(x):
    return x * _sigmoid(x)


def _mod_fwd(craw, w, b):
    def body(c_ref, w_ref, b_ref, o_ref):
        s = _silu(c_ref[...]).astype(BF16)
        for layer in range(w.shape[0]):
            o_ref[layer] = _dot(s, w_ref[layer].astype(BF16)) + b_ref[layer]

    return pl.pallas_call(body, name="mod_fwd",
                          out_shape=jax.ShapeDtypeStruct((w.shape[0], craw.shape[0], w.shape[2]), F32),
                          compiler_params=_params(None, 40 << 20))(craw, w, b)


def _mod_bwd(craw, w, dlat_sh, dctx_sh, dlat_full, dctx_full):
    nl, dm, ns = w.shape
    nb = dlat_sh.shape[1]

    def body(c_ref, w_ref, dl_ref, dc_ref, dlf_ref, dcf_ref, dw_ref, db_ref, dcc_ref):
        craw_v = c_ref[...]
        s = _silu(craw_v)
        s_lat = s[:nb].astype(BF16)
        s_ctx = s[nb:].astype(BF16)
        row = lax.broadcasted_iota(jnp.int32, (SUBLANES, ns), 0)
        dsc = jnp.zeros((SUBLANES, dm), F32)
        for layer in range(nl):
            tot = dc_ref[0, pl.ds(layer, 1), :]
            totf = dcf_ref[0, pl.ds(layer, 1), :]
            for i in range(1, NDEV):
                tot = tot + dc_ref[i, pl.ds(layer, 1), :]
                totf = totf + dcf_ref[i, pl.ds(layer, 1), :]
            dc8 = jnp.where(row == 0, jnp.broadcast_to(tot, (SUBLANES, ns)), 0.0).astype(BF16)
            dw_ref[layer] = _dot(s_lat, dl_ref[layer].astype(BF16), TN) + _dot(s_ctx, dc8, TN)
            db_ref[layer] = jnp.sum(dlf_ref[layer], axis=0, keepdims=True) + totf
            dsc = dsc + _dot(dc8, w_ref[layer].astype(BF16), NT)
        cc = craw_v[nb:]
        sg = _sigmoid(cc)
        dcc_ref[...] = dsc * (sg * (1.0 + cc * (1.0 - sg)))

    sd = jax.ShapeDtypeStruct
    return pl.pallas_call(body, name="mod_bwd",
                          out_shape=[sd((nl, dm, ns), F32), sd((nl, 1, dlat_full.shape[2]), F32), sd((SUBLANES, dm), F32)],
                          compiler_params=_params(None, 48 << 20))(craw, w, dlat_sh, dctx_sh, dlat_full, dctx_full)


def _adam_rows(r):
    best = None
    for t in range(2 * SUBLANES, min(r, 128) + 1, 2 * SUBLANES):
        if r % t == 0:
            best = t
    return best if best is not None else r


def _adamw(parts, w, m, v, *, name):
    npart, r, c = parts.shape
    tr = _adam_rows(r)

    def body(p_ref, w_ref, m_ref, v_ref, g_ref, d_ref, nm_ref, nv_ref):
        g = p_ref[0].astype(F32)
        for i in range(1, npart):
            g = g + p_ref[i].astype(F32)
        nm = ADAM_B1 * m_ref[...] + (1.0 - ADAM_B1) * g
        nv = ADAM_B2 * v_ref[...] + (1.0 - ADAM_B2) * (g * g)
        m_hat = nm / (1.0 - ADAM_B1 ** ADAM_STEP)
        v_hat = nv / (1.0 - ADAM_B2 ** ADAM_STEP)
        g_ref[...] = g
        d_ref[...] = -ADAM_LR * (m_hat / (jnp.sqrt(v_hat) + ADAM_EPS) + ADAM_WD * w_ref[...])
        nm_ref[...] = nm
        nv_ref[...] = nv

    spec = pl.BlockSpec((tr, c), lambda i: (i, 0))
    vmem = 2 * (npart + 7) * tr * c * 4 + (8 << 20)
    return pl.pallas_call(
        body, name=name, out_shape=[jax.ShapeDtypeStruct((r, c), F32)] * 4, grid=(r // tr,),
        in_specs=[pl.BlockSpec((npart, tr, c), lambda i: (0, i, 0)), spec, spec, spec], out_specs=[spec] * 4,
        compiler_params=_params(("parallel",), vmem),
    )(parts, w, m, v)


def _to_tm(a):
    return jnp.transpose(a, (1, 0, 2)).reshape(a.shape[1] * a.shape[0], a.shape[2])


def _pattern(mod_lat, mod_ctx, m, dm):
    lat = mod_lat[:, m * dm:(m + 1) * dm]
    ctx = jnp.broadcast_to(mod_ctx[None, m * dm:(m + 1) * dm], (SUBLANES, dm))
    return jnp.stack([ctx, jnp.concatenate([lat, lat], axis=0)])


def _blockdiag_b(bt, nk):
    g, h, p = bt.shape
    t = bt.reshape(nk, 8, h, p)
    return jnp.einsum("kghp,gj->kghjp", t, jnp.eye(8, dtype=bt.dtype)).reshape(nk, 8 * h, 8 * p)


def _blockdiag_c(ct, nk):
    g, h, p = ct.shape
    t = ct.reshape(nk, 8, h, p)
    return jnp.einsum("kghp,gj->kgpjh", t, jnp.eye(8, dtype=ct.dtype)).reshape(nk, 8 * p, 8 * h)


def _diag_b(dbb, h, p):
    nk = dbb.shape[0]
    return jnp.einsum("kghgp->kghp", dbb.reshape(nk, 8, h, 8, p)).reshape(nk * 8, h, p)


def _diag_c(dcc, h, p):
    nk = dcc.shape[0]
    return jnp.einsum("kgpgh->kghp", dcc.reshape(nk, 8, p, 8, h)).reshape(nk * 8, h, p)


def kernel(x, c, ctx, c_ctx, w_mod, b_mod, norm1_w, norm2_w, final_norm_w, s5_w_in, s5_lam_re, s5_lam_im, s5_log_step, s5_b_re, s5_b_im, s5_c_re, s5_c_im, s5_d, s5_w_glu, s5_w_out, hg_w_in, hg_lower_bounds, hg_gnorm_w, hg_w_out, ffn_w_up, ffn_conv_w, ffn_conv_b, ffn_w_down, loss_target, m_c_ctx, m_w_mod, m_b_mod, m_norm1_w, m_norm2_w, m_final_norm_w, m_s5_w_in, m_s5_lam_re, m_s5_lam_im, m_s5_log_step, m_s5_b_re, m_s5_b_im, m_s5_c_re, m_s5_c_im, m_s5_d, m_s5_w_glu, m_s5_w_out, m_hg_w_in, m_hg_lower_bounds, m_hg_gnorm_w, m_hg_w_out, m_ffn_w_up, m_ffn_conv_w, m_ffn_conv_b, m_ffn_w_down, v_c_ctx, v_w_mod, v_b_mod, v_norm1_w, v_norm2_w, v_final_norm_w, v_s5_w_in, v_s5_lam_re, v_s5_lam_im, v_s5_log_step, v_s5_b_re, v_s5_b_im, v_s5_c_re, v_s5_c_im, v_s5_d, v_s5_w_glu, v_s5_w_out, v_hg_w_in, v_hg_lower_bounds, v_hg_gnorm_w, v_hg_w_out, v_ffn_w_up, v_ffn_conv_w, v_ffn_conv_b, v_ffn_w_down):
    given = dict(locals())
    bsz, lx, dm = x.shape
    lc = ctx.shape[1]
    assert bsz == LOCAL_B and w_mod.shape[0] == 2 and dm % LANES == 0
    n, nctx = (lc + lx) * bsz, lc * bsz
    ngrp, nstate, hgrp = dm // S5_GROUP, S5_STATE, S5_GROUP
    nk = dm // LANES
    dims = dict(nctx=nctx, tm=min(512, nctx), tm_row=min(256, nctx), s5_rows=min(256, nctx),
                s5_lane_block=min(512, 8 * nstate))
    tm = dims["tm"]
    assert nctx % HG_ROWS == 0 and (lx * bsz) % nctx == 0 and lc % GRID_W == 0 and lc & (lc - 1) == 0
    me = 4 * lax.axis_index("x") + 2 * lax.axis_index("y") + lax.axis_index("c")

    gath = _exchange([given[k].astype(BF16) for k in ("s5_w_in", "s5_w_glu", "s5_w_out")]
                     + [c, hg_lower_bounds, ffn_conv_w], a2a=False, name="gather_weights")
    w_s5in, w_glu, w_s5out = (g.reshape(dm, dm) for g in gath[:3])
    c_all, lb_all, cw_all = gath[3:]
    ns_up = ffn_w_up.shape[2]
    w_up, w_dn = [None, None], [None, None]
    cols = lambda g: jnp.transpose(g, (1, 0, 2)).reshape(g.shape[1], -1)
    shards = lambda w: jnp.transpose(w.reshape(w.shape[0], NDEV, -1), (1, 0, 2))
    tn_up, tn_hg, tkr = 2 * ns_up, 2 * hg_w_in.shape[2], 1152
    assert n % tkr == 0 and n % 1024 == 0
    gather = lambda arrs: _Xchg([a.astype(BF16) for a in arrs], [False] * len(arrs))
    scatter = lambda arrs: _Xchg(arrs, [True] * len(arrs))
    cw = [cols(cw_all[:, layer]) for layer in range(2)]
    cb = [ffn_conv_b[layer].reshape(1, -1) for layer in range(2)]
    lb2 = jnp.transpose(lb_all, (1, 2, 0, 3)).reshape(2, 2, 1, dm)

    nsm = w_mod.shape[2]
    craw = jnp.concatenate([c_all.reshape(NDEV * bsz, dm), c_ctx[None], jnp.zeros((SUBLANES - 1, dm), F32)], axis=0)
    b_sh = lax.dynamic_slice(b_mod, (0, me * nsm), (2, nsm)).reshape(2, 1, nsm)
    mod_sh = _mod_fwd(craw, w_mod, b_sh)
    (mod_g,) = _exchange([mod_sh], a2a=False, name="gather_mod")
    mod_full = jnp.transpose(mod_g, (1, 2, 0, 3)).reshape(2, craw.shape[0], NDEV * nsm)
    pat = []
    for layer in range(2):
        mlat = lax.dynamic_slice(mod_full[layer], (me * bsz, 0), (bsz, N_MOD * dm))
        mctx = mod_full[layer, NDEV * bsz]
        pat.append([_pattern(mlat, mctx, m, dm) for m in range(N_MOD)])

    lr4 = s5_lam_re[0].reshape(2, ngrp, 1, nstate)
    li4 = s5_lam_im[0].reshape(2, ngrp, 1, nstate)
    ls4 = s5_log_step[0].reshape(2, ngrp, 1, 1)
    brt = jnp.transpose(s5_b_re[0], (0, 1, 3, 2))
    bit = jnp.transpose(s5_b_im[0], (0, 1, 3, 2))
    abar_r, abar_i, bbar_r, bbar_i = _s5_disc_fwd(lr4, li4, ls4, brt, bit)
    sw = ngrp * nstate
    a8r = [jnp.broadcast_to(abar_r[d].reshape(1, sw), (SUBLANES, sw)) for d in range(2)]
    a8i = [jnp.broadcast_to(abar_i[d].reshape(1, sw), (SUBLANES, sw)) for d in range(2)]
    bbr = [_blockdiag_b(bbar_r[d], nk).astype(BF16) for d in range(2)]
    bbi = [_blockdiag_b(bbar_i[d], nk).astype(BF16) for d in range(2)]
    ccr = [_blockdiag_c(s5_c_re[0, d], nk).astype(BF16) for d in range(2)]
    cci = [_blockdiag_c(s5_c_im[0, d], nk).astype(BF16) for d in range(2)]
    dsk = s5_d.reshape(1, dm)

    z0 = jnp.concatenate([_to_tm(ctx), _to_tm(x)], axis=0)
    tgt = _to_tm(loss_target)
    n1w = [norm1_w[layer].reshape(1, dm) for layer in range(2)]
    n2w = [norm2_w[layer].reshape(1, dm) for layer in range(2)]

    def ffn_fwd(layer, h2):
        u = _lin(h2, w_up[layer], name=f"ffn_up{layer}", tm=1024, tn=tn_up)
        hm = _convffn_fwd(u, cw[layer], cb[layer], name=f"convffn_fwd{layer}", dims=dims)
        f = _lin(hm, w_dn[layer], name=f"ffn_down{layer}", tm=tm, tn=dm)
        return u, hm, f

    _, h0 = _norm_mod_fwd(z0, n1w[0], pat[0][0], pat[0][1], name="norm1_l0", dims=dims)
    u_s5 = _lin(h0, w_s5in, name="s5_in", tm=1024, tn=dm)
    (y_a, st0r, st0i), (g_up0, g_dn0) = _s5_fwd_dir(
        u_s5, None, a8r[0], a8i[0], bbr[0], bbi[0], ccr[0], cci[0], dsk, d=0, name="s5_fwd_d0", dims=dims,
        xch=gather([ffn_w_up[0], ffn_w_down[0]]))
    w_up[0], w_dn[0] = cols(g_up0), g_dn0.reshape(-1, dm)
    (y_s5, st1r, st1i), (g_hgin, g_hgout, g_dn1) = _s5_fwd_dir(
        u_s5, y_a, a8r[1], a8i[1], bbr[1], bbi[1], ccr[1], cci[1], dsk, d=1, name="s5_fwd_d1", dims=dims,
        xch=gather([hg_w_in[0], hg_w_out[0], ffn_w_down[1]]))
    w_hgin, w_hgout, w_dn[1] = cols(g_hgin), g_hgout.reshape(dm, dm), g_dn1.reshape(-1, dm)
    (zg,) = _rowk(lambda rv, pv, cv, il: ([_gelu(rv[0])], [], []), name="s5_gelu", n=n, tm=dims["tm_row"], nctx=nctx,
                  rows=[(y_s5, dm, 0, 0)], out_rows=[(dm, BF16, dm, 0)])
    t_glu = _lin(zg, w_glu, name="s5_glu", tm=1024, tn=dm)
    (z2g,) = _rowk(lambda rv, pv, cv, il: ([rv[0] * _sigmoid(rv[1])], [], []), name="s5_gate", n=n,
                   tm=dims["tm_row"], nctx=nctx, rows=[(zg, dm, 0, 0), (t_glu, dm, 0, 0)],
                   out_rows=[(dm, BF16, dm, 0)])
    ymix0 = _lin(z2g, w_s5out, name="s5_out", tm=1024, tn=dm)
    z1_l0, h2_l0 = _norm_mod_fwd(z0, n2w[0], pat[0][3], pat[0][4], name="norm2_l0", dims=dims,
                                 res=(ymix0, pat[0][2]))
    u_l0, hm_l0, f_l0 = ffn_fwd(0, h2_l0)

    z2_l0, h1 = _norm_mod_fwd(z1_l0, n1w[1], pat[1][0], pat[1][1], name="norm1_l1", dims=dims,
                              res=(f_l0, pat[0][5]))
    zz = _lin(h1, w_hgin, name="hg_in", tm=1024, tn=tn_hg)
    (o_f, sts_f), (g_up1,) = _hg_fwd_dir(zz, lb2, d=0, name="hg_fwd_d0", dims=dims, xch=gather([ffn_w_up[1]]))
    w_up[1] = cols(g_up1)
    (o_b, sts_b), _ = _hg_fwd_dir(zz, lb2, d=1, name="hg_fwd_d1", dims=dims)
    gnw = hg_gnorm_w.reshape(1, HG_HEAD)
    (og,) = _rowk(lambda rv, pv, cv, il: ([_hg_readout(rv[0] + rv[1], rv[2], cv[0])], [], []), name="hg_readout",
                  n=n, tm=dims["tm_row"], nctx=nctx, rows=[(o_f, dm, 0, 0), (o_b, dm, 0, 0), (zz, dm, 4, 0)],
                  consts=[gnw], out_rows=[(dm, BF16, dm, 0)])
    ymix1 = _lin(og, w_hgout, name="hg_out", tm=1024, tn=dm)
    z1_l1, h2_l1 = _norm_mod_fwd(z2_l0, n2w[1], pat[1][3], pat[1][4], name="norm2_l1", dims=dims,
                                 res=(ymix1, pat[1][2]))
    u_l1, hm_l1, f_l1 = ffn_fwd(1, h2_l1)

    dz, df, dgate2_l1, loss_part, dfinal_w = _loss_bwd(z1_l1, f_l1, pat[1][5], tgt, final_norm_w.reshape(1, dm),
                                                        name="loss_bwd", dims=dims)

    def ffn_bwd(layer, df_, u, hm, h2, xch=None):
        dff = hm.shape[1]
        dhm = _lin(df_, w_dn[layer], name=f"ffn_down_bwd_in{layer}", trans_w=True, tm=1024, tn=dff // 2, o_dtype=BF16)
        dwd = _lin_w(hm, df_, name=f"ffn_down_bwd_w{layer}", ta=dff // 2, tn=dm, tkr=tkr)
        (dua, dug, dcwa, dcwg, dcba, dcbg), got = _convffn_bwd(u, dhm, cw[layer], cb[layer],
                                                               name=f"convffn_bwd{layer}", dims=dims, xch=xch)
        dh2 = _lin(dua, w_up[layer], name=f"ffn_up_bwd_in_a{layer}", trans_w=True, tm=tm, tn=dm, kblk=0)
        dh2 = _lin(dug, w_up[layer], name=f"ffn_up_bwd_in_g{layer}", trans_w=True, tm=tm, tn=dm, kblk=1, base=dh2)
        dwu = jnp.concatenate([_lin_w(h2, dua, name=f"ffn_up_bwd_w_a{layer}", ta=dm, tn=tn_up, tkr=tkr),
                               _lin_w(h2, dug, name=f"ffn_up_bwd_w_g{layer}", ta=dm, tn=tn_up, tkr=tkr)], axis=1)
        dcw = shards(jnp.concatenate([dcwa, dcwg], axis=1))
        return dh2, shards(dwu), dwd, dcw, jnp.concatenate([dcba, dcbg], axis=1), got

    dh2, dwu_l1, dwd_l1, dcw_l1, dcb_l1, _ = ffn_bwd(1, df, u_l1, hm_l1, h2_l1)
    dz, dymix, dsh2_l1, dsc2_l1, dgate1_l1, dn2w_l1 = _norm_mod_bwd(dh2, z1_l1, dz, n2w[1], pat[1][4], name="norm2_bwd_l1",
                                                                    dims=dims, res=(ymix1, pat[1][2]))
    dog = _lin(dymix, w_hgout, name="hg_out_bwd_in", trans_w=True, tm=1024, tn=dm)
    dw_hgout = _lin_w(og, dymix, name="hg_out_bwd_w", ta=dm, tn=dm, tkr=tkr)

    def readout_bwd(rv, pv, cv, il):
        _, vjp = jax.vjp(_hg_readout, rv[0] + rv[1], rv[2], cv[0])
        do, dg, dw = vjp(rv[3])
        return [do, dg], [], [jnp.broadcast_to(dw, (SUBLANES, HG_HEAD)) * (1.0 / SUBLANES)]

    do, dg, dgnw = _rowk(readout_bwd, name="hg_readout_bwd", n=n, tm=dims["tm_row"], nctx=nctx,
                         rows=[(o_f, dm, 0, 0), (o_b, dm, 0, 0), (zz, dm, 4, 0), (dog, dm, 0, 0)], consts=[gnw],
                         out_rows=[(dm, F32, dm, 0), (dm, BF16, dm, 0)], out_acc=[HG_HEAD])
    (dq0, dv0, dff, dl_f), (p_up1, p_dn1) = _hg_bwd_dir(
        zz, lb2, do, sts_f, None, d=0, name="hg_bwd_d0", dims=dims,
        xch=scatter([dwu_l1, dwd_l1.reshape(NDEV, -1, dm)]))
    (dq, dv, dfb, dl_b), (p_hgout,) = _hg_bwd_dir(
        zz, lb2, do, sts_b, (dq0, dv0), d=1, name="hg_bwd_d1", dims=dims,
        xch=scatter([dw_hgout.reshape(NDEV, -1, dm)]))
    dzz = jnp.concatenate([dq, dv, dff, dfb, dg], axis=1)
    dh1 = _lin(dzz, w_hgin, name="hg_in_bwd_in", trans_w=True, tm=tm, tn=dm)
    dw_hgin = shards(_lin_w(h1, dzz, name="hg_in_bwd_w", ta=dm, tn=tn_hg, tkr=tkr))
    dz, df0, dsh1_l1, dsc1_l1, dgate2_l0, dn1w_l1 = _norm_mod_bwd(dh1, z2_l0, dz, n1w[1], pat[1][1], name="norm1_bwd_l1",
                                                                  dims=dims, res=(f_l0, pat[0][5]))
    dh2, dwu_l0, dwd_l0, dcw_l0, dcb_l0, (p_hgin,) = ffn_bwd(0, df0, u_l0, hm_l0, h2_l0, xch=scatter([dw_hgin]))
    dz, dymix, dsh2_l0, dsc2_l0, dgate1_l0, dn2w_l0 = _norm_mod_bwd(dh2, z1_l0, dz, n2w[0], pat[0][4], name="norm2_bwd_l0",
                                                                    dims=dims, res=(ymix0, pat[0][2]))
    dz2g = _lin(dymix, w_s5out, name="s5_out_bwd_in", trans_w=True, tm=1024, tn=dm)
    dw_s5out = _lin_w(z2g, dymix, name="s5_out_bwd_w", ta=dm, tn=dm, tkr=tkr)

    def gate_bwd(rv, pv, cv, il):
        sg = _sigmoid(rv[1])
        return [rv[2] * rv[0] * sg * (1.0 - sg), rv[2] * sg], [], []

    dt_glu, dzg_a = _rowk(gate_bwd, name="s5_gate_bwd", n=n, tm=dims["tm_row"], nctx=nctx,
                          rows=[(zg, dm, 0, 0), (t_glu, dm, 0, 0), (dz2g, dm, 0, 0)],
                          out_rows=[(dm, BF16, dm, 0), (dm, F32, dm, 0)])
    dzg_b = _lin(dt_glu, w_glu, name="s5_glu_bwd_in", trans_w=True, tm=1024, tn=dm)
    dw_glu = _lin_w(zg, dt_glu, name="s5_glu_bwd_w", ta=dm, tn=dm, tkr=tkr)

    def gelu_bwd(rv, pv, cv, il):
        _, vjp = jax.vjp(_gelu, rv[0])
        return [vjp(rv[1] + rv[2])[0]], [], []

    (dy_s5,) = _rowk(gelu_bwd, name="s5_gelu_bwd", n=n, tm=dims["tm_row"], nctx=nctx,
                     rows=[(y_s5, dm, 0, 0), (dzg_a, dm, 0, 0), (dzg_b, dm, 0, 0)], out_rows=[(dm, F32, dm, 0)])
    dcw_both = jnp.stack([dcw_l0, dcw_l1], axis=1)
    (du_a, dbbr0, dbbi0, dccr0, dcci0, dar0, dai0, ddsk), (p_up0, p_dn0, p_cw) = _s5_bwd_dir(
        u_s5, dy_s5, None, a8r[0], a8i[0], bbr[0], bbi[0], ccr[0], cci[0], dsk, st0r, st0i, d=0, name="s5_bwd_d0",
        dims=dims, xch=scatter([dwu_l0, dwd_l0.reshape(NDEV, -1, dm), dcw_both]))
    (du_s5, dbbr1, dbbi1, dccr1, dcci1, dar1, dai1), (p_s5out, p_glu) = _s5_bwd_dir(
        u_s5, dy_s5, du_a, a8r[1], a8i[1], bbr[1], bbi[1], ccr[1], cci[1], dsk, st1r, st1i, d=1, name="s5_bwd_d1",
        dims=dims, xch=scatter([dw_s5out.reshape(NDEV, -1, dm), dw_glu.reshape(NDEV, -1, dm)]))
    dh0 = _lin(du_s5, w_s5in, name="s5_in_bwd_in", trans_w=True, tm=1024, tn=dm)
    dw_s5in = _lin_w(h0, du_s5, name="s5_in_bwd_w", ta=dm, tn=dm, tkr=tkr)
    dz0, dsh1_l0, dsc1_l0, dn1w_l0 = _norm_mod_bwd(dh0, z0, dz, n1w[0], pat[0][1], name="norm1_bwd_l0", dims=dims)

    dar = jnp.stack([dar0, dar1]).reshape(2, ngrp, 1, nstate)
    dai = jnp.stack([dai0, dai1]).reshape(2, ngrp, 1, nstate)
    dbbr = jnp.stack([_diag_b(dbbr0, hgrp, nstate), _diag_b(dbbr1, hgrp, nstate)])
    dbbi = jnp.stack([_diag_b(dbbi0, hgrp, nstate), _diag_b(dbbi1, hgrp, nstate)])
    dlr, dli, dls, dbrt, dbit = _s5_disc_bwd(lr4, li4, ls4, brt, bit, dar, dai, dbbr, dbbi)
    g_c_re = jnp.stack([_diag_c(dccr0, hgrp, nstate), _diag_c(dccr1, hgrp, nstate)])
    g_c_im = jnp.stack([_diag_c(dcci0, hgrp, nstate), _diag_c(dcci1, hgrp, nstate)])

    dmod = jnp.stack([
        jnp.concatenate([dsh1_l0, dsc1_l0, dgate1_l0, dsh2_l0, dsc2_l0, dgate2_l0], axis=1),
        jnp.concatenate([dsh1_l1, dsc1_l1, dgate1_l1, dsh2_l1, dsc2_l1, dgate2_l1], axis=1)])
    dmod_g, p_s5in = _exchange([dmod, dw_s5in.reshape(NDEV, -1, dm)], a2a=[False, True], name="gather_dmod")
    dlat_full = jnp.transpose(dmod_g[:, :, :bsz], (1, 0, 2, 3)).reshape(2, NDEV * bsz, N_MOD * dm)
    dctx_full = dmod_g[:, :, bsz]
    dlat_sh = lax.dynamic_slice(dlat_full, (0, 0, me * nsm), (2, NDEV * bsz, nsm))
    dctx_sh = lax.dynamic_slice(dctx_full, (0, 0, me * nsm), (NDEV, 2, nsm))
    g_w_mod, g_b_mod, dcctx8 = _mod_bwd(craw, w_mod, dlat_sh, dctx_sh, dlat_full, dctx_full)

    dl_hg = jnp.stack([dl_f[:, 0], dl_b[:, 0]])
    wide = lambda g: g.reshape(-1, dm)
    small = [("c_ctx", wide(dcctx8[:1])), ("norm1_w", jnp.concatenate([dn1w_l0, dn1w_l1])),
             ("norm2_w", jnp.concatenate([dn2w_l0, dn2w_l1])), ("final_norm_w", dfinal_w),
             ("s5_lam_re", dlr.reshape(-1, nstate)), ("s5_lam_im", dli.reshape(-1, nstate)),
             ("s5_log_step", dls.reshape(2, ngrp)),
             ("s5_b_re", wide(jnp.transpose(dbrt, (0, 1, 3, 2)).astype(BF16))),
             ("s5_b_im", wide(jnp.transpose(dbit, (0, 1, 3, 2)).astype(BF16))),
             ("s5_c_re", wide(g_c_re.astype(BF16))), ("s5_c_im", wide(g_c_im.astype(BF16))), ("s5_d", ddsk),
             ("hg_gnorm_w", dgnw), ("ffn_conv_b", jnp.stack([dcb_l0.reshape(-1), dcb_l1.reshape(-1)]))]
    gathered = _exchange([g for _, g in small] + [wide(dl_hg), loss_part], a2a=False, name="gather_small")
    res = {}
    for (k, g), parts in zip(small, gathered):
        w2, m2, v2 = (given[p + k].reshape(g.shape) for p in ("", "m_", "v_"))
        res[k] = tuple(o.reshape(given[k].shape) for o in _adamw(parts, w2, m2, v2, name="adamw_" + k))

    def total(parts, name):
        z = jnp.zeros(parts.shape[1:], F32)
        return _adamw(parts, z, z, z, name=name)[0]

    loss = jnp.sum(total(gathered[-1], "sum_loss"))
    dl_tot = total(gathered[-2], "sum_dlb").reshape(dl_hg.shape)
    nlb = hg_lower_bounds.shape[2]
    g_lb = lax.dynamic_slice(dl_tot, (0, 0, me * nlb), (2, 2, nlb))

    def adam_local(name, g, shape2):
        w, m, v = given[name], given["m_" + name], given["v_" + name]
        out = _adamw(g.reshape((1,) + shape2), w.reshape(shape2), m.reshape(shape2), v.reshape(shape2),
                     name="adamw_" + name)
        return tuple(o.reshape(w.shape) for o in out)

    def adam_parts(name, p):
        w, m, v = given[name], given["m_" + name], given["v_" + name]
        shape2 = (p.shape[0], -1, w.shape[-1])
        p3 = p.reshape(shape2)
        s2 = p3.shape[1:]
        out = _adamw(p3, w.reshape(s2), m.reshape(s2), v.reshape(s2), name="adamw_" + name)
        return tuple(o.reshape(w.shape) for o in out)

    res["hg_lower_bounds"] = adam_local("hg_lower_bounds", g_lb, (2 * 2, nlb))
    res["w_mod"] = adam_local("w_mod", g_w_mod, (2 * dm, nsm))
    res["b_mod"] = adam_local("b_mod", g_b_mod, (2, N_MOD * dm))
    res["s5_w_in"] = adam_parts("s5_w_in", p_s5in)
    res["s5_w_glu"] = adam_parts("s5_w_glu", p_glu)
    res["s5_w_out"] = adam_parts("s5_w_out", p_s5out)
    res["hg_w_in"] = adam_parts("hg_w_in", p_hgin)
    res["hg_w_out"] = adam_parts("hg_w_out", p_hgout)
    res["ffn_w_up"] = adam_parts("ffn_w_up", jnp.stack([p_up0, p_up1], axis=1))
    res["ffn_w_down"] = adam_parts("ffn_w_down", jnp.stack([p_dn0, p_dn1], axis=1))
    res["ffn_conv_w"] = adam_parts("ffn_conv_w", p_cw)

    grad_x = jnp.transpose(dz0[nctx:].reshape(lx, bsz, dm), (1, 0, 2))
    order = ["c_ctx", "w_mod", "b_mod", "norm1_w", "norm2_w", "final_norm_w", "s5_w_in", "s5_lam_re", "s5_lam_im",
             "s5_log_step", "s5_b_re", "s5_b_im", "s5_c_re", "s5_c_im", "s5_d", "s5_w_glu", "s5_w_out", "hg_w_in",
             "hg_lower_bounds", "hg_gnorm_w", "hg_w_out", "ffn_w_up", "ffn_conv_w", "ffn_conv_b", "ffn_w_down"]
    outs = [loss, grad_x]
    for j in range(4):
        outs += [res[k][j].reshape(given[k].shape) for k in order]
    return tuple(outs)
```

```python
import functools

import jax
import jax.numpy as jnp
from jax import lax
from jax.experimental import pallas as pl
from jax.experimental.pallas import tpu as pltpu

F32 = jnp.float32
BF16 = jnp.bfloat16
NDEV = 8
LOCAL_B = 4
NORM_EPS = 1e-6
N_MOD = 6
S5_GROUP = 16
S5_STATE = 64
S5_LAM_RE_MAX = -1e-4
HG_HEAD = 128
HG_ROWS = 128
GRID_W = 64
ADAM_LR, ADAM_B1, ADAM_B2, ADAM_EPS, ADAM_WD, ADAM_STEP = 0.001, 0.9, 0.999, 1e-08, 0.01, 10
VMEM_BYTES_V7X = 64 * 1024 * 1024
LANES = 128
SUBLANES = 8

NN = (((1,), (0,)), ((), ()))
NT = (((1,), (1,)), ((), ()))
TN = (((0,), (0,)), ((), ()))
MESH = pl.DeviceIdType.MESH


def _params(sem=None, vmem=None):
    kw = {}
    if sem is not None:
        kw["dimension_semantics"] = sem
    if vmem is not None:
        kw["vmem_limit_bytes"] = int(min(vmem, VMEM_BYTES_V7X - (4 << 20)))
    return pltpu.CompilerParams(**kw)


def _nbytes(shape, dtype):
    n = 1
    for s in shape:
        n *= 1 if s is None else s
    return n * jnp.dtype(dtype).itemsize


def _dot(a, b, dims=NN, precision=None):
    return lax.dot_general(a, b, dims, preferred_element_type=F32, precision=precision)


def _sigmoid(x):
    return 1.0 / (1.0 + jnp.exp(-x))


class _Xchg:
    def __init__(self, arrs, a2a):
        self.arrs, self.a2a, self.n = list(arrs), list(a2a), len(arrs)

    def out_shape(self):
        return [jax.ShapeDtypeStruct(a.shape if f else (NDEV,) + a.shape, a.dtype) for a, f in zip(self.arrs, self.a2a)]

    def scratch(self):
        return [pltpu.SemaphoreType.DMA((self.n * (NDEV - 1),)), pltpu.SemaphoreType.DMA((self.n * (NDEV - 1),)),
                pltpu.SemaphoreType.DMA((self.n,))]

    def _copies(self, ins, outs, sems, with_recvs):
        send_sems, recv_sems, loc_sems = sems
        x, y, c = lax.axis_index("x"), lax.axis_index("y"), lax.axis_index("c")
        me = 4 * x + 2 * y + c
        local, sends, recvs = [], [], []
        for a in range(self.n):
            src = ins[a].at[me] if self.a2a[a] else ins[a]
            local.append(pltpu.make_async_copy(src, outs[a].at[me], loc_sems.at[a]))
            for k in range(1, NDEV):
                px = (1 - x) if (k >> 2) & 1 else x
                py = (1 - y) if (k >> 1) & 1 else y
                pc = (1 - c) if k & 1 else c
                p = 4 * px + 2 * py + pc
                s = a * (NDEV - 1) + k - 1
                src = ins[a].at[p] if self.a2a[a] else ins[a]
                kw = dict(src_ref=src, send_sem=send_sems.at[s], recv_sem=recv_sems.at[s], device_id=(px, py, pc),
                          device_id_type=MESH)
                sends.append(pltpu.make_async_remote_copy(dst_ref=outs[a].at[me], **kw))
                if with_recvs:
                    recvs.append(pltpu.make_async_remote_copy(dst_ref=outs[a].at[p], **kw))
        return local, sends, recvs

    def start(self, ins, outs, sems):
        local, sends, _ = self._copies(ins, outs, sems, False)
        for cp in local + sends:
            cp.start()

    def wait(self, ins, outs, sems):
        local, sends, recvs = self._copies(ins, outs, sems, True)
        for cp in sends:
            cp.wait_send()
        for cp in recvs:
            cp.wait_recv()
        for cp in local:
            cp.wait()


def _exchange(arrs, *, a2a, name):
    xch = _Xchg(arrs, a2a if isinstance(a2a, (list, tuple)) else [a2a] * len(arrs))
    n = xch.n

    def body(*refs):
        xch.start(refs[:n], refs[n:2 * n], refs[2 * n:])
        xch.wait(refs[:n], refs[n:2 * n], refs[2 * n:])

    res = pl.pallas_call(
        body, name=name, out_shape=xch.out_shape(),
        in_specs=[pl.BlockSpec(memory_space=pl.ANY)] * n, out_specs=[pl.BlockSpec(memory_space=pl.ANY)] * n,
        scratch_shapes=xch.scratch(),
    )(*arrs)
    return list(res)


def _call(body, *, name, out_shape, grid, in_specs, out_specs, scratch, params, args, xch=None):
    in_specs, out_specs, out_shape, scratch, args = list(in_specs), list(out_specs), list(out_shape), list(scratch), list(args)
    n_in, n_out, n_scr = len(in_specs), len(out_shape), len(scratch)
    if xch is not None:
        k = xch.n
        inner = body

        def body(*refs):
            ins, xin = refs[:n_in], refs[n_in:n_in + k]
            outs, xout = refs[n_in + k:n_in + k + n_out], refs[n_in + k + n_out:n_in + 2 * k + n_out]
            scr = refs[n_in + 2 * k + n_out:n_in + 2 * k + n_out + n_scr]
            sems = refs[n_in + 2 * k + n_out + n_scr:]
            first = pl.program_id(0) == 0
            last = pl.program_id(0) == grid[0] - 1
            for ax in range(1, len(grid)):
                first = jnp.logical_and(first, pl.program_id(ax) == 0)
                last = jnp.logical_and(last, pl.program_id(ax) == grid[ax] - 1)

            @pl.when(first)
            def _():
                xch.start(xin, xout, sems)

            inner(*ins, *outs, *scr)

            @pl.when(last)
            def _():
                xch.wait(xin, xout, sems)

        anyspec = pl.BlockSpec(memory_space=pl.ANY)
        in_specs += [anyspec] * k
        out_specs += [anyspec] * k
        out_shape += xch.out_shape()
        scratch += xch.scratch()
        args += xch.arrs
    res = pl.pallas_call(body, name=name, out_shape=out_shape, grid=grid, in_specs=in_specs, out_specs=out_specs,
                         scratch_shapes=scratch, compiler_params=params)(*args)
    return list(res[:n_out]), list(res[n_out:])


def _mm(a, b, *, name, grid, a_spec, b_spec, o_spec, o_shape, o_dtype, dims, base=None):
    nk = grid[2]
    o_block = tuple(s for s in o_spec.block_shape if s is not None)

    def body(a_ref, b_ref, *rest):
        base_ref = rest[0] if base is not None else None
        o_ref, scr = rest[1 if base is not None else 0], rest[2 if base is not None else 1:]
        r = _dot(a_ref[...].astype(BF16), b_ref[...].astype(BF16), dims)
        if nk == 1:
            if base is not None:
                r = r + base_ref[...].astype(F32)
            o_ref[...] = r.astype(o_dtype)
        else:
            acc = scr[0]
            k = pl.program_id(2)

            @pl.when(k == 0)
            def _():
                acc[...] = r

            @pl.when(k > 0)
            def _():
                acc[...] += r

            @pl.when(k == nk - 1)
            def _():
                tot = acc[...] if base is None else acc[...] + base_ref[...].astype(F32)
                o_ref[...] = tot.astype(o_dtype)

    blocks = (_nbytes(a_spec.block_shape, a.dtype) + _nbytes(b_spec.block_shape, b.dtype) + _nbytes(o_block, o_dtype)
              + (_nbytes(o_block, base.dtype) if base is not None else 0))
    scratch = [pltpu.VMEM(o_block, F32)] if nk > 1 else []
    vmem = 2 * blocks + 3 * _nbytes(o_block, F32) + (8 << 20)
    return pl.pallas_call(
        body, name=name, out_shape=jax.ShapeDtypeStruct(o_shape, o_dtype), grid=grid,
        in_specs=[a_spec, b_spec] + ([o_spec] if base is not None else []), out_specs=o_spec, scratch_shapes=scratch,
        compiler_params=_params(("parallel", "parallel", "arbitrary"), vmem),
    )(a, b, *([base] if base is not None else []))


def _lin(a, w, *, name, trans_w=False, tm, tn, o_dtype=F32, kblk=0, base=None):
    m, kk = a.shape
    nout = w.shape[0] if trans_w else w.shape[1]
    if trans_w:
        b_spec = pl.BlockSpec((tn, kk), lambda j, i, k: (j, kblk))
    else:
        b_spec = pl.BlockSpec((kk, tn), lambda j, i, k: (0, j))
    return _mm(a, w, name=name, grid=(nout // tn, m // tm, 1), dims=NT if trans_w else NN, o_shape=(m, nout),
               o_dtype=o_dtype, o_spec=pl.BlockSpec((tm, tn), lambda j, i, k: (i, j)),
               a_spec=pl.BlockSpec((tm, kk), lambda j, i, k: (i, 0)), b_spec=b_spec, base=base)


def _lin_w(a, dy, *, name, ta, tn, tkr):
    m, ka = a.shape
    nout = dy.shape[1]
    return _mm(a, dy, name=name, grid=(ka // ta, nout // tn, m // tkr), dims=TN, o_shape=(ka, nout), o_dtype=BF16,
               o_spec=pl.BlockSpec((ta, tn), lambda i, j, k: (i, j)),
               a_spec=pl.BlockSpec((tkr, ta), lambda i, j, k: (k, i)),
               b_spec=pl.BlockSpec((tkr, tn), lambda i, j, k: (k, j)))


def _rowk(fn, *, name, n, tm, nctx, rows=(), pats=(), consts=(), out_rows=(), out_seg=(), out_acc=()):
    nb, ncb = n // tm, nctx // tm
    nr, npat, ncst = len(rows), len(pats), len(consts)
    no, nseg, nacc = len(out_rows), len(out_seg), len(out_acc)
    in_specs, blocks = [], 0
    for arr, w, cb, off in rows:
        in_specs.append(pl.BlockSpec((tm, w), lambda i, cb=cb, off=off: (jnp.maximum(i - off, 0), cb)))
        blocks += _nbytes((tm, w), arr.dtype)
    for p in pats:
        in_specs.append(pl.BlockSpec((None, SUBLANES, p.shape[2]), lambda i: (jnp.where(i >= ncb, 1, 0), 0, 0)))
    for cst in consts:
        in_specs.append(pl.BlockSpec(cst.shape, lambda i: (0, 0)))
    out_shape, out_specs = [], []
    for wt, dt, w, cb in out_rows:
        out_shape.append(jax.ShapeDtypeStruct((n, wt), dt))
        out_specs.append(pl.BlockSpec((tm, w), lambda i, cb=cb: (i, cb)))
        blocks += _nbytes((tm, w), dt)
    for w in out_seg:
        out_shape.append(jax.ShapeDtypeStruct((SUBLANES, w), F32))
        out_specs.append(pl.BlockSpec((SUBLANES, w), lambda i: (0, 0)))
    for w in out_acc:
        out_shape.append(jax.ShapeDtypeStruct((1, w), F32))
        out_specs.append(pl.BlockSpec((1, w), lambda i: (0, 0)))
    scratch = [pltpu.VMEM((2, SUBLANES, w), F32) for w in out_seg] + [pltpu.VMEM((SUBLANES, w), F32) for w in out_acc]

    def body(*refs):
        r_in = refs[:nr]
        p_in = refs[nr:nr + npat]
        c_in = refs[nr + npat:nr + npat + ncst]
        base = nr + npat + ncst
        o_rows = refs[base:base + no]
        o_seg = refs[base + no:base + no + nseg]
        o_acc = refs[base + no + nseg:base + no + nseg + nacc]
        s_seg = refs[base + no + nseg + nacc:base + no + nseg + nacc + nseg]
        s_acc = refs[base + no + nseg + nacc + nseg:]
        i = pl.program_id(0)
        rv = [r[...].astype(F32).reshape(tm // SUBLANES, SUBLANES, r.shape[1]) for r in r_in]
        pv = [p[...] for p in p_in]
        cv = [c[...] for c in c_in]
        is_lat = (i >= ncb).astype(F32)
        ro, so, ao = fn(rv, pv, cv, is_lat)
        for ref, val in zip(o_rows, ro):
            ref[...] = val.reshape(tm, ref.shape[1]).astype(ref.dtype)
        if nseg or nacc:
            @pl.when(i == 0)
            def _():
                for s in list(s_seg) + list(s_acc):
                    s[...] = jnp.zeros(s.shape, F32)

            seg = jnp.where(i >= ncb, 1, 0)
            for s, val in zip(s_seg, so):
                s[seg] = s[seg] + val
            for s, val in zip(s_acc, ao):
                s[...] = s[...] + val

            @pl.when(i == nb - 1)
            def _():
                for o, s in zip(o_seg, s_seg):
                    lat, ctx = s[1], s[0]
                    row = lax.broadcasted_iota(jnp.int32, lat.shape, 0)
                    lat = lat + pltpu.roll(lat, 4, 0)
                    ctx = jnp.broadcast_to(jnp.sum(ctx, axis=0, keepdims=True), lat.shape)
                    o[...] = jnp.where(row < 4, lat, jnp.where(row == 4, ctx, 0.0))
                for o, s in zip(o_acc, s_acc):
                    o[...] = jnp.sum(s[...], axis=0, keepdims=True)

    vmem = 2 * blocks + 8 * tm * 1024 * 4 + (8 << 20)
    res = pl.pallas_call(
        body, name=name, out_shape=out_shape, grid=(nb,), in_specs=in_specs, out_specs=out_specs,
        scratch_shapes=scratch, compiler_params=_params(("arbitrary",), vmem),
    )(*[r[0] for r in rows], *pats, *consts)
    return list(res)


def _rms(z):
    return lax.rsqrt(jnp.mean(z * z, axis=-1, keepdims=True) + NORM_EPS)


def _norm_mod_fwd(z, w, sh, sc, *, name, dims, res=None):
    n, d = z.shape

    def fn(rv, pv, cv, is_lat):
        zz = rv[0]
        if res is not None:
            zz = zz + pv[2][None] * rv[1]
        h = (zz * _rms(zz) * cv[0]) * (1.0 + pv[1][None]) + pv[0][None]
        return ([zz, h] if res is not None else [h]), [], []

    rows = [(z, d, 0, 0)] + ([(res[0], d, 0, 0)] if res is not None else [])
    pats = [sh, sc] + ([res[1]] if res is not None else [])
    outs = ([(d, F32, d, 0)] if res is not None else []) + [(d, BF16, d, 0)]
    out = _rowk(fn, name=name, n=n, tm=dims["tm_row"], nctx=dims["nctx"], rows=rows, pats=pats, consts=[w],
                out_rows=outs)
    return (out[0], out[1]) if res is not None else (None, out[0])


def _norm_core_bwd(zin, dh, w, sc):
    r = _rms(zin)
    xh = zin * r
    dsh = jnp.sum(dh, axis=0)
    dsc = jnp.sum(dh * (xh * w), axis=0)
    dyv = dh * (1.0 + sc[None])
    dw = jnp.sum(dyv * xh, axis=0)
    dxh = dyv * w
    dx = r * (dxh - xh * jnp.mean(dxh * xh, axis=-1, keepdims=True))
    return dx, dsh, dsc, dw


def _norm_mod_bwd(dh, zin, dz_up, w, sc, *, name, dims, res=None):
    n, d = zin.shape

    def fn(rv, pv, cv, is_lat):
        dx, dsh, dsc, dw = _norm_core_bwd(rv[1], rv[0], cv[0], pv[0])
        dz = rv[2] + dx
        if res is None:
            return [dz], [dsh, dsc], [dw]
        return [dz, dz * pv[1][None]], [dsh, dsc, jnp.sum(dz * rv[3], axis=0)], [dw]

    rows = [(dh, d, 0, 0), (zin, d, 0, 0), (dz_up, d, 0, 0)] + ([(res[0], d, 0, 0)] if res is not None else [])
    pats = [sc] + ([res[1]] if res is not None else [])
    outs = [(d, F32, d, 0)] + ([(d, BF16, d, 0)] if res is not None else [])
    return _rowk(fn, name=name, n=n, tm=dims["tm_row"], nctx=dims["nctx"], rows=rows, pats=pats, consts=[w],
                 out_rows=outs, out_seg=[d] * (3 if res is not None else 2), out_acc=[d])


def _loss_bwd(z1, f, gate, tgt, w, *, name, dims):
    n, d = z1.shape

    def fn(rv, pv, cv, is_lat):
        z2 = rv[0] + pv[0][None] * rv[1]
        r = _rms(z2)
        xh = z2 * r
        err = (xh * cv[0] - rv[2]) * is_lat
        dout = err * (1.0 / d)
        dxh = dout * cv[0]
        dz = r * (dxh - xh * jnp.mean(dxh * xh, axis=-1, keepdims=True))
        return ([dz, dz * pv[0][None]], [jnp.sum(dz * rv[1], axis=0)],
                [jnp.sum(0.5 * err * err * (1.0 / d), axis=0), jnp.sum(dout * xh, axis=0)])

    tm = dims["tm_row"]
    rows = [(z1, d, 0, 0), (f, d, 0, 0), (tgt, d, 0, dims["nctx"] // tm)]
    return _rowk(fn, name=name, n=n, tm=tm, nctx=dims["nctx"], rows=rows, pats=[gate], consts=[w],
                 out_rows=[(d, F32, d, 0), (d, BF16, d, 0)], out_seg=[d], out_acc=[d, d])


def _gelu(y):
    return jax.nn.gelu(y, approximate=True)


def _conv_masks(tb, i):
    tok = lax.broadcasted_iota(jnp.int32, (tb, 1), 0) >> 2
    last = jnp.where(i == 0, tb // LOCAL_B - 1, GRID_W - 1)
    wpos = tok & last
    return wpos == 0, wpos == last


CONV_LANES = 2 * LANES


def _conv_taps(u_ref, cw_ref, cb_ref, no_left, no_right, tb):
    uu = u_ref[...]
    ul = jnp.where(no_left, 0.0, pltpu.roll(uu, LOCAL_B, 0))
    ur = jnp.where(no_right, 0.0, pltpu.roll(uu, tb - LOCAL_B, 0))
    val = cb_ref[...] + ul * cw_ref[pl.ds(0, 1), :] + uu * cw_ref[pl.ds(1, 1), :] + ur * cw_ref[pl.ds(2, 1), :]
    return val, ul, uu, ur


def _convffn_specs(tb, nj):
    cl = CONV_LANES
    return [pl.BlockSpec((tb, cl), lambda j, i: (i, j)), pl.BlockSpec((tb, cl), lambda j, i: (i, nj + j)),
            pl.BlockSpec((3, cl), lambda j, i: (0, j)), pl.BlockSpec((3, cl), lambda j, i: (0, nj + j)),
            pl.BlockSpec((1, cl), lambda j, i: (0, j)), pl.BlockSpec((1, cl), lambda j, i: (0, nj + j))]


def _convffn_fwd(u, cw, cb, *, name, dims):
    n, f2 = u.shape
    tb, nj = dims["nctx"], f2 // 2 // CONV_LANES

    def body(ua_ref, ug_ref, cwa_ref, cwg_ref, cba_ref, cbg_ref, o_ref):
        no_left, no_right = _conv_masks(tb, pl.program_id(1))
        a = _conv_taps(ua_ref, cwa_ref, cba_ref, no_left, no_right, tb)[0]
        g = _conv_taps(ug_ref, cwg_ref, cbg_ref, no_left, no_right, tb)[0]
        o_ref[...] = (a * _sigmoid(a) * g).astype(BF16)

    vmem = 16 * tb * CONV_LANES * 4 + (8 << 20)
    return pl.pallas_call(
        body, name=name, out_shape=jax.ShapeDtypeStruct((n, f2 // 2), BF16), grid=(nj, n // tb),
        in_specs=_convffn_specs(tb, nj), out_specs=pl.BlockSpec((tb, CONV_LANES), lambda j, i: (i, j)),
        compiler_params=_params(("parallel", "arbitrary"), vmem),
    )(u, u, cw, cw, cb, cb)


def _convffn_bwd(u, dhm, cw, cb, *, name, dims, xch=None):
    n, f2 = u.shape
    tb, nj = dims["nctx"], f2 // 2 // CONV_LANES

    def body(ua_ref, ug_ref, cwa_ref, cwg_ref, cba_ref, cbg_ref, dh_ref, dua_ref, dug_ref, dcwa_ref, dcwg_ref, dcba_ref,
             dcbg_ref):
        i = pl.program_id(1)
        no_left, no_right = _conv_masks(tb, i)

        @pl.when(i == 0)
        def _():
            for ref in (dcwa_ref, dcwg_ref, dcba_ref, dcbg_ref):
                ref[...] = jnp.zeros(ref.shape, F32)

        a, al, ac, ar = _conv_taps(ua_ref, cwa_ref, cba_ref, no_left, no_right, tb)
        g, gl, gc, gr = _conv_taps(ug_ref, cwg_ref, cbg_ref, no_left, no_right, tb)
        dh = dh_ref[...].astype(F32)
        sa = _sigmoid(a)
        dg = dh * (a * sa)
        da = dh * g * (sa * (1.0 + a * (1.0 - sa)))
        for dc, (tl, tc, tr), cw_ref, du_ref, dcw_ref, dcb_ref in (
                (da, (al, ac, ar), cwa_ref, dua_ref, dcwa_ref, dcba_ref),
                (dg, (gl, gc, gr), cwg_ref, dug_ref, dcwg_ref, dcbg_ref)):
            dcb_ref[...] += jnp.sum(dc, axis=0, keepdims=True)
            dcw_ref[pl.ds(0, 1), :] += jnp.sum(dc * tl, axis=0, keepdims=True)
            dcw_ref[pl.ds(1, 1), :] += jnp.sum(dc * tc, axis=0, keepdims=True)
            dcw_ref[pl.ds(2, 1), :] += jnp.sum(dc * tr, axis=0, keepdims=True)
            du = (dc * cw_ref[pl.ds(1, 1), :]
                  + pltpu.roll(jnp.where(no_left, 0.0, dc) * cw_ref[pl.ds(0, 1), :], tb - LOCAL_B, 0)
                  + pltpu.roll(jnp.where(no_right, 0.0, dc) * cw_ref[pl.ds(2, 1), :], LOCAL_B, 0))
            du_ref[...] = du.astype(BF16)

    cl, f = CONV_LANES, f2 // 2
    sd = jax.ShapeDtypeStruct
    row = pl.BlockSpec((tb, cl), lambda j, i: (i, j))
    vmem = 24 * tb * cl * 4 + (8 << 20)
    return _call(
        body, name=name, xch=xch, args=[u, u, cw, cw, cb, cb, dhm], scratch=[],
        out_shape=[sd((n, f), BF16), sd((n, f), BF16), sd((3, f), F32), sd((3, f), F32), sd((1, f), F32), sd((1, f), F32)],
        grid=(nj, n // tb), in_specs=_convffn_specs(tb, nj) + [row],
        out_specs=[row, row, pl.BlockSpec((3, cl), lambda j, i: (0, j)), pl.BlockSpec((3, cl), lambda j, i: (0, j)),
                   pl.BlockSpec((1, cl), lambda j, i: (0, j)), pl.BlockSpec((1, cl), lambda j, i: (0, j))],
        params=_params(("arbitrary", "arbitrary"), vmem))


def _s5_disc(lr, li, ls, brt, bit):
    lr = jnp.minimum(lr, S5_LAM_RE_MAX)
    dt = jnp.exp(ls)
    mag = jnp.exp(lr * dt)
    ar = mag * jnp.cos(li * dt)
    ai = mag * jnp.sin(li * dt)
    den = lr * lr + li * li
    nr = ar - 1.0
    cr = (nr * lr + ai * li) / den
    ci = (ai * lr - nr * li) / den
    return ar, ai, cr * brt - ci * bit, cr * bit + ci * brt


def _s5_disc_fwd(lr, li, ls, brt, bit):
    def body(lr_ref, li_ref, ls_ref, br_ref, bi_ref, ar_ref, ai_ref, bbr_ref, bbi_ref):
        ar, ai, bbr, bbi = _s5_disc(lr_ref[...], li_ref[...], ls_ref[...], br_ref[...], bi_ref[...])
        ar_ref[...] = ar
        ai_ref[...] = ai
        bbr_ref[...] = bbr
        bbi_ref[...] = bbi

    sd = jax.ShapeDtypeStruct
    return pl.pallas_call(body, name="s5_disc_fwd",
                          out_shape=[sd(lr.shape, F32), sd(lr.shape, F32), sd(brt.shape, F32), sd(brt.shape, F32)],
                          compiler_params=_params(None, 32 << 20))(lr, li, ls, brt, bit)


def _s5_disc_bwd(lr, li, ls, brt, bit, dar, dai, dbbr, dbbi):
    def body(lr_ref, li_ref, ls_ref, br_ref, bi_ref, dar_ref, dai_ref, dbbr_ref, dbbi_ref,
             dlr_ref, dli_ref, dls_ref, dbr_ref, dbi_ref):
        _, vjp = jax.vjp(_s5_disc, lr_ref[...], li_ref[...], ls_ref[...], br_ref[...], bi_ref[...])
        dlr, dli, dls, dbr, dbi = vjp((dar_ref[...], dai_ref[...], dbbr_ref[...], dbbi_ref[...]))
        dlr_ref[...] = dlr
        dli_ref[...] = dli
        dls_ref[...] = dls
        dbr_ref[...] = dbr
        dbi_ref[...] = dbi

    sd = jax.ShapeDtypeStruct
    return pl.pallas_call(body, name="s5_disc_bwd",
                          out_shape=[sd(lr.shape, F32), sd(lr.shape, F32), sd(ls.shape, F32), sd(brt.shape, F32),
                                     sd(brt.shape, F32)],
                          compiler_params=_params(None, 48 << 20))(lr, li, ls, brt, bit, dar, dai, dbbr, dbbi)


def _cmul(ar, ai, xr, xi):
    return ar * xr - ai * xi, ar * xi + ai * xr


def _scan_consts(a_r, a_i, rev):
    row = lax.broadcasted_iota(jnp.int32, a_r.shape, 0)
    second = (row < 4) if rev else (row >= 4)
    a2r, a2i = _cmul(a_r, a_i, a_r, a_i)
    a1r, a1i = jnp.where(second, a_r, 0.0), jnp.where(second, a_i, 0.0)
    apr, api = jnp.where(second, a2r, a_r), jnp.where(second, a2i, a_i)
    return second, a1r, a1i, apr, api


def _scan_tile(xr, xi, pr, pi, consts):
    second, a1r, a1i, apr, api = consts
    sr, si = pltpu.roll(xr, 4, 0), pltpu.roll(xi, 4, 0)
    t1r, t1i = _cmul(a1r, a1i, sr, si)
    t2r, t2i = _cmul(apr, api, pr, pi)
    yr, yi = xr + t1r + t2r, xi + t1i + t2i
    npr = jnp.where(second, yr, pltpu.roll(yr, 4, 0))
    npi = jnp.where(second, yi, pltpu.roll(yi, 4, 0))
    return yr, yi, npr, npi


def _s5_scan(xr_ref, xi_ref, row0, nrows, a_r_ref, a_i_ref, cr_ref, ci_ref, *, rev, conj, lane_block, extra=None):
    width = xr_ref.shape[1]
    nt = nrows // SUBLANES
    for lb in range(width // lane_block):
        lanes = pl.ds(lb * lane_block, lane_block)
        a_r = a_r_ref[:, lanes]
        a_i = a_i_ref[:, lanes]
        if conj:
            a_i = -a_i
        consts = _scan_consts(a_r, a_i, rev)

        def step(t, carry):
            pr, pi = carry[0], carry[1]
            j = (nt - 1 - t) if rev else t
            rows = pl.ds(pl.multiple_of(row0 + j * SUBLANES, SUBLANES), SUBLANES)
            yr, yi, pr, pi = _scan_tile(xr_ref[rows, lanes], xi_ref[rows, lanes], pr, pi, consts)
            xr_ref[rows, lanes] = yr
            xi_ref[rows, lanes] = yi
            if extra is None:
                return pr, pi
            return (pr, pi) + tuple(extra(j, lanes, yr, yi, carry[2:]))

        init = (cr_ref[:, lanes], ci_ref[:, lanes])
        if extra is not None:
            init = init + tuple(extra.init(lanes))
        out = lax.fori_loop(0, nt, step, init)
        cr_ref[:, lanes] = out[0]
        ci_ref[:, lanes] = out[1]
        if extra is not None:
            extra.done(lanes, out[2:])


def _s5_chunk_of(step, ncc, nc, rev):
    if not rev:
        return step
    return jnp.where(step < ncc, ncc - 1 - step, nc - 1 - (step - ncc))


def _s5_fwd_dir(u, base, a8r, a8i, bbr, bbi, ccr, cci, dsk, *, d, name, dims, xch=None):
    n, dm = u.shape
    nk, swk = bbr.shape[0], bbr.shape[2]
    rr, sw = dims["s5_rows"], nk * swk
    nc, ncc = n // rr, dims["nctx"] // rr
    rev = d == 1
    cmap = lambda i: (_s5_chunk_of(i, ncc, nc, rev), 0)

    def body(*refs):
        if d == 0:
            u_ref, a8r_ref, a8i_ref, bbr_ref, bbi_ref, ccr_ref, cci_ref, dsk_ref = refs[:8]
            rest = refs[8:]
        else:
            u_ref, base_ref, a8r_ref, a8i_ref, bbr_ref, bbi_ref, ccr_ref, cci_ref = refs[:8]
            rest = refs[8:]
        y_ref, str_ref, sti_ref, sr, si, cr, ci = rest
        i = pl.program_id(0)

        @pl.when(i == 0)
        def _():
            cr[...] = jnp.zeros(cr.shape, F32)
            ci[...] = jnp.zeros(ci.shape, F32)

        str_ref[...] = cr[...]
        sti_ref[...] = ci[...]
        ub = u_ref[...].astype(BF16)
        for k in range(nk):
            uk = ub[:, k * LANES:(k + 1) * LANES]
            sr[:, k * swk:(k + 1) * swk] = _dot(uk, bbr_ref[k])
            si[:, k * swk:(k + 1) * swk] = _dot(uk, bbi_ref[k])
        _s5_scan(sr, si, 0, rr, a8r_ref, a8i_ref, cr, ci, rev=rev, conj=False, lane_block=dims["s5_lane_block"])
        for k in range(nk):
            cols = slice(k * LANES, (k + 1) * LANES)
            yk = (_dot(sr[:, k * swk:(k + 1) * swk].astype(BF16), ccr_ref[k])
                  - _dot(si[:, k * swk:(k + 1) * swk].astype(BF16), cci_ref[k]))
            if d == 0:
                y_ref[:, cols] = yk + dsk_ref[:, cols] * u_ref[:, cols]
            else:
                y_ref[:, cols] = yk + base_ref[:, cols]

    row_spec = pl.BlockSpec((rr, dm), cmap)
    full = lambda a: pl.BlockSpec(a.shape, lambda i: (0,) * a.ndim)
    ins = [u] + ([] if d == 0 else [base]) + [a8r, a8i, bbr, bbi, ccr, cci] + ([dsk] if d == 0 else [])
    in_specs = [row_spec] + ([] if d == 0 else [row_spec]) + [full(a) for a in (a8r, a8i, bbr, bbi, ccr, cci)]
    in_specs += [full(dsk)] if d == 0 else []
    st_spec = pl.BlockSpec((None, SUBLANES, sw), lambda i: (_s5_chunk_of(i, ncc, nc, rev), 0, 0))
    vmem = 2 * rr * sw * 4 + 6 * rr * dm * 4 + 8 * nk * LANES * swk * 2 + (12 << 20)
    return _call(
        body, name=name, xch=xch, args=ins,
        out_shape=[jax.ShapeDtypeStruct((n, dm), F32), jax.ShapeDtypeStruct((nc, SUBLANES, sw), F32),
                   jax.ShapeDtypeStruct((nc, SUBLANES, sw), F32)],
        grid=(nc,), in_specs=in_specs, out_specs=[row_spec, st_spec, st_spec],
        scratch=[pltpu.VMEM((rr, sw), F32), pltpu.VMEM((rr, sw), F32), pltpu.VMEM((SUBLANES, sw), F32),
                 pltpu.VMEM((SUBLANES, sw), F32)],
        params=_params(("arbitrary",), vmem))


class _DaHook:
    def __init__(self, sr, si, accr, acci, rev_fwd):
        self.sr, self.si, self.accr, self.acci, self.rev_fwd = sr, si, accr, acci, rev_fwd

    def init(self, lanes):
        return self.accr[:, lanes], self.acci[:, lanes]

    def done(self, lanes, acc):
        self.accr[:, lanes] = acc[0]
        self.acci[:, lanes] = acc[1]

    def __call__(self, j, lanes, lr, li, acc):
        base = pl.multiple_of(SUBLANES + j * SUBLANES, SUBLANES)
        cur = pl.ds(base, SUBLANES)
        row = lax.broadcasted_iota(jnp.int32, lr.shape, 0)
        if self.rev_fwd:
            oth = pl.ds(pl.multiple_of(base + SUBLANES, SUBLANES), SUBLANES)
            spr = pltpu.roll(jnp.where(row >= 4, self.sr[cur, lanes], self.sr[oth, lanes]), 4, 0)
            spi = pltpu.roll(jnp.where(row >= 4, self.si[cur, lanes], self.si[oth, lanes]), 4, 0)
        else:
            oth = pl.ds(pl.multiple_of(base - SUBLANES, SUBLANES), SUBLANES)
            spr = pltpu.roll(jnp.where(row >= 4, self.sr[oth, lanes], self.sr[cur, lanes]), 4, 0)
            spi = pltpu.roll(jnp.where(row >= 4, self.si[oth, lanes], self.si[cur, lanes]), 4, 0)
        return acc[0] + spr * lr + spi * li, acc[1] + spr * li - spi * lr


def _s5_bwd_dir(u, dy, du_prev, a8r, a8i, bbr, bbi, ccr, cci, dsk, st_r, st_i, *, d, name, dims, xch=None):
    n, dm = u.shape
    rr = dims["s5_rows"]
    nk, swk = bbr.shape[0], bbr.shape[2]
    sw = nk * swk
    nc, ncc = n // rr, dims["nctx"] // rr
    rev = d == 1
    chunk = lambda i: _s5_chunk_of(nc - 1 - i, ncc, nc, rev)

    def body(*refs):
        u_ref, dy_ref = refs[0], refs[1]
        pos = 2
        dup_ref = None
        if d == 1:
            dup_ref = refs[pos]
            pos += 1
        a8r_ref, a8i_ref, bbr_ref, bbi_ref, ccr_ref, cci_ref = refs[pos:pos + 6]
        pos += 6
        dsk_ref = None
        if d == 0:
            dsk_ref = refs[pos]
            pos += 1
        str_ref, sti_ref = refs[pos:pos + 2]
        pos += 2
        du_ref, dbbr_ref, dbbi_ref, dccr_ref, dcci_ref, dar_ref, dai_ref = refs[pos:pos + 7]
        pos += 7
        dd_ref = None
        if d == 0:
            dd_ref = refs[pos]
            pos += 1
        sr, si, lr, li, cr, ci, lcr, lci, accr, acci, dda = refs[pos:]
        i = pl.program_id(0)

        @pl.when(i == 0)
        def _():
            for ref in (lcr, lci, accr, acci, dda, dbbr_ref, dbbi_ref, dccr_ref, dcci_ref):
                ref[...] = jnp.zeros(ref.shape, F32)

        cr[...] = str_ref[...]
        ci[...] = sti_ref[...]
        spare = pl.ds(rr + SUBLANES, SUBLANES) if rev else pl.ds(0, SUBLANES)
        sr[spare, :] = str_ref[...]
        si[spare, :] = sti_ref[...]
        ub = u_ref[...].astype(BF16)
        dyb = dy_ref[...].astype(BF16)
        for k in range(nk):
            uk = ub[:, k * LANES:(k + 1) * LANES]
            sr[pl.ds(SUBLANES, rr), k * swk:(k + 1) * swk] = _dot(uk, bbr_ref[k])
            si[pl.ds(SUBLANES, rr), k * swk:(k + 1) * swk] = _dot(uk, bbi_ref[k])
        _s5_scan(sr, si, SUBLANES, rr, a8r_ref, a8i_ref, cr, ci, rev=rev, conj=False,
                 lane_block=dims["s5_lane_block"])
        for k in range(nk):
            dyk = dyb[:, k * LANES:(k + 1) * LANES]
            sl = slice(k * swk, (k + 1) * swk)
            lr[:, sl] = _dot(dyk, ccr_ref[k], NT)
            li[:, sl] = -_dot(dyk, cci_ref[k], NT)
            dccr_ref[k] += _dot(sr[pl.ds(SUBLANES, rr), sl].astype(BF16), dyk, TN)
            dcci_ref[k] -= _dot(si[pl.ds(SUBLANES, rr), sl].astype(BF16), dyk, TN)
        _s5_scan(lr, li, 0, rr, a8r_ref, a8i_ref, lcr, lci, rev=not rev, conj=True,
                 lane_block=dims["s5_lane_block"], extra=_DaHook(sr, si, accr, acci, rev))
        for k in range(nk):
            cols = slice(k * LANES, (k + 1) * LANES)
            sl = slice(k * swk, (k + 1) * swk)
            uk = ub[:, cols]
            lrk, lik = lr[:, sl].astype(BF16), li[:, sl].astype(BF16)
            dbbr_ref[k] += _dot(uk, lrk, TN)
            dbbi_ref[k] += _dot(uk, lik, TN)
            duk = _dot(lrk, bbr_ref[k], NT) + _dot(lik, bbi_ref[k], NT)
            if d == 0:
                du_ref[:, cols] = duk + dsk_ref[:, cols] * dy_ref[:, cols]
            else:
                du_ref[:, cols] = duk + dup_ref[:, cols]
        if d == 0:
            prod = (dy_ref[...] * u_ref[...]).reshape(rr // SUBLANES, SUBLANES, dm)
            dda[...] += jnp.sum(prod, axis=0)

        @pl.when(i == nc - 1)
        def _():
            dar_ref[...] = jnp.sum(accr[...], axis=0, keepdims=True)
            dai_ref[...] = jnp.sum(acci[...], axis=0, keepdims=True)
            if d == 0:
                dd_ref[...] = jnp.sum(dda[...], axis=0, keepdims=True)

    row_spec = pl.BlockSpec((rr, dm), lambda i: (chunk(i), 0))
    full = lambda a: pl.BlockSpec(a.shape, lambda i: (0,) * a.ndim)
    st_spec = pl.BlockSpec((None, SUBLANES, sw), lambda i: (chunk(i), 0, 0))
    ins = [u, dy] + ([du_prev] if d == 1 else []) + [a8r, a8i, bbr, bbi, ccr, cci] + ([dsk] if d == 0 else [])
    ins += [st_r, st_i]
    in_specs = [row_spec, row_spec] + ([row_spec] if d == 1 else []) + [full(a) for a in (a8r, a8i, bbr, bbi, ccr, cci)]
    in_specs += ([full(dsk)] if d == 0 else []) + [st_spec, st_spec]
    sd = jax.ShapeDtypeStruct
    out_shape = [sd((n, dm), F32), sd(bbr.shape, F32), sd(bbr.shape, F32), sd(ccr.shape, F32),
                 sd(ccr.shape, F32), sd((1, sw), F32), sd((1, sw), F32)] + ([sd((1, dm), F32)] if d == 0 else [])
    out_specs = [row_spec] + [pl.BlockSpec(s.shape, lambda i, nd=len(s.shape): (0,) * nd) for s in out_shape[1:]]
    scratch = [pltpu.VMEM((rr + 2 * SUBLANES, sw), F32), pltpu.VMEM((rr + 2 * SUBLANES, sw), F32),
               pltpu.VMEM((rr, sw), F32), pltpu.VMEM((rr, sw), F32)]
    scratch += [pltpu.VMEM((SUBLANES, sw), F32)] * 6 + [pltpu.VMEM((SUBLANES, dm), F32)]
    vmem = 4 * (rr + 16) * sw * 4 + 10 * rr * dm * 4 + 24 * nk * LANES * swk * 4 + (12 << 20)
    return _call(body, name=name, xch=xch, args=ins, out_shape=out_shape, grid=(nc,), in_specs=in_specs,
                 out_specs=out_specs, scratch=scratch, params=_params(("arbitrary",), vmem))


def _hg_mask(kind, rev):
    if kind == "tot":
        r = lax.broadcasted_iota(jnp.int32, (SUBLANES, HG_ROWS), 0)
        c = lax.broadcasted_iota(jnp.int32, (SUBLANES, HG_ROWS), 1)
        return (c & 3) == r
    r = lax.broadcasted_iota(jnp.int32, (HG_ROWS, HG_ROWS), 0)
    c = lax.broadcasted_iota(jnp.int32, (HG_ROWS, HG_ROWS), 1)
    same = (r & 3) == (c & 3)
    before = ((c >> 2) >= (r >> 2)) if rev else ((c >> 2) <= (r >> 2))
    return jnp.logical_and(same, before if kind == "cum" else jnp.logical_not(before))


def _split2(x):
    hi = x.astype(BF16)
    return hi, (x - hi.astype(F32)).astype(BF16)


@functools.partial(jax.custom_vjp, nondiff_argnums=(1, 2))
def _mask_sum(x, kind, rev):
    m = _hg_mask(kind, rev).astype(BF16)
    hi, lo = _split2(x)
    return _dot(m, hi) + _dot(m, lo)


def _mask_sum_fwd(x, kind, rev):
    return _mask_sum(x, kind, rev), None


def _mask_sum_bwd(kind, rev, _, g):
    m = _hg_mask(kind, rev).astype(BF16)
    hi, lo = _split2(g)
    return (_dot(m, hi, TN) + _dot(m, lo, TN),)


_mask_sum.defvjp(_mask_sum_fwd, _mask_sum_bwd)


def _hg_chunk(q, v, fraw, l0, l1, st, *, rev):
    nh = q.shape[1] // HG_HEAD
    lb = _sigmoid(l1 - l0)
    logf = jnp.logaddexp(jnp.log(lb), jnp.log1p(-lb) + jax.nn.log_sigmoid(fraw))
    kk = (1.0 - lb) * _sigmoid(fraw * -1.0)
    tri = _hg_mask("cum", rev)
    bcum = _mask_sum(logf, "cum", rev)
    brem = _mask_sum(logf, "rem", rev)
    bend8 = _mask_sum(logf, "tot", rev)
    r8d = lax.broadcasted_iota(jnp.int32, bend8.shape, 0)
    decs = [jnp.exp(jnp.sum(jnp.where(r8d == b, bend8, 0.0), axis=0, keepdims=True)) for b in range(LOCAL_B)]
    qd = (q * jnp.exp(bcum)).astype(BF16)
    kd = (kk * jnp.exp(-bcum)).astype(BF16)
    ke = (kk * jnp.exp(brem)).astype(BF16)
    wide = (HG_ROWS, LOCAL_B * HG_HEAD)
    mine = (lax.broadcasted_iota(jnp.int32, wide, 1) >> 7) == (lax.broadcasted_iota(jnp.int32, wide, 0) & 3)
    per_example = lambda x: jnp.where(mine, jnp.concatenate([x] * LOCAL_B, axis=1), jnp.zeros(wide, x.dtype))
    outs, new = [], []
    for h in range(nh):
        sl = slice(h * HG_HEAD, (h + 1) * HG_HEAD)
        vh = v[:, sl].astype(BF16)
        att = jnp.where(tri, _dot(qd[:, sl], kd[:, sl], NT), 0.0)
        outs.append(_dot(att.astype(BF16), vh) + _dot(per_example(qd[:, sl]), st[h].astype(BF16), NT))
        dec = jnp.concatenate([d[:, sl] for d in decs], axis=1)
        new.append(st[h] * dec + _dot(vh, per_example(ke[:, sl]), TN))
    return jnp.concatenate(outs, axis=1), tuple(new)


def _hg_chunk_of(step, ncc, nc, rev):
    return _s5_chunk_of(step, ncc, nc, rev)


def _hg_fwd_dir(zz, lb2, *, d, name, dims, xch=None):
    n = zz.shape[0]
    dm = zz.shape[1] // 5
    ns, sw = dm // HG_HEAD, LOCAL_B * HG_HEAD
    nc, ncc = n // HG_ROWS, dims["nctx"] // HG_ROWS
    rev = d == 1
    ch = lambda i: _hg_chunk_of(i, ncc, nc, rev)

    def body(q_ref, v_ref, f_ref, l0_ref, l1_ref, o_ref, st_ref, st):
        @pl.when(pl.program_id(0) == 0)
        def _():
            st[...] = jnp.zeros(st.shape, F32)

        st_ref[...] = st[...]
        o, new = _hg_chunk(q_ref[...], v_ref[...], f_ref[...], l0_ref[...], l1_ref[...],
                           tuple(st[j] for j in range(ns)), rev=rev)
        o_ref[...] = o
        for j in range(ns):
            st[j] = new[j]

    blk = lambda off: pl.BlockSpec((HG_ROWS, dm), lambda i, off=off: (ch(i), off))
    lspec = lambda layer: pl.BlockSpec((None, None, 1, dm), lambda i, layer=layer: (d, layer, 0, 0))
    return _call(
        body, name=name, xch=xch, args=[zz, zz, zz, lb2, lb2],
        out_shape=[jax.ShapeDtypeStruct((n, dm), F32), jax.ShapeDtypeStruct((nc, ns, HG_HEAD, sw), F32)],
        grid=(nc,),
        in_specs=[blk(0), blk(1), blk(2 + d), lspec(0), lspec(1)],
        out_specs=[pl.BlockSpec((HG_ROWS, dm), lambda i: (ch(i), 0)),
                   pl.BlockSpec((None, ns, HG_HEAD, sw), lambda i: (ch(i), 0, 0, 0))],
        scratch=[pltpu.VMEM((ns, HG_HEAD, sw), F32)],
        params=_params(("arbitrary",), 48 << 20))


def _hg_bwd_dir(zz, lb2, do, sts, dqv_prev, *, d, name, dims, xch=None):
    n = zz.shape[0]
    dm = zz.shape[1] // 5
    ns, sw = dm // HG_HEAD, LOCAL_B * HG_HEAD
    nc, ncc = n // HG_ROWS, dims["nctx"] // HG_ROWS
    rev = d == 1
    ch = lambda i: _hg_chunk_of(nc - 1 - i, ncc, nc, rev)
    qv_dtype = F32 if d == 0 else BF16

    def body(*refs):
        q_ref, v_ref, f_ref, l0_ref, l1_ref, do_ref, st_ref = refs[:7]
        pos = 7
        if d == 1:
            dqp_ref, dvp_ref = refs[7:9]
            pos = 9
        dq_ref, dv_ref, df_ref, dl_ref, dst = refs[pos:]
        i = pl.program_id(0)

        @pl.when(i == 0)
        def _():
            dst[...] = jnp.zeros(dst.shape, F32)
            dl_ref[...] = jnp.zeros(dl_ref.shape, F32)

        _, vjp = jax.vjp(functools.partial(_hg_chunk, rev=rev), q_ref[...], v_ref[...], f_ref[...], l0_ref[...],
                         l1_ref[...], tuple(st_ref[j] for j in range(ns)))
        dq, dv, df, dl0, dl1, dstn = vjp((do_ref[...], tuple(dst[j] for j in range(ns))))
        for j in range(ns):
            dst[j] = dstn[j]
        if d == 1:
            dq = dq + dqp_ref[...]
            dv = dv + dvp_ref[...]
        dq_ref[...] = dq.astype(qv_dtype)
        dv_ref[...] = dv.astype(qv_dtype)
        df_ref[...] = df.astype(BF16)
        dl_ref[0] += dl0
        dl_ref[1] += dl1

    blk = lambda off: pl.BlockSpec((HG_ROWS, dm), lambda i, off=off: (ch(i), off))
    oblk = pl.BlockSpec((HG_ROWS, dm), lambda i: (ch(i), 0))
    lspec = lambda layer: pl.BlockSpec((None, None, 1, dm), lambda i, layer=layer: (d, layer, 0, 0))
    ins = [zz, zz, zz, lb2, lb2, do, sts] + (list(dqv_prev) if d == 1 else [])
    in_specs = [blk(0), blk(1), blk(2 + d), lspec(0), lspec(1), oblk,
                pl.BlockSpec((None, ns, HG_HEAD, sw), lambda i: (ch(i), 0, 0, 0))]
    in_specs += [oblk, oblk] if d == 1 else []
    sd = jax.ShapeDtypeStruct
    return _call(
        body, name=name, xch=xch, args=ins,
        out_shape=[sd((n, dm), qv_dtype), sd((n, dm), qv_dtype), sd((n, dm), BF16), sd((2, 1, dm), F32)],
        grid=(nc,), in_specs=in_specs,
        out_specs=[oblk, oblk, oblk, pl.BlockSpec((2, 1, dm), lambda i: (0, 0, 0))],
        scratch=[pltpu.VMEM((ns, HG_HEAD, sw), F32)],
        params=_params(("arbitrary",), 56 << 20))


def _hg_readout(o, g, w):
    outs = []
    for h in range(o.shape[-1] // HG_HEAD):
        sl = slice(h * HG_HEAD, (h + 1) * HG_HEAD)
        oh = o[..., sl]
        outs.append(oh * _rms(oh) * w * _sigmoid(g[..., sl]))
    return jnp.concatenate(outs, axis=-1)


def _silu(x):
    return x * _sigmoid(x)


def _mod_fwd(craw, w, b):
    def body(c_ref, w_ref, b_ref, o_ref):
        s = _silu(c_ref[...]).astype(BF16)
        for layer in range(w.shape[0]):
            o_ref[layer] = _dot(s, w_ref[layer].astype(BF16)) + b_ref[layer]

    return pl.pallas_call(body, name="mod_fwd",
                          out_shape=jax.ShapeDtypeStruct((w.shape[0], craw.shape[0], w.shape[2]), F32),
                          compiler_params=_params(None, 40 << 20))(craw, w, b)


def _mod_bwd(craw, w, dlat_sh, dctx_sh, dlat_full, dctx_full):
    nl, dm, ns = w.shape
    nb = dlat_sh.shape[1]

    def body(c_ref, w_ref, dl_ref, dc_ref, dlf_ref, dcf_ref, dw_ref, db_ref, dcc_ref):
        craw_v = c_ref[...]
        s = _silu(craw_v)
        s_lat = s[:nb].astype(BF16)
        s_ctx = s[nb:].astype(BF16)
        row = lax.broadcasted_iota(jnp.int32, (SUBLANES, ns), 0)
        dsc = jnp.zeros((SUBLANES, dm), F32)
        for layer in range(nl):
            tot = dc_ref[0, pl.ds(layer, 1), :]
            totf = dcf_ref[0, pl.ds(layer, 1), :]
            for i in range(1, NDEV):
                tot = tot + dc_ref[i, pl.ds(layer, 1), :]
                totf = totf + dcf_ref[i, pl.ds(layer, 1), :]
            dc8 = jnp.where(row == 0, jnp.broadcast_to(tot, (SUBLANES, ns)), 0.0).astype(BF16)
            dw_ref[layer] = _dot(s_lat, dl_ref[layer].astype(BF16), TN) + _dot(s_ctx, dc8, TN)
            db_ref[layer] = jnp.sum(dlf_ref[layer], axis=0, keepdims=True) + totf
            dsc = dsc + _dot(dc8, w_ref[layer].astype(BF16), NT)
        cc = craw_v[nb:]
        sg = _sigmoid(cc)
        dcc_ref[...] = dsc * (sg * (1.0 + cc * (1.0 - sg)))

    sd = jax.ShapeDtypeStruct
    return pl.pallas_call(body, name="mod_bwd",
                          out_shape=[sd((nl, dm, ns), F32), sd((nl, 1, dlat_full.shape[2]), F32), sd((SUBLANES, dm), F32)],
                          compiler_params=_params(None, 48 << 20))(craw, w, dlat_sh, dctx_sh, dlat_full, dctx_full)


def _adam_rows(r):
    best = None
    for t in range(2 * SUBLANES, min(r, 128) + 1, 2 * SUBLANES):
        if r % t == 0:
            best = t
    return best if best is not None else r


def _adamw(parts, w, m, v, *, name):
    npart, r, c = parts.shape
    tr = _adam_rows(r)

    def body(p_ref, w_ref, m_ref, v_ref, g_ref, d_ref, nm_ref, nv_ref):
        g = p_ref[0].astype(F32)
        for i in range(1, npart):
            g = g + p_ref[i].astype(F32)
        nm = ADAM_B1 * m_ref[...] + (1.0 - ADAM_B1) * g
        nv = ADAM_B2 * v_ref[...] + (1.0 - ADAM_B2) * (g * g)
        m_hat = nm / (1.0 - ADAM_B1 ** ADAM_STEP)
        v_hat = nv / (1.0 - ADAM_B2 ** ADAM_STEP)
        g_ref[...] = g
        d_ref[...] = -ADAM_LR * (m_hat / (jnp.sqrt(v_hat) + ADAM_EPS) + ADAM_WD * w_ref[...])
        nm_ref[...] = nm
        nv_ref[...] = nv

    spec = pl.BlockSpec((tr, c), lambda i: (i, 0))
    vmem = 2 * (npart + 7) * tr * c * 4 + (8 << 20)
    return pl.pallas_call(
        body, name=name, out_shape=[jax.ShapeDtypeStruct((r, c), F32)] * 4, grid=(r // tr,),
        in_specs=[pl.BlockSpec((npart, tr, c), lambda i: (0, i, 0)), spec, spec, spec], out_specs=[spec] * 4,
        compiler_params=_params(("parallel",), vmem),
    )(parts, w, m, v)


def _to_tm(a):
    return jnp.transpose(a, (1, 0, 2)).reshape(a.shape[1] * a.shape[0], a.shape[2])


def _pattern(mod_lat, mod_ctx, m, dm):
    lat = mod_lat[:, m * dm:(m + 1) * dm]
    ctx = jnp.broadcast_to(mod_ctx[None, m * dm:(m + 1) * dm], (SUBLANES, dm))
    return jnp.stack([ctx, jnp.concatenate([lat, lat], axis=0)])


def _blockdiag_b(bt, nk):
    g, h, p = bt.shape
    t = bt.reshape(nk, 8, h, p)
    return jnp.einsum("kghp,gj->kghjp", t, jnp.eye(8, dtype=bt.dtype)).reshape(nk, 8 * h, 8 * p)


def _blockdiag_c(ct, nk):
    g, h, p = ct.shape
    t = ct.reshape(nk, 8, h, p)
    return jnp.einsum("kghp,gj->kgpjh", t, jnp.eye(8, dtype=ct.dtype)).reshape(nk, 8 * p, 8 * h)


def _diag_b(dbb, h, p):
    nk = dbb.shape[0]
    return jnp.einsum("kghgp->kghp", dbb.reshape(nk, 8, h, 8, p)).reshape(nk * 8, h, p)


def _diag_c(dcc, h, p):
    nk = dcc.shape[0]
    return jnp.einsum("kgpgh->kghp", dcc.reshape(nk, 8, p, 8, h)).reshape(nk * 8, h, p)


def kernel(x, c, ctx, c_ctx, w_mod, b_mod, norm1_w, norm2_w, final_norm_w, s5_w_in, s5_lam_re, s5_lam_im, s5_log_step, s5_b_re, s5_b_im, s5_c_re, s5_c_im, s5_d, s5_w_glu, s5_w_out, hg_w_in, hg_lower_bounds, hg_gnorm_w, hg_w_out, ffn_w_up, ffn_conv_w, ffn_conv_b, ffn_w_down, loss_target, m_c_ctx, m_w_mod, m_b_mod, m_norm1_w, m_norm2_w, m_final_norm_w, m_s5_w_in, m_s5_lam_re, m_s5_lam_im, m_s5_log_step, m_s5_b_re, m_s5_b_im, m_s5_c_re, m_s5_c_im, m_s5_d, m_s5_w_glu, m_s5_w_out, m_hg_w_in, m_hg_lower_bounds, m_hg_gnorm_w, m_hg_w_out, m_ffn_w_up, m_ffn_conv_w, m_ffn_conv_b, m_ffn_w_down, v_c_ctx, v_w_mod, v_b_mod, v_norm1_w, v_norm2_w, v_final_norm_w, v_s5_w_in, v_s5_lam_re, v_s5_lam_im, v_s5_log_step, v_s5_b_re, v_s5_b_im, v_s5_c_re, v_s5_c_im, v_s5_d, v_s5_w_glu, v_s5_w_out, v_hg_w_in, v_hg_lower_bounds, v_hg_gnorm_w, v_hg_w_out, v_ffn_w_up, v_ffn_conv_w, v_ffn_conv_b, v_ffn_w_down):
    given = dict(locals())
    bsz, lx, dm = x.shape
    lc = ctx.shape[1]
    assert bsz == LOCAL_B and w_mod.shape[0] == 2 and dm % LANES == 0
    n, nctx = (lc + lx) * bsz, lc * bsz
    ngrp, nstate, hgrp = dm // S5_GROUP, S5_STATE, S5_GROUP
    nk = dm // LANES
    dims = dict(nctx=nctx, tm=min(512, nctx), tm_row=min(256, nctx), s5_rows=min(256, nctx),
                s5_lane_block=min(512, 8 * nstate))
    tm = dims["tm"]
    assert nctx % HG_ROWS == 0 and (lx * bsz) % nctx == 0 and lc % GRID_W == 0 and lc & (lc - 1) == 0
    me = 4 * lax.axis_index("x") + 2 * lax.axis_index("y") + lax.axis_index("c")

    gath = _exchange([given[k].astype(BF16) for k in ("s5_w_in", "s5_w_glu", "s5_w_out")]
                     + [c, hg_lower_bounds, ffn_conv_w], a2a=False, name="gather_weights")
    w_s5in, w_glu, w_s5out = (g.reshape(dm, dm) for g in gath[:3])
    c_all, lb_all, cw_all = gath[3:]
    ns_up = ffn_w_up.shape[2]
    w_up, w_dn = [None, None], [None, None]
    cols = lambda g: jnp.transpose(g, (1, 0, 2)).reshape(g.shape[1], -1)
    shards = lambda w: jnp.transpose(w.reshape(w.shape[0], NDEV, -1), (1, 0, 2))
    tn_up, tn_hg, tkr = 2 * ns_up, 2 * hg_w_in.shape[2], 1152
    assert n % tkr == 0 and n % 1024 == 0
    gather = lambda arrs: _Xchg([a.astype(BF16) for a in arrs], [False] * len(arrs))
    scatter = lambda arrs: _Xchg(arrs, [True] * len(arrs))
    cw = [cols(cw_all[:, layer]) for layer in range(2)]
    cb = [ffn_conv_b[layer].reshape(1, -1) for layer in range(2)]
    lb2 = jnp.transpose(lb_all, (1, 2, 0, 3)).reshape(2, 2, 1, dm)

    nsm = w_mod.shape[2]
    craw = jnp.concatenate([c_all.reshape(NDEV * bsz, dm), c_ctx[None], jnp.zeros((SUBLANES - 1, dm), F32)], axis=0)
    b_sh = lax.dynamic_slice(b_mod, (0, me * nsm), (2, nsm)).reshape(2, 1, nsm)
    mod_sh = _mod_fwd(craw, w_mod, b_sh)
    (mod_g,) = _exchange([mod_sh], a2a=False, name="gather_mod")
    mod_full = jnp.transpose(mod_g, (1, 2, 0, 3)).reshape(2, craw.shape[0], NDEV * nsm)
    pat = []
    for layer in range(2):
        mlat = lax.dynamic_slice(mod_full[layer], (me * bsz, 0), (bsz, N_MOD * dm))
        mctx = mod_full[layer, NDEV * bsz]
        pat.append([_pattern(mlat, mctx, m, dm) for m in range(N_MOD)])

    lr4 = s5_lam_re[0].reshape(2, ngrp, 1, nstate)
    li4 = s5_lam_im[0].reshape(2, ngrp, 1, nstate)
    ls4 = s5_log_step[0].reshape(2, ngrp, 1, 1)
    brt = jnp.transpose(s5_b_re[0], (0, 1, 3, 2))
    bit = jnp.transpose(s5_b_im[0], (0, 1, 3, 2))
    abar_r, abar_i, bbar_r, bbar_i = _s5_disc_fwd(lr4, li4, ls4, brt, bit)
    sw = ngrp * nstate
    a8r = [jnp.broadcast_to(abar_r[d].reshape(1, sw), (SUBLANES, sw)) for d in range(2)]
    a8i = [jnp.broadcast_to(abar_i[d].reshape(1, sw), (SUBLANES, sw)) for d in range(2)]
    bbr = [_blockdiag_b(bbar_r[d], nk).astype(BF16) for d in range(2)]
    bbi = [_blockdiag_b(bbar_i[d], nk).astype(BF16) for d in range(2)]
    ccr = [_blockdiag_c(s5_c_re[0, d], nk).astype(BF16) for d in range(2)]
    cci = [_blockdiag_c(s5_c_im[0, d], nk).astype(BF16) for d in range(2)]
    dsk = s5_d.reshape(1, dm)

    z0 = jnp.concatenate([_to_tm(ctx), _to_tm(x)], axis=0)
    tgt = _to_tm(loss_target)
    n1w = [norm1_w[layer].reshape(1, dm) for layer in range(2)]
    n2w = [norm2_w[layer].reshape(1, dm) for layer in range(2)]

    def ffn_fwd(layer, h2):
        u = _lin(h2, w_up[layer], name=f"ffn_up{layer}", tm=1024, tn=tn_up)
        hm = _convffn_fwd(u, cw[layer], cb[layer], name=f"convffn_fwd{layer}", dims=dims)
        f = _lin(hm, w_dn[layer], name=f"ffn_down{layer}", tm=tm, tn=dm)
        return u, hm, f

    _, h0 = _norm_mod_fwd(z0, n1w[0], pat[0][0], pat[0][1], name="norm1_l0", dims=dims)
    u_s5 = _lin(h0, w_s5in, name="s5_in", tm=1024, tn=dm)
    (y_a, st0r, st0i), (g_up0, g_dn0) = _s5_fwd_dir(
        u_s5, None, a8r[0], a8i[0], bbr[0], bbi[0], ccr[0], cci[0], dsk, d=0, name="s5_fwd_d0", dims=dims,
        xch=gather([ffn_w_up[0], ffn_w_down[0]]))
    w_up[0], w_dn[0] = cols(g_up0), g_dn0.reshape(-1, dm)
    (y_s5, st1r, st1i), (g_hgin, g_hgout, g_dn1) = _s5_fwd_dir(
        u_s5, y_a, a8r[1], a8i[1], bbr[1], bbi[1], ccr[1], cci[1], dsk, d=1, name="s5_fwd_d1", dims=dims,
        xch=gather([hg_w_in[0], hg_w_out[0], ffn_w_down[1]]))
    w_hgin, w_hgout, w_dn[1] = cols(g_hgin), g_hgout.reshape(dm, dm), g_dn1.reshape(-1, dm)
    (zg,) = _rowk(lambda rv, pv, cv, il: ([_gelu(rv[0])], [], []), name="s5_gelu", n=n, tm=dims["tm_row"], nctx=nctx,
                  rows=[(y_s5, dm, 0, 0)], out_rows=[(dm, BF16, dm, 0)])
    t_glu = _lin(zg, w_glu, name="s5_glu", tm=1024, tn=dm)
    (z2g,) = _rowk(lambda rv, pv, cv, il: ([rv[0] * _sigmoid(rv[1])], [], []), name="s5_gate", n=n,
                   tm=dims["tm_row"], nctx=nctx, rows=[(zg, dm, 0, 0), (t_glu, dm, 0, 0)],
                   out_rows=[(dm, BF16, dm, 0)])
    ymix0 = _lin(z2g, w_s5out, name="s5_out", tm=1024, tn=dm)
    z1_l0, h2_l0 = _norm_mod_fwd(z0, n2w[0], pat[0][3], pat[0][4], name="norm2_l0", dims=dims,
                                 res=(ymix0, pat[0][2]))
    u_l0, hm_l0, f_l0 = ffn_fwd(0, h2_l0)

    z2_l0, h1 = _norm_mod_fwd(z1_l0, n1w[1], pat[1][0], pat[1][1], name="norm1_l1", dims=dims,
                              res=(f_l0, pat[0][5]))
    zz = _lin(h1, w_hgin, name="hg_in", tm=1024, tn=tn_hg)
    (o_f, sts_f), (g_up1,) = _hg_fwd_dir(zz, lb2, d=0, name="hg_fwd_d0", dims=dims, xch=gather([ffn_w_up[1]]))
    w_up[1] = cols(g_up1)
    (o_b, sts_b), _ = _hg_fwd_dir(zz, lb2, d=1, name="hg_fwd_d1", dims=dims)
    gnw = hg_gnorm_w.reshape(1, HG_HEAD)
    (og,) = _rowk(lambda rv, pv, cv, il: ([_hg_readout(rv[0] + rv[1], rv[2], cv[0])], [], []), name="hg_readout",
                  n=n, tm=dims["tm_row"], nctx=nctx, rows=[(o_f, dm, 0, 0), (o_b, dm, 0, 0), (zz, dm, 4, 0)],
                  consts=[gnw], out_rows=[(dm, BF16, dm, 0)])
    ymix1 = _lin(og, w_hgout, name="hg_out", tm=1024, tn=dm)
    z1_l1, h2_l1 = _norm_mod_fwd(z2_l0, n2w[1], pat[1][3], pat[1][4], name="norm2_l1", dims=dims,
                                 res=(ymix1, pat[1][2]))
    u_l1, hm_l1, f_l1 = ffn_fwd(1, h2_l1)

    dz, df, dgate2_l1, loss_part, dfinal_w = _loss_bwd(z1_l1, f_l1, pat[1][5], tgt, final_norm_w.reshape(1, dm),
                                                        name="loss_bwd", dims=dims)

    def ffn_bwd(layer, df_, u, hm, h2, xch=None):
        dff = hm.shape[1]
        dhm = _lin(df_, w_dn[layer], name=f"ffn_down_bwd_in{layer}", trans_w=True, tm=1024, tn=dff // 2, o_dtype=BF16)
        dwd = _lin_w(hm, df_, name=f"ffn_down_bwd_w{layer}", ta=dff // 2, tn=dm, tkr=tkr)
        (dua, dug, dcwa, dcwg, dcba, dcbg), got = _convffn_bwd(u, dhm, cw[layer], cb[layer],
                                                               name=f"convffn_bwd{layer}", dims=dims, xch=xch)
        dh2 = _lin(dua, w_up[layer], name=f"ffn_up_bwd_in_a{layer}", trans_w=True, tm=tm, tn=dm, kblk=0)
        dh2 = _lin(dug, w_up[layer], name=f"ffn_up_bwd_in_g{layer}", trans_w=True, tm=tm, tn=dm, kblk=1, base=dh2)
        dwu = jnp.concatenate([_lin_w(h2, dua, name=f"ffn_up_bwd_w_a{layer}", ta=dm, tn=tn_up, tkr=tkr),
                               _lin_w(h2, dug, name=f"ffn_up_bwd_w_g{layer}", ta=dm, tn=tn_up, tkr=tkr)], axis=1)
        dcw = shards(jnp.concatenate([dcwa, dcwg], axis=1))
        return dh2, shards(dwu), dwd, dcw, jnp.concatenate([dcba, dcbg], axis=1), got

    dh2, dwu_l1, dwd_l1, dcw_l1, dcb_l1, _ = ffn_bwd(1, df, u_l1, hm_l1, h2_l1)
    dz, dymix, dsh2_l1, dsc2_l1, dgate1_l1, dn2w_l1 = _norm_mod_bwd(dh2, z1_l1, dz, n2w[1], pat[1][4], name="norm2_bwd_l1",
                                                                    dims=dims, res=(ymix1, pat[1][2]))
    dog = _lin(dymix, w_hgout, name="hg_out_bwd_in", trans_w=True, tm=1024, tn=dm)
    dw_hgout = _lin_w(og, dymix, name="hg_out_bwd_w", ta=dm, tn=dm, tkr=tkr)

    def readout_bwd(rv, pv, cv, il):
        _, vjp = jax.vjp(_hg_readout, rv[0] + rv[1], rv[2], cv[0])
        do, dg, dw = vjp(rv[3])
        return [do, dg], [], [jnp.broadcast_to(dw, (SUBLANES, HG_HEAD)) * (1.0 / SUBLANES)]

    do, dg, dgnw = _rowk(readout_bwd, name="hg_readout_bwd", n=n, tm=dims["tm_row"], nctx=nctx,
                         rows=[(o_f, dm, 0, 0), (o_b, dm, 0, 0), (zz, dm, 4, 0), (dog, dm, 0, 0)], consts=[gnw],
                         out_rows=[(dm, F32, dm, 0), (dm, BF16, dm, 0)], out_acc=[HG_HEAD])
    (dq0, dv0, dff, dl_f), (p_up1, p_dn1) = _hg_bwd_dir(
        zz, lb2, do, sts_f, None, d=0, name="hg_bwd_d0", dims=dims,
        xch=scatter([dwu_l1, dwd_l1.reshape(NDEV, -1, dm)]))
    (dq, dv, dfb, dl_b), (p_hgout,) = _hg_bwd_dir(
        zz, lb2, do, sts_b, (dq0, dv0), d=1, name="hg_bwd_d1", dims=dims,
        xch=scatter([dw_hgout.reshape(NDEV, -1, dm)]))
    dzz = jnp.concatenate([dq, dv, dff, dfb, dg], axis=1)
    dh1 = _lin(dzz, w_hgin, name="hg_in_bwd_in", trans_w=True, tm=tm, tn=dm)
    dw_hgin = shards(_lin_w(h1, dzz, name="hg_in_bwd_w", ta=dm, tn=tn_hg, tkr=tkr))
    dz, df0, dsh1_l1, dsc1_l1, dgate2_l0, dn1w_l1 = _norm_mod_bwd(dh1, z2_l0, dz, n1w[1], pat[1][1], name="norm1_bwd_l1",
                                                                  dims=dims, res=(f_l0, pat[0][5]))
    dh2, dwu_l0, dwd_l0, dcw_l0, dcb_l0, _ = ffn_bwd(0, df0, u_l0, hm_l0, h2_l0)
    dz, dymix, dsh2_l0, dsc2_l0, dgate1_l0, dn2w_l0 = _norm_mod_bwd(dh2, z1_l0, dz, n2w[0], pat[0][4], name="norm2_bwd_l0",
                                                                    dims=dims, res=(ymix0, pat[0][2]))
    dz2g = _lin(dymix, w_s5out, name="s5_out_bwd_in", trans_w=True, tm=1024, tn=dm)
    dw_s5out = _lin_w(z2g, dymix, name="s5_out_bwd_w", ta=dm, tn=dm, tkr=tkr)

    def gate_bwd(rv, pv, cv, il):
        sg = _sigmoid(rv[1])
        return [rv[2] * rv[0] * sg * (1.0 - sg), rv[2] * sg], [], []

    dt_glu, dzg_a = _rowk(gate_bwd, name="s5_gate_bwd", n=n, tm=dims["tm_row"], nctx=nctx,
                          rows=[(zg, dm, 0, 0), (t_glu, dm, 0, 0), (dz2g, dm, 0, 0)],
                          out_rows=[(dm, BF16, dm, 0), (dm, F32, dm, 0)])
    dzg_b = _lin(dt_glu, w_glu, name="s5_glu_bwd_in", trans_w=True, tm=1024, tn=dm)
    dw_glu = _lin_w(zg, dt_glu, name="s5_glu_bwd_w", ta=dm, tn=dm, tkr=tkr)

    def gelu_bwd(rv, pv, cv, il):
        _, vjp = jax.vjp(_gelu, rv[0])
        return [vjp(rv[1] + rv[2])[0]], [], []

    (dy_s5,) = _rowk(gelu_bwd, name="s5_gelu_bwd", n=n, tm=dims["tm_row"], nctx=nctx,
                     rows=[(y_s5, dm, 0, 0), (dzg_a, dm, 0, 0), (dzg_b, dm, 0, 0)], out_rows=[(dm, F32, dm, 0)])
    dcw_both = jnp.stack([dcw_l0, dcw_l1], axis=1)
    (du_a, dbbr0, dbbi0, dccr0, dcci0, dar0, dai0, ddsk), (p_up0, p_dn0, p_cw) = _s5_bwd_dir(
        u_s5, dy_s5, None, a8r[0], a8i[0], bbr[0], bbi[0], ccr[0], cci[0], dsk, st0r, st0i, d=0, name="s5_bwd_d0",
        dims=dims, xch=scatter([dwu_l0, dwd_l0.reshape(NDEV, -1, dm), dcw_both]))
    (du_s5, dbbr1, dbbi1, dccr1, dcci1, dar1, dai1), (p_s5out, p_glu, p_hgin) = _s5_bwd_dir(
        u_s5, dy_s5, du_a, a8r[1], a8i[1], bbr[1], bbi[1], ccr[1], cci[1], dsk, st1r, st1i, d=1, name="s5_bwd_d1",
        dims=dims, xch=scatter([dw_s5out.reshape(NDEV, -1, dm), dw_glu.reshape(NDEV, -1, dm), dw_hgin]))
    dh0 = _lin(du_s5, w_s5in, name="s5_in_bwd_in", trans_w=True, tm=1024, tn=dm)
    dw_s5in = _lin_w(h0, du_s5, name="s5_in_bwd_w", ta=dm, tn=dm, tkr=tkr)
    dz0, dsh1_l0, dsc1_l0, dn1w_l0 = _norm_mod_bwd(dh0, z0, dz, n1w[0], pat[0][1], name="norm1_bwd_l0", dims=dims)

    dar = jnp.stack([dar0, dar1]).reshape(2, ngrp, 1, nstate)
    dai = jnp.stack([dai0, dai1]).reshape(2, ngrp, 1, nstate)
    dbbr = jnp.stack([_diag_b(dbbr0, hgrp, nstate), _diag_b(dbbr1, hgrp, nstate)])
    dbbi = jnp.stack([_diag_b(dbbi0, hgrp, nstate), _diag_b(dbbi1, hgrp, nstate)])
    dlr, dli, dls, dbrt, dbit = _s5_disc_bwd(lr4, li4, ls4, brt, bit, dar, dai, dbbr, dbbi)
    g_c_re = jnp.stack([_diag_c(dccr0, hgrp, nstate), _diag_c(dccr1, hgrp, nstate)])
    g_c_im = jnp.stack([_diag_c(dcci0, hgrp, nstate), _diag_c(dcci1, hgrp, nstate)])

    dmod = jnp.stack([
        jnp.concatenate([dsh1_l0, dsc1_l0, dgate1_l0, dsh2_l0, dsc2_l0, dgate2_l0], axis=1),
        jnp.concatenate([dsh1_l1, dsc1_l1, dgate1_l1, dsh2_l1, dsc2_l1, dgate2_l1], axis=1)])
    dmod_g, p_s5in = _exchange([dmod, dw_s5in.reshape(NDEV, -1, dm)], a2a=[False, True], name="gather_dmod")
    dlat_full = jnp.transpose(dmod_g[:, :, :bsz], (1, 0, 2, 3)).reshape(2, NDEV * bsz, N_MOD * dm)
    dctx_full = dmod_g[:, :, bsz]
    dlat_sh = lax.dynamic_slice(dlat_full, (0, 0, me * nsm), (2, NDEV * bsz, nsm))
    dctx_sh = lax.dynamic_slice(dctx_full, (0, 0, me * nsm), (NDEV, 2, nsm))
    g_w_mod, g_b_mod, dcctx8 = _mod_bwd(craw, w_mod, dlat_sh, dctx_sh, dlat_full, dctx_full)

    dl_hg = jnp.stack([dl_f[:, 0], dl_b[:, 0]])
    wide = lambda g: g.reshape(-1, dm)
    small = [("c_ctx", wide(dcctx8[:1])), ("norm1_w", jnp.concatenate([dn1w_l0, dn1w_l1])),
             ("norm2_w", jnp.concatenate([dn2w_l0, dn2w_l1])), ("final_norm_w", dfinal_w),
             ("s5_lam_re", dlr.reshape(-1, nstate)), ("s5_lam_im", dli.reshape(-1, nstate)),
             ("s5_log_step", dls.reshape(2, ngrp)),
             ("s5_b_re", wide(jnp.transpose(dbrt, (0, 1, 3, 2)).astype(BF16))),
             ("s5_b_im", wide(jnp.transpose(dbit, (0, 1, 3, 2)).astype(BF16))),
             ("s5_c_re", wide(g_c_re.astype(BF16))), ("s5_c_im", wide(g_c_im.astype(BF16))), ("s5_d", ddsk),
             ("hg_gnorm_w", dgnw), ("ffn_conv_b", jnp.stack([dcb_l0.reshape(-1), dcb_l1.reshape(-1)]))]
    gathered = _exchange([g for _, g in small] + [wide(dl_hg), loss_part], a2a=False, name="gather_small")
    res = {}
    for (k, g), parts in zip(small, gathered):
        w2, m2, v2 = (given[p + k].reshape(g.shape) for p in ("", "m_", "v_"))
        res[k] = tuple(o.reshape(given[k].shape) for o in _adamw(parts, w2, m2, v2, name="adamw_" + k))

    def total(parts, name):
        z = jnp.zeros(parts.shape[1:], F32)
        return _adamw(parts, z, z, z, name=name)[0]

    loss = jnp.sum(total(gathered[-1], "sum_loss"))
    dl_tot = total(gathered[-2], "sum_dlb").reshape(dl_hg.shape)
    nlb = hg_lower_bounds.shape[2]
    g_lb = lax.dynamic_slice(dl_tot, (0, 0, me * nlb), (2, 2, nlb))

    def adam_local(name, g, shape2):
        w, m, v = given[name], given["m_" + name], given["v_" + name]
        out = _adamw(g.reshape((1,) + shape2), w.reshape(shape2), m.reshape(shape2), v.reshape(shape2),
                     name="adamw_" + name)
        return tuple(o.reshape(w.shape) for o in out)

    def adam_parts(name, p):
        w, m, v = given[name], given["m_" + name], given["v_" + name]
        shape2 = (p.shape[0], -1, w.shape[-1])
        p3 = p.reshape(shape2)
        s2 = p3.shape[1:]
        out = _adamw(p3, w.reshape(s2), m.reshape(s2), v.reshape(s2), name="adamw_" + name)
        return tuple(o.reshape(w.shape) for o in out)

    res["hg_lower_bounds"] = adam_local("hg_lower_bounds", g_lb, (2 * 2, nlb))
    res["w_mod"] = adam_local("w_mod", g_w_mod, (2 * dm, nsm))
    res["b_mod"] = adam_local("b_mod", g_b_mod, (2, N_MOD * dm))
    res["s5_w_in"] = adam_parts("s5_w_in", p_s5in)
    res["s5_w_glu"] = adam_parts("s5_w_glu", p_glu)
    res["s5_w_out"] = adam_parts("s5_w_out", p_s5out)
    res["hg_w_in"] = adam_parts("hg_w_in", p_hgin)
    res["hg_w_out"] = adam_parts("hg_w_out", p_hgout)
    res["ffn_w_up"] = adam_parts("ffn_w_up", jnp.stack([p_up0, p_up1], axis=1))
    res["ffn_w_down"] = adam_parts("ffn_w_down", jnp.stack([p_dn0, p_dn1], axis=1))
    res["ffn_conv_w"] = adam_parts("ffn_conv_w", p_cw)

    grad_x = jnp.transpose(dz0[nctx:].reshape(lx, bsz, dm), (1, 0, 2))
    order = ["c_ctx", "w_mod", "b_mod", "norm1_w", "norm2_w", "final_norm_w", "s5_w_in", "s5_lam_re", "s5_lam_im",
             "s5_log_step", "s5_b_re", "s5_b_im", "s5_c_re", "s5_c_im", "s5_d", "s5_w_glu", "s5_w_out", "hg_w_in",
             "hg_lower_bounds", "hg_gnorm_w", "hg_w_out", "ffn_w_up", "ffn_conv_w", "ffn_conv_b", "ffn_w_down"]
    outs = [loss, grad_x]
    for j in range(4):
        outs += [res[k][j].reshape(given[k].shape) for k in order]
    return tuple(outs)
```

```python
import functools

import jax
import jax.numpy as jnp
from jax import lax
from jax.experimental import pallas as pl
from jax.experimental.pallas import tpu as pltpu

F32 = jnp.float32
BF16 = jnp.bfloat16
NDEV = 8
LOCAL_B = 4
NORM_EPS = 1e-6
N_MOD = 6
S5_GROUP = 16
S5_STATE = 64
S5_LAM_RE_MAX = -1e-4
HG_HEAD = 128
HG_ROWS = 128
GRID_W = 64
ADAM_LR, ADAM_B1, ADAM_B2, ADAM_EPS, ADAM_WD, ADAM_STEP = 0.001, 0.9, 0.999, 1e-08, 0.01, 10
VMEM_BYTES_V7X = 64 * 1024 * 1024
LANES = 128
SUBLANES = 8

NN = (((1,), (0,)), ((), ()))
NT = (((1,), (1,)), ((), ()))
TN = (((0,), (0,)), ((), ()))
MESH = pl.DeviceIdType.MESH


def _params(sem=None, vmem=None):
    kw = {}
    if sem is not None:
        kw["dimension_semantics"] = sem
    if vmem is not None:
        kw["vmem_limit_bytes"] = int(min(vmem, VMEM_BYTES_V7X - (4 << 20)))
    return pltpu.CompilerParams(**kw)


def _nbytes(shape, dtype):
    n = 1
    for s in shape:
        n *= 1 if s is None else s
    return n * jnp.dtype(dtype).itemsize


def _dot(a, b, dims=NN, precision=None):
    return lax.dot_general(a, b, dims, preferred_element_type=F32, precision=precision)


def _sigmoid(x):
    return 1.0 / (1.0 + jnp.exp(-x))


class _Xchg:
    def __init__(self, arrs, a2a):
        self.arrs, self.a2a, self.n = list(arrs), list(a2a), len(arrs)

    def out_shape(self):
        return [jax.ShapeDtypeStruct(a.shape if f else (NDEV,) + a.shape, a.dtype) for a, f in zip(self.arrs, self.a2a)]

    def scratch(self):
        return [pltpu.SemaphoreType.DMA((self.n * (NDEV - 1),)), pltpu.SemaphoreType.DMA((self.n * (NDEV - 1),)),
                pltpu.SemaphoreType.DMA((self.n,))]

    def _copies(self, ins, outs, sems, with_recvs):
        send_sems, recv_sems, loc_sems = sems
        x, y, c = lax.axis_index("x"), lax.axis_index("y"), lax.axis_index("c")
        me = 4 * x + 2 * y + c
        local, sends, recvs = [], [], []
        for a in range(self.n):
            src = ins[a].at[me] if self.a2a[a] else ins[a]
            local.append(pltpu.make_async_copy(src, outs[a].at[me], loc_sems.at[a]))
            for k in range(1, NDEV):
                px = (1 - x) if (k >> 2) & 1 else x
                py = (1 - y) if (k >> 1) & 1 else y
                pc = (1 - c) if k & 1 else c
                p = 4 * px + 2 * py + pc
                s = a * (NDEV - 1) + k - 1
                src = ins[a].at[p] if self.a2a[a] else ins[a]
                kw = dict(src_ref=src, send_sem=send_sems.at[s], recv_sem=recv_sems.at[s], device_id=(px, py, pc),
                          device_id_type=MESH)
                sends.append(pltpu.make_async_remote_copy(dst_ref=outs[a].at[me], **kw))
                if with_recvs:
                    recvs.append(pltpu.make_async_remote_copy(dst_ref=outs[a].at[p], **kw))
        return local, sends, recvs

    def start(self, ins, outs, sems):
        local, sends, _ = self._copies(ins, outs, sems, False)
        for cp in local + sends:
            cp.start()

    def wait(self, ins, outs, sems):
        local, sends, recvs = self._copies(ins, outs, sems, True)
        for cp in sends:
            cp.wait_send()
        for cp in recvs:
            cp.wait_recv()
        for cp in local:
            cp.wait()


def _exchange(arrs, *, a2a, name):
    xch = _Xchg(arrs, a2a if isinstance(a2a, (list, tuple)) else [a2a] * len(arrs))
    n = xch.n

    def body(*refs):
        xch.start(refs[:n], refs[n:2 * n], refs[2 * n:])
        xch.wait(refs[:n], refs[n:2 * n], refs[2 * n:])

    res = pl.pallas_call(
        body, name=name, out_shape=xch.out_shape(),
        in_specs=[pl.BlockSpec(memory_space=pl.ANY)] * n, out_specs=[pl.BlockSpec(memory_space=pl.ANY)] * n,
        scratch_shapes=xch.scratch(),
    )(*arrs)
    return list(res)


def _call(body, *, name, out_shape, grid, in_specs, out_specs, scratch, params, args, xch=None):
    in_specs, out_specs, out_shape, scratch, args = list(in_specs), list(out_specs), list(out_shape), list(scratch), list(args)
    n_in, n_out, n_scr = len(in_specs), len(out_shape), len(scratch)
    if xch is not None:
        k = xch.n
        inner = body

        def body(*refs):
            ins, xin = refs[:n_in], refs[n_in:n_in + k]
            outs, xout = refs[n_in + k:n_in + k + n_out], refs[n_in + k + n_out:n_in + 2 * k + n_out]
            scr = refs[n_in + 2 * k + n_out:n_in + 2 * k + n_out + n_scr]
            sems = refs[n_in + 2 * k + n_out + n_scr:]
            first = pl.program_id(0) == 0
            last = pl.program_id(0) == grid[0] - 1
            for ax in range(1, len(grid)):
                first = jnp.logical_and(first, pl.program_id(ax) == 0)
                last = jnp.logical_and(last, pl.program_id(ax) == grid[ax] - 1)

            @pl.when(first)
            def _():
                xch.start(xin, xout, sems)

            inner(*ins, *outs, *scr)

            @pl.when(last)
            def _():
                xch.wait(xin, xout, sems)

        anyspec = pl.BlockSpec(memory_space=pl.ANY)
        in_specs += [anyspec] * k
        out_specs += [anyspec] * k
        out_shape += xch.out_shape()
        scratch += xch.scratch()
        args += xch.arrs
    res = pl.pallas_call(body, name=name, out_shape=out_shape, grid=grid, in_specs=in_specs, out_specs=out_specs,
                         scratch_shapes=scratch, compiler_params=params)(*args)
    return list(res[:n_out]), list(res[n_out:])


def _mm(a, b, *, name, grid, a_spec, b_spec, o_spec, o_shape, o_dtype, dims, base=None):
    nk = grid[2]
    o_block = tuple(s for s in o_spec.block_shape if s is not None)

    def body(a_ref, b_ref, *rest):
        base_ref = rest[0] if base is not None else None
        o_ref, scr = rest[1 if base is not None else 0], rest[2 if base is not None else 1:]
        r = _dot(a_ref[...].astype(BF16), b_ref[...].astype(BF16), dims)
        if nk == 1:
            if base is not None:
                r = r + base_ref[...].astype(F32)
            o_ref[...] = r.astype(o_dtype)
        else:
            acc = scr[0]
            k = pl.program_id(2)

            @pl.when(k == 0)
            def _():
                acc[...] = r

            @pl.when(k > 0)
            def _():
                acc[...] += r

            @pl.when(k == nk - 1)
            def _():
                tot = acc[...] if base is None else acc[...] + base_ref[...].astype(F32)
                o_ref[...] = tot.astype(o_dtype)

    blocks = (_nbytes(a_spec.block_shape, a.dtype) + _nbytes(b_spec.block_shape, b.dtype) + _nbytes(o_block, o_dtype)
              + (_nbytes(o_block, base.dtype) if base is not None else 0))
    scratch = [pltpu.VMEM(o_block, F32)] if nk > 1 else []
    vmem = 2 * blocks + 3 * _nbytes(o_block, F32) + (8 << 20)
    return pl.pallas_call(
        body, name=name, out_shape=jax.ShapeDtypeStruct(o_shape, o_dtype), grid=grid,
        in_specs=[a_spec, b_spec] + ([o_spec] if base is not None else []), out_specs=o_spec, scratch_shapes=scratch,
        compiler_params=_params(("parallel", "parallel", "arbitrary"), vmem),
    )(a, b, *([base] if base is not None else []))


def _lin(a, w, *, name, trans_w=False, tm, tn, o_dtype=F32, kblk=0, base=None):
    m, kk = a.shape
    nout = w.shape[0] if trans_w else w.shape[1]
    if trans_w:
        b_spec = pl.BlockSpec((tn, kk), lambda j, i, k: (j, kblk))
    else:
        b_spec = pl.BlockSpec((kk, tn), lambda j, i, k: (0, j))
    return _mm(a, w, name=name, grid=(nout // tn, m // tm, 1), dims=NT if trans_w else NN, o_shape=(m, nout),
               o_dtype=o_dtype, o_spec=pl.BlockSpec((tm, tn), lambda j, i, k: (i, j)),
               a_spec=pl.BlockSpec((tm, kk), lambda j, i, k: (i, 0)), b_spec=b_spec, base=base)


def _lin_w(a, dy, *, name, ta, tn, tkr):
    m, ka = a.shape
    nout = dy.shape[1]
    return _mm(a, dy, name=name, grid=(ka // ta, nout // tn, m // tkr), dims=TN, o_shape=(ka, nout), o_dtype=BF16,
               o_spec=pl.BlockSpec((ta, tn), lambda i, j, k: (i, j)),
               a_spec=pl.BlockSpec((tkr, ta), lambda i, j, k: (k, i)),
               b_spec=pl.BlockSpec((tkr, tn), lambda i, j, k: (k, j)))


def _rowk(fn, *, name, n, tm, nctx, rows=(), pats=(), consts=(), out_rows=(), out_seg=(), out_acc=()):
    nb, ncb = n // tm, nctx // tm
    nr, npat, ncst = len(rows), len(pats), len(consts)
    no, nseg, nacc = len(out_rows), len(out_seg), len(out_acc)
    in_specs, blocks = [], 0
    for arr, w, cb, off in rows:
        in_specs.append(pl.BlockSpec((tm, w), lambda i, cb=cb, off=off: (jnp.maximum(i - off, 0), cb)))
        blocks += _nbytes((tm, w), arr.dtype)
    for p in pats:
        in_specs.append(pl.BlockSpec((None, SUBLANES, p.shape[2]), lambda i: (jnp.where(i >= ncb, 1, 0), 0, 0)))
    for cst in consts:
        in_specs.append(pl.BlockSpec(cst.shape, lambda i: (0, 0)))
    out_shape, out_specs = [], []
    for wt, dt, w, cb in out_rows:
        out_shape.append(jax.ShapeDtypeStruct((n, wt), dt))
        out_specs.append(pl.BlockSpec((tm, w), lambda i, cb=cb: (i, cb)))
        blocks += _nbytes((tm, w), dt)
    for w in out_seg:
        out_shape.append(jax.ShapeDtypeStruct((SUBLANES, w), F32))
        out_specs.append(pl.BlockSpec((SUBLANES, w), lambda i: (0, 0)))
    for w in out_acc:
        out_shape.append(jax.ShapeDtypeStruct((1, w), F32))
        out_specs.append(pl.BlockSpec((1, w), lambda i: (0, 0)))
    scratch = [pltpu.VMEM((2, SUBLANES, w), F32) for w in out_seg] + [pltpu.VMEM((SUBLANES, w), F32) for w in out_acc]

    def body(*refs):
        r_in = refs[:nr]
        p_in = refs[nr:nr + npat]
        c_in = refs[nr + npat:nr + npat + ncst]
        base = nr + npat + ncst
        o_rows = refs[base:base + no]
        o_seg = refs[base + no:base + no + nseg]
        o_acc = refs[base + no + nseg:base + no + nseg + nacc]
        s_seg = refs[base + no + nseg + nacc:base + no + nseg + nacc + nseg]
        s_acc = refs[base + no + nseg + nacc + nseg:]
        i = pl.program_id(0)
        rv = [r[...].astype(F32).reshape(tm // SUBLANES, SUBLANES, r.shape[1]) for r in r_in]
        pv = [p[...] for p in p_in]
        cv = [c[...] for c in c_in]
        is_lat = (i >= ncb).astype(F32)
        ro, so, ao = fn(rv, pv, cv, is_lat)
        for ref, val in zip(o_rows, ro):
            ref[...] = val.reshape(tm, ref.shape[1]).astype(ref.dtype)
        if nseg or nacc:
            @pl.when(i == 0)
            def _():
                for s in list(s_seg) + list(s_acc):
                    s[...] = jnp.zeros(s.shape, F32)

            seg = jnp.where(i >= ncb, 1, 0)
            for s, val in zip(s_seg, so):
                s[seg] = s[seg] + val
            for s, val in zip(s_acc, ao):
                s[...] = s[...] + val

            @pl.when(i == nb - 1)
            def _():
                for o, s in zip(o_seg, s_seg):
                    lat, ctx = s[1], s[0]
                    row = lax.broadcasted_iota(jnp.int32, lat.shape, 0)
                    lat = lat + pltpu.roll(lat, 4, 0)
                    ctx = jnp.broadcast_to(jnp.sum(ctx, axis=0, keepdims=True), lat.shape)
                    o[...] = jnp.where(row < 4, lat, jnp.where(row == 4, ctx, 0.0))
                for o, s in zip(o_acc, s_acc):
                    o[...] = jnp.sum(s[...], axis=0, keepdims=True)

    vmem = 2 * blocks + 8 * tm * 1024 * 4 + (8 << 20)
    res = pl.pallas_call(
        body, name=name, out_shape=out_shape, grid=(nb,), in_specs=in_specs, out_specs=out_specs,
        scratch_shapes=scratch, compiler_params=_params(("arbitrary",), vmem),
    )(*[r[0] for r in rows], *pats, *consts)
    return list(res)


def _rms(z):
    return lax.rsqrt(jnp.mean(z * z, axis=-1, keepdims=True) + NORM_EPS)


def _norm_mod_fwd(z, w, sh, sc, *, name, dims, res=None):
    n, d = z.shape

    def fn(rv, pv, cv, is_lat):
        zz = rv[0]
        if res is not None:
            zz = zz + pv[2][None] * rv[1]
        h = (zz * _rms(zz) * cv[0]) * (1.0 + pv[1][None]) + pv[0][None]
        return ([zz, h] if res is not None else [h]), [], []

    rows = [(z, d, 0, 0)] + ([(res[0], d, 0, 0)] if res is not None else [])
    pats = [sh, sc] + ([res[1]] if res is not None else [])
    outs = ([(d, F32, d, 0)] if res is not None else []) + [(d, BF16, d, 0)]
    out = _rowk(fn, name=name, n=n, tm=dims["tm_row"], nctx=dims["nctx"], rows=rows, pats=pats, consts=[w],
                out_rows=outs)
    return (out[0], out[1]) if res is not None else (None, out[0])


def _norm_core_bwd(zin, dh, w, sc):
    r = _rms(zin)
    xh = zin * r
    dsh = jnp.sum(dh, axis=0)
    dsc = jnp.sum(dh * (xh * w), axis=0)
    dyv = dh * (1.0 + sc[None])
    dw = jnp.sum(dyv * xh, axis=0)
    dxh = dyv * w
    dx = r * (dxh - xh * jnp.mean(dxh * xh, axis=-1, keepdims=True))
    return dx, dsh, dsc, dw


def _norm_mod_bwd(dh, zin, dz_up, w, sc, *, name, dims, res=None):
    n, d = zin.shape

    def fn(rv, pv, cv, is_lat):
        dx, dsh, dsc, dw = _norm_core_bwd(rv[1], rv[0], cv[0], pv[0])
        dz = rv[2] + dx
        if res is None:
            return [dz], [dsh, dsc], [dw]
        return [dz, dz * pv[1][None]], [dsh, dsc, jnp.sum(dz * rv[3], axis=0)], [dw]

    rows = [(dh, d, 0, 0), (zin, d, 0, 0), (dz_up, d, 0, 0)] + ([(res[0], d, 0, 0)] if res is not None else [])
    pats = [sc] + ([res[1]] if res is not None else [])
    outs = [(d, F32, d, 0)] + ([(d, BF16, d, 0)] if res is not None else [])
    return _rowk(fn, name=name, n=n, tm=dims["tm_row"], nctx=dims["nctx"], rows=rows, pats=pats, consts=[w],
                 out_rows=outs, out_seg=[d] * (3 if res is not None else 2), out_acc=[d])


def _loss_bwd(z1, f, gate, tgt, w, *, name, dims):
    n, d = z1.shape

    def fn(rv, pv, cv, is_lat):
        z2 = rv[0] + pv[0][None] * rv[1]
        r = _rms(z2)
        xh = z2 * r
        err = (xh * cv[0] - rv[2]) * is_lat
        dout = err * (1.0 / d)
        dxh = dout * cv[0]
        dz = r * (dxh - xh * jnp.mean(dxh * xh, axis=-1, keepdims=True))
        return ([dz, dz * pv[0][None]], [jnp.sum(dz * rv[1], axis=0)],
                [jnp.sum(0.5 * err * err * (1.0 / d), axis=0), jnp.sum(dout * xh, axis=0)])

    tm = dims["tm_row"]
    rows = [(z1, d, 0, 0), (f, d, 0, 0), (tgt, d, 0, dims["nctx"] // tm)]
    return _rowk(fn, name=name, n=n, tm=tm, nctx=dims["nctx"], rows=rows, pats=[gate], consts=[w],
                 out_rows=[(d, F32, d, 0), (d, BF16, d, 0)], out_seg=[d], out_acc=[d, d])


def _gelu(y):
    return jax.nn.gelu(y, approximate=True)


def _conv_masks(tb, i):
    tok = lax.broadcasted_iota(jnp.int32, (tb, 1), 0) >> 2
    last = jnp.where(i == 0, tb // LOCAL_B - 1, GRID_W - 1)
    wpos = tok & last
    return wpos == 0, wpos == last


CONV_LANES = 2 * LANES


def _conv_taps(u_ref, cw_ref, cb_ref, no_left, no_right, tb):
    uu = u_ref[...]
    ul = jnp.where(no_left, 0.0, pltpu.roll(uu, LOCAL_B, 0))
    ur = jnp.where(no_right, 0.0, pltpu.roll(uu, tb - LOCAL_B, 0))
    val = cb_ref[...] + ul * cw_ref[pl.ds(0, 1), :] + uu * cw_ref[pl.ds(1, 1), :] + ur * cw_ref[pl.ds(2, 1), :]
    return val, ul, uu, ur


def _convffn_specs(tb, nj):
    cl = CONV_LANES
    return [pl.BlockSpec((tb, cl), lambda j, i: (i, j)), pl.BlockSpec((tb, cl), lambda j, i: (i, nj + j)),
            pl.BlockSpec((3, cl), lambda j, i: (0, j)), pl.BlockSpec((3, cl), lambda j, i: (0, nj + j)),
            pl.BlockSpec((1, cl), lambda j, i: (0, j)), pl.BlockSpec((1, cl), lambda j, i: (0, nj + j))]


def _convffn_fwd(u, cw, cb, *, name, dims):
    n, f2 = u.shape
    tb, nj = dims["nctx"], f2 // 2 // CONV_LANES

    def body(ua_ref, ug_ref, cwa_ref, cwg_ref, cba_ref, cbg_ref, o_ref):
        no_left, no_right = _conv_masks(tb, pl.program_id(1))
        a = _conv_taps(ua_ref, cwa_ref, cba_ref, no_left, no_right, tb)[0]
        g = _conv_taps(ug_ref, cwg_ref, cbg_ref, no_left, no_right, tb)[0]
        o_ref[...] = (a * _sigmoid(a) * g).astype(BF16)

    vmem = 16 * tb * CONV_LANES * 4 + (8 << 20)
    return pl.pallas_call(
        body, name=name, out_shape=jax.ShapeDtypeStruct((n, f2 // 2), BF16), grid=(nj, n // tb),
        in_specs=_convffn_specs(tb, nj), out_specs=pl.BlockSpec((tb, CONV_LANES), lambda j, i: (i, j)),
        compiler_params=_params(("parallel", "arbitrary"), vmem),
    )(u, u, cw, cw, cb, cb)


def _convffn_bwd(u, dhm, cw, cb, *, name, dims, xch=None):
    n, f2 = u.shape
    tb, nj = dims["nctx"], f2 // 2 // CONV_LANES

    def body(ua_ref, ug_ref, cwa_ref, cwg_ref, cba_ref, cbg_ref, dh_ref, dua_ref, dug_ref, dcwa_ref, dcwg_ref, dcba_ref,
             dcbg_ref):
        i = pl.program_id(1)
        no_left, no_right = _conv_masks(tb, i)

        @pl.when(i == 0)
        def _():
            for ref in (dcwa_ref, dcwg_ref, dcba_ref, dcbg_ref):
                ref[...] = jnp.zeros(ref.shape, F32)

        a, al, ac, ar = _conv_taps(ua_ref, cwa_ref, cba_ref, no_left, no_right, tb)
        g, gl, gc, gr = _conv_taps(ug_ref, cwg_ref, cbg_ref, no_left, no_right, tb)
        dh = dh_ref[...].astype(F32)
        sa = _sigmoid(a)
        dg = dh * (a * sa)
        da = dh * g * (sa * (1.0 + a * (1.0 - sa)))
        for dc, (tl, tc, tr), cw_ref, du_ref, dcw_ref, dcb_ref in (
                (da, (al, ac, ar), cwa_ref, dua_ref, dcwa_ref, dcba_ref),
                (dg, (gl, gc, gr), cwg_ref, dug_ref, dcwg_ref, dcbg_ref)):
            dcb_ref[...] += jnp.sum(dc, axis=0, keepdims=True)
            dcw_ref[pl.ds(0, 1), :] += jnp.sum(dc * tl, axis=0, keepdims=True)
            dcw_ref[pl.ds(1, 1), :] += jnp.sum(dc * tc, axis=0, keepdims=True)
            dcw_ref[pl.ds(2, 1), :] += jnp.sum(dc * tr, axis=0, keepdims=True)
            du = (dc * cw_ref[pl.ds(1, 1), :]
                  + pltpu.roll(jnp.where(no_left, 0.0, dc) * cw_ref[pl.ds(0, 1), :], tb - LOCAL_B, 0)
                  + pltpu.roll(jnp.where(no_right, 0.0, dc) * cw_ref[pl.ds(2, 1), :], LOCAL_B, 0))
            du_ref[...] = du.astype(BF16)

    cl, f = CONV_LANES, f2 // 2
    sd = jax.ShapeDtypeStruct
    row = pl.BlockSpec((tb, cl), lambda j, i: (i, j))
    vmem = 24 * tb * cl * 4 + (8 << 20)
    return _call(
        body, name=name, xch=xch, args=[u, u, cw, cw, cb, cb, dhm], scratch=[],
        out_shape=[sd((n, f), BF16), sd((n, f), BF16), sd((3, f), F32), sd((3, f), F32), sd((1, f), F32), sd((1, f), F32)],
        grid=(nj, n // tb), in_specs=_convffn_specs(tb, nj) + [row],
        out_specs=[row, row, pl.BlockSpec((3, cl), lambda j, i: (0, j)), pl.BlockSpec((3, cl), lambda j, i: (0, j)),
                   pl.BlockSpec((1, cl), lambda j, i: (0, j)), pl.BlockSpec((1, cl), lambda j, i: (0, j))],
        params=_params(("arbitrary", "arbitrary"), vmem))


def _s5_disc(lr, li, ls, brt, bit):
    lr = jnp.minimum(lr, S5_LAM_RE_MAX)
    dt = jnp.exp(ls)
    mag = jnp.exp(lr * dt)
    ar = mag * jnp.cos(li * dt)
    ai = mag * jnp.sin(li * dt)
    den = lr * lr + li * li
    nr = ar - 1.0
    cr = (nr * lr + ai * li) / den
    ci = (ai * lr - nr * li) / den
    return ar, ai, cr * brt - ci * bit, cr * bit + ci * brt


def _s5_disc_fwd(lr, li, ls, brt, bit):
    def body(lr_ref, li_ref, ls_ref, br_ref, bi_ref, ar_ref, ai_ref, bbr_ref, bbi_ref):
        ar, ai, bbr, bbi = _s5_disc(lr_ref[...], li_ref[...], ls_ref[...], br_ref[...], bi_ref[...])
        ar_ref[...] = ar
        ai_ref[...] = ai
        bbr_ref[...] = bbr
        bbi_ref[...] = bbi

    sd = jax.ShapeDtypeStruct
    return pl.pallas_call(body, name="s5_disc_fwd",
                          out_shape=[sd(lr.shape, F32), sd(lr.shape, F32), sd(brt.shape, F32), sd(brt.shape, F32)],
                          compiler_params=_params(None, 32 << 20))(lr, li, ls, brt, bit)


def _s5_disc_bwd(lr, li, ls, brt, bit, dar, dai, dbbr, dbbi):
    def body(lr_ref, li_ref, ls_ref, br_ref, bi_ref, dar_ref, dai_ref, dbbr_ref, dbbi_ref,
             dlr_ref, dli_ref, dls_ref, dbr_ref, dbi_ref):
        _, vjp = jax.vjp(_s5_disc, lr_ref[...], li_ref[...], ls_ref[...], br_ref[...], bi_ref[...])
        dlr, dli, dls, dbr, dbi = vjp((dar_ref[...], dai_ref[...], dbbr_ref[...], dbbi_ref[...]))
        dlr_ref[...] = dlr
        dli_ref[...] = dli
        dls_ref[...] = dls
        dbr_ref[...] = dbr
        dbi_ref[...] = dbi

    sd = jax.ShapeDtypeStruct
    return pl.pallas_call(body, name="s5_disc_bwd",
                          out_shape=[sd(lr.shape, F32), sd(lr.shape, F32), sd(ls.shape, F32), sd(brt.shape, F32),
                                     sd(brt.shape, F32)],
                          compiler_params=_params(None, 48 << 20))(lr, li, ls, brt, bit, dar, dai, dbbr, dbbi)


def _cmul(ar, ai, xr, xi):
    return ar * xr - ai * xi, ar * xi + ai * xr


def _s5_chunk_of(step, ncc, nc, rev):
    if not rev:
        return step
    return jnp.where(step < ncc, ncc - 1 - step, nc - 1 - (step - ncc))


def _s5_scan2(asc, desc, row0, nrows, a_r_ref, a_i_ref, cr_ref, ci_ref, *, lane_block, extra=None):
    width = asc[0].shape[1]
    nt = nrows // SUBLANES
    for lb in range(width // lane_block):
        lanes = pl.ds(lb * lane_block, lane_block)
        a1r, a1i = a_r_ref[:, lanes], a_i_ref[:, lanes]
        a2r, a2i = pltpu.roll(a1r, 4, 0), pltpu.roll(a1i, 4, 0)
        lo = lax.broadcasted_iota(jnp.int32, a1r.shape, 0) < 4

        def step(t, carry):
            pr, pi = carry[0], carry[1]
            ra = pl.ds(pl.multiple_of(row0 + t * SUBLANES, SUBLANES), SUBLANES)
            rd = pl.ds(pl.multiple_of(row0 + (nt - 1 - t) * SUBLANES, SUBLANES), SUBLANES)
            ur, ui = asc[0][ra, lanes], asc[1][ra, lanes]
            dr, di = desc[0][rd, lanes], desc[1][rd, lanes]
            mr, mi = _cmul(a1r, a1i, pr, pi)
            y1r, y1i = jnp.where(lo, ur, dr) + mr, jnp.where(lo, ui, di) + mi
            mr, mi = _cmul(a2r, a2i, pltpu.roll(y1r, 4, 0), pltpu.roll(y1i, 4, 0))
            y2r, y2i = jnp.where(lo, dr, ur) + mr, jnp.where(lo, di, ui) + mi
            our, oui = jnp.where(lo, y1r, y2r), jnp.where(lo, y1i, y2i)
            odr, odi = jnp.where(lo, y2r, y1r), jnp.where(lo, y2i, y1i)
            asc[0][ra, lanes] = our
            asc[1][ra, lanes] = oui
            desc[0][rd, lanes] = odr
            desc[1][rd, lanes] = odi
            nxt = (pltpu.roll(y2r, 4, 0), pltpu.roll(y2i, 4, 0))
            if extra is None:
                return nxt
            return nxt + tuple(extra(t, nt - 1 - t, lanes, (our, oui), (odr, odi), carry[2:]))

        init = (cr_ref[:, lanes], ci_ref[:, lanes])
        if extra is not None:
            init = init + tuple(extra.init(lanes))
        out = lax.fori_loop(0, nt, step, init)
        cr_ref[:, lanes] = out[0]
        ci_ref[:, lanes] = out[1]
        if extra is not None:
            extra.done(lanes, out[2:])


S5_SPLIT = 2


def _s5_fwd(u, af_r, af_i, bb, cc, dsk, *, name, dims, xch=None):
    n, dm = u.shape
    nk, swk = bb[0].shape[0], bb[0].shape[2]
    rr, sw = dims["s5_rows"], nk * swk
    nkh, dmh, swh = nk // S5_SPLIT, dm // S5_SPLIT, sw // S5_SPLIT
    nc, ncc = n // rr, dims["nctx"] // rr
    c1 = lambda i: _s5_chunk_of(i, ncc, nc, True)

    def body(u0_ref, u1_ref, afr_ref, afi_ref, b0r, b0i, b1r, b1i, c0r, c0i, c1r, c1i, dsk_ref,
             y0_ref, y1_ref, str_ref, sti_ref, s0r, s0i, s1r, s1i, cr, ci):
        @pl.when(pl.program_id(1) == 0)
        def _():
            cr[...] = jnp.zeros(cr.shape, F32)
            ci[...] = jnp.zeros(ci.shape, F32)

        str_ref[...] = cr[...]
        sti_ref[...] = ci[...]
        ub0, ub1 = u0_ref[...].astype(BF16), u1_ref[...].astype(BF16)
        for k in range(nkh):
            cols, sl = slice(k * LANES, (k + 1) * LANES), slice(k * swk, (k + 1) * swk)
            s0r[:, sl] = _dot(ub0[:, cols], b0r[k])
            s0i[:, sl] = _dot(ub0[:, cols], b0i[k])
            s1r[:, sl] = _dot(ub1[:, cols], b1r[k])
            s1i[:, sl] = _dot(ub1[:, cols], b1i[k])
        _s5_scan2((s0r, s0i), (s1r, s1i), 0, rr, afr_ref, afi_ref, cr, ci, lane_block=dims["s5_lane_block"])
        for k in range(nkh):
            cols, sl = slice(k * LANES, (k + 1) * LANES), slice(k * swk, (k + 1) * swk)
            y0_ref[:, cols] = (_dot(s0r[:, sl].astype(BF16), c0r[k]) - _dot(s0i[:, sl].astype(BF16), c0i[k])
                               + dsk_ref[:, cols] * u0_ref[:, cols])
            y1_ref[:, cols] = _dot(s1r[:, sl].astype(BF16), c1r[k]) - _dot(s1i[:, sl].astype(BF16), c1i[k])

    row0 = pl.BlockSpec((rr, dmh), lambda h, i: (i, h))
    row1 = pl.BlockSpec((rr, dmh), lambda h, i: (c1(i), h))
    tile = pl.BlockSpec((SUBLANES, swh), lambda h, i: (0, h))
    wspec = lambda a: pl.BlockSpec((nkh,) + a.shape[1:], lambda h, i: (h, 0, 0))
    st_spec = pl.BlockSpec((None, SUBLANES, swh), lambda h, i: (i, 0, h))
    sd = jax.ShapeDtypeStruct
    vmem = 4 * rr * swh * 4 + 12 * rr * dmh * 4 + 16 * nkh * LANES * swk * 2 + (12 << 20)
    return _call(
        body, name=name, xch=xch, args=[u, u, af_r, af_i, *bb, *cc, dsk],
        out_shape=[sd((n, dm), F32), sd((n, dm), F32), sd((nc, SUBLANES, sw), F32), sd((nc, SUBLANES, sw), F32)],
        grid=(S5_SPLIT, nc),
        in_specs=[row0, row1, tile, tile] + [wspec(a) for a in (*bb, *cc)] + [pl.BlockSpec((1, dmh), lambda h, i: (0, h))],
        out_specs=[row0, row1, st_spec, st_spec],
        scratch=[pltpu.VMEM((rr, swh), F32)] * 4 + [pltpu.VMEM((SUBLANES, swh), F32)] * 2,
        params=_params(("arbitrary", "arbitrary"), vmem))


class _DaHook2:
    def __init__(self, s0, s1, accs):
        self.s0, self.s1, self.accs = s0, s1, accs

    def init(self, lanes):
        return tuple(a[:, lanes] for a in self.accs)

    def done(self, lanes, acc):
        for a, v in zip(self.accs, acc):
            a[:, lanes] = v

    def __call__(self, t1, t0, lanes, l1, l0, acc):
        row = lax.broadcasted_iota(jnp.int32, l1[0].shape, 0)
        b1 = pl.multiple_of(SUBLANES + t1 * SUBLANES, SUBLANES)
        b0 = pl.multiple_of(SUBLANES + t0 * SUBLANES, SUBLANES)
        cur1, nxt1 = pl.ds(b1, SUBLANES), pl.ds(pl.multiple_of(b1 + SUBLANES, SUBLANES), SUBLANES)
        cur0, prv0 = pl.ds(b0, SUBLANES), pl.ds(pl.multiple_of(b0 - SUBLANES, SUBLANES), SUBLANES)
        p1r = pltpu.roll(jnp.where(row >= 4, self.s1[0][cur1, lanes], self.s1[0][nxt1, lanes]), 4, 0)
        p1i = pltpu.roll(jnp.where(row >= 4, self.s1[1][cur1, lanes], self.s1[1][nxt1, lanes]), 4, 0)
        p0r = pltpu.roll(jnp.where(row >= 4, self.s0[0][prv0, lanes], self.s0[0][cur0, lanes]), 4, 0)
        p0i = pltpu.roll(jnp.where(row >= 4, self.s0[1][prv0, lanes], self.s0[1][cur0, lanes]), 4, 0)
        return (acc[0] + p0r * l0[0] + p0i * l0[1], acc[1] + p0r * l0[1] - p0i * l0[0],
                acc[2] + p1r * l1[0] + p1i * l1[1], acc[3] + p1r * l1[1] - p1i * l1[0])


def _s5_bwd(u, dy, af_r, af_i, ab_r, ab_i, bb, cc, dsk, st_r, st_i, *, name, dims, xch=None):
    n, dm = u.shape
    nk, swk = bb[0].shape[0], bb[0].shape[2]
    rr, sw = dims["s5_rows"], nk * swk
    nkh, dmh, swh = nk // S5_SPLIT, dm // S5_SPLIT, sw // S5_SPLIT
    nc, ncc = n // rr, dims["nctx"] // rr
    f0 = lambda i: nc - 1 - i
    f1 = lambda i: _s5_chunk_of(nc - 1 - i, ncc, nc, True)

    def body(u0_ref, u1_ref, dy0_ref, dy1_ref, afr_ref, afi_ref, abr_ref, abi_ref, b0r, b0i, b1r, b1i, c0r, c0i, c1r, c1i,
             dsk_ref, str_ref, sti_ref,
             du0_ref, du1_ref, db0r, db0i, db1r, db1i, dc0r, dc0i, dc1r, dc1i, da0r_ref, da0i_ref, da1r_ref, da1i_ref, dd_ref,
             s0r, s0i, s1r, s1i, l0r, l0i, l1r, l1i, cr, ci, lcr, lci, a0r, a0i, a1r, a1i, dda):
        i = pl.program_id(1)

        @pl.when(i == 0)
        def _():
            for ref in (lcr, lci, a0r, a0i, a1r, a1i, dda, db0r, db0i, db1r, db1i, dc0r, dc0i, dc1r, dc1i):
                ref[...] = jnp.zeros(ref.shape, F32)

        row = lax.broadcasted_iota(jnp.int32, (SUBLANES, swh), 0)
        for st_ref, car, z0, z1 in ((str_ref, cr, s0r, s1r), (sti_ref, ci, s0i, s1i)):
            st = st_ref[...]
            car[...] = st
            z0[pl.ds(0, SUBLANES), :] = jnp.where(row < 4, st, pltpu.roll(st, 4, 0))
            z1[pl.ds(rr + SUBLANES, SUBLANES), :] = jnp.where(row >= 4, st, pltpu.roll(st, 4, 0))
        body_rows = pl.ds(SUBLANES, rr)
        ub0, ub1 = u0_ref[...].astype(BF16), u1_ref[...].astype(BF16)
        dyb0, dyb1 = dy0_ref[...].astype(BF16), dy1_ref[...].astype(BF16)
        for k in range(nkh):
            cols, sl = slice(k * LANES, (k + 1) * LANES), slice(k * swk, (k + 1) * swk)
            s0r[body_rows, sl] = _dot(ub0[:, cols], b0r[k])
            s0i[body_rows, sl] = _dot(ub0[:, cols], b0i[k])
            s1r[body_rows, sl] = _dot(ub1[:, cols], b1r[k])
            s1i[body_rows, sl] = _dot(ub1[:, cols], b1i[k])
        _s5_scan2((s0r, s0i), (s1r, s1i), SUBLANES, rr, afr_ref, afi_ref, cr, ci, lane_block=dims["s5_lane_block"])
        for k in range(nkh):
            cols, sl = slice(k * LANES, (k + 1) * LANES), slice(k * swk, (k + 1) * swk)
            for dyk, lr, li, sr, si, ccr, cci, dcr, dci in ((dyb0[:, cols], l0r, l0i, s0r, s0i, c0r, c0i, dc0r, dc0i),
                                                           (dyb1[:, cols], l1r, l1i, s1r, s1i, c1r, c1i, dc1r, dc1i)):
                lr[:, sl] = _dot(dyk, ccr[k], NT)
                li[:, sl] = -_dot(dyk, cci[k], NT)
                dcr[k] += _dot(sr[body_rows, sl].astype(BF16), dyk, TN)
                dci[k] -= _dot(si[body_rows, sl].astype(BF16), dyk, TN)
        _s5_scan2((l1r, l1i), (l0r, l0i), 0, rr, abr_ref, abi_ref, lcr, lci, lane_block=dims["s5_lane_block"] // 2,
                  extra=_DaHook2((s0r, s0i), (s1r, s1i), (a0r, a0i, a1r, a1i)))
        for k in range(nkh):
            cols, sl = slice(k * LANES, (k + 1) * LANES), slice(k * swk, (k + 1) * swk)
            for uk, lr, li, br, bi, dbr, dbi, du_ref, first in ((ub0[:, cols], l0r, l0i, b0r, b0i, db0r, db0i, du0_ref, True),
                                                              (ub1[:, cols], l1r, l1i, b1r, b1i, db1r, db1i, du1_ref, False)):
                lrk, lik = lr[:, sl].astype(BF16), li[:, sl].astype(BF16)
                dbr[k] += _dot(uk, lrk, TN)
                dbi[k] += _dot(uk, lik, TN)
                duk = _dot(lrk, br[k], NT) + _dot(lik, bi[k], NT)
                if first:
                    duk = duk + dsk_ref[:, cols] * dy0_ref[:, cols]
                du_ref[:, cols] = duk
        dda[...] += jnp.sum((dy0_ref[...] * u0_ref[...]).reshape(rr // SUBLANES, SUBLANES, dmh), axis=0)

        @pl.when(i == nc - 1)
        def _():
            for o, a in ((da0r_ref, a0r), (da0i_ref, a0i), (da1r_ref, a1r), (da1i_ref, a1i), (dd_ref, dda)):
                o[...] = jnp.sum(a[...], axis=0, keepdims=True)

    row0 = pl.BlockSpec((rr, dmh), lambda h, i: (f0(i), h))
    row1 = pl.BlockSpec((rr, dmh), lambda h, i: (f1(i), h))
    tile = pl.BlockSpec((SUBLANES, swh), lambda h, i: (0, h))
    wspec = lambda a: pl.BlockSpec((nkh,) + a.shape[1:], lambda h, i: (h, 0, 0))
    st_spec = pl.BlockSpec((None, SUBLANES, swh), lambda h, i: (f0(i), 0, h))
    vec = lambda w: pl.BlockSpec((1, w), lambda h, i: (0, h))
    sd = jax.ShapeDtypeStruct
    out_shape = ([sd((n, dm), F32)] * 2 + [sd(a.shape, F32) for a in (*bb, *cc)] + [sd((1, sw), F32)] * 4 + [sd((1, dm), F32)])
    out_specs = [row0, row1] + [wspec(a) for a in (*bb, *cc)] + [vec(swh)] * 4 + [vec(dmh)]
    scratch = ([pltpu.VMEM((rr + 2 * SUBLANES, swh), F32)] * 4 + [pltpu.VMEM((rr, swh), F32)] * 4
               + [pltpu.VMEM((SUBLANES, swh), F32)] * 8 + [pltpu.VMEM((SUBLANES, dmh), F32)])
    vmem = 8 * (rr + 16) * swh * 4 + 16 * rr * dmh * 4 + 48 * nkh * LANES * swk * 4 + (10 << 20)
    return _call(
        body, name=name, xch=xch, args=[u, u, dy, dy, af_r, af_i, ab_r, ab_i, *bb, *cc, dsk, st_r, st_i],
        out_shape=out_shape, grid=(S5_SPLIT, nc),
        in_specs=[row0, row1, row0, row1, tile, tile, tile, tile] + [wspec(a) for a in (*bb, *cc)] + [vec(dmh), st_spec, st_spec],
        out_specs=out_specs, scratch=scratch, params=_params(("arbitrary", "arbitrary"), vmem))


def _hg_mask(kind, rev):
    if kind == "tot":
        r = lax.broadcasted_iota(jnp.int32, (SUBLANES, HG_ROWS), 0)
        c = lax.broadcasted_iota(jnp.int32, (SUBLANES, HG_ROWS), 1)
        return (c & 3) == r
    r = lax.broadcasted_iota(jnp.int32, (HG_ROWS, HG_ROWS), 0)
    c = lax.broadcasted_iota(jnp.int32, (HG_ROWS, HG_ROWS), 1)
    same = (r & 3) == (c & 3)
    before = ((c >> 2) >= (r >> 2)) if rev else ((c >> 2) <= (r >> 2))
    return jnp.logical_and(same, before if kind == "cum" else jnp.logical_not(before))


def _split2(x):
    hi = x.astype(BF16)
    return hi, (x - hi.astype(F32)).astype(BF16)


@functools.partial(jax.custom_vjp, nondiff_argnums=(1, 2))
def _mask_sum(x, kind, rev):
    m = _hg_mask(kind, rev).astype(BF16)
    hi, lo = _split2(x)
    return _dot(m, hi) + _dot(m, lo)


def _mask_sum_fwd(x, kind, rev):
    return _mask_sum(x, kind, rev), None


def _mask_sum_bwd(kind, rev, _, g):
    m = _hg_mask(kind, rev).astype(BF16)
    hi, lo = _split2(g)
    return (_dot(m, hi, TN) + _dot(m, lo, TN),)


_mask_sum.defvjp(_mask_sum_fwd, _mask_sum_bwd)


def _hg_chunk(q, v, fraw, l0, l1, st, *, rev):
    nh = q.shape[1] // HG_HEAD
    lb = _sigmoid(l1 - l0)
    logf = jnp.logaddexp(jnp.log(lb), jnp.log1p(-lb) + jax.nn.log_sigmoid(fraw))
    kk = (1.0 - lb) * _sigmoid(fraw * -1.0)
    tri = _hg_mask("cum", rev)
    bcum = _mask_sum(logf, "cum", rev)
    brem = _mask_sum(logf, "rem", rev)
    bend8 = _mask_sum(logf, "tot", rev)
    r8d = lax.broadcasted_iota(jnp.int32, bend8.shape, 0)
    decs = [jnp.exp(jnp.sum(jnp.where(r8d == b, bend8, 0.0), axis=0, keepdims=True)) for b in range(LOCAL_B)]
    qd = (q * jnp.exp(bcum)).astype(BF16)
    kd = (kk * jnp.exp(-bcum)).astype(BF16)
    ke = (kk * jnp.exp(brem)).astype(BF16)
    wide = (HG_ROWS, LOCAL_B * HG_HEAD)
    mine = (lax.broadcasted_iota(jnp.int32, wide, 1) >> 7) == (lax.broadcasted_iota(jnp.int32, wide, 0) & 3)
    per_example = lambda x: jnp.where(mine, jnp.concatenate([x] * LOCAL_B, axis=1), jnp.zeros(wide, x.dtype))
    outs, new = [], []
    for h in range(nh):
        sl = slice(h * HG_HEAD, (h + 1) * HG_HEAD)
        vh = v[:, sl].astype(BF16)
        att = jnp.where(tri, _dot(qd[:, sl], kd[:, sl], NT), 0.0)
        outs.append(_dot(att.astype(BF16), vh) + _dot(per_example(qd[:, sl]), st[h].astype(BF16), NT))
        dec = jnp.concatenate([d[:, sl] for d in decs], axis=1)
        new.append(st[h] * dec + _dot(vh, per_example(ke[:, sl]), TN))
    return jnp.concatenate(outs, axis=1), tuple(new)


def _hg_chunk_of(step, ncc, nc, rev):
    return _s5_chunk_of(step, ncc, nc, rev)


def _hg_fwd_dir(zz, lb2, *, d, name, dims, xch=None):
    n = zz.shape[0]
    dm = zz.shape[1] // 5
    ns, sw = dm // HG_HEAD, LOCAL_B * HG_HEAD
    nc, ncc = n // HG_ROWS, dims["nctx"] // HG_ROWS
    rev = d == 1
    ch = lambda i: _hg_chunk_of(i, ncc, nc, rev)

    def body(q_ref, v_ref, f_ref, l0_ref, l1_ref, o_ref, st_ref, st):
        @pl.when(pl.program_id(0) == 0)
        def _():
            st[...] = jnp.zeros(st.shape, F32)

        st_ref[...] = st[...]
        o, new = _hg_chunk(q_ref[...], v_ref[...], f_ref[...], l0_ref[...], l1_ref[...],
                           tuple(st[j] for j in range(ns)), rev=rev)
        o_ref[...] = o
        for j in range(ns):
            st[j] = new[j]

    blk = lambda off: pl.BlockSpec((HG_ROWS, dm), lambda i, off=off: (ch(i), off))
    lspec = lambda layer: pl.BlockSpec((None, None, 1, dm), lambda i, layer=layer: (d, layer, 0, 0))
    return _call(
        body, name=name, xch=xch, args=[zz, zz, zz, lb2, lb2],
        out_shape=[jax.ShapeDtypeStruct((n, dm), F32), jax.ShapeDtypeStruct((nc, ns, HG_HEAD, sw), F32)],
        grid=(nc,),
        in_specs=[blk(0), blk(1), blk(2 + d), lspec(0), lspec(1)],
        out_specs=[pl.BlockSpec((HG_ROWS, dm), lambda i: (ch(i), 0)),
                   pl.BlockSpec((None, ns, HG_HEAD, sw), lambda i: (ch(i), 0, 0, 0))],
        scratch=[pltpu.VMEM((ns, HG_HEAD, sw), F32)],
        params=_params(("arbitrary",), 48 << 20))


def _hg_bwd_dir(zz, lb2, do, sts, dqv_prev, *, d, name, dims, xch=None):
    n = zz.shape[0]
    dm = zz.shape[1] // 5
    ns, sw = dm // HG_HEAD, LOCAL_B * HG_HEAD
    nc, ncc = n // HG_ROWS, dims["nctx"] // HG_ROWS
    rev = d == 1
    ch = lambda i: _hg_chunk_of(nc - 1 - i, ncc, nc, rev)
    qv_dtype = F32 if d == 0 else BF16

    def body(*refs):
        q_ref, v_ref, f_ref, l0_ref, l1_ref, do_ref, st_ref = refs[:7]
        pos = 7
        if d == 1:
            dqp_ref, dvp_ref = refs[7:9]
            pos = 9
        dq_ref, dv_ref, df_ref, dl_ref, dst = refs[pos:]
        i = pl.program_id(0)

        @pl.when(i == 0)
        def _():
            dst[...] = jnp.zeros(dst.shape, F32)
            dl_ref[...] = jnp.zeros(dl_ref.shape, F32)

        _, vjp = jax.vjp(functools.partial(_hg_chunk, rev=rev), q_ref[...], v_ref[...], f_ref[...], l0_ref[...],
                         l1_ref[...], tuple(st_ref[j] for j in range(ns)))
        dq, dv, df, dl0, dl1, dstn = vjp((do_ref[...], tuple(dst[j] for j in range(ns))))
        for j in range(ns):
            dst[j] = dstn[j]
        if d == 1:
            dq = dq + dqp_ref[...]
            dv = dv + dvp_ref[...]
        dq_ref[...] = dq.astype(qv_dtype)
        dv_ref[...] = dv.astype(qv_dtype)
        df_ref[...] = df.astype(BF16)
        dl_ref[0] += dl0
        dl_ref[1] += dl1

    blk = lambda off: pl.BlockSpec((HG_ROWS, dm), lambda i, off=off: (ch(i), off))
    oblk = pl.BlockSpec((HG_ROWS, dm), lambda i: (ch(i), 0))
    lspec = lambda layer: pl.BlockSpec((None, None, 1, dm), lambda i, layer=layer: (d, layer, 0, 0))
    ins = [zz, zz, zz, lb2, lb2, do, sts] + (list(dqv_prev) if d == 1 else [])
    in_specs = [blk(0), blk(1), blk(2 + d), lspec(0), lspec(1), oblk,
                pl.BlockSpec((None, ns, HG_HEAD, sw), lambda i: (ch(i), 0, 0, 0))]
    in_specs += [oblk, oblk] if d == 1 else []
    sd = jax.ShapeDtypeStruct
    return _call(
        body, name=name, xch=xch, args=ins,
        out_shape=[sd((n, dm), qv_dtype), sd((n, dm), qv_dtype), sd((n, dm), BF16), sd((2, 1, dm), F32)],
        grid=(nc,), in_specs=in_specs,
        out_specs=[oblk, oblk, oblk, pl.BlockSpec((2, 1, dm), lambda i: (0, 0, 0))],
        scratch=[pltpu.VMEM((ns, HG_HEAD, sw), F32)],
        params=_params(("arbitrary",), 56 << 20))


def _hg_readout(o, g, w):
    outs = []
    for h in range(o.shape[-1] // HG_HEAD):
        sl = slice(h * HG_HEAD, (h + 1) * HG_HEAD)
        oh = o[..., sl]
        outs.append(oh * _rms(oh) * w * _sigmoid(g[..., sl]))
    return jnp.concatenate(outs, axis=-1)


def _silu(x):
    return x * _sigmoid(x)


def _mod_fwd(craw, w, b):
    def body(c_ref, w_ref, b_ref, o_ref):
        s = _silu(c_ref[...]).astype(BF16)
        for layer in range(w.shape[0]):
            o_ref[layer] = _dot(s, w_ref[layer].astype(BF16)) + b_ref[layer]

    return pl.pallas_call(body, name="mod_fwd",
                          out_shape=jax.ShapeDtypeStruct((w.shape[0], craw.shape[0], w.shape[2]), F32),
                          compiler_params=_params(None, 40 << 20))(craw, w, b)


def _mod_bwd(craw, w, dlat_sh, dctx_sh, dlat_full, dctx_full):
    nl, dm, ns = w.shape
    nb = dlat_sh.shape[1]

    def body(c_ref, w_ref, dl_ref, dc_ref, dlf_ref, dcf_ref, dw_ref, db_ref, dcc_ref):
        craw_v = c_ref[...]
        s = _silu(craw_v)
        s_lat = s[:nb].astype(BF16)
        s_ctx = s[nb:].astype(BF16)
        row = lax.broadcasted_iota(jnp.int32, (SUBLANES, ns), 0)
        dsc = jnp.zeros((SUBLANES, dm), F32)
        for layer in range(nl):
            tot = dc_ref[0, pl.ds(layer, 1), :]
            totf = dcf_ref[0, pl.ds(layer, 1), :]
            for i in range(1, NDEV):
                tot = tot + dc_ref[i, pl.ds(layer, 1), :]
                totf = totf + dcf_ref[i, pl.ds(layer, 1), :]
            dc8 = jnp.where(row == 0, jnp.broadcast_to(tot, (SUBLANES, ns)), 0.0).astype(BF16)
            dw_ref[layer] = _dot(s_lat, dl_ref[layer].astype(BF16), TN) + _dot(s_ctx, dc8, TN)
            db_ref[layer] = jnp.sum(dlf_ref[layer], axis=0, keepdims=True) + totf
            dsc = dsc + _dot(dc8, w_ref[layer].astype(BF16), NT)
        cc = craw_v[nb:]
        sg = _sigmoid(cc)
        dcc_ref[...] = dsc * (sg * (1.0 + cc * (1.0 - sg)))

    sd = jax.ShapeDtypeStruct
    return pl.pallas_call(body, name="mod_bwd",
                          out_shape=[sd((nl, dm, ns), F32), sd((nl, 1, dlat_full.shape[2]), F32), sd((SUBLANES, dm), F32)],
                          compiler_params=_params(None, 48 << 20))(craw, w, dlat_sh, dctx_sh, dlat_full, dctx_full)


def _adam_rows(r):
    best = None
    for t in range(2 * SUBLANES, min(r, 128) + 1, 2 * SUBLANES):
        if r % t == 0:
            best = t
    return best if best is not None else r


def _adamw(parts, w, m, v, *, name):
    npart, r, c = parts.shape
    tr = _adam_rows(r)

    def body(p_ref, w_ref, m_ref, v_ref, g_ref, d_ref, nm_ref, nv_ref):
        g = p_ref[0].astype(F32)
        for i in range(1, npart):
            g = g + p_ref[i].astype(F32)
        nm = ADAM_B1 * m_ref[...] + (1.0 - ADAM_B1) * g
        nv = ADAM_B2 * v_ref[...] + (1.0 - ADAM_B2) * (g * g)
        m_hat = nm / (1.0 - ADAM_B1 ** ADAM_STEP)
        v_hat = nv / (1.0 - ADAM_B2 ** ADAM_STEP)
        g_ref[...] = g
        d_ref[...] = -ADAM_LR * (m_hat / (jnp.sqrt(v_hat) + ADAM_EPS) + ADAM_WD * w_ref[...])
        nm_ref[...] = nm
        nv_ref[...] = nv

    spec = pl.BlockSpec((tr, c), lambda i: (i, 0))
    vmem = 2 * (npart + 7) * tr * c * 4 + (8 << 20)
    return pl.pallas_call(
        body, name=name, out_shape=[jax.ShapeDtypeStruct((r, c), F32)] * 4, grid=(r // tr,),
        in_specs=[pl.BlockSpec((npart, tr, c), lambda i: (0, i, 0)), spec, spec, spec], out_specs=[spec] * 4,
        compiler_params=_params(("parallel",), vmem),
    )(parts, w, m, v)


def _to_tm(a):
    return jnp.transpose(a, (1, 0, 2)).reshape(a.shape[1] * a.shape[0], a.shape[2])


def _pattern(mod_lat, mod_ctx, m, dm):
    lat = mod_lat[:, m * dm:(m + 1) * dm]
    ctx = jnp.broadcast_to(mod_ctx[None, m * dm:(m + 1) * dm], (SUBLANES, dm))
    return jnp.stack([ctx, jnp.concatenate([lat, lat], axis=0)])


def _blockdiag_b(bt, nk):
    g, h, p = bt.shape
    t = bt.reshape(nk, 8, h, p)
    return jnp.einsum("kghp,gj->kghjp", t, jnp.eye(8, dtype=bt.dtype)).reshape(nk, 8 * h, 8 * p)


def _blockdiag_c(ct, nk):
    g, h, p = ct.shape
    t = ct.reshape(nk, 8, h, p)
    return jnp.einsum("kghp,gj->kgpjh", t, jnp.eye(8, dtype=ct.dtype)).reshape(nk, 8 * p, 8 * h)


def _diag_b(dbb, h, p):
    nk = dbb.shape[0]
    return jnp.einsum("kghgp->kghp", dbb.reshape(nk, 8, h, 8, p)).reshape(nk * 8, h, p)


def _diag_c(dcc, h, p):
    nk = dcc.shape[0]
    return jnp.einsum("kgpgh->kghp", dcc.reshape(nk, 8, p, 8, h)).reshape(nk * 8, h, p)


def _s5_prep(lam_re, lam_im, log_step, b_re, b_im, c_re, c_im, dm):
    ngrp, nk = dm // S5_GROUP, dm // LANES
    sw = ngrp * S5_STATE
    lr4 = lam_re.reshape(2, ngrp, 1, S5_STATE)
    li4 = lam_im.reshape(2, ngrp, 1, S5_STATE)
    ls4 = log_step.reshape(2, ngrp, 1, 1)
    brt = jnp.transpose(b_re, (0, 1, 3, 2))
    bit = jnp.transpose(b_im, (0, 1, 3, 2))
    abar_r, abar_i, bbar_r, bbar_i = _s5_disc_fwd(lr4, li4, ls4, brt, bit)
    half = lambda a, d: jnp.broadcast_to(a[d].reshape(1, sw), (LOCAL_B, sw))
    tile = lambda a, first: jnp.concatenate([half(a, first), half(a, 1 - first)], axis=0)
    bb = tuple(_blockdiag_b(w[d], nk).astype(BF16) for d in range(2) for w in (bbar_r, bbar_i))
    cc = tuple(_blockdiag_c(w[d], nk).astype(BF16) for d in range(2) for w in (c_re, c_im))
    return dict(disc_in=(lr4, li4, ls4, brt, bit), af=(tile(abar_r, 0), tile(abar_i, 0)),
                ab=(tile(abar_r, 1), -tile(abar_i, 1)), bb=bb, cc=cc)


def _s5_param_grads(prep, dbb, dcc, da):
    lr4 = prep["disc_in"][0]
    dar = jnp.stack([da[0], da[2]]).reshape(lr4.shape)
    dai = jnp.stack([da[1], da[3]]).reshape(lr4.shape)
    dbbr = jnp.stack([_diag_b(dbb[0], S5_GROUP, S5_STATE), _diag_b(dbb[2], S5_GROUP, S5_STATE)])
    dbbi = jnp.stack([_diag_b(dbb[1], S5_GROUP, S5_STATE), _diag_b(dbb[3], S5_GROUP, S5_STATE)])
    dlr, dli, dls, dbrt, dbit = _s5_disc_bwd(*prep["disc_in"], dar, dai, dbbr, dbbi)
    g_c_re = jnp.stack([_diag_c(dcc[0], S5_GROUP, S5_STATE), _diag_c(dcc[2], S5_GROUP, S5_STATE)])
    g_c_im = jnp.stack([_diag_c(dcc[1], S5_GROUP, S5_STATE), _diag_c(dcc[3], S5_GROUP, S5_STATE)])
    return dlr, dli, dls, jnp.transpose(dbrt, (0, 1, 3, 2)), jnp.transpose(dbit, (0, 1, 3, 2)), g_c_re, g_c_im


def kernel(x, c, ctx, c_ctx, w_mod, b_mod, norm1_w, norm2_w, final_norm_w, s5_w_in, s5_lam_re, s5_lam_im, s5_log_step, s5_b_re, s5_b_im, s5_c_re, s5_c_im, s5_d, s5_w_glu, s5_w_out, hg_w_in, hg_lower_bounds, hg_gnorm_w, hg_w_out, ffn_w_up, ffn_conv_w, ffn_conv_b, ffn_w_down, loss_target, m_c_ctx, m_w_mod, m_b_mod, m_norm1_w, m_norm2_w, m_final_norm_w, m_s5_w_in, m_s5_lam_re, m_s5_lam_im, m_s5_log_step, m_s5_b_re, m_s5_b_im, m_s5_c_re, m_s5_c_im, m_s5_d, m_s5_w_glu, m_s5_w_out, m_hg_w_in, m_hg_lower_bounds, m_hg_gnorm_w, m_hg_w_out, m_ffn_w_up, m_ffn_conv_w, m_ffn_conv_b, m_ffn_w_down, v_c_ctx, v_w_mod, v_b_mod, v_norm1_w, v_norm2_w, v_final_norm_w, v_s5_w_in, v_s5_lam_re, v_s5_lam_im, v_s5_log_step, v_s5_b_re, v_s5_b_im, v_s5_c_re, v_s5_c_im, v_s5_d, v_s5_w_glu, v_s5_w_out, v_hg_w_in, v_hg_lower_bounds, v_hg_gnorm_w, v_hg_w_out, v_ffn_w_up, v_ffn_conv_w, v_ffn_conv_b, v_ffn_w_down):
    given = dict(locals())
    bsz, lx, dm = x.shape
    lc = ctx.shape[1]
    assert bsz == LOCAL_B and w_mod.shape[0] == 2 and dm % LANES == 0
    n, nctx = (lc + lx) * bsz, lc * bsz
    ngrp, nstate, hgrp = dm // S5_GROUP, S5_STATE, S5_GROUP
    nk = dm // LANES
    dims = dict(nctx=nctx, tm=min(512, nctx), tm_row=min(256, nctx), s5_rows=min(256, nctx),
                s5_lane_block=min(512, 8 * nstate))
    tm = dims["tm"]
    assert nctx % HG_ROWS == 0 and (lx * bsz) % nctx == 0 and lc % GRID_W == 0 and lc & (lc - 1) == 0
    me = 4 * lax.axis_index("x") + 2 * lax.axis_index("y") + lax.axis_index("c")

    gath = _exchange([given[k].astype(BF16) for k in ("s5_w_in", "s5_w_glu", "s5_w_out")]
                     + [c, hg_lower_bounds, ffn_conv_w], a2a=False, name="gather_weights")
    w_s5in, w_glu, w_s5out = (g.reshape(dm, dm) for g in gath[:3])
    c_all, lb_all, cw_all = gath[3:]
    ns_up = ffn_w_up.shape[2]
    w_up, w_dn = [None, None], [None, None]
    cols = lambda g: jnp.transpose(g, (1, 0, 2)).reshape(g.shape[1], -1)
    shards = lambda w: jnp.transpose(w.reshape(w.shape[0], NDEV, -1), (1, 0, 2))
    tn_up, tn_hg, tkr = 2 * ns_up, 2 * hg_w_in.shape[2], 1152
    assert n % tkr == 0 and n % 1024 == 0
    gather = lambda arrs: _Xchg([a.astype(BF16) for a in arrs], [False] * len(arrs))
    scatter = lambda arrs: _Xchg(arrs, [True] * len(arrs))
    cw = [cols(cw_all[:, layer]) for layer in range(2)]
    cb = [ffn_conv_b[layer].reshape(1, -1) for layer in range(2)]
    lb2 = jnp.transpose(lb_all, (1, 2, 0, 3)).reshape(2, 2, 1, dm)

    nsm = w_mod.shape[2]
    craw = jnp.concatenate([c_all.reshape(NDEV * bsz, dm), c_ctx[None], jnp.zeros((SUBLANES - 1, dm), F32)], axis=0)
    b_sh = lax.dynamic_slice(b_mod, (0, me * nsm), (2, nsm)).reshape(2, 1, nsm)
    mod_sh = _mod_fwd(craw, w_mod, b_sh)
    (mod_g,) = _exchange([mod_sh], a2a=False, name="gather_mod")
    mod_full = jnp.transpose(mod_g, (1, 2, 0, 3)).reshape(2, craw.shape[0], NDEV * nsm)
    pat = []
    for layer in range(2):
        mlat = lax.dynamic_slice(mod_full[layer], (me * bsz, 0), (bsz, N_MOD * dm))
        mctx = mod_full[layer, NDEV * bsz]
        pat.append([_pattern(mlat, mctx, m, dm) for m in range(N_MOD)])

    s5p = _s5_prep(s5_lam_re[0], s5_lam_im[0], s5_log_step[0], s5_b_re[0], s5_b_im[0], s5_c_re[0], s5_c_im[0], dm)
    dsk = s5_d.reshape(1, dm)

    z0 = jnp.concatenate([_to_tm(ctx), _to_tm(x)], axis=0)
    tgt = _to_tm(loss_target)
    n1w = [norm1_w[layer].reshape(1, dm) for layer in range(2)]
    n2w = [norm2_w[layer].reshape(1, dm) for layer in range(2)]

    def ffn_fwd(layer, h2):
        u = _lin(h2, w_up[layer], name=f"ffn_up{layer}", tm=1024, tn=tn_up)
        hm = _convffn_fwd(u, cw[layer], cb[layer], name=f"convffn_fwd{layer}", dims=dims)
        f = _lin(hm, w_dn[layer], name=f"ffn_down{layer}", tm=tm, tn=dm)
        return u, hm, f

    _, h0 = _norm_mod_fwd(z0, n1w[0], pat[0][0], pat[0][1], name="norm1_l0", dims=dims)
    u_s5 = _lin(h0, w_s5in, name="s5_in", tm=1024, tn=dm)
    (y_s5a, y_s5b, st_r, st_i), (g_up0, g_dn0, g_hgin, g_hgout, g_dn1) = _s5_fwd(
        u_s5, *s5p["af"], s5p["bb"], s5p["cc"], dsk, name="s5_fwd", dims=dims,
        xch=gather([ffn_w_up[0], ffn_w_down[0], hg_w_in[0], hg_w_out[0], ffn_w_down[1]]))
    w_up[0], w_dn[0] = cols(g_up0), g_dn0.reshape(-1, dm)
    w_hgin, w_hgout, w_dn[1] = cols(g_hgin), g_hgout.reshape(dm, dm), g_dn1.reshape(-1, dm)
    (zg,) = _rowk(lambda rv, pv, cv, il: ([_gelu(rv[0] + rv[1])], [], []), name="s5_gelu", n=n, tm=dims["tm_row"],
                  nctx=nctx, rows=[(y_s5a, dm, 0, 0), (y_s5b, dm, 0, 0)], out_rows=[(dm, BF16, dm, 0)])
    t_glu = _lin(zg, w_glu, name="s5_glu", tm=1024, tn=dm)
    (z2g,) = _rowk(lambda rv, pv, cv, il: ([rv[0] * _sigmoid(rv[1])], [], []), name="s5_gate", n=n,
                   tm=dims["tm_row"], nctx=nctx, rows=[(zg, dm, 0, 0), (t_glu, dm, 0, 0)],
                   out_rows=[(dm, BF16, dm, 0)])
    ymix0 = _lin(z2g, w_s5out, name="s5_out", tm=1024, tn=dm)
    z1_l0, h2_l0 = _norm_mod_fwd(z0, n2w[0], pat[0][3], pat[0][4], name="norm2_l0", dims=dims,
                                 res=(ymix0, pat[0][2]))
    u_l0, hm_l0, f_l0 = ffn_fwd(0, h2_l0)

    z2_l0, h1 = _norm_mod_fwd(z1_l0, n1w[1], pat[1][0], pat[1][1], name="norm1_l1", dims=dims,
                              res=(f_l0, pat[0][5]))
    zz = _lin(h1, w_hgin, name="hg_in", tm=1024, tn=tn_hg)
    (o_f, sts_f), (g_up1,) = _hg_fwd_dir(zz, lb2, d=0, name="hg_fwd_d0", dims=dims, xch=gather([ffn_w_up[1]]))
    w_up[1] = cols(g_up1)
    (o_b, sts_b), _ = _hg_fwd_dir(zz, lb2, d=1, name="hg_fwd_d1", dims=dims)
    gnw = hg_gnorm_w.reshape(1, HG_HEAD)
    (og,) = _rowk(lambda rv, pv, cv, il: ([_hg_readout(rv[0] + rv[1], rv[2], cv[0])], [], []), name="hg_readout",
                  n=n, tm=dims["tm_row"], nctx=nctx, rows=[(o_f, dm, 0, 0), (o_b, dm, 0, 0), (zz, dm, 4, 0)],
                  consts=[gnw], out_rows=[(dm, BF16, dm, 0)])
    ymix1 = _lin(og, w_hgout, name="hg_out", tm=1024, tn=dm)
    z1_l1, h2_l1 = _norm_mod_fwd(z2_l0, n2w[1], pat[1][3], pat[1][4], name="norm2_l1", dims=dims,
                                 res=(ymix1, pat[1][2]))
    u_l1, hm_l1, f_l1 = ffn_fwd(1, h2_l1)

    dz, df, dgate2_l1, loss_part, dfinal_w = _loss_bwd(z1_l1, f_l1, pat[1][5], tgt, final_norm_w.reshape(1, dm),
                                                        name="loss_bwd", dims=dims)

    def ffn_bwd(layer, df_, u, hm, h2, xch=None):
        dff = hm.shape[1]
        dhm = _lin(df_, w_dn[layer], name=f"ffn_down_bwd_in{layer}", trans_w=True, tm=1024, tn=dff // 2, o_dtype=BF16)
        dwd = _lin_w(hm, df_, name=f"ffn_down_bwd_w{layer}", ta=dff // 2, tn=dm, tkr=tkr)
        (dua, dug, dcwa, dcwg, dcba, dcbg), got = _convffn_bwd(u, dhm, cw[layer], cb[layer],
                                                               name=f"convffn_bwd{layer}", dims=dims, xch=xch)
        dh2 = _lin(dua, w_up[layer], name=f"ffn_up_bwd_in_a{layer}", trans_w=True, tm=tm, tn=dm, kblk=0)
        dh2 = _lin(dug, w_up[layer], name=f"ffn_up_bwd_in_g{layer}", trans_w=True, tm=tm, tn=dm, kblk=1, base=dh2)
        dwu = jnp.concatenate([_lin_w(h2, dua, name=f"ffn_up_bwd_w_a{layer}", ta=dm, tn=tn_up, tkr=tkr),
                               _lin_w(h2, dug, name=f"ffn_up_bwd_w_g{layer}", ta=dm, tn=tn_up, tkr=tkr)], axis=1)
        dcw = shards(jnp.concatenate([dcwa, dcwg], axis=1))
        return dh2, shards(dwu), dwd, dcw, jnp.concatenate([dcba, dcbg], axis=1), got

    dh2, dwu_l1, dwd_l1, dcw_l1, dcb_l1, _ = ffn_bwd(1, df, u_l1, hm_l1, h2_l1)
    dz, dymix, dsh2_l1, dsc2_l1, dgate1_l1, dn2w_l1 = _norm_mod_bwd(dh2, z1_l1, dz, n2w[1], pat[1][4], name="norm2_bwd_l1",
                                                                    dims=dims, res=(ymix1, pat[1][2]))
    dog = _lin(dymix, w_hgout, name="hg_out_bwd_in", trans_w=True, tm=1024, tn=dm)
    dw_hgout = _lin_w(og, dymix, name="hg_out_bwd_w", ta=dm, tn=dm, tkr=tkr)

    def readout_bwd(rv, pv, cv, il):
        _, vjp = jax.vjp(_hg_readout, rv[0] + rv[1], rv[2], cv[0])
        do, dg, dw = vjp(rv[3])
        return [do, dg], [], [jnp.broadcast_to(dw, (SUBLANES, HG_HEAD)) * (1.0 / SUBLANES)]

    do, dg, dgnw = _rowk(readout_bwd, name="hg_readout_bwd", n=n, tm=dims["tm_row"], nctx=nctx,
                         rows=[(o_f, dm, 0, 0), (o_b, dm, 0, 0), (zz, dm, 4, 0), (dog, dm, 0, 0)], consts=[gnw],
                         out_rows=[(dm, F32, dm, 0), (dm, BF16, dm, 0)], out_acc=[HG_HEAD])
    (dq0, dv0, dff, dl_f), (p_up1, p_dn1) = _hg_bwd_dir(
        zz, lb2, do, sts_f, None, d=0, name="hg_bwd_d0", dims=dims,
        xch=scatter([dwu_l1, dwd_l1.reshape(NDEV, -1, dm)]))
    (dq, dv, dfb, dl_b), (p_hgout,) = _hg_bwd_dir(
        zz, lb2, do, sts_b, (dq0, dv0), d=1, name="hg_bwd_d1", dims=dims,
        xch=scatter([dw_hgout.reshape(NDEV, -1, dm)]))
    dzz = jnp.concatenate([dq, dv, dff, dfb, dg], axis=1)
    dh1 = _lin(dzz, w_hgin, name="hg_in_bwd_in", trans_w=True, tm=tm, tn=dm)
    dw_hgin = shards(_lin_w(h1, dzz, name="hg_in_bwd_w", ta=dm, tn=tn_hg, tkr=tkr))
    dz, df0, dsh1_l1, dsc1_l1, dgate2_l0, dn1w_l1 = _norm_mod_bwd(dh1, z2_l0, dz, n1w[1], pat[1][1], name="norm1_bwd_l1",
                                                                  dims=dims, res=(f_l0, pat[0][5]))
    dh2, dwu_l0, dwd_l0, dcw_l0, dcb_l0, _ = ffn_bwd(0, df0, u_l0, hm_l0, h2_l0)
    dz, dymix, dsh2_l0, dsc2_l0, dgate1_l0, dn2w_l0 = _norm_mod_bwd(dh2, z1_l0, dz, n2w[0], pat[0][4], name="norm2_bwd_l0",
                                                                    dims=dims, res=(ymix0, pat[0][2]))
    dz2g = _lin(dymix, w_s5out, name="s5_out_bwd_in", trans_w=True, tm=1024, tn=dm)
    dw_s5out = _lin_w(z2g, dymix, name="s5_out_bwd_w", ta=dm, tn=dm, tkr=tkr)

    def gate_bwd(rv, pv, cv, il):
        sg = _sigmoid(rv[1])
        return [rv[2] * rv[0] * sg * (1.0 - sg), rv[2] * sg], [], []

    dt_glu, dzg_a = _rowk(gate_bwd, name="s5_gate_bwd", n=n, tm=dims["tm_row"], nctx=nctx,
                          rows=[(zg, dm, 0, 0), (t_glu, dm, 0, 0), (dz2g, dm, 0, 0)],
                          out_rows=[(dm, BF16, dm, 0), (dm, F32, dm, 0)])
    dzg_b = _lin(dt_glu, w_glu, name="s5_glu_bwd_in", trans_w=True, tm=1024, tn=dm)
    dw_glu = _lin_w(zg, dt_glu, name="s5_glu_bwd_w", ta=dm, tn=dm, tkr=tkr)

    def gelu_bwd(rv, pv, cv, il):
        _, vjp = jax.vjp(_gelu, rv[0] + rv[1])
        return [vjp(rv[2] + rv[3])[0]], [], []

    (dy_s5,) = _rowk(gelu_bwd, name="s5_gelu_bwd", n=n, tm=dims["tm_row"], nctx=nctx,
                     rows=[(y_s5a, dm, 0, 0), (y_s5b, dm, 0, 0), (dzg_a, dm, 0, 0), (dzg_b, dm, 0, 0)],
                     out_rows=[(dm, F32, dm, 0)])
    dcw_both = jnp.stack([dcw_l0, dcw_l1], axis=1)
    s5g, (p_up0, p_dn0, p_cw, p_s5out, p_glu, p_hgin) = _s5_bwd(
        u_s5, dy_s5, *s5p["af"], *s5p["ab"], s5p["bb"], s5p["cc"], dsk, st_r, st_i, name="s5_bwd", dims=dims,
        xch=scatter([dwu_l0, dwd_l0.reshape(NDEV, -1, dm), dcw_both, dw_s5out.reshape(NDEV, -1, dm),
                     dw_glu.reshape(NDEV, -1, dm), dw_hgin]))
    (du_s5,) = _rowk(lambda rv, pv, cv, il: ([rv[0] + rv[1]], [], []), name="s5_du", n=n, tm=dims["tm_row"], nctx=nctx,
                     rows=[(s5g[0], dm, 0, 0), (s5g[1], dm, 0, 0)], out_rows=[(dm, BF16, dm, 0)])
    ddsk = s5g[14]
    dh0 = _lin(du_s5, w_s5in, name="s5_in_bwd_in", trans_w=True, tm=1024, tn=dm)
    dw_s5in = _lin_w(h0, du_s5, name="s5_in_bwd_w", ta=dm, tn=dm, tkr=tkr)
    dz0, dsh1_l0, dsc1_l0, dn1w_l0 = _norm_mod_bwd(dh0, z0, dz, n1w[0], pat[0][1], name="norm1_bwd_l0", dims=dims)

    dlr, dli, dls, g_b_re, g_b_im, g_c_re, g_c_im = _s5_param_grads(s5p, s5g[2:6], s5g[6:10], s5g[10:14])

    dmod = jnp.stack([
        jnp.concatenate([dsh1_l0, dsc1_l0, dgate1_l0, dsh2_l0, dsc2_l0, dgate2_l0], axis=1),
        jnp.concatenate([dsh1_l1, dsc1_l1, dgate1_l1, dsh2_l1, dsc2_l1, dgate2_l1], axis=1)])
    dmod_g, p_s5in = _exchange([dmod, dw_s5in.reshape(NDEV, -1, dm)], a2a=[False, True], name="gather_dmod")
    dlat_full = jnp.transpose(dmod_g[:, :, :bsz], (1, 0, 2, 3)).reshape(2, NDEV * bsz, N_MOD * dm)
    dctx_full = dmod_g[:, :, bsz]
    dlat_sh = lax.dynamic_slice(dlat_full, (0, 0, me * nsm), (2, NDEV * bsz, nsm))
    dctx_sh = lax.dynamic_slice(dctx_full, (0, 0, me * nsm), (NDEV, 2, nsm))
    g_w_mod, g_b_mod, dcctx8 = _mod_bwd(craw, w_mod, dlat_sh, dctx_sh, dlat_full, dctx_full)

    dl_hg = jnp.stack([dl_f[:, 0], dl_b[:, 0]])
    wide = lambda g: g.reshape(-1, dm)
    small = [("c_ctx", wide(dcctx8[:1])), ("norm1_w", jnp.concatenate([dn1w_l0, dn1w_l1])),
             ("norm2_w", jnp.concatenate([dn2w_l0, dn2w_l1])), ("final_norm_w", dfinal_w),
             ("s5_lam_re", dlr.reshape(-1, nstate)), ("s5_lam_im", dli.reshape(-1, nstate)),
             ("s5_log_step", dls.reshape(2, ngrp)),
             ("s5_b_re", wide(g_b_re.astype(BF16))), ("s5_b_im", wide(g_b_im.astype(BF16))),
             ("s5_c_re", wide(g_c_re.astype(BF16))), ("s5_c_im", wide(g_c_im.astype(BF16))), ("s5_d", ddsk),
             ("hg_gnorm_w", dgnw), ("ffn_conv_b", jnp.stack([dcb_l0.reshape(-1), dcb_l1.reshape(-1)]))]
    gathered = _exchange([g for _, g in small] + [wide(dl_hg), loss_part], a2a=False, name="gather_small")
    res = {}
    for (k, g), parts in zip(small, gathered):
        w2, m2, v2 = (given[p + k].reshape(g.shape) for p in ("", "m_", "v_"))
        res[k] = tuple(o.reshape(given[k].shape) for o in _adamw(parts, w2, m2, v2, name="adamw_" + k))

    def total(parts, name):
        z = jnp.zeros(parts.shape[1:], F32)
        return _adamw(parts, z, z, z, name=name)[0]

    loss = jnp.sum(total(gathered[-1], "sum_loss"))
    dl_tot = total(gathered[-2], "sum_dlb").reshape(dl_hg.shape)
    nlb = hg_lower_bounds.shape[2]
    g_lb = lax.dynamic_slice(dl_tot, (0, 0, me * nlb), (2, 2, nlb))

    def adam_local(name, g, shape2):
        w, m, v = given[name], given["m_" + name], given["v_" + name]
        out = _adamw(g.reshape((1,) + shape2), w.reshape(shape2), m.reshape(shape2), v.reshape(shape2),
                     name="adamw_" + name)
        return tuple(o.reshape(w.shape) for o in out)

    def adam_parts(name, p):
        w, m, v = given[name], given["m_" + name], given["v_" + name]
        shape2 = (p.shape[0], -1, w.shape[-1])
        p3 = p.reshape(shape2)
        s2 = p3.shape[1:]
        out = _adamw(p3, w.reshape(s2), m.reshape(s2), v.reshape(s2), name="adamw_" + name)
        return tuple(o.reshape(w.shape) for o in out)

    res["hg_lower_bounds"] = adam_local("hg_lower_bounds", g_lb, (2 * 2, nlb))
    res["w_mod"] = adam_local("w_mod", g_w_mod, (2 * dm, nsm))
    res["b_mod"] = adam_local("b_mod", g_b_mod, (2, N_MOD * dm))
    res["s5_w_in"] = adam_parts("s5_w_in", p_s5in)
    res["s5_w_glu"] = adam_parts("s5_w_glu", p_glu)
    res["s5_w_out"] = adam_parts("s5_w_out", p_s5out)
    res["hg_w_in"] = adam_parts("hg_w_in", p_hgin)
    res["hg_w_out"] = adam_parts("hg_w_out", p_hgout)
    res["ffn_w_up"] = adam_parts("ffn_w_up", jnp.stack([p_up0, p_up1], axis=1))
    res["ffn_w_down"] = adam_parts("ffn_w_down", jnp.stack([p_dn0, p_dn1], axis=1))
    res["ffn_conv_w"] = adam_parts("ffn_conv_w", p_cw)

    grad_x = jnp.transpose(dz0[nctx:].reshape(lx, bsz, dm), (1, 0, 2))
    order = ["c_ctx", "w_mod", "b_mod", "norm1_w", "norm2_w", "final_norm_w", "s5_w_in", "s5_lam_re", "s5_lam_im",
             "s5_log_step", "s5_b_re", "s5_b_im", "s5_c_re", "s5_c_im", "s5_d", "s5_w_glu", "s5_w_out", "hg_w_in",
             "hg_lower_bounds", "hg_gnorm_w", "hg_w_out", "ffn_w_up", "ffn_conv_w", "ffn_conv_b", "ffn_w_down"]
    outs = [loss, grad_x]
    for j in range(4):
        outs += [res[k][j].reshape(given[k].shape) for k in order]
    return tuple(outs)
```

```python
import functools

import jax
import jax.numpy as jnp
from jax import lax
from jax.experimental import pallas as pl
from jax.experimental.pallas import tpu as pltpu

F32 = jnp.float32
BF16 = jnp.bfloat16
NDEV = 8
LOCAL_B = 4
NORM_EPS = 1e-6
N_MOD = 6
S5_GROUP = 16
S5_STATE = 64
S5_LAM_RE_MAX = -1e-4
HG_HEAD = 128
HG_ROWS = 128
GRID_W = 64
ADAM_LR, ADAM_B1, ADAM_B2, ADAM_EPS, ADAM_WD, ADAM_STEP = 0.001, 0.9, 0.999, 1e-08, 0.01, 10
VMEM_BYTES_V7X = 64 * 1024 * 1024
LANES = 128
SUBLANES = 8

NN = (((1,), (0,)), ((), ()))
NT = (((1,), (1,)), ((), ()))
TN = (((0,), (0,)), ((), ()))
MESH = pl.DeviceIdType.MESH


def _params(sem=None, vmem=None):
    kw = {}
    if sem is not None:
        kw["dimension_semantics"] = sem
    if vmem is not None:
        kw["vmem_limit_bytes"] = int(min(vmem, VMEM_BYTES_V7X - (4 << 20)))
    return pltpu.CompilerParams(**kw)


def _nbytes(shape, dtype):
    n = 1
    for s in shape:
        n *= 1 if s is None else s
    return n * jnp.dtype(dtype).itemsize


def _dot(a, b, dims=NN, precision=None):
    return lax.dot_general(a, b, dims, preferred_element_type=F32, precision=precision)


def _sigmoid(x):
    return 1.0 / (1.0 + jnp.exp(-x))


class _Xchg:
    def __init__(self, arrs, a2a):
        self.arrs, self.a2a, self.n = list(arrs), list(a2a), len(arrs)

    def out_shape(self):
        return [jax.ShapeDtypeStruct(a.shape if f else (NDEV,) + a.shape, a.dtype) for a, f in zip(self.arrs, self.a2a)]

    def scratch(self):
        return [pltpu.SemaphoreType.DMA((self.n * (NDEV - 1),)), pltpu.SemaphoreType.DMA((self.n * (NDEV - 1),)),
                pltpu.SemaphoreType.DMA((self.n,))]

    def _copies(self, ins, outs, sems, with_recvs):
        send_sems, recv_sems, loc_sems = sems
        x, y, c = lax.axis_index("x"), lax.axis_index("y"), lax.axis_index("c")
        me = 4 * x + 2 * y + c
        local, sends, recvs = [], [], []
        for a in range(self.n):
            src = ins[a].at[me] if self.a2a[a] else ins[a]
            local.append(pltpu.make_async_copy(src, outs[a].at[me], loc_sems.at[a]))
            for k in range(1, NDEV):
                px = (1 - x) if (k >> 2) & 1 else x
                py = (1 - y) if (k >> 1) & 1 else y
                pc = (1 - c) if k & 1 else c
                p = 4 * px + 2 * py + pc
                s = a * (NDEV - 1) + k - 1
                src = ins[a].at[p] if self.a2a[a] else ins[a]
                kw = dict(src_ref=src, send_sem=send_sems.at[s], recv_sem=recv_sems.at[s], device_id=(px, py, pc),
                          device_id_type=MESH)
                sends.append(pltpu.make_async_remote_copy(dst_ref=outs[a].at[me], **kw))
                if with_recvs:
                    recvs.append(pltpu.make_async_remote_copy(dst_ref=outs[a].at[p], **kw))
        return local, sends, recvs

    def start(self, ins, outs, sems):
        local, sends, _ = self._copies(ins, outs, sems, False)
        for cp in local + sends:
            cp.start()

    def wait(self, ins, outs, sems):
        local, sends, recvs = self._copies(ins, outs, sems, True)
        for cp in sends:
            cp.wait_send()
        for cp in recvs:
            cp.wait_recv()
        for cp in local:
            cp.wait()


def _exchange(arrs, *, a2a, name):
    xch = _Xchg(arrs, a2a if isinstance(a2a, (list, tuple)) else [a2a] * len(arrs))
    n = xch.n

    def body(*refs):
        xch.start(refs[:n], refs[n:2 * n], refs[2 * n:])
        xch.wait(refs[:n], refs[n:2 * n], refs[2 * n:])

    res = pl.pallas_call(
        body, name=name, out_shape=xch.out_shape(),
        in_specs=[pl.BlockSpec(memory_space=pl.ANY)] * n, out_specs=[pl.BlockSpec(memory_space=pl.ANY)] * n,
        scratch_shapes=xch.scratch(),
    )(*arrs)
    return list(res)


def _call(body, *, name, out_shape, grid, in_specs, out_specs, scratch, params, args, xch=None):
    in_specs, out_specs, out_shape, scratch, args = list(in_specs), list(out_specs), list(out_shape), list(scratch), list(args)
    n_in, n_out, n_scr = len(in_specs), len(out_shape), len(scratch)
    if xch is not None:
        k = xch.n
        inner = body

        def body(*refs):
            ins, xin = refs[:n_in], refs[n_in:n_in + k]
            outs, xout = refs[n_in + k:n_in + k + n_out], refs[n_in + k + n_out:n_in + 2 * k + n_out]
            scr = refs[n_in + 2 * k + n_out:n_in + 2 * k + n_out + n_scr]
            sems = refs[n_in + 2 * k + n_out + n_scr:]
            first = pl.program_id(0) == 0
            last = pl.program_id(0) == grid[0] - 1
            for ax in range(1, len(grid)):
                first = jnp.logical_and(first, pl.program_id(ax) == 0)
                last = jnp.logical_and(last, pl.program_id(ax) == grid[ax] - 1)

            @pl.when(first)
            def _():
                xch.start(xin, xout, sems)

            inner(*ins, *outs, *scr)

            @pl.when(last)
            def _():
                xch.wait(xin, xout, sems)

        anyspec = pl.BlockSpec(memory_space=pl.ANY)
        in_specs += [anyspec] * k
        out_specs += [anyspec] * k
        out_shape += xch.out_shape()
        scratch += xch.scratch()
        args += xch.arrs
    res = pl.pallas_call(body, name=name, out_shape=out_shape, grid=grid, in_specs=in_specs, out_specs=out_specs,
                         scratch_shapes=scratch, compiler_params=params)(*args)
    return list(res[:n_out]), list(res[n_out:])


def _mm(a, b, *, name, grid, a_spec, b_spec, o_spec, o_shape, o_dtype, dims, base=None):
    nk = grid[2]
    o_block = tuple(s for s in o_spec.block_shape if s is not None)

    def body(a_ref, b_ref, *rest):
        base_ref = rest[0] if base is not None else None
        o_ref, scr = rest[1 if base is not None else 0], rest[2 if base is not None else 1:]
        r = _dot(a_ref[...].astype(BF16), b_ref[...].astype(BF16), dims)
        if nk == 1:
            if base is not None:
                r = r + base_ref[...].astype(F32)
            o_ref[...] = r.astype(o_dtype)
        else:
            acc = scr[0]
            k = pl.program_id(2)

            @pl.when(k == 0)
            def _():
                acc[...] = r

            @pl.when(k > 0)
            def _():
                acc[...] += r

            @pl.when(k == nk - 1)
            def _():
                tot = acc[...] if base is None else acc[...] + base_ref[...].astype(F32)
                o_ref[...] = tot.astype(o_dtype)

    blocks = (_nbytes(a_spec.block_shape, a.dtype) + _nbytes(b_spec.block_shape, b.dtype) + _nbytes(o_block, o_dtype)
              + (_nbytes(o_block, base.dtype) if base is not None else 0))
    scratch = [pltpu.VMEM(o_block, F32)] if nk > 1 else []
    vmem = 2 * blocks + 3 * _nbytes(o_block, F32) + (8 << 20)
    return pl.pallas_call(
        body, name=name, out_shape=jax.ShapeDtypeStruct(o_shape, o_dtype), grid=grid,
        in_specs=[a_spec, b_spec] + ([o_spec] if base is not None else []), out_specs=o_spec, scratch_shapes=scratch,
        compiler_params=_params(("parallel", "parallel", "arbitrary"), vmem),
    )(a, b, *([base] if base is not None else []))


def _lin(a, w, *, name, trans_w=False, tm, tn, o_dtype=F32, kblk=0, base=None):
    m, kk = a.shape
    nout = w.shape[0] if trans_w else w.shape[1]
    if trans_w:
        b_spec = pl.BlockSpec((tn, kk), lambda j, i, k: (j, kblk))
    else:
        b_spec = pl.BlockSpec((kk, tn), lambda j, i, k: (0, j))
    return _mm(a, w, name=name, grid=(nout // tn, m // tm, 1), dims=NT if trans_w else NN, o_shape=(m, nout),
               o_dtype=o_dtype, o_spec=pl.BlockSpec((tm, tn), lambda j, i, k: (i, j)),
               a_spec=pl.BlockSpec((tm, kk), lambda j, i, k: (i, 0)), b_spec=b_spec, base=base)


def _lin_w(a, dy, *, name, ta, tn, tkr):
    m, ka = a.shape
    nout = dy.shape[1]
    return _mm(a, dy, name=name, grid=(ka // ta, nout // tn, m // tkr), dims=TN, o_shape=(ka, nout), o_dtype=BF16,
               o_spec=pl.BlockSpec((ta, tn), lambda i, j, k: (i, j)),
               a_spec=pl.BlockSpec((tkr, ta), lambda i, j, k: (k, i)),
               b_spec=pl.BlockSpec((tkr, tn), lambda i, j, k: (k, j)))


def _rowk(fn, *, name, n, tm, nctx, rows=(), pats=(), consts=(), out_rows=(), out_seg=(), out_acc=()):
    nb, ncb = n // tm, nctx // tm
    nr, npat, ncst = len(rows), len(pats), len(consts)
    no, nseg, nacc = len(out_rows), len(out_seg), len(out_acc)
    in_specs, blocks = [], 0
    for arr, w, cb, off in rows:
        in_specs.append(pl.BlockSpec((tm, w), lambda i, cb=cb, off=off: (jnp.maximum(i - off, 0), cb)))
        blocks += _nbytes((tm, w), arr.dtype)
    for p in pats:
        in_specs.append(pl.BlockSpec((None, SUBLANES, p.shape[2]), lambda i: (jnp.where(i >= ncb, 1, 0), 0, 0)))
    for cst in consts:
        in_specs.append(pl.BlockSpec(cst.shape, lambda i: (0, 0)))
    out_shape, out_specs = [], []
    for wt, dt, w, cb in out_rows:
        out_shape.append(jax.ShapeDtypeStruct((n, wt), dt))
        out_specs.append(pl.BlockSpec((tm, w), lambda i, cb=cb: (i, cb)))
        blocks += _nbytes((tm, w), dt)
    for w in out_seg:
        out_shape.append(jax.ShapeDtypeStruct((SUBLANES, w), F32))
        out_specs.append(pl.BlockSpec((SUBLANES, w), lambda i: (0, 0)))
    for w in out_acc:
        out_shape.append(jax.ShapeDtypeStruct((1, w), F32))
        out_specs.append(pl.BlockSpec((1, w), lambda i: (0, 0)))
    scratch = [pltpu.VMEM((2, SUBLANES, w), F32) for w in out_seg] + [pltpu.VMEM((SUBLANES, w), F32) for w in out_acc]

    def body(*refs):
        r_in = refs[:nr]
        p_in = refs[nr:nr + npat]
        c_in = refs[nr + npat:nr + npat + ncst]
        base = nr + npat + ncst
        o_rows = refs[base:base + no]
        o_seg = refs[base + no:base + no + nseg]
        o_acc = refs[base + no + nseg:base + no + nseg + nacc]
        s_seg = refs[base + no + nseg + nacc:base + no + nseg + nacc + nseg]
        s_acc = refs[base + no + nseg + nacc + nseg:]
        i = pl.program_id(0)
        rv = [r[...].astype(F32).reshape(tm // SUBLANES, SUBLANES, r.shape[1]) for r in r_in]
        pv = [p[...] for p in p_in]
        cv = [c[...] for c in c_in]
        is_lat = (i >= ncb).astype(F32)
        ro, so, ao = fn(rv, pv, cv, is_lat)
        for ref, val in zip(o_rows, ro):
            ref[...] = val.reshape(tm, ref.shape[1]).astype(ref.dtype)
        if nseg or nacc:
            @pl.when(i == 0)
            def _():
                for s in list(s_seg) + list(s_acc):
                    s[...] = jnp.zeros(s.shape, F32)

            seg = jnp.where(i >= ncb, 1, 0)
            for s, val in zip(s_seg, so):
                s[seg] = s[seg] + val
            for s, val in zip(s_acc, ao):
                s[...] = s[...] + val

            @pl.when(i == nb - 1)
            def _():
                for o, s in zip(o_seg, s_seg):
                    lat, ctx = s[1], s[0]
                    row = lax.broadcasted_iota(jnp.int32, lat.shape, 0)
                    lat = lat + pltpu.roll(lat, 4, 0)
                    ctx = jnp.broadcast_to(jnp.sum(ctx, axis=0, keepdims=True), lat.shape)
                    o[...] = jnp.where(row < 4, lat, jnp.where(row == 4, ctx, 0.0))
                for o, s in zip(o_acc, s_acc):
                    o[...] = jnp.sum(s[...], axis=0, keepdims=True)

    vmem = 2 * blocks + 8 * tm * 1024 * 4 + (8 << 20)
    res = pl.pallas_call(
        body, name=name, out_shape=out_shape, grid=(nb,), in_specs=in_specs, out_specs=out_specs,
        scratch_shapes=scratch, compiler_params=_params(("arbitrary",), vmem),
    )(*[r[0] for r in rows], *pats, *consts)
    return list(res)


def _rms(z):
    return lax.rsqrt(jnp.mean(z * z, axis=-1, keepdims=True) + NORM_EPS)


def _norm_mod_fwd(z, w, sh, sc, *, name, dims, res=None):
    n, d = z.shape

    def fn(rv, pv, cv, is_lat):
        zz = rv[0]
        if res is not None:
            zz = zz + pv[2][None] * rv[1]
        h = (zz * _rms(zz) * cv[0]) * (1.0 + pv[1][None]) + pv[0][None]
        return ([zz, h] if res is not None else [h]), [], []

    rows = [(z, d, 0, 0)] + ([(res[0], d, 0, 0)] if res is not None else [])
    pats = [sh, sc] + ([res[1]] if res is not None else [])
    outs = ([(d, F32, d, 0)] if res is not None else []) + [(d, BF16, d, 0)]
    out = _rowk(fn, name=name, n=n, tm=dims["tm_row"], nctx=dims["nctx"], rows=rows, pats=pats, consts=[w],
                out_rows=outs)
    return (out[0], out[1]) if res is not None else (None, out[0])


def _norm_core_bwd(zin, dh, w, sc):
    r = _rms(zin)
    xh = zin * r
    dsh = jnp.sum(dh, axis=0)
    dsc = jnp.sum(dh * (xh * w), axis=0)
    dyv = dh * (1.0 + sc[None])
    dw = jnp.sum(dyv * xh, axis=0)
    dxh = dyv * w
    dx = r * (dxh - xh * jnp.mean(dxh * xh, axis=-1, keepdims=True))
    return dx, dsh, dsc, dw


def _norm_mod_bwd(dh, zin, dz_up, w, sc, *, name, dims, res=None):
    n, d = zin.shape

    def fn(rv, pv, cv, is_lat):
        dx, dsh, dsc, dw = _norm_core_bwd(rv[1], rv[0], cv[0], pv[0])
        dz = rv[2] + dx
        if res is None:
            return [dz], [dsh, dsc], [dw]
        return [dz, dz * pv[1][None]], [dsh, dsc, jnp.sum(dz * rv[3], axis=0)], [dw]

    rows = [(dh, d, 0, 0), (zin, d, 0, 0), (dz_up, d, 0, 0)] + ([(res[0], d, 0, 0)] if res is not None else [])
    pats = [sc] + ([res[1]] if res is not None else [])
    outs = [(d, F32, d, 0)] + ([(d, BF16, d, 0)] if res is not None else [])
    return _rowk(fn, name=name, n=n, tm=dims["tm_row"], nctx=dims["nctx"], rows=rows, pats=pats, consts=[w],
                 out_rows=outs, out_seg=[d] * (3 if res is not None else 2), out_acc=[d])


def _loss_bwd(z1, f, gate, tgt, w, *, name, dims):
    n, d = z1.shape

    def fn(rv, pv, cv, is_lat):
        z2 = rv[0] + pv[0][None] * rv[1]
        r = _rms(z2)
        xh = z2 * r
        err = (xh * cv[0] - rv[2]) * is_lat
        dout = err * (1.0 / d)
        dxh = dout * cv[0]
        dz = r * (dxh - xh * jnp.mean(dxh * xh, axis=-1, keepdims=True))
        return ([dz, dz * pv[0][None]], [jnp.sum(dz * rv[1], axis=0)],
                [jnp.sum(0.5 * err * err * (1.0 / d), axis=0), jnp.sum(dout * xh, axis=0)])

    tm = dims["tm_row"]
    rows = [(z1, d, 0, 0), (f, d, 0, 0), (tgt, d, 0, dims["nctx"] // tm)]
    return _rowk(fn, name=name, n=n, tm=tm, nctx=dims["nctx"], rows=rows, pats=[gate], consts=[w],
                 out_rows=[(d, F32, d, 0), (d, BF16, d, 0)], out_seg=[d], out_acc=[d, d])


def _gelu(y):
    return jax.nn.gelu(y, approximate=True)


def _conv_masks(tb, i):
    tok = lax.broadcasted_iota(jnp.int32, (tb, 1), 0) >> 2
    last = jnp.where(i == 0, tb // LOCAL_B - 1, GRID_W - 1)
    wpos = tok & last
    return wpos == 0, wpos == last


CONV_LANES = 2 * LANES


def _conv_taps(u_ref, cw_ref, cb_ref, no_left, no_right, tb):
    uu = u_ref[...]
    ul = jnp.where(no_left, 0.0, pltpu.roll(uu, LOCAL_B, 0))
    ur = jnp.where(no_right, 0.0, pltpu.roll(uu, tb - LOCAL_B, 0))
    val = cb_ref[...] + ul * cw_ref[pl.ds(0, 1), :] + uu * cw_ref[pl.ds(1, 1), :] + ur * cw_ref[pl.ds(2, 1), :]
    return val, ul, uu, ur


def _convffn_specs(tb, nj):
    cl = CONV_LANES
    return [pl.BlockSpec((tb, cl), lambda j, i: (i, j)), pl.BlockSpec((tb, cl), lambda j, i: (i, nj + j)),
            pl.BlockSpec((3, cl), lambda j, i: (0, j)), pl.BlockSpec((3, cl), lambda j, i: (0, nj + j)),
            pl.BlockSpec((1, cl), lambda j, i: (0, j)), pl.BlockSpec((1, cl), lambda j, i: (0, nj + j))]


def _convffn_fwd(u, cw, cb, *, name, dims):
    n, f2 = u.shape
    tb, nj = dims["nctx"], f2 // 2 // CONV_LANES

    def body(ua_ref, ug_ref, cwa_ref, cwg_ref, cba_ref, cbg_ref, o_ref):
        no_left, no_right = _conv_masks(tb, pl.program_id(1))
        a = _conv_taps(ua_ref, cwa_ref, cba_ref, no_left, no_right, tb)[0]
        g = _conv_taps(ug_ref, cwg_ref, cbg_ref, no_left, no_right, tb)[0]
        o_ref[...] = (a * _sigmoid(a) * g).astype(BF16)

    vmem = 16 * tb * CONV_LANES * 4 + (8 << 20)
    return pl.pallas_call(
        body, name=name, out_shape=jax.ShapeDtypeStruct((n, f2 // 2), BF16), grid=(nj, n // tb),
        in_specs=_convffn_specs(tb, nj), out_specs=pl.BlockSpec((tb, CONV_LANES), lambda j, i: (i, j)),
        compiler_params=_params(("parallel", "arbitrary"), vmem),
    )(u, u, cw, cw, cb, cb)


def _convffn_bwd(u, dhm, cw, cb, *, name, dims, xch=None):
    n, f2 = u.shape
    tb, nj = dims["nctx"], f2 // 2 // CONV_LANES

    def body(ua_ref, ug_ref, cwa_ref, cwg_ref, cba_ref, cbg_ref, dh_ref, dua_ref, dug_ref, dcwa_ref, dcwg_ref, dcba_ref,
             dcbg_ref):
        i = pl.program_id(1)
        no_left, no_right = _conv_masks(tb, i)

        @pl.when(i == 0)
        def _():
            for ref in (dcwa_ref, dcwg_ref, dcba_ref, dcbg_ref):
                ref[...] = jnp.zeros(ref.shape, F32)

        a, al, ac, ar = _conv_taps(ua_ref, cwa_ref, cba_ref, no_left, no_right, tb)
        g, gl, gc, gr = _conv_taps(ug_ref, cwg_ref, cbg_ref, no_left, no_right, tb)
        dh = dh_ref[...].astype(F32)
        sa = _sigmoid(a)
        dg = dh * (a * sa)
        da = dh * g * (sa * (1.0 + a * (1.0 - sa)))
        for dc, (tl, tc, tr), cw_ref, du_ref, dcw_ref, dcb_ref in (
                (da, (al, ac, ar), cwa_ref, dua_ref, dcwa_ref, dcba_ref),
                (dg, (gl, gc, gr), cwg_ref, dug_ref, dcwg_ref, dcbg_ref)):
            dcb_ref[...] += jnp.sum(dc, axis=0, keepdims=True)
            dcw_ref[pl.ds(0, 1), :] += jnp.sum(dc * tl, axis=0, keepdims=True)
            dcw_ref[pl.ds(1, 1), :] += jnp.sum(dc * tc, axis=0, keepdims=True)
            dcw_ref[pl.ds(2, 1), :] += jnp.sum(dc * tr, axis=0, keepdims=True)
            du = (dc * cw_ref[pl.ds(1, 1), :]
                  + pltpu.roll(jnp.where(no_left, 0.0, dc) * cw_ref[pl.ds(0, 1), :], tb - LOCAL_B, 0)
                  + pltpu.roll(jnp.where(no_right, 0.0, dc) * cw_ref[pl.ds(2, 1), :], LOCAL_B, 0))
            du_ref[...] = du.astype(BF16)

    cl, f = CONV_LANES, f2 // 2
    sd = jax.ShapeDtypeStruct
    row = pl.BlockSpec((tb, cl), lambda j, i: (i, j))
    vmem = 24 * tb * cl * 4 + (8 << 20)
    return _call(
        body, name=name, xch=xch, args=[u, u, cw, cw, cb, cb, dhm], scratch=[],
        out_shape=[sd((n, f), BF16), sd((n, f), BF16), sd((3, f), F32), sd((3, f), F32), sd((1, f), F32), sd((1, f), F32)],
        grid=(nj, n // tb), in_specs=_convffn_specs(tb, nj) + [row],
        out_specs=[row, row, pl.BlockSpec((3, cl), lambda j, i: (0, j)), pl.BlockSpec((3, cl), lambda j, i: (0, j)),
                   pl.BlockSpec((1, cl), lambda j, i: (0, j)), pl.BlockSpec((1, cl), lambda j, i: (0, j))],
        params=_params(("arbitrary", "arbitrary"), vmem))


def _s5_disc(lr, li, ls, brt, bit):
    lr = jnp.minimum(lr, S5_LAM_RE_MAX)
    dt = jnp.exp(ls)
    mag = jnp.exp(lr * dt)
    ar = mag * jnp.cos(li * dt)
    ai = mag * jnp.sin(li * dt)
    den = lr * lr + li * li
    nr = ar - 1.0
    cr = (nr * lr + ai * li) / den
    ci = (ai * lr - nr * li) / den
    return ar, ai, cr * brt - ci * bit, cr * bit + ci * brt


def _s5_disc_fwd(lr, li, ls, brt, bit):
    def body(lr_ref, li_ref, ls_ref, br_ref, bi_ref, ar_ref, ai_ref, bbr_ref, bbi_ref):
        ar, ai, bbr, bbi = _s5_disc(lr_ref[...], li_ref[...], ls_ref[...], br_ref[...], bi_ref[...])
        ar_ref[...] = ar
        ai_ref[...] = ai
        bbr_ref[...] = bbr
        bbi_ref[...] = bbi

    sd = jax.ShapeDtypeStruct
    return pl.pallas_call(body, name="s5_disc_fwd",
                          out_shape=[sd(lr.shape, F32), sd(lr.shape, F32), sd(brt.shape, F32), sd(brt.shape, F32)],
                          compiler_params=_params(None, 32 << 20))(lr, li, ls, brt, bit)


def _s5_disc_bwd(lr, li, ls, brt, bit, dar, dai, dbbr, dbbi):
    def body(lr_ref, li_ref, ls_ref, br_ref, bi_ref, dar_ref, dai_ref, dbbr_ref, dbbi_ref,
             dlr_ref, dli_ref, dls_ref, dbr_ref, dbi_ref):
        _, vjp = jax.vjp(_s5_disc, lr_ref[...], li_ref[...], ls_ref[...], br_ref[...], bi_ref[...])
        dlr, dli, dls, dbr, dbi = vjp((dar_ref[...], dai_ref[...], dbbr_ref[...], dbbi_ref[...]))
        dlr_ref[...] = dlr
        dli_ref[...] = dli
        dls_ref[...] = dls
        dbr_ref[...] = dbr
        dbi_ref[...] = dbi

    sd = jax.ShapeDtypeStruct
    return pl.pallas_call(body, name="s5_disc_bwd",
                          out_shape=[sd(lr.shape, F32), sd(lr.shape, F32), sd(ls.shape, F32), sd(brt.shape, F32),
                                     sd(brt.shape, F32)],
                          compiler_params=_params(None, 48 << 20))(lr, li, ls, brt, bit, dar, dai, dbbr, dbbi)


def _cmul(ar, ai, xr, xi):
    return ar * xr - ai * xi, ar * xi + ai * xr


def _s5_chunk_of(step, ncc, nc, rev):
    if not rev:
        return step
    return jnp.where(step < ncc, ncc - 1 - step, nc - 1 - (step - ncc))


def _s5_scan2(asc, desc, row0, nrows, a_r_ref, a_i_ref, cr_ref, ci_ref, *, lane_block, extra=None):
    width = asc[0].shape[1]
    nt = nrows // SUBLANES
    for lb in range(width // lane_block):
        lanes = pl.ds(lb * lane_block, lane_block)
        a1r, a1i = a_r_ref[:, lanes], a_i_ref[:, lanes]
        a2r, a2i = pltpu.roll(a1r, 4, 0), pltpu.roll(a1i, 4, 0)
        lo = lax.broadcasted_iota(jnp.int32, a1r.shape, 0) < 4

        def step(t, carry):
            pr, pi = carry[0], carry[1]
            ra = pl.ds(pl.multiple_of(row0 + t * SUBLANES, SUBLANES), SUBLANES)
            rd = pl.ds(pl.multiple_of(row0 + (nt - 1 - t) * SUBLANES, SUBLANES), SUBLANES)
            ur, ui = asc[0][ra, lanes], asc[1][ra, lanes]
            dr, di = desc[0][rd, lanes], desc[1][rd, lanes]
            mr, mi = _cmul(a1r, a1i, pr, pi)
            y1r, y1i = jnp.where(lo, ur, dr) + mr, jnp.where(lo, ui, di) + mi
            mr, mi = _cmul(a2r, a2i, pltpu.roll(y1r, 4, 0), pltpu.roll(y1i, 4, 0))
            y2r, y2i = jnp.where(lo, dr, ur) + mr, jnp.where(lo, di, ui) + mi
            our, oui = jnp.where(lo, y1r, y2r), jnp.where(lo, y1i, y2i)
            odr, odi = jnp.where(lo, y2r, y1r), jnp.where(lo, y2i, y1i)
            asc[0][ra, lanes] = our
            asc[1][ra, lanes] = oui
            desc[0][rd, lanes] = odr
            desc[1][rd, lanes] = odi
            nxt = (pltpu.roll(y2r, 4, 0), pltpu.roll(y2i, 4, 0))
            if extra is None:
                return nxt
            return nxt + tuple(extra(t, nt - 1 - t, lanes, (our, oui), (odr, odi), carry[2:]))

        init = (cr_ref[:, lanes], ci_ref[:, lanes])
        if extra is not None:
            init = init + tuple(extra.init(lanes))
        out = lax.fori_loop(0, nt, step, init)
        cr_ref[:, lanes] = out[0]
        ci_ref[:, lanes] = out[1]
        if extra is not None:
            extra.done(lanes, out[2:])


S5_SPLIT = 2


def _s5_fwd(u, af_r, af_i, bb, cc, dsk, *, name, dims, xch=None):
    n, dm = u.shape
    nk, swk = bb[0].shape[0], bb[0].shape[2]
    rr, sw = dims["s5_rows"], nk * swk
    nkh, dmh, swh = nk // S5_SPLIT, dm // S5_SPLIT, sw // S5_SPLIT
    nc, ncc = n // rr, dims["nctx"] // rr
    c1 = lambda i: _s5_chunk_of(i, ncc, nc, True)

    def body(u0_ref, u1_ref, afr_ref, afi_ref, b0r, b0i, b1r, b1i, c0r, c0i, c1r, c1i, dsk_ref,
             y0_ref, y1_ref, str_ref, sti_ref, s0r, s0i, s1r, s1i, cr, ci):
        @pl.when(pl.program_id(1) == 0)
        def _():
            cr[...] = jnp.zeros(cr.shape, F32)
            ci[...] = jnp.zeros(ci.shape, F32)

        str_ref[...] = cr[...]
        sti_ref[...] = ci[...]
        ub0, ub1 = u0_ref[...].astype(BF16), u1_ref[...].astype(BF16)
        for k in range(nkh):
            cols, sl = slice(k * LANES, (k + 1) * LANES), slice(k * swk, (k + 1) * swk)
            s0r[:, sl] = _dot(ub0[:, cols], b0r[k])
            s0i[:, sl] = _dot(ub0[:, cols], b0i[k])
            s1r[:, sl] = _dot(ub1[:, cols], b1r[k])
            s1i[:, sl] = _dot(ub1[:, cols], b1i[k])
        _s5_scan2((s0r, s0i), (s1r, s1i), 0, rr, afr_ref, afi_ref, cr, ci, lane_block=dims["s5_lane_block"])
        for k in range(nkh):
            cols, sl = slice(k * LANES, (k + 1) * LANES), slice(k * swk, (k + 1) * swk)
            y0_ref[:, cols] = (_dot(s0r[:, sl].astype(BF16), c0r[k]) - _dot(s0i[:, sl].astype(BF16), c0i[k])
                               + dsk_ref[:, cols] * u0_ref[:, cols])
            y1_ref[:, cols] = _dot(s1r[:, sl].astype(BF16), c1r[k]) - _dot(s1i[:, sl].astype(BF16), c1i[k])

    row0 = pl.BlockSpec((rr, dmh), lambda h, i: (i, h))
    row1 = pl.BlockSpec((rr, dmh), lambda h, i: (c1(i), h))
    tile = pl.BlockSpec((SUBLANES, swh), lambda h, i: (0, h))
    wspec = lambda a: pl.BlockSpec((nkh,) + a.shape[1:], lambda h, i: (h, 0, 0))
    st_spec = pl.BlockSpec((None, SUBLANES, swh), lambda h, i: (i, 0, h))
    sd = jax.ShapeDtypeStruct
    vmem = 4 * rr * swh * 4 + 12 * rr * dmh * 4 + 16 * nkh * LANES * swk * 2 + (12 << 20)
    return _call(
        body, name=name, xch=xch, args=[u, u, af_r, af_i, *bb, *cc, dsk],
        out_shape=[sd((n, dm), F32), sd((n, dm), F32), sd((nc, SUBLANES, sw), F32), sd((nc, SUBLANES, sw), F32)],
        grid=(S5_SPLIT, nc),
        in_specs=[row0, row1, tile, tile] + [wspec(a) for a in (*bb, *cc)] + [pl.BlockSpec((1, dmh), lambda h, i: (0, h))],
        out_specs=[row0, row1, st_spec, st_spec],
        scratch=[pltpu.VMEM((rr, swh), F32)] * 4 + [pltpu.VMEM((SUBLANES, swh), F32)] * 2,
        params=_params(("arbitrary", "arbitrary"), vmem))


class _DaHook2:
    def __init__(self, s0, s1, accs):
        self.s0, self.s1, self.accs = s0, s1, accs

    def init(self, lanes):
        return tuple(a[:, lanes] for a in self.accs)

    def done(self, lanes, acc):
        for a, v in zip(self.accs, acc):
            a[:, lanes] = v

    def __call__(self, t1, t0, lanes, l1, l0, acc):
        row = lax.broadcasted_iota(jnp.int32, l1[0].shape, 0)
        b1 = pl.multiple_of(SUBLANES + t1 * SUBLANES, SUBLANES)
        b0 = pl.multiple_of(SUBLANES + t0 * SUBLANES, SUBLANES)
        cur1, nxt1 = pl.ds(b1, SUBLANES), pl.ds(pl.multiple_of(b1 + SUBLANES, SUBLANES), SUBLANES)
        cur0, prv0 = pl.ds(b0, SUBLANES), pl.ds(pl.multiple_of(b0 - SUBLANES, SUBLANES), SUBLANES)
        p1r = pltpu.roll(jnp.where(row >= 4, self.s1[0][cur1, lanes], self.s1[0][nxt1, lanes]), 4, 0)
        p1i = pltpu.roll(jnp.where(row >= 4, self.s1[1][cur1, lanes], self.s1[1][nxt1, lanes]), 4, 0)
        p0r = pltpu.roll(jnp.where(row >= 4, self.s0[0][prv0, lanes], self.s0[0][cur0, lanes]), 4, 0)
        p0i = pltpu.roll(jnp.where(row >= 4, self.s0[1][prv0, lanes], self.s0[1][cur0, lanes]), 4, 0)
        return (acc[0] + p0r * l0[0] + p0i * l0[1], acc[1] + p0r * l0[1] - p0i * l0[0],
                acc[2] + p1r * l1[0] + p1i * l1[1], acc[3] + p1r * l1[1] - p1i * l1[0])


def _s5_bwd(u, dy, af_r, af_i, ab_r, ab_i, bb, cc, dsk, st_r, st_i, *, name, dims, xch=None):
    n, dm = u.shape
    nk, swk = bb[0].shape[0], bb[0].shape[2]
    rr, sw = dims["s5_rows"], nk * swk
    nkh, dmh, swh = nk // S5_SPLIT, dm // S5_SPLIT, sw // S5_SPLIT
    nc, ncc = n // rr, dims["nctx"] // rr
    f0 = lambda i: nc - 1 - i
    f1 = lambda i: _s5_chunk_of(nc - 1 - i, ncc, nc, True)

    def body(u0_ref, u1_ref, dy0_ref, dy1_ref, afr_ref, afi_ref, abr_ref, abi_ref, b0r, b0i, b1r, b1i, c0r, c0i, c1r, c1i,
             dsk_ref, str_ref, sti_ref,
             du0_ref, du1_ref, db0r, db0i, db1r, db1i, dc0r, dc0i, dc1r, dc1i, da0r_ref, da0i_ref, da1r_ref, da1i_ref, dd_ref,
             s0r, s0i, s1r, s1i, l0r, l0i, l1r, l1i, cr, ci, lcr, lci, a0r, a0i, a1r, a1i, dda):
        i = pl.program_id(1)

        @pl.when(i == 0)
        def _():
            for ref in (lcr, lci, a0r, a0i, a1r, a1i, dda, db0r, db0i, db1r, db1i, dc0r, dc0i, dc1r, dc1i):
                ref[...] = jnp.zeros(ref.shape, F32)

        row = lax.broadcasted_iota(jnp.int32, (SUBLANES, swh), 0)
        for st_ref, car, z0, z1 in ((str_ref, cr, s0r, s1r), (sti_ref, ci, s0i, s1i)):
            st = st_ref[...]
            car[...] = st
            z0[pl.ds(0, SUBLANES), :] = jnp.where(row < 4, st, pltpu.roll(st, 4, 0))
            z1[pl.ds(rr + SUBLANES, SUBLANES), :] = jnp.where(row >= 4, st, pltpu.roll(st, 4, 0))
        body_rows = pl.ds(SUBLANES, rr)
        ub0, ub1 = u0_ref[...].astype(BF16), u1_ref[...].astype(BF16)
        dyb0, dyb1 = dy0_ref[...].astype(BF16), dy1_ref[...].astype(BF16)
        for k in range(nkh):
            cols, sl = slice(k * LANES, (k + 1) * LANES), slice(k * swk, (k + 1) * swk)
            s0r[body_rows, sl] = _dot(ub0[:, cols], b0r[k])
            s0i[body_rows, sl] = _dot(ub0[:, cols], b0i[k])
            s1r[body_rows, sl] = _dot(ub1[:, cols], b1r[k])
            s1i[body_rows, sl] = _dot(ub1[:, cols], b1i[k])
        _s5_scan2((s0r, s0i), (s1r, s1i), SUBLANES, rr, afr_ref, afi_ref, cr, ci, lane_block=dims["s5_lane_block"])
        for k in range(nkh):
            cols, sl = slice(k * LANES, (k + 1) * LANES), slice(k * swk, (k + 1) * swk)
            for dyk, lr, li, sr, si, ccr, cci, dcr, dci in ((dyb0[:, cols], l0r, l0i, s0r, s0i, c0r, c0i, dc0r, dc0i),
                                                           (dyb1[:, cols], l1r, l1i, s1r, s1i, c1r, c1i, dc1r, dc1i)):
                lr[:, sl] = _dot(dyk, ccr[k], NT)
                li[:, sl] = -_dot(dyk, cci[k], NT)
                dcr[k] += _dot(dyk, sr[body_rows, sl].astype(BF16), TN)
                dci[k] -= _dot(dyk, si[body_rows, sl].astype(BF16), TN)
        _s5_scan2((l1r, l1i), (l0r, l0i), 0, rr, abr_ref, abi_ref, lcr, lci, lane_block=dims["s5_lane_block"] // 2,
                  extra=_DaHook2((s0r, s0i), (s1r, s1i), (a0r, a0i, a1r, a1i)))
        for k in range(nkh):
            cols, sl = slice(k * LANES, (k + 1) * LANES), slice(k * swk, (k + 1) * swk)
            for uk, lr, li, br, bi, dbr, dbi, du_ref, first in ((ub0[:, cols], l0r, l0i, b0r, b0i, db0r, db0i, du0_ref, True),
                                                              (ub1[:, cols], l1r, l1i, b1r, b1i, db1r, db1i, du1_ref, False)):
                lrk, lik = lr[:, sl].astype(BF16), li[:, sl].astype(BF16)
                dbr[k] += _dot(uk, lrk, TN)
                dbi[k] += _dot(uk, lik, TN)
                duk = _dot(lrk, br[k], NT) + _dot(lik, bi[k], NT)
                if first:
                    duk = duk + dsk_ref[:, cols] * dy0_ref[:, cols]
                du_ref[:, cols] = duk
        dda[...] += jnp.sum((dy0_ref[...] * u0_ref[...]).reshape(rr // SUBLANES, SUBLANES, dmh), axis=0)

        @pl.when(i == nc - 1)
        def _():
            for o, a in ((da0r_ref, a0r), (da0i_ref, a0i), (da1r_ref, a1r), (da1i_ref, a1i), (dd_ref, dda)):
                o[...] = jnp.sum(a[...], axis=0, keepdims=True)

    row0 = pl.BlockSpec((rr, dmh), lambda h, i: (f0(i), h))
    row1 = pl.BlockSpec((rr, dmh), lambda h, i: (f1(i), h))
    tile = pl.BlockSpec((SUBLANES, swh), lambda h, i: (0, h))
    wspec = lambda a: pl.BlockSpec((nkh,) + a.shape[1:], lambda h, i: (h, 0, 0))
    st_spec = pl.BlockSpec((None, SUBLANES, swh), lambda h, i: (f0(i), 0, h))
    vec = lambda w: pl.BlockSpec((1, w), lambda h, i: (0, h))
    sd = jax.ShapeDtypeStruct
    out_shape = ([sd((n, dm), F32)] * 2 + [sd(a.shape, F32) for a in (*bb, *bb)] + [sd((1, sw), F32)] * 4 + [sd((1, dm), F32)])
    out_specs = [row0, row1] + [wspec(a) for a in (*bb, *bb)] + [vec(swh)] * 4 + [vec(dmh)]
    scratch = ([pltpu.VMEM((rr + 2 * SUBLANES, swh), F32)] * 4 + [pltpu.VMEM((rr, swh), F32)] * 4
               + [pltpu.VMEM((SUBLANES, swh), F32)] * 8 + [pltpu.VMEM((SUBLANES, dmh), F32)])
    vmem = 8 * (rr + 16) * swh * 4 + 16 * rr * dmh * 4 + 48 * nkh * LANES * swk * 4 + (10 << 20)
    return _call(
        body, name=name, xch=xch, args=[u, u, dy, dy, af_r, af_i, ab_r, ab_i, *bb, *cc, dsk, st_r, st_i],
        out_shape=out_shape, grid=(S5_SPLIT, nc),
        in_specs=[row0, row1, row0, row1, tile, tile, tile, tile] + [wspec(a) for a in (*bb, *cc)] + [vec(dmh), st_spec, st_spec],
        out_specs=out_specs, scratch=scratch, params=_params(("arbitrary", "arbitrary"), vmem))


def _hg_mask(kind, rev):
    if kind == "tot":
        r = lax.broadcasted_iota(jnp.int32, (SUBLANES, HG_ROWS), 0)
        c = lax.broadcasted_iota(jnp.int32, (SUBLANES, HG_ROWS), 1)
        return (c & 3) == r
    r = lax.broadcasted_iota(jnp.int32, (HG_ROWS, HG_ROWS), 0)
    c = lax.broadcasted_iota(jnp.int32, (HG_ROWS, HG_ROWS), 1)
    same = (r & 3) == (c & 3)
    before = ((c >> 2) >= (r >> 2)) if rev else ((c >> 2) <= (r >> 2))
    return jnp.logical_and(same, before if kind == "cum" else jnp.logical_not(before))


def _split2(x):
    hi = x.astype(BF16)
    return hi, (x - hi.astype(F32)).astype(BF16)


@functools.partial(jax.custom_vjp, nondiff_argnums=(1, 2))
def _mask_sum(x, kind, rev):
    m = _hg_mask(kind, rev).astype(BF16)
    hi, lo = _split2(x)
    return _dot(m, hi) + _dot(m, lo)


def _mask_sum_fwd(x, kind, rev):
    return _mask_sum(x, kind, rev), None


def _mask_sum_bwd(kind, rev, _, g):
    m = _hg_mask(kind, rev).astype(BF16)
    hi, lo = _split2(g)
    return (_dot(m, hi, TN) + _dot(m, lo, TN),)


_mask_sum.defvjp(_mask_sum_fwd, _mask_sum_bwd)


def _hg_chunk(q, v, fraw, l0, l1, st, *, rev):
    nh = q.shape[1] // HG_HEAD
    lb = _sigmoid(l1 - l0)
    logf = jnp.logaddexp(jnp.log(lb), jnp.log1p(-lb) + jax.nn.log_sigmoid(fraw))
    kk = (1.0 - lb) * _sigmoid(fraw * -1.0)
    tri = _hg_mask("cum", rev)
    bcum = _mask_sum(logf, "cum", rev)
    brem = _mask_sum(logf, "rem", rev)
    bend8 = _mask_sum(logf, "tot", rev)
    r8d = lax.broadcasted_iota(jnp.int32, bend8.shape, 0)
    decs = [jnp.exp(jnp.sum(jnp.where(r8d == b, bend8, 0.0), axis=0, keepdims=True)) for b in range(LOCAL_B)]
    qd = (q * jnp.exp(bcum)).astype(BF16)
    kd = (kk * jnp.exp(-bcum)).astype(BF16)
    ke = (kk * jnp.exp(brem)).astype(BF16)
    wide = (HG_ROWS, LOCAL_B * HG_HEAD)
    mine = (lax.broadcasted_iota(jnp.int32, wide, 1) >> 7) == (lax.broadcasted_iota(jnp.int32, wide, 0) & 3)
    per_example = lambda x: jnp.where(mine, jnp.concatenate([x] * LOCAL_B, axis=1), jnp.zeros(wide, x.dtype))
    outs, new = [], []
    for h in range(nh):
        sl = slice(h * HG_HEAD, (h + 1) * HG_HEAD)
        vh = v[:, sl].astype(BF16)
        att = jnp.where(tri, _dot(qd[:, sl], kd[:, sl], NT), 0.0)
        outs.append(_dot(att.astype(BF16), vh) + _dot(per_example(qd[:, sl]), st[h].astype(BF16), NT))
        dec = jnp.concatenate([d[:, sl] for d in decs], axis=1)
        new.append(st[h] * dec + _dot(vh, per_example(ke[:, sl]), TN))
    return jnp.concatenate(outs, axis=1), tuple(new)


def _hg_chunk_of(step, ncc, nc, rev):
    return _s5_chunk_of(step, ncc, nc, rev)


def _hg_fwd_dir(zz, lb2, *, d, name, dims, xch=None):
    n = zz.shape[0]
    dm = zz.shape[1] // 5
    ns, sw = dm // HG_HEAD, LOCAL_B * HG_HEAD
    nc, ncc = n // HG_ROWS, dims["nctx"] // HG_ROWS
    rev = d == 1
    ch = lambda i: _hg_chunk_of(i, ncc, nc, rev)

    def body(q_ref, v_ref, f_ref, l0_ref, l1_ref, o_ref, st_ref, st):
        @pl.when(pl.program_id(0) == 0)
        def _():
            st[...] = jnp.zeros(st.shape, F32)

        st_ref[...] = st[...]
        o, new = _hg_chunk(q_ref[...], v_ref[...], f_ref[...], l0_ref[...], l1_ref[...],
                           tuple(st[j] for j in range(ns)), rev=rev)
        o_ref[...] = o
        for j in range(ns):
            st[j] = new[j]

    blk = lambda off: pl.BlockSpec((HG_ROWS, dm), lambda i, off=off: (ch(i), off))
    lspec = lambda layer: pl.BlockSpec((None, None, 1, dm), lambda i, layer=layer: (d, layer, 0, 0))
    return _call(
        body, name=name, xch=xch, args=[zz, zz, zz, lb2, lb2],
        out_shape=[jax.ShapeDtypeStruct((n, dm), F32), jax.ShapeDtypeStruct((nc, ns, HG_HEAD, sw), F32)],
        grid=(nc,),
        in_specs=[blk(0), blk(1), blk(2 + d), lspec(0), lspec(1)],
        out_specs=[pl.BlockSpec((HG_ROWS, dm), lambda i: (ch(i), 0)),
                   pl.BlockSpec((None, ns, HG_HEAD, sw), lambda i: (ch(i), 0, 0, 0))],
        scratch=[pltpu.VMEM((ns, HG_HEAD, sw), F32)],
        params=_params(("arbitrary",), 48 << 20))


def _hg_bwd_dir(zz, lb2, do, sts, dqv_prev, *, d, name, dims, xch=None):
    n = zz.shape[0]
    dm = zz.shape[1] // 5
    ns, sw = dm // HG_HEAD, LOCAL_B * HG_HEAD
    nc, ncc = n // HG_ROWS, dims["nctx"] // HG_ROWS
    rev = d == 1
    ch = lambda i: _hg_chunk_of(nc - 1 - i, ncc, nc, rev)
    qv_dtype = F32 if d == 0 else BF16

    def body(*refs):
        q_ref, v_ref, f_ref, l0_ref, l1_ref, do_ref, st_ref = refs[:7]
        pos = 7
        if d == 1:
            dqp_ref, dvp_ref = refs[7:9]
            pos = 9
        dq_ref, dv_ref, df_ref, dl_ref, dst = refs[pos:]
        i = pl.program_id(0)

        @pl.when(i == 0)
        def _():
            dst[...] = jnp.zeros(dst.shape, F32)
            dl_ref[...] = jnp.zeros(dl_ref.shape, F32)

        _, vjp = jax.vjp(functools.partial(_hg_chunk, rev=rev), q_ref[...], v_ref[...], f_ref[...], l0_ref[...],
                         l1_ref[...], tuple(st_ref[j] for j in range(ns)))
        dq, dv, df, dl0, dl1, dstn = vjp((do_ref[...], tuple(dst[j] for j in range(ns))))
        for j in range(ns):
            dst[j] = dstn[j]
        if d == 1:
            dq = dq + dqp_ref[...]
            dv = dv + dvp_ref[...]
        dq_ref[...] = dq.astype(qv_dtype)
        dv_ref[...] = dv.astype(qv_dtype)
        df_ref[...] = df.astype(BF16)
        dl_ref[0] += dl0
        dl_ref[1] += dl1

    blk = lambda off: pl.BlockSpec((HG_ROWS, dm), lambda i, off=off: (ch(i), off))
    oblk = pl.BlockSpec((HG_ROWS, dm), lambda i: (ch(i), 0))
    lspec = lambda layer: pl.BlockSpec((None, None, 1, dm), lambda i, layer=layer: (d, layer, 0, 0))
    ins = [zz, zz, zz, lb2, lb2, do, sts] + (list(dqv_prev) if d == 1 else [])
    in_specs = [blk(0), blk(1), blk(2 + d), lspec(0), lspec(1), oblk,
                pl.BlockSpec((None, ns, HG_HEAD, sw), lambda i: (ch(i), 0, 0, 0))]
    in_specs += [oblk, oblk] if d == 1 else []
    sd = jax.ShapeDtypeStruct
    return _call(
        body, name=name, xch=xch, args=ins,
        out_shape=[sd((n, dm), qv_dtype), sd((n, dm), qv_dtype), sd((n, dm), BF16), sd((2, 1, dm), F32)],
        grid=(nc,), in_specs=in_specs,
        out_specs=[oblk, oblk, oblk, pl.BlockSpec((2, 1, dm), lambda i: (0, 0, 0))],
        scratch=[pltpu.VMEM((ns, HG_HEAD, sw), F32)],
        params=_params(("arbitrary",), 56 << 20))


def _hg_readout(o, g, w):
    outs = []
    for h in range(o.shape[-1] // HG_HEAD):
        sl = slice(h * HG_HEAD, (h + 1) * HG_HEAD)
        oh = o[..., sl]
        outs.append(oh * _rms(oh) * w * _sigmoid(g[..., sl]))
    return jnp.concatenate(outs, axis=-1)


def _silu(x):
    return x * _sigmoid(x)


def _mod_fwd(craw, w, b):
    def body(c_ref, w_ref, b_ref, o_ref):
        s = _silu(c_ref[...]).astype(BF16)
        for layer in range(w.shape[0]):
            o_ref[layer] = _dot(s, w_ref[layer].astype(BF16)) + b_ref[layer]

    return pl.pallas_call(body, name="mod_fwd",
                          out_shape=jax.ShapeDtypeStruct((w.shape[0], craw.shape[0], w.shape[2]), F32),
                          compiler_params=_params(None, 40 << 20))(craw, w, b)


def _mod_bwd(craw, w, dlat_sh, dctx_sh, dlat_full, dctx_full):
    nl, dm, ns = w.shape
    nb = dlat_sh.shape[1]

    def body(c_ref, w_ref, dl_ref, dc_ref, dlf_ref, dcf_ref, dw_ref, db_ref, dcc_ref):
        craw_v = c_ref[...]
        s = _silu(craw_v)
        s_lat = s[:nb].astype(BF16)
        s_ctx = s[nb:].astype(BF16)
        row = lax.broadcasted_iota(jnp.int32, (SUBLANES, ns), 0)
        dsc = jnp.zeros((SUBLANES, dm), F32)
        for layer in range(nl):
            tot = dc_ref[0, pl.ds(layer, 1), :]
            totf = dcf_ref[0, pl.ds(layer, 1), :]
            for i in range(1, NDEV):
                tot = tot + dc_ref[i, pl.ds(layer, 1), :]
                totf = totf + dcf_ref[i, pl.ds(layer, 1), :]
            dc8 = jnp.where(row == 0, jnp.broadcast_to(tot, (SUBLANES, ns)), 0.0).astype(BF16)
            dw_ref[layer] = _dot(s_lat, dl_ref[layer].astype(BF16), TN) + _dot(s_ctx, dc8, TN)
            db_ref[layer] = jnp.sum(dlf_ref[layer], axis=0, keepdims=True) + totf
            dsc = dsc + _dot(dc8, w_ref[layer].astype(BF16), NT)
        cc = craw_v[nb:]
        sg = _sigmoid(cc)
        dcc_ref[...] = dsc * (sg * (1.0 + cc * (1.0 - sg)))

    sd = jax.ShapeDtypeStruct
    return pl.pallas_call(body, name="mod_bwd",
                          out_shape=[sd((nl, dm, ns), F32), sd((nl, 1, dlat_full.shape[2]), F32), sd((SUBLANES, dm), F32)],
                          compiler_params=_params(None, 48 << 20))(craw, w, dlat_sh, dctx_sh, dlat_full, dctx_full)


def _adam_rows(r):
    best = None
    for t in range(2 * SUBLANES, min(r, 128) + 1, 2 * SUBLANES):
        if r % t == 0:
            best = t
    return best if best is not None else r


def _adamw(parts, w, m, v, *, name):
    npart, r, c = parts.shape
    tr = _adam_rows(r)

    def body(p_ref, w_ref, m_ref, v_ref, g_ref, d_ref, nm_ref, nv_ref):
        g = p_ref[0].astype(F32)
        for i in range(1, npart):
            g = g + p_ref[i].astype(F32)
        nm = ADAM_B1 * m_ref[...] + (1.0 - ADAM_B1) * g
        nv = ADAM_B2 * v_ref[...] + (1.0 - ADAM_B2) * (g * g)
        m_hat = nm / (1.0 - ADAM_B1 ** ADAM_STEP)
        v_hat = nv / (1.0 - ADAM_B2 ** ADAM_STEP)
        g_ref[...] = g
        d_ref[...] = -ADAM_LR * (m_hat / (jnp.sqrt(v_hat) + ADAM_EPS) + ADAM_WD * w_ref[...])
        nm_ref[...] = nm
        nv_ref[...] = nv

    spec = pl.BlockSpec((tr, c), lambda i: (i, 0))
    vmem = 2 * (npart + 7) * tr * c * 4 + (8 << 20)
    return pl.pallas_call(
        body, name=name, out_shape=[jax.ShapeDtypeStruct((r, c), F32)] * 4, grid=(r // tr,),
        in_specs=[pl.BlockSpec((npart, tr, c), lambda i: (0, i, 0)), spec, spec, spec], out_specs=[spec] * 4,
        compiler_params=_params(("parallel",), vmem),
    )(parts, w, m, v)


def _to_tm(a):
    return jnp.transpose(a, (1, 0, 2)).reshape(a.shape[1] * a.shape[0], a.shape[2])


def _pattern(mod_lat, mod_ctx, m, dm):
    lat = mod_lat[:, m * dm:(m + 1) * dm]
    ctx = jnp.broadcast_to(mod_ctx[None, m * dm:(m + 1) * dm], (SUBLANES, dm))
    return jnp.stack([ctx, jnp.concatenate([lat, lat], axis=0)])


def _blockdiag_b(bt, nk):
    g, h, p = bt.shape
    t = bt.reshape(nk, 8, h, p)
    return jnp.einsum("kghp,gj->kghjp", t, jnp.eye(8, dtype=bt.dtype)).reshape(nk, 8 * h, 8 * p)


def _blockdiag_c(ct, nk):
    g, h, p = ct.shape
    t = ct.reshape(nk, 8, h, p)
    return jnp.einsum("kghp,gj->kgpjh", t, jnp.eye(8, dtype=ct.dtype)).reshape(nk, 8 * p, 8 * h)


def _diag_b(dbb, h, p):
    nk = dbb.shape[0]
    return jnp.einsum("kghgp->kghp", dbb.reshape(nk, 8, h, 8, p)).reshape(nk * 8, h, p)


def _s5_prep(lam_re, lam_im, log_step, b_re, b_im, c_re, c_im, dm):
    ngrp, nk = dm // S5_GROUP, dm // LANES
    sw = ngrp * S5_STATE
    lr4 = lam_re.reshape(2, ngrp, 1, S5_STATE)
    li4 = lam_im.reshape(2, ngrp, 1, S5_STATE)
    ls4 = log_step.reshape(2, ngrp, 1, 1)
    brt = jnp.transpose(b_re, (0, 1, 3, 2))
    bit = jnp.transpose(b_im, (0, 1, 3, 2))
    abar_r, abar_i, bbar_r, bbar_i = _s5_disc_fwd(lr4, li4, ls4, brt, bit)
    half = lambda a, d: jnp.broadcast_to(a[d].reshape(1, sw), (LOCAL_B, sw))
    tile = lambda a, first: jnp.concatenate([half(a, first), half(a, 1 - first)], axis=0)
    bb = tuple(_blockdiag_b(w[d], nk).astype(BF16) for d in range(2) for w in (bbar_r, bbar_i))
    cc = tuple(_blockdiag_c(w[d], nk).astype(BF16) for d in range(2) for w in (c_re, c_im))
    return dict(disc_in=(lr4, li4, ls4, brt, bit), af=(tile(abar_r, 0), tile(abar_i, 0)),
                ab=(tile(abar_r, 1), -tile(abar_i, 1)), bb=bb, cc=cc)


def _s5_param_grads(prep, dbb, dcc, da):
    lr4 = prep["disc_in"][0]
    dar = jnp.stack([da[0], da[2]]).reshape(lr4.shape)
    dai = jnp.stack([da[1], da[3]]).reshape(lr4.shape)
    dbbr = jnp.stack([_diag_b(dbb[0], S5_GROUP, S5_STATE), _diag_b(dbb[2], S5_GROUP, S5_STATE)])
    dbbi = jnp.stack([_diag_b(dbb[1], S5_GROUP, S5_STATE), _diag_b(dbb[3], S5_GROUP, S5_STATE)])
    dlr, dli, dls, dbrt, dbit = _s5_disc_bwd(*prep["disc_in"], dar, dai, dbbr, dbbi)
    g_c_re = jnp.stack([_diag_b(dcc[0], S5_GROUP, S5_STATE), _diag_b(dcc[2], S5_GROUP, S5_STATE)])
    g_c_im = jnp.stack([_diag_b(dcc[1], S5_GROUP, S5_STATE), _diag_b(dcc[3], S5_GROUP, S5_STATE)])
    return dlr, dli, dls, jnp.transpose(dbrt, (0, 1, 3, 2)), jnp.transpose(dbit, (0, 1, 3, 2)), g_c_re, g_c_im


def kernel(x, c, ctx, c_ctx, w_mod, b_mod, norm1_w, norm2_w, final_norm_w, s5_w_in, s5_lam_re, s5_lam_im, s5_log_step, s5_b_re, s5_b_im, s5_c_re, s5_c_im, s5_d, s5_w_glu, s5_w_out, hg_w_in, hg_lower_bounds, hg_gnorm_w, hg_w_out, ffn_w_up, ffn_conv_w, ffn_conv_b, ffn_w_down, loss_target, m_c_ctx, m_w_mod, m_b_mod, m_norm1_w, m_norm2_w, m_final_norm_w, m_s5_w_in, m_s5_lam_re, m_s5_lam_im, m_s5_log_step, m_s5_b_re, m_s5_b_im, m_s5_c_re, m_s5_c_im, m_s5_d, m_s5_w_glu, m_s5_w_out, m_hg_w_in, m_hg_lower_bounds, m_hg_gnorm_w, m_hg_w_out, m_ffn_w_up, m_ffn_conv_w, m_ffn_conv_b, m_ffn_w_down, v_c_ctx, v_w_mod, v_b_mod, v_norm1_w, v_norm2_w, v_final_norm_w, v_s5_w_in, v_s5_lam_re, v_s5_lam_im, v_s5_log_step, v_s5_b_re, v_s5_b_im, v_s5_c_re, v_s5_c_im, v_s5_d, v_s5_w_glu, v_s5_w_out, v_hg_w_in, v_hg_lower_bounds, v_hg_gnorm_w, v_hg_w_out, v_ffn_w_up, v_ffn_conv_w, v_ffn_conv_b, v_ffn_w_down):
    given = dict(locals())
    bsz, lx, dm = x.shape
    lc = ctx.shape[1]
    assert bsz == LOCAL_B and w_mod.shape[0] == 2 and dm % LANES == 0
    n, nctx = (lc + lx) * bsz, lc * bsz
    ngrp, nstate, hgrp = dm // S5_GROUP, S5_STATE, S5_GROUP
    nk = dm // LANES
    dims = dict(nctx=nctx, tm=min(512, nctx), tm_row=min(256, nctx), s5_rows=min(256, nctx),
                s5_lane_block=min(512, 8 * nstate))
    tm = dims["tm"]
    assert nctx % HG_ROWS == 0 and (lx * bsz) % nctx == 0 and lc % GRID_W == 0 and lc & (lc - 1) == 0
    me = 4 * lax.axis_index("x") + 2 * lax.axis_index("y") + lax.axis_index("c")

    gath = _exchange([given[k].astype(BF16) for k in ("s5_w_in", "s5_w_glu", "s5_w_out")]
                     + [c, hg_lower_bounds, ffn_conv_w], a2a=False, name="gather_weights")
    w_s5in, w_glu, w_s5out = (g.reshape(dm, dm) for g in gath[:3])
    c_all, lb_all, cw_all = gath[3:]
    ns_up = ffn_w_up.shape[2]
    w_up, w_dn = [None, None], [None, None]
    cols = lambda g: jnp.transpose(g, (1, 0, 2)).reshape(g.shape[1], -1)
    shards = lambda w: jnp.transpose(w.reshape(w.shape[0], NDEV, -1), (1, 0, 2))
    tn_up, tn_hg, tkr = 2 * ns_up, 2 * hg_w_in.shape[2], 1152
    assert n % tkr == 0 and n % 1024 == 0
    gather = lambda arrs: _Xchg([a.astype(BF16) for a in arrs], [False] * len(arrs))
    scatter = lambda arrs: _Xchg(arrs, [True] * len(arrs))
    cw = [cols(cw_all[:, layer]) for layer in range(2)]
    cb = [ffn_conv_b[layer].reshape(1, -1) for layer in range(2)]
    lb2 = jnp.transpose(lb_all, (1, 2, 0, 3)).reshape(2, 2, 1, dm)

    nsm = w_mod.shape[2]
    craw = jnp.concatenate([c_all.reshape(NDEV * bsz, dm), c_ctx[None], jnp.zeros((SUBLANES - 1, dm), F32)], axis=0)
    b_sh = lax.dynamic_slice(b_mod, (0, me * nsm), (2, nsm)).reshape(2, 1, nsm)
    mod_sh = _mod_fwd(craw, w_mod, b_sh)
    (mod_g,) = _exchange([mod_sh], a2a=False, name="gather_mod")
    mod_full = jnp.transpose(mod_g, (1, 2, 0, 3)).reshape(2, craw.shape[0], NDEV * nsm)
    pat = []
    for layer in range(2):
        mlat = lax.dynamic_slice(mod_full[layer], (me * bsz, 0), (bsz, N_MOD * dm))
        mctx = mod_full[layer, NDEV * bsz]
        pat.append([_pattern(mlat, mctx, m, dm) for m in range(N_MOD)])

    s5p = _s5_prep(s5_lam_re[0], s5_lam_im[0], s5_log_step[0], s5_b_re[0], s5_b_im[0], s5_c_re[0], s5_c_im[0], dm)
    dsk = s5_d.reshape(1, dm)

    z0 = jnp.concatenate([_to_tm(ctx), _to_tm(x)], axis=0)
    tgt = _to_tm(loss_target)
    n1w = [norm1_w[layer].reshape(1, dm) for layer in range(2)]
    n2w = [norm2_w[layer].reshape(1, dm) for layer in range(2)]

    def ffn_fwd(layer, h2):
        u = _lin(h2, w_up[layer], name=f"ffn_up{layer}", tm=1024, tn=tn_up)
        hm = _convffn_fwd(u, cw[layer], cb[layer], name=f"convffn_fwd{layer}", dims=dims)
        f = _lin(hm, w_dn[layer], name=f"ffn_down{layer}", tm=tm, tn=dm)
        return u, hm, f

    _, h0 = _norm_mod_fwd(z0, n1w[0], pat[0][0], pat[0][1], name="norm1_l0", dims=dims)
    u_s5 = _lin(h0, w_s5in, name="s5_in", tm=1024, tn=dm)
    (y_s5a, y_s5b, st_r, st_i), (g_up0, g_dn0, g_hgin, g_hgout, g_dn1) = _s5_fwd(
        u_s5, *s5p["af"], s5p["bb"], s5p["cc"], dsk, name="s5_fwd", dims=dims,
        xch=gather([ffn_w_up[0], ffn_w_down[0], hg_w_in[0], hg_w_out[0], ffn_w_down[1]]))
    w_up[0], w_dn[0] = cols(g_up0), g_dn0.reshape(-1, dm)
    w_hgin, w_hgout, w_dn[1] = cols(g_hgin), g_hgout.reshape(dm, dm), g_dn1.reshape(-1, dm)
    (zg,) = _rowk(lambda rv, pv, cv, il: ([_gelu(rv[0] + rv[1])], [], []), name="s5_gelu", n=n, tm=dims["tm_row"],
                  nctx=nctx, rows=[(y_s5a, dm, 0, 0), (y_s5b, dm, 0, 0)], out_rows=[(dm, BF16, dm, 0)])
    t_glu = _lin(zg, w_glu, name="s5_glu", tm=1024, tn=dm)
    (z2g,) = _rowk(lambda rv, pv, cv, il: ([rv[0] * _sigmoid(rv[1])], [], []), name="s5_gate", n=n,
                   tm=dims["tm_row"], nctx=nctx, rows=[(zg, dm, 0, 0), (t_glu, dm, 0, 0)],
                   out_rows=[(dm, BF16, dm, 0)])
    ymix0 = _lin(z2g, w_s5out, name="s5_out", tm=1024, tn=dm)
    z1_l0, h2_l0 = _norm_mod_fwd(z0, n2w[0], pat[0][3], pat[0][4], name="norm2_l0", dims=dims,
                                 res=(ymix0, pat[0][2]))
    u_l0, hm_l0, f_l0 = ffn_fwd(0, h2_l0)

    z2_l0, h1 = _norm_mod_fwd(z1_l0, n1w[1], pat[1][0], pat[1][1], name="norm1_l1", dims=dims,
                              res=(f_l0, pat[0][5]))
    zz = _lin(h1, w_hgin, name="hg_in", tm=1024, tn=tn_hg)
    (o_f, sts_f), (g_up1,) = _hg_fwd_dir(zz, lb2, d=0, name="hg_fwd_d0", dims=dims, xch=gather([ffn_w_up[1]]))
    w_up[1] = cols(g_up1)
    (o_b, sts_b), _ = _hg_fwd_dir(zz, lb2, d=1, name="hg_fwd_d1", dims=dims)
    gnw = hg_gnorm_w.reshape(1, HG_HEAD)
    (og,) = _rowk(lambda rv, pv, cv, il: ([_hg_readout(rv[0] + rv[1], rv[2], cv[0])], [], []), name="hg_readout",
                  n=n, tm=dims["tm_row"], nctx=nctx, rows=[(o_f, dm, 0, 0), (o_b, dm, 0, 0), (zz, dm, 4, 0)],
                  consts=[gnw], out_rows=[(dm, BF16, dm, 0)])
    ymix1 = _lin(og, w_hgout, name="hg_out", tm=1024, tn=dm)
    z1_l1, h2_l1 = _norm_mod_fwd(z2_l0, n2w[1], pat[1][3], pat[1][4], name="norm2_l1", dims=dims,
                                 res=(ymix1, pat[1][2]))
    u_l1, hm_l1, f_l1 = ffn_fwd(1, h2_l1)

    dz, df, dgate2_l1, loss_part, dfinal_w = _loss_bwd(z1_l1, f_l1, pat[1][5], tgt, final_norm_w.reshape(1, dm),
                                                        name="loss_bwd", dims=dims)

    def ffn_bwd(layer, df_, u, hm, h2, xch=None):
        dff = hm.shape[1]
        dhm = _lin(df_, w_dn[layer], name=f"ffn_down_bwd_in{layer}", trans_w=True, tm=1024, tn=dff // 2, o_dtype=BF16)
        dwd = _lin_w(hm, df_, name=f"ffn_down_bwd_w{layer}", ta=dff // 2, tn=dm, tkr=tkr)
        (dua, dug, dcwa, dcwg, dcba, dcbg), got = _convffn_bwd(u, dhm, cw[layer], cb[layer],
                                                               name=f"convffn_bwd{layer}", dims=dims, xch=xch)
        dh2 = _lin(dua, w_up[layer], name=f"ffn_up_bwd_in_a{layer}", trans_w=True, tm=tm, tn=dm, kblk=0)
        dh2 = _lin(dug, w_up[layer], name=f"ffn_up_bwd_in_g{layer}", trans_w=True, tm=tm, tn=dm, kblk=1, base=dh2)
        dwu = jnp.concatenate([_lin_w(h2, dua, name=f"ffn_up_bwd_w_a{layer}", ta=dm, tn=tn_up, tkr=tkr),
                               _lin_w(h2, dug, name=f"ffn_up_bwd_w_g{layer}", ta=dm, tn=tn_up, tkr=tkr)], axis=1)
        dcw = shards(jnp.concatenate([dcwa, dcwg], axis=1))
        return dh2, shards(dwu), dwd, dcw, jnp.concatenate([dcba, dcbg], axis=1), got

    dh2, dwu_l1, dwd_l1, dcw_l1, dcb_l1, _ = ffn_bwd(1, df, u_l1, hm_l1, h2_l1)
    dz, dymix, dsh2_l1, dsc2_l1, dgate1_l1, dn2w_l1 = _norm_mod_bwd(dh2, z1_l1, dz, n2w[1], pat[1][4], name="norm2_bwd_l1",
                                                                    dims=dims, res=(ymix1, pat[1][2]))
    dog = _lin(dymix, w_hgout, name="hg_out_bwd_in", trans_w=True, tm=1024, tn=dm)
    dw_hgout = _lin_w(og, dymix, name="hg_out_bwd_w", ta=dm, tn=dm, tkr=tkr)

    def readout_bwd(rv, pv, cv, il):
        _, vjp = jax.vjp(_hg_readout, rv[0] + rv[1], rv[2], cv[0])
        do, dg, dw = vjp(rv[3])
        return [do, dg], [], [jnp.broadcast_to(dw, (SUBLANES, HG_HEAD)) * (1.0 / SUBLANES)]

    do, dg, dgnw = _rowk(readout_bwd, name="hg_readout_bwd", n=n, tm=dims["tm_row"], nctx=nctx,
                         rows=[(o_f, dm, 0, 0), (o_b, dm, 0, 0), (zz, dm, 4, 0), (dog, dm, 0, 0)], consts=[gnw],
                         out_rows=[(dm, F32, dm, 0), (dm, BF16, dm, 0)], out_acc=[HG_HEAD])
    (dq0, dv0, dff, dl_f), (p_up1, p_dn1) = _hg_bwd_dir(
        zz, lb2, do, sts_f, None, d=0, name="hg_bwd_d0", dims=dims,
        xch=scatter([dwu_l1, dwd_l1.reshape(NDEV, -1, dm)]))
    (dq, dv, dfb, dl_b), (p_hgout,) = _hg_bwd_dir(
        zz, lb2, do, sts_b, (dq0, dv0), d=1, name="hg_bwd_d1", dims=dims,
        xch=scatter([dw_hgout.reshape(NDEV, -1, dm)]))
    dh1, dw_parts = None, []
    for p, piece in enumerate((dq, dv, dff, dfb, dg)):
        dh1 = _lin(piece, w_hgin, name=f"hg_in_bwd_in{p}", trans_w=True, tm=1024, tn=dm, kblk=p, base=dh1)
        dw_parts.append(_lin_w(h1, piece, name=f"hg_in_bwd_w{p}", ta=dm, tn=dm, tkr=tkr))
    dw_hgin = shards(jnp.concatenate(dw_parts, axis=1))
    dz, df0, dsh1_l1, dsc1_l1, dgate2_l0, dn1w_l1 = _norm_mod_bwd(dh1, z2_l0, dz, n1w[1], pat[1][1], name="norm1_bwd_l1",
                                                                  dims=dims, res=(f_l0, pat[0][5]))
    dh2, dwu_l0, dwd_l0, dcw_l0, dcb_l0, _ = ffn_bwd(0, df0, u_l0, hm_l0, h2_l0)
    dz, dymix, dsh2_l0, dsc2_l0, dgate1_l0, dn2w_l0 = _norm_mod_bwd(dh2, z1_l0, dz, n2w[0], pat[0][4], name="norm2_bwd_l0",
                                                                    dims=dims, res=(ymix0, pat[0][2]))
    dz2g = _lin(dymix, w_s5out, name="s5_out_bwd_in", trans_w=True, tm=1024, tn=dm)
    dw_s5out = _lin_w(z2g, dymix, name="s5_out_bwd_w", ta=dm, tn=dm, tkr=tkr)

    def gate_bwd(rv, pv, cv, il):
        sg = _sigmoid(rv[1])
        return [rv[2] * rv[0] * sg * (1.0 - sg), rv[2] * sg], [], []

    dt_glu, dzg_a = _rowk(gate_bwd, name="s5_gate_bwd", n=n, tm=dims["tm_row"], nctx=nctx,
                          rows=[(zg, dm, 0, 0), (t_glu, dm, 0, 0), (dz2g, dm, 0, 0)],
                          out_rows=[(dm, BF16, dm, 0), (dm, F32, dm, 0)])
    dzg_b = _lin(dt_glu, w_glu, name="s5_glu_bwd_in", trans_w=True, tm=1024, tn=dm)
    dw_glu = _lin_w(zg, dt_glu, name="s5_glu_bwd_w", ta=dm, tn=dm, tkr=tkr)

    def gelu_bwd(rv, pv, cv, il):
        _, vjp = jax.vjp(_gelu, rv[0] + rv[1])
        return [vjp(rv[2] + rv[3])[0]], [], []

    (dy_s5,) = _rowk(gelu_bwd, name="s5_gelu_bwd", n=n, tm=dims["tm_row"], nctx=nctx,
                     rows=[(y_s5a, dm, 0, 0), (y_s5b, dm, 0, 0), (dzg_a, dm, 0, 0), (dzg_b, dm, 0, 0)],
                     out_rows=[(dm, F32, dm, 0)])
    dcw_both = jnp.stack([dcw_l0, dcw_l1], axis=1)
    s5g, (p_up0, p_dn0, p_cw, p_s5out, p_glu, p_hgin) = _s5_bwd(
        u_s5, dy_s5, *s5p["af"], *s5p["ab"], s5p["bb"], s5p["cc"], dsk, st_r, st_i, name="s5_bwd", dims=dims,
        xch=scatter([dwu_l0, dwd_l0.reshape(NDEV, -1, dm), dcw_both, dw_s5out.reshape(NDEV, -1, dm),
                     dw_glu.reshape(NDEV, -1, dm), dw_hgin]))
    (du_s5,) = _rowk(lambda rv, pv, cv, il: ([rv[0] + rv[1]], [], []), name="s5_du", n=n, tm=dims["tm_row"], nctx=nctx,
                     rows=[(s5g[0], dm, 0, 0), (s5g[1], dm, 0, 0)], out_rows=[(dm, BF16, dm, 0)])
    ddsk = s5g[14]
    dh0 = _lin(du_s5, w_s5in, name="s5_in_bwd_in", trans_w=True, tm=1024, tn=dm)
    dw_s5in = _lin_w(h0, du_s5, name="s5_in_bwd_w", ta=dm, tn=dm, tkr=tkr)
    dz0, dsh1_l0, dsc1_l0, dn1w_l0 = _norm_mod_bwd(dh0, z0, dz, n1w[0], pat[0][1], name="norm1_bwd_l0", dims=dims)

    dlr, dli, dls, g_b_re, g_b_im, g_c_re, g_c_im = _s5_param_grads(s5p, s5g[2:6], s5g[6:10], s5g[10:14])

    dmod = jnp.stack([
        jnp.concatenate([dsh1_l0, dsc1_l0, dgate1_l0, dsh2_l0, dsc2_l0, dgate2_l0], axis=1),
        jnp.concatenate([dsh1_l1, dsc1_l1, dgate1_l1, dsh2_l1, dsc2_l1, dgate2_l1], axis=1)])
    dl_hg = jnp.stack([dl_f[:, 0], dl_b[:, 0]])
    wide = lambda g: g.reshape(-1, dm)
    small = [("norm1_w", jnp.concatenate([dn1w_l0, dn1w_l1])),
             ("norm2_w", jnp.concatenate([dn2w_l0, dn2w_l1])), ("final_norm_w", dfinal_w),
             ("s5_lam_re", dlr.reshape(-1, nstate)), ("s5_lam_im", dli.reshape(-1, nstate)),
             ("s5_log_step", dls.reshape(2, ngrp)),
             ("s5_b_re", wide(g_b_re.astype(BF16))), ("s5_b_im", wide(g_b_im.astype(BF16))),
             ("s5_c_re", wide(g_c_re.astype(BF16))), ("s5_c_im", wide(g_c_im.astype(BF16))), ("s5_d", ddsk),
             ("hg_gnorm_w", dgnw), ("ffn_conv_b", jnp.stack([dcb_l0.reshape(-1), dcb_l1.reshape(-1)]))]
    tail = _exchange([dmod, dw_s5in.reshape(NDEV, -1, dm)] + [g for _, g in small] + [wide(dl_hg), loss_part],
                     a2a=[False, True] + [False] * (len(small) + 2), name="gather_tail")
    dmod_g, p_s5in, gathered = tail[0], tail[1], tail[2:]
    dlat_full = jnp.transpose(dmod_g[:, :, :bsz], (1, 0, 2, 3)).reshape(2, NDEV * bsz, N_MOD * dm)
    dctx_full = dmod_g[:, :, bsz]
    dlat_sh = lax.dynamic_slice(dlat_full, (0, 0, me * nsm), (2, NDEV * bsz, nsm))
    dctx_sh = lax.dynamic_slice(dctx_full, (0, 0, me * nsm), (NDEV, 2, nsm))
    g_w_mod, g_b_mod, dcctx8 = _mod_bwd(craw, w_mod, dlat_sh, dctx_sh, dlat_full, dctx_full)

    (g_cctx,) = _exchange([wide(dcctx8[:1])], a2a=False, name="gather_cctx")
    small, gathered = [("c_ctx", wide(dcctx8[:1]))] + small, [g_cctx] + gathered
    res = {}
    for (k, g), parts in zip(small, gathered):
        w2, m2, v2 = (given[p + k].reshape(g.shape) for p in ("", "m_", "v_"))
        res[k] = tuple(o.reshape(given[k].shape) for o in _adamw(parts, w2, m2, v2, name="adamw_" + k))

    def total(parts, name):
        z = jnp.zeros(parts.shape[1:], F32)
        return _adamw(parts, z, z, z, name=name)[0]

    loss = jnp.sum(total(gathered[-1], "sum_loss"))
    dl_tot = total(gathered[-2], "sum_dlb").reshape(dl_hg.shape)
    nlb = hg_lower_bounds.shape[2]
    g_lb = lax.dynamic_slice(dl_tot, (0, 0, me * nlb), (2, 2, nlb))

    def adam_local(name, g, shape2):
        w, m, v = given[name], given["m_" + name], given["v_" + name]
        out = _adamw(g.reshape((1,) + shape2), w.reshape(shape2), m.reshape(shape2), v.reshape(shape2),
                     name="adamw_" + name)
        return tuple(o.reshape(w.shape) for o in out)

    def adam_parts(name, p):
        w, m, v = given[name], given["m_" + name], given["v_" + name]
        shape2 = (p.shape[0], -1, w.shape[-1])
        p3 = p.reshape(shape2)
        s2 = p3.shape[1:]
        out = _adamw(p3, w.reshape(s2), m.reshape(s2), v.reshape(s2), name="adamw_" + name)
        return tuple(o.reshape(w.shape) for o in out)

    res["hg_lower_bounds"] = adam_local("hg_lower_bounds", g_lb, (2 * 2, nlb))
    res["w_mod"] = adam_local("w_mod", g_w_mod, (2 * dm, nsm))
    res["b_mod"] = adam_local("b_mod", g_b_mod, (2, N_MOD * dm))
    res["s5_w_in"] = adam_parts("s5_w_in", p_s5in)
    res["s5_w_glu"] = adam_parts("s5_w_glu", p_glu)
    res["s5_w_out"] = adam_parts("s5_w_out", p_s5out)
    res["hg_w_in"] = adam_parts("hg_w_in", p_hgin)
    res["hg_w_out"] = adam_parts("hg_w_out", p_hgout)
    res["ffn_w_up"] = adam_parts("ffn_w_up", jnp.stack([p_up0, p_up1], axis=1))
    res["ffn_w_down"] = adam_parts("ffn_w_down", jnp.stack([p_dn0, p_dn1], axis=1))
    res["ffn_conv_w"] = adam_parts("ffn_conv_w", p_cw)

    grad_x = jnp.transpose(dz0[nctx:].reshape(lx, bsz, dm), (1, 0, 2))
    order = ["c_ctx", "w_mod", "b_mod", "norm1_w", "norm2_w", "final_norm_w", "s5_w_in", "s5_lam_re", "s5_lam_im",
             "s5_log_step", "s5_b_re", "s5_b_im", "s5_c_re", "s5_c_im", "s5_d", "s5_w_glu", "s5_w_out", "hg_w_in",
             "hg_lower_bounds", "hg_gnorm_w", "hg_w_out", "ffn_w_up", "ffn_conv_w", "ffn_conv_b", "ffn_w_down"]
    outs = [loss, grad_x]
    for j in range(4):
        outs += [res[k][j].reshape(given[k].shape) for k in order]
    return tuple(outs)
```

```python
import functools

import jax
import jax.numpy as jnp
from jax import lax
from jax.experimental import pallas as pl
from jax.experimental.pallas import tpu as pltpu

F32 = jnp.float32
BF16 = jnp.bfloat16
NDEV = 8
LOCAL_B = 4
NORM_EPS = 1e-6
N_MOD = 6
S5_GROUP = 16
S5_STATE = 64
S5_LAM_RE_MAX = -1e-4
HG_HEAD = 128
HG_ROWS = 128
GRID_W = 64
ADAM_LR, ADAM_B1, ADAM_B2, ADAM_EPS, ADAM_WD, ADAM_STEP = 0.001, 0.9, 0.999, 1e-08, 0.01, 10
VMEM_BYTES_V7X = 64 * 1024 * 1024
LANES = 128
SUBLANES = 8

NN = (((1,), (0,)), ((), ()))
NT = (((1,), (1,)), ((), ()))
TN = (((0,), (0,)), ((), ()))
MESH = pl.DeviceIdType.MESH


def _params(sem=None, vmem=None):
    kw = {}
    if sem is not None:
        kw["dimension_semantics"] = sem
    if vmem is not None:
        kw["vmem_limit_bytes"] = int(min(vmem, VMEM_BYTES_V7X - (4 << 20)))
    return pltpu.CompilerParams(**kw)


def _nbytes(shape, dtype):
    n = 1
    for s in shape:
        n *= 1 if s is None else s
    return n * jnp.dtype(dtype).itemsize


def _dot(a, b, dims=NN, precision=None):
    return lax.dot_general(a, b, dims, preferred_element_type=F32, precision=precision)


def _sigmoid(x):
    return 1.0 / (1.0 + jnp.exp(-x))


class _Xchg:
    def __init__(self, arrs, a2a):
        self.arrs, self.a2a, self.n = list(arrs), list(a2a), len(arrs)

    def out_shape(self):
        return [jax.ShapeDtypeStruct(a.shape if f else (NDEV,) + a.shape, a.dtype) for a, f in zip(self.arrs, self.a2a)]

    def scratch(self):
        return [pltpu.SemaphoreType.DMA((self.n * (NDEV - 1),)), pltpu.SemaphoreType.DMA((self.n * (NDEV - 1),)),
                pltpu.SemaphoreType.DMA((self.n,))]

    def _copies(self, ins, outs, sems, with_recvs):
        send_sems, recv_sems, loc_sems = sems
        x, y, c = lax.axis_index("x"), lax.axis_index("y"), lax.axis_index("c")
        me = 4 * x + 2 * y + c
        local, sends, recvs = [], [], []
        for a in range(self.n):
            src = ins[a].at[me] if self.a2a[a] else ins[a]
            local.append(pltpu.make_async_copy(src, outs[a].at[me], loc_sems.at[a]))
            for k in range(1, NDEV):
                px = (1 - x) if (k >> 2) & 1 else x
                py = (1 - y) if (k >> 1) & 1 else y
                pc = (1 - c) if k & 1 else c
                p = 4 * px + 2 * py + pc
                s = a * (NDEV - 1) + k - 1
                src = ins[a].at[p] if self.a2a[a] else ins[a]
                kw = dict(src_ref=src, send_sem=send_sems.at[s], recv_sem=recv_sems.at[s], device_id=(px, py, pc),
                          device_id_type=MESH)
                sends.append(pltpu.make_async_remote_copy(dst_ref=outs[a].at[me], **kw))
                if with_recvs:
                    recvs.append(pltpu.make_async_remote_copy(dst_ref=outs[a].at[p], **kw))
        return local, sends, recvs

    def start(self, ins, outs, sems):
        local, sends, _ = self._copies(ins, outs, sems, False)
        for cp in local + sends:
            cp.start()

    def wait(self, ins, outs, sems):
        local, sends, recvs = self._copies(ins, outs, sems, True)
        for cp in sends:
            cp.wait_send()
        for cp in recvs:
            cp.wait_recv()
        for cp in local:
            cp.wait()


def _exchange(arrs, *, a2a, name):
    xch = _Xchg(arrs, a2a if isinstance(a2a, (list, tuple)) else [a2a] * len(arrs))
    n = xch.n

    def body(*refs):
        xch.start(refs[:n], refs[n:2 * n], refs[2 * n:])
        xch.wait(refs[:n], refs[n:2 * n], refs[2 * n:])

    res = pl.pallas_call(
        body, name=name, out_shape=xch.out_shape(),
        in_specs=[pl.BlockSpec(memory_space=pl.ANY)] * n, out_specs=[pl.BlockSpec(memory_space=pl.ANY)] * n,
        scratch_shapes=xch.scratch(),
    )(*arrs)
    return list(res)


def _call(body, *, name, out_shape, grid, in_specs, out_specs, scratch, params, args, xch=None):
    in_specs, out_specs, out_shape, scratch, args = list(in_specs), list(out_specs), list(out_shape), list(scratch), list(args)
    n_in, n_out, n_scr = len(in_specs), len(out_shape), len(scratch)
    if xch is not None:
        k = xch.n
        inner = body

        def body(*refs):
            ins, xin = refs[:n_in], refs[n_in:n_in + k]
            outs, xout = refs[n_in + k:n_in + k + n_out], refs[n_in + k + n_out:n_in + 2 * k + n_out]
            scr = refs[n_in + 2 * k + n_out:n_in + 2 * k + n_out + n_scr]
            sems = refs[n_in + 2 * k + n_out + n_scr:]
            first = pl.program_id(0) == 0
            last = pl.program_id(0) == grid[0] - 1
            for ax in range(1, len(grid)):
                first = jnp.logical_and(first, pl.program_id(ax) == 0)
                last = jnp.logical_and(last, pl.program_id(ax) == grid[ax] - 1)

            @pl.when(first)
            def _():
                xch.start(xin, xout, sems)

            inner(*ins, *outs, *scr)

            @pl.when(last)
            def _():
                xch.wait(xin, xout, sems)

        anyspec = pl.BlockSpec(memory_space=pl.ANY)
        in_specs += [anyspec] * k
        out_specs += [anyspec] * k
        out_shape += xch.out_shape()
        scratch += xch.scratch()
        args += xch.arrs
    res = pl.pallas_call(body, name=name, out_shape=out_shape, grid=grid, in_specs=in_specs, out_specs=out_specs,
                         scratch_shapes=scratch, compiler_params=params)(*args)
    return list(res[:n_out]), list(res[n_out:])


def _mm(a, b, *, name, grid, a_spec, b_spec, o_spec, o_shape, o_dtype, dims, base=None):
    nk = grid[2]
    o_block = tuple(s for s in o_spec.block_shape if s is not None)

    def body(a_ref, b_ref, *rest):
        base_ref = rest[0] if base is not None else None
        o_ref, scr = rest[1 if base is not None else 0], rest[2 if base is not None else 1:]
        r = _dot(a_ref[...].astype(BF16), b_ref[...].astype(BF16), dims)
        if nk == 1:
            if base is not None:
                r = r + base_ref[...].astype(F32)
            o_ref[...] = r.astype(o_dtype)
        else:
            acc = scr[0]
            k = pl.program_id(2)

            @pl.when(k == 0)
            def _():
                acc[...] = r

            @pl.when(k > 0)
            def _():
                acc[...] += r

            @pl.when(k == nk - 1)
            def _():
                tot = acc[...] if base is None else acc[...] + base_ref[...].astype(F32)
                o_ref[...] = tot.astype(o_dtype)

    blocks = (_nbytes(a_spec.block_shape, a.dtype) + _nbytes(b_spec.block_shape, b.dtype) + _nbytes(o_block, o_dtype)
              + (_nbytes(o_block, base.dtype) if base is not None else 0))
    scratch = [pltpu.VMEM(o_block, F32)] if nk > 1 else []
    vmem = 2 * blocks + 3 * _nbytes(o_block, F32) + (8 << 20)
    return pl.pallas_call(
        body, name=name, out_shape=jax.ShapeDtypeStruct(o_shape, o_dtype), grid=grid,
        in_specs=[a_spec, b_spec] + ([o_spec] if base is not None else []), out_specs=o_spec, scratch_shapes=scratch,
        compiler_params=_params(("parallel", "parallel", "arbitrary"), vmem),
    )(a, b, *([base] if base is not None else []))


def _lin(a, w, *, name, trans_w=False, tm, tn, o_dtype=F32, kblk=0, base=None):
    m, kk = a.shape
    nout = w.shape[0] if trans_w else w.shape[1]
    if trans_w:
        b_spec = pl.BlockSpec((tn, kk), lambda j, i, k: (j, kblk))
    else:
        b_spec = pl.BlockSpec((kk, tn), lambda j, i, k: (0, j))
    return _mm(a, w, name=name, grid=(nout // tn, m // tm, 1), dims=NT if trans_w else NN, o_shape=(m, nout),
               o_dtype=o_dtype, o_spec=pl.BlockSpec((tm, tn), lambda j, i, k: (i, j)),
               a_spec=pl.BlockSpec((tm, kk), lambda j, i, k: (i, 0)), b_spec=b_spec, base=base)


def _lin_w(a, dy, *, name, ta, tn, tkr):
    m, ka = a.shape
    nout = dy.shape[1]
    return _mm(a, dy, name=name, grid=(ka // ta, nout // tn, m // tkr), dims=TN, o_shape=(ka, nout), o_dtype=BF16,
               o_spec=pl.BlockSpec((ta, tn), lambda i, j, k: (i, j)),
               a_spec=pl.BlockSpec((tkr, ta), lambda i, j, k: (k, i)),
               b_spec=pl.BlockSpec((tkr, tn), lambda i, j, k: (k, j)))


def _rowk(fn, *, name, n, tm, nctx, rows=(), pats=(), consts=(), out_rows=(), out_seg=(), out_acc=()):
    nb, ncb = n // tm, nctx // tm
    nr, npat, ncst = len(rows), len(pats), len(consts)
    no, nseg, nacc = len(out_rows), len(out_seg), len(out_acc)
    in_specs, blocks = [], 0
    for arr, w, cb, off in rows:
        in_specs.append(pl.BlockSpec((tm, w), lambda i, cb=cb, off=off: (jnp.maximum(i - off, 0), cb)))
        blocks += _nbytes((tm, w), arr.dtype)
    for p in pats:
        in_specs.append(pl.BlockSpec((None, SUBLANES, p.shape[2]), lambda i: (jnp.where(i >= ncb, 1, 0), 0, 0)))
    for cst in consts:
        in_specs.append(pl.BlockSpec(cst.shape, lambda i: (0, 0)))
    out_shape, out_specs = [], []
    for wt, dt, w, cb in out_rows:
        out_shape.append(jax.ShapeDtypeStruct((n, wt), dt))
        out_specs.append(pl.BlockSpec((tm, w), lambda i, cb=cb: (i, cb)))
        blocks += _nbytes((tm, w), dt)
    for w in out_seg:
        out_shape.append(jax.ShapeDtypeStruct((SUBLANES, w), F32))
        out_specs.append(pl.BlockSpec((SUBLANES, w), lambda i: (0, 0)))
    for w in out_acc:
        out_shape.append(jax.ShapeDtypeStruct((1, w), F32))
        out_specs.append(pl.BlockSpec((1, w), lambda i: (0, 0)))
    scratch = [pltpu.VMEM((2, SUBLANES, w), F32) for w in out_seg] + [pltpu.VMEM((SUBLANES, w), F32) for w in out_acc]

    def body(*refs):
        r_in = refs[:nr]
        p_in = refs[nr:nr + npat]
        c_in = refs[nr + npat:nr + npat + ncst]
        base = nr + npat + ncst
        o_rows = refs[base:base + no]
        o_seg = refs[base + no:base + no + nseg]
        o_acc = refs[base + no + nseg:base + no + nseg + nacc]
        s_seg = refs[base + no + nseg + nacc:base + no + nseg + nacc + nseg]
        s_acc = refs[base + no + nseg + nacc + nseg:]
        i = pl.program_id(0)
        rv = [r[...].astype(F32).reshape(tm // SUBLANES, SUBLANES, r.shape[1]) for r in r_in]
        pv = [p[...] for p in p_in]
        cv = [c[...] for c in c_in]
        is_lat = (i >= ncb).astype(F32)
        ro, so, ao = fn(rv, pv, cv, is_lat)
        for ref, val in zip(o_rows, ro):
            ref[...] = val.reshape(tm, ref.shape[1]).astype(ref.dtype)
        if nseg or nacc:
            @pl.when(i == 0)
            def _():
                for s in list(s_seg) + list(s_acc):
                    s[...] = jnp.zeros(s.shape, F32)

            seg = jnp.where(i >= ncb, 1, 0)
            for s, val in zip(s_seg, so):
                s[seg] = s[seg] + val
            for s, val in zip(s_acc, ao):
                s[...] = s[...] + val

            @pl.when(i == nb - 1)
            def _():
                for o, s in zip(o_seg, s_seg):
                    lat, ctx = s[1], s[0]
                    row = lax.broadcasted_iota(jnp.int32, lat.shape, 0)
                    lat = lat + pltpu.roll(lat, 4, 0)
                    ctx = jnp.broadcast_to(jnp.sum(ctx, axis=0, keepdims=True), lat.shape)
                    o[...] = jnp.where(row < 4, lat, jnp.where(row == 4, ctx, 0.0))
                for o, s in zip(o_acc, s_acc):
                    o[...] = jnp.sum(s[...], axis=0, keepdims=True)

    vmem = 2 * blocks + 8 * tm * 1024 * 4 + (8 << 20)
    res = pl.pallas_call(
        body, name=name, out_shape=out_shape, grid=(nb,), in_specs=in_specs, out_specs=out_specs,
        scratch_shapes=scratch, compiler_params=_params(("arbitrary",), vmem),
    )(*[r[0] for r in rows], *pats, *consts)
    return list(res)


def _rms(z):
    return lax.rsqrt(jnp.mean(z * z, axis=-1, keepdims=True) + NORM_EPS)


def _norm_mod_fwd(z, w, sh, sc, *, name, dims, res=None):
    n, d = z.shape

    def fn(rv, pv, cv, is_lat):
        zz = rv[0]
        if res is not None:
            zz = zz + pv[2][None] * rv[1]
        h = (zz * _rms(zz) * cv[0]) * (1.0 + pv[1][None]) + pv[0][None]
        return ([zz, h] if res is not None else [h]), [], []

    rows = [(z, d, 0, 0)] + ([(res[0], d, 0, 0)] if res is not None else [])
    pats = [sh, sc] + ([res[1]] if res is not None else [])
    outs = ([(d, F32, d, 0)] if res is not None else []) + [(d, BF16, d, 0)]
    out = _rowk(fn, name=name, n=n, tm=dims["tm_row"], nctx=dims["nctx"], rows=rows, pats=pats, consts=[w],
                out_rows=outs)
    return (out[0], out[1]) if res is not None else (None, out[0])


def _norm_core_bwd(zin, dh, w, sc):
    r = _rms(zin)
    xh = zin * r
    dsh = jnp.sum(dh, axis=0)
    dsc = jnp.sum(dh * (xh * w), axis=0)
    dyv = dh * (1.0 + sc[None])
    dw = jnp.sum(dyv * xh, axis=0)
    dxh = dyv * w
    dx = r * (dxh - xh * jnp.mean(dxh * xh, axis=-1, keepdims=True))
    return dx, dsh, dsc, dw


def _norm_mod_bwd(dh, zin, dz_up, w, sc, *, name, dims, res=None):
    n, d = zin.shape

    def fn(rv, pv, cv, is_lat):
        dx, dsh, dsc, dw = _norm_core_bwd(rv[1], rv[0], cv[0], pv[0])
        dz = rv[2] + dx
        if res is None:
            return [dz], [dsh, dsc], [dw]
        return [dz, dz * pv[1][None]], [dsh, dsc, jnp.sum(dz * rv[3], axis=0)], [dw]

    rows = [(dh, d, 0, 0), (zin, d, 0, 0), (dz_up, d, 0, 0)] + ([(res[0], d, 0, 0)] if res is not None else [])
    pats = [sc] + ([res[1]] if res is not None else [])
    outs = [(d, F32, d, 0)] + ([(d, BF16, d, 0)] if res is not None else [])
    return _rowk(fn, name=name, n=n, tm=dims["tm_row"], nctx=dims["nctx"], rows=rows, pats=pats, consts=[w],
                 out_rows=outs, out_seg=[d] * (3 if res is not None else 2), out_acc=[d])


def _loss_bwd(z1, f, gate, tgt, w, *, name, dims):
    n, d = z1.shape

    def fn(rv, pv, cv, is_lat):
        z2 = rv[0] + pv[0][None] * rv[1]
        r = _rms(z2)
        xh = z2 * r
        err = (xh * cv[0] - rv[2]) * is_lat
        dout = err * (1.0 / d)
        dxh = dout * cv[0]
        dz = r * (dxh - xh * jnp.mean(dxh * xh, axis=-1, keepdims=True))
        return ([dz, dz * pv[0][None]], [jnp.sum(dz * rv[1], axis=0)],
                [jnp.sum(0.5 * err * err * (1.0 / d), axis=0), jnp.sum(dout * xh, axis=0)])

    tm = dims["tm_row"]
    rows = [(z1, d, 0, 0), (f, d, 0, 0), (tgt, d, 0, dims["nctx"] // tm)]
    return _rowk(fn, name=name, n=n, tm=tm, nctx=dims["nctx"], rows=rows, pats=[gate], consts=[w],
                 out_rows=[(d, F32, d, 0), (d, BF16, d, 0)], out_seg=[d], out_acc=[d, d])


def _gelu(y):
    return jax.nn.gelu(y, approximate=True)


def _conv_masks(tb, i):
    tok = lax.broadcasted_iota(jnp.int32, (tb, 1), 0) >> 2
    last = jnp.where(i == 0, tb // LOCAL_B - 1, GRID_W - 1)
    wpos = tok & last
    return wpos == 0, wpos == last


CONV_LANES = 2 * LANES


def _conv_taps(u_ref, cw_ref, cb_ref, no_left, no_right, tb):
    uu = u_ref[...]
    ul = jnp.where(no_left, 0.0, pltpu.roll(uu, LOCAL_B, 0))
    ur = jnp.where(no_right, 0.0, pltpu.roll(uu, tb - LOCAL_B, 0))
    val = cb_ref[...] + ul * cw_ref[pl.ds(0, 1), :] + uu * cw_ref[pl.ds(1, 1), :] + ur * cw_ref[pl.ds(2, 1), :]
    return val, ul, uu, ur


def _convffn_specs(tb, nj):
    cl = CONV_LANES
    return [pl.BlockSpec((tb, cl), lambda j, i: (i, j)), pl.BlockSpec((tb, cl), lambda j, i: (i, nj + j)),
            pl.BlockSpec((3, cl), lambda j, i: (0, j)), pl.BlockSpec((3, cl), lambda j, i: (0, nj + j)),
            pl.BlockSpec((1, cl), lambda j, i: (0, j)), pl.BlockSpec((1, cl), lambda j, i: (0, nj + j))]


def _convffn_fwd(u, cw, cb, *, name, dims):
    n, f2 = u.shape
    tb, nj = dims["nctx"], f2 // 2 // CONV_LANES

    def body(ua_ref, ug_ref, cwa_ref, cwg_ref, cba_ref, cbg_ref, o_ref):
        no_left, no_right = _conv_masks(tb, pl.program_id(1))
        a = _conv_taps(ua_ref, cwa_ref, cba_ref, no_left, no_right, tb)[0]
        g = _conv_taps(ug_ref, cwg_ref, cbg_ref, no_left, no_right, tb)[0]
        o_ref[...] = (a * _sigmoid(a) * g).astype(BF16)

    vmem = 16 * tb * CONV_LANES * 4 + (8 << 20)
    return pl.pallas_call(
        body, name=name, out_shape=jax.ShapeDtypeStruct((n, f2 // 2), BF16), grid=(nj, n // tb),
        in_specs=_convffn_specs(tb, nj), out_specs=pl.BlockSpec((tb, CONV_LANES), lambda j, i: (i, j)),
        compiler_params=_params(("parallel", "arbitrary"), vmem),
    )(u, u, cw, cw, cb, cb)


def _convffn_bwd(u, dhm, cw, cb, *, name, dims, xch=None):
    n, f2 = u.shape
    tb, nj = dims["nctx"], f2 // 2 // CONV_LANES

    def body(ua_ref, ug_ref, cwa_ref, cwg_ref, cba_ref, cbg_ref, dh_ref, dua_ref, dug_ref, dcwa_ref, dcwg_ref, dcba_ref,
             dcbg_ref):
        i = pl.program_id(1)
        no_left, no_right = _conv_masks(tb, i)

        @pl.when(i == 0)
        def _():
            for ref in (dcwa_ref, dcwg_ref, dcba_ref, dcbg_ref):
                ref[...] = jnp.zeros(ref.shape, F32)

        a, al, ac, ar = _conv_taps(ua_ref, cwa_ref, cba_ref, no_left, no_right, tb)
        g, gl, gc, gr = _conv_taps(ug_ref, cwg_ref, cbg_ref, no_left, no_right, tb)
        dh = dh_ref[...].astype(F32)
        sa = _sigmoid(a)
        dg = dh * (a * sa)
        da = dh * g * (sa * (1.0 + a * (1.0 - sa)))
        for dc, (tl, tc, tr), cw_ref, du_ref, dcw_ref, dcb_ref in (
                (da, (al, ac, ar), cwa_ref, dua_ref, dcwa_ref, dcba_ref),
                (dg, (gl, gc, gr), cwg_ref, dug_ref, dcwg_ref, dcbg_ref)):
            dcb_ref[...] += jnp.sum(dc, axis=0, keepdims=True)
            dcw_ref[pl.ds(0, 1), :] += jnp.sum(dc * tl, axis=0, keepdims=True)
            dcw_ref[pl.ds(1, 1), :] += jnp.sum(dc * tc, axis=0, keepdims=True)
            dcw_ref[pl.ds(2, 1), :] += jnp.sum(dc * tr, axis=0, keepdims=True)
            du = (dc * cw_ref[pl.ds(1, 1), :]
                  + pltpu.roll(jnp.where(no_left, 0.0, dc) * cw_ref[pl.ds(0, 1), :], tb - LOCAL_B, 0)
                  + pltpu.roll(jnp.where(no_right, 0.0, dc) * cw_ref[pl.ds(2, 1), :], LOCAL_B, 0))
            du_ref[...] = du.astype(BF16)

    cl, f = CONV_LANES, f2 // 2
    sd = jax.ShapeDtypeStruct
    row = pl.BlockSpec((tb, cl), lambda j, i: (i, j))
    vmem = 24 * tb * cl * 4 + (8 << 20)
    return _call(
        body, name=name, xch=xch, args=[u, u, cw, cw, cb, cb, dhm], scratch=[],
        out_shape=[sd((n, f), BF16), sd((n, f), BF16), sd((3, f), F32), sd((3, f), F32), sd((1, f), F32), sd((1, f), F32)],
        grid=(nj, n // tb), in_specs=_convffn_specs(tb, nj) + [row],
        out_specs=[row, row, pl.BlockSpec((3, cl), lambda j, i: (0, j)), pl.BlockSpec((3, cl), lambda j, i: (0, j)),
                   pl.BlockSpec((1, cl), lambda j, i: (0, j)), pl.BlockSpec((1, cl), lambda j, i: (0, j))],
        params=_params(("arbitrary", "arbitrary"), vmem))


def _s5_disc(lr, li, ls, brt, bit):
    lr = jnp.minimum(lr, S5_LAM_RE_MAX)
    dt = jnp.exp(ls)
    mag = jnp.exp(lr * dt)
    ar = mag * jnp.cos(li * dt)
    ai = mag * jnp.sin(li * dt)
    den = lr * lr + li * li
    nr = ar - 1.0
    cr = (nr * lr + ai * li) / den
    ci = (ai * lr - nr * li) / den
    return ar, ai, cr * brt - ci * bit, cr * bit + ci * brt


def _s5_disc_fwd(lr, li, ls, brt, bit):
    def body(lr_ref, li_ref, ls_ref, br_ref, bi_ref, ar_ref, ai_ref, bbr_ref, bbi_ref):
        ar, ai, bbr, bbi = _s5_disc(lr_ref[...], li_ref[...], ls_ref[...], br_ref[...], bi_ref[...])
        ar_ref[...] = ar
        ai_ref[...] = ai
        bbr_ref[...] = bbr
        bbi_ref[...] = bbi

    sd = jax.ShapeDtypeStruct
    return pl.pallas_call(body, name="s5_disc_fwd",
                          out_shape=[sd(lr.shape, F32), sd(lr.shape, F32), sd(brt.shape, F32), sd(brt.shape, F32)],
                          compiler_params=_params(None, 32 << 20))(lr, li, ls, brt, bit)


def _s5_disc_bwd(lr, li, ls, brt, bit, dar, dai, dbbr, dbbi):
    def body(lr_ref, li_ref, ls_ref, br_ref, bi_ref, dar_ref, dai_ref, dbbr_ref, dbbi_ref,
             dlr_ref, dli_ref, dls_ref, dbr_ref, dbi_ref):
        _, vjp = jax.vjp(_s5_disc, lr_ref[...], li_ref[...], ls_ref[...], br_ref[...], bi_ref[...])
        dlr, dli, dls, dbr, dbi = vjp((dar_ref[...], dai_ref[...], dbbr_ref[...], dbbi_ref[...]))
        dlr_ref[...] = dlr
        dli_ref[...] = dli
        dls_ref[...] = dls
        dbr_ref[...] = dbr
        dbi_ref[...] = dbi

    sd = jax.ShapeDtypeStruct
    return pl.pallas_call(body, name="s5_disc_bwd",
                          out_shape=[sd(lr.shape, F32), sd(lr.shape, F32), sd(ls.shape, F32), sd(brt.shape, F32),
                                     sd(brt.shape, F32)],
                          compiler_params=_params(None, 48 << 20))(lr, li, ls, brt, bit, dar, dai, dbbr, dbbi)


def _cmul(ar, ai, xr, xi):
    return ar * xr - ai * xi, ar * xi + ai * xr


def _s5_chunk_of(step, ncc, nc, rev):
    if not rev:
        return step
    return jnp.where(step < ncc, ncc - 1 - step, nc - 1 - (step - ncc))


def _s5_scan2(asc, desc, row0, nrows, a_r_ref, a_i_ref, cr_ref, ci_ref, *, lane_block, extra=None):
    width = asc[0].shape[1]
    nt = nrows // SUBLANES
    for lb in range(width // lane_block):
        lanes = pl.ds(lb * lane_block, lane_block)
        a1r, a1i = a_r_ref[:, lanes], a_i_ref[:, lanes]
        a2r, a2i = pltpu.roll(a1r, 4, 0), pltpu.roll(a1i, 4, 0)
        lo = lax.broadcasted_iota(jnp.int32, a1r.shape, 0) < 4

        def step(t, carry):
            pr, pi = carry[0], carry[1]
            ra = pl.ds(pl.multiple_of(row0 + t * SUBLANES, SUBLANES), SUBLANES)
            rd = pl.ds(pl.multiple_of(row0 + (nt - 1 - t) * SUBLANES, SUBLANES), SUBLANES)
            ur, ui = asc[0][ra, lanes], asc[1][ra, lanes]
            dr, di = desc[0][rd, lanes], desc[1][rd, lanes]
            mr, mi = _cmul(a1r, a1i, pr, pi)
            y1r, y1i = jnp.where(lo, ur, dr) + mr, jnp.where(lo, ui, di) + mi
            mr, mi = _cmul(a2r, a2i, pltpu.roll(y1r, 4, 0), pltpu.roll(y1i, 4, 0))
            y2r, y2i = jnp.where(lo, dr, ur) + mr, jnp.where(lo, di, ui) + mi
            our, oui = jnp.where(lo, y1r, y2r), jnp.where(lo, y1i, y2i)
            odr, odi = jnp.where(lo, y2r, y1r), jnp.where(lo, y2i, y1i)
            asc[0][ra, lanes] = our
            asc[1][ra, lanes] = oui
            desc[0][rd, lanes] = odr
            desc[1][rd, lanes] = odi
            nxt = (pltpu.roll(y2r, 4, 0), pltpu.roll(y2i, 4, 0))
            if extra is None:
                return nxt
            return nxt + tuple(extra(t, nt - 1 - t, lanes, (our, oui), (odr, odi), carry[2:]))

        init = (cr_ref[:, lanes], ci_ref[:, lanes])
        if extra is not None:
            init = init + tuple(extra.init(lanes))
        out = lax.fori_loop(0, nt, step, init)
        cr_ref[:, lanes] = out[0]
        ci_ref[:, lanes] = out[1]
        if extra is not None:
            extra.done(lanes, out[2:])


S5_SPLIT = 2


def _s5_fwd(u, af_r, af_i, bb, cc, dsk, *, name, dims, xch=None):
    n, dm = u.shape
    nk, swk = bb[0].shape[0], bb[0].shape[2]
    rr, sw = dims["s5_rows"], nk * swk
    nkh, dmh, swh = nk // S5_SPLIT, dm // S5_SPLIT, sw // S5_SPLIT
    nc, ncc = n // rr, dims["nctx"] // rr
    c1 = lambda i: _s5_chunk_of(i, ncc, nc, True)

    def body(u0_ref, u1_ref, afr_ref, afi_ref, b0r, b0i, b1r, b1i, c0r, c0i, c1r, c1i, dsk_ref,
             y0_ref, y1_ref, str_ref, sti_ref, s0r, s0i, s1r, s1i, cr, ci):
        @pl.when(pl.program_id(1) == 0)
        def _():
            cr[...] = jnp.zeros(cr.shape, F32)
            ci[...] = jnp.zeros(ci.shape, F32)

        str_ref[...] = cr[...]
        sti_ref[...] = ci[...]
        ub0, ub1 = u0_ref[...].astype(BF16), u1_ref[...].astype(BF16)
        for k in range(nkh):
            cols, sl = slice(k * LANES, (k + 1) * LANES), slice(k * swk, (k + 1) * swk)
            s0r[:, sl] = _dot(ub0[:, cols], b0r[k])
            s0i[:, sl] = _dot(ub0[:, cols], b0i[k])
            s1r[:, sl] = _dot(ub1[:, cols], b1r[k])
            s1i[:, sl] = _dot(ub1[:, cols], b1i[k])
        _s5_scan2((s0r, s0i), (s1r, s1i), 0, rr, afr_ref, afi_ref, cr, ci, lane_block=dims["s5_lane_block"])
        for k in range(nkh):
            cols, sl = slice(k * LANES, (k + 1) * LANES), slice(k * swk, (k + 1) * swk)
            y0_ref[:, cols] = (_dot(s0r[:, sl].astype(BF16), c0r[k]) - _dot(s0i[:, sl].astype(BF16), c0i[k])
                               + dsk_ref[:, cols] * u0_ref[:, cols])
            y1_ref[:, cols] = _dot(s1r[:, sl].astype(BF16), c1r[k]) - _dot(s1i[:, sl].astype(BF16), c1i[k])

    row0 = pl.BlockSpec((rr, dmh), lambda h, i: (i, h))
    row1 = pl.BlockSpec((rr, dmh), lambda h, i: (c1(i), h))
    tile = pl.BlockSpec((SUBLANES, swh), lambda h, i: (0, h))
    wspec = lambda a: pl.BlockSpec((nkh,) + a.shape[1:], lambda h, i: (h, 0, 0))
    st_spec = pl.BlockSpec((None, SUBLANES, swh), lambda h, i: (i, 0, h))
    sd = jax.ShapeDtypeStruct
    vmem = 4 * rr * swh * 4 + 12 * rr * dmh * 4 + 16 * nkh * LANES * swk * 2 + (12 << 20)
    return _call(
        body, name=name, xch=xch, args=[u, u, af_r, af_i, *bb, *cc, dsk],
        out_shape=[sd((n, dm), F32), sd((n, dm), F32), sd((nc, SUBLANES, sw), F32), sd((nc, SUBLANES, sw), F32)],
        grid=(S5_SPLIT, nc),
        in_specs=[row0, row1, tile, tile] + [wspec(a) for a in (*bb, *cc)] + [pl.BlockSpec((1, dmh), lambda h, i: (0, h))],
        out_specs=[row0, row1, st_spec, st_spec],
        scratch=[pltpu.VMEM((rr, swh), F32)] * 4 + [pltpu.VMEM((SUBLANES, swh), F32)] * 2,
        params=_params(("arbitrary", "arbitrary"), vmem))


class _DaHook2:
    def __init__(self, s0, s1, accs):
        self.s0, self.s1, self.accs = s0, s1, accs

    def init(self, lanes):
        return tuple(a[:, lanes] for a in self.accs)

    def done(self, lanes, acc):
        for a, v in zip(self.accs, acc):
            a[:, lanes] = v

    def __call__(self, t1, t0, lanes, l1, l0, acc):
        row = lax.broadcasted_iota(jnp.int32, l1[0].shape, 0)
        b1 = pl.multiple_of(SUBLANES + t1 * SUBLANES, SUBLANES)
        b0 = pl.multiple_of(SUBLANES + t0 * SUBLANES, SUBLANES)
        cur1, nxt1 = pl.ds(b1, SUBLANES), pl.ds(pl.multiple_of(b1 + SUBLANES, SUBLANES), SUBLANES)
        cur0, prv0 = pl.ds(b0, SUBLANES), pl.ds(pl.multiple_of(b0 - SUBLANES, SUBLANES), SUBLANES)
        p1r = pltpu.roll(jnp.where(row >= 4, self.s1[0][cur1, lanes], self.s1[0][nxt1, lanes]), 4, 0)
        p1i = pltpu.roll(jnp.where(row >= 4, self.s1[1][cur1, lanes], self.s1[1][nxt1, lanes]), 4, 0)
        p0r = pltpu.roll(jnp.where(row >= 4, self.s0[0][prv0, lanes], self.s0[0][cur0, lanes]), 4, 0)
        p0i = pltpu.roll(jnp.where(row >= 4, self.s0[1][prv0, lanes], self.s0[1][cur0, lanes]), 4, 0)
        return (acc[0] + p0r * l0[0] + p0i * l0[1], acc[1] + p0r * l0[1] - p0i * l0[0],
                acc[2] + p1r * l1[0] + p1i * l1[1], acc[3] + p1r * l1[1] - p1i * l1[0])


def _s5_bwd(u, dy, af_r, af_i, ab_r, ab_i, bb, cc, dsk, st_r, st_i, *, name, dims, xch=None):
    n, dm = u.shape
    nk, swk = bb[0].shape[0], bb[0].shape[2]
    rr, sw = dims["s5_rows"], nk * swk
    nkh, dmh, swh = nk // S5_SPLIT, dm // S5_SPLIT, sw // S5_SPLIT
    nc, ncc = n // rr, dims["nctx"] // rr
    f0 = lambda i: nc - 1 - i
    f1 = lambda i: _s5_chunk_of(nc - 1 - i, ncc, nc, True)

    def body(u0_ref, u1_ref, dy0_ref, dy1_ref, afr_ref, afi_ref, abr_ref, abi_ref, b0r, b0i, b1r, b1i, c0r, c0i, c1r, c1i,
             dsk_ref, str_ref, sti_ref,
             du0_ref, du1_ref, db0r, db0i, db1r, db1i, dc0r, dc0i, dc1r, dc1i, da0r_ref, da0i_ref, da1r_ref, da1i_ref, dd_ref,
             s0r, s0i, s1r, s1i, l0r, l0i, l1r, l1i, cr, ci, lcr, lci, a0r, a0i, a1r, a1i, dda):
        i = pl.program_id(1)

        @pl.when(i == 0)
        def _():
            for ref in (lcr, lci, a0r, a0i, a1r, a1i, dda, db0r, db0i, db1r, db1i, dc0r, dc0i, dc1r, dc1i):
                ref[...] = jnp.zeros(ref.shape, F32)

        row = lax.broadcasted_iota(jnp.int32, (SUBLANES, swh), 0)
        for st_ref, car, z0, z1 in ((str_ref, cr, s0r, s1r), (sti_ref, ci, s0i, s1i)):
            st = st_ref[...]
            car[...] = st
            z0[pl.ds(0, SUBLANES), :] = jnp.where(row < 4, st, pltpu.roll(st, 4, 0))
            z1[pl.ds(rr + SUBLANES, SUBLANES), :] = jnp.where(row >= 4, st, pltpu.roll(st, 4, 0))
        body_rows = pl.ds(SUBLANES, rr)
        ub0, ub1 = u0_ref[...].astype(BF16), u1_ref[...].astype(BF16)
        dyb0, dyb1 = dy0_ref[...].astype(BF16), dy1_ref[...].astype(BF16)
        for k in range(nkh):
            cols, sl = slice(k * LANES, (k + 1) * LANES), slice(k * swk, (k + 1) * swk)
            s0r[body_rows, sl] = _dot(ub0[:, cols], b0r[k])
            s0i[body_rows, sl] = _dot(ub0[:, cols], b0i[k])
            s1r[body_rows, sl] = _dot(ub1[:, cols], b1r[k])
            s1i[body_rows, sl] = _dot(ub1[:, cols], b1i[k])
        _s5_scan2((s0r, s0i), (s1r, s1i), SUBLANES, rr, afr_ref, afi_ref, cr, ci, lane_block=dims["s5_lane_block"])
        for k in range(nkh):
            cols, sl = slice(k * LANES, (k + 1) * LANES), slice(k * swk, (k + 1) * swk)
            for dyk, lr, li, sr, si, ccr, cci, dcr, dci in ((dyb0[:, cols], l0r, l0i, s0r, s0i, c0r, c0i, dc0r, dc0i),
                                                           (dyb1[:, cols], l1r, l1i, s1r, s1i, c1r, c1i, dc1r, dc1i)):
                lr[:, sl] = _dot(dyk, ccr[k], NT)
                li[:, sl] = -_dot(dyk, cci[k], NT)
                dcr[k] += _dot(dyk, sr[body_rows, sl].astype(BF16), TN)
                dci[k] -= _dot(dyk, si[body_rows, sl].astype(BF16), TN)
        _s5_scan2((l1r, l1i), (l0r, l0i), 0, rr, abr_ref, abi_ref, lcr, lci, lane_block=dims["s5_lane_block"] // 2,
                  extra=_DaHook2((s0r, s0i), (s1r, s1i), (a0r, a0i, a1r, a1i)))
        for k in range(nkh):
            cols, sl = slice(k * LANES, (k + 1) * LANES), slice(k * swk, (k + 1) * swk)
            for uk, lr, li, br, bi, dbr, dbi, du_ref, first in ((ub0[:, cols], l0r, l0i, b0r, b0i, db0r, db0i, du0_ref, True),
                                                              (ub1[:, cols], l1r, l1i, b1r, b1i, db1r, db1i, du1_ref, False)):
                lrk, lik = lr[:, sl].astype(BF16), li[:, sl].astype(BF16)
                dbr[k] += _dot(uk, lrk, TN)
                dbi[k] += _dot(uk, lik, TN)
                duk = _dot(lrk, br[k], NT) + _dot(lik, bi[k], NT)
                if first:
                    duk = duk + dsk_ref[:, cols] * dy0_ref[:, cols]
                du_ref[:, cols] = duk
        dda[...] += jnp.sum((dy0_ref[...] * u0_ref[...]).reshape(rr // SUBLANES, SUBLANES, dmh), axis=0)

        @pl.when(i == nc - 1)
        def _():
            for o, a in ((da0r_ref, a0r), (da0i_ref, a0i), (da1r_ref, a1r), (da1i_ref, a1i), (dd_ref, dda)):
                o[...] = jnp.sum(a[...], axis=0, keepdims=True)

    row0 = pl.BlockSpec((rr, dmh), lambda h, i: (f0(i), h))
    row1 = pl.BlockSpec((rr, dmh), lambda h, i: (f1(i), h))
    tile = pl.BlockSpec((SUBLANES, swh), lambda h, i: (0, h))
    wspec = lambda a: pl.BlockSpec((nkh,) + a.shape[1:], lambda h, i: (h, 0, 0))
    st_spec = pl.BlockSpec((None, SUBLANES, swh), lambda h, i: (f0(i), 0, h))
    vec = lambda w: pl.BlockSpec((1, w), lambda h, i: (0, h))
    sd = jax.ShapeDtypeStruct
    out_shape = ([sd((n, dm), F32)] * 2 + [sd(a.shape, F32) for a in (*bb, *bb)] + [sd((1, sw), F32)] * 4 + [sd((1, dm), F32)])
    out_specs = [row0, row1] + [wspec(a) for a in (*bb, *bb)] + [vec(swh)] * 4 + [vec(dmh)]
    scratch = ([pltpu.VMEM((rr + 2 * SUBLANES, swh), F32)] * 4 + [pltpu.VMEM((rr, swh), F32)] * 4
               + [pltpu.VMEM((SUBLANES, swh), F32)] * 8 + [pltpu.VMEM((SUBLANES, dmh), F32)])
    vmem = 8 * (rr + 16) * swh * 4 + 16 * rr * dmh * 4 + 48 * nkh * LANES * swk * 4 + (10 << 20)
    return _call(
        body, name=name, xch=xch, args=[u, u, dy, dy, af_r, af_i, ab_r, ab_i, *bb, *cc, dsk, st_r, st_i],
        out_shape=out_shape, grid=(S5_SPLIT, nc),
        in_specs=[row0, row1, row0, row1, tile, tile, tile, tile] + [wspec(a) for a in (*bb, *cc)] + [vec(dmh), st_spec, st_spec],
        out_specs=out_specs, scratch=scratch, params=_params(("arbitrary", "arbitrary"), vmem))


def _hg_mask(kind, rev):
    if kind == "tot":
        r = lax.broadcasted_iota(jnp.int32, (SUBLANES, HG_ROWS), 0)
        c = lax.broadcasted_iota(jnp.int32, (SUBLANES, HG_ROWS), 1)
        return (c & 3) == r
    r = lax.broadcasted_iota(jnp.int32, (HG_ROWS, HG_ROWS), 0)
    c = lax.broadcasted_iota(jnp.int32, (HG_ROWS, HG_ROWS), 1)
    same = (r & 3) == (c & 3)
    before = ((c >> 2) >= (r >> 2)) if rev else ((c >> 2) <= (r >> 2))
    return jnp.logical_and(same, before if kind == "cum" else jnp.logical_not(before))


def _split2(x):
    hi = x.astype(BF16)
    return hi, (x - hi.astype(F32)).astype(BF16)


@functools.partial(jax.custom_vjp, nondiff_argnums=(1, 2))
def _mask_sum(x, kind, rev):
    m = _hg_mask(kind, rev).astype(BF16)
    hi, lo = _split2(x)
    return _dot(m, hi) + _dot(m, lo)


def _mask_sum_fwd(x, kind, rev):
    return _mask_sum(x, kind, rev), None


def _mask_sum_bwd(kind, rev, _, g):
    m = _hg_mask(kind, rev).astype(BF16)
    hi, lo = _split2(g)
    return (_dot(m, hi, TN) + _dot(m, lo, TN),)


_mask_sum.defvjp(_mask_sum_fwd, _mask_sum_bwd)


def _hg_chunk(q, v, fraw, l0, l1, st, *, rev):
    nh = q.shape[1] // HG_HEAD
    lb = _sigmoid(l1 - l0)
    logf = jnp.logaddexp(jnp.log(lb), jnp.log1p(-lb) + jax.nn.log_sigmoid(fraw))
    kk = (1.0 - lb) * _sigmoid(fraw * -1.0)
    tri = _hg_mask("cum", rev)
    bcum = _mask_sum(logf, "cum", rev)
    brem = _mask_sum(logf, "rem", rev)
    bend8 = _mask_sum(logf, "tot", rev)
    r8d = lax.broadcasted_iota(jnp.int32, bend8.shape, 0)
    decs = [jnp.exp(jnp.sum(jnp.where(r8d == b, bend8, 0.0), axis=0, keepdims=True)) for b in range(LOCAL_B)]
    qd = (q * jnp.exp(bcum)).astype(BF16)
    kd = (kk * jnp.exp(-bcum)).astype(BF16)
    ke = (kk * jnp.exp(brem)).astype(BF16)
    wide = (HG_ROWS, LOCAL_B * HG_HEAD)
    mine = (lax.broadcasted_iota(jnp.int32, wide, 1) >> 7) == (lax.broadcasted_iota(jnp.int32, wide, 0) & 3)
    per_example = lambda x: jnp.where(mine, jnp.concatenate([x] * LOCAL_B, axis=1), jnp.zeros(wide, x.dtype))
    outs, new = [], []
    for h in range(nh):
        sl = slice(h * HG_HEAD, (h + 1) * HG_HEAD)
        vh = v[:, sl].astype(BF16)
        att = jnp.where(tri, _dot(qd[:, sl], kd[:, sl], NT), 0.0)
        outs.append(_dot(att.astype(BF16), vh) + _dot(per_example(qd[:, sl]), st[h].astype(BF16), NT))
        dec = jnp.concatenate([d[:, sl] for d in decs], axis=1)
        new.append(st[h] * dec + _dot(vh, per_example(ke[:, sl]), TN))
    return jnp.concatenate(outs, axis=1), tuple(new)


def _hg_chunk_of(step, ncc, nc, rev):
    return _s5_chunk_of(step, ncc, nc, rev)


def _hg_fwd_dir(zz, lb2, *, d, name, dims, xch=None):
    n = zz.shape[0]
    dm = zz.shape[1] // 5
    ns, sw = dm // HG_HEAD, LOCAL_B * HG_HEAD
    nc, ncc = n // HG_ROWS, dims["nctx"] // HG_ROWS
    rev = d == 1
    ch = lambda i: _hg_chunk_of(i, ncc, nc, rev)

    def body(q_ref, v_ref, f_ref, l0_ref, l1_ref, o_ref, st_ref, st):
        @pl.when(pl.program_id(0) == 0)
        def _():
            st[...] = jnp.zeros(st.shape, F32)

        st_ref[...] = st[...]
        o, new = _hg_chunk(q_ref[...], v_ref[...], f_ref[...], l0_ref[...], l1_ref[...],
                           tuple(st[j] for j in range(ns)), rev=rev)
        o_ref[...] = o
        for j in range(ns):
            st[j] = new[j]

    blk = lambda off: pl.BlockSpec((HG_ROWS, dm), lambda i, off=off: (ch(i), off))
    lspec = lambda layer: pl.BlockSpec((None, None, 1, dm), lambda i, layer=layer: (d, layer, 0, 0))
    return _call(
        body, name=name, xch=xch, args=[zz, zz, zz, lb2, lb2],
        out_shape=[jax.ShapeDtypeStruct((n, dm), F32), jax.ShapeDtypeStruct((nc, ns, HG_HEAD, sw), F32)],
        grid=(nc,),
        in_specs=[blk(0), blk(1), blk(2 + d), lspec(0), lspec(1)],
        out_specs=[pl.BlockSpec((HG_ROWS, dm), lambda i: (ch(i), 0)),
                   pl.BlockSpec((None, ns, HG_HEAD, sw), lambda i: (ch(i), 0, 0, 0))],
        scratch=[pltpu.VMEM((ns, HG_HEAD, sw), F32)],
        params=_params(("arbitrary",), 48 << 20))


def _hg_bwd_dir(zz, lb2, do, sts, dqv_prev, *, d, name, dims, xch=None):
    n = zz.shape[0]
    dm = zz.shape[1] // 5
    ns, sw = dm // HG_HEAD, LOCAL_B * HG_HEAD
    nc, ncc = n // HG_ROWS, dims["nctx"] // HG_ROWS
    rev = d == 1
    ch = lambda i: _hg_chunk_of(nc - 1 - i, ncc, nc, rev)
    qv_dtype = F32 if d == 0 else BF16

    def body(*refs):
        q_ref, v_ref, f_ref, l0_ref, l1_ref, do_ref, st_ref = refs[:7]
        pos = 7
        if d == 1:
            dqp_ref, dvp_ref = refs[7:9]
            pos = 9
        dq_ref, dv_ref, df_ref, dl_ref, dst = refs[pos:]
        i = pl.program_id(0)

        @pl.when(i == 0)
        def _():
            dst[...] = jnp.zeros(dst.shape, F32)
            dl_ref[...] = jnp.zeros(dl_ref.shape, F32)

        _, vjp = jax.vjp(functools.partial(_hg_chunk, rev=rev), q_ref[...], v_ref[...], f_ref[...], l0_ref[...],
                         l1_ref[...], tuple(st_ref[j] for j in range(ns)))
        dq, dv, df, dl0, dl1, dstn = vjp((do_ref[...].astype(F32), tuple(dst[j] for j in range(ns))))
        for j in range(ns):
            dst[j] = dstn[j]
        if d == 1:
            dq = dq + dqp_ref[...]
            dv = dv + dvp_ref[...]
        dq_ref[...] = dq.astype(qv_dtype)
        dv_ref[...] = dv.astype(qv_dtype)
        df_ref[...] = df.astype(BF16)
        dl_ref[0] += dl0
        dl_ref[1] += dl1

    blk = lambda off: pl.BlockSpec((HG_ROWS, dm), lambda i, off=off: (ch(i), off))
    oblk = pl.BlockSpec((HG_ROWS, dm), lambda i: (ch(i), 0))
    lspec = lambda layer: pl.BlockSpec((None, None, 1, dm), lambda i, layer=layer: (d, layer, 0, 0))
    ins = [zz, zz, zz, lb2, lb2, do, sts] + (list(dqv_prev) if d == 1 else [])
    in_specs = [blk(0), blk(1), blk(2 + d), lspec(0), lspec(1), oblk,
                pl.BlockSpec((None, ns, HG_HEAD, sw), lambda i: (ch(i), 0, 0, 0))]
    in_specs += [oblk, oblk] if d == 1 else []
    sd = jax.ShapeDtypeStruct
    return _call(
        body, name=name, xch=xch, args=ins,
        out_shape=[sd((n, dm), qv_dtype), sd((n, dm), qv_dtype), sd((n, dm), BF16), sd((2, 1, dm), F32)],
        grid=(nc,), in_specs=in_specs,
        out_specs=[oblk, oblk, oblk, pl.BlockSpec((2, 1, dm), lambda i: (0, 0, 0))],
        scratch=[pltpu.VMEM((ns, HG_HEAD, sw), F32)],
        params=_params(("arbitrary",), 56 << 20))


def _hg_readout(o, g, w):
    outs = []
    for h in range(o.shape[-1] // HG_HEAD):
        sl = slice(h * HG_HEAD, (h + 1) * HG_HEAD)
        oh = o[..., sl]
        outs.append(oh * _rms(oh) * w * _sigmoid(g[..., sl]))
    return jnp.concatenate(outs, axis=-1)


def _silu(x):
    return x * _sigmoid(x)


def _mod_fwd(craw, w, b):
    def body(c_ref, w_ref, b_ref, o_ref):
        s = _silu(c_ref[...]).astype(BF16)
        for layer in range(w.shape[0]):
            o_ref[layer] = _dot(s, w_ref[layer].astype(BF16)) + b_ref[layer]

    return pl.pallas_call(body, name="mod_fwd",
                          out_shape=jax.ShapeDtypeStruct((w.shape[0], craw.shape[0], w.shape[2]), F32),
                          compiler_params=_params(None, 40 << 20))(craw, w, b)


def _mod_bwd(craw, w, dlat_sh, dctx_sh, dlat_full, dctx_full):
    nl, dm, ns = w.shape
    nb = dlat_sh.shape[1]

    def body(c_ref, w_ref, dl_ref, dc_ref, dlf_ref, dcf_ref, dw_ref, db_ref, dcc_ref):
        craw_v = c_ref[...]
        s = _silu(craw_v)
        s_lat = s[:nb].astype(BF16)
        s_ctx = s[nb:].astype(BF16)
        row = lax.broadcasted_iota(jnp.int32, (SUBLANES, ns), 0)
        dsc = jnp.zeros((SUBLANES, dm), F32)
        for layer in range(nl):
            tot = dc_ref[0, pl.ds(layer, 1), :]
            totf = dcf_ref[0, pl.ds(layer, 1), :]
            for i in range(1, NDEV):
                tot = tot + dc_ref[i, pl.ds(layer, 1), :]
                totf = totf + dcf_ref[i, pl.ds(layer, 1), :]
            dc8 = jnp.where(row == 0, jnp.broadcast_to(tot, (SUBLANES, ns)), 0.0).astype(BF16)
            dw_ref[layer] = _dot(s_lat, dl_ref[layer].astype(BF16), TN) + _dot(s_ctx, dc8, TN)
            db_ref[layer] = jnp.sum(dlf_ref[layer], axis=0, keepdims=True) + totf
            dsc = dsc + _dot(dc8, w_ref[layer].astype(BF16), NT)
        cc = craw_v[nb:]
        sg = _sigmoid(cc)
        dcc_ref[...] = dsc * (sg * (1.0 + cc * (1.0 - sg)))

    sd = jax.ShapeDtypeStruct
    return pl.pallas_call(body, name="mod_bwd",
                          out_shape=[sd((nl, dm, ns), F32), sd((nl, 1, dlat_full.shape[2]), F32), sd((SUBLANES, dm), F32)],
                          compiler_params=_params(None, 48 << 20))(craw, w, dlat_sh, dctx_sh, dlat_full, dctx_full)


def _adam_rows(r):
    best = None
    for t in range(2 * SUBLANES, min(r, 128) + 1, 2 * SUBLANES):
        if r % t == 0:
            best = t
    return best if best is not None else r


def _adamw(parts, w, m, v, *, name):
    npart, r, c = parts.shape
    tr = _adam_rows(r)

    def body(p_ref, w_ref, m_ref, v_ref, g_ref, d_ref, nm_ref, nv_ref):
        g = p_ref[0].astype(F32)
        for i in range(1, npart):
            g = g + p_ref[i].astype(F32)
        nm = ADAM_B1 * m_ref[...] + (1.0 - ADAM_B1) * g
        nv = ADAM_B2 * v_ref[...] + (1.0 - ADAM_B2) * (g * g)
        m_hat = nm / (1.0 - ADAM_B1 ** ADAM_STEP)
        v_hat = nv / (1.0 - ADAM_B2 ** ADAM_STEP)
        g_ref[...] = g
        d_ref[...] = -ADAM_LR * (m_hat / (jnp.sqrt(v_hat) + ADAM_EPS) + ADAM_WD * w_ref[...])
        nm_ref[...] = nm
        nv_ref[...] = nv

    spec = pl.BlockSpec((tr, c), lambda i: (i, 0))
    vmem = 2 * (npart + 7) * tr * c * 4 + (8 << 20)
    return pl.pallas_call(
        body, name=name, out_shape=[jax.ShapeDtypeStruct((r, c), F32)] * 4, grid=(r // tr,),
        in_specs=[pl.BlockSpec((npart, tr, c), lambda i: (0, i, 0)), spec, spec, spec], out_specs=[spec] * 4,
        compiler_params=_params(("parallel",), vmem),
    )(parts, w, m, v)


def _to_tm(a):
    return jnp.transpose(a, (1, 0, 2)).reshape(a.shape[1] * a.shape[0], a.shape[2])


def _pattern(mod_lat, mod_ctx, m, dm):
    lat = mod_lat[:, m * dm:(m + 1) * dm]
    ctx = jnp.broadcast_to(mod_ctx[None, m * dm:(m + 1) * dm], (SUBLANES, dm))
    return jnp.stack([ctx, jnp.concatenate([lat, lat], axis=0)])


def _blockdiag_b(bt, nk):
    g, h, p = bt.shape
    t = bt.reshape(nk, 8, h, p)
    return jnp.einsum("kghp,gj->kghjp", t, jnp.eye(8, dtype=bt.dtype)).reshape(nk, 8 * h, 8 * p)


def _blockdiag_c(ct, nk):
    g, h, p = ct.shape
    t = ct.reshape(nk, 8, h, p)
    return jnp.einsum("kghp,gj->kgpjh", t, jnp.eye(8, dtype=ct.dtype)).reshape(nk, 8 * p, 8 * h)


def _diag_b(dbb, h, p):
    nk = dbb.shape[0]
    return jnp.einsum("kghgp->kghp", dbb.reshape(nk, 8, h, 8, p)).reshape(nk * 8, h, p)


def _s5_prep(lam_re, lam_im, log_step, b_re, b_im, c_re, c_im, dm):
    ngrp, nk = dm // S5_GROUP, dm // LANES
    sw = ngrp * S5_STATE
    lr4 = lam_re.reshape(2, ngrp, 1, S5_STATE)
    li4 = lam_im.reshape(2, ngrp, 1, S5_STATE)
    ls4 = log_step.reshape(2, ngrp, 1, 1)
    brt = jnp.transpose(b_re, (0, 1, 3, 2))
    bit = jnp.transpose(b_im, (0, 1, 3, 2))
    abar_r, abar_i, bbar_r, bbar_i = _s5_disc_fwd(lr4, li4, ls4, brt, bit)
    half = lambda a, d: jnp.broadcast_to(a[d].reshape(1, sw), (LOCAL_B, sw))
    tile = lambda a, first: jnp.concatenate([half(a, first), half(a, 1 - first)], axis=0)
    bb = tuple(_blockdiag_b(w[d], nk).astype(BF16) for d in range(2) for w in (bbar_r, bbar_i))
    cc = tuple(_blockdiag_c(w[d], nk).astype(BF16) for d in range(2) for w in (c_re, c_im))
    return dict(disc_in=(lr4, li4, ls4, brt, bit), af=(tile(abar_r, 0), tile(abar_i, 0)),
                ab=(tile(abar_r, 1), -tile(abar_i, 1)), bb=bb, cc=cc)


def _s5_param_grads(prep, dbb, dcc, da):
    lr4 = prep["disc_in"][0]
    dar = jnp.stack([da[0], da[2]]).reshape(lr4.shape)
    dai = jnp.stack([da[1], da[3]]).reshape(lr4.shape)
    dbbr = jnp.stack([_diag_b(dbb[0], S5_GROUP, S5_STATE), _diag_b(dbb[2], S5_GROUP, S5_STATE)])
    dbbi = jnp.stack([_diag_b(dbb[1], S5_GROUP, S5_STATE), _diag_b(dbb[3], S5_GROUP, S5_STATE)])
    dlr, dli, dls, dbrt, dbit = _s5_disc_bwd(*prep["disc_in"], dar, dai, dbbr, dbbi)
    g_c_re = jnp.stack([_diag_b(dcc[0], S5_GROUP, S5_STATE), _diag_b(dcc[2], S5_GROUP, S5_STATE)])
    g_c_im = jnp.stack([_diag_b(dcc[1], S5_GROUP, S5_STATE), _diag_b(dcc[3], S5_GROUP, S5_STATE)])
    return dlr, dli, dls, jnp.transpose(dbrt, (0, 1, 3, 2)), jnp.transpose(dbit, (0, 1, 3, 2)), g_c_re, g_c_im


def kernel(x, c, ctx, c_ctx, w_mod, b_mod, norm1_w, norm2_w, final_norm_w, s5_w_in, s5_lam_re, s5_lam_im, s5_log_step, s5_b_re, s5_b_im, s5_c_re, s5_c_im, s5_d, s5_w_glu, s5_w_out, hg_w_in, hg_lower_bounds, hg_gnorm_w, hg_w_out, ffn_w_up, ffn_conv_w, ffn_conv_b, ffn_w_down, loss_target, m_c_ctx, m_w_mod, m_b_mod, m_norm1_w, m_norm2_w, m_final_norm_w, m_s5_w_in, m_s5_lam_re, m_s5_lam_im, m_s5_log_step, m_s5_b_re, m_s5_b_im, m_s5_c_re, m_s5_c_im, m_s5_d, m_s5_w_glu, m_s5_w_out, m_hg_w_in, m_hg_lower_bounds, m_hg_gnorm_w, m_hg_w_out, m_ffn_w_up, m_ffn_conv_w, m_ffn_conv_b, m_ffn_w_down, v_c_ctx, v_w_mod, v_b_mod, v_norm1_w, v_norm2_w, v_final_norm_w, v_s5_w_in, v_s5_lam_re, v_s5_lam_im, v_s5_log_step, v_s5_b_re, v_s5_b_im, v_s5_c_re, v_s5_c_im, v_s5_d, v_s5_w_glu, v_s5_w_out, v_hg_w_in, v_hg_lower_bounds, v_hg_gnorm_w, v_hg_w_out, v_ffn_w_up, v_ffn_conv_w, v_ffn_conv_b, v_ffn_w_down):
    given = dict(locals())
    bsz, lx, dm = x.shape
    lc = ctx.shape[1]
    assert bsz == LOCAL_B and w_mod.shape[0] == 2 and dm % LANES == 0
    n, nctx = (lc + lx) * bsz, lc * bsz
    ngrp, nstate, hgrp = dm // S5_GROUP, S5_STATE, S5_GROUP
    nk = dm // LANES
    dims = dict(nctx=nctx, tm=min(512, nctx), tm_row=min(512, nctx), s5_rows=min(256, nctx),
                s5_lane_block=min(512, 8 * nstate))
    tm = dims["tm"]
    assert nctx % HG_ROWS == 0 and (lx * bsz) % nctx == 0 and lc % GRID_W == 0 and lc & (lc - 1) == 0
    me = 4 * lax.axis_index("x") + 2 * lax.axis_index("y") + lax.axis_index("c")

    gath = _exchange([given[k].astype(BF16) for k in ("s5_w_in", "s5_w_glu", "s5_w_out")]
                     + [c, hg_lower_bounds, ffn_conv_w], a2a=False, name="gather_weights")
    w_s5in, w_glu, w_s5out = (g.reshape(dm, dm) for g in gath[:3])
    c_all, lb_all, cw_all = gath[3:]
    ns_up = ffn_w_up.shape[2]
    w_up, w_dn = [None, None], [None, None]
    cols = lambda g: jnp.transpose(g, (1, 0, 2)).reshape(g.shape[1], -1)
    shards = lambda w: jnp.transpose(w.reshape(w.shape[0], NDEV, -1), (1, 0, 2))
    tn_up, tn_hg, tkr = 2 * ns_up, 2 * hg_w_in.shape[2], 1152
    assert n % tkr == 0 and n % 1024 == 0
    gather = lambda arrs: _Xchg([a.astype(BF16) for a in arrs], [False] * len(arrs))
    scatter = lambda arrs: _Xchg(arrs, [True] * len(arrs))
    cw = [cols(cw_all[:, layer]) for layer in range(2)]
    cb = [ffn_conv_b[layer].reshape(1, -1) for layer in range(2)]
    lb2 = jnp.transpose(lb_all, (1, 2, 0, 3)).reshape(2, 2, 1, dm)

    nsm = w_mod.shape[2]
    craw = jnp.concatenate([c_all.reshape(NDEV * bsz, dm), c_ctx[None], jnp.zeros((SUBLANES - 1, dm), F32)], axis=0)
    b_sh = lax.dynamic_slice(b_mod, (0, me * nsm), (2, nsm)).reshape(2, 1, nsm)
    mod_sh = _mod_fwd(craw, w_mod, b_sh)
    (mod_g,) = _exchange([mod_sh], a2a=False, name="gather_mod")
    mod_full = jnp.transpose(mod_g, (1, 2, 0, 3)).reshape(2, craw.shape[0], NDEV * nsm)
    pat = []
    for layer in range(2):
        mlat = lax.dynamic_slice(mod_full[layer], (me * bsz, 0), (bsz, N_MOD * dm))
        mctx = mod_full[layer, NDEV * bsz]
        pat.append([_pattern(mlat, mctx, m, dm) for m in range(N_MOD)])

    s5p = _s5_prep(s5_lam_re[0], s5_lam_im[0], s5_log_step[0], s5_b_re[0], s5_b_im[0], s5_c_re[0], s5_c_im[0], dm)
    dsk = s5_d.reshape(1, dm)

    z0 = jnp.concatenate([_to_tm(ctx), _to_tm(x)], axis=0)
    tgt = _to_tm(loss_target)
    n1w = [norm1_w[layer].reshape(1, dm) for layer in range(2)]
    n2w = [norm2_w[layer].reshape(1, dm) for layer in range(2)]

    def ffn_fwd(layer, h2):
        u = _lin(h2, w_up[layer], name=f"ffn_up{layer}", tm=1024, tn=tn_up)
        hm = _convffn_fwd(u, cw[layer], cb[layer], name=f"convffn_fwd{layer}", dims=dims)
        f = _lin(hm, w_dn[layer], name=f"ffn_down{layer}", tm=tm, tn=dm)
        return u, hm, f

    _, h0 = _norm_mod_fwd(z0, n1w[0], pat[0][0], pat[0][1], name="norm1_l0", dims=dims)
    u_s5 = _lin(h0, w_s5in, name="s5_in", tm=1024, tn=dm)
    (y_s5a, y_s5b, st_r, st_i), (g_up0, g_dn0, g_hgin, g_hgout, g_dn1) = _s5_fwd(
        u_s5, *s5p["af"], s5p["bb"], s5p["cc"], dsk, name="s5_fwd", dims=dims,
        xch=gather([ffn_w_up[0], ffn_w_down[0], hg_w_in[0], hg_w_out[0], ffn_w_down[1]]))
    w_up[0], w_dn[0] = cols(g_up0), g_dn0.reshape(-1, dm)
    w_hgin, w_hgout, w_dn[1] = cols(g_hgin), g_hgout.reshape(dm, dm), g_dn1.reshape(-1, dm)
    (zg,) = _rowk(lambda rv, pv, cv, il: ([_gelu(rv[0] + rv[1])], [], []), name="s5_gelu", n=n, tm=dims["tm_row"],
                  nctx=nctx, rows=[(y_s5a, dm, 0, 0), (y_s5b, dm, 0, 0)], out_rows=[(dm, BF16, dm, 0)])
    t_glu = _lin(zg, w_glu, name="s5_glu", tm=1024, tn=dm)
    (z2g,) = _rowk(lambda rv, pv, cv, il: ([rv[0] * _sigmoid(rv[1])], [], []), name="s5_gate", n=n,
                   tm=dims["tm_row"], nctx=nctx, rows=[(zg, dm, 0, 0), (t_glu, dm, 0, 0)],
                   out_rows=[(dm, BF16, dm, 0)])
    ymix0 = _lin(z2g, w_s5out, name="s5_out", tm=1024, tn=dm)
    z1_l0, h2_l0 = _norm_mod_fwd(z0, n2w[0], pat[0][3], pat[0][4], name="norm2_l0", dims=dims,
                                 res=(ymix0, pat[0][2]))
    u_l0, hm_l0, f_l0 = ffn_fwd(0, h2_l0)

    z2_l0, h1 = _norm_mod_fwd(z1_l0, n1w[1], pat[1][0], pat[1][1], name="norm1_l1", dims=dims,
                              res=(f_l0, pat[0][5]))
    zz = _lin(h1, w_hgin, name="hg_in", tm=1024, tn=tn_hg)
    (o_f, sts_f), (g_up1,) = _hg_fwd_dir(zz, lb2, d=0, name="hg_fwd_d0", dims=dims, xch=gather([ffn_w_up[1]]))
    w_up[1] = cols(g_up1)
    (o_b, sts_b), _ = _hg_fwd_dir(zz, lb2, d=1, name="hg_fwd_d1", dims=dims)
    gnw = hg_gnorm_w.reshape(1, HG_HEAD)
    (og,) = _rowk(lambda rv, pv, cv, il: ([_hg_readout(rv[0] + rv[1], rv[2], cv[0])], [], []), name="hg_readout",
                  n=n, tm=dims["tm_row"], nctx=nctx, rows=[(o_f, dm, 0, 0), (o_b, dm, 0, 0), (zz, dm, 4, 0)],
                  consts=[gnw], out_rows=[(dm, BF16, dm, 0)])
    ymix1 = _lin(og, w_hgout, name="hg_out", tm=1024, tn=dm)
    z1_l1, h2_l1 = _norm_mod_fwd(z2_l0, n2w[1], pat[1][3], pat[1][4], name="norm2_l1", dims=dims,
                                 res=(ymix1, pat[1][2]))
    u_l1, hm_l1, f_l1 = ffn_fwd(1, h2_l1)

    dz, df, dgate2_l1, loss_part, dfinal_w = _loss_bwd(z1_l1, f_l1, pat[1][5], tgt, final_norm_w.reshape(1, dm),
                                                        name="loss_bwd", dims=dims)

    def ffn_bwd(layer, df_, u, hm, h2, xch=None):
        dff = hm.shape[1]
        dhm = _lin(df_, w_dn[layer], name=f"ffn_down_bwd_in{layer}", trans_w=True, tm=1024, tn=dff // 2, o_dtype=BF16)
        dwd = _lin_w(hm, df_, name=f"ffn_down_bwd_w{layer}", ta=dff // 2, tn=dm, tkr=tkr)
        (dua, dug, dcwa, dcwg, dcba, dcbg), got = _convffn_bwd(u, dhm, cw[layer], cb[layer],
                                                               name=f"convffn_bwd{layer}", dims=dims, xch=xch)
        dh2 = _lin(dua, w_up[layer], name=f"ffn_up_bwd_in_a{layer}", trans_w=True, tm=tm, tn=dm, kblk=0)
        dh2 = _lin(dug, w_up[layer], name=f"ffn_up_bwd_in_g{layer}", trans_w=True, tm=tm, tn=dm, kblk=1, base=dh2,
                   o_dtype=BF16)
        dwu = jnp.concatenate([_lin_w(h2, dua, name=f"ffn_up_bwd_w_a{layer}", ta=dm, tn=tn_up, tkr=tkr),
                               _lin_w(h2, dug, name=f"ffn_up_bwd_w_g{layer}", ta=dm, tn=tn_up, tkr=tkr)], axis=1)
        dcw = shards(jnp.concatenate([dcwa, dcwg], axis=1))
        return dh2, shards(dwu), dwd, dcw, jnp.concatenate([dcba, dcbg], axis=1), got

    dh2, dwu_l1, dwd_l1, dcw_l1, dcb_l1, _ = ffn_bwd(1, df, u_l1, hm_l1, h2_l1)
    dz, dymix, dsh2_l1, dsc2_l1, dgate1_l1, dn2w_l1 = _norm_mod_bwd(dh2, z1_l1, dz, n2w[1], pat[1][4], name="norm2_bwd_l1",
                                                                    dims=dims, res=(ymix1, pat[1][2]))
    dog = _lin(dymix, w_hgout, name="hg_out_bwd_in", trans_w=True, tm=1024, tn=dm, o_dtype=BF16)
    dw_hgout = _lin_w(og, dymix, name="hg_out_bwd_w", ta=dm, tn=dm, tkr=tkr)

    def readout_bwd(rv, pv, cv, il):
        _, vjp = jax.vjp(_hg_readout, rv[0] + rv[1], rv[2], cv[0])
        do, dg, dw = vjp(rv[3])
        return [do, dg], [], [jnp.broadcast_to(dw, (SUBLANES, HG_HEAD)) * (1.0 / SUBLANES)]

    do, dg, dgnw = _rowk(readout_bwd, name="hg_readout_bwd", n=n, tm=dims["tm_row"], nctx=nctx,
                         rows=[(o_f, dm, 0, 0), (o_b, dm, 0, 0), (zz, dm, 4, 0), (dog, dm, 0, 0)], consts=[gnw],
                         out_rows=[(dm, BF16, dm, 0), (dm, BF16, dm, 0)], out_acc=[HG_HEAD])
    (dq0, dv0, dff, dl_f), (p_up1, p_dn1) = _hg_bwd_dir(
        zz, lb2, do, sts_f, None, d=0, name="hg_bwd_d0", dims=dims,
        xch=scatter([dwu_l1, dwd_l1.reshape(NDEV, -1, dm)]))
    (dq, dv, dfb, dl_b), (p_hgout,) = _hg_bwd_dir(
        zz, lb2, do, sts_b, (dq0, dv0), d=1, name="hg_bwd_d1", dims=dims,
        xch=scatter([dw_hgout.reshape(NDEV, -1, dm)]))
    dh1, dw_parts = None, []
    for p, piece in enumerate((dq, dv, dff, dfb, dg)):
        dh1 = _lin(piece, w_hgin, name=f"hg_in_bwd_in{p}", trans_w=True, tm=1024, tn=dm, kblk=p, base=dh1,
                   o_dtype=BF16 if p == 4 else F32)
        dw_parts.append(_lin_w(h1, piece, name=f"hg_in_bwd_w{p}", ta=dm, tn=dm, tkr=tkr))
    dw_hgin = shards(jnp.concatenate(dw_parts, axis=1))
    dz, df0, dsh1_l1, dsc1_l1, dgate2_l0, dn1w_l1 = _norm_mod_bwd(dh1, z2_l0, dz, n1w[1], pat[1][1], name="norm1_bwd_l1",
                                                                  dims=dims, res=(f_l0, pat[0][5]))
    dh2, dwu_l0, dwd_l0, dcw_l0, dcb_l0, _ = ffn_bwd(0, df0, u_l0, hm_l0, h2_l0)
    dz, dymix, dsh2_l0, dsc2_l0, dgate1_l0, dn2w_l0 = _norm_mod_bwd(dh2, z1_l0, dz, n2w[0], pat[0][4], name="norm2_bwd_l0",
                                                                    dims=dims, res=(ymix0, pat[0][2]))
    dz2g = _lin(dymix, w_s5out, name="s5_out_bwd_in", trans_w=True, tm=1024, tn=dm, o_dtype=BF16)
    dw_s5out = _lin_w(z2g, dymix, name="s5_out_bwd_w", ta=dm, tn=dm, tkr=tkr)

    def gate_bwd(rv, pv, cv, il):
        sg = _sigmoid(rv[1])
        return [rv[2] * rv[0] * sg * (1.0 - sg), rv[2] * sg], [], []

    dt_glu, dzg_a = _rowk(gate_bwd, name="s5_gate_bwd", n=n, tm=dims["tm_row"], nctx=nctx,
                          rows=[(zg, dm, 0, 0), (t_glu, dm, 0, 0), (dz2g, dm, 0, 0)],
                          out_rows=[(dm, BF16, dm, 0), (dm, BF16, dm, 0)])
    dzg_b = _lin(dt_glu, w_glu, name="s5_glu_bwd_in", trans_w=True, tm=1024, tn=dm, o_dtype=BF16)
    dw_glu = _lin_w(zg, dt_glu, name="s5_glu_bwd_w", ta=dm, tn=dm, tkr=tkr)

    def gelu_bwd(rv, pv, cv, il):
        _, vjp = jax.vjp(_gelu, rv[0] + rv[1])
        return [vjp(rv[2] + rv[3])[0]], [], []

    (dy_s5,) = _rowk(gelu_bwd, name="s5_gelu_bwd", n=n, tm=dims["tm_row"], nctx=nctx,
                     rows=[(y_s5a, dm, 0, 0), (y_s5b, dm, 0, 0), (dzg_a, dm, 0, 0), (dzg_b, dm, 0, 0)],
                     out_rows=[(dm, F32, dm, 0)])
    dcw_both = jnp.stack([dcw_l0, dcw_l1], axis=1)
    s5g, (p_up0, p_dn0, p_cw, p_s5out, p_glu, p_hgin) = _s5_bwd(
        u_s5, dy_s5, *s5p["af"], *s5p["ab"], s5p["bb"], s5p["cc"], dsk, st_r, st_i, name="s5_bwd", dims=dims,
        xch=scatter([dwu_l0, dwd_l0.reshape(NDEV, -1, dm), dcw_both, dw_s5out.reshape(NDEV, -1, dm),
                     dw_glu.reshape(NDEV, -1, dm), dw_hgin]))
    (du_s5,) = _rowk(lambda rv, pv, cv, il: ([rv[0] + rv[1]], [], []), name="s5_du", n=n, tm=dims["tm_row"], nctx=nctx,
                     rows=[(s5g[0], dm, 0, 0), (s5g[1], dm, 0, 0)], out_rows=[(dm, BF16, dm, 0)])
    ddsk = s5g[14]
    dh0 = _lin(du_s5, w_s5in, name="s5_in_bwd_in", trans_w=True, tm=1024, tn=dm, o_dtype=BF16)
    dw_s5in = _lin_w(h0, du_s5, name="s5_in_bwd_w", ta=dm, tn=dm, tkr=tkr)
    dz0, dsh1_l0, dsc1_l0, dn1w_l0 = _norm_mod_bwd(dh0, z0, dz, n1w[0], pat[0][1], name="norm1_bwd_l0", dims=dims)

    dlr, dli, dls, g_b_re, g_b_im, g_c_re, g_c_im = _s5_param_grads(s5p, s5g[2:6], s5g[6:10], s5g[10:14])

    dmod = jnp.stack([
        jnp.concatenate([dsh1_l0, dsc1_l0, dgate1_l0, dsh2_l0, dsc2_l0, dgate2_l0], axis=1),
        jnp.concatenate([dsh1_l1, dsc1_l1, dgate1_l1, dsh2_l1, dsc2_l1, dgate2_l1], axis=1)])
    dl_hg = jnp.stack([dl_f[:, 0], dl_b[:, 0]])
    wide = lambda g: g.reshape(-1, dm)
    small = [("norm1_w", jnp.concatenate([dn1w_l0, dn1w_l1])),
             ("norm2_w", jnp.concatenate([dn2w_l0, dn2w_l1])), ("final_norm_w", dfinal_w),
             ("s5_lam_re", dlr.reshape(-1, nstate)), ("s5_lam_im", dli.reshape(-1, nstate)),
             ("s5_log_step", dls.reshape(2, ngrp)),
             ("s5_b_re", wide(g_b_re.astype(BF16))), ("s5_b_im", wide(g_b_im.astype(BF16))),
             ("s5_c_re", wide(g_c_re.astype(BF16))), ("s5_c_im", wide(g_c_im.astype(BF16))), ("s5_d", ddsk),
             ("hg_gnorm_w", dgnw), ("ffn_conv_b", jnp.stack([dcb_l0.reshape(-1), dcb_l1.reshape(-1)]))]
    tail = _exchange([dmod, dw_s5in.reshape(NDEV, -1, dm)] + [g for _, g in small] + [wide(dl_hg), loss_part],
                     a2a=[False, True] + [False] * (len(small) + 2), name="gather_tail")
    dmod_g, p_s5in, gathered = tail[0], tail[1], tail[2:]
    dlat_full = jnp.transpose(dmod_g[:, :, :bsz], (1, 0, 2, 3)).reshape(2, NDEV * bsz, N_MOD * dm)
    dctx_full = dmod_g[:, :, bsz]
    dlat_sh = lax.dynamic_slice(dlat_full, (0, 0, me * nsm), (2, NDEV * bsz, nsm))
    dctx_sh = lax.dynamic_slice(dctx_full, (0, 0, me * nsm), (NDEV, 2, nsm))
    g_w_mod, g_b_mod, dcctx8 = _mod_bwd(craw, w_mod, dlat_sh, dctx_sh, dlat_full, dctx_full)

    (g_cctx,) = _exchange([wide(dcctx8[:1])], a2a=False, name="gather_cctx")
    small, gathered = [("c_ctx", wide(dcctx8[:1]))] + small, [g_cctx] + gathered
    res = {}
    for (k, g), parts in zip(small, gathered):
        w2, m2, v2 = (given[p + k].reshape(g.shape) for p in ("", "m_", "v_"))
        res[k] = tuple(o.reshape(given[k].shape) for o in _adamw(parts, w2, m2, v2, name="adamw_" + k))

    def total(parts, name):
        z = jnp.zeros(parts.shape[1:], F32)
        return _adamw(parts, z, z, z, name=name)[0]

    loss = jnp.sum(total(gathered[-1], "sum_loss"))
    dl_tot = total(gathered[-2], "sum_dlb").reshape(dl_hg.shape)
    nlb = hg_lower_bounds.shape[2]
    g_lb = lax.dynamic_slice(dl_tot, (0, 0, me * nlb), (2, 2, nlb))

    def adam_local(name, g, shape2):
        w, m, v = given[name], given["m_" + name], given["v_" + name]
        out = _adamw(g.reshape((1,) + shape2), w.reshape(shape2), m.reshape(shape2), v.reshape(shape2),
                     name="adamw_" + name)
        return tuple(o.reshape(w.shape) for o in out)

    def adam_parts(name, p):
        w, m, v = given[name], given["m_" + name], given["v_" + name]
        shape2 = (p.shape[0], -1, w.shape[-1])
        p3 = p.reshape(shape2)
        s2 = p3.shape[1:]
        out = _adamw(p3, w.reshape(s2), m.reshape(s2), v.reshape(s2), name="adamw_" + name)
        return tuple(o.reshape(w.shape) for o in out)

    res["hg_lower_bounds"] = adam_local("hg_lower_bounds", g_lb, (2 * 2, nlb))
    res["w_mod"] = adam_local("w_mod", g_w_mod, (2 * dm, nsm))
    res["b_mod"] = adam_local("b_mod", g_b_mod, (2, N_MOD * dm))
    res["s5_w_in"] = adam_parts("s5_w_in", p_s5in)
    res["s5_w_glu"] = adam_parts("s5_w_glu", p_glu)
    res["s5_w_out"] = adam_parts("s5_w_out", p_s5out)
    res["hg_w_in"] = adam_parts("hg_w_in", p_hgin)
    res["hg_w_out"] = adam_parts("hg_w_out", p_hgout)
    res["ffn_w_up"] = adam_parts("ffn_w_up", jnp.stack([p_up0, p_up1], axis=1))
    res["ffn_w_down"] = adam_parts("ffn_w_down", jnp.stack([p_dn0, p_dn1], axis=1))
    res["ffn_conv_w"] = adam_parts("ffn_conv_w", p_cw)

    grad_x = jnp.transpose(dz0[nctx:].reshape(lx, bsz, dm), (1, 0, 2))
    order = ["c_ctx", "w_mod", "b_mod", "norm1_w", "norm2_w", "final_norm_w", "s5_w_in", "s5_lam_re", "s5_lam_im",
             "s5_log_step", "s5_b_re", "s5_b_im", "s5_c_re", "s5_c_im", "s5_d", "s5_w_glu", "s5_w_out", "hg_w_in",
             "hg_lower_bounds", "hg_gnorm_w", "hg_w_out", "ffn_w_up", "ffn_conv_w", "ffn_conv_b", "ffn_w_down"]
    outs = [loss, grad_x]
    for j in range(4):
        outs += [res[k][j].reshape(given[k].shape) for k in order]
    return tuple(outs)
```

```python
import functools

import jax
import jax.numpy as jnp
from jax import lax
from jax.experimental import pallas as pl
from jax.experimental.pallas import tpu as pltpu

F32 = jnp.float32
BF16 = jnp.bfloat16
NDEV = 8
LOCAL_B = 4
NORM_EPS = 1e-6
N_MOD = 6
S5_GROUP = 16
S5_STATE = 64
S5_LAM_RE_MAX = -1e-4
HG_HEAD = 128
HG_ROWS = 128
GRID_W = 64
ADAM_LR, ADAM_B1, ADAM_B2, ADAM_EPS, ADAM_WD, ADAM_STEP = 0.001, 0.9, 0.999, 1e-08, 0.01, 10
VMEM_BYTES_V7X = 64 * 1024 * 1024
LANES = 128
SUBLANES = 8

NN = (((1,), (0,)), ((), ()))
NT = (((1,), (1,)), ((), ()))
TN = (((0,), (0,)), ((), ()))
MESH = pl.DeviceIdType.MESH


def _params(sem=None, vmem=None):
    kw = {}
    if sem is not None:
        kw["dimension_semantics"] = sem
    if vmem is not None:
        kw["vmem_limit_bytes"] = int(min(vmem, VMEM_BYTES_V7X - (4 << 20)))
    return pltpu.CompilerParams(**kw)


def _nbytes(shape, dtype):
    n = 1
    for s in shape:
        n *= 1 if s is None else s
    return n * jnp.dtype(dtype).itemsize


def _dot(a, b, dims=NN, precision=None):
    return lax.dot_general(a, b, dims, preferred_element_type=F32, precision=precision)


def _sigmoid(x):
    return 1.0 / (1.0 + jnp.exp(-x))


class _Xchg:
    def __init__(self, arrs, a2a):
        self.arrs, self.a2a, self.n = list(arrs), list(a2a), len(arrs)

    def out_shape(self):
        return [jax.ShapeDtypeStruct(a.shape if f else (NDEV,) + a.shape, a.dtype) for a, f in zip(self.arrs, self.a2a)]

    def scratch(self):
        return [pltpu.SemaphoreType.DMA((self.n * (NDEV - 1),)), pltpu.SemaphoreType.DMA((self.n * (NDEV - 1),)),
                pltpu.SemaphoreType.DMA((self.n,))]

    def _copies(self, ins, outs, sems, with_recvs):
        send_sems, recv_sems, loc_sems = sems
        x, y, c = lax.axis_index("x"), lax.axis_index("y"), lax.axis_index("c")
        me = 4 * x + 2 * y + c
        local, sends, recvs = [], [], []
        for a in range(self.n):
            src = ins[a].at[me] if self.a2a[a] else ins[a]
            local.append(pltpu.make_async_copy(src, outs[a].at[me], loc_sems.at[a]))
            for k in range(1, NDEV):
                px = (1 - x) if (k >> 2) & 1 else x
                py = (1 - y) if (k >> 1) & 1 else y
                pc = (1 - c) if k & 1 else c
                p = 4 * px + 2 * py + pc
                s = a * (NDEV - 1) + k - 1
                src = ins[a].at[p] if self.a2a[a] else ins[a]
                kw = dict(src_ref=src, send_sem=send_sems.at[s], recv_sem=recv_sems.at[s], device_id=(px, py, pc),
                          device_id_type=MESH)
                sends.append(pltpu.make_async_remote_copy(dst_ref=outs[a].at[me], **kw))
                if with_recvs:
                    recvs.append(pltpu.make_async_remote_copy(dst_ref=outs[a].at[p], **kw))
        return local, sends, recvs

    def start(self, ins, outs, sems):
        local, sends, _ = self._copies(ins, outs, sems, False)
        for cp in local + sends:
            cp.start()

    def wait(self, ins, outs, sems):
        local, sends, recvs = self._copies(ins, outs, sems, True)
        for cp in sends:
            cp.wait_send()
        for cp in recvs:
            cp.wait_recv()
        for cp in local:
            cp.wait()


def _exchange(arrs, *, a2a, name):
    xch = _Xchg(arrs, a2a if isinstance(a2a, (list, tuple)) else [a2a] * len(arrs))
    n = xch.n

    def body(*refs):
        xch.start(refs[:n], refs[n:2 * n], refs[2 * n:])
        xch.wait(refs[:n], refs[n:2 * n], refs[2 * n:])

    res = pl.pallas_call(
        body, name=name, out_shape=xch.out_shape(),
        in_specs=[pl.BlockSpec(memory_space=pl.ANY)] * n, out_specs=[pl.BlockSpec(memory_space=pl.ANY)] * n,
        scratch_shapes=xch.scratch(),
    )(*arrs)
    return list(res)


def _call(body, *, name, out_shape, grid, in_specs, out_specs, scratch, params, args, xch=None):
    in_specs, out_specs, out_shape, scratch, args = list(in_specs), list(out_specs), list(out_shape), list(scratch), list(args)
    n_in, n_out, n_scr = len(in_specs), len(out_shape), len(scratch)
    if xch is not None:
        k = xch.n
        inner = body

        def body(*refs):
            ins, xin = refs[:n_in], refs[n_in:n_in + k]
            outs, xout = refs[n_in + k:n_in + k + n_out], refs[n_in + k + n_out:n_in + 2 * k + n_out]
            scr = refs[n_in + 2 * k + n_out:n_in + 2 * k + n_out + n_scr]
            sems = refs[n_in + 2 * k + n_out + n_scr:]
            first = pl.program_id(0) == 0
            last = pl.program_id(0) == grid[0] - 1
            for ax in range(1, len(grid)):
                first = jnp.logical_and(first, pl.program_id(ax) == 0)
                last = jnp.logical_and(last, pl.program_id(ax) == grid[ax] - 1)

            @pl.when(first)
            def _():
                xch.start(xin, xout, sems)

            inner(*ins, *outs, *scr)

            @pl.when(last)
            def _():
                xch.wait(xin, xout, sems)

        anyspec = pl.BlockSpec(memory_space=pl.ANY)
        in_specs += [anyspec] * k
        out_specs += [anyspec] * k
        out_shape += xch.out_shape()
        scratch += xch.scratch()
        args += xch.arrs
    res = pl.pallas_call(body, name=name, out_shape=out_shape, grid=grid, in_specs=in_specs, out_specs=out_specs,
                         scratch_shapes=scratch, compiler_params=params)(*args)
    return list(res[:n_out]), list(res[n_out:])


def _mm(a, b, *, name, grid, a_spec, b_spec, o_spec, o_shape, o_dtype, dims, base=None):
    nk = grid[2]
    o_block = tuple(s for s in o_spec.block_shape if s is not None)

    def body(a_ref, b_ref, *rest):
        base_ref = rest[0] if base is not None else None
        o_ref, scr = rest[1 if base is not None else 0], rest[2 if base is not None else 1:]
        r = _dot(a_ref[...].astype(BF16), b_ref[...].astype(BF16), dims)
        if nk == 1:
            if base is not None:
                r = r + base_ref[...].astype(F32)
            o_ref[...] = r.astype(o_dtype)
        else:
            acc = scr[0]
            k = pl.program_id(2)

            @pl.when(k == 0)
            def _():
                acc[...] = r

            @pl.when(k > 0)
            def _():
                acc[...] += r

            @pl.when(k == nk - 1)
            def _():
                tot = acc[...] if base is None else acc[...] + base_ref[...].astype(F32)
                o_ref[...] = tot.astype(o_dtype)

    blocks = (_nbytes(a_spec.block_shape, a.dtype) + _nbytes(b_spec.block_shape, b.dtype) + _nbytes(o_block, o_dtype)
              + (_nbytes(o_block, base.dtype) if base is not None else 0))
    scratch = [pltpu.VMEM(o_block, F32)] if nk > 1 else []
    vmem = 2 * blocks + 3 * _nbytes(o_block, F32) + (8 << 20)
    return pl.pallas_call(
        body, name=name, out_shape=jax.ShapeDtypeStruct(o_shape, o_dtype), grid=grid,
        in_specs=[a_spec, b_spec] + ([o_spec] if base is not None else []), out_specs=o_spec, scratch_shapes=scratch,
        compiler_params=_params(("parallel", "parallel", "arbitrary"), vmem),
    )(a, b, *([base] if base is not None else []))


def _lin(a, w, *, name, trans_w=False, tm, tn, o_dtype=F32, kblk=0, base=None):
    m, kk = a.shape
    nout = w.shape[0] if trans_w else w.shape[1]
    if trans_w:
        b_spec = pl.BlockSpec((tn, kk), lambda j, i, k: (j, kblk))
    else:
        b_spec = pl.BlockSpec((kk, tn), lambda j, i, k: (0, j))
    return _mm(a, w, name=name, grid=(nout // tn, m // tm, 1), dims=NT if trans_w else NN, o_shape=(m, nout),
               o_dtype=o_dtype, o_spec=pl.BlockSpec((tm, tn), lambda j, i, k: (i, j)),
               a_spec=pl.BlockSpec((tm, kk), lambda j, i, k: (i, 0)), b_spec=b_spec, base=base)


def _lin_w(a, dy, *, name, ta, tn, tkr):
    m, ka = a.shape
    nout = dy.shape[1]
    return _mm(a, dy, name=name, grid=(ka // ta, nout // tn, m // tkr), dims=TN, o_shape=(ka, nout), o_dtype=BF16,
               o_spec=pl.BlockSpec((ta, tn), lambda i, j, k: (i, j)),
               a_spec=pl.BlockSpec((tkr, ta), lambda i, j, k: (k, i)),
               b_spec=pl.BlockSpec((tkr, tn), lambda i, j, k: (k, j)))


def _rowk(fn, *, name, n, tm, nctx, rows=(), pats=(), consts=(), out_rows=(), out_seg=(), out_acc=()):
    nb, ncb = n // tm, nctx // tm
    nr, npat, ncst = len(rows), len(pats), len(consts)
    no, nseg, nacc = len(out_rows), len(out_seg), len(out_acc)
    in_specs, blocks = [], 0
    for arr, w, cb, off in rows:
        in_specs.append(pl.BlockSpec((tm, w), lambda i, cb=cb, off=off: (jnp.maximum(i - off, 0), cb)))
        blocks += _nbytes((tm, w), arr.dtype)
    for p in pats:
        in_specs.append(pl.BlockSpec((None, SUBLANES, p.shape[2]), lambda i: (jnp.where(i >= ncb, 1, 0), 0, 0)))
    for cst in consts:
        in_specs.append(pl.BlockSpec(cst.shape, lambda i: (0, 0)))
    out_shape, out_specs = [], []
    for wt, dt, w, cb in out_rows:
        out_shape.append(jax.ShapeDtypeStruct((n, wt), dt))
        out_specs.append(pl.BlockSpec((tm, w), lambda i, cb=cb: (i, cb)))
        blocks += _nbytes((tm, w), dt)
    for w in out_seg:
        out_shape.append(jax.ShapeDtypeStruct((SUBLANES, w), F32))
        out_specs.append(pl.BlockSpec((SUBLANES, w), lambda i: (0, 0)))
    for w in out_acc:
        out_shape.append(jax.ShapeDtypeStruct((1, w), F32))
        out_specs.append(pl.BlockSpec((1, w), lambda i: (0, 0)))
    scratch = [pltpu.VMEM((2, SUBLANES, w), F32) for w in out_seg] + [pltpu.VMEM((SUBLANES, w), F32) for w in out_acc]

    def body(*refs):
        r_in = refs[:nr]
        p_in = refs[nr:nr + npat]
        c_in = refs[nr + npat:nr + npat + ncst]
        base = nr + npat + ncst
        o_rows = refs[base:base + no]
        o_seg = refs[base + no:base + no + nseg]
        o_acc = refs[base + no + nseg:base + no + nseg + nacc]
        s_seg = refs[base + no + nseg + nacc:base + no + nseg + nacc + nseg]
        s_acc = refs[base + no + nseg + nacc + nseg:]
        i = pl.program_id(0)
        rv = [r[...].astype(F32).reshape(tm // SUBLANES, SUBLANES, r.shape[1]) for r in r_in]
        pv = [p[...] for p in p_in]
        cv = [c[...] for c in c_in]
        is_lat = (i >= ncb).astype(F32)
        ro, so, ao = fn(rv, pv, cv, is_lat)
        for ref, val in zip(o_rows, ro):
            ref[...] = val.reshape(tm, ref.shape[1]).astype(ref.dtype)
        if nseg or nacc:
            @pl.when(i == 0)
            def _():
                for s in list(s_seg) + list(s_acc):
                    s[...] = jnp.zeros(s.shape, F32)

            seg = jnp.where(i >= ncb, 1, 0)
            for s, val in zip(s_seg, so):
                s[seg] = s[seg] + val
            for s, val in zip(s_acc, ao):
                s[...] = s[...] + val

            @pl.when(i == nb - 1)
            def _():
                for o, s in zip(o_seg, s_seg):
                    lat, ctx = s[1], s[0]
                    row = lax.broadcasted_iota(jnp.int32, lat.shape, 0)
                    lat = lat + pltpu.roll(lat, 4, 0)
                    ctx = jnp.broadcast_to(jnp.sum(ctx, axis=0, keepdims=True), lat.shape)
                    o[...] = jnp.where(row < 4, lat, jnp.where(row == 4, ctx, 0.0))
                for o, s in zip(o_acc, s_acc):
                    o[...] = jnp.sum(s[...], axis=0, keepdims=True)

    vmem = 2 * blocks + 8 * tm * 1024 * 4 + (8 << 20)
    res = pl.pallas_call(
        body, name=name, out_shape=out_shape, grid=(nb,), in_specs=in_specs, out_specs=out_specs,
        scratch_shapes=scratch, compiler_params=_params(("arbitrary",), vmem),
    )(*[r[0] for r in rows], *pats, *consts)
    return list(res)


def _rms(z):
    return lax.rsqrt(jnp.mean(z * z, axis=-1, keepdims=True) + NORM_EPS)


def _norm_mod_fwd(z, w, sh, sc, *, name, dims, res=None):
    n, d = z.shape

    def fn(rv, pv, cv, is_lat):
        zz = rv[0]
        if res is not None:
            zz = zz + pv[2][None] * rv[1]
        h = (zz * _rms(zz) * cv[0]) * (1.0 + pv[1][None]) + pv[0][None]
        return ([zz, h] if res is not None else [h]), [], []

    rows = [(z, d, 0, 0)] + ([(res[0], d, 0, 0)] if res is not None else [])
    pats = [sh, sc] + ([res[1]] if res is not None else [])
    outs = ([(d, F32, d, 0)] if res is not None else []) + [(d, BF16, d, 0)]
    out = _rowk(fn, name=name, n=n, tm=dims["tm_row"], nctx=dims["nctx"], rows=rows, pats=pats, consts=[w],
                out_rows=outs)
    return (out[0], out[1]) if res is not None else (None, out[0])


def _norm_core_bwd(zin, dh, w, sc):
    r = _rms(zin)
    xh = zin * r
    dsh = jnp.sum(dh, axis=0)
    dsc = jnp.sum(dh * (xh * w), axis=0)
    dyv = dh * (1.0 + sc[None])
    dw = jnp.sum(dyv * xh, axis=0)
    dxh = dyv * w
    dx = r * (dxh - xh * jnp.mean(dxh * xh, axis=-1, keepdims=True))
    return dx, dsh, dsc, dw


def _norm_mod_bwd(dh, zin, dz_up, w, sc, *, name, dims, res=None):
    n, d = zin.shape

    def fn(rv, pv, cv, is_lat):
        dx, dsh, dsc, dw = _norm_core_bwd(rv[1], rv[0], cv[0], pv[0])
        dz = rv[2] + dx
        if res is None:
            return [dz], [dsh, dsc], [dw]
        return [dz, dz * pv[1][None]], [dsh, dsc, jnp.sum(dz * rv[3], axis=0)], [dw]

    rows = [(dh, d, 0, 0), (zin, d, 0, 0), (dz_up, d, 0, 0)] + ([(res[0], d, 0, 0)] if res is not None else [])
    pats = [sc] + ([res[1]] if res is not None else [])
    outs = [(d, F32, d, 0)] + ([(d, BF16, d, 0)] if res is not None else [])
    return _rowk(fn, name=name, n=n, tm=dims["tm_row"], nctx=dims["nctx"], rows=rows, pats=pats, consts=[w],
                 out_rows=outs, out_seg=[d] * (3 if res is not None else 2), out_acc=[d])


def _loss_bwd(z1, f, gate, tgt, w, *, name, dims):
    n, d = z1.shape

    def fn(rv, pv, cv, is_lat):
        z2 = rv[0] + pv[0][None] * rv[1]
        r = _rms(z2)
        xh = z2 * r
        err = (xh * cv[0] - rv[2]) * is_lat
        dout = err * (1.0 / d)
        dxh = dout * cv[0]
        dz = r * (dxh - xh * jnp.mean(dxh * xh, axis=-1, keepdims=True))
        return ([dz, dz * pv[0][None]], [jnp.sum(dz * rv[1], axis=0)],
                [jnp.sum(0.5 * err * err * (1.0 / d), axis=0), jnp.sum(dout * xh, axis=0)])

    tm = dims["tm_row"]
    rows = [(z1, d, 0, 0), (f, d, 0, 0), (tgt, d, 0, dims["nctx"] // tm)]
    return _rowk(fn, name=name, n=n, tm=tm, nctx=dims["nctx"], rows=rows, pats=[gate], consts=[w],
                 out_rows=[(d, F32, d, 0), (d, BF16, d, 0)], out_seg=[d], out_acc=[d, d])


def _gelu(y):
    return jax.nn.gelu(y, approximate=True)


def _conv_masks(tb, i):
    tok = lax.broadcasted_iota(jnp.int32, (tb, 1), 0) >> 2
    last = jnp.where(i == 0, tb // LOCAL_B - 1, GRID_W - 1)
    wpos = tok & last
    return wpos == 0, wpos == last


CONV_LANES = 2 * LANES


def _conv_taps(u_ref, cw_ref, cb_ref, no_left, no_right, tb):
    uu = u_ref[...].astype(F32)
    ul = jnp.where(no_left, 0.0, pltpu.roll(uu, LOCAL_B, 0))
    ur = jnp.where(no_right, 0.0, pltpu.roll(uu, tb - LOCAL_B, 0))
    val = cb_ref[...] + ul * cw_ref[pl.ds(0, 1), :] + uu * cw_ref[pl.ds(1, 1), :] + ur * cw_ref[pl.ds(2, 1), :]
    return val, ul, uu, ur


def _convffn_specs(tb, nj):
    cl = CONV_LANES
    return [pl.BlockSpec((tb, cl), lambda j, i: (i, j)), pl.BlockSpec((tb, cl), lambda j, i: (i, nj + j)),
            pl.BlockSpec((3, cl), lambda j, i: (0, j)), pl.BlockSpec((3, cl), lambda j, i: (0, nj + j)),
            pl.BlockSpec((1, cl), lambda j, i: (0, j)), pl.BlockSpec((1, cl), lambda j, i: (0, nj + j))]


def _convffn_fwd(u, cw, cb, *, name, dims):
    n, f2 = u.shape
    tb, nj = dims["nctx"], f2 // 2 // CONV_LANES

    def body(ua_ref, ug_ref, cwa_ref, cwg_ref, cba_ref, cbg_ref, o_ref):
        no_left, no_right = _conv_masks(tb, pl.program_id(1))
        a = _conv_taps(ua_ref, cwa_ref, cba_ref, no_left, no_right, tb)[0]
        g = _conv_taps(ug_ref, cwg_ref, cbg_ref, no_left, no_right, tb)[0]
        o_ref[...] = (a * _sigmoid(a) * g).astype(BF16)

    vmem = 16 * tb * CONV_LANES * 4 + (8 << 20)
    return pl.pallas_call(
        body, name=name, out_shape=jax.ShapeDtypeStruct((n, f2 // 2), BF16), grid=(nj, n // tb),
        in_specs=_convffn_specs(tb, nj), out_specs=pl.BlockSpec((tb, CONV_LANES), lambda j, i: (i, j)),
        compiler_params=_params(("parallel", "arbitrary"), vmem),
    )(u, u, cw, cw, cb, cb)


def _convffn_bwd(u, dhm, cw, cb, *, name, dims, xch=None):
    n, f2 = u.shape
    tb, nj = dims["nctx"], f2 // 2 // CONV_LANES

    def body(ua_ref, ug_ref, cwa_ref, cwg_ref, cba_ref, cbg_ref, dh_ref, dua_ref, dug_ref, dcwa_ref, dcwg_ref, dcba_ref,
             dcbg_ref):
        i = pl.program_id(1)
        no_left, no_right = _conv_masks(tb, i)

        @pl.when(i == 0)
        def _():
            for ref in (dcwa_ref, dcwg_ref, dcba_ref, dcbg_ref):
                ref[...] = jnp.zeros(ref.shape, F32)

        a, al, ac, ar = _conv_taps(ua_ref, cwa_ref, cba_ref, no_left, no_right, tb)
        g, gl, gc, gr = _conv_taps(ug_ref, cwg_ref, cbg_ref, no_left, no_right, tb)
        dh = dh_ref[...].astype(F32)
        sa = _sigmoid(a)
        dg = dh * (a * sa)
        da = dh * g * (sa * (1.0 + a * (1.0 - sa)))
        for dc, (tl, tc, tr), cw_ref, du_ref, dcw_ref, dcb_ref in (
                (da, (al, ac, ar), cwa_ref, dua_ref, dcwa_ref, dcba_ref),
                (dg, (gl, gc, gr), cwg_ref, dug_ref, dcwg_ref, dcbg_ref)):
            dcb_ref[...] += jnp.sum(dc, axis=0, keepdims=True)
            dcw_ref[pl.ds(0, 1), :] += jnp.sum(dc * tl, axis=0, keepdims=True)
            dcw_ref[pl.ds(1, 1), :] += jnp.sum(dc * tc, axis=0, keepdims=True)
            dcw_ref[pl.ds(2, 1), :] += jnp.sum(dc * tr, axis=0, keepdims=True)
            du = (dc * cw_ref[pl.ds(1, 1), :]
                  + pltpu.roll(jnp.where(no_left, 0.0, dc) * cw_ref[pl.ds(0, 1), :], tb - LOCAL_B, 0)
                  + pltpu.roll(jnp.where(no_right, 0.0, dc) * cw_ref[pl.ds(2, 1), :], LOCAL_B, 0))
            du_ref[...] = du.astype(BF16)

    cl, f = CONV_LANES, f2 // 2
    sd = jax.ShapeDtypeStruct
    row = pl.BlockSpec((tb, cl), lambda j, i: (i, j))
    vmem = 24 * tb * cl * 4 + (8 << 20)
    return _call(
        body, name=name, xch=xch, args=[u, u, cw, cw, cb, cb, dhm], scratch=[],
        out_shape=[sd((n, f), BF16), sd((n, f), BF16), sd((3, f), F32), sd((3, f), F32), sd((1, f), F32), sd((1, f), F32)],
        grid=(nj, n // tb), in_specs=_convffn_specs(tb, nj) + [row],
        out_specs=[row, row, pl.BlockSpec((3, cl), lambda j, i: (0, j)), pl.BlockSpec((3, cl), lambda j, i: (0, j)),
                   pl.BlockSpec((1, cl), lambda j, i: (0, j)), pl.BlockSpec((1, cl), lambda j, i: (0, j))],
        params=_params(("arbitrary", "arbitrary"), vmem))


def _s5_disc(lr, li, ls, brt, bit):
    lr = jnp.minimum(lr, S5_LAM_RE_MAX)
    dt = jnp.exp(ls)
    mag = jnp.exp(lr * dt)
    ar = mag * jnp.cos(li * dt)
    ai = mag * jnp.sin(li * dt)
    den = lr * lr + li * li
    nr = ar - 1.0
    cr = (nr * lr + ai * li) / den
    ci = (ai * lr - nr * li) / den
    return ar, ai, cr * brt - ci * bit, cr * bit + ci * brt


def _s5_disc_fwd(lr, li, ls, brt, bit):
    def body(lr_ref, li_ref, ls_ref, br_ref, bi_ref, ar_ref, ai_ref, bbr_ref, bbi_ref):
        ar, ai, bbr, bbi = _s5_disc(lr_ref[...], li_ref[...], ls_ref[...], br_ref[...], bi_ref[...])
        ar_ref[...] = ar
        ai_ref[...] = ai
        bbr_ref[...] = bbr
        bbi_ref[...] = bbi

    sd = jax.ShapeDtypeStruct
    return pl.pallas_call(body, name="s5_disc_fwd",
                          out_shape=[sd(lr.shape, F32), sd(lr.shape, F32), sd(brt.shape, F32), sd(brt.shape, F32)],
                          compiler_params=_params(None, 32 << 20))(lr, li, ls, brt, bit)


def _s5_disc_bwd(lr, li, ls, brt, bit, dar, dai, dbbr, dbbi):
    def body(lr_ref, li_ref, ls_ref, br_ref, bi_ref, dar_ref, dai_ref, dbbr_ref, dbbi_ref,
             dlr_ref, dli_ref, dls_ref, dbr_ref, dbi_ref):
        _, vjp = jax.vjp(_s5_disc, lr_ref[...], li_ref[...], ls_ref[...], br_ref[...], bi_ref[...])
        dlr, dli, dls, dbr, dbi = vjp((dar_ref[...], dai_ref[...], dbbr_ref[...], dbbi_ref[...]))
        dlr_ref[...] = dlr
        dli_ref[...] = dli
        dls_ref[...] = dls
        dbr_ref[...] = dbr
        dbi_ref[...] = dbi

    sd = jax.ShapeDtypeStruct
    return pl.pallas_call(body, name="s5_disc_bwd",
                          out_shape=[sd(lr.shape, F32), sd(lr.shape, F32), sd(ls.shape, F32), sd(brt.shape, F32),
                                     sd(brt.shape, F32)],
                          compiler_params=_params(None, 48 << 20))(lr, li, ls, brt, bit, dar, dai, dbbr, dbbi)


def _cmul(ar, ai, xr, xi):
    return ar * xr - ai * xi, ar * xi + ai * xr


def _s5_chunk_of(step, ncc, nc, rev):
    if not rev:
        return step
    return jnp.where(step < ncc, ncc - 1 - step, nc - 1 - (step - ncc))


def _s5_scan2(asc, desc, row0, nrows, a_r_ref, a_i_ref, cr_ref, ci_ref, *, lane_block, extra=None):
    width = asc[0].shape[1]
    nt = nrows // SUBLANES
    for lb in range(width // lane_block):
        lanes = pl.ds(lb * lane_block, lane_block)
        a1r, a1i = a_r_ref[:, lanes], a_i_ref[:, lanes]
        a2r, a2i = pltpu.roll(a1r, 4, 0), pltpu.roll(a1i, 4, 0)
        lo = lax.broadcasted_iota(jnp.int32, a1r.shape, 0) < 4

        def step(t, carry):
            pr, pi = carry[0], carry[1]
            ra = pl.ds(pl.multiple_of(row0 + t * SUBLANES, SUBLANES), SUBLANES)
            rd = pl.ds(pl.multiple_of(row0 + (nt - 1 - t) * SUBLANES, SUBLANES), SUBLANES)
            ur, ui = asc[0][ra, lanes], asc[1][ra, lanes]
            dr, di = desc[0][rd, lanes], desc[1][rd, lanes]
            mr, mi = _cmul(a1r, a1i, pr, pi)
            y1r, y1i = jnp.where(lo, ur, dr) + mr, jnp.where(lo, ui, di) + mi
            mr, mi = _cmul(a2r, a2i, pltpu.roll(y1r, 4, 0), pltpu.roll(y1i, 4, 0))
            y2r, y2i = jnp.where(lo, dr, ur) + mr, jnp.where(lo, di, ui) + mi
            our, oui = jnp.where(lo, y1r, y2r), jnp.where(lo, y1i, y2i)
            odr, odi = jnp.where(lo, y2r, y1r), jnp.where(lo, y2i, y1i)
            asc[0][ra, lanes] = our
            asc[1][ra, lanes] = oui
            desc[0][rd, lanes] = odr
            desc[1][rd, lanes] = odi
            nxt = (pltpu.roll(y2r, 4, 0), pltpu.roll(y2i, 4, 0))
            if extra is None:
                return nxt
            return nxt + tuple(extra(t, nt - 1 - t, lanes, (our, oui), (odr, odi), carry[2:]))

        init = (cr_ref[:, lanes], ci_ref[:, lanes])
        if extra is not None:
            init = init + tuple(extra.init(lanes))
        out = lax.fori_loop(0, nt, step, init)
        cr_ref[:, lanes] = out[0]
        ci_ref[:, lanes] = out[1]
        if extra is not None:
            extra.done(lanes, out[2:])


S5_SPLIT = 2


def _s5_fwd(u, af_r, af_i, bb, cc, dsk, *, name, dims, xch=None):
    n, dm = u.shape
    nk, swk = bb[0].shape[0], bb[0].shape[2]
    rr, sw = dims["s5_rows"], nk * swk
    nkh, dmh, swh = nk // S5_SPLIT, dm // S5_SPLIT, sw // S5_SPLIT
    nc, ncc = n // rr, dims["nctx"] // rr
    c1 = lambda i: _s5_chunk_of(i, ncc, nc, True)

    def body(u0_ref, u1_ref, afr_ref, afi_ref, b0r, b0i, b1r, b1i, c0r, c0i, c1r, c1i, dsk_ref,
             y0_ref, y1_ref, str_ref, sti_ref, s0r, s0i, s1r, s1i, cr, ci):
        @pl.when(pl.program_id(1) == 0)
        def _():
            cr[...] = jnp.zeros(cr.shape, F32)
            ci[...] = jnp.zeros(ci.shape, F32)

        str_ref[...] = cr[...]
        sti_ref[...] = ci[...]
        ub0, ub1 = u0_ref[...].astype(BF16), u1_ref[...].astype(BF16)
        for k in range(nkh):
            cols, sl = slice(k * LANES, (k + 1) * LANES), slice(k * swk, (k + 1) * swk)
            s0r[:, sl] = _dot(ub0[:, cols], b0r[k])
            s0i[:, sl] = _dot(ub0[:, cols], b0i[k])
            s1r[:, sl] = _dot(ub1[:, cols], b1r[k])
            s1i[:, sl] = _dot(ub1[:, cols], b1i[k])
        _s5_scan2((s0r, s0i), (s1r, s1i), 0, rr, afr_ref, afi_ref, cr, ci, lane_block=dims["s5_lane_block"])
        for k in range(nkh):
            cols, sl = slice(k * LANES, (k + 1) * LANES), slice(k * swk, (k + 1) * swk)
            y0_ref[:, cols] = (_dot(s0r[:, sl].astype(BF16), c0r[k]) - _dot(s0i[:, sl].astype(BF16), c0i[k])
                               + dsk_ref[:, cols] * u0_ref[:, cols].astype(F32)).astype(BF16)
            y1_ref[:, cols] = (_dot(s1r[:, sl].astype(BF16), c1r[k])
                               - _dot(s1i[:, sl].astype(BF16), c1i[k])).astype(BF16)

    row0 = pl.BlockSpec((rr, dmh), lambda h, i: (i, h))
    row1 = pl.BlockSpec((rr, dmh), lambda h, i: (c1(i), h))
    tile = pl.BlockSpec((SUBLANES, swh), lambda h, i: (0, h))
    wspec = lambda a: pl.BlockSpec((nkh,) + a.shape[1:], lambda h, i: (h, 0, 0))
    st_spec = pl.BlockSpec((None, SUBLANES, swh), lambda h, i: (i, 0, h))
    sd = jax.ShapeDtypeStruct
    vmem = 4 * rr * swh * 4 + 12 * rr * dmh * 4 + 16 * nkh * LANES * swk * 2 + (12 << 20)
    return _call(
        body, name=name, xch=xch, args=[u, u, af_r, af_i, *bb, *cc, dsk],
        out_shape=[sd((n, dm), BF16), sd((n, dm), BF16), sd((nc, SUBLANES, sw), F32), sd((nc, SUBLANES, sw), F32)],
        grid=(S5_SPLIT, nc),
        in_specs=[row0, row1, tile, tile] + [wspec(a) for a in (*bb, *cc)] + [pl.BlockSpec((1, dmh), lambda h, i: (0, h))],
        out_specs=[row0, row1, st_spec, st_spec],
        scratch=[pltpu.VMEM((rr, swh), F32)] * 4 + [pltpu.VMEM((SUBLANES, swh), F32)] * 2,
        params=_params(("arbitrary", "arbitrary"), vmem))


class _DaHook2:
    def __init__(self, s0, s1, accs):
        self.s0, self.s1, self.accs = s0, s1, accs

    def init(self, lanes):
        return tuple(a[:, lanes] for a in self.accs)

    def done(self, lanes, acc):
        for a, v in zip(self.accs, acc):
            a[:, lanes] = v

    def __call__(self, t1, t0, lanes, l1, l0, acc):
        row = lax.broadcasted_iota(jnp.int32, l1[0].shape, 0)
        b1 = pl.multiple_of(SUBLANES + t1 * SUBLANES, SUBLANES)
        b0 = pl.multiple_of(SUBLANES + t0 * SUBLANES, SUBLANES)
        cur1, nxt1 = pl.ds(b1, SUBLANES), pl.ds(pl.multiple_of(b1 + SUBLANES, SUBLANES), SUBLANES)
        cur0, prv0 = pl.ds(b0, SUBLANES), pl.ds(pl.multiple_of(b0 - SUBLANES, SUBLANES), SUBLANES)
        p1r = pltpu.roll(jnp.where(row >= 4, self.s1[0][cur1, lanes], self.s1[0][nxt1, lanes]), 4, 0)
        p1i = pltpu.roll(jnp.where(row >= 4, self.s1[1][cur1, lanes], self.s1[1][nxt1, lanes]), 4, 0)
        p0r = pltpu.roll(jnp.where(row >= 4, self.s0[0][prv0, lanes], self.s0[0][cur0, lanes]), 4, 0)
        p0i = pltpu.roll(jnp.where(row >= 4, self.s0[1][prv0, lanes], self.s0[1][cur0, lanes]), 4, 0)
        return (acc[0] + p0r * l0[0] + p0i * l0[1], acc[1] + p0r * l0[1] - p0i * l0[0],
                acc[2] + p1r * l1[0] + p1i * l1[1], acc[3] + p1r * l1[1] - p1i * l1[0])


def _s5_bwd(u, dy, af_r, af_i, ab_r, ab_i, bb, cc, dsk, st_r, st_i, *, name, dims, xch=None):
    n, dm = u.shape
    nk, swk = bb[0].shape[0], bb[0].shape[2]
    rr, sw = dims["s5_rows"], nk * swk
    nkh, dmh, swh = nk // S5_SPLIT, dm // S5_SPLIT, sw // S5_SPLIT
    nc, ncc = n // rr, dims["nctx"] // rr
    f0 = lambda i: nc - 1 - i
    f1 = lambda i: _s5_chunk_of(nc - 1 - i, ncc, nc, True)

    def body(u0_ref, u1_ref, dy0_ref, dy1_ref, afr_ref, afi_ref, abr_ref, abi_ref, b0r, b0i, b1r, b1i, c0r, c0i, c1r, c1i,
             dsk_ref, str_ref, sti_ref,
             du0_ref, du1_ref, db0r, db0i, db1r, db1i, dc0r, dc0i, dc1r, dc1i, da0r_ref, da0i_ref, da1r_ref, da1i_ref, dd_ref,
             s0r, s0i, s1r, s1i, l0r, l0i, l1r, l1i, cr, ci, lcr, lci, a0r, a0i, a1r, a1i, dda):
        i = pl.program_id(1)

        @pl.when(i == 0)
        def _():
            for ref in (lcr, lci, a0r, a0i, a1r, a1i, dda, db0r, db0i, db1r, db1i, dc0r, dc0i, dc1r, dc1i):
                ref[...] = jnp.zeros(ref.shape, F32)

        row = lax.broadcasted_iota(jnp.int32, (SUBLANES, swh), 0)
        for st_ref, car, z0, z1 in ((str_ref, cr, s0r, s1r), (sti_ref, ci, s0i, s1i)):
            st = st_ref[...]
            car[...] = st
            z0[pl.ds(0, SUBLANES), :] = jnp.where(row < 4, st, pltpu.roll(st, 4, 0))
            z1[pl.ds(rr + SUBLANES, SUBLANES), :] = jnp.where(row >= 4, st, pltpu.roll(st, 4, 0))
        body_rows = pl.ds(SUBLANES, rr)
        ub0, ub1 = u0_ref[...].astype(BF16), u1_ref[...].astype(BF16)
        dyb0, dyb1 = dy0_ref[...].astype(BF16), dy1_ref[...].astype(BF16)
        for k in range(nkh):
            cols, sl = slice(k * LANES, (k + 1) * LANES), slice(k * swk, (k + 1) * swk)
            s0r[body_rows, sl] = _dot(ub0[:, cols], b0r[k])
            s0i[body_rows, sl] = _dot(ub0[:, cols], b0i[k])
            s1r[body_rows, sl] = _dot(ub1[:, cols], b1r[k])
            s1i[body_rows, sl] = _dot(ub1[:, cols], b1i[k])
        _s5_scan2((s0r, s0i), (s1r, s1i), SUBLANES, rr, afr_ref, afi_ref, cr, ci, lane_block=dims["s5_lane_block"])
        for k in range(nkh):
            cols, sl = slice(k * LANES, (k + 1) * LANES), slice(k * swk, (k + 1) * swk)
            for dyk, lr, li, sr, si, ccr, cci, dcr, dci in ((dyb0[:, cols], l0r, l0i, s0r, s0i, c0r, c0i, dc0r, dc0i),
                                                           (dyb1[:, cols], l1r, l1i, s1r, s1i, c1r, c1i, dc1r, dc1i)):
                lr[:, sl] = _dot(dyk, ccr[k], NT)
                li[:, sl] = -_dot(dyk, cci[k], NT)
                dcr[k] += _dot(dyk, sr[body_rows, sl].astype(BF16), TN)
                dci[k] -= _dot(dyk, si[body_rows, sl].astype(BF16), TN)
        _s5_scan2((l1r, l1i), (l0r, l0i), 0, rr, abr_ref, abi_ref, lcr, lci, lane_block=dims["s5_lane_block"] // 2,
                  extra=_DaHook2((s0r, s0i), (s1r, s1i), (a0r, a0i, a1r, a1i)))
        for k in range(nkh):
            cols, sl = slice(k * LANES, (k + 1) * LANES), slice(k * swk, (k + 1) * swk)
            for uk, lr, li, br, bi, dbr, dbi, du_ref, first in ((ub0[:, cols], l0r, l0i, b0r, b0i, db0r, db0i, du0_ref, True),
                                                              (ub1[:, cols], l1r, l1i, b1r, b1i, db1r, db1i, du1_ref, False)):
                lrk, lik = lr[:, sl].astype(BF16), li[:, sl].astype(BF16)
                dbr[k] += _dot(uk, lrk, TN)
                dbi[k] += _dot(uk, lik, TN)
                duk = _dot(lrk, br[k], NT) + _dot(lik, bi[k], NT)
                if first:
                    duk = duk + dsk_ref[:, cols] * dy0_ref[:, cols]
                du_ref[:, cols] = duk
        dda[...] += jnp.sum((dy0_ref[...] * u0_ref[...].astype(F32)).reshape(rr // SUBLANES, SUBLANES, dmh), axis=0)

        @pl.when(i == nc - 1)
        def _():
            for o, a in ((da0r_ref, a0r), (da0i_ref, a0i), (da1r_ref, a1r), (da1i_ref, a1i), (dd_ref, dda)):
                o[...] = jnp.sum(a[...], axis=0, keepdims=True)

    row0 = pl.BlockSpec((rr, dmh), lambda h, i: (f0(i), h))
    row1 = pl.BlockSpec((rr, dmh), lambda h, i: (f1(i), h))
    tile = pl.BlockSpec((SUBLANES, swh), lambda h, i: (0, h))
    wspec = lambda a: pl.BlockSpec((nkh,) + a.shape[1:], lambda h, i: (h, 0, 0))
    st_spec = pl.BlockSpec((None, SUBLANES, swh), lambda h, i: (f0(i), 0, h))
    vec = lambda w: pl.BlockSpec((1, w), lambda h, i: (0, h))
    sd = jax.ShapeDtypeStruct
    out_shape = ([sd((n, dm), F32)] * 2 + [sd(a.shape, F32) for a in (*bb, *bb)] + [sd((1, sw), F32)] * 4 + [sd((1, dm), F32)])
    out_specs = [row0, row1] + [wspec(a) for a in (*bb, *bb)] + [vec(swh)] * 4 + [vec(dmh)]
    scratch = ([pltpu.VMEM((rr + 2 * SUBLANES, swh), F32)] * 4 + [pltpu.VMEM((rr, swh), F32)] * 4
               + [pltpu.VMEM((SUBLANES, swh), F32)] * 8 + [pltpu.VMEM((SUBLANES, dmh), F32)])
    vmem = 8 * (rr + 16) * swh * 4 + 16 * rr * dmh * 4 + 48 * nkh * LANES * swk * 4 + (10 << 20)
    return _call(
        body, name=name, xch=xch, args=[u, u, dy, dy, af_r, af_i, ab_r, ab_i, *bb, *cc, dsk, st_r, st_i],
        out_shape=out_shape, grid=(S5_SPLIT, nc),
        in_specs=[row0, row1, row0, row1, tile, tile, tile, tile] + [wspec(a) for a in (*bb, *cc)] + [vec(dmh), st_spec, st_spec],
        out_specs=out_specs, scratch=scratch, params=_params(("arbitrary", "arbitrary"), vmem))


def _hg_mask(kind, rev):
    if kind == "tot":
        r = lax.broadcasted_iota(jnp.int32, (SUBLANES, HG_ROWS), 0)
        c = lax.broadcasted_iota(jnp.int32, (SUBLANES, HG_ROWS), 1)
        return (c & 3) == r
    r = lax.broadcasted_iota(jnp.int32, (HG_ROWS, HG_ROWS), 0)
    c = lax.broadcasted_iota(jnp.int32, (HG_ROWS, HG_ROWS), 1)
    same = (r & 3) == (c & 3)
    before = ((c >> 2) >= (r >> 2)) if rev else ((c >> 2) <= (r >> 2))
    return jnp.logical_and(same, before if kind == "cum" else jnp.logical_not(before))


def _split2(x):
    hi = x.astype(BF16)
    return hi, (x - hi.astype(F32)).astype(BF16)


@functools.partial(jax.custom_vjp, nondiff_argnums=(1, 2))
def _mask_sum(x, kind, rev):
    m = _hg_mask(kind, rev).astype(BF16)
    hi, lo = _split2(x)
    return _dot(m, hi) + _dot(m, lo)


def _mask_sum_fwd(x, kind, rev):
    return _mask_sum(x, kind, rev), None


def _mask_sum_bwd(kind, rev, _, g):
    m = _hg_mask(kind, rev).astype(BF16)
    hi, lo = _split2(g)
    return (_dot(m, hi, TN) + _dot(m, lo, TN),)


_mask_sum.defvjp(_mask_sum_fwd, _mask_sum_bwd)


def _hg_chunk(q, v, fraw, l0, l1, st, *, rev):
    nh = q.shape[1] // HG_HEAD
    lb = _sigmoid(l1 - l0)
    logf = jnp.logaddexp(jnp.log(lb), jnp.log1p(-lb) + jax.nn.log_sigmoid(fraw))
    kk = (1.0 - lb) * _sigmoid(fraw * -1.0)
    tri = _hg_mask("cum", rev)
    bcum = _mask_sum(logf, "cum", rev)
    brem = _mask_sum(logf, "rem", rev)
    bend8 = _mask_sum(logf, "tot", rev)
    r8d = lax.broadcasted_iota(jnp.int32, bend8.shape, 0)
    decs = [jnp.exp(jnp.sum(jnp.where(r8d == b, bend8, 0.0), axis=0, keepdims=True)) for b in range(LOCAL_B)]
    qd = (q * jnp.exp(bcum)).astype(BF16)
    kd = (kk * jnp.exp(-bcum)).astype(BF16)
    ke = (kk * jnp.exp(brem)).astype(BF16)
    wide = (HG_ROWS, LOCAL_B * HG_HEAD)
    mine = (lax.broadcasted_iota(jnp.int32, wide, 1) >> 7) == (lax.broadcasted_iota(jnp.int32, wide, 0) & 3)
    per_example = lambda x: jnp.where(mine, jnp.concatenate([x] * LOCAL_B, axis=1), jnp.zeros(wide, x.dtype))
    outs, new = [], []
    for h in range(nh):
        sl = slice(h * HG_HEAD, (h + 1) * HG_HEAD)
        vh = v[:, sl].astype(BF16)
        att = jnp.where(tri, _dot(qd[:, sl], kd[:, sl], NT), 0.0)
        outs.append(_dot(att.astype(BF16), vh) + _dot(per_example(qd[:, sl]), st[h].astype(BF16), NT))
        dec = jnp.concatenate([d[:, sl] for d in decs], axis=1)
        new.append(st[h] * dec + _dot(vh, per_example(ke[:, sl]), TN))
    return jnp.concatenate(outs, axis=1), tuple(new)


def _hg_chunk_of(step, ncc, nc, rev):
    return _s5_chunk_of(step, ncc, nc, rev)


def _hg_fwd_dir(zz, lb2, *, d, name, dims, xch=None):
    n = zz.shape[0]
    dm = zz.shape[1] // 5
    ns, sw = dm // HG_HEAD, LOCAL_B * HG_HEAD
    nc, ncc = n // HG_ROWS, dims["nctx"] // HG_ROWS
    rev = d == 1
    ch = lambda i: _hg_chunk_of(i, ncc, nc, rev)

    def body(q_ref, v_ref, f_ref, l0_ref, l1_ref, o_ref, st_ref, st):
        @pl.when(pl.program_id(0) == 0)
        def _():
            st[...] = jnp.zeros(st.shape, F32)

        st_ref[...] = st[...]
        o, new = _hg_chunk(q_ref[...].astype(F32), v_ref[...].astype(F32), f_ref[...].astype(F32), l0_ref[...],
                           l1_ref[...], tuple(st[j] for j in range(ns)), rev=rev)
        o_ref[...] = o.astype(BF16)
        for j in range(ns):
            st[j] = new[j]

    blk = lambda off: pl.BlockSpec((HG_ROWS, dm), lambda i, off=off: (ch(i), off))
    lspec = lambda layer: pl.BlockSpec((None, None, 1, dm), lambda i, layer=layer: (d, layer, 0, 0))
    return _call(
        body, name=name, xch=xch, args=[zz, zz, zz, lb2, lb2],
        out_shape=[jax.ShapeDtypeStruct((n, dm), BF16), jax.ShapeDtypeStruct((nc, ns, HG_HEAD, sw), F32)],
        grid=(nc,),
        in_specs=[blk(0), blk(1), blk(2 + d), lspec(0), lspec(1)],
        out_specs=[pl.BlockSpec((HG_ROWS, dm), lambda i: (ch(i), 0)),
                   pl.BlockSpec((None, ns, HG_HEAD, sw), lambda i: (ch(i), 0, 0, 0))],
        scratch=[pltpu.VMEM((ns, HG_HEAD, sw), F32)],
        params=_params(("arbitrary",), 48 << 20))


def _hg_bwd_dir(zz, lb2, do, sts, dqv_prev, *, d, name, dims, xch=None):
    n = zz.shape[0]
    dm = zz.shape[1] // 5
    ns, sw = dm // HG_HEAD, LOCAL_B * HG_HEAD
    nc, ncc = n // HG_ROWS, dims["nctx"] // HG_ROWS
    rev = d == 1
    ch = lambda i: _hg_chunk_of(nc - 1 - i, ncc, nc, rev)
    qv_dtype = F32 if d == 0 else BF16

    def body(*refs):
        q_ref, v_ref, f_ref, l0_ref, l1_ref, do_ref, st_ref = refs[:7]
        pos = 7
        if d == 1:
            dqp_ref, dvp_ref = refs[7:9]
            pos = 9
        dq_ref, dv_ref, df_ref, dl_ref, dst = refs[pos:]
        i = pl.program_id(0)

        @pl.when(i == 0)
        def _():
            dst[...] = jnp.zeros(dst.shape, F32)
            dl_ref[...] = jnp.zeros(dl_ref.shape, F32)

        _, vjp = jax.vjp(functools.partial(_hg_chunk, rev=rev), q_ref[...].astype(F32), v_ref[...].astype(F32),
                         f_ref[...].astype(F32), l0_ref[...], l1_ref[...], tuple(st_ref[j] for j in range(ns)))
        dq, dv, df, dl0, dl1, dstn = vjp((do_ref[...].astype(F32), tuple(dst[j] for j in range(ns))))
        for j in range(ns):
            dst[j] = dstn[j]
        if d == 1:
            dq = dq + dqp_ref[...]
            dv = dv + dvp_ref[...]
        dq_ref[...] = dq.astype(qv_dtype)
        dv_ref[...] = dv.astype(qv_dtype)
        df_ref[...] = df.astype(BF16)
        dl_ref[0] += dl0
        dl_ref[1] += dl1

    blk = lambda off: pl.BlockSpec((HG_ROWS, dm), lambda i, off=off: (ch(i), off))
    oblk = pl.BlockSpec((HG_ROWS, dm), lambda i: (ch(i), 0))
    lspec = lambda layer: pl.BlockSpec((None, None, 1, dm), lambda i, layer=layer: (d, layer, 0, 0))
    ins = [zz, zz, zz, lb2, lb2, do, sts] + (list(dqv_prev) if d == 1 else [])
    in_specs = [blk(0), blk(1), blk(2 + d), lspec(0), lspec(1), oblk,
                pl.BlockSpec((None, ns, HG_HEAD, sw), lambda i: (ch(i), 0, 0, 0))]
    in_specs += [oblk, oblk] if d == 1 else []
    sd = jax.ShapeDtypeStruct
    return _call(
        body, name=name, xch=xch, args=ins,
        out_shape=[sd((n, dm), qv_dtype), sd((n, dm), qv_dtype), sd((n, dm), BF16), sd((2, 1, dm), F32)],
        grid=(nc,), in_specs=in_specs,
        out_specs=[oblk, oblk, oblk, pl.BlockSpec((2, 1, dm), lambda i: (0, 0, 0))],
        scratch=[pltpu.VMEM((ns, HG_HEAD, sw), F32)],
        params=_params(("arbitrary",), 56 << 20))


def _hg_readout(o, g, w):
    outs = []
    for h in range(o.shape[-1] // HG_HEAD):
        sl = slice(h * HG_HEAD, (h + 1) * HG_HEAD)
        oh = o[..., sl]
        outs.append(oh * _rms(oh) * w * _sigmoid(g[..., sl]))
    return jnp.concatenate(outs, axis=-1)


def _silu(x):
    return x * _sigmoid(x)


def _mod_fwd(craw, w, b):
    def body(c_ref, w_ref, b_ref, o_ref):
        s = _silu(c_ref[...]).astype(BF16)
        for layer in range(w.shape[0]):
            o_ref[layer] = _dot(s, w_ref[layer].astype(BF16)) + b_ref[layer]

    return pl.pallas_call(body, name="mod_fwd",
                          out_shape=jax.ShapeDtypeStruct((w.shape[0], craw.shape[0], w.shape[2]), F32),
                          compiler_params=_params(None, 40 << 20))(craw, w, b)


def _mod_bwd(craw, w, dlat_sh, dctx_sh, dlat_full, dctx_full):
    nl, dm, ns = w.shape
    nb = dlat_sh.shape[1]

    def body(c_ref, w_ref, dl_ref, dc_ref, dlf_ref, dcf_ref, dw_ref, db_ref, dcc_ref):
        craw_v = c_ref[...]
        s = _silu(craw_v)
        s_lat = s[:nb].astype(BF16)
        s_ctx = s[nb:].astype(BF16)
        row = lax.broadcasted_iota(jnp.int32, (SUBLANES, ns), 0)
        dsc = jnp.zeros((SUBLANES, dm), F32)
        for layer in range(nl):
            tot = dc_ref[0, pl.ds(layer, 1), :]
            totf = dcf_ref[0, pl.ds(layer, 1), :]
            for i in range(1, NDEV):
                tot = tot + dc_ref[i, pl.ds(layer, 1), :]
                totf = totf + dcf_ref[i, pl.ds(layer, 1), :]
            dc8 = jnp.where(row == 0, jnp.broadcast_to(tot, (SUBLANES, ns)), 0.0).astype(BF16)
            dw_ref[layer] = _dot(s_lat, dl_ref[layer].astype(BF16), TN) + _dot(s_ctx, dc8, TN)
            db_ref[layer] = jnp.sum(dlf_ref[layer], axis=0, keepdims=True) + totf
            dsc = dsc + _dot(dc8, w_ref[layer].astype(BF16), NT)
        cc = craw_v[nb:]
        sg = _sigmoid(cc)
        dcc_ref[...] = dsc * (sg * (1.0 + cc * (1.0 - sg)))

    sd = jax.ShapeDtypeStruct
    return pl.pallas_call(body, name="mod_bwd",
                          out_shape=[sd((nl, dm, ns), F32), sd((nl, 1, dlat_full.shape[2]), F32), sd((SUBLANES, dm), F32)],
                          compiler_params=_params(None, 48 << 20))(craw, w, dlat_sh, dctx_sh, dlat_full, dctx_full)


def _adam_rows(r):
    best = None
    for t in range(2 * SUBLANES, min(r, 128) + 1, 2 * SUBLANES):
        if r % t == 0:
            best = t
    return best if best is not None else r


def _adamw(parts, w, m, v, *, name):
    npart, r, c = parts.shape
    tr = _adam_rows(r)

    def body(p_ref, w_ref, m_ref, v_ref, g_ref, d_ref, nm_ref, nv_ref):
        g = p_ref[0].astype(F32)
        for i in range(1, npart):
            g = g + p_ref[i].astype(F32)
        nm = ADAM_B1 * m_ref[...] + (1.0 - ADAM_B1) * g
        nv = ADAM_B2 * v_ref[...] + (1.0 - ADAM_B2) * (g * g)
        m_hat = nm / (1.0 - ADAM_B1 ** ADAM_STEP)
        v_hat = nv / (1.0 - ADAM_B2 ** ADAM_STEP)
        g_ref[...] = g
        d_ref[...] = -ADAM_LR * (m_hat / (jnp.sqrt(v_hat) + ADAM_EPS) + ADAM_WD * w_ref[...])
        nm_ref[...] = nm
        nv_ref[...] = nv

    spec = pl.BlockSpec((tr, c), lambda i: (i, 0))
    vmem = 2 * (npart + 7) * tr * c * 4 + (8 << 20)
    return pl.pallas_call(
        body, name=name, out_shape=[jax.ShapeDtypeStruct((r, c), F32)] * 4, grid=(r // tr,),
        in_specs=[pl.BlockSpec((npart, tr, c), lambda i: (0, i, 0)), spec, spec, spec], out_specs=[spec] * 4,
        compiler_params=_params(("parallel",), vmem),
    )(parts, w, m, v)


def _to_tm(a):
    return jnp.transpose(a, (1, 0, 2)).reshape(a.shape[1] * a.shape[0], a.shape[2])


def _pattern(mod_lat, mod_ctx, m, dm):
    lat = mod_lat[:, m * dm:(m + 1) * dm]
    ctx = jnp.broadcast_to(mod_ctx[None, m * dm:(m + 1) * dm], (SUBLANES, dm))
    return jnp.stack([ctx, jnp.concatenate([lat, lat], axis=0)])


def _blockdiag_b(bt, nk):
    g, h, p = bt.shape
    t = bt.reshape(nk, 8, h, p)
    return jnp.einsum("kghp,gj->kghjp", t, jnp.eye(8, dtype=bt.dtype)).reshape(nk, 8 * h, 8 * p)


def _blockdiag_c(ct, nk):
    g, h, p = ct.shape
    t = ct.reshape(nk, 8, h, p)
    return jnp.einsum("kghp,gj->kgpjh", t, jnp.eye(8, dtype=ct.dtype)).reshape(nk, 8 * p, 8 * h)


def _diag_b(dbb, h, p):
    nk = dbb.shape[0]
    return jnp.einsum("kghgp->kghp", dbb.reshape(nk, 8, h, 8, p)).reshape(nk * 8, h, p)


def _s5_prep(lam_re, lam_im, log_step, b_re, b_im, c_re, c_im, dm):
    ngrp, nk = dm // S5_GROUP, dm // LANES
    sw = ngrp * S5_STATE
    lr4 = lam_re.reshape(2, ngrp, 1, S5_STATE)
    li4 = lam_im.reshape(2, ngrp, 1, S5_STATE)
    ls4 = log_step.reshape(2, ngrp, 1, 1)
    brt = jnp.transpose(b_re, (0, 1, 3, 2))
    bit = jnp.transpose(b_im, (0, 1, 3, 2))
    abar_r, abar_i, bbar_r, bbar_i = _s5_disc_fwd(lr4, li4, ls4, brt, bit)
    half = lambda a, d: jnp.broadcast_to(a[d].reshape(1, sw), (LOCAL_B, sw))
    tile = lambda a, first: jnp.concatenate([half(a, first), half(a, 1 - first)], axis=0)
    bb = tuple(_blockdiag_b(w[d], nk).astype(BF16) for d in range(2) for w in (bbar_r, bbar_i))
    cc = tuple(_blockdiag_c(w[d], nk).astype(BF16) for d in range(2) for w in (c_re, c_im))
    return dict(disc_in=(lr4, li4, ls4, brt, bit), af=(tile(abar_r, 0), tile(abar_i, 0)),
                ab=(tile(abar_r, 1), -tile(abar_i, 1)), bb=bb, cc=cc)


def _s5_param_grads(prep, dbb, dcc, da):
    lr4 = prep["disc_in"][0]
    dar = jnp.stack([da[0], da[2]]).reshape(lr4.shape)
    dai = jnp.stack([da[1], da[3]]).reshape(lr4.shape)
    dbbr = jnp.stack([_diag_b(dbb[0], S5_GROUP, S5_STATE), _diag_b(dbb[2], S5_GROUP, S5_STATE)])
    dbbi = jnp.stack([_diag_b(dbb[1], S5_GROUP, S5_STATE), _diag_b(dbb[3], S5_GROUP, S5_STATE)])
    dlr, dli, dls, dbrt, dbit = _s5_disc_bwd(*prep["disc_in"], dar, dai, dbbr, dbbi)
    g_c_re = jnp.stack([_diag_b(dcc[0], S5_GROUP, S5_STATE), _diag_b(dcc[2], S5_GROUP, S5_STATE)])
    g_c_im = jnp.stack([_diag_b(dcc[1], S5_GROUP, S5_STATE), _diag_b(dcc[3], S5_GROUP, S5_STATE)])
    return dlr, dli, dls, jnp.transpose(dbrt, (0, 1, 3, 2)), jnp.transpose(dbit, (0, 1, 3, 2)), g_c_re, g_c_im


def kernel(x, c, ctx, c_ctx, w_mod, b_mod, norm1_w, norm2_w, final_norm_w, s5_w_in, s5_lam_re, s5_lam_im, s5_log_step, s5_b_re, s5_b_im, s5_c_re, s5_c_im, s5_d, s5_w_glu, s5_w_out, hg_w_in, hg_lower_bounds, hg_gnorm_w, hg_w_out, ffn_w_up, ffn_conv_w, ffn_conv_b, ffn_w_down, loss_target, m_c_ctx, m_w_mod, m_b_mod, m_norm1_w, m_norm2_w, m_final_norm_w, m_s5_w_in, m_s5_lam_re, m_s5_lam_im, m_s5_log_step, m_s5_b_re, m_s5_b_im, m_s5_c_re, m_s5_c_im, m_s5_d, m_s5_w_glu, m_s5_w_out, m_hg_w_in, m_hg_lower_bounds, m_hg_gnorm_w, m_hg_w_out, m_ffn_w_up, m_ffn_conv_w, m_ffn_conv_b, m_ffn_w_down, v_c_ctx, v_w_mod, v_b_mod, v_norm1_w, v_norm2_w, v_final_norm_w, v_s5_w_in, v_s5_lam_re, v_s5_lam_im, v_s5_log_step, v_s5_b_re, v_s5_b_im, v_s5_c_re, v_s5_c_im, v_s5_d, v_s5_w_glu, v_s5_w_out, v_hg_w_in, v_hg_lower_bounds, v_hg_gnorm_w, v_hg_w_out, v_ffn_w_up, v_ffn_conv_w, v_ffn_conv_b, v_ffn_w_down):
    given = dict(locals())
    bsz, lx, dm = x.shape
    lc = ctx.shape[1]
    assert bsz == LOCAL_B and w_mod.shape[0] == 2 and dm % LANES == 0
    n, nctx = (lc + lx) * bsz, lc * bsz
    ngrp, nstate, hgrp = dm // S5_GROUP, S5_STATE, S5_GROUP
    nk = dm // LANES
    dims = dict(nctx=nctx, tm=min(512, nctx), tm_row=min(512, nctx), s5_rows=min(256, nctx),
                s5_lane_block=min(512, 8 * nstate))
    tm = dims["tm"]
    assert nctx % HG_ROWS == 0 and (lx * bsz) % nctx == 0 and lc % GRID_W == 0 and lc & (lc - 1) == 0
    me = 4 * lax.axis_index("x") + 2 * lax.axis_index("y") + lax.axis_index("c")

    gath = _exchange([given[k].astype(BF16) for k in ("s5_w_in", "s5_w_glu", "s5_w_out")]
                     + [c, hg_lower_bounds, ffn_conv_w], a2a=False, name="gather_weights")
    w_s5in, w_glu, w_s5out = (g.reshape(dm, dm) for g in gath[:3])
    c_all, lb_all, cw_all = gath[3:]
    ns_up = ffn_w_up.shape[2]
    w_up, w_dn = [None, None], [None, None]
    cols = lambda g: jnp.transpose(g, (1, 0, 2)).reshape(g.shape[1], -1)
    shards = lambda w: jnp.transpose(w.reshape(w.shape[0], NDEV, -1), (1, 0, 2))
    tn_up, tn_hg, tkr = 2 * ns_up, 2 * hg_w_in.shape[2], 1152
    assert n % tkr == 0 and n % 1024 == 0
    gather = lambda arrs: _Xchg([a.astype(BF16) for a in arrs], [False] * len(arrs))
    scatter = lambda arrs: _Xchg(arrs, [True] * len(arrs))
    cw = [cols(cw_all[:, layer]) for layer in range(2)]
    cb = [ffn_conv_b[layer].reshape(1, -1) for layer in range(2)]
    lb2 = jnp.transpose(lb_all, (1, 2, 0, 3)).reshape(2, 2, 1, dm)

    nsm = w_mod.shape[2]
    craw = jnp.concatenate([c_all.reshape(NDEV * bsz, dm), c_ctx[None], jnp.zeros((SUBLANES - 1, dm), F32)], axis=0)
    b_sh = lax.dynamic_slice(b_mod, (0, me * nsm), (2, nsm)).reshape(2, 1, nsm)
    mod_sh = _mod_fwd(craw, w_mod, b_sh)
    (mod_g,) = _exchange([mod_sh], a2a=False, name="gather_mod")
    mod_full = jnp.transpose(mod_g, (1, 2, 0, 3)).reshape(2, craw.shape[0], NDEV * nsm)
    pat = []
    for layer in range(2):
        mlat = lax.dynamic_slice(mod_full[layer], (me * bsz, 0), (bsz, N_MOD * dm))
        mctx = mod_full[layer, NDEV * bsz]
        pat.append([_pattern(mlat, mctx, m, dm) for m in range(N_MOD)])

    s5p = _s5_prep(s5_lam_re[0], s5_lam_im[0], s5_log_step[0], s5_b_re[0], s5_b_im[0], s5_c_re[0], s5_c_im[0], dm)
    dsk = s5_d.reshape(1, dm)

    z0 = jnp.concatenate([_to_tm(ctx), _to_tm(x)], axis=0)
    tgt = _to_tm(loss_target)
    n1w = [norm1_w[layer].reshape(1, dm) for layer in range(2)]
    n2w = [norm2_w[layer].reshape(1, dm) for layer in range(2)]

    def ffn_fwd(layer, h2):
        u = _lin(h2, w_up[layer], name=f"ffn_up{layer}", tm=1024, tn=tn_up, o_dtype=BF16)
        hm = _convffn_fwd(u, cw[layer], cb[layer], name=f"convffn_fwd{layer}", dims=dims)
        f = _lin(hm, w_dn[layer], name=f"ffn_down{layer}", tm=tm, tn=dm, o_dtype=BF16)
        return u, hm, f

    _, h0 = _norm_mod_fwd(z0, n1w[0], pat[0][0], pat[0][1], name="norm1_l0", dims=dims)
    u_s5 = _lin(h0, w_s5in, name="s5_in", tm=1024, tn=dm, o_dtype=BF16)
    (y_s5a, y_s5b, st_r, st_i), (g_up0, g_dn0, g_hgin, g_hgout, g_dn1) = _s5_fwd(
        u_s5, *s5p["af"], s5p["bb"], s5p["cc"], dsk, name="s5_fwd", dims=dims,
        xch=gather([ffn_w_up[0], ffn_w_down[0], hg_w_in[0], hg_w_out[0], ffn_w_down[1]]))
    w_up[0], w_dn[0] = cols(g_up0), g_dn0.reshape(-1, dm)
    w_hgin, w_hgout, w_dn[1] = cols(g_hgin), g_hgout.reshape(dm, dm), g_dn1.reshape(-1, dm)
    (zg,) = _rowk(lambda rv, pv, cv, il: ([_gelu(rv[0] + rv[1])], [], []), name="s5_gelu", n=n, tm=dims["tm_row"],
                  nctx=nctx, rows=[(y_s5a, dm, 0, 0), (y_s5b, dm, 0, 0)], out_rows=[(dm, BF16, dm, 0)])
    t_glu = _lin(zg, w_glu, name="s5_glu", tm=1024, tn=dm, o_dtype=BF16)
    (z2g,) = _rowk(lambda rv, pv, cv, il: ([rv[0] * _sigmoid(rv[1])], [], []), name="s5_gate", n=n,
                   tm=dims["tm_row"], nctx=nctx, rows=[(zg, dm, 0, 0), (t_glu, dm, 0, 0)],
                   out_rows=[(dm, BF16, dm, 0)])
    ymix0 = _lin(z2g, w_s5out, name="s5_out", tm=1024, tn=dm, o_dtype=BF16)
    z1_l0, h2_l0 = _norm_mod_fwd(z0, n2w[0], pat[0][3], pat[0][4], name="norm2_l0", dims=dims,
                                 res=(ymix0, pat[0][2]))
    u_l0, hm_l0, f_l0 = ffn_fwd(0, h2_l0)

    z2_l0, h1 = _norm_mod_fwd(z1_l0, n1w[1], pat[1][0], pat[1][1], name="norm1_l1", dims=dims,
                              res=(f_l0, pat[0][5]))
    zz = _lin(h1, w_hgin, name="hg_in", tm=1024, tn=tn_hg, o_dtype=BF16)
    (o_f, sts_f), (g_up1,) = _hg_fwd_dir(zz, lb2, d=0, name="hg_fwd_d0", dims=dims, xch=gather([ffn_w_up[1]]))
    w_up[1] = cols(g_up1)
    (o_b, sts_b), _ = _hg_fwd_dir(zz, lb2, d=1, name="hg_fwd_d1", dims=dims)
    gnw = hg_gnorm_w.reshape(1, HG_HEAD)
    (og,) = _rowk(lambda rv, pv, cv, il: ([_hg_readout(rv[0] + rv[1], rv[2], cv[0])], [], []), name="hg_readout",
                  n=n, tm=dims["tm_row"], nctx=nctx, rows=[(o_f, dm, 0, 0), (o_b, dm, 0, 0), (zz, dm, 4, 0)],
                  consts=[gnw], out_rows=[(dm, BF16, dm, 0)])
    ymix1 = _lin(og, w_hgout, name="hg_out", tm=1024, tn=dm, o_dtype=BF16)
    z1_l1, h2_l1 = _norm_mod_fwd(z2_l0, n2w[1], pat[1][3], pat[1][4], name="norm2_l1", dims=dims,
                                 res=(ymix1, pat[1][2]))
    u_l1, hm_l1, f_l1 = ffn_fwd(1, h2_l1)

    dz, df, dgate2_l1, loss_part, dfinal_w = _loss_bwd(z1_l1, f_l1, pat[1][5], tgt, final_norm_w.reshape(1, dm),
                                                        name="loss_bwd", dims=dims)

    def ffn_bwd(layer, df_, u, hm, h2, xch=None):
        dff = hm.shape[1]
        dhm = _lin(df_, w_dn[layer], name=f"ffn_down_bwd_in{layer}", trans_w=True, tm=1024, tn=dff // 2, o_dtype=BF16)
        dwd = _lin_w(hm, df_, name=f"ffn_down_bwd_w{layer}", ta=dff // 2, tn=dm, tkr=tkr)
        (dua, dug, dcwa, dcwg, dcba, dcbg), got = _convffn_bwd(u, dhm, cw[layer], cb[layer],
                                                               name=f"convffn_bwd{layer}", dims=dims, xch=xch)
        dh2 = _lin(dua, w_up[layer], name=f"ffn_up_bwd_in_a{layer}", trans_w=True, tm=tm, tn=dm, kblk=0)
        dh2 = _lin(dug, w_up[layer], name=f"ffn_up_bwd_in_g{layer}", trans_w=True, tm=tm, tn=dm, kblk=1, base=dh2,
                   o_dtype=BF16)
        dwu = jnp.concatenate([_lin_w(h2, dua, name=f"ffn_up_bwd_w_a{layer}", ta=dm, tn=tn_up, tkr=tkr),
                               _lin_w(h2, dug, name=f"ffn_up_bwd_w_g{layer}", ta=dm, tn=tn_up, tkr=tkr)], axis=1)
        dcw = shards(jnp.concatenate([dcwa, dcwg], axis=1))
        return dh2, shards(dwu), dwd, dcw, jnp.concatenate([dcba, dcbg], axis=1), got

    dh2, dwu_l1, dwd_l1, dcw_l1, dcb_l1, _ = ffn_bwd(1, df, u_l1, hm_l1, h2_l1)
    dz, dymix, dsh2_l1, dsc2_l1, dgate1_l1, dn2w_l1 = _norm_mod_bwd(dh2, z1_l1, dz, n2w[1], pat[1][4], name="norm2_bwd_l1",
                                                                    dims=dims, res=(ymix1, pat[1][2]))
    dog = _lin(dymix, w_hgout, name="hg_out_bwd_in", trans_w=True, tm=1024, tn=dm, o_dtype=BF16)
    dw_hgout = _lin_w(og, dymix, name="hg_out_bwd_w", ta=dm, tn=dm, tkr=tkr)

    def readout_bwd(rv, pv, cv, il):
        _, vjp = jax.vjp(_hg_readout, rv[0] + rv[1], rv[2], cv[0])
        do, dg, dw = vjp(rv[3])
        return [do, dg], [], [jnp.broadcast_to(dw, (SUBLANES, HG_HEAD)) * (1.0 / SUBLANES)]

    do, dg, dgnw = _rowk(readout_bwd, name="hg_readout_bwd", n=n, tm=dims["tm_row"], nctx=nctx,
                         rows=[(o_f, dm, 0, 0), (o_b, dm, 0, 0), (zz, dm, 4, 0), (dog, dm, 0, 0)], consts=[gnw],
                         out_rows=[(dm, BF16, dm, 0), (dm, BF16, dm, 0)], out_acc=[HG_HEAD])
    (dq0, dv0, dff, dl_f), (p_up1, p_dn1) = _hg_bwd_dir(
        zz, lb2, do, sts_f, None, d=0, name="hg_bwd_d0", dims=dims,
        xch=scatter([dwu_l1, dwd_l1.reshape(NDEV, -1, dm)]))
    (dq, dv, dfb, dl_b), (p_hgout,) = _hg_bwd_dir(
        zz, lb2, do, sts_b, (dq0, dv0), d=1, name="hg_bwd_d1", dims=dims,
        xch=scatter([dw_hgout.reshape(NDEV, -1, dm)]))
    dh1, dw_parts = None, []
    for p, piece in enumerate((dq, dv, dff, dfb, dg)):
        dh1 = _lin(piece, w_hgin, name=f"hg_in_bwd_in{p}", trans_w=True, tm=1024, tn=dm, kblk=p, base=dh1,
                   o_dtype=BF16 if p == 4 else F32)
        dw_parts.append(_lin_w(h1, piece, name=f"hg_in_bwd_w{p}", ta=dm, tn=dm, tkr=tkr))
    dw_hgin = shards(jnp.concatenate(dw_parts, axis=1))
    dz, df0, dsh1_l1, dsc1_l1, dgate2_l0, dn1w_l1 = _norm_mod_bwd(dh1, z2_l0, dz, n1w[1], pat[1][1], name="norm1_bwd_l1",
                                                                  dims=dims, res=(f_l0, pat[0][5]))
    dh2, dwu_l0, dwd_l0, dcw_l0, dcb_l0, _ = ffn_bwd(0, df0, u_l0, hm_l0, h2_l0)
    dz, dymix, dsh2_l0, dsc2_l0, dgate1_l0, dn2w_l0 = _norm_mod_bwd(dh2, z1_l0, dz, n2w[0], pat[0][4], name="norm2_bwd_l0",
                                                                    dims=dims, res=(ymix0, pat[0][2]))
    dz2g = _lin(dymix, w_s5out, name="s5_out_bwd_in", trans_w=True, tm=1024, tn=dm, o_dtype=BF16)
    dw_s5out = _lin_w(z2g, dymix, name="s5_out_bwd_w", ta=dm, tn=dm, tkr=tkr)

    def gate_bwd(rv, pv, cv, il):
        sg = _sigmoid(rv[1])
        return [rv[2] * rv[0] * sg * (1.0 - sg), rv[2] * sg], [], []

    dt_glu, dzg_a = _rowk(gate_bwd, name="s5_gate_bwd", n=n, tm=dims["tm_row"], nctx=nctx,
                          rows=[(zg, dm, 0, 0), (t_glu, dm, 0, 0), (dz2g, dm, 0, 0)],
                          out_rows=[(dm, BF16, dm, 0), (dm, BF16, dm, 0)])
    dzg_b = _lin(dt_glu, w_glu, name="s5_glu_bwd_in", trans_w=True, tm=1024, tn=dm, o_dtype=BF16)
    dw_glu = _lin_w(zg, dt_glu, name="s5_glu_bwd_w", ta=dm, tn=dm, tkr=tkr)

    def gelu_bwd(rv, pv, cv, il):
        _, vjp = jax.vjp(_gelu, rv[0] + rv[1])
        return [vjp(rv[2] + rv[3])[0]], [], []

    (dy_s5,) = _rowk(gelu_bwd, name="s5_gelu_bwd", n=n, tm=dims["tm_row"], nctx=nctx,
                     rows=[(y_s5a, dm, 0, 0), (y_s5b, dm, 0, 0), (dzg_a, dm, 0, 0), (dzg_b, dm, 0, 0)],
                     out_rows=[(dm, F32, dm, 0)])
    dcw_both = jnp.stack([dcw_l0, dcw_l1], axis=1)
    s5g, (p_up0, p_dn0, p_cw, p_s5out, p_glu, p_hgin) = _s5_bwd(
        u_s5, dy_s5, *s5p["af"], *s5p["ab"], s5p["bb"], s5p["cc"], dsk, st_r, st_i, name="s5_bwd", dims=dims,
        xch=scatter([dwu_l0, dwd_l0.reshape(NDEV, -1, dm), dcw_both, dw_s5out.reshape(NDEV, -1, dm),
                     dw_glu.reshape(NDEV, -1, dm), dw_hgin]))
    (du_s5,) = _rowk(lambda rv, pv, cv, il: ([rv[0] + rv[1]], [], []), name="s5_du", n=n, tm=dims["tm_row"], nctx=nctx,
                     rows=[(s5g[0], dm, 0, 0), (s5g[1], dm, 0, 0)], out_rows=[(dm, BF16, dm, 0)])
    ddsk = s5g[14]
    dh0 = _lin(du_s5, w_s5in, name="s5_in_bwd_in", trans_w=True, tm=1024, tn=dm, o_dtype=BF16)
    dw_s5in = _lin_w(h0, du_s5, name="s5_in_bwd_w", ta=dm, tn=dm, tkr=tkr)
    dz0, dsh1_l0, dsc1_l0, dn1w_l0 = _norm_mod_bwd(dh0, z0, dz, n1w[0], pat[0][1], name="norm1_bwd_l0", dims=dims)

    dlr, dli, dls, g_b_re, g_b_im, g_c_re, g_c_im = _s5_param_grads(s5p, s5g[2:6], s5g[6:10], s5g[10:14])

    dmod = jnp.stack([
        jnp.concatenate([dsh1_l0, dsc1_l0, dgate1_l0, dsh2_l0, dsc2_l0, dgate2_l0], axis=1),
        jnp.concatenate([dsh1_l1, dsc1_l1, dgate1_l1, dsh2_l1, dsc2_l1, dgate2_l1], axis=1)])
    dl_hg = jnp.stack([dl_f[:, 0], dl_b[:, 0]])
    wide = lambda g: g.reshape(-1, dm)
    small = [("norm1_w", jnp.concatenate([dn1w_l0, dn1w_l1])),
             ("norm2_w", jnp.concatenate([dn2w_l0, dn2w_l1])), ("final_norm_w", dfinal_w),
             ("s5_lam_re", dlr.reshape(-1, nstate)), ("s5_lam_im", dli.reshape(-1, nstate)),
             ("s5_log_step", dls.reshape(2, ngrp)),
             ("s5_b_re", wide(g_b_re.astype(BF16))), ("s5_b_im", wide(g_b_im.astype(BF16))),
             ("s5_c_re", wide(g_c_re.astype(BF16))), ("s5_c_im", wide(g_c_im.astype(BF16))), ("s5_d", ddsk),
             ("hg_gnorm_w", dgnw), ("ffn_conv_b", jnp.stack([dcb_l0.reshape(-1), dcb_l1.reshape(-1)]))]
    tail = _exchange([dmod, dw_s5in.reshape(NDEV, -1, dm)] + [g for _, g in small] + [wide(dl_hg), loss_part],
                     a2a=[False, True] + [False] * (len(small) + 2), name="gather_tail")
    dmod_g, p_s5in, gathered = tail[0], tail[1], tail[2:]
    dlat_full = jnp.transpose(dmod_g[:, :, :bsz], (1, 0, 2, 3)).reshape(2, NDEV * bsz, N_MOD * dm)
    dctx_full = dmod_g[:, :, bsz]
    dlat_sh = lax.dynamic_slice(dlat_full, (0, 0, me * nsm), (2, NDEV * bsz, nsm))
    dctx_sh = lax.dynamic_slice(dctx_full, (0, 0, me * nsm), (NDEV, 2, nsm))
    g_w_mod, g_b_mod, dcctx8 = _mod_bwd(craw, w_mod, dlat_sh, dctx_sh, dlat_full, dctx_full)

    (g_cctx,) = _exchange([wide(dcctx8[:1])], a2a=False, name="gather_cctx")
    small, gathered = [("c_ctx", wide(dcctx8[:1]))] + small, [g_cctx] + gathered
    res = {}
    for (k, g), parts in zip(small, gathered):
        w2, m2, v2 = (given[p + k].reshape(g.shape) for p in ("", "m_", "v_"))
        res[k] = tuple(o.reshape(given[k].shape) for o in _adamw(parts, w2, m2, v2, name="adamw_" + k))

    def total(parts, name):
        z = jnp.zeros(parts.shape[1:], F32)
        return _adamw(parts, z, z, z, name=name)[0]

    loss = jnp.sum(total(gathered[-1], "sum_loss"))
    dl_tot = total(gathered[-2], "sum_dlb").reshape(dl_hg.shape)
    nlb = hg_lower_bounds.shape[2]
    g_lb = lax.dynamic_slice(dl_tot, (0, 0, me * nlb), (2, 2, nlb))

    def adam_local(name, g, shape2):
        w, m, v = given[name], given["m_" + name], given["v_" + name]
        out = _adamw(g.reshape((1,) + shape2), w.reshape(shape2), m.reshape(shape2), v.reshape(shape2),
                     name="adamw_" + name)
        return tuple(o.reshape(w.shape) for o in out)

    def adam_parts(name, p):
        w, m, v = given[name], given["m_" + name], given["v_" + name]
        shape2 = (p.shape[0], -1, w.shape[-1])
        p3 = p.reshape(shape2)
        s2 = p3.shape[1:]
        out = _adamw(p3, w.reshape(s2), m.reshape(s2), v.reshape(s2), name="adamw_" + name)
        return tuple(o.reshape(w.shape) for o in out)

    res["hg_lower_bounds"] = adam_local("hg_lower_bounds", g_lb, (2 * 2, nlb))
    res["w_mod"] = adam_local("w_mod", g_w_mod, (2 * dm, nsm))
    res["b_mod"] = adam_local("b_mod", g_b_mod, (2, N_MOD * dm))
    res["s5_w_in"] = adam_parts("s5_w_in", p_s5in)
    res["s5_w_glu"] = adam_parts("s5_w_glu", p_glu)
    res["s5_w_out"] = adam_parts("s5_w_out", p_s5out)
    res["hg_w_in"] = adam_parts("hg_w_in", p_hgin)
    res["hg_w_out"] = adam_parts("hg_w_out", p_hgout)
    res["ffn_w_up"] = adam_parts("ffn_w_up", jnp.stack([p_up0, p_up1], axis=1))
    res["ffn_w_down"] = adam_parts("ffn_w_down", jnp.stack([p_dn0, p_dn1], axis=1))
    res["ffn_conv_w"] = adam_parts("ffn_conv_w", p_cw)

    grad_x = jnp.transpose(dz0[nctx:].reshape(lx, bsz, dm), (1, 0, 2))
    order = ["c_ctx", "w_mod", "b_mod", "norm1_w", "norm2_w", "final_norm_w", "s5_w_in", "s5_lam_re", "s5_lam_im",
             "s5_log_step", "s5_b_re", "s5_b_im", "s5_c_re", "s5_c_im", "s5_d", "s5_w_glu", "s5_w_out", "hg_w_in",
             "hg_lower_bounds", "hg_gnorm_w", "hg_w_out", "ffn_w_up", "ffn_conv_w", "ffn_conv_b", "ffn_w_down"]
    outs = [loss, grad_x]
    for j in range(4):
        outs += [res[k][j].reshape(given[k].shape) for k in order]
    return tuple(outs)
```

```python
import functools

import jax
import jax.numpy as jnp
from jax import lax
from jax.experimental import pallas as pl
from jax.experimental.pallas import tpu as pltpu

F32 = jnp.float32
BF16 = jnp.bfloat16
NDEV = 8
LOCAL_B = 4
NORM_EPS = 1e-6
N_MOD = 6
S5_GROUP = 16
S5_STATE = 64
S5_LAM_RE_MAX = -1e-4
HG_HEAD = 128
HG_ROWS = 128
GRID_W = 64
ADAM_LR, ADAM_B1, ADAM_B2, ADAM_EPS, ADAM_WD, ADAM_STEP = 0.001, 0.9, 0.999, 1e-08, 0.01, 10
VMEM_BYTES_V7X = 64 * 1024 * 1024
LANES = 128
SUBLANES = 8

NN = (((1,), (0,)), ((), ()))
NT = (((1,), (1,)), ((), ()))
TN = (((0,), (0,)), ((), ()))
MESH = pl.DeviceIdType.MESH


def _params(sem=None, vmem=None):
    kw = {}
    if sem is not None:
        kw["dimension_semantics"] = sem
    if vmem is not None:
        kw["vmem_limit_bytes"] = int(min(vmem, VMEM_BYTES_V7X - (4 << 20)))
    return pltpu.CompilerParams(**kw)


def _nbytes(shape, dtype):
    n = 1
    for s in shape:
        n *= 1 if s is None else s
    return n * jnp.dtype(dtype).itemsize


def _dot(a, b, dims=NN, precision=None):
    return lax.dot_general(a, b, dims, preferred_element_type=F32, precision=precision)


def _sigmoid(x):
    return 1.0 / (1.0 + jnp.exp(-x))


class _Xchg:
    def __init__(self, arrs, a2a):
        self.arrs, self.a2a, self.n = list(arrs), list(a2a), len(arrs)

    def out_shape(self):
        return [jax.ShapeDtypeStruct(a.shape if f else (NDEV,) + a.shape, a.dtype) for a, f in zip(self.arrs, self.a2a)]

    def scratch(self):
        return [pltpu.SemaphoreType.DMA((self.n * (NDEV - 1),)), pltpu.SemaphoreType.DMA((self.n * (NDEV - 1),)),
                pltpu.SemaphoreType.DMA((self.n,))]

    def _copies(self, ins, outs, sems, with_recvs):
        send_sems, recv_sems, loc_sems = sems
        x, y, c = lax.axis_index("x"), lax.axis_index("y"), lax.axis_index("c")
        me = 4 * x + 2 * y + c
        local, sends, recvs = [], [], []
        for a in range(self.n):
            src = ins[a].at[me] if self.a2a[a] else ins[a]
            local.append(pltpu.make_async_copy(src, outs[a].at[me], loc_sems.at[a]))
            for k in range(1, NDEV):
                px = (1 - x) if (k >> 2) & 1 else x
                py = (1 - y) if (k >> 1) & 1 else y
                pc = (1 - c) if k & 1 else c
                p = 4 * px + 2 * py + pc
                s = a * (NDEV - 1) + k - 1
                src = ins[a].at[p] if self.a2a[a] else ins[a]
                kw = dict(src_ref=src, send_sem=send_sems.at[s], recv_sem=recv_sems.at[s], device_id=(px, py, pc),
                          device_id_type=MESH)
                sends.append(pltpu.make_async_remote_copy(dst_ref=outs[a].at[me], **kw))
                if with_recvs:
                    recvs.append(pltpu.make_async_remote_copy(dst_ref=outs[a].at[p], **kw))
        return local, sends, recvs

    def start(self, ins, outs, sems):
        local, sends, _ = self._copies(ins, outs, sems, False)
        for cp in local + sends:
            cp.start()

    def wait(self, ins, outs, sems):
        local, sends, recvs = self._copies(ins, outs, sems, True)
        for cp in sends:
            cp.wait_send()
        for cp in recvs:
            cp.wait_recv()
        for cp in local:
            cp.wait()


def _exchange(arrs, *, a2a, name):
    xch = _Xchg(arrs, a2a if isinstance(a2a, (list, tuple)) else [a2a] * len(arrs))
    n = xch.n

    def body(*refs):
        xch.start(refs[:n], refs[n:2 * n], refs[2 * n:])
        xch.wait(refs[:n], refs[n:2 * n], refs[2 * n:])

    res = pl.pallas_call(
        body, name=name, out_shape=xch.out_shape(),
        in_specs=[pl.BlockSpec(memory_space=pl.ANY)] * n, out_specs=[pl.BlockSpec(memory_space=pl.ANY)] * n,
        scratch_shapes=xch.scratch(),
    )(*arrs)
    return list(res)


def _call(body, *, name, out_shape, grid, in_specs, out_specs, scratch, params, args, xch=None):
    in_specs, out_specs, out_shape, scratch, args = list(in_specs), list(out_specs), list(out_shape), list(scratch), list(args)
    n_in, n_out, n_scr = len(in_specs), len(out_shape), len(scratch)
    if xch is not None:
        k = xch.n
        inner = body

        def body(*refs):
            ins, xin = refs[:n_in], refs[n_in:n_in + k]
            outs, xout = refs[n_in + k:n_in + k + n_out], refs[n_in + k + n_out:n_in + 2 * k + n_out]
            scr = refs[n_in + 2 * k + n_out:n_in + 2 * k + n_out + n_scr]
            sems = refs[n_in + 2 * k + n_out + n_scr:]
            first = pl.program_id(0) == 0
            last = pl.program_id(0) == grid[0] - 1
            for ax in range(1, len(grid)):
                first = jnp.logical_and(first, pl.program_id(ax) == 0)
                last = jnp.logical_and(last, pl.program_id(ax) == grid[ax] - 1)

            @pl.when(first)
            def _():
                xch.start(xin, xout, sems)

            inner(*ins, *outs, *scr)

            @pl.when(last)
            def _():
                xch.wait(xin, xout, sems)

        anyspec = pl.BlockSpec(memory_space=pl.ANY)
        in_specs += [anyspec] * k
        out_specs += [anyspec] * k
        out_shape += xch.out_shape()
        scratch += xch.scratch()
        args += xch.arrs
    res = pl.pallas_call(body, name=name, out_shape=out_shape, grid=grid, in_specs=in_specs, out_specs=out_specs,
                         scratch_shapes=scratch, compiler_params=params)(*args)
    return list(res[:n_out]), list(res[n_out:])


def _mm(a, b, *, name, grid, a_spec, b_spec, o_spec, o_shape, o_dtype, dims, base=None):
    nk = grid[2]
    o_block = tuple(s for s in o_spec.block_shape if s is not None)

    def body(a_ref, b_ref, *rest):
        base_ref = rest[0] if base is not None else None
        o_ref, scr = rest[1 if base is not None else 0], rest[2 if base is not None else 1:]
        r = _dot(a_ref[...].astype(BF16), b_ref[...].astype(BF16), dims)
        if nk == 1:
            if base is not None:
                r = r + base_ref[...].astype(F32)
            o_ref[...] = r.astype(o_dtype)
        else:
            acc = scr[0]
            k = pl.program_id(2)

            @pl.when(k == 0)
            def _():
                acc[...] = r

            @pl.when(k > 0)
            def _():
                acc[...] += r

            @pl.when(k == nk - 1)
            def _():
                tot = acc[...] if base is None else acc[...] + base_ref[...].astype(F32)
                o_ref[...] = tot.astype(o_dtype)

    blocks = (_nbytes(a_spec.block_shape, a.dtype) + _nbytes(b_spec.block_shape, b.dtype) + _nbytes(o_block, o_dtype)
              + (_nbytes(o_block, base.dtype) if base is not None else 0))
    scratch = [pltpu.VMEM(o_block, F32)] if nk > 1 else []
    vmem = 2 * blocks + 3 * _nbytes(o_block, F32) + (8 << 20)
    return pl.pallas_call(
        body, name=name, out_shape=jax.ShapeDtypeStruct(o_shape, o_dtype), grid=grid,
        in_specs=[a_spec, b_spec] + ([o_spec] if base is not None else []), out_specs=o_spec, scratch_shapes=scratch,
        compiler_params=_params(("parallel", "parallel", "arbitrary"), vmem),
    )(a, b, *([base] if base is not None else []))


def _lin(a, w, *, name, trans_w=False, tm, tn, o_dtype=F32, kblk=0, base=None):
    m, kk = a.shape
    nout = w.shape[0] if trans_w else w.shape[1]
    if trans_w:
        b_spec = pl.BlockSpec((tn, kk), lambda j, i, k: (j, kblk))
    else:
        b_spec = pl.BlockSpec((kk, tn), lambda j, i, k: (0, j))
    return _mm(a, w, name=name, grid=(nout // tn, m // tm, 1), dims=NT if trans_w else NN, o_shape=(m, nout),
               o_dtype=o_dtype, o_spec=pl.BlockSpec((tm, tn), lambda j, i, k: (i, j)),
               a_spec=pl.BlockSpec((tm, kk), lambda j, i, k: (i, 0)), b_spec=b_spec, base=base)


def _lin_cat(pieces, w, *, name, tm, o_dtype=F32):
    m, kk = pieces[0].shape
    nout, npc = w.shape[0], len(pieces)

    def body(*refs):
        w_ref, o_ref = refs[npc], refs[npc + 1]
        acc = _dot(refs[0][...], w_ref[:, 0:kk], NT)
        for p in range(1, npc):
            acc = acc + _dot(refs[p][...], w_ref[:, p * kk:(p + 1) * kk], NT)
        o_ref[...] = acc.astype(o_dtype)

    row = pl.BlockSpec((tm, kk), lambda i: (i, 0))
    vmem = 2 * (npc * _nbytes((tm, kk), pieces[0].dtype) + _nbytes(w.shape, w.dtype)) + 4 * _nbytes((tm, nout), F32) + (8 << 20)
    return pl.pallas_call(
        body, name=name, out_shape=jax.ShapeDtypeStruct((m, nout), o_dtype), grid=(m // tm,),
        in_specs=[row] * npc + [pl.BlockSpec(w.shape, lambda i: (0, 0))], out_specs=pl.BlockSpec((tm, nout), lambda i: (i, 0)),
        compiler_params=_params(("parallel",), vmem),
    )(*pieces, w)


def _lin_w(a, dy, *, name, ta, tn, tkr):
    m, ka = a.shape
    nout = dy.shape[1]
    return _mm(a, dy, name=name, grid=(ka // ta, nout // tn, m // tkr), dims=TN, o_shape=(ka, nout), o_dtype=BF16,
               o_spec=pl.BlockSpec((ta, tn), lambda i, j, k: (i, j)),
               a_spec=pl.BlockSpec((tkr, ta), lambda i, j, k: (k, i)),
               b_spec=pl.BlockSpec((tkr, tn), lambda i, j, k: (k, j)))


def _rowk(fn, *, name, n, tm, nctx, rows=(), pats=(), consts=(), out_rows=(), out_seg=(), out_acc=()):
    nb, ncb = n // tm, nctx // tm
    nr, npat, ncst = len(rows), len(pats), len(consts)
    no, nseg, nacc = len(out_rows), len(out_seg), len(out_acc)
    in_specs, blocks = [], 0
    for arr, w, cb, off in rows:
        in_specs.append(pl.BlockSpec((tm, w), lambda i, cb=cb, off=off: (jnp.maximum(i - off, 0), cb)))
        blocks += _nbytes((tm, w), arr.dtype)
    for p in pats:
        in_specs.append(pl.BlockSpec((None, SUBLANES, p.shape[2]), lambda i: (jnp.where(i >= ncb, 1, 0), 0, 0)))
    for cst in consts:
        in_specs.append(pl.BlockSpec(cst.shape, lambda i: (0, 0)))
    out_shape, out_specs = [], []
    for wt, dt, w, cb in out_rows:
        out_shape.append(jax.ShapeDtypeStruct((n, wt), dt))
        out_specs.append(pl.BlockSpec((tm, w), lambda i, cb=cb: (i, cb)))
        blocks += _nbytes((tm, w), dt)
    for w in out_seg:
        out_shape.append(jax.ShapeDtypeStruct((SUBLANES, w), F32))
        out_specs.append(pl.BlockSpec((SUBLANES, w), lambda i: (0, 0)))
    for w in out_acc:
        out_shape.append(jax.ShapeDtypeStruct((1, w), F32))
        out_specs.append(pl.BlockSpec((1, w), lambda i: (0, 0)))
    scratch = [pltpu.VMEM((2, SUBLANES, w), F32) for w in out_seg] + [pltpu.VMEM((SUBLANES, w), F32) for w in out_acc]

    def body(*refs):
        r_in = refs[:nr]
        p_in = refs[nr:nr + npat]
        c_in = refs[nr + npat:nr + npat + ncst]
        base = nr + npat + ncst
        o_rows = refs[base:base + no]
        o_seg = refs[base + no:base + no + nseg]
        o_acc = refs[base + no + nseg:base + no + nseg + nacc]
        s_seg = refs[base + no + nseg + nacc:base + no + nseg + nacc + nseg]
        s_acc = refs[base + no + nseg + nacc + nseg:]
        i = pl.program_id(0)
        rv = [r[...].astype(F32).reshape(tm // SUBLANES, SUBLANES, r.shape[1]) for r in r_in]
        pv = [p[...] for p in p_in]
        cv = [c[...] for c in c_in]
        is_lat = (i >= ncb).astype(F32)
        ro, so, ao = fn(rv, pv, cv, is_lat)
        for ref, val in zip(o_rows, ro):
            ref[...] = val.reshape(tm, ref.shape[1]).astype(ref.dtype)
        if nseg or nacc:
            @pl.when(i == 0)
            def _():
                for s in list(s_seg) + list(s_acc):
                    s[...] = jnp.zeros(s.shape, F32)

            seg = jnp.where(i >= ncb, 1, 0)
            for s, val in zip(s_seg, so):
                s[seg] = s[seg] + val
            for s, val in zip(s_acc, ao):
                s[...] = s[...] + val

            @pl.when(i == nb - 1)
            def _():
                for o, s in zip(o_seg, s_seg):
                    lat, ctx = s[1], s[0]
                    row = lax.broadcasted_iota(jnp.int32, lat.shape, 0)
                    lat = lat + pltpu.roll(lat, 4, 0)
                    ctx = jnp.broadcast_to(jnp.sum(ctx, axis=0, keepdims=True), lat.shape)
                    o[...] = jnp.where(row < 4, lat, jnp.where(row == 4, ctx, 0.0))
                for o, s in zip(o_acc, s_acc):
                    o[...] = jnp.sum(s[...], axis=0, keepdims=True)

    vmem = 2 * blocks + 8 * tm * 1024 * 4 + (8 << 20)
    res = pl.pallas_call(
        body, name=name, out_shape=out_shape, grid=(nb,), in_specs=in_specs, out_specs=out_specs,
        scratch_shapes=scratch, compiler_params=_params(("arbitrary",), vmem),
    )(*[r[0] for r in rows], *pats, *consts)
    return list(res)


def _rms(z):
    return lax.rsqrt(jnp.mean(z * z, axis=-1, keepdims=True) + NORM_EPS)


def _norm_mod_fwd(z, w, sh, sc, *, name, dims, res=None):
    n, d = z.shape

    def fn(rv, pv, cv, is_lat):
        zz = rv[0]
        if res is not None:
            zz = zz + pv[2][None] * rv[1]
        h = (zz * _rms(zz) * cv[0]) * (1.0 + pv[1][None]) + pv[0][None]
        return ([zz, h] if res is not None else [h]), [], []

    rows = [(z, d, 0, 0)] + ([(res[0], d, 0, 0)] if res is not None else [])
    pats = [sh, sc] + ([res[1]] if res is not None else [])
    outs = ([(d, F32, d, 0)] if res is not None else []) + [(d, BF16, d, 0)]
    out = _rowk(fn, name=name, n=n, tm=dims["tm_row"], nctx=dims["nctx"], rows=rows, pats=pats, consts=[w],
                out_rows=outs)
    return (out[0], out[1]) if res is not None else (None, out[0])


def _norm_core_bwd(zin, dh, w, sc):
    r = _rms(zin)
    xh = zin * r
    dsh = jnp.sum(dh, axis=0)
    dsc = jnp.sum(dh * (xh * w), axis=0)
    dyv = dh * (1.0 + sc[None])
    dw = jnp.sum(dyv * xh, axis=0)
    dxh = dyv * w
    dx = r * (dxh - xh * jnp.mean(dxh * xh, axis=-1, keepdims=True))
    return dx, dsh, dsc, dw


def _norm_mod_bwd(dh, zin, dz_up, w, sc, *, name, dims, res=None):
    n, d = zin.shape

    def fn(rv, pv, cv, is_lat):
        dx, dsh, dsc, dw = _norm_core_bwd(rv[1], rv[0], cv[0], pv[0])
        dz = rv[2] + dx
        if res is None:
            return [dz], [dsh, dsc], [dw]
        return [dz, dz * pv[1][None]], [dsh, dsc, jnp.sum(dz * rv[3], axis=0)], [dw]

    rows = [(dh, d, 0, 0), (zin, d, 0, 0), (dz_up, d, 0, 0)] + ([(res[0], d, 0, 0)] if res is not None else [])
    pats = [sc] + ([res[1]] if res is not None else [])
    outs = [(d, F32, d, 0)] + ([(d, BF16, d, 0)] if res is not None else [])
    return _rowk(fn, name=name, n=n, tm=dims["tm_row"], nctx=dims["nctx"], rows=rows, pats=pats, consts=[w],
                 out_rows=outs, out_seg=[d] * (3 if res is not None else 2), out_acc=[d])


def _loss_bwd(z1, f, gate, tgt, w, *, name, dims):
    n, d = z1.shape

    def fn(rv, pv, cv, is_lat):
        z2 = rv[0] + pv[0][None] * rv[1]
        r = _rms(z2)
        xh = z2 * r
        err = (xh * cv[0] - rv[2]) * is_lat
        dout = err * (1.0 / d)
        dxh = dout * cv[0]
        dz = r * (dxh - xh * jnp.mean(dxh * xh, axis=-1, keepdims=True))
        return ([dz, dz * pv[0][None]], [jnp.sum(dz * rv[1], axis=0)],
                [jnp.sum(0.5 * err * err * (1.0 / d), axis=0), jnp.sum(dout * xh, axis=0)])

    tm = dims["tm_row"]
    rows = [(z1, d, 0, 0), (f, d, 0, 0), (tgt, d, 0, dims["nctx"] // tm)]
    return _rowk(fn, name=name, n=n, tm=tm, nctx=dims["nctx"], rows=rows, pats=[gate], consts=[w],
                 out_rows=[(d, F32, d, 0), (d, BF16, d, 0)], out_seg=[d], out_acc=[d, d])


def _gelu(y):
    return jax.nn.gelu(y, approximate=True)


def _conv_masks(tb, i):
    tok = lax.broadcasted_iota(jnp.int32, (tb, 1), 0) >> 2
    last = jnp.where(i == 0, tb // LOCAL_B - 1, GRID_W - 1)
    wpos = tok & last
    return wpos == 0, wpos == last


CONV_LANES = 2 * LANES


def _conv_taps(u_ref, cw_ref, cb_ref, no_left, no_right, tb):
    uu = u_ref[...].astype(F32)
    ul = jnp.where(no_left, 0.0, pltpu.roll(uu, LOCAL_B, 0))
    ur = jnp.where(no_right, 0.0, pltpu.roll(uu, tb - LOCAL_B, 0))
    val = cb_ref[...] + ul * cw_ref[pl.ds(0, 1), :] + uu * cw_ref[pl.ds(1, 1), :] + ur * cw_ref[pl.ds(2, 1), :]
    return val, ul, uu, ur


def _convffn_specs(tb, nj):
    cl = CONV_LANES
    return [pl.BlockSpec((tb, cl), lambda j, i: (i, j)), pl.BlockSpec((tb, cl), lambda j, i: (i, nj + j)),
            pl.BlockSpec((3, cl), lambda j, i: (0, j)), pl.BlockSpec((3, cl), lambda j, i: (0, nj + j)),
            pl.BlockSpec((1, cl), lambda j, i: (0, j)), pl.BlockSpec((1, cl), lambda j, i: (0, nj + j))]


def _convffn_fwd(u, cw, cb, *, name, dims):
    n, f2 = u.shape
    tb, nj = dims["nctx"], f2 // 2 // CONV_LANES

    def body(ua_ref, ug_ref, cwa_ref, cwg_ref, cba_ref, cbg_ref, o_ref):
        no_left, no_right = _conv_masks(tb, pl.program_id(1))
        a = _conv_taps(ua_ref, cwa_ref, cba_ref, no_left, no_right, tb)[0]
        g = _conv_taps(ug_ref, cwg_ref, cbg_ref, no_left, no_right, tb)[0]
        o_ref[...] = (a * _sigmoid(a) * g).astype(BF16)

    vmem = 16 * tb * CONV_LANES * 4 + (8 << 20)
    return pl.pallas_call(
        body, name=name, out_shape=jax.ShapeDtypeStruct((n, f2 // 2), BF16), grid=(nj, n // tb),
        in_specs=_convffn_specs(tb, nj), out_specs=pl.BlockSpec((tb, CONV_LANES), lambda j, i: (i, j)),
        compiler_params=_params(("parallel", "arbitrary"), vmem),
    )(u, u, cw, cw, cb, cb)


def _convffn_bwd(u, dhm, cw, cb, *, name, dims, xch=None):
    n, f2 = u.shape
    tb, nj = dims["nctx"], f2 // 2 // CONV_LANES

    def body(ua_ref, ug_ref, cwa_ref, cwg_ref, cba_ref, cbg_ref, dh_ref, dua_ref, dug_ref, dcwa_ref, dcwg_ref, dcba_ref,
             dcbg_ref):
        i = pl.program_id(1)
        no_left, no_right = _conv_masks(tb, i)

        @pl.when(i == 0)
        def _():
            for ref in (dcwa_ref, dcwg_ref, dcba_ref, dcbg_ref):
                ref[...] = jnp.zeros(ref.shape, F32)

        a, al, ac, ar = _conv_taps(ua_ref, cwa_ref, cba_ref, no_left, no_right, tb)
        g, gl, gc, gr = _conv_taps(ug_ref, cwg_ref, cbg_ref, no_left, no_right, tb)
        dh = dh_ref[...].astype(F32)
        sa = _sigmoid(a)
        dg = dh * (a * sa)
        da = dh * g * (sa * (1.0 + a * (1.0 - sa)))
        for dc, (tl, tc, tr), cw_ref, du_ref, dcw_ref, dcb_ref in (
                (da, (al, ac, ar), cwa_ref, dua_ref, dcwa_ref, dcba_ref),
                (dg, (gl, gc, gr), cwg_ref, dug_ref, dcwg_ref, dcbg_ref)):
            dcb_ref[...] += jnp.sum(dc, axis=0, keepdims=True)
            dcw_ref[pl.ds(0, 1), :] += jnp.sum(dc * tl, axis=0, keepdims=True)
            dcw_ref[pl.ds(1, 1), :] += jnp.sum(dc * tc, axis=0, keepdims=True)
            dcw_ref[pl.ds(2, 1), :] += jnp.sum(dc * tr, axis=0, keepdims=True)
            du = (dc * cw_ref[pl.ds(1, 1), :]
                  + pltpu.roll(jnp.where(no_left, 0.0, dc) * cw_ref[pl.ds(0, 1), :], tb - LOCAL_B, 0)
                  + pltpu.roll(jnp.where(no_right, 0.0, dc) * cw_ref[pl.ds(2, 1), :], LOCAL_B, 0))
            du_ref[...] = du.astype(BF16)

    cl, f = CONV_LANES, f2 // 2
    sd = jax.ShapeDtypeStruct
    row = pl.BlockSpec((tb, cl), lambda j, i: (i, j))
    vmem = 24 * tb * cl * 4 + (8 << 20)
    return _call(
        body, name=name, xch=xch, args=[u, u, cw, cw, cb, cb, dhm], scratch=[],
        out_shape=[sd((n, f), BF16), sd((n, f), BF16), sd((3, f), F32), sd((3, f), F32), sd((1, f), F32), sd((1, f), F32)],
        grid=(nj, n // tb), in_specs=_convffn_specs(tb, nj) + [row],
        out_specs=[row, row, pl.BlockSpec((3, cl), lambda j, i: (0, j)), pl.BlockSpec((3, cl), lambda j, i: (0, j)),
                   pl.BlockSpec((1, cl), lambda j, i: (0, j)), pl.BlockSpec((1, cl), lambda j, i: (0, j))],
        params=_params(("arbitrary", "arbitrary"), vmem))


def _s5_disc(lr, li, ls, brt, bit):
    lr = jnp.minimum(lr, S5_LAM_RE_MAX)
    dt = jnp.exp(ls)
    mag = jnp.exp(lr * dt)
    ar = mag * jnp.cos(li * dt)
    ai = mag * jnp.sin(li * dt)
    den = lr * lr + li * li
    nr = ar - 1.0
    cr = (nr * lr + ai * li) / den
    ci = (ai * lr - nr * li) / den
    return ar, ai, cr * brt - ci * bit, cr * bit + ci * brt


def _s5_disc_fwd(lr, li, ls, brt, bit):
    def body(lr_ref, li_ref, ls_ref, br_ref, bi_ref, ar_ref, ai_ref, bbr_ref, bbi_ref):
        ar, ai, bbr, bbi = _s5_disc(lr_ref[...], li_ref[...], ls_ref[...], br_ref[...], bi_ref[...])
        ar_ref[...] = ar
        ai_ref[...] = ai
        bbr_ref[...] = bbr
        bbi_ref[...] = bbi

    sd = jax.ShapeDtypeStruct
    return pl.pallas_call(body, name="s5_disc_fwd",
                          out_shape=[sd(lr.shape, F32), sd(lr.shape, F32), sd(brt.shape, F32), sd(brt.shape, F32)],
                          compiler_params=_params(None, 32 << 20))(lr, li, ls, brt, bit)


def _s5_disc_bwd(lr, li, ls, brt, bit, dar, dai, dbbr, dbbi):
    def body(lr_ref, li_ref, ls_ref, br_ref, bi_ref, dar_ref, dai_ref, dbbr_ref, dbbi_ref,
             dlr_ref, dli_ref, dls_ref, dbr_ref, dbi_ref):
        _, vjp = jax.vjp(_s5_disc, lr_ref[...], li_ref[...], ls_ref[...], br_ref[...], bi_ref[...])
        dlr, dli, dls, dbr, dbi = vjp((dar_ref[...], dai_ref[...], dbbr_ref[...], dbbi_ref[...]))
        dlr_ref[...] = dlr
        dli_ref[...] = dli
        dls_ref[...] = dls
        dbr_ref[...] = dbr
        dbi_ref[...] = dbi

    sd = jax.ShapeDtypeStruct
    return pl.pallas_call(body, name="s5_disc_bwd",
                          out_shape=[sd(lr.shape, F32), sd(lr.shape, F32), sd(ls.shape, F32), sd(brt.shape, F32),
                                     sd(brt.shape, F32)],
                          compiler_params=_params(None, 48 << 20))(lr, li, ls, brt, bit, dar, dai, dbbr, dbbi)


def _cmul(ar, ai, xr, xi):
    return ar * xr - ai * xi, ar * xi + ai * xr


def _s5_chunk_of(step, ncc, nc, rev):
    if not rev:
        return step
    return jnp.where(step < ncc, ncc - 1 - step, nc - 1 - (step - ncc))


def _s5_scan2(asc, desc, row0, nrows, a_r_ref, a_i_ref, cr_ref, ci_ref, *, lane_block, extra=None):
    width = asc[0].shape[1]
    nt = nrows // SUBLANES
    for lb in range(width // lane_block):
        lanes = pl.ds(lb * lane_block, lane_block)
        a1r, a1i = a_r_ref[:, lanes], a_i_ref[:, lanes]
        a2r, a2i = pltpu.roll(a1r, 4, 0), pltpu.roll(a1i, 4, 0)
        lo = lax.broadcasted_iota(jnp.int32, a1r.shape, 0) < 4

        def step(t, carry):
            pr, pi = carry[0], carry[1]
            ra = pl.ds(pl.multiple_of(row0 + t * SUBLANES, SUBLANES), SUBLANES)
            rd = pl.ds(pl.multiple_of(row0 + (nt - 1 - t) * SUBLANES, SUBLANES), SUBLANES)
            ur, ui = asc[0][ra, lanes], asc[1][ra, lanes]
            dr, di = desc[0][rd, lanes], desc[1][rd, lanes]
            mr, mi = _cmul(a1r, a1i, pr, pi)
            y1r, y1i = jnp.where(lo, ur, dr) + mr, jnp.where(lo, ui, di) + mi
            mr, mi = _cmul(a2r, a2i, pltpu.roll(y1r, 4, 0), pltpu.roll(y1i, 4, 0))
            y2r, y2i = jnp.where(lo, dr, ur) + mr, jnp.where(lo, di, ui) + mi
            our, oui = jnp.where(lo, y1r, y2r), jnp.where(lo, y1i, y2i)
            odr, odi = jnp.where(lo, y2r, y1r), jnp.where(lo, y2i, y1i)
            asc[0][ra, lanes] = our
            asc[1][ra, lanes] = oui
            desc[0][rd, lanes] = odr
            desc[1][rd, lanes] = odi
            nxt = (pltpu.roll(y2r, 4, 0), pltpu.roll(y2i, 4, 0))
            if extra is None:
                return nxt
            return nxt + tuple(extra(t, nt - 1 - t, lanes, (our, oui), (odr, odi), carry[2:]))

        init = (cr_ref[:, lanes], ci_ref[:, lanes])
        if extra is not None:
            init = init + tuple(extra.init(lanes))
        out = lax.fori_loop(0, nt, step, init)
        cr_ref[:, lanes] = out[0]
        ci_ref[:, lanes] = out[1]
        if extra is not None:
            extra.done(lanes, out[2:])


S5_SPLIT = 2


def _s5_fwd(u, af_r, af_i, bb, cc, dsk, *, name, dims, xch=None):
    n, dm = u.shape
    nk, swk = bb[0].shape[0], bb[0].shape[2]
    rr, sw = dims["s5_rows"], nk * swk
    nkh, dmh, swh = nk // S5_SPLIT, dm // S5_SPLIT, sw // S5_SPLIT
    nc, ncc = n // rr, dims["nctx"] // rr
    c1 = lambda i: _s5_chunk_of(i, ncc, nc, True)

    def body(u0_ref, u1_ref, afr_ref, afi_ref, b0r, b0i, b1r, b1i, c0r, c0i, c1r, c1i, dsk_ref,
             y0_ref, y1_ref, str_ref, sti_ref, s0r, s0i, s1r, s1i, cr, ci):
        @pl.when(pl.program_id(1) == 0)
        def _():
            cr[...] = jnp.zeros(cr.shape, F32)
            ci[...] = jnp.zeros(ci.shape, F32)

        str_ref[...] = cr[...]
        sti_ref[...] = ci[...]
        ub0, ub1 = u0_ref[...].astype(BF16), u1_ref[...].astype(BF16)
        for k in range(nkh):
            cols, sl = slice(k * LANES, (k + 1) * LANES), slice(k * swk, (k + 1) * swk)
            s0r[:, sl] = _dot(ub0[:, cols], b0r[k])
            s0i[:, sl] = _dot(ub0[:, cols], b0i[k])
            s1r[:, sl] = _dot(ub1[:, cols], b1r[k])
            s1i[:, sl] = _dot(ub1[:, cols], b1i[k])
        _s5_scan2((s0r, s0i), (s1r, s1i), 0, rr, afr_ref, afi_ref, cr, ci, lane_block=dims["s5_lane_block"])
        for k in range(nkh):
            cols, sl = slice(k * LANES, (k + 1) * LANES), slice(k * swk, (k + 1) * swk)
            y0_ref[:, cols] = (_dot(s0r[:, sl].astype(BF16), c0r[k]) - _dot(s0i[:, sl].astype(BF16), c0i[k])
                               + dsk_ref[:, cols] * u0_ref[:, cols].astype(F32)).astype(BF16)
            y1_ref[:, cols] = (_dot(s1r[:, sl].astype(BF16), c1r[k])
                               - _dot(s1i[:, sl].astype(BF16), c1i[k])).astype(BF16)

    row0 = pl.BlockSpec((rr, dmh), lambda h, i: (i, h))
    row1 = pl.BlockSpec((rr, dmh), lambda h, i: (c1(i), h))
    tile = pl.BlockSpec((SUBLANES, swh), lambda h, i: (0, h))
    wspec = lambda a: pl.BlockSpec((nkh,) + a.shape[1:], lambda h, i: (h, 0, 0))
    st_spec = pl.BlockSpec((None, SUBLANES, swh), lambda h, i: (i, 0, h))
    sd = jax.ShapeDtypeStruct
    vmem = 4 * rr * swh * 4 + 12 * rr * dmh * 4 + 16 * nkh * LANES * swk * 2 + (12 << 20)
    return _call(
        body, name=name, xch=xch, args=[u, u, af_r, af_i, *bb, *cc, dsk],
        out_shape=[sd((n, dm), BF16), sd((n, dm), BF16), sd((nc, SUBLANES, sw), F32), sd((nc, SUBLANES, sw), F32)],
        grid=(S5_SPLIT, nc),
        in_specs=[row0, row1, tile, tile] + [wspec(a) for a in (*bb, *cc)] + [pl.BlockSpec((1, dmh), lambda h, i: (0, h))],
        out_specs=[row0, row1, st_spec, st_spec],
        scratch=[pltpu.VMEM((rr, swh), F32)] * 4 + [pltpu.VMEM((SUBLANES, swh), F32)] * 2,
        params=_params(("arbitrary", "arbitrary"), vmem))


class _DaHook2:
    def __init__(self, s0, s1, accs):
        self.s0, self.s1, self.accs = s0, s1, accs

    def init(self, lanes):
        return tuple(a[:, lanes] for a in self.accs)

    def done(self, lanes, acc):
        for a, v in zip(self.accs, acc):
            a[:, lanes] = v

    def __call__(self, t1, t0, lanes, l1, l0, acc):
        row = lax.broadcasted_iota(jnp.int32, l1[0].shape, 0)
        b1 = pl.multiple_of(SUBLANES + t1 * SUBLANES, SUBLANES)
        b0 = pl.multiple_of(SUBLANES + t0 * SUBLANES, SUBLANES)
        cur1, nxt1 = pl.ds(b1, SUBLANES), pl.ds(pl.multiple_of(b1 + SUBLANES, SUBLANES), SUBLANES)
        cur0, prv0 = pl.ds(b0, SUBLANES), pl.ds(pl.multiple_of(b0 - SUBLANES, SUBLANES), SUBLANES)
        p1r = pltpu.roll(jnp.where(row >= 4, self.s1[0][cur1, lanes], self.s1[0][nxt1, lanes]), 4, 0)
        p1i = pltpu.roll(jnp.where(row >= 4, self.s1[1][cur1, lanes], self.s1[1][nxt1, lanes]), 4, 0)
        p0r = pltpu.roll(jnp.where(row >= 4, self.s0[0][prv0, lanes], self.s0[0][cur0, lanes]), 4, 0)
        p0i = pltpu.roll(jnp.where(row >= 4, self.s0[1][prv0, lanes], self.s0[1][cur0, lanes]), 4, 0)
        return (acc[0] + p0r * l0[0] + p0i * l0[1], acc[1] + p0r * l0[1] - p0i * l0[0],
                acc[2] + p1r * l1[0] + p1i * l1[1], acc[3] + p1r * l1[1] - p1i * l1[0])


def _s5_bwd(u, dy, af_r, af_i, ab_r, ab_i, bb, cc, dsk, st_r, st_i, *, name, dims, xch=None):
    n, dm = u.shape
    nk, swk = bb[0].shape[0], bb[0].shape[2]
    rr, sw = dims["s5_rows"], nk * swk
    nkh, dmh, swh = nk // S5_SPLIT, dm // S5_SPLIT, sw // S5_SPLIT
    nc, ncc = n // rr, dims["nctx"] // rr
    f0 = lambda i: nc - 1 - i
    f1 = lambda i: _s5_chunk_of(nc - 1 - i, ncc, nc, True)

    def body(u0_ref, u1_ref, dy0_ref, dy1_ref, afr_ref, afi_ref, abr_ref, abi_ref, b0r, b0i, b1r, b1i, c0r, c0i, c1r, c1i,
             dsk_ref, str_ref, sti_ref,
             du0_ref, du1_ref, db0r, db0i, db1r, db1i, dc0r, dc0i, dc1r, dc1i, da0r_ref, da0i_ref, da1r_ref, da1i_ref, dd_ref,
             s0r, s0i, s1r, s1i, l0r, l0i, l1r, l1i, cr, ci, lcr, lci, a0r, a0i, a1r, a1i, dda):
        i = pl.program_id(1)

        @pl.when(i == 0)
        def _():
            for ref in (lcr, lci, a0r, a0i, a1r, a1i, dda, db0r, db0i, db1r, db1i, dc0r, dc0i, dc1r, dc1i):
                ref[...] = jnp.zeros(ref.shape, F32)

        row = lax.broadcasted_iota(jnp.int32, (SUBLANES, swh), 0)
        for st_ref, car, z0, z1 in ((str_ref, cr, s0r, s1r), (sti_ref, ci, s0i, s1i)):
            st = st_ref[...]
            car[...] = st
            z0[pl.ds(0, SUBLANES), :] = jnp.where(row < 4, st, pltpu.roll(st, 4, 0))
            z1[pl.ds(rr + SUBLANES, SUBLANES), :] = jnp.where(row >= 4, st, pltpu.roll(st, 4, 0))
        body_rows = pl.ds(SUBLANES, rr)
        ub0, ub1 = u0_ref[...].astype(BF16), u1_ref[...].astype(BF16)
        dyb0, dyb1 = dy0_ref[...].astype(BF16), dy1_ref[...].astype(BF16)
        for k in range(nkh):
            cols, sl = slice(k * LANES, (k + 1) * LANES), slice(k * swk, (k + 1) * swk)
            s0r[body_rows, sl] = _dot(ub0[:, cols], b0r[k])
            s0i[body_rows, sl] = _dot(ub0[:, cols], b0i[k])
            s1r[body_rows, sl] = _dot(ub1[:, cols], b1r[k])
            s1i[body_rows, sl] = _dot(ub1[:, cols], b1i[k])
        _s5_scan2((s0r, s0i), (s1r, s1i), SUBLANES, rr, afr_ref, afi_ref, cr, ci, lane_block=dims["s5_lane_block"])
        for k in range(nkh):
            cols, sl = slice(k * LANES, (k + 1) * LANES), slice(k * swk, (k + 1) * swk)
            for dyk, lr, li, sr, si, ccr, cci, dcr, dci in ((dyb0[:, cols], l0r, l0i, s0r, s0i, c0r, c0i, dc0r, dc0i),
                                                           (dyb1[:, cols], l1r, l1i, s1r, s1i, c1r, c1i, dc1r, dc1i)):
                lr[:, sl] = _dot(dyk, ccr[k], NT)
                li[:, sl] = -_dot(dyk, cci[k], NT)
                dcr[k] += _dot(dyk, sr[body_rows, sl].astype(BF16), TN)
                dci[k] -= _dot(dyk, si[body_rows, sl].astype(BF16), TN)
        _s5_scan2((l1r, l1i), (l0r, l0i), 0, rr, abr_ref, abi_ref, lcr, lci, lane_block=dims["s5_lane_block"] // 2,
                  extra=_DaHook2((s0r, s0i), (s1r, s1i), (a0r, a0i, a1r, a1i)))
        for k in range(nkh):
            cols, sl = slice(k * LANES, (k + 1) * LANES), slice(k * swk, (k + 1) * swk)
            for uk, lr, li, br, bi, dbr, dbi, du_ref, first in ((ub0[:, cols], l0r, l0i, b0r, b0i, db0r, db0i, du0_ref, True),
                                                              (ub1[:, cols], l1r, l1i, b1r, b1i, db1r, db1i, du1_ref, False)):
                lrk, lik = lr[:, sl].astype(BF16), li[:, sl].astype(BF16)
                dbr[k] += _dot(uk, lrk, TN)
                dbi[k] += _dot(uk, lik, TN)
                duk = _dot(lrk, br[k], NT) + _dot(lik, bi[k], NT)
                if first:
                    duk = duk + dsk_ref[:, cols] * dy0_ref[:, cols]
                du_ref[:, cols] = duk
        dda[...] += jnp.sum((dy0_ref[...] * u0_ref[...].astype(F32)).reshape(rr // SUBLANES, SUBLANES, dmh), axis=0)

        @pl.when(i == nc - 1)
        def _():
            for o, a in ((da0r_ref, a0r), (da0i_ref, a0i), (da1r_ref, a1r), (da1i_ref, a1i), (dd_ref, dda)):
                o[...] = jnp.sum(a[...], axis=0, keepdims=True)

    row0 = pl.BlockSpec((rr, dmh), lambda h, i: (f0(i), h))
    row1 = pl.BlockSpec((rr, dmh), lambda h, i: (f1(i), h))
    tile = pl.BlockSpec((SUBLANES, swh), lambda h, i: (0, h))
    wspec = lambda a: pl.BlockSpec((nkh,) + a.shape[1:], lambda h, i: (h, 0, 0))
    st_spec = pl.BlockSpec((None, SUBLANES, swh), lambda h, i: (f0(i), 0, h))
    vec = lambda w: pl.BlockSpec((1, w), lambda h, i: (0, h))
    sd = jax.ShapeDtypeStruct
    out_shape = ([sd((n, dm), F32)] * 2 + [sd(a.shape, F32) for a in (*bb, *bb)] + [sd((1, sw), F32)] * 4 + [sd((1, dm), F32)])
    out_specs = [row0, row1] + [wspec(a) for a in (*bb, *bb)] + [vec(swh)] * 4 + [vec(dmh)]
    scratch = ([pltpu.VMEM((rr + 2 * SUBLANES, swh), F32)] * 4 + [pltpu.VMEM((rr, swh), F32)] * 4
               + [pltpu.VMEM((SUBLANES, swh), F32)] * 8 + [pltpu.VMEM((SUBLANES, dmh), F32)])
    vmem = 8 * (rr + 16) * swh * 4 + 16 * rr * dmh * 4 + 48 * nkh * LANES * swk * 4 + (10 << 20)
    return _call(
        body, name=name, xch=xch, args=[u, u, dy, dy, af_r, af_i, ab_r, ab_i, *bb, *cc, dsk, st_r, st_i],
        out_shape=out_shape, grid=(S5_SPLIT, nc),
        in_specs=[row0, row1, row0, row1, tile, tile, tile, tile] + [wspec(a) for a in (*bb, *cc)] + [vec(dmh), st_spec, st_spec],
        out_specs=out_specs, scratch=scratch, params=_params(("arbitrary", "arbitrary"), vmem))


def _hg_mask(kind, rev):
    if kind == "tot":
        r = lax.broadcasted_iota(jnp.int32, (SUBLANES, HG_ROWS), 0)
        c = lax.broadcasted_iota(jnp.int32, (SUBLANES, HG_ROWS), 1)
        return (c & 3) == r
    r = lax.broadcasted_iota(jnp.int32, (HG_ROWS, HG_ROWS), 0)
    c = lax.broadcasted_iota(jnp.int32, (HG_ROWS, HG_ROWS), 1)
    same = (r & 3) == (c & 3)
    before = ((c >> 2) >= (r >> 2)) if rev else ((c >> 2) <= (r >> 2))
    return jnp.logical_and(same, before if kind == "cum" else jnp.logical_not(before))


def _split2(x):
    hi = x.astype(BF16)
    return hi, (x - hi.astype(F32)).astype(BF16)


@functools.partial(jax.custom_vjp, nondiff_argnums=(1, 2))
def _mask_sum(x, kind, rev):
    m = _hg_mask(kind, rev).astype(BF16)
    hi, lo = _split2(x)
    return _dot(m, hi) + _dot(m, lo)


def _mask_sum_fwd(x, kind, rev):
    return _mask_sum(x, kind, rev), None


def _mask_sum_bwd(kind, rev, _, g):
    m = _hg_mask(kind, rev).astype(BF16)
    hi, lo = _split2(g)
    return (_dot(m, hi, TN) + _dot(m, lo, TN),)


_mask_sum.defvjp(_mask_sum_fwd, _mask_sum_bwd)


def _hg_chunk(q, v, fraw, l0, l1, st, *, rev):
    nh = q.shape[1] // HG_HEAD
    lb = _sigmoid(l1 - l0)
    logf = jnp.logaddexp(jnp.log(lb), jnp.log1p(-lb) + jax.nn.log_sigmoid(fraw))
    kk = (1.0 - lb) * _sigmoid(fraw * -1.0)
    tri = _hg_mask("cum", rev)
    bcum = _mask_sum(logf, "cum", rev)
    brem = _mask_sum(logf, "rem", rev)
    bend8 = _mask_sum(logf, "tot", rev)
    r8d = lax.broadcasted_iota(jnp.int32, bend8.shape, 0)
    decs = [jnp.exp(jnp.sum(jnp.where(r8d == b, bend8, 0.0), axis=0, keepdims=True)) for b in range(LOCAL_B)]
    qd = (q * jnp.exp(bcum)).astype(BF16)
    kd = (kk * jnp.exp(-bcum)).astype(BF16)
    ke = (kk * jnp.exp(brem)).astype(BF16)
    wide = (HG_ROWS, LOCAL_B * HG_HEAD)
    mine = (lax.broadcasted_iota(jnp.int32, wide, 1) >> 7) == (lax.broadcasted_iota(jnp.int32, wide, 0) & 3)
    per_example = lambda x: jnp.where(mine, jnp.concatenate([x] * LOCAL_B, axis=1), jnp.zeros(wide, x.dtype))
    outs, new = [], []
    for h in range(nh):
        sl = slice(h * HG_HEAD, (h + 1) * HG_HEAD)
        vh = v[:, sl].astype(BF16)
        att = jnp.where(tri, _dot(qd[:, sl], kd[:, sl], NT), 0.0)
        outs.append(_dot(att.astype(BF16), vh) + _dot(per_example(qd[:, sl]), st[h].astype(BF16), NT))
        dec = jnp.concatenate([d[:, sl] for d in decs], axis=1)
        new.append(st[h] * dec + _dot(vh, per_example(ke[:, sl]), TN))
    return jnp.concatenate(outs, axis=1), tuple(new)


def _hg_chunk_of(step, ncc, nc, rev):
    return _s5_chunk_of(step, ncc, nc, rev)


def _hg_fwd_dir(zz, lb2, *, d, name, dims, xch=None):
    n = zz.shape[0]
    dm = zz.shape[1] // 5
    ns, sw = dm // HG_HEAD, LOCAL_B * HG_HEAD
    nc, ncc = n // HG_ROWS, dims["nctx"] // HG_ROWS
    rev = d == 1
    ch = lambda i: _hg_chunk_of(i, ncc, nc, rev)

    def body(q_ref, v_ref, f_ref, l0_ref, l1_ref, o_ref, st_ref, st):
        @pl.when(pl.program_id(0) == 0)
        def _():
            st[...] = jnp.zeros(st.shape, F32)

        st_ref[...] = st[...]
        o, new = _hg_chunk(q_ref[...].astype(F32), v_ref[...].astype(F32), f_ref[...].astype(F32), l0_ref[...],
                           l1_ref[...], tuple(st[j] for j in range(ns)), rev=rev)
        o_ref[...] = o.astype(BF16)
        for j in range(ns):
            st[j] = new[j]

    blk = lambda off: pl.BlockSpec((HG_ROWS, dm), lambda i, off=off: (ch(i), off))
    lspec = lambda layer: pl.BlockSpec((None, None, 1, dm), lambda i, layer=layer: (d, layer, 0, 0))
    return _call(
        body, name=name, xch=xch, args=[zz, zz, zz, lb2, lb2],
        out_shape=[jax.ShapeDtypeStruct((n, dm), BF16), jax.ShapeDtypeStruct((nc, ns, HG_HEAD, sw), F32)],
        grid=(nc,),
        in_specs=[blk(0), blk(1), blk(2 + d), lspec(0), lspec(1)],
        out_specs=[pl.BlockSpec((HG_ROWS, dm), lambda i: (ch(i), 0)),
                   pl.BlockSpec((None, ns, HG_HEAD, sw), lambda i: (ch(i), 0, 0, 0))],
        scratch=[pltpu.VMEM((ns, HG_HEAD, sw), F32)],
        params=_params(("arbitrary",), 48 << 20))


def _hg_bwd_dir(zz, lb2, do, sts, dqv_prev, *, d, name, dims, xch=None):
    n = zz.shape[0]
    dm = zz.shape[1] // 5
    ns, sw = dm // HG_HEAD, LOCAL_B * HG_HEAD
    nc, ncc = n // HG_ROWS, dims["nctx"] // HG_ROWS
    rev = d == 1
    ch = lambda i: _hg_chunk_of(nc - 1 - i, ncc, nc, rev)
    qv_dtype = F32 if d == 0 else BF16

    def body(*refs):
        q_ref, v_ref, f_ref, l0_ref, l1_ref, do_ref, st_ref = refs[:7]
        pos = 7
        if d == 1:
            dqp_ref, dvp_ref = refs[7:9]
            pos = 9
        dq_ref, dv_ref, df_ref, dl_ref, dst = refs[pos:]
        i = pl.program_id(0)

        @pl.when(i == 0)
        def _():
            dst[...] = jnp.zeros(dst.shape, F32)
            dl_ref[...] = jnp.zeros(dl_ref.shape, F32)

        _, vjp = jax.vjp(functools.partial(_hg_chunk, rev=rev), q_ref[...].astype(F32), v_ref[...].astype(F32),
                         f_ref[...].astype(F32), l0_ref[...], l1_ref[...], tuple(st_ref[j] for j in range(ns)))
        dq, dv, df, dl0, dl1, dstn = vjp((do_ref[...].astype(F32), tuple(dst[j] for j in range(ns))))
        for j in range(ns):
            dst[j] = dstn[j]
        if d == 1:
            dq = dq + dqp_ref[...]
            dv = dv + dvp_ref[...]
        dq_ref[...] = dq.astype(qv_dtype)
        dv_ref[...] = dv.astype(qv_dtype)
        df_ref[...] = df.astype(BF16)
        dl_ref[0] += dl0
        dl_ref[1] += dl1

    blk = lambda off: pl.BlockSpec((HG_ROWS, dm), lambda i, off=off: (ch(i), off))
    oblk = pl.BlockSpec((HG_ROWS, dm), lambda i: (ch(i), 0))
    lspec = lambda layer: pl.BlockSpec((None, None, 1, dm), lambda i, layer=layer: (d, layer, 0, 0))
    ins = [zz, zz, zz, lb2, lb2, do, sts] + (list(dqv_prev) if d == 1 else [])
    in_specs = [blk(0), blk(1), blk(2 + d), lspec(0), lspec(1), oblk,
                pl.BlockSpec((None, ns, HG_HEAD, sw), lambda i: (ch(i), 0, 0, 0))]
    in_specs += [oblk, oblk] if d == 1 else []
    sd = jax.ShapeDtypeStruct
    return _call(
        body, name=name, xch=xch, args=ins,
        out_shape=[sd((n, dm), qv_dtype), sd((n, dm), qv_dtype), sd((n, dm), BF16), sd((2, 1, dm), F32)],
        grid=(nc,), in_specs=in_specs,
        out_specs=[oblk, oblk, oblk, pl.BlockSpec((2, 1, dm), lambda i: (0, 0, 0))],
        scratch=[pltpu.VMEM((ns, HG_HEAD, sw), F32)],
        params=_params(("arbitrary",), 56 << 20))


def _hg_readout(o, g, w):
    outs = []
    for h in range(o.shape[-1] // HG_HEAD):
        sl = slice(h * HG_HEAD, (h + 1) * HG_HEAD)
        oh = o[..., sl]
        outs.append(oh * _rms(oh) * w * _sigmoid(g[..., sl]))
    return jnp.concatenate(outs, axis=-1)


def _silu(x):
    return x * _sigmoid(x)


def _mod_fwd(craw, w, b):
    def body(c_ref, w_ref, b_ref, o_ref):
        s = _silu(c_ref[...]).astype(BF16)
        for layer in range(w.shape[0]):
            o_ref[layer] = _dot(s, w_ref[layer].astype(BF16)) + b_ref[layer]

    return pl.pallas_call(body, name="mod_fwd",
                          out_shape=jax.ShapeDtypeStruct((w.shape[0], craw.shape[0], w.shape[2]), F32),
                          compiler_params=_params(None, 40 << 20))(craw, w, b)


def _mod_bwd(craw, w, dlat_sh, dctx_sh, dlat_full, dctx_full):
    nl, dm, ns = w.shape
    nb = dlat_sh.shape[1]

    def body(c_ref, w_ref, dl_ref, dc_ref, dlf_ref, dcf_ref, dw_ref, db_ref, dcc_ref):
        craw_v = c_ref[...]
        s = _silu(craw_v)
        s_lat = s[:nb].astype(BF16)
        s_ctx = s[nb:].astype(BF16)
        row = lax.broadcasted_iota(jnp.int32, (SUBLANES, ns), 0)
        dsc = jnp.zeros((SUBLANES, dm), F32)
        for layer in range(nl):
            tot = dc_ref[0, pl.ds(layer, 1), :]
            totf = dcf_ref[0, pl.ds(layer, 1), :]
            for i in range(1, NDEV):
                tot = tot + dc_ref[i, pl.ds(layer, 1), :]
                totf = totf + dcf_ref[i, pl.ds(layer, 1), :]
            dc8 = jnp.where(row == 0, jnp.broadcast_to(tot, (SUBLANES, ns)), 0.0).astype(BF16)
            dw_ref[layer] = _dot(s_lat, dl_ref[layer].astype(BF16), TN) + _dot(s_ctx, dc8, TN)
            db_ref[layer] = jnp.sum(dlf_ref[layer], axis=0, keepdims=True) + totf
            dsc = dsc + _dot(dc8, w_ref[layer].astype(BF16), NT)
        cc = craw_v[nb:]
        sg = _sigmoid(cc)
        dcc_ref[...] = dsc * (sg * (1.0 + cc * (1.0 - sg)))

    sd = jax.ShapeDtypeStruct
    return pl.pallas_call(body, name="mod_bwd",
                          out_shape=[sd((nl, dm, ns), F32), sd((nl, 1, dlat_full.shape[2]), F32), sd((SUBLANES, dm), F32)],
                          compiler_params=_params(None, 48 << 20))(craw, w, dlat_sh, dctx_sh, dlat_full, dctx_full)


def _adam_rows(r):
    best = None
    for t in range(2 * SUBLANES, min(r, 128) + 1, 2 * SUBLANES):
        if r % t == 0:
            best = t
    return best if best is not None else r


def _adamw(parts, w, m, v, *, name):
    npart, r, c = parts.shape
    tr = _adam_rows(r)

    def body(p_ref, w_ref, m_ref, v_ref, g_ref, d_ref, nm_ref, nv_ref):
        g = p_ref[0].astype(F32)
        for i in range(1, npart):
            g = g + p_ref[i].astype(F32)
        nm = ADAM_B1 * m_ref[...] + (1.0 - ADAM_B1) * g
        nv = ADAM_B2 * v_ref[...] + (1.0 - ADAM_B2) * (g * g)
        m_hat = nm / (1.0 - ADAM_B1 ** ADAM_STEP)
        v_hat = nv / (1.0 - ADAM_B2 ** ADAM_STEP)
        g_ref[...] = g
        d_ref[...] = -ADAM_LR * (m_hat / (jnp.sqrt(v_hat) + ADAM_EPS) + ADAM_WD * w_ref[...])
        nm_ref[...] = nm
        nv_ref[...] = nv

    spec = pl.BlockSpec((tr, c), lambda i: (i, 0))
    vmem = 2 * (npart + 7) * tr * c * 4 + (8 << 20)
    return pl.pallas_call(
        body, name=name, out_shape=[jax.ShapeDtypeStruct((r, c), F32)] * 4, grid=(r // tr,),
        in_specs=[pl.BlockSpec((npart, tr, c), lambda i: (0, i, 0)), spec, spec, spec], out_specs=[spec] * 4,
        compiler_params=_params(("parallel",), vmem),
    )(parts, w, m, v)


def _to_tm(a):
    return jnp.transpose(a, (1, 0, 2)).reshape(a.shape[1] * a.shape[0], a.shape[2])


def _pattern(mod_lat, mod_ctx, m, dm):
    lat = mod_lat[:, m * dm:(m + 1) * dm]
    ctx = jnp.broadcast_to(mod_ctx[None, m * dm:(m + 1) * dm], (SUBLANES, dm))
    return jnp.stack([ctx, jnp.concatenate([lat, lat], axis=0)])


def _blockdiag_b(bt, nk):
    g, h, p = bt.shape
    t = bt.reshape(nk, 8, h, p)
    return jnp.einsum("kghp,gj->kghjp", t, jnp.eye(8, dtype=bt.dtype)).reshape(nk, 8 * h, 8 * p)


def _blockdiag_c(ct, nk):
    g, h, p = ct.shape
    t = ct.reshape(nk, 8, h, p)
    return jnp.einsum("kghp,gj->kgpjh", t, jnp.eye(8, dtype=ct.dtype)).reshape(nk, 8 * p, 8 * h)


def _diag_b(dbb, h, p):
    nk = dbb.shape[0]
    return jnp.einsum("kghgp->kghp", dbb.reshape(nk, 8, h, 8, p)).reshape(nk * 8, h, p)


def _s5_prep(lam_re, lam_im, log_step, b_re, b_im, c_re, c_im, dm):
    ngrp, nk = dm // S5_GROUP, dm // LANES
    sw = ngrp * S5_STATE
    lr4 = lam_re.reshape(2, ngrp, 1, S5_STATE)
    li4 = lam_im.reshape(2, ngrp, 1, S5_STATE)
    ls4 = log_step.reshape(2, ngrp, 1, 1)
    brt = jnp.transpose(b_re, (0, 1, 3, 2))
    bit = jnp.transpose(b_im, (0, 1, 3, 2))
    abar_r, abar_i, bbar_r, bbar_i = _s5_disc_fwd(lr4, li4, ls4, brt, bit)
    half = lambda a, d: jnp.broadcast_to(a[d].reshape(1, sw), (LOCAL_B, sw))
    tile = lambda a, first: jnp.concatenate([half(a, first), half(a, 1 - first)], axis=0)
    bb = tuple(_blockdiag_b(w[d], nk).astype(BF16) for d in range(2) for w in (bbar_r, bbar_i))
    cc = tuple(_blockdiag_c(w[d], nk).astype(BF16) for d in range(2) for w in (c_re, c_im))
    return dict(disc_in=(lr4, li4, ls4, brt, bit), af=(tile(abar_r, 0), tile(abar_i, 0)),
                ab=(tile(abar_r, 1), -tile(abar_i, 1)), bb=bb, cc=cc)


def _s5_param_grads(prep, dbb, dcc, da):
    lr4 = prep["disc_in"][0]
    dar = jnp.stack([da[0], da[2]]).reshape(lr4.shape)
    dai = jnp.stack([da[1], da[3]]).reshape(lr4.shape)
    dbbr = jnp.stack([_diag_b(dbb[0], S5_GROUP, S5_STATE), _diag_b(dbb[2], S5_GROUP, S5_STATE)])
    dbbi = jnp.stack([_diag_b(dbb[1], S5_GROUP, S5_STATE), _diag_b(dbb[3], S5_GROUP, S5_STATE)])
    dlr, dli, dls, dbrt, dbit = _s5_disc_bwd(*prep["disc_in"], dar, dai, dbbr, dbbi)
    g_c_re = jnp.stack([_diag_b(dcc[0], S5_GROUP, S5_STATE), _diag_b(dcc[2], S5_GROUP, S5_STATE)])
    g_c_im = jnp.stack([_diag_b(dcc[1], S5_GROUP, S5_STATE), _diag_b(dcc[3], S5_GROUP, S5_STATE)])
    return dlr, dli, dls, jnp.transpose(dbrt, (0, 1, 3, 2)), jnp.transpose(dbit, (0, 1, 3, 2)), g_c_re, g_c_im


def kernel(x, c, ctx, c_ctx, w_mod, b_mod, norm1_w, norm2_w, final_norm_w, s5_w_in, s5_lam_re, s5_lam_im, s5_log_step, s5_b_re, s5_b_im, s5_c_re, s5_c_im, s5_d, s5_w_glu, s5_w_out, hg_w_in, hg_lower_bounds, hg_gnorm_w, hg_w_out, ffn_w_up, ffn_conv_w, ffn_conv_b, ffn_w_down, loss_target, m_c_ctx, m_w_mod, m_b_mod, m_norm1_w, m_norm2_w, m_final_norm_w, m_s5_w_in, m_s5_lam_re, m_s5_lam_im, m_s5_log_step, m_s5_b_re, m_s5_b_im, m_s5_c_re, m_s5_c_im, m_s5_d, m_s5_w_glu, m_s5_w_out, m_hg_w_in, m_hg_lower_bounds, m_hg_gnorm_w, m_hg_w_out, m_ffn_w_up, m_ffn_conv_w, m_ffn_conv_b, m_ffn_w_down, v_c_ctx, v_w_mod, v_b_mod, v_norm1_w, v_norm2_w, v_final_norm_w, v_s5_w_in, v_s5_lam_re, v_s5_lam_im, v_s5_log_step, v_s5_b_re, v_s5_b_im, v_s5_c_re, v_s5_c_im, v_s5_d, v_s5_w_glu, v_s5_w_out, v_hg_w_in, v_hg_lower_bounds, v_hg_gnorm_w, v_hg_w_out, v_ffn_w_up, v_ffn_conv_w, v_ffn_conv_b, v_ffn_w_down):
    given = dict(locals())
    bsz, lx, dm = x.shape
    lc = ctx.shape[1]
    assert bsz == LOCAL_B and w_mod.shape[0] == 2 and dm % LANES == 0
    n, nctx = (lc + lx) * bsz, lc * bsz
    ngrp, nstate, hgrp = dm // S5_GROUP, S5_STATE, S5_GROUP
    nk = dm // LANES
    dims = dict(nctx=nctx, tm=min(512, nctx), tm_row=min(512, nctx), s5_rows=min(256, nctx),
                s5_lane_block=min(512, 8 * nstate))
    tm = dims["tm"]
    assert nctx % HG_ROWS == 0 and (lx * bsz) % nctx == 0 and lc % GRID_W == 0 and lc & (lc - 1) == 0
    me = 4 * lax.axis_index("x") + 2 * lax.axis_index("y") + lax.axis_index("c")

    gath = _exchange([given[k].astype(BF16) for k in ("s5_w_in", "s5_w_glu", "s5_w_out")]
                     + [c, hg_lower_bounds, ffn_conv_w], a2a=False, name="gather_weights")
    w_s5in, w_glu, w_s5out = (g.reshape(dm, dm) for g in gath[:3])
    c_all, lb_all, cw_all = gath[3:]
    ns_up = ffn_w_up.shape[2]
    w_up, w_dn = [None, None], [None, None]
    cols = lambda g: jnp.transpose(g, (1, 0, 2)).reshape(g.shape[1], -1)
    shards = lambda w: jnp.transpose(w.reshape(w.shape[0], NDEV, -1), (1, 0, 2))
    tn_up, tn_hg, tkr = 2 * ns_up, 2 * hg_w_in.shape[2], 1152
    assert n % tkr == 0 and n % 1024 == 0
    gather = lambda arrs: _Xchg([a.astype(BF16) for a in arrs], [False] * len(arrs))
    scatter = lambda arrs: _Xchg(arrs, [True] * len(arrs))
    cw = [cols(cw_all[:, layer]) for layer in range(2)]
    cb = [ffn_conv_b[layer].reshape(1, -1) for layer in range(2)]
    lb2 = jnp.transpose(lb_all, (1, 2, 0, 3)).reshape(2, 2, 1, dm)

    nsm = w_mod.shape[2]
    craw = jnp.concatenate([c_all.reshape(NDEV * bsz, dm), c_ctx[None], jnp.zeros((SUBLANES - 1, dm), F32)], axis=0)
    b_sh = lax.dynamic_slice(b_mod, (0, me * nsm), (2, nsm)).reshape(2, 1, nsm)
    mod_sh = _mod_fwd(craw, w_mod, b_sh)
    (mod_g,) = _exchange([mod_sh], a2a=False, name="gather_mod")
    mod_full = jnp.transpose(mod_g, (1, 2, 0, 3)).reshape(2, craw.shape[0], NDEV * nsm)
    pat = []
    for layer in range(2):
        mlat = lax.dynamic_slice(mod_full[layer], (me * bsz, 0), (bsz, N_MOD * dm))
        mctx = mod_full[layer, NDEV * bsz]
        pat.append([_pattern(mlat, mctx, m, dm) for m in range(N_MOD)])

    s5p = _s5_prep(s5_lam_re[0], s5_lam_im[0], s5_log_step[0], s5_b_re[0], s5_b_im[0], s5_c_re[0], s5_c_im[0], dm)
    dsk = s5_d.reshape(1, dm)

    z0 = jnp.concatenate([_to_tm(ctx), _to_tm(x)], axis=0)
    tgt = _to_tm(loss_target)
    n1w = [norm1_w[layer].reshape(1, dm) for layer in range(2)]
    n2w = [norm2_w[layer].reshape(1, dm) for layer in range(2)]

    def ffn_fwd(layer, h2):
        u = _lin(h2, w_up[layer], name=f"ffn_up{layer}", tm=1024, tn=tn_up, o_dtype=BF16)
        hm = _convffn_fwd(u, cw[layer], cb[layer], name=f"convffn_fwd{layer}", dims=dims)
        f = _lin(hm, w_dn[layer], name=f"ffn_down{layer}", tm=tm, tn=dm, o_dtype=BF16)
        return u, hm, f

    _, h0 = _norm_mod_fwd(z0, n1w[0], pat[0][0], pat[0][1], name="norm1_l0", dims=dims)
    u_s5 = _lin(h0, w_s5in, name="s5_in", tm=1024, tn=dm, o_dtype=BF16)
    (y_s5a, y_s5b, st_r, st_i), (g_up0, g_dn0, g_hgin, g_hgout, g_dn1) = _s5_fwd(
        u_s5, *s5p["af"], s5p["bb"], s5p["cc"], dsk, name="s5_fwd", dims=dims,
        xch=gather([ffn_w_up[0], ffn_w_down[0], hg_w_in[0], hg_w_out[0], ffn_w_down[1]]))
    w_up[0], w_dn[0] = cols(g_up0), g_dn0.reshape(-1, dm)
    w_hgin, w_hgout, w_dn[1] = cols(g_hgin), g_hgout.reshape(dm, dm), g_dn1.reshape(-1, dm)
    (zg,) = _rowk(lambda rv, pv, cv, il: ([_gelu(rv[0] + rv[1])], [], []), name="s5_gelu", n=n, tm=dims["tm_row"],
                  nctx=nctx, rows=[(y_s5a, dm, 0, 0), (y_s5b, dm, 0, 0)], out_rows=[(dm, BF16, dm, 0)])
    t_glu = _lin(zg, w_glu, name="s5_glu", tm=1024, tn=dm, o_dtype=BF16)
    (z2g,) = _rowk(lambda rv, pv, cv, il: ([rv[0] * _sigmoid(rv[1])], [], []), name="s5_gate", n=n,
                   tm=dims["tm_row"], nctx=nctx, rows=[(zg, dm, 0, 0), (t_glu, dm, 0, 0)],
                   out_rows=[(dm, BF16, dm, 0)])
    ymix0 = _lin(z2g, w_s5out, name="s5_out", tm=1024, tn=dm, o_dtype=BF16)
    z1_l0, h2_l0 = _norm_mod_fwd(z0, n2w[0], pat[0][3], pat[0][4], name="norm2_l0", dims=dims,
                                 res=(ymix0, pat[0][2]))
    u_l0, hm_l0, f_l0 = ffn_fwd(0, h2_l0)

    z2_l0, h1 = _norm_mod_fwd(z1_l0, n1w[1], pat[1][0], pat[1][1], name="norm1_l1", dims=dims,
                              res=(f_l0, pat[0][5]))
    zz = _lin(h1, w_hgin, name="hg_in", tm=1024, tn=tn_hg, o_dtype=BF16)
    (o_f, sts_f), (g_up1,) = _hg_fwd_dir(zz, lb2, d=0, name="hg_fwd_d0", dims=dims, xch=gather([ffn_w_up[1]]))
    w_up[1] = cols(g_up1)
    (o_b, sts_b), _ = _hg_fwd_dir(zz, lb2, d=1, name="hg_fwd_d1", dims=dims)
    gnw = hg_gnorm_w.reshape(1, HG_HEAD)
    (og,) = _rowk(lambda rv, pv, cv, il: ([_hg_readout(rv[0] + rv[1], rv[2], cv[0])], [], []), name="hg_readout",
                  n=n, tm=dims["tm_row"], nctx=nctx, rows=[(o_f, dm, 0, 0), (o_b, dm, 0, 0), (zz, dm, 4, 0)],
                  consts=[gnw], out_rows=[(dm, BF16, dm, 0)])
    ymix1 = _lin(og, w_hgout, name="hg_out", tm=1024, tn=dm, o_dtype=BF16)
    z1_l1, h2_l1 = _norm_mod_fwd(z2_l0, n2w[1], pat[1][3], pat[1][4], name="norm2_l1", dims=dims,
                                 res=(ymix1, pat[1][2]))
    u_l1, hm_l1, f_l1 = ffn_fwd(1, h2_l1)

    dz, df, dgate2_l1, loss_part, dfinal_w = _loss_bwd(z1_l1, f_l1, pat[1][5], tgt, final_norm_w.reshape(1, dm),
                                                        name="loss_bwd", dims=dims)

    def ffn_bwd(layer, df_, u, hm, h2, xch=None):
        dff = hm.shape[1]
        dhm = _lin(df_, w_dn[layer], name=f"ffn_down_bwd_in{layer}", trans_w=True, tm=1024, tn=dff // 2, o_dtype=BF16)
        dwd = _lin_w(hm, df_, name=f"ffn_down_bwd_w{layer}", ta=dff // 2, tn=dm, tkr=tkr)
        (dua, dug, dcwa, dcwg, dcba, dcbg), got = _convffn_bwd(u, dhm, cw[layer], cb[layer],
                                                               name=f"convffn_bwd{layer}", dims=dims, xch=xch)
        dh2 = _lin(dua, w_up[layer], name=f"ffn_up_bwd_in_a{layer}", trans_w=True, tm=tm, tn=dm, kblk=0)
        dh2 = _lin(dug, w_up[layer], name=f"ffn_up_bwd_in_g{layer}", trans_w=True, tm=tm, tn=dm, kblk=1, base=dh2,
                   o_dtype=BF16)
        dwu = jnp.concatenate([_lin_w(h2, dua, name=f"ffn_up_bwd_w_a{layer}", ta=dm, tn=tn_up, tkr=tkr),
                               _lin_w(h2, dug, name=f"ffn_up_bwd_w_g{layer}", ta=dm, tn=tn_up, tkr=tkr)], axis=1)
        dcw = shards(jnp.concatenate([dcwa, dcwg], axis=1))
        return dh2, shards(dwu), dwd, dcw, jnp.concatenate([dcba, dcbg], axis=1), got

    dh2, dwu_l1, dwd_l1, dcw_l1, dcb_l1, _ = ffn_bwd(1, df, u_l1, hm_l1, h2_l1)
    dz, dymix, dsh2_l1, dsc2_l1, dgate1_l1, dn2w_l1 = _norm_mod_bwd(dh2, z1_l1, dz, n2w[1], pat[1][4], name="norm2_bwd_l1",
                                                                    dims=dims, res=(ymix1, pat[1][2]))
    dog = _lin(dymix, w_hgout, name="hg_out_bwd_in", trans_w=True, tm=1024, tn=dm, o_dtype=BF16)
    dw_hgout = _lin_w(og, dymix, name="hg_out_bwd_w", ta=dm, tn=dm, tkr=tkr)

    def readout_bwd(rv, pv, cv, il):
        _, vjp = jax.vjp(_hg_readout, rv[0] + rv[1], rv[2], cv[0])
        do, dg, dw = vjp(rv[3])
        return [do, dg], [], [jnp.broadcast_to(dw, (SUBLANES, HG_HEAD)) * (1.0 / SUBLANES)]

    do, dg, dgnw = _rowk(readout_bwd, name="hg_readout_bwd", n=n, tm=dims["tm_row"], nctx=nctx,
                         rows=[(o_f, dm, 0, 0), (o_b, dm, 0, 0), (zz, dm, 4, 0), (dog, dm, 0, 0)], consts=[gnw],
                         out_rows=[(dm, BF16, dm, 0), (dm, BF16, dm, 0)], out_acc=[HG_HEAD])
    (dq0, dv0, dff, dl_f), (p_up1, p_dn1) = _hg_bwd_dir(
        zz, lb2, do, sts_f, None, d=0, name="hg_bwd_d0", dims=dims,
        xch=scatter([dwu_l1, dwd_l1.reshape(NDEV, -1, dm)]))
    (dq, dv, dfb, dl_b), (p_hgout,) = _hg_bwd_dir(
        zz, lb2, do, sts_b, (dq0, dv0), d=1, name="hg_bwd_d1", dims=dims,
        xch=scatter([dw_hgout.reshape(NDEV, -1, dm)]))
    pieces = (dq, dv, dff, dfb, dg)
    dh1 = _lin_cat(pieces, w_hgin, name="hg_in_bwd_in", tm=tm, o_dtype=BF16)
    dw_hgin = shards(jnp.concatenate([_lin_w(h1, piece, name=f"hg_in_bwd_w{p}", ta=dm, tn=dm, tkr=tkr)
                                      for p, piece in enumerate(pieces)], axis=1))
    dz, df0, dsh1_l1, dsc1_l1, dgate2_l0, dn1w_l1 = _norm_mod_bwd(dh1, z2_l0, dz, n1w[1], pat[1][1], name="norm1_bwd_l1",
                                                                  dims=dims, res=(f_l0, pat[0][5]))
    dh2, dwu_l0, dwd_l0, dcw_l0, dcb_l0, _ = ffn_bwd(0, df0, u_l0, hm_l0, h2_l0)
    dz, dymix, dsh2_l0, dsc2_l0, dgate1_l0, dn2w_l0 = _norm_mod_bwd(dh2, z1_l0, dz, n2w[0], pat[0][4], name="norm2_bwd_l0",
                                                                    dims=dims, res=(ymix0, pat[0][2]))
    dz2g = _lin(dymix, w_s5out, name="s5_out_bwd_in", trans_w=True, tm=1024, tn=dm, o_dtype=BF16)
    dw_s5out = _lin_w(z2g, dymix, name="s5_out_bwd_w", ta=dm, tn=dm, tkr=tkr)

    def gate_bwd(rv, pv, cv, il):
        sg = _sigmoid(rv[1])
        return [rv[2] * rv[0] * sg * (1.0 - sg), rv[2] * sg], [], []

    dt_glu, dzg_a = _rowk(gate_bwd, name="s5_gate_bwd", n=n, tm=dims["tm_row"], nctx=nctx,
                          rows=[(zg, dm, 0, 0), (t_glu, dm, 0, 0), (dz2g, dm, 0, 0)],
                          out_rows=[(dm, BF16, dm, 0), (dm, BF16, dm, 0)])
    dzg_b = _lin(dt_glu, w_glu, name="s5_glu_bwd_in", trans_w=True, tm=1024, tn=dm, o_dtype=BF16)
    dw_glu = _lin_w(zg, dt_glu, name="s5_glu_bwd_w", ta=dm, tn=dm, tkr=tkr)

    def gelu_bwd(rv, pv, cv, il):
        _, vjp = jax.vjp(_gelu, rv[0] + rv[1])
        return [vjp(rv[2] + rv[3])[0]], [], []

    (dy_s5,) = _rowk(gelu_bwd, name="s5_gelu_bwd", n=n, tm=dims["tm_row"], nctx=nctx,
                     rows=[(y_s5a, dm, 0, 0), (y_s5b, dm, 0, 0), (dzg_a, dm, 0, 0), (dzg_b, dm, 0, 0)],
                     out_rows=[(dm, F32, dm, 0)])
    dcw_both = jnp.stack([dcw_l0, dcw_l1], axis=1)
    s5g, (p_up0, p_dn0, p_cw, p_s5out, p_glu, p_hgin) = _s5_bwd(
        u_s5, dy_s5, *s5p["af"], *s5p["ab"], s5p["bb"], s5p["cc"], dsk, st_r, st_i, name="s5_bwd", dims=dims,
        xch=scatter([dwu_l0, dwd_l0.reshape(NDEV, -1, dm), dcw_both, dw_s5out.reshape(NDEV, -1, dm),
                     dw_glu.reshape(NDEV, -1, dm), dw_hgin]))
    (du_s5,) = _rowk(lambda rv, pv, cv, il: ([rv[0] + rv[1]], [], []), name="s5_du", n=n, tm=dims["tm_row"], nctx=nctx,
                     rows=[(s5g[0], dm, 0, 0), (s5g[1], dm, 0, 0)], out_rows=[(dm, BF16, dm, 0)])
    ddsk = s5g[14]
    dh0 = _lin(du_s5, w_s5in, name="s5_in_bwd_in", trans_w=True, tm=1024, tn=dm, o_dtype=BF16)
    dw_s5in = _lin_w(h0, du_s5, name="s5_in_bwd_w", ta=dm, tn=dm, tkr=tkr)
    dz0, dsh1_l0, dsc1_l0, dn1w_l0 = _norm_mod_bwd(dh0, z0, dz, n1w[0], pat[0][1], name="norm1_bwd_l0", dims=dims)

    dlr, dli, dls, g_b_re, g_b_im, g_c_re, g_c_im = _s5_param_grads(s5p, s5g[2:6], s5g[6:10], s5g[10:14])

    dmod = jnp.stack([
        jnp.concatenate([dsh1_l0, dsc1_l0, dgate1_l0, dsh2_l0, dsc2_l0, dgate2_l0], axis=1),
        jnp.concatenate([dsh1_l1, dsc1_l1, dgate1_l1, dsh2_l1, dsc2_l1, dgate2_l1], axis=1)])
    dl_hg = jnp.stack([dl_f[:, 0], dl_b[:, 0]])
    wide = lambda g: g.reshape(-1, dm)
    small = [("norm1_w", jnp.concatenate([dn1w_l0, dn1w_l1])),
             ("norm2_w", jnp.concatenate([dn2w_l0, dn2w_l1])), ("final_norm_w", dfinal_w),
             ("s5_lam_re", dlr.reshape(-1, nstate)), ("s5_lam_im", dli.reshape(-1, nstate)),
             ("s5_log_step", dls.reshape(2, ngrp)),
             ("s5_b_re", wide(g_b_re.astype(BF16))), ("s5_b_im", wide(g_b_im.astype(BF16))),
             ("s5_c_re", wide(g_c_re.astype(BF16))), ("s5_c_im", wide(g_c_im.astype(BF16))), ("s5_d", ddsk),
             ("hg_gnorm_w", dgnw), ("ffn_conv_b", jnp.stack([dcb_l0.reshape(-1), dcb_l1.reshape(-1)]))]
    tail = _exchange([dmod, dw_s5in.reshape(NDEV, -1, dm)] + [g for _, g in small] + [wide(dl_hg), loss_part],
                     a2a=[False, True] + [False] * (len(small) + 2), name="gather_tail")
    dmod_g, p_s5in, gathered = tail[0], tail[1], tail[2:]
    dlat_full = jnp.transpose(dmod_g[:, :, :bsz], (1, 0, 2, 3)).reshape(2, NDEV * bsz, N_MOD * dm)
    dctx_full = dmod_g[:, :, bsz]
    dlat_sh = lax.dynamic_slice(dlat_full, (0, 0, me * nsm), (2, NDEV * bsz, nsm))
    dctx_sh = lax.dynamic_slice(dctx_full, (0, 0, me * nsm), (NDEV, 2, nsm))
    g_w_mod, g_b_mod, dcctx8 = _mod_bwd(craw, w_mod, dlat_sh, dctx_sh, dlat_full, dctx_full)

    (g_cctx,) = _exchange([wide(dcctx8[:1])], a2a=False, name="gather_cctx")
    small, gathered = [("c_ctx", wide(dcctx8[:1]))] + small, [g_cctx] + gathered
    res = {}
    for (k, g), parts in zip(small, gathered):
        w2, m2, v2 = (given[p + k].reshape(g.shape) for p in ("", "m_", "v_"))
        res[k] = tuple(o.reshape(given[k].shape) for o in _adamw(parts, w2, m2, v2, name="adamw_" + k))

    def total(parts, name):
        z = jnp.zeros(parts.shape[1:], F32)
        return _adamw(parts, z, z, z, name=name)[0]

    loss = jnp.sum(total(gathered[-1], "sum_loss"))
    dl_tot = total(gathered[-2], "sum_dlb").reshape(dl_hg.shape)
    nlb = hg_lower_bounds.shape[2]
    g_lb = lax.dynamic_slice(dl_tot, (0, 0, me * nlb), (2, 2, nlb))

    def adam_local(name, g, shape2):
        w, m, v = given[name], given["m_" + name], given["v_" + name]
        out = _adamw(g.reshape((1,) + shape2), w.reshape(shape2), m.reshape(shape2), v.reshape(shape2),
                     name="adamw_" + name)
        return tuple(o.reshape(w.shape) for o in out)

    def adam_parts(name, p):
        w, m, v = given[name], given["m_" + name], given["v_" + name]
        shape2 = (p.shape[0], -1, w.shape[-1])
        p3 = p.reshape(shape2)
        s2 = p3.shape[1:]
        out = _adamw(p3, w.reshape(s2), m.reshape(s2), v.reshape(s2), name="adamw_" + name)
        return tuple(o.reshape(w.shape) for o in out)

    res["hg_lower_bounds"] = adam_local("hg_lower_bounds", g_lb, (2 * 2, nlb))
    res["w_mod"] = adam_local("w_mod", g_w_mod, (2 * dm, nsm))
    res["b_mod"] = adam_local("b_mod", g_b_mod, (2, N_MOD * dm))
    res["s5_w_in"] = adam_parts("s5_w_in", p_s5in)
    res["s5_w_glu"] = adam_parts("s5_w_glu", p_glu)
    res["s5_w_out"] = adam_parts("s5_w_out", p_s5out)
    res["hg_w_in"] = adam_parts("hg_w_in", p_hgin)
    res["hg_w_out"] = adam_parts("hg_w_out", p_hgout)
    res["ffn_w_up"] = adam_parts("ffn_w_up", jnp.stack([p_up0, p_up1], axis=1))
    res["ffn_w_down"] = adam_parts("ffn_w_down", jnp.stack([p_dn0, p_dn1], axis=1))
    res["ffn_conv_w"] = adam_parts("ffn_conv_w", p_cw)

    grad_x = jnp.transpose(dz0[nctx:].reshape(lx, bsz, dm), (1, 0, 2))
    order = ["c_ctx", "w_mod", "b_mod", "norm1_w", "norm2_w", "final_norm_w", "s5_w_in", "s5_lam_re", "s5_lam_im",
             "s5_log_step", "s5_b_re", "s5_b_im", "s5_c_re", "s5_c_im", "s5_d", "s5_w_glu", "s5_w_out", "hg_w_in",
             "hg_lower_bounds", "hg_gnorm_w", "hg_w_out", "ffn_w_up", "ffn_conv_w", "ffn_conv_b", "ffn_w_down"]
    outs = [loss, grad_x]
    for j in range(4):
        outs += [res[k][j].reshape(given[k].shape) for k in order]
    return tuple(outs)
```

```python
import functools

import jax
import jax.numpy as jnp
from jax import lax
from jax.experimental import pallas as pl
from jax.experimental.pallas import tpu as pltpu

F32 = jnp.float32
BF16 = jnp.bfloat16
NDEV = 8
LOCAL_B = 4
NORM_EPS = 1e-6
N_MOD = 6
S5_GROUP = 16
S5_STATE = 64
S5_LAM_RE_MAX = -1e-4
HG_HEAD = 128
HG_ROWS = 128
GRID_W = 64
ADAM_LR, ADAM_B1, ADAM_B2, ADAM_EPS, ADAM_WD, ADAM_STEP = 0.001, 0.9, 0.999, 1e-08, 0.01, 10
VMEM_BYTES_V7X = 64 * 1024 * 1024
LANES = 128
SUBLANES = 8

NN = (((1,), (0,)), ((), ()))
NT = (((1,), (1,)), ((), ()))
TN = (((0,), (0,)), ((), ()))
MESH = pl.DeviceIdType.MESH


def _params(sem=None, vmem=None):
    kw = {}
    if sem is not None:
        kw["dimension_semantics"] = sem
    if vmem is not None:
        kw["vmem_limit_bytes"] = int(min(vmem, VMEM_BYTES_V7X - (4 << 20)))
    return pltpu.CompilerParams(**kw)


def _nbytes(shape, dtype):
    n = 1
    for s in shape:
        n *= 1 if s is None else s
    return n * jnp.dtype(dtype).itemsize


def _dot(a, b, dims=NN, precision=None):
    return lax.dot_general(a, b, dims, preferred_element_type=F32, precision=precision)


def _sigmoid(x):
    return 1.0 / (1.0 + jnp.exp(-x))


class _Xchg:
    def __init__(self, arrs, a2a):
        self.arrs, self.a2a, self.n = list(arrs), list(a2a), len(arrs)

    def out_shape(self):
        return [jax.ShapeDtypeStruct(a.shape if f else (NDEV,) + a.shape, a.dtype) for a, f in zip(self.arrs, self.a2a)]

    def scratch(self):
        return [pltpu.SemaphoreType.DMA((self.n * (NDEV - 1),)), pltpu.SemaphoreType.DMA((self.n * (NDEV - 1),)),
                pltpu.SemaphoreType.DMA((self.n,))]

    def _copies(self, ins, outs, sems, with_recvs):
        send_sems, recv_sems, loc_sems = sems
        x, y, c = lax.axis_index("x"), lax.axis_index("y"), lax.axis_index("c")
        me = 4 * x + 2 * y + c
        local, sends, recvs = [], [], []
        for a in range(self.n):
            src = ins[a].at[me] if self.a2a[a] else ins[a]
            local.append(pltpu.make_async_copy(src, outs[a].at[me], loc_sems.at[a]))
            for k in range(1, NDEV):
                px = (1 - x) if (k >> 2) & 1 else x
                py = (1 - y) if (k >> 1) & 1 else y
                pc = (1 - c) if k & 1 else c
                p = 4 * px + 2 * py + pc
                s = a * (NDEV - 1) + k - 1
                src = ins[a].at[p] if self.a2a[a] else ins[a]
                kw = dict(src_ref=src, send_sem=send_sems.at[s], recv_sem=recv_sems.at[s], device_id=(px, py, pc),
                          device_id_type=MESH)
                sends.append(pltpu.make_async_remote_copy(dst_ref=outs[a].at[me], **kw))
                if with_recvs:
                    recvs.append(pltpu.make_async_remote_copy(dst_ref=outs[a].at[p], **kw))
        return local, sends, recvs

    def start(self, ins, outs, sems):
        local, sends, _ = self._copies(ins, outs, sems, False)
        for cp in local + sends:
            cp.start()

    def wait(self, ins, outs, sems):
        local, sends, recvs = self._copies(ins, outs, sems, True)
        for cp in sends:
            cp.wait_send()
        for cp in recvs:
            cp.wait_recv()
        for cp in local:
            cp.wait()


def _exchange(arrs, *, a2a, name):
    xch = _Xchg(arrs, a2a if isinstance(a2a, (list, tuple)) else [a2a] * len(arrs))
    n = xch.n

    def body(*refs):
        xch.start(refs[:n], refs[n:2 * n], refs[2 * n:])
        xch.wait(refs[:n], refs[n:2 * n], refs[2 * n:])

    res = pl.pallas_call(
        body, name=name, out_shape=xch.out_shape(),
        in_specs=[pl.BlockSpec(memory_space=pl.ANY)] * n, out_specs=[pl.BlockSpec(memory_space=pl.ANY)] * n,
        scratch_shapes=xch.scratch(),
    )(*arrs)
    return list(res)


def _call(body, *, name, out_shape, grid, in_specs, out_specs, scratch, params, args, xch=None):
    in_specs, out_specs, out_shape, scratch, args = list(in_specs), list(out_specs), list(out_shape), list(scratch), list(args)
    n_in, n_out, n_scr = len(in_specs), len(out_shape), len(scratch)
    if xch is not None:
        k = xch.n
        inner = body

        def body(*refs):
            ins, xin = refs[:n_in], refs[n_in:n_in + k]
            outs, xout = refs[n_in + k:n_in + k + n_out], refs[n_in + k + n_out:n_in + 2 * k + n_out]
            scr = refs[n_in + 2 * k + n_out:n_in + 2 * k + n_out + n_scr]
            sems = refs[n_in + 2 * k + n_out + n_scr:]
            first = pl.program_id(0) == 0
            last = pl.program_id(0) == grid[0] - 1
            for ax in range(1, len(grid)):
                first = jnp.logical_and(first, pl.program_id(ax) == 0)
                last = jnp.logical_and(last, pl.program_id(ax) == grid[ax] - 1)

            @pl.when(first)
            def _():
                xch.start(xin, xout, sems)

            inner(*ins, *outs, *scr)

            @pl.when(last)
            def _():
                xch.wait(xin, xout, sems)

        anyspec = pl.BlockSpec(memory_space=pl.ANY)
        in_specs += [anyspec] * k
        out_specs += [anyspec] * k
        out_shape += xch.out_shape()
        scratch += xch.scratch()
        args += xch.arrs
    res = pl.pallas_call(body, name=name, out_shape=out_shape, grid=grid, in_specs=in_specs, out_specs=out_specs,
                         scratch_shapes=scratch, compiler_params=params)(*args)
    return list(res[:n_out]), list(res[n_out:])


def _mm(a, b, *, name, grid, a_spec, b_spec, o_spec, o_shape, o_dtype, dims, base=None):
    nk = grid[2]
    o_block = tuple(s for s in o_spec.block_shape if s is not None)

    def body(a_ref, b_ref, *rest):
        base_ref = rest[0] if base is not None else None
        o_ref, scr = rest[1 if base is not None else 0], rest[2 if base is not None else 1:]
        r = _dot(a_ref[...].astype(BF16), b_ref[...].astype(BF16), dims)
        if nk == 1:
            if base is not None:
                r = r + base_ref[...].astype(F32)
            o_ref[...] = r.astype(o_dtype)
        else:
            acc = scr[0]
            k = pl.program_id(2)

            @pl.when(k == 0)
            def _():
                acc[...] = r

            @pl.when(k > 0)
            def _():
                acc[...] += r

            @pl.when(k == nk - 1)
            def _():
                tot = acc[...] if base is None else acc[...] + base_ref[...].astype(F32)
                o_ref[...] = tot.astype(o_dtype)

    blocks = (_nbytes(a_spec.block_shape, a.dtype) + _nbytes(b_spec.block_shape, b.dtype) + _nbytes(o_block, o_dtype)
              + (_nbytes(o_block, base.dtype) if base is not None else 0))
    scratch = [pltpu.VMEM(o_block, F32)] if nk > 1 else []
    vmem = 2 * blocks + 3 * _nbytes(o_block, F32) + (8 << 20)
    return pl.pallas_call(
        body, name=name, out_shape=jax.ShapeDtypeStruct(o_shape, o_dtype), grid=grid,
        in_specs=[a_spec, b_spec] + ([o_spec] if base is not None else []), out_specs=o_spec, scratch_shapes=scratch,
        compiler_params=_params(("parallel", "parallel", "arbitrary"), vmem),
    )(a, b, *([base] if base is not None else []))


def _lin(a, w, *, name, trans_w=False, tm, tn, o_dtype=F32, kblk=0, base=None):
    m, kk = a.shape
    nout = w.shape[0] if trans_w else w.shape[1]
    if trans_w:
        b_spec = pl.BlockSpec((tn, kk), lambda j, i, k: (j, kblk))
    else:
        b_spec = pl.BlockSpec((kk, tn), lambda j, i, k: (0, j))
    return _mm(a, w, name=name, grid=(nout // tn, m // tm, 1), dims=NT if trans_w else NN, o_shape=(m, nout),
               o_dtype=o_dtype, o_spec=pl.BlockSpec((tm, tn), lambda j, i, k: (i, j)),
               a_spec=pl.BlockSpec((tm, kk), lambda j, i, k: (i, 0)), b_spec=b_spec, base=base)


def _lin_cat(pieces, w, *, name, tm, o_dtype=F32):
    m, kk = pieces[0].shape
    nout, npc = w.shape[0], len(pieces)

    def body(*refs):
        w_ref, o_ref = refs[npc], refs[npc + 1]
        acc = _dot(refs[0][...], w_ref[:, 0:kk], NT)
        for p in range(1, npc):
            acc = acc + _dot(refs[p][...], w_ref[:, p * kk:(p + 1) * kk], NT)
        o_ref[...] = acc.astype(o_dtype)

    row = pl.BlockSpec((tm, kk), lambda i: (i, 0))
    vmem = 2 * (npc * _nbytes((tm, kk), pieces[0].dtype) + _nbytes(w.shape, w.dtype)) + 4 * _nbytes((tm, nout), F32) + (8 << 20)
    return pl.pallas_call(
        body, name=name, out_shape=jax.ShapeDtypeStruct((m, nout), o_dtype), grid=(m // tm,),
        in_specs=[row] * npc + [pl.BlockSpec(w.shape, lambda i: (0, 0))], out_specs=pl.BlockSpec((tm, nout), lambda i: (i, 0)),
        compiler_params=_params(("parallel",), vmem),
    )(*pieces, w)


def _lin_w(a, dy, *, name, ta, tn, tkr):
    m, ka = a.shape
    nout = dy.shape[1]
    return _mm(a, dy, name=name, grid=(ka // ta, nout // tn, m // tkr), dims=TN, o_shape=(ka, nout), o_dtype=BF16,
               o_spec=pl.BlockSpec((ta, tn), lambda i, j, k: (i, j)),
               a_spec=pl.BlockSpec((tkr, ta), lambda i, j, k: (k, i)),
               b_spec=pl.BlockSpec((tkr, tn), lambda i, j, k: (k, j)))


def _rowk(fn, *, name, n, tm, nctx, rows=(), pats=(), consts=(), out_rows=(), out_seg=(), out_acc=()):
    nb, ncb = n // tm, nctx // tm
    nr, npat, ncst = len(rows), len(pats), len(consts)
    no, nseg, nacc = len(out_rows), len(out_seg), len(out_acc)
    in_specs, blocks = [], 0
    for arr, w, cb, off in rows:
        in_specs.append(pl.BlockSpec((tm, w), lambda i, cb=cb, off=off: (jnp.maximum(i - off, 0), cb)))
        blocks += _nbytes((tm, w), arr.dtype)
    for p in pats:
        in_specs.append(pl.BlockSpec((None, SUBLANES, p.shape[2]), lambda i: (jnp.where(i >= ncb, 1, 0), 0, 0)))
    for cst in consts:
        in_specs.append(pl.BlockSpec(cst.shape, lambda i: (0, 0)))
    out_shape, out_specs = [], []
    for wt, dt, w, cb in out_rows:
        out_shape.append(jax.ShapeDtypeStruct((n, wt), dt))
        out_specs.append(pl.BlockSpec((tm, w), lambda i, cb=cb: (i, cb)))
        blocks += _nbytes((tm, w), dt)
    for w in out_seg:
        out_shape.append(jax.ShapeDtypeStruct((SUBLANES, w), F32))
        out_specs.append(pl.BlockSpec((SUBLANES, w), lambda i: (0, 0)))
    for w in out_acc:
        out_shape.append(jax.ShapeDtypeStruct((1, w), F32))
        out_specs.append(pl.BlockSpec((1, w), lambda i: (0, 0)))
    scratch = [pltpu.VMEM((2, SUBLANES, w), F32) for w in out_seg] + [pltpu.VMEM((SUBLANES, w), F32) for w in out_acc]

    def body(*refs):
        r_in = refs[:nr]
        p_in = refs[nr:nr + npat]
        c_in = refs[nr + npat:nr + npat + ncst]
        base = nr + npat + ncst
        o_rows = refs[base:base + no]
        o_seg = refs[base + no:base + no + nseg]
        o_acc = refs[base + no + nseg:base + no + nseg + nacc]
        s_seg = refs[base + no + nseg + nacc:base + no + nseg + nacc + nseg]
        s_acc = refs[base + no + nseg + nacc + nseg:]
        i = pl.program_id(0)
        rv = [r[...].astype(F32).reshape(tm // SUBLANES, SUBLANES, r.shape[1]) for r in r_in]
        pv = [p[...] for p in p_in]
        cv = [c[...] for c in c_in]
        is_lat = (i >= ncb).astype(F32)
        ro, so, ao = fn(rv, pv, cv, is_lat)
        for ref, val in zip(o_rows, ro):
            ref[...] = val.reshape(tm, ref.shape[1]).astype(ref.dtype)
        if nseg or nacc:
            @pl.when(i == 0)
            def _():
                for s in list(s_seg) + list(s_acc):
                    s[...] = jnp.zeros(s.shape, F32)

            seg = jnp.where(i >= ncb, 1, 0)
            for s, val in zip(s_seg, so):
                s[seg] = s[seg] + val
            for s, val in zip(s_acc, ao):
                s[...] = s[...] + val

            @pl.when(i == nb - 1)
            def _():
                for o, s in zip(o_seg, s_seg):
                    lat, ctx = s[1], s[0]
                    row = lax.broadcasted_iota(jnp.int32, lat.shape, 0)
                    lat = lat + pltpu.roll(lat, 4, 0)
                    ctx = jnp.broadcast_to(jnp.sum(ctx, axis=0, keepdims=True), lat.shape)
                    o[...] = jnp.where(row < 4, lat, jnp.where(row == 4, ctx, 0.0))
                for o, s in zip(o_acc, s_acc):
                    o[...] = jnp.sum(s[...], axis=0, keepdims=True)

    vmem = 2 * blocks + 8 * tm * 1024 * 4 + (8 << 20)
    res = pl.pallas_call(
        body, name=name, out_shape=out_shape, grid=(nb,), in_specs=in_specs, out_specs=out_specs,
        scratch_shapes=scratch, compiler_params=_params(("arbitrary",), vmem),
    )(*[r[0] for r in rows], *pats, *consts)
    return list(res)


def _rms(z):
    return lax.rsqrt(jnp.mean(z * z, axis=-1, keepdims=True) + NORM_EPS)


def _norm_mod_fwd(z, w, sh, sc, *, name, dims, res=None):
    n, d = z.shape

    def fn(rv, pv, cv, is_lat):
        zz = rv[0]
        if res is not None:
            zz = zz + pv[2][None] * rv[1]
        h = (zz * _rms(zz) * cv[0]) * (1.0 + pv[1][None]) + pv[0][None]
        return ([zz, h] if res is not None else [h]), [], []

    rows = [(z, d, 0, 0)] + ([(res[0], d, 0, 0)] if res is not None else [])
    pats = [sh, sc] + ([res[1]] if res is not None else [])
    outs = ([(d, F32, d, 0)] if res is not None else []) + [(d, BF16, d, 0)]
    out = _rowk(fn, name=name, n=n, tm=dims["tm_row"], nctx=dims["nctx"], rows=rows, pats=pats, consts=[w],
                out_rows=outs)
    return (out[0], out[1]) if res is not None else (None, out[0])


def _norm_core_bwd(zin, dh, w, sc):
    r = _rms(zin)
    xh = zin * r
    dsh = jnp.sum(dh, axis=0)
    dsc = jnp.sum(dh * (xh * w), axis=0)
    dyv = dh * (1.0 + sc[None])
    dw = jnp.sum(dyv * xh, axis=0)
    dxh = dyv * w
    dx = r * (dxh - xh * jnp.mean(dxh * xh, axis=-1, keepdims=True))
    return dx, dsh, dsc, dw


def _norm_mod_bwd(dh, zin, dz_up, w, sc, *, name, dims, res=None):
    n, d = zin.shape

    def fn(rv, pv, cv, is_lat):
        dx, dsh, dsc, dw = _norm_core_bwd(rv[1], rv[0], cv[0], pv[0])
        dz = rv[2] + dx
        if res is None:
            return [dz], [dsh, dsc], [dw]
        return [dz, dz * pv[1][None]], [dsh, dsc, jnp.sum(dz * rv[3], axis=0)], [dw]

    rows = [(dh, d, 0, 0), (zin, d, 0, 0), (dz_up, d, 0, 0)] + ([(res[0], d, 0, 0)] if res is not None else [])
    pats = [sc] + ([res[1]] if res is not None else [])
    outs = [(d, F32, d, 0)] + ([(d, BF16, d, 0)] if res is not None else [])
    return _rowk(fn, name=name, n=n, tm=dims["tm_row"], nctx=dims["nctx"], rows=rows, pats=pats, consts=[w],
                 out_rows=outs, out_seg=[d] * (3 if res is not None else 2), out_acc=[d])


def _loss_bwd(z1, f, gate, tgt, w, *, name, dims):
    n, d = z1.shape

    def fn(rv, pv, cv, is_lat):
        z2 = rv[0] + pv[0][None] * rv[1]
        r = _rms(z2)
        xh = z2 * r
        err = (xh * cv[0] - rv[2]) * is_lat
        dout = err * (1.0 / d)
        dxh = dout * cv[0]
        dz = r * (dxh - xh * jnp.mean(dxh * xh, axis=-1, keepdims=True))
        return ([dz, dz * pv[0][None]], [jnp.sum(dz * rv[1], axis=0)],
                [jnp.sum(0.5 * err * err * (1.0 / d), axis=0), jnp.sum(dout * xh, axis=0)])

    tm = dims["tm_row"]
    rows = [(z1, d, 0, 0), (f, d, 0, 0), (tgt, d, 0, dims["nctx"] // tm)]
    return _rowk(fn, name=name, n=n, tm=tm, nctx=dims["nctx"], rows=rows, pats=[gate], consts=[w],
                 out_rows=[(d, F32, d, 0), (d, BF16, d, 0)], out_seg=[d], out_acc=[d, d])


def _gelu(y):
    return jax.nn.gelu(y, approximate=True)


def _conv_masks(tb, i):
    tok = lax.broadcasted_iota(jnp.int32, (tb, 1), 0) >> 2
    last = jnp.where(i == 0, tb // LOCAL_B - 1, GRID_W - 1)
    wpos = tok & last
    return wpos == 0, wpos == last


CONV_LANES = 2 * LANES


def _conv_taps(u_ref, cw_ref, cb_ref, no_left, no_right, tb):
    uu = u_ref[...].astype(F32)
    ul = jnp.where(no_left, 0.0, pltpu.roll(uu, LOCAL_B, 0))
    ur = jnp.where(no_right, 0.0, pltpu.roll(uu, tb - LOCAL_B, 0))
    val = cb_ref[...] + ul * cw_ref[pl.ds(0, 1), :] + uu * cw_ref[pl.ds(1, 1), :] + ur * cw_ref[pl.ds(2, 1), :]
    return val, ul, uu, ur


def _convffn_specs(tb, nj):
    cl = CONV_LANES
    return [pl.BlockSpec((tb, cl), lambda j, i: (i, j)), pl.BlockSpec((tb, cl), lambda j, i: (i, nj + j)),
            pl.BlockSpec((3, cl), lambda j, i: (0, j)), pl.BlockSpec((3, cl), lambda j, i: (0, nj + j)),
            pl.BlockSpec((1, cl), lambda j, i: (0, j)), pl.BlockSpec((1, cl), lambda j, i: (0, nj + j))]


def _convffn_fwd(u, cw, cb, *, name, dims):
    n, f2 = u.shape
    tb, nj = dims["nctx"], f2 // 2 // CONV_LANES

    def body(ua_ref, ug_ref, cwa_ref, cwg_ref, cba_ref, cbg_ref, o_ref):
        no_left, no_right = _conv_masks(tb, pl.program_id(1))
        a = _conv_taps(ua_ref, cwa_ref, cba_ref, no_left, no_right, tb)[0]
        g = _conv_taps(ug_ref, cwg_ref, cbg_ref, no_left, no_right, tb)[0]
        o_ref[...] = (a * _sigmoid(a) * g).astype(BF16)

    vmem = 16 * tb * CONV_LANES * 4 + (8 << 20)
    return pl.pallas_call(
        body, name=name, out_shape=jax.ShapeDtypeStruct((n, f2 // 2), BF16), grid=(nj, n // tb),
        in_specs=_convffn_specs(tb, nj), out_specs=pl.BlockSpec((tb, CONV_LANES), lambda j, i: (i, j)),
        compiler_params=_params(("parallel", "arbitrary"), vmem),
    )(u, u, cw, cw, cb, cb)


def _convffn_bwd(u, dhm, cw, cb, *, name, dims, xch=None):
    n, f2 = u.shape
    tb, nj = dims["nctx"], f2 // 2 // CONV_LANES

    def body(ua_ref, ug_ref, cwa_ref, cwg_ref, cba_ref, cbg_ref, dh_ref, dua_ref, dug_ref, dcwa_ref, dcwg_ref, dcba_ref,
             dcbg_ref):
        i = pl.program_id(1)
        no_left, no_right = _conv_masks(tb, i)

        @pl.when(i == 0)
        def _():
            for ref in (dcwa_ref, dcwg_ref, dcba_ref, dcbg_ref):
                ref[...] = jnp.zeros(ref.shape, F32)

        a, al, ac, ar = _conv_taps(ua_ref, cwa_ref, cba_ref, no_left, no_right, tb)
        g, gl, gc, gr = _conv_taps(ug_ref, cwg_ref, cbg_ref, no_left, no_right, tb)
        dh = dh_ref[...].astype(F32)
        sa = _sigmoid(a)
        dg = dh * (a * sa)
        da = dh * g * (sa * (1.0 + a * (1.0 - sa)))
        for dc, (tl, tc, tr), cw_ref, du_ref, dcw_ref, dcb_ref in (
                (da, (al, ac, ar), cwa_ref, dua_ref, dcwa_ref, dcba_ref),
                (dg, (gl, gc, gr), cwg_ref, dug_ref, dcwg_ref, dcbg_ref)):
            dcb_ref[...] += jnp.sum(dc, axis=0, keepdims=True)
            dcw_ref[pl.ds(0, 1), :] += jnp.sum(dc * tl, axis=0, keepdims=True)
            dcw_ref[pl.ds(1, 1), :] += jnp.sum(dc * tc, axis=0, keepdims=True)
            dcw_ref[pl.ds(2, 1), :] += jnp.sum(dc * tr, axis=0, keepdims=True)
            du = (dc * cw_ref[pl.ds(1, 1), :]
                  + pltpu.roll(jnp.where(no_left, 0.0, dc) * cw_ref[pl.ds(0, 1), :], tb - LOCAL_B, 0)
                  + pltpu.roll(jnp.where(no_right, 0.0, dc) * cw_ref[pl.ds(2, 1), :], LOCAL_B, 0))
            du_ref[...] = du.astype(BF16)

    cl, f = CONV_LANES, f2 // 2
    sd = jax.ShapeDtypeStruct
    row = pl.BlockSpec((tb, cl), lambda j, i: (i, j))
    vmem = 24 * tb * cl * 4 + (8 << 20)
    return _call(
        body, name=name, xch=xch, args=[u, u, cw, cw, cb, cb, dhm], scratch=[],
        out_shape=[sd((n, f), BF16), sd((n, f), BF16), sd((3, f), F32), sd((3, f), F32), sd((1, f), F32), sd((1, f), F32)],
        grid=(nj, n // tb), in_specs=_convffn_specs(tb, nj) + [row],
        out_specs=[row, row, pl.BlockSpec((3, cl), lambda j, i: (0, j)), pl.BlockSpec((3, cl), lambda j, i: (0, j)),
                   pl.BlockSpec((1, cl), lambda j, i: (0, j)), pl.BlockSpec((1, cl), lambda j, i: (0, j))],
        params=_params(("arbitrary", "arbitrary"), vmem))


def _s5_disc(lr, li, ls, brt, bit):
    lr = jnp.minimum(lr, S5_LAM_RE_MAX)
    dt = jnp.exp(ls)
    mag = jnp.exp(lr * dt)
    ar = mag * jnp.cos(li * dt)
    ai = mag * jnp.sin(li * dt)
    den = lr * lr + li * li
    nr = ar - 1.0
    cr = (nr * lr + ai * li) / den
    ci = (ai * lr - nr * li) / den
    return ar, ai, cr * brt - ci * bit, cr * bit + ci * brt


def _s5_disc_fwd(lr, li, ls, brt, bit):
    def body(lr_ref, li_ref, ls_ref, br_ref, bi_ref, ar_ref, ai_ref, bbr_ref, bbi_ref):
        ar, ai, bbr, bbi = _s5_disc(lr_ref[...], li_ref[...], ls_ref[...], br_ref[...], bi_ref[...])
        ar_ref[...] = ar
        ai_ref[...] = ai
        bbr_ref[...] = bbr
        bbi_ref[...] = bbi

    sd = jax.ShapeDtypeStruct
    return pl.pallas_call(body, name="s5_disc_fwd",
                          out_shape=[sd(lr.shape, F32), sd(lr.shape, F32), sd(brt.shape, F32), sd(brt.shape, F32)],
                          compiler_params=_params(None, 32 << 20))(lr, li, ls, brt, bit)


def _s5_disc_bwd(lr, li, ls, brt, bit, dar, dai, dbbr, dbbi):
    def body(lr_ref, li_ref, ls_ref, br_ref, bi_ref, dar_ref, dai_ref, dbbr_ref, dbbi_ref,
             dlr_ref, dli_ref, dls_ref, dbr_ref, dbi_ref):
        _, vjp = jax.vjp(_s5_disc, lr_ref[...], li_ref[...], ls_ref[...], br_ref[...], bi_ref[...])
        dlr, dli, dls, dbr, dbi = vjp((dar_ref[...], dai_ref[...], dbbr_ref[...], dbbi_ref[...]))
        dlr_ref[...] = dlr
        dli_ref[...] = dli
        dls_ref[...] = dls
        dbr_ref[...] = dbr
        dbi_ref[...] = dbi

    sd = jax.ShapeDtypeStruct
    return pl.pallas_call(body, name="s5_disc_bwd",
                          out_shape=[sd(lr.shape, F32), sd(lr.shape, F32), sd(ls.shape, F32), sd(brt.shape, F32),
                                     sd(brt.shape, F32)],
                          compiler_params=_params(None, 48 << 20))(lr, li, ls, brt, bit, dar, dai, dbbr, dbbi)


def _cmul(ar, ai, xr, xi):
    return ar * xr - ai * xi, ar * xi + ai * xr


def _s5_chunk_of(step, ncc, nc, rev):
    if not rev:
        return step
    return jnp.where(step < ncc, ncc - 1 - step, nc - 1 - (step - ncc))


def _s5_scan2(asc, desc, row0, nrows, a_r_ref, a_i_ref, cr_ref, ci_ref, *, lane_block, extra=None):
    width = asc[0].shape[1]
    nt = nrows // SUBLANES
    for lb in range(width // lane_block):
        lanes = pl.ds(lb * lane_block, lane_block)
        a1r, a1i = a_r_ref[:, lanes], a_i_ref[:, lanes]
        a2r, a2i = pltpu.roll(a1r, 4, 0), pltpu.roll(a1i, 4, 0)
        lo = lax.broadcasted_iota(jnp.int32, a1r.shape, 0) < 4

        def step(t, carry):
            pr, pi = carry[0], carry[1]
            ra = pl.ds(pl.multiple_of(row0 + t * SUBLANES, SUBLANES), SUBLANES)
            rd = pl.ds(pl.multiple_of(row0 + (nt - 1 - t) * SUBLANES, SUBLANES), SUBLANES)
            ur, ui = asc[0][ra, lanes], asc[1][ra, lanes]
            dr, di = desc[0][rd, lanes], desc[1][rd, lanes]
            mr, mi = _cmul(a1r, a1i, pr, pi)
            y1r, y1i = jnp.where(lo, ur, dr) + mr, jnp.where(lo, ui, di) + mi
            mr, mi = _cmul(a2r, a2i, pltpu.roll(y1r, 4, 0), pltpu.roll(y1i, 4, 0))
            y2r, y2i = jnp.where(lo, dr, ur) + mr, jnp.where(lo, di, ui) + mi
            our, oui = jnp.where(lo, y1r, y2r), jnp.where(lo, y1i, y2i)
            odr, odi = jnp.where(lo, y2r, y1r), jnp.where(lo, y2i, y1i)
            asc[0][ra, lanes] = our
            asc[1][ra, lanes] = oui
            desc[0][rd, lanes] = odr
            desc[1][rd, lanes] = odi
            nxt = (pltpu.roll(y2r, 4, 0), pltpu.roll(y2i, 4, 0))
            if extra is None:
                return nxt
            return nxt + tuple(extra(t, nt - 1 - t, lanes, (our, oui), (odr, odi), carry[2:]))

        init = (cr_ref[:, lanes], ci_ref[:, lanes])
        if extra is not None:
            init = init + tuple(extra.init(lanes))
        out = lax.fori_loop(0, nt, step, init)
        cr_ref[:, lanes] = out[0]
        ci_ref[:, lanes] = out[1]
        if extra is not None:
            extra.done(lanes, out[2:])


S5_SPLIT = 2


def _s5_fwd(u, af_r, af_i, bb, cc, dsk, *, name, dims, xch=None):
    n, dm = u.shape
    nk, swk = bb[0].shape[0], bb[0].shape[2]
    rr, sw = dims["s5_rows"], nk * swk
    nkh, dmh, swh = nk // S5_SPLIT, dm // S5_SPLIT, sw // S5_SPLIT
    nc, ncc = n // rr, dims["nctx"] // rr
    c1 = lambda i: _s5_chunk_of(i, ncc, nc, True)

    def body(u0_ref, u1_ref, afr_ref, afi_ref, b0r, b0i, b1r, b1i, c0r, c0i, c1r, c1i, dsk_ref,
             y0_ref, y1_ref, str_ref, sti_ref, s0r, s0i, s1r, s1i, cr, ci):
        @pl.when(pl.program_id(1) == 0)
        def _():
            cr[...] = jnp.zeros(cr.shape, F32)
            ci[...] = jnp.zeros(ci.shape, F32)

        str_ref[...] = cr[...]
        sti_ref[...] = ci[...]
        ub0, ub1 = u0_ref[...].astype(BF16), u1_ref[...].astype(BF16)
        for k in range(nkh):
            cols, sl = slice(k * LANES, (k + 1) * LANES), slice(k * swk, (k + 1) * swk)
            s0r[:, sl] = _dot(ub0[:, cols], b0r[k])
            s0i[:, sl] = _dot(ub0[:, cols], b0i[k])
            s1r[:, sl] = _dot(ub1[:, cols], b1r[k])
            s1i[:, sl] = _dot(ub1[:, cols], b1i[k])
        _s5_scan2((s0r, s0i), (s1r, s1i), 0, rr, afr_ref, afi_ref, cr, ci, lane_block=dims["s5_lane_block"])
        for k in range(nkh):
            cols, sl = slice(k * LANES, (k + 1) * LANES), slice(k * swk, (k + 1) * swk)
            y0_ref[:, cols] = (_dot(s0r[:, sl].astype(BF16), c0r[k]) - _dot(s0i[:, sl].astype(BF16), c0i[k])
                               + dsk_ref[:, cols] * u0_ref[:, cols].astype(F32)).astype(BF16)
            y1_ref[:, cols] = (_dot(s1r[:, sl].astype(BF16), c1r[k])
                               - _dot(s1i[:, sl].astype(BF16), c1i[k])).astype(BF16)

    row0 = pl.BlockSpec((rr, dmh), lambda h, i: (i, h))
    row1 = pl.BlockSpec((rr, dmh), lambda h, i: (c1(i), h))
    tile = pl.BlockSpec((SUBLANES, swh), lambda h, i: (0, h))
    wspec = lambda a: pl.BlockSpec((nkh,) + a.shape[1:], lambda h, i: (h, 0, 0))
    st_spec = pl.BlockSpec((None, SUBLANES, swh), lambda h, i: (i, 0, h))
    sd = jax.ShapeDtypeStruct
    vmem = 4 * rr * swh * 4 + 12 * rr * dmh * 4 + 16 * nkh * LANES * swk * 2 + (12 << 20)
    return _call(
        body, name=name, xch=xch, args=[u, u, af_r, af_i, *bb, *cc, dsk],
        out_shape=[sd((n, dm), BF16), sd((n, dm), BF16), sd((nc, SUBLANES, sw), F32), sd((nc, SUBLANES, sw), F32)],
        grid=(S5_SPLIT, nc),
        in_specs=[row0, row1, tile, tile] + [wspec(a) for a in (*bb, *cc)] + [pl.BlockSpec((1, dmh), lambda h, i: (0, h))],
        out_specs=[row0, row1, st_spec, st_spec],
        scratch=[pltpu.VMEM((rr, swh), F32)] * 4 + [pltpu.VMEM((SUBLANES, swh), F32)] * 2,
        params=_params(("arbitrary", "arbitrary"), vmem))


class _DaHook2:
    def __init__(self, s0, s1, accs):
        self.s0, self.s1, self.accs = s0, s1, accs

    def init(self, lanes):
        return tuple(a[:, lanes] for a in self.accs)

    def done(self, lanes, acc):
        for a, v in zip(self.accs, acc):
            a[:, lanes] = v

    def __call__(self, t1, t0, lanes, l1, l0, acc):
        row = lax.broadcasted_iota(jnp.int32, l1[0].shape, 0)
        b1 = pl.multiple_of(SUBLANES + t1 * SUBLANES, SUBLANES)
        b0 = pl.multiple_of(SUBLANES + t0 * SUBLANES, SUBLANES)
        cur1, nxt1 = pl.ds(b1, SUBLANES), pl.ds(pl.multiple_of(b1 + SUBLANES, SUBLANES), SUBLANES)
        cur0, prv0 = pl.ds(b0, SUBLANES), pl.ds(pl.multiple_of(b0 - SUBLANES, SUBLANES), SUBLANES)
        p1r = pltpu.roll(jnp.where(row >= 4, self.s1[0][cur1, lanes], self.s1[0][nxt1, lanes]), 4, 0)
        p1i = pltpu.roll(jnp.where(row >= 4, self.s1[1][cur1, lanes], self.s1[1][nxt1, lanes]), 4, 0)
        p0r = pltpu.roll(jnp.where(row >= 4, self.s0[0][prv0, lanes], self.s0[0][cur0, lanes]), 4, 0)
        p0i = pltpu.roll(jnp.where(row >= 4, self.s0[1][prv0, lanes], self.s0[1][cur0, lanes]), 4, 0)
        return (acc[0] + p0r * l0[0] + p0i * l0[1], acc[1] + p0r * l0[1] - p0i * l0[0],
                acc[2] + p1r * l1[0] + p1i * l1[1], acc[3] + p1r * l1[1] - p1i * l1[0])


def _s5_bwd(u, dy, af_r, af_i, ab_r, ab_i, bb, cc, dsk, st_r, st_i, *, name, dims, xch=None):
    n, dm = u.shape
    nk, swk = bb[0].shape[0], bb[0].shape[2]
    rr, sw = dims["s5_rows"], nk * swk
    nkh, dmh, swh = nk // S5_SPLIT, dm // S5_SPLIT, sw // S5_SPLIT
    nc, ncc = n // rr, dims["nctx"] // rr
    f0 = lambda i: nc - 1 - i
    f1 = lambda i: _s5_chunk_of(nc - 1 - i, ncc, nc, True)

    def body(u0_ref, u1_ref, dy0_ref, dy1_ref, afr_ref, afi_ref, abr_ref, abi_ref, b0r, b0i, b1r, b1i, c0r, c0i, c1r, c1i,
             dsk_ref, str_ref, sti_ref,
             du0_ref, du1_ref, db0r, db0i, db1r, db1i, dc0r, dc0i, dc1r, dc1i, da0r_ref, da0i_ref, da1r_ref, da1i_ref, dd_ref,
             s0r, s0i, s1r, s1i, l0r, l0i, l1r, l1i, cr, ci, lcr, lci, a0r, a0i, a1r, a1i, dda):
        i = pl.program_id(1)

        @pl.when(i == 0)
        def _():
            for ref in (lcr, lci, a0r, a0i, a1r, a1i, dda, db0r, db0i, db1r, db1i, dc0r, dc0i, dc1r, dc1i):
                ref[...] = jnp.zeros(ref.shape, F32)

        row = lax.broadcasted_iota(jnp.int32, (SUBLANES, swh), 0)
        for st_ref, car, z0, z1 in ((str_ref, cr, s0r, s1r), (sti_ref, ci, s0i, s1i)):
            st = st_ref[...]
            car[...] = st
            z0[pl.ds(0, SUBLANES), :] = jnp.where(row < 4, st, pltpu.roll(st, 4, 0))
            z1[pl.ds(rr + SUBLANES, SUBLANES), :] = jnp.where(row >= 4, st, pltpu.roll(st, 4, 0))
        body_rows = pl.ds(SUBLANES, rr)
        ub0, ub1 = u0_ref[...].astype(BF16), u1_ref[...].astype(BF16)
        dyb0, dyb1 = dy0_ref[...].astype(BF16), dy1_ref[...].astype(BF16)
        for k in range(nkh):
            cols, sl = slice(k * LANES, (k + 1) * LANES), slice(k * swk, (k + 1) * swk)
            s0r[body_rows, sl] = _dot(ub0[:, cols], b0r[k])
            s0i[body_rows, sl] = _dot(ub0[:, cols], b0i[k])
            s1r[body_rows, sl] = _dot(ub1[:, cols], b1r[k])
            s1i[body_rows, sl] = _dot(ub1[:, cols], b1i[k])
        _s5_scan2((s0r, s0i), (s1r, s1i), SUBLANES, rr, afr_ref, afi_ref, cr, ci, lane_block=dims["s5_lane_block"])
        for k in range(nkh):
            cols, sl = slice(k * LANES, (k + 1) * LANES), slice(k * swk, (k + 1) * swk)
            for dyk, lr, li, sr, si, ccr, cci, dcr, dci in ((dyb0[:, cols], l0r, l0i, s0r, s0i, c0r, c0i, dc0r, dc0i),
                                                           (dyb1[:, cols], l1r, l1i, s1r, s1i, c1r, c1i, dc1r, dc1i)):
                lr[:, sl] = _dot(dyk, ccr[k], NT)
                li[:, sl] = -_dot(dyk, cci[k], NT)
                dcr[k] += _dot(dyk, sr[body_rows, sl].astype(BF16), TN)
                dci[k] -= _dot(dyk, si[body_rows, sl].astype(BF16), TN)
        _s5_scan2((l1r, l1i), (l0r, l0i), 0, rr, abr_ref, abi_ref, lcr, lci, lane_block=dims["s5_lane_block"] // 2,
                  extra=_DaHook2((s0r, s0i), (s1r, s1i), (a0r, a0i, a1r, a1i)))
        for k in range(nkh):
            cols, sl = slice(k * LANES, (k + 1) * LANES), slice(k * swk, (k + 1) * swk)
            for uk, lr, li, br, bi, dbr, dbi, du_ref, first in ((ub0[:, cols], l0r, l0i, b0r, b0i, db0r, db0i, du0_ref, True),
                                                              (ub1[:, cols], l1r, l1i, b1r, b1i, db1r, db1i, du1_ref, False)):
                lrk, lik = lr[:, sl].astype(BF16), li[:, sl].astype(BF16)
                dbr[k] += _dot(uk, lrk, TN)
                dbi[k] += _dot(uk, lik, TN)
                duk = _dot(lrk, br[k], NT) + _dot(lik, bi[k], NT)
                if first:
                    duk = duk + dsk_ref[:, cols] * dy0_ref[:, cols]
                du_ref[:, cols] = duk
        dda[...] += jnp.sum((dy0_ref[...] * u0_ref[...].astype(F32)).reshape(rr // SUBLANES, SUBLANES, dmh), axis=0)

        @pl.when(i == nc - 1)
        def _():
            for o, a in ((da0r_ref, a0r), (da0i_ref, a0i), (da1r_ref, a1r), (da1i_ref, a1i), (dd_ref, dda)):
                o[...] = jnp.sum(a[...], axis=0, keepdims=True)

    row0 = pl.BlockSpec((rr, dmh), lambda h, i: (f0(i), h))
    row1 = pl.BlockSpec((rr, dmh), lambda h, i: (f1(i), h))
    tile = pl.BlockSpec((SUBLANES, swh), lambda h, i: (0, h))
    wspec = lambda a: pl.BlockSpec((nkh,) + a.shape[1:], lambda h, i: (h, 0, 0))
    st_spec = pl.BlockSpec((None, SUBLANES, swh), lambda h, i: (f0(i), 0, h))
    vec = lambda w: pl.BlockSpec((1, w), lambda h, i: (0, h))
    sd = jax.ShapeDtypeStruct
    out_shape = ([sd((n, dm), F32)] * 2 + [sd(a.shape, F32) for a in (*bb, *bb)] + [sd((1, sw), F32)] * 4 + [sd((1, dm), F32)])
    out_specs = [row0, row1] + [wspec(a) for a in (*bb, *bb)] + [vec(swh)] * 4 + [vec(dmh)]
    scratch = ([pltpu.VMEM((rr + 2 * SUBLANES, swh), F32)] * 4 + [pltpu.VMEM((rr, swh), F32)] * 4
               + [pltpu.VMEM((SUBLANES, swh), F32)] * 8 + [pltpu.VMEM((SUBLANES, dmh), F32)])
    vmem = 8 * (rr + 16) * swh * 4 + 16 * rr * dmh * 4 + 48 * nkh * LANES * swk * 4 + (10 << 20)
    return _call(
        body, name=name, xch=xch, args=[u, u, dy, dy, af_r, af_i, ab_r, ab_i, *bb, *cc, dsk, st_r, st_i],
        out_shape=out_shape, grid=(S5_SPLIT, nc),
        in_specs=[row0, row1, row0, row1, tile, tile, tile, tile] + [wspec(a) for a in (*bb, *cc)] + [vec(dmh), st_spec, st_spec],
        out_specs=out_specs, scratch=scratch, params=_params(("arbitrary", "arbitrary"), vmem))


def _hg_mask(kind, rev):
    if kind == "tot":
        r = lax.broadcasted_iota(jnp.int32, (SUBLANES, HG_ROWS), 0)
        c = lax.broadcasted_iota(jnp.int32, (SUBLANES, HG_ROWS), 1)
        return (c & 3) == r
    r = lax.broadcasted_iota(jnp.int32, (HG_ROWS, HG_ROWS), 0)
    c = lax.broadcasted_iota(jnp.int32, (HG_ROWS, HG_ROWS), 1)
    same = (r & 3) == (c & 3)
    before = ((c >> 2) >= (r >> 2)) if rev else ((c >> 2) <= (r >> 2))
    return jnp.logical_and(same, before if kind == "cum" else jnp.logical_not(before))


def _split2(x):
    hi = x.astype(BF16)
    return hi, (x - hi.astype(F32)).astype(BF16)


@functools.partial(jax.custom_vjp, nondiff_argnums=(1, 2))
def _mask_sum(x, kind, rev):
    m = _hg_mask(kind, rev).astype(BF16)
    hi, lo = _split2(x)
    return _dot(m, hi) + _dot(m, lo)


def _mask_sum_fwd(x, kind, rev):
    return _mask_sum(x, kind, rev), None


def _mask_sum_bwd(kind, rev, _, g):
    m = _hg_mask(kind, rev).astype(BF16)
    hi, lo = _split2(g)
    return (_dot(m, hi, TN) + _dot(m, lo, TN),)


_mask_sum.defvjp(_mask_sum_fwd, _mask_sum_bwd)


def _hg_chunk(q, v, fraw, l0, l1, st, *, rev):
    nh = q.shape[1] // HG_HEAD
    lb = _sigmoid(l1 - l0)
    logf = jnp.logaddexp(jnp.log(lb), jnp.log1p(-lb) + jax.nn.log_sigmoid(fraw))
    kk = (1.0 - lb) * _sigmoid(fraw * -1.0)
    tri = _hg_mask("cum", rev)
    bcum = _mask_sum(logf, "cum", rev)
    brem = _mask_sum(logf, "rem", rev)
    bend8 = _mask_sum(logf, "tot", rev)
    r8d = lax.broadcasted_iota(jnp.int32, bend8.shape, 0)
    decs = [jnp.exp(jnp.sum(jnp.where(r8d == b, bend8, 0.0), axis=0, keepdims=True)) for b in range(LOCAL_B)]
    qd = (q * jnp.exp(bcum)).astype(BF16)
    kd = (kk * jnp.exp(-bcum)).astype(BF16)
    ke = (kk * jnp.exp(brem)).astype(BF16)
    wide = (HG_ROWS, LOCAL_B * HG_HEAD)
    mine = (lax.broadcasted_iota(jnp.int32, wide, 1) >> 7) == (lax.broadcasted_iota(jnp.int32, wide, 0) & 3)
    per_example = lambda x: jnp.where(mine, jnp.concatenate([x] * LOCAL_B, axis=1), jnp.zeros(wide, x.dtype))
    outs, new = [], []
    for h in range(nh):
        sl = slice(h * HG_HEAD, (h + 1) * HG_HEAD)
        vh = v[:, sl].astype(BF16)
        att = jnp.where(tri, _dot(qd[:, sl], kd[:, sl], NT), 0.0)
        outs.append(_dot(att.astype(BF16), vh) + _dot(per_example(qd[:, sl]), st[h].astype(BF16), NT))
        dec = jnp.concatenate([d[:, sl] for d in decs], axis=1)
        new.append(st[h] * dec + _dot(vh, per_example(ke[:, sl]), TN))
    return jnp.concatenate(outs, axis=1), tuple(new)


def _hg_chunk_of(step, ncc, nc, rev):
    return _s5_chunk_of(step, ncc, nc, rev)


def _hg_fwd_dir(zz, lb2, *, d, name, dims, xch=None):
    n = zz.shape[0]
    dm = zz.shape[1] // 5
    ns, sw = dm // HG_HEAD, LOCAL_B * HG_HEAD
    nc, ncc = n // HG_ROWS, dims["nctx"] // HG_ROWS
    rev = d == 1
    ch = lambda i: _hg_chunk_of(i, ncc, nc, rev)

    def body(q_ref, v_ref, f_ref, l0_ref, l1_ref, o_ref, st_ref, st):
        @pl.when(pl.program_id(0) == 0)
        def _():
            st[...] = jnp.zeros(st.shape, F32)

        st_ref[...] = st[...]
        o, new = _hg_chunk(q_ref[...].astype(F32), v_ref[...].astype(F32), f_ref[...].astype(F32), l0_ref[...],
                           l1_ref[...], tuple(st[j] for j in range(ns)), rev=rev)
        o_ref[...] = o.astype(BF16)
        for j in range(ns):
            st[j] = new[j]

    blk = lambda off: pl.BlockSpec((HG_ROWS, dm), lambda i, off=off: (ch(i), off))
    lspec = lambda layer: pl.BlockSpec((None, None, 1, dm), lambda i, layer=layer: (d, layer, 0, 0))
    return _call(
        body, name=name, xch=xch, args=[zz, zz, zz, lb2, lb2],
        out_shape=[jax.ShapeDtypeStruct((n, dm), BF16), jax.ShapeDtypeStruct((nc, ns, HG_HEAD, sw), F32)],
        grid=(nc,),
        in_specs=[blk(0), blk(1), blk(2 + d), lspec(0), lspec(1)],
        out_specs=[pl.BlockSpec((HG_ROWS, dm), lambda i: (ch(i), 0)),
                   pl.BlockSpec((None, ns, HG_HEAD, sw), lambda i: (ch(i), 0, 0, 0))],
        scratch=[pltpu.VMEM((ns, HG_HEAD, sw), F32)],
        params=_params(("arbitrary",), 48 << 20))


def _hg_bwd_dir(zz, lb2, do, sts, dqv_prev, *, d, name, dims, xch=None):
    n = zz.shape[0]
    dm = zz.shape[1] // 5
    ns, sw = dm // HG_HEAD, LOCAL_B * HG_HEAD
    nc, ncc = n // HG_ROWS, dims["nctx"] // HG_ROWS
    rev = d == 1
    ch = lambda i: _hg_chunk_of(nc - 1 - i, ncc, nc, rev)
    qv_dtype = F32 if d == 0 else BF16

    def body(*refs):
        q_ref, v_ref, f_ref, l0_ref, l1_ref, do_ref, st_ref = refs[:7]
        pos = 7
        if d == 1:
            dqp_ref, dvp_ref = refs[7:9]
            pos = 9
        dq_ref, dv_ref, df_ref, dl_ref, dst = refs[pos:]
        i = pl.program_id(0)

        @pl.when(i == 0)
        def _():
            dst[...] = jnp.zeros(dst.shape, F32)
            dl_ref[...] = jnp.zeros(dl_ref.shape, F32)

        _, vjp = jax.vjp(functools.partial(_hg_chunk, rev=rev), q_ref[...].astype(F32), v_ref[...].astype(F32),
                         f_ref[...].astype(F32), l0_ref[...], l1_ref[...], tuple(st_ref[j] for j in range(ns)))
        dq, dv, df, dl0, dl1, dstn = vjp((do_ref[...].astype(F32), tuple(dst[j] for j in range(ns))))
        for j in range(ns):
            dst[j] = dstn[j]
        if d == 1:
            dq = dq + dqp_ref[...]
            dv = dv + dvp_ref[...]
        dq_ref[...] = dq.astype(qv_dtype)
        dv_ref[...] = dv.astype(qv_dtype)
        df_ref[...] = df.astype(BF16)
        dl_ref[0] += dl0
        dl_ref[1] += dl1

    blk = lambda off: pl.BlockSpec((HG_ROWS, dm), lambda i, off=off: (ch(i), off))
    oblk = pl.BlockSpec((HG_ROWS, dm), lambda i: (ch(i), 0))
    lspec = lambda layer: pl.BlockSpec((None, None, 1, dm), lambda i, layer=layer: (d, layer, 0, 0))
    ins = [zz, zz, zz, lb2, lb2, do, sts] + (list(dqv_prev) if d == 1 else [])
    in_specs = [blk(0), blk(1), blk(2 + d), lspec(0), lspec(1), oblk,
                pl.BlockSpec((None, ns, HG_HEAD, sw), lambda i: (ch(i), 0, 0, 0))]
    in_specs += [oblk, oblk] if d == 1 else []
    sd = jax.ShapeDtypeStruct
    return _call(
        body, name=name, xch=xch, args=ins,
        out_shape=[sd((n, dm), qv_dtype), sd((n, dm), qv_dtype), sd((n, dm), BF16), sd((2, 1, dm), F32)],
        grid=(nc,), in_specs=in_specs,
        out_specs=[oblk, oblk, oblk, pl.BlockSpec((2, 1, dm), lambda i: (0, 0, 0))],
        scratch=[pltpu.VMEM((ns, HG_HEAD, sw), F32)],
        params=_params(("arbitrary",), 56 << 20))


def _hg_readout(o, g, w):
    outs = []
    for h in range(o.shape[-1] // HG_HEAD):
        sl = slice(h * HG_HEAD, (h + 1) * HG_HEAD)
        oh = o[..., sl]
        outs.append(oh * _rms(oh) * w * _sigmoid(g[..., sl]))
    return jnp.concatenate(outs, axis=-1)


def _silu(x):
    return x * _sigmoid(x)


def _mod_fwd(craw, w, b):
    def body(c_ref, w_ref, b_ref, o_ref):
        s = _silu(c_ref[...]).astype(BF16)
        for layer in range(w.shape[0]):
            o_ref[layer] = _dot(s, w_ref[layer].astype(BF16)) + b_ref[layer]

    return pl.pallas_call(body, name="mod_fwd",
                          out_shape=jax.ShapeDtypeStruct((w.shape[0], craw.shape[0], w.shape[2]), F32),
                          compiler_params=_params(None, 40 << 20))(craw, w, b)


def _mod_bwd(craw, w, dlat_sh, dctx_sh, dlat_full, dctx_full):
    nl, dm, ns = w.shape
    nb = dlat_sh.shape[1]

    def body(c_ref, w_ref, dl_ref, dc_ref, dlf_ref, dcf_ref, dw_ref, db_ref, dcc_ref):
        craw_v = c_ref[...]
        s = _silu(craw_v)
        s_lat = s[:nb].astype(BF16)
        s_ctx = s[nb:].astype(BF16)
        row = lax.broadcasted_iota(jnp.int32, (SUBLANES, ns), 0)
        dsc = jnp.zeros((SUBLANES, dm), F32)
        for layer in range(nl):
            tot = dc_ref[0, pl.ds(layer, 1), :]
            totf = dcf_ref[0, pl.ds(layer, 1), :]
            for i in range(1, NDEV):
                tot = tot + dc_ref[i, pl.ds(layer, 1), :]
                totf = totf + dcf_ref[i, pl.ds(layer, 1), :]
            dc8 = jnp.where(row == 0, jnp.broadcast_to(tot, (SUBLANES, ns)), 0.0).astype(BF16)
            dw_ref[layer] = _dot(s_lat, dl_ref[layer].astype(BF16), TN) + _dot(s_ctx, dc8, TN)
            db_ref[layer] = jnp.sum(dlf_ref[layer], axis=0, keepdims=True) + totf
            dsc = dsc + _dot(dc8, w_ref[layer].astype(BF16), NT)
        cc = craw_v[nb:]
        sg = _sigmoid(cc)
        dcc_ref[...] = dsc * (sg * (1.0 + cc * (1.0 - sg)))

    sd = jax.ShapeDtypeStruct
    return pl.pallas_call(body, name="mod_bwd",
                          out_shape=[sd((nl, dm, ns), F32), sd((nl, 1, dlat_full.shape[2]), F32), sd((SUBLANES, dm), F32)],
                          compiler_params=_params(None, 48 << 20))(craw, w, dlat_sh, dctx_sh, dlat_full, dctx_full)


def _adam_rows(r):
    best = None
    for t in range(2 * SUBLANES, min(r, 128) + 1, 2 * SUBLANES):
        if r % t == 0:
            best = t
    return best if best is not None else r


def _adamw(parts, w, m, v, *, name):
    npart, r, c = parts.shape
    tr = _adam_rows(r)

    def body(p_ref, w_ref, m_ref, v_ref, g_ref, d_ref, nm_ref, nv_ref):
        g = p_ref[0].astype(F32)
        for i in range(1, npart):
            g = g + p_ref[i].astype(F32)
        nm = ADAM_B1 * m_ref[...] + (1.0 - ADAM_B1) * g
        nv = ADAM_B2 * v_ref[...] + (1.0 - ADAM_B2) * (g * g)
        m_hat = nm / (1.0 - ADAM_B1 ** ADAM_STEP)
        v_hat = nv / (1.0 - ADAM_B2 ** ADAM_STEP)
        g_ref[...] = g
        d_ref[...] = -ADAM_LR * (m_hat / (jnp.sqrt(v_hat) + ADAM_EPS) + ADAM_WD * w_ref[...])
        nm_ref[...] = nm
        nv_ref[...] = nv

    spec = pl.BlockSpec((tr, c), lambda i: (i, 0))
    vmem = 2 * (npart + 7) * tr * c * 4 + (8 << 20)
    return pl.pallas_call(
        body, name=name, out_shape=[jax.ShapeDtypeStruct((r, c), F32)] * 4, grid=(r // tr,),
        in_specs=[pl.BlockSpec((npart, tr, c), lambda i: (0, i, 0)), spec, spec, spec], out_specs=[spec] * 4,
        compiler_params=_params(("parallel",), vmem),
    )(parts, w, m, v)


def _to_tm(a):
    return jnp.transpose(a, (1, 0, 2)).reshape(a.shape[1] * a.shape[0], a.shape[2])


def _pattern(mod_lat, mod_ctx, m, dm):
    lat = mod_lat[:, m * dm:(m + 1) * dm]
    ctx = jnp.broadcast_to(mod_ctx[None, m * dm:(m + 1) * dm], (SUBLANES, dm))
    return jnp.stack([ctx, jnp.concatenate([lat, lat], axis=0)])


def _blockdiag_b(bt, nk):
    g, h, p = bt.shape
    t = bt.reshape(nk, 8, h, p)
    return jnp.einsum("kghp,gj->kghjp", t, jnp.eye(8, dtype=bt.dtype)).reshape(nk, 8 * h, 8 * p)


def _blockdiag_c(ct, nk):
    g, h, p = ct.shape
    t = ct.reshape(nk, 8, h, p)
    return jnp.einsum("kghp,gj->kgpjh", t, jnp.eye(8, dtype=ct.dtype)).reshape(nk, 8 * p, 8 * h)


def _diag_b(dbb, h, p):
    nk = dbb.shape[0]
    return jnp.einsum("kghgp->kghp", dbb.reshape(nk, 8, h, 8, p)).reshape(nk * 8, h, p)


def _s5_prep(lam_re, lam_im, log_step, b_re, b_im, c_re, c_im, dm):
    ngrp, nk = dm // S5_GROUP, dm // LANES
    sw = ngrp * S5_STATE
    lr4 = lam_re.reshape(2, ngrp, 1, S5_STATE)
    li4 = lam_im.reshape(2, ngrp, 1, S5_STATE)
    ls4 = log_step.reshape(2, ngrp, 1, 1)
    brt = jnp.transpose(b_re, (0, 1, 3, 2))
    bit = jnp.transpose(b_im, (0, 1, 3, 2))
    abar_r, abar_i, bbar_r, bbar_i = _s5_disc_fwd(lr4, li4, ls4, brt, bit)
    half = lambda a, d: jnp.broadcast_to(a[d].reshape(1, sw), (LOCAL_B, sw))
    tile = lambda a, first: jnp.concatenate([half(a, first), half(a, 1 - first)], axis=0)
    bb = tuple(_blockdiag_b(w[d], nk).astype(BF16) for d in range(2) for w in (bbar_r, bbar_i))
    cc = tuple(_blockdiag_c(w[d], nk).astype(BF16) for d in range(2) for w in (c_re, c_im))
    return dict(disc_in=(lr4, li4, ls4, brt, bit), af=(tile(abar_r, 0), tile(abar_i, 0)),
                ab=(tile(abar_r, 1), -tile(abar_i, 1)), bb=bb, cc=cc)


def _s5_param_grads(prep, dbb, dcc, da):
    lr4 = prep["disc_in"][0]
    dar = jnp.stack([da[0], da[2]]).reshape(lr4.shape)
    dai = jnp.stack([da[1], da[3]]).reshape(lr4.shape)
    dbbr = jnp.stack([_diag_b(dbb[0], S5_GROUP, S5_STATE), _diag_b(dbb[2], S5_GROUP, S5_STATE)])
    dbbi = jnp.stack([_diag_b(dbb[1], S5_GROUP, S5_STATE), _diag_b(dbb[3], S5_GROUP, S5_STATE)])
    dlr, dli, dls, dbrt, dbit = _s5_disc_bwd(*prep["disc_in"], dar, dai, dbbr, dbbi)
    g_c_re = jnp.stack([_diag_b(dcc[0], S5_GROUP, S5_STATE), _diag_b(dcc[2], S5_GROUP, S5_STATE)])
    g_c_im = jnp.stack([_diag_b(dcc[1], S5_GROUP, S5_STATE), _diag_b(dcc[3], S5_GROUP, S5_STATE)])
    return dlr, dli, dls, jnp.transpose(dbrt, (0, 1, 3, 2)), jnp.transpose(dbit, (0, 1, 3, 2)), g_c_re, g_c_im


def kernel(x, c, ctx, c_ctx, w_mod, b_mod, norm1_w, norm2_w, final_norm_w, s5_w_in, s5_lam_re, s5_lam_im, s5_log_step, s5_b_re, s5_b_im, s5_c_re, s5_c_im, s5_d, s5_w_glu, s5_w_out, hg_w_in, hg_lower_bounds, hg_gnorm_w, hg_w_out, ffn_w_up, ffn_conv_w, ffn_conv_b, ffn_w_down, loss_target, m_c_ctx, m_w_mod, m_b_mod, m_norm1_w, m_norm2_w, m_final_norm_w, m_s5_w_in, m_s5_lam_re, m_s5_lam_im, m_s5_log_step, m_s5_b_re, m_s5_b_im, m_s5_c_re, m_s5_c_im, m_s5_d, m_s5_w_glu, m_s5_w_out, m_hg_w_in, m_hg_lower_bounds, m_hg_gnorm_w, m_hg_w_out, m_ffn_w_up, m_ffn_conv_w, m_ffn_conv_b, m_ffn_w_down, v_c_ctx, v_w_mod, v_b_mod, v_norm1_w, v_norm2_w, v_final_norm_w, v_s5_w_in, v_s5_lam_re, v_s5_lam_im, v_s5_log_step, v_s5_b_re, v_s5_b_im, v_s5_c_re, v_s5_c_im, v_s5_d, v_s5_w_glu, v_s5_w_out, v_hg_w_in, v_hg_lower_bounds, v_hg_gnorm_w, v_hg_w_out, v_ffn_w_up, v_ffn_conv_w, v_ffn_conv_b, v_ffn_w_down):
    given = dict(locals())
    bsz, lx, dm = x.shape
    lc = ctx.shape[1]
    assert bsz == LOCAL_B and w_mod.shape[0] == 2 and dm % LANES == 0
    n, nctx = (lc + lx) * bsz, lc * bsz
    ngrp, nstate, hgrp = dm // S5_GROUP, S5_STATE, S5_GROUP
    nk = dm // LANES
    dims = dict(nctx=nctx, tm=min(512, nctx), tm_row=min(512, nctx), s5_rows=min(256, nctx),
                s5_lane_block=min(512, 8 * nstate))
    tm = dims["tm"]
    assert nctx % HG_ROWS == 0 and (lx * bsz) % nctx == 0 and lc % GRID_W == 0 and lc & (lc - 1) == 0
    me = 4 * lax.axis_index("x") + 2 * lax.axis_index("y") + lax.axis_index("c")

    gath = _exchange([given[k].astype(BF16) for k in ("s5_w_in", "s5_w_glu", "s5_w_out")]
                     + [c, hg_lower_bounds, ffn_conv_w], a2a=False, name="gather_weights")
    w_s5in, w_glu, w_s5out = (g.reshape(dm, dm) for g in gath[:3])
    c_all, lb_all, cw_all = gath[3:]
    ns_up = ffn_w_up.shape[2]
    w_up, w_dn = [None, None], [None, None]
    cols = lambda g: jnp.transpose(g, (1, 0, 2)).reshape(g.shape[1], -1)
    shards = lambda w: jnp.transpose(w.reshape(w.shape[0], NDEV, -1), (1, 0, 2))
    tn_up, tn_hg, tkr = 2 * ns_up, 2 * hg_w_in.shape[2], 1152
    assert n % tkr == 0 and n % 1024 == 0
    gather = lambda arrs: _Xchg([a.astype(BF16) for a in arrs], [False] * len(arrs))
    scatter = lambda arrs: _Xchg(arrs, [True] * len(arrs))
    cw = [cols(cw_all[:, layer]) for layer in range(2)]
    cb = [ffn_conv_b[layer].reshape(1, -1) for layer in range(2)]
    lb2 = jnp.transpose(lb_all, (1, 2, 0, 3)).reshape(2, 2, 1, dm)

    nsm = w_mod.shape[2]
    craw = jnp.concatenate([c_all.reshape(NDEV * bsz, dm), c_ctx[None], jnp.zeros((SUBLANES - 1, dm), F32)], axis=0)
    b_sh = lax.dynamic_slice(b_mod, (0, me * nsm), (2, nsm)).reshape(2, 1, nsm)
    mod_sh = _mod_fwd(craw, w_mod, b_sh)
    (mod_g,) = _exchange([mod_sh], a2a=False, name="gather_mod")
    mod_full = jnp.transpose(mod_g, (1, 2, 0, 3)).reshape(2, craw.shape[0], NDEV * nsm)
    pat = []
    for layer in range(2):
        mlat = lax.dynamic_slice(mod_full[layer], (me * bsz, 0), (bsz, N_MOD * dm))
        mctx = mod_full[layer, NDEV * bsz]
        pat.append([_pattern(mlat, mctx, m, dm) for m in range(N_MOD)])

    s5p = _s5_prep(s5_lam_re[0], s5_lam_im[0], s5_log_step[0], s5_b_re[0], s5_b_im[0], s5_c_re[0], s5_c_im[0], dm)
    dsk = s5_d.reshape(1, dm)

    z0 = jnp.concatenate([_to_tm(ctx), _to_tm(x)], axis=0)
    tgt = _to_tm(loss_target)
    n1w = [norm1_w[layer].reshape(1, dm) for layer in range(2)]
    n2w = [norm2_w[layer].reshape(1, dm) for layer in range(2)]

    def ffn_fwd(layer, h2):
        u = _lin(h2, w_up[layer], name=f"ffn_up{layer}", tm=1024, tn=tn_up, o_dtype=BF16)
        hm = _convffn_fwd(u, cw[layer], cb[layer], name=f"convffn_fwd{layer}", dims=dims)
        f = _lin(hm, w_dn[layer], name=f"ffn_down{layer}", tm=tm, tn=dm, o_dtype=BF16)
        return u, hm, f

    _, h0 = _norm_mod_fwd(z0, n1w[0], pat[0][0], pat[0][1], name="norm1_l0", dims=dims)
    u_s5 = _lin(h0, w_s5in, name="s5_in", tm=1024, tn=dm, o_dtype=BF16)
    (y_s5a, y_s5b, st_r, st_i), (g_up0, g_dn0, g_hgin, g_hgout, g_dn1) = _s5_fwd(
        u_s5, *s5p["af"], s5p["bb"], s5p["cc"], dsk, name="s5_fwd", dims=dims,
        xch=gather([ffn_w_up[0], ffn_w_down[0], hg_w_in[0], hg_w_out[0], ffn_w_down[1]]))
    w_up[0], w_dn[0] = cols(g_up0), g_dn0.reshape(-1, dm)
    w_hgin, w_hgout, w_dn[1] = cols(g_hgin), g_hgout.reshape(dm, dm), g_dn1.reshape(-1, dm)
    (zg,) = _rowk(lambda rv, pv, cv, il: ([_gelu(rv[0] + rv[1])], [], []), name="s5_gelu", n=n, tm=dims["tm_row"],
                  nctx=nctx, rows=[(y_s5a, dm, 0, 0), (y_s5b, dm, 0, 0)], out_rows=[(dm, BF16, dm, 0)])
    t_glu = _lin(zg, w_glu, name="s5_glu", tm=1024, tn=dm, o_dtype=BF16)
    (z2g,) = _rowk(lambda rv, pv, cv, il: ([rv[0] * _sigmoid(rv[1])], [], []), name="s5_gate", n=n,
                   tm=dims["tm_row"], nctx=nctx, rows=[(zg, dm, 0, 0), (t_glu, dm, 0, 0)],
                   out_rows=[(dm, BF16, dm, 0)])
    ymix0 = _lin(z2g, w_s5out, name="s5_out", tm=1024, tn=dm, o_dtype=BF16)
    z1_l0, h2_l0 = _norm_mod_fwd(z0, n2w[0], pat[0][3], pat[0][4], name="norm2_l0", dims=dims,
                                 res=(ymix0, pat[0][2]))
    u_l0, hm_l0, f_l0 = ffn_fwd(0, h2_l0)

    z2_l0, h1 = _norm_mod_fwd(z1_l0, n1w[1], pat[1][0], pat[1][1], name="norm1_l1", dims=dims,
                              res=(f_l0, pat[0][5]))
    zz = _lin(h1, w_hgin, name="hg_in", tm=1024, tn=tn_hg, o_dtype=BF16)
    (o_f, sts_f), (g_up1,) = _hg_fwd_dir(zz, lb2, d=0, name="hg_fwd_d0", dims=dims, xch=gather([ffn_w_up[1]]))
    w_up[1] = cols(g_up1)
    (o_b, sts_b), _ = _hg_fwd_dir(zz, lb2, d=1, name="hg_fwd_d1", dims=dims)
    gnw = hg_gnorm_w.reshape(1, HG_HEAD)
    (og,) = _rowk(lambda rv, pv, cv, il: ([_hg_readout(rv[0] + rv[1], rv[2], cv[0])], [], []), name="hg_readout",
                  n=n, tm=dims["tm_row"], nctx=nctx, rows=[(o_f, dm, 0, 0), (o_b, dm, 0, 0), (zz, dm, 4, 0)],
                  consts=[gnw], out_rows=[(dm, BF16, dm, 0)])
    ymix1 = _lin(og, w_hgout, name="hg_out", tm=1024, tn=dm, o_dtype=BF16)
    z1_l1, h2_l1 = _norm_mod_fwd(z2_l0, n2w[1], pat[1][3], pat[1][4], name="norm2_l1", dims=dims,
                                 res=(ymix1, pat[1][2]))
    u_l1, hm_l1, f_l1 = ffn_fwd(1, h2_l1)

    dz, df, dgate2_l1, loss_part, dfinal_w = _loss_bwd(z1_l1, f_l1, pat[1][5], tgt, final_norm_w.reshape(1, dm),
                                                        name="loss_bwd", dims=dims)

    def ffn_bwd(layer, df_, u, hm, h2, xch=None):
        dff = hm.shape[1]
        dhm = _lin(df_, w_dn[layer], name=f"ffn_down_bwd_in{layer}", trans_w=True, tm=1024, tn=dff // 2, o_dtype=BF16)
        dwd = _lin_w(hm, df_, name=f"ffn_down_bwd_w{layer}", ta=dff // 2, tn=dm, tkr=tkr)
        (dua, dug, dcwa, dcwg, dcba, dcbg), got = _convffn_bwd(u, dhm, cw[layer], cb[layer],
                                                               name=f"convffn_bwd{layer}", dims=dims, xch=xch)
        dh2 = _lin_cat((dua, dug), w_up[layer], name=f"ffn_up_bwd_in{layer}", tm=tm, o_dtype=BF16)
        dwu = jnp.concatenate([_lin_w(h2, dua, name=f"ffn_up_bwd_w_a{layer}", ta=dm, tn=tn_up, tkr=tkr),
                               _lin_w(h2, dug, name=f"ffn_up_bwd_w_g{layer}", ta=dm, tn=tn_up, tkr=tkr)], axis=1)
        dcw = shards(jnp.concatenate([dcwa, dcwg], axis=1))
        return dh2, shards(dwu), dwd, dcw, jnp.concatenate([dcba, dcbg], axis=1), got

    dh2, dwu_l1, dwd_l1, dcw_l1, dcb_l1, _ = ffn_bwd(1, df, u_l1, hm_l1, h2_l1)
    dz, dymix, dsh2_l1, dsc2_l1, dgate1_l1, dn2w_l1 = _norm_mod_bwd(dh2, z1_l1, dz, n2w[1], pat[1][4], name="norm2_bwd_l1",
                                                                    dims=dims, res=(ymix1, pat[1][2]))
    dog = _lin(dymix, w_hgout, name="hg_out_bwd_in", trans_w=True, tm=1024, tn=dm, o_dtype=BF16)
    dw_hgout = _lin_w(og, dymix, name="hg_out_bwd_w", ta=dm, tn=dm, tkr=tkr)

    def readout_bwd(rv, pv, cv, il):
        _, vjp = jax.vjp(_hg_readout, rv[0] + rv[1], rv[2], cv[0])
        do, dg, dw = vjp(rv[3])
        return [do, dg], [], [jnp.broadcast_to(dw, (SUBLANES, HG_HEAD)) * (1.0 / SUBLANES)]

    do, dg, dgnw = _rowk(readout_bwd, name="hg_readout_bwd", n=n, tm=dims["tm_row"], nctx=nctx,
                         rows=[(o_f, dm, 0, 0), (o_b, dm, 0, 0), (zz, dm, 4, 0), (dog, dm, 0, 0)], consts=[gnw],
                         out_rows=[(dm, BF16, dm, 0), (dm, BF16, dm, 0)], out_acc=[HG_HEAD])
    (dq0, dv0, dff, dl_f), (p_up1, p_dn1) = _hg_bwd_dir(
        zz, lb2, do, sts_f, None, d=0, name="hg_bwd_d0", dims=dims,
        xch=scatter([dwu_l1, dwd_l1.reshape(NDEV, -1, dm)]))
    (dq, dv, dfb, dl_b), (p_hgout,) = _hg_bwd_dir(
        zz, lb2, do, sts_b, (dq0, dv0), d=1, name="hg_bwd_d1", dims=dims,
        xch=scatter([dw_hgout.reshape(NDEV, -1, dm)]))
    pieces = (dq, dv, dff, dfb, dg)
    dh1 = _lin_cat(pieces, w_hgin, name="hg_in_bwd_in", tm=tm, o_dtype=BF16)
    dw_hgin = shards(jnp.concatenate([_lin_w(h1, piece, name=f"hg_in_bwd_w{p}", ta=dm, tn=dm, tkr=tkr)
                                      for p, piece in enumerate(pieces)], axis=1))
    dz, df0, dsh1_l1, dsc1_l1, dgate2_l0, dn1w_l1 = _norm_mod_bwd(dh1, z2_l0, dz, n1w[1], pat[1][1], name="norm1_bwd_l1",
                                                                  dims=dims, res=(f_l0, pat[0][5]))
    dh2, dwu_l0, dwd_l0, dcw_l0, dcb_l0, _ = ffn_bwd(0, df0, u_l0, hm_l0, h2_l0)
    dz, dymix, dsh2_l0, dsc2_l0, dgate1_l0, dn2w_l0 = _norm_mod_bwd(dh2, z1_l0, dz, n2w[0], pat[0][4], name="norm2_bwd_l0",
                                                                    dims=dims, res=(ymix0, pat[0][2]))
    dz2g = _lin(dymix, w_s5out, name="s5_out_bwd_in", trans_w=True, tm=1024, tn=dm, o_dtype=BF16)
    dw_s5out = _lin_w(z2g, dymix, name="s5_out_bwd_w", ta=dm, tn=dm, tkr=tkr)

    def gate_bwd(rv, pv, cv, il):
        sg = _sigmoid(rv[1])
        return [rv[2] * rv[0] * sg * (1.0 - sg), rv[2] * sg], [], []

    dt_glu, dzg_a = _rowk(gate_bwd, name="s5_gate_bwd", n=n, tm=dims["tm_row"], nctx=nctx,
                          rows=[(zg, dm, 0, 0), (t_glu, dm, 0, 0), (dz2g, dm, 0, 0)],
                          out_rows=[(dm, BF16, dm, 0), (dm, BF16, dm, 0)])
    dzg_b = _lin(dt_glu, w_glu, name="s5_glu_bwd_in", trans_w=True, tm=1024, tn=dm, o_dtype=BF16)
    dw_glu = _lin_w(zg, dt_glu, name="s5_glu_bwd_w", ta=dm, tn=dm, tkr=tkr)

    def gelu_bwd(rv, pv, cv, il):
        _, vjp = jax.vjp(_gelu, rv[0] + rv[1])
        return [vjp(rv[2] + rv[3])[0]], [], []

    (dy_s5,) = _rowk(gelu_bwd, name="s5_gelu_bwd", n=n, tm=dims["tm_row"], nctx=nctx,
                     rows=[(y_s5a, dm, 0, 0), (y_s5b, dm, 0, 0), (dzg_a, dm, 0, 0), (dzg_b, dm, 0, 0)],
                     out_rows=[(dm, F32, dm, 0)])
    dcw_both = jnp.stack([dcw_l0, dcw_l1], axis=1)
    s5g, (p_up0, p_dn0, p_cw, p_s5out, p_glu, p_hgin) = _s5_bwd(
        u_s5, dy_s5, *s5p["af"], *s5p["ab"], s5p["bb"], s5p["cc"], dsk, st_r, st_i, name="s5_bwd", dims=dims,
        xch=scatter([dwu_l0, dwd_l0.reshape(NDEV, -1, dm), dcw_both, dw_s5out.reshape(NDEV, -1, dm),
                     dw_glu.reshape(NDEV, -1, dm), dw_hgin]))
    (du_s5,) = _rowk(lambda rv, pv, cv, il: ([rv[0] + rv[1]], [], []), name="s5_du", n=n, tm=dims["tm_row"], nctx=nctx,
                     rows=[(s5g[0], dm, 0, 0), (s5g[1], dm, 0, 0)], out_rows=[(dm, BF16, dm, 0)])
    ddsk = s5g[14]
    dh0 = _lin(du_s5, w_s5in, name="s5_in_bwd_in", trans_w=True, tm=1024, tn=dm, o_dtype=BF16)
    dw_s5in = _lin_w(h0, du_s5, name="s5_in_bwd_w", ta=dm, tn=dm, tkr=tkr)
    dz0, dsh1_l0, dsc1_l0, dn1w_l0 = _norm_mod_bwd(dh0, z0, dz, n1w[0], pat[0][1], name="norm1_bwd_l0", dims=dims)

    dlr, dli, dls, g_b_re, g_b_im, g_c_re, g_c_im = _s5_param_grads(s5p, s5g[2:6], s5g[6:10], s5g[10:14])

    dmod = jnp.stack([
        jnp.concatenate([dsh1_l0, dsc1_l0, dgate1_l0, dsh2_l0, dsc2_l0, dgate2_l0], axis=1),
        jnp.concatenate([dsh1_l1, dsc1_l1, dgate1_l1, dsh2_l1, dsc2_l1, dgate2_l1], axis=1)])
    dl_hg = jnp.stack([dl_f[:, 0], dl_b[:, 0]])
    wide = lambda g: g.reshape(-1, dm)
    small = [("norm1_w", jnp.concatenate([dn1w_l0, dn1w_l1])),
             ("norm2_w", jnp.concatenate([dn2w_l0, dn2w_l1])), ("final_norm_w", dfinal_w),
             ("s5_lam_re", dlr.reshape(-1, nstate)), ("s5_lam_im", dli.reshape(-1, nstate)),
             ("s5_log_step", dls.reshape(2, ngrp)),
             ("s5_b_re", wide(g_b_re.astype(BF16))), ("s5_b_im", wide(g_b_im.astype(BF16))),
             ("s5_c_re", wide(g_c_re.astype(BF16))), ("s5_c_im", wide(g_c_im.astype(BF16))), ("s5_d", ddsk),
             ("hg_gnorm_w", dgnw), ("ffn_conv_b", jnp.stack([dcb_l0.reshape(-1), dcb_l1.reshape(-1)]))]
    tail = _exchange([dmod, dw_s5in.reshape(NDEV, -1, dm)] + [g for _, g in small] + [wide(dl_hg), loss_part],
                     a2a=[False, True] + [False] * (len(small) + 2), name="gather_tail")
    dmod_g, p_s5in, gathered = tail[0], tail[1], tail[2:]
    dlat_full = jnp.transpose(dmod_g[:, :, :bsz], (1, 0, 2, 3)).reshape(2, NDEV * bsz, N_MOD * dm)
    dctx_full = dmod_g[:, :, bsz]
    dlat_sh = lax.dynamic_slice(dlat_full, (0, 0, me * nsm), (2, NDEV * bsz, nsm))
    dctx_sh = lax.dynamic_slice(dctx_full, (0, 0, me * nsm), (NDEV, 2, nsm))
    g_w_mod, g_b_mod, dcctx8 = _mod_bwd(craw, w_mod, dlat_sh, dctx_sh, dlat_full, dctx_full)

    (g_cctx,) = _exchange([wide(dcctx8[:1])], a2a=False, name="gather_cctx")
    small, gathered = [("c_ctx", wide(dcctx8[:1]))] + small, [g_cctx] + gathered
    res = {}
    for (k, g), parts in zip(small, gathered):
        w2, m2, v2 = (given[p + k].reshape(g.shape) for p in ("", "m_", "v_"))
        res[k] = tuple(o.reshape(given[k].shape) for o in _adamw(parts, w2, m2, v2, name="adamw_" + k))

    def total(parts, name):
        z = jnp.zeros(parts.shape[1:], F32)
        return _adamw(parts, z, z, z, name=name)[0]

    loss = jnp.sum(total(gathered[-1], "sum_loss"))
    dl_tot = total(gathered[-2], "sum_dlb").reshape(dl_hg.shape)
    nlb = hg_lower_bounds.shape[2]
    g_lb = lax.dynamic_slice(dl_tot, (0, 0, me * nlb), (2, 2, nlb))

    def adam_local(name, g, shape2):
        w, m, v = given[name], given["m_" + name], given["v_" + name]
        out = _adamw(g.reshape((1,) + shape2), w.reshape(shape2), m.reshape(shape2), v.reshape(shape2),
                     name="adamw_" + name)
        return tuple(o.reshape(w.shape) for o in out)

    def adam_parts(name, p):
        w, m, v = given[name], given["m_" + name], given["v_" + name]
        shape2 = (p.shape[0], -1, w.shape[-1])
        p3 = p.reshape(shape2)
        s2 = p3.shape[1:]
        out = _adamw(p3, w.reshape(s2), m.reshape(s2), v.reshape(s2), name="adamw_" + name)
        return tuple(o.reshape(w.shape) for o in out)

    res["hg_lower_bounds"] = adam_local("hg_lower_bounds", g_lb, (2 * 2, nlb))
    res["w_mod"] = adam_local("w_mod", g_w_mod, (2 * dm, nsm))
    res["b_mod"] = adam_local("b_mod", g_b_mod, (2, N_MOD * dm))
    res["s5_w_in"] = adam_parts("s5_w_in", p_s5in)
    res["s5_w_glu"] = adam_parts("s5_w_glu", p_glu)
    res["s5_w_out"] = adam_parts("s5_w_out", p_s5out)
    res["hg_w_in"] = adam_parts("hg_w_in", p_hgin)
    res["hg_w_out"] = adam_parts("hg_w_out", p_hgout)
    res["ffn_w_up"] = adam_parts("ffn_w_up", jnp.stack([p_up0, p_up1], axis=1))
    res["ffn_w_down"] = adam_parts("ffn_w_down", jnp.stack([p_dn0, p_dn1], axis=1))
    res["ffn_conv_w"] = adam_parts("ffn_conv_w", p_cw)

    grad_x = jnp.transpose(dz0[nctx:].reshape(lx, bsz, dm), (1, 0, 2))
    order = ["c_ctx", "w_mod", "b_mod", "norm1_w", "norm2_w", "final_norm_w", "s5_w_in", "s5_lam_re", "s5_lam_im",
             "s5_log_step", "s5_b_re", "s5_b_im", "s5_c_re", "s5_c_im", "s5_d", "s5_w_glu", "s5_w_out", "hg_w_in",
             "hg_lower_bounds", "hg_gnorm_w", "hg_w_out", "ffn_w_up", "ffn_conv_w", "ffn_conv_b", "ffn_w_down"]
    outs = [loss, grad_x]
    for j in range(4):
        outs += [res[k][j].reshape(given[k].shape) for k in order]
    return tuple(outs)
```

```python
import functools

import jax
import jax.numpy as jnp
from jax import lax
from jax.experimental import pallas as pl
from jax.experimental.pallas import tpu as pltpu

F32 = jnp.float32
BF16 = jnp.bfloat16
NDEV = 8
LOCAL_B = 4
NORM_EPS = 1e-6
N_MOD = 6
S5_GROUP = 16
S5_STATE = 64
S5_LAM_RE_MAX = -1e-4
HG_HEAD = 128
HG_ROWS = 128
GRID_W = 64
ADAM_LR, ADAM_B1, ADAM_B2, ADAM_EPS, ADAM_WD, ADAM_STEP = 0.001, 0.9, 0.999, 1e-08, 0.01, 10
VMEM_BYTES_V7X = 64 * 1024 * 1024
LANES = 128
SUBLANES = 8

NN = (((1,), (0,)), ((), ()))
NT = (((1,), (1,)), ((), ()))
TN = (((0,), (0,)), ((), ()))
MESH = pl.DeviceIdType.MESH


def _params(sem=None, vmem=None):
    kw = {}
    if sem is not None:
        kw["dimension_semantics"] = sem
    if vmem is not None:
        kw["vmem_limit_bytes"] = int(min(vmem, VMEM_BYTES_V7X - (4 << 20)))
    return pltpu.CompilerParams(**kw)


def _nbytes(shape, dtype):
    n = 1
    for s in shape:
        n *= 1 if s is None else s
    return n * jnp.dtype(dtype).itemsize


def _dot(a, b, dims=NN, precision=None):
    return lax.dot_general(a, b, dims, preferred_element_type=F32, precision=precision)


def _sigmoid(x):
    return 1.0 / (1.0 + jnp.exp(-x))


class _Xchg:
    def __init__(self, arrs, a2a):
        self.arrs, self.a2a, self.n = list(arrs), list(a2a), len(arrs)

    def out_shape(self):
        return [jax.ShapeDtypeStruct(a.shape if f else (NDEV,) + a.shape, a.dtype) for a, f in zip(self.arrs, self.a2a)]

    def scratch(self):
        return [pltpu.SemaphoreType.DMA((self.n * (NDEV - 1),)), pltpu.SemaphoreType.DMA((self.n * (NDEV - 1),)),
                pltpu.SemaphoreType.DMA((self.n,))]

    def _copies(self, ins, outs, sems, with_recvs):
        send_sems, recv_sems, loc_sems = sems
        x, y, c = lax.axis_index("x"), lax.axis_index("y"), lax.axis_index("c")
        me = 4 * x + 2 * y + c
        local, sends, recvs = [], [], []
        for a in range(self.n):
            src = ins[a].at[me] if self.a2a[a] else ins[a]
            local.append(pltpu.make_async_copy(src, outs[a].at[me], loc_sems.at[a]))
            for k in range(1, NDEV):
                px = (1 - x) if (k >> 2) & 1 else x
                py = (1 - y) if (k >> 1) & 1 else y
                pc = (1 - c) if k & 1 else c
                p = 4 * px + 2 * py + pc
                s = a * (NDEV - 1) + k - 1
                src = ins[a].at[p] if self.a2a[a] else ins[a]
                kw = dict(src_ref=src, send_sem=send_sems.at[s], recv_sem=recv_sems.at[s], device_id=(px, py, pc),
                          device_id_type=MESH)
                sends.append(pltpu.make_async_remote_copy(dst_ref=outs[a].at[me], **kw))
                if with_recvs:
                    recvs.append(pltpu.make_async_remote_copy(dst_ref=outs[a].at[p], **kw))
        return local, sends, recvs

    def start(self, ins, outs, sems):
        local, sends, _ = self._copies(ins, outs, sems, False)
        for cp in local + sends:
            cp.start()

    def wait(self, ins, outs, sems):
        local, sends, recvs = self._copies(ins, outs, sems, True)
        for cp in sends:
            cp.wait_send()
        for cp in recvs:
            cp.wait_recv()
        for cp in local:
            cp.wait()


def _exchange(arrs, *, a2a, name):
    xch = _Xchg(arrs, a2a if isinstance(a2a, (list, tuple)) else [a2a] * len(arrs))
    n = xch.n

    def body(*refs):
        xch.start(refs[:n], refs[n:2 * n], refs[2 * n:])
        xch.wait(refs[:n], refs[n:2 * n], refs[2 * n:])

    res = pl.pallas_call(
        body, name=name, out_shape=xch.out_shape(),
        in_specs=[pl.BlockSpec(memory_space=pl.ANY)] * n, out_specs=[pl.BlockSpec(memory_space=pl.ANY)] * n,
        scratch_shapes=xch.scratch(),
    )(*arrs)
    return list(res)


def _call(body, *, name, out_shape, grid, in_specs, out_specs, scratch, params, args, xch=None):
    in_specs, out_specs, out_shape, scratch, args = list(in_specs), list(out_specs), list(out_shape), list(scratch), list(args)
    n_in, n_out, n_scr = len(in_specs), len(out_shape), len(scratch)
    if xch is not None:
        k = xch.n
        inner = body

        def body(*refs):
            ins, xin = refs[:n_in], refs[n_in:n_in + k]
            outs, xout = refs[n_in + k:n_in + k + n_out], refs[n_in + k + n_out:n_in + 2 * k + n_out]
            scr = refs[n_in + 2 * k + n_out:n_in + 2 * k + n_out + n_scr]
            sems = refs[n_in + 2 * k + n_out + n_scr:]
            first = pl.program_id(0) == 0
            last = pl.program_id(0) == grid[0] - 1
            for ax in range(1, len(grid)):
                first = jnp.logical_and(first, pl.program_id(ax) == 0)
                last = jnp.logical_and(last, pl.program_id(ax) == grid[ax] - 1)

            @pl.when(first)
            def _():
                xch.start(xin, xout, sems)

            inner(*ins, *outs, *scr)

            @pl.when(last)
            def _():
                xch.wait(xin, xout, sems)

        anyspec = pl.BlockSpec(memory_space=pl.ANY)
        in_specs += [anyspec] * k
        out_specs += [anyspec] * k
        out_shape += xch.out_shape()
        scratch += xch.scratch()
        args += xch.arrs
    res = pl.pallas_call(body, name=name, out_shape=out_shape, grid=grid, in_specs=in_specs, out_specs=out_specs,
                         scratch_shapes=scratch, compiler_params=params)(*args)
    return list(res[:n_out]), list(res[n_out:])


def _mm(a, b, *, name, grid, a_spec, b_spec, o_spec, o_shape, o_dtype, dims, base=None):
    nk = grid[2]
    o_block = tuple(s for s in o_spec.block_shape if s is not None)

    def body(a_ref, b_ref, *rest):
        base_ref = rest[0] if base is not None else None
        o_ref, scr = rest[1 if base is not None else 0], rest[2 if base is not None else 1:]
        r = _dot(a_ref[...].astype(BF16), b_ref[...].astype(BF16), dims)
        if nk == 1:
            if base is not None:
                r = r + base_ref[...].astype(F32)
            o_ref[...] = r.astype(o_dtype)
        else:
            acc = scr[0]
            k = pl.program_id(2)

            @pl.when(k == 0)
            def _():
                acc[...] = r

            @pl.when(k > 0)
            def _():
                acc[...] += r

            @pl.when(k == nk - 1)
            def _():
                tot = acc[...] if base is None else acc[...] + base_ref[...].astype(F32)
                o_ref[...] = tot.astype(o_dtype)

    blocks = (_nbytes(a_spec.block_shape, a.dtype) + _nbytes(b_spec.block_shape, b.dtype) + _nbytes(o_block, o_dtype)
              + (_nbytes(o_block, base.dtype) if base is not None else 0))
    scratch = [pltpu.VMEM(o_block, F32)] if nk > 1 else []
    vmem = 2 * blocks + 3 * _nbytes(o_block, F32) + (8 << 20)
    return pl.pallas_call(
        body, name=name, out_shape=jax.ShapeDtypeStruct(o_shape, o_dtype), grid=grid,
        in_specs=[a_spec, b_spec] + ([o_spec] if base is not None else []), out_specs=o_spec, scratch_shapes=scratch,
        compiler_params=_params(("parallel", "parallel", "arbitrary"), vmem),
    )(a, b, *([base] if base is not None else []))


def _lin(a, w, *, name, trans_w=False, tm, tn, o_dtype=F32, kblk=0, base=None):
    m, kk = a.shape
    nout = w.shape[0] if trans_w else w.shape[1]
    if trans_w:
        b_spec = pl.BlockSpec((tn, kk), lambda j, i, k: (j, kblk))
    else:
        b_spec = pl.BlockSpec((kk, tn), lambda j, i, k: (0, j))
    return _mm(a, w, name=name, grid=(nout // tn, m // tm, 1), dims=NT if trans_w else NN, o_shape=(m, nout),
               o_dtype=o_dtype, o_spec=pl.BlockSpec((tm, tn), lambda j, i, k: (i, j)),
               a_spec=pl.BlockSpec((tm, kk), lambda j, i, k: (i, 0)), b_spec=b_spec, base=base)


def _lin_cat(pieces, w, *, name, tm, o_dtype=F32):
    m, kk = pieces[0].shape
    nout, npc = w.shape[0], len(pieces)

    def body(*refs):
        w_ref, o_ref = refs[npc], refs[npc + 1]
        acc = _dot(refs[0][...], w_ref[:, 0:kk], NT)
        for p in range(1, npc):
            acc = acc + _dot(refs[p][...], w_ref[:, p * kk:(p + 1) * kk], NT)
        o_ref[...] = acc.astype(o_dtype)

    row = pl.BlockSpec((tm, kk), lambda i: (i, 0))
    vmem = 2 * (npc * _nbytes((tm, kk), pieces[0].dtype) + _nbytes(w.shape, w.dtype)) + 4 * _nbytes((tm, nout), F32) + (8 << 20)
    return pl.pallas_call(
        body, name=name, out_shape=jax.ShapeDtypeStruct((m, nout), o_dtype), grid=(m // tm,),
        in_specs=[row] * npc + [pl.BlockSpec(w.shape, lambda i: (0, 0))], out_specs=pl.BlockSpec((tm, nout), lambda i: (i, 0)),
        compiler_params=_params(("parallel",), vmem),
    )(*pieces, w)


def _lin_w(a, dy, *, name, ta, tn, tkr):
    m, ka = a.shape
    nout = dy.shape[1]
    return _mm(a, dy, name=name, grid=(ka // ta, nout // tn, m // tkr), dims=TN, o_shape=(ka, nout), o_dtype=BF16,
               o_spec=pl.BlockSpec((ta, tn), lambda i, j, k: (i, j)),
               a_spec=pl.BlockSpec((tkr, ta), lambda i, j, k: (k, i)),
               b_spec=pl.BlockSpec((tkr, tn), lambda i, j, k: (k, j)))


def _rowk(fn, *, name, n, tm, nctx, rows=(), pats=(), consts=(), out_rows=(), out_seg=(), out_acc=()):
    nb, ncb = n // tm, nctx // tm
    nr, npat, ncst = len(rows), len(pats), len(consts)
    no, nseg, nacc = len(out_rows), len(out_seg), len(out_acc)
    in_specs, blocks = [], 0
    for arr, w, cb, off in rows:
        in_specs.append(pl.BlockSpec((tm, w), lambda i, cb=cb, off=off: (jnp.maximum(i - off, 0), cb)))
        blocks += _nbytes((tm, w), arr.dtype)
    for p in pats:
        in_specs.append(pl.BlockSpec((None, SUBLANES, p.shape[2]), lambda i: (jnp.where(i >= ncb, 1, 0), 0, 0)))
    for cst in consts:
        in_specs.append(pl.BlockSpec(cst.shape, lambda i: (0, 0)))
    out_shape, out_specs = [], []
    for wt, dt, w, cb in out_rows:
        out_shape.append(jax.ShapeDtypeStruct((n, wt), dt))
        out_specs.append(pl.BlockSpec((tm, w), lambda i, cb=cb: (i, cb)))
        blocks += _nbytes((tm, w), dt)
    for w in out_seg:
        out_shape.append(jax.ShapeDtypeStruct((SUBLANES, w), F32))
        out_specs.append(pl.BlockSpec((SUBLANES, w), lambda i: (0, 0)))
    for w in out_acc:
        out_shape.append(jax.ShapeDtypeStruct((1, w), F32))
        out_specs.append(pl.BlockSpec((1, w), lambda i: (0, 0)))
    scratch = [pltpu.VMEM((2, SUBLANES, w), F32) for w in out_seg] + [pltpu.VMEM((SUBLANES, w), F32) for w in out_acc]

    def body(*refs):
        r_in = refs[:nr]
        p_in = refs[nr:nr + npat]
        c_in = refs[nr + npat:nr + npat + ncst]
        base = nr + npat + ncst
        o_rows = refs[base:base + no]
        o_seg = refs[base + no:base + no + nseg]
        o_acc = refs[base + no + nseg:base + no + nseg + nacc]
        s_seg = refs[base + no + nseg + nacc:base + no + nseg + nacc + nseg]
        s_acc = refs[base + no + nseg + nacc + nseg:]
        i = pl.program_id(0)
        rv = [r[...].astype(F32).reshape(tm // SUBLANES, SUBLANES, r.shape[1]) for r in r_in]
        pv = [p[...] for p in p_in]
        cv = [c[...] for c in c_in]
        is_lat = (i >= ncb).astype(F32)
        ro, so, ao = fn(rv, pv, cv, is_lat)
        for ref, val in zip(o_rows, ro):
            ref[...] = val.reshape(tm, ref.shape[1]).astype(ref.dtype)
        if nseg or nacc:
            @pl.when(i == 0)
            def _():
                for s in list(s_seg) + list(s_acc):
                    s[...] = jnp.zeros(s.shape, F32)

            seg = jnp.where(i >= ncb, 1, 0)
            for s, val in zip(s_seg, so):
                s[seg] = s[seg] + val
            for s, val in zip(s_acc, ao):
                s[...] = s[...] + val

            @pl.when(i == nb - 1)
            def _():
                for o, s in zip(o_seg, s_seg):
                    lat, ctx = s[1], s[0]
                    row = lax.broadcasted_iota(jnp.int32, lat.shape, 0)
                    lat = lat + pltpu.roll(lat, 4, 0)
                    ctx = jnp.broadcast_to(jnp.sum(ctx, axis=0, keepdims=True), lat.shape)
                    o[...] = jnp.where(row < 4, lat, jnp.where(row == 4, ctx, 0.0))
                for o, s in zip(o_acc, s_acc):
                    o[...] = jnp.sum(s[...], axis=0, keepdims=True)

    vmem = 2 * blocks + 8 * tm * 1024 * 4 + (8 << 20)
    res = pl.pallas_call(
        body, name=name, out_shape=out_shape, grid=(nb,), in_specs=in_specs, out_specs=out_specs,
        scratch_shapes=scratch, compiler_params=_params(("arbitrary",), vmem),
    )(*[r[0] for r in rows], *pats, *consts)
    return list(res)


def _rms(z):
    return lax.rsqrt(jnp.mean(z * z, axis=-1, keepdims=True) + NORM_EPS)


def _norm_mod_fwd(z, w, sh, sc, *, name, dims, res=None):
    n, d = z.shape

    def fn(rv, pv, cv, is_lat):
        zz = rv[0]
        if res is not None:
            zz = zz + pv[2][None] * rv[1]
        h = (zz * _rms(zz) * cv[0]) * (1.0 + pv[1][None]) + pv[0][None]
        return ([zz, h] if res is not None else [h]), [], []

    rows = [(z, d, 0, 0)] + ([(res[0], d, 0, 0)] if res is not None else [])
    pats = [sh, sc] + ([res[1]] if res is not None else [])
    outs = ([(d, F32, d, 0)] if res is not None else []) + [(d, BF16, d, 0)]
    out = _rowk(fn, name=name, n=n, tm=dims["tm_row"], nctx=dims["nctx"], rows=rows, pats=pats, consts=[w],
                out_rows=outs)
    return (out[0], out[1]) if res is not None else (None, out[0])


def _norm_core_bwd(zin, dh, w, sc):
    r = _rms(zin)
    xh = zin * r
    dsh = jnp.sum(dh, axis=0)
    dsc = jnp.sum(dh * (xh * w), axis=0)
    dyv = dh * (1.0 + sc[None])
    dw = jnp.sum(dyv * xh, axis=0)
    dxh = dyv * w
    dx = r * (dxh - xh * jnp.mean(dxh * xh, axis=-1, keepdims=True))
    return dx, dsh, dsc, dw


def _norm_mod_bwd(dh, zin, dz_up, w, sc, *, name, dims, res=None):
    n, d = zin.shape

    def fn(rv, pv, cv, is_lat):
        dx, dsh, dsc, dw = _norm_core_bwd(rv[1], rv[0], cv[0], pv[0])
        dz = rv[2] + dx
        if res is None:
            return [dz], [dsh, dsc], [dw]
        return [dz, dz * pv[1][None]], [dsh, dsc, jnp.sum(dz * rv[3], axis=0)], [dw]

    rows = [(dh, d, 0, 0), (zin, d, 0, 0), (dz_up, d, 0, 0)] + ([(res[0], d, 0, 0)] if res is not None else [])
    pats = [sc] + ([res[1]] if res is not None else [])
    outs = [(d, F32, d, 0)] + ([(d, BF16, d, 0)] if res is not None else [])
    return _rowk(fn, name=name, n=n, tm=dims["tm_row"], nctx=dims["nctx"], rows=rows, pats=pats, consts=[w],
                 out_rows=outs, out_seg=[d] * (3 if res is not None else 2), out_acc=[d])


def _loss_bwd(z1, f, gate, tgt, w, *, name, dims):
    n, d = z1.shape

    def fn(rv, pv, cv, is_lat):
        z2 = rv[0] + pv[0][None] * rv[1]
        r = _rms(z2)
        xh = z2 * r
        err = (xh * cv[0] - rv[2]) * is_lat
        dout = err * (1.0 / d)
        dxh = dout * cv[0]
        dz = r * (dxh - xh * jnp.mean(dxh * xh, axis=-1, keepdims=True))
        return ([dz, dz * pv[0][None]], [jnp.sum(dz * rv[1], axis=0)],
                [jnp.sum(0.5 * err * err * (1.0 / d), axis=0), jnp.sum(dout * xh, axis=0)])

    tm = dims["tm_row"]
    rows = [(z1, d, 0, 0), (f, d, 0, 0), (tgt, d, 0, dims["nctx"] // tm)]
    return _rowk(fn, name=name, n=n, tm=tm, nctx=dims["nctx"], rows=rows, pats=[gate], consts=[w],
                 out_rows=[(d, F32, d, 0), (d, BF16, d, 0)], out_seg=[d], out_acc=[d, d])


def _gelu(y):
    return jax.nn.gelu(y, approximate=True)


def _conv_masks(tb, i):
    tok = lax.broadcasted_iota(jnp.int32, (tb, 1), 0) >> 2
    last = jnp.where(i == 0, tb // LOCAL_B - 1, GRID_W - 1)
    wpos = tok & last
    return wpos == 0, wpos == last


CONV_LANES = 2 * LANES


def _conv_taps(u_ref, cw_ref, cb_ref, no_left, no_right, tb):
    uu = u_ref[...].astype(F32)
    ul = jnp.where(no_left, 0.0, pltpu.roll(uu, LOCAL_B, 0))
    ur = jnp.where(no_right, 0.0, pltpu.roll(uu, tb - LOCAL_B, 0))
    val = cb_ref[...] + ul * cw_ref[pl.ds(0, 1), :] + uu * cw_ref[pl.ds(1, 1), :] + ur * cw_ref[pl.ds(2, 1), :]
    return val, ul, uu, ur


def _convffn_specs(tb, nj):
    cl = CONV_LANES
    return [pl.BlockSpec((tb, cl), lambda j, i: (i, j)), pl.BlockSpec((tb, cl), lambda j, i: (i, nj + j)),
            pl.BlockSpec((3, cl), lambda j, i: (0, j)), pl.BlockSpec((3, cl), lambda j, i: (0, nj + j)),
            pl.BlockSpec((1, cl), lambda j, i: (0, j)), pl.BlockSpec((1, cl), lambda j, i: (0, nj + j))]


def _convffn_fwd(u, cw, cb, *, name, dims):
    n, f2 = u.shape
    tb, nj = dims["nctx"], f2 // 2 // CONV_LANES

    def body(ua_ref, ug_ref, cwa_ref, cwg_ref, cba_ref, cbg_ref, o_ref):
        no_left, no_right = _conv_masks(tb, pl.program_id(1))
        a = _conv_taps(ua_ref, cwa_ref, cba_ref, no_left, no_right, tb)[0]
        g = _conv_taps(ug_ref, cwg_ref, cbg_ref, no_left, no_right, tb)[0]
        o_ref[...] = (a * _sigmoid(a) * g).astype(BF16)

    vmem = 16 * tb * CONV_LANES * 4 + (8 << 20)
    return pl.pallas_call(
        body, name=name, out_shape=jax.ShapeDtypeStruct((n, f2 // 2), BF16), grid=(nj, n // tb),
        in_specs=_convffn_specs(tb, nj), out_specs=pl.BlockSpec((tb, CONV_LANES), lambda j, i: (i, j)),
        compiler_params=_params(("parallel", "arbitrary"), vmem),
    )(u, u, cw, cw, cb, cb)


def _convffn_bwd(u, dhm, cw, cb, *, name, dims, xch=None):
    n, f2 = u.shape
    tb, nj = dims["nctx"], f2 // 2 // CONV_LANES

    def body(ua_ref, ug_ref, cwa_ref, cwg_ref, cba_ref, cbg_ref, dh_ref, dua_ref, dug_ref, dcwa_ref, dcwg_ref, dcba_ref,
             dcbg_ref):
        i = pl.program_id(1)
        no_left, no_right = _conv_masks(tb, i)

        @pl.when(i == 0)
        def _():
            for ref in (dcwa_ref, dcwg_ref, dcba_ref, dcbg_ref):
                ref[...] = jnp.zeros(ref.shape, F32)

        a, al, ac, ar = _conv_taps(ua_ref, cwa_ref, cba_ref, no_left, no_right, tb)
        g, gl, gc, gr = _conv_taps(ug_ref, cwg_ref, cbg_ref, no_left, no_right, tb)
        dh = dh_ref[...].astype(F32)
        sa = _sigmoid(a)
        dg = dh * (a * sa)
        da = dh * g * (sa * (1.0 + a * (1.0 - sa)))
        for dc, (tl, tc, tr), cw_ref, du_ref, dcw_ref, dcb_ref in (
                (da, (al, ac, ar), cwa_ref, dua_ref, dcwa_ref, dcba_ref),
                (dg, (gl, gc, gr), cwg_ref, dug_ref, dcwg_ref, dcbg_ref)):
            dcb_ref[...] += jnp.sum(dc, axis=0, keepdims=True)
            dcw_ref[pl.ds(0, 1), :] += jnp.sum(dc * tl, axis=0, keepdims=True)
            dcw_ref[pl.ds(1, 1), :] += jnp.sum(dc * tc, axis=0, keepdims=True)
            dcw_ref[pl.ds(2, 1), :] += jnp.sum(dc * tr, axis=0, keepdims=True)
            du = (dc * cw_ref[pl.ds(1, 1), :]
                  + pltpu.roll(jnp.where(no_left, 0.0, dc) * cw_ref[pl.ds(0, 1), :], tb - LOCAL_B, 0)
                  + pltpu.roll(jnp.where(no_right, 0.0, dc) * cw_ref[pl.ds(2, 1), :], LOCAL_B, 0))
            du_ref[...] = du.astype(BF16)

    cl, f = CONV_LANES, f2 // 2
    sd = jax.ShapeDtypeStruct
    row = pl.BlockSpec((tb, cl), lambda j, i: (i, j))
    vmem = 24 * tb * cl * 4 + (8 << 20)
    return _call(
        body, name=name, xch=xch, args=[u, u, cw, cw, cb, cb, dhm], scratch=[],
        out_shape=[sd((n, f), BF16), sd((n, f), BF16), sd((3, f), F32), sd((3, f), F32), sd((1, f), F32), sd((1, f), F32)],
        grid=(nj, n // tb), in_specs=_convffn_specs(tb, nj) + [row],
        out_specs=[row, row, pl.BlockSpec((3, cl), lambda j, i: (0, j)), pl.BlockSpec((3, cl), lambda j, i: (0, j)),
                   pl.BlockSpec((1, cl), lambda j, i: (0, j)), pl.BlockSpec((1, cl), lambda j, i: (0, j))],
        params=_params(("arbitrary", "arbitrary"), vmem))


def _s5_disc(lr, li, ls, brt, bit):
    lr = jnp.minimum(lr, S5_LAM_RE_MAX)
    dt = jnp.exp(ls)
    mag = jnp.exp(lr * dt)
    ar = mag * jnp.cos(li * dt)
    ai = mag * jnp.sin(li * dt)
    den = lr * lr + li * li
    nr = ar - 1.0
    cr = (nr * lr + ai * li) / den
    ci = (ai * lr - nr * li) / den
    return ar, ai, cr * brt - ci * bit, cr * bit + ci * brt


def _s5_disc_fwd(lr, li, ls, brt, bit):
    def body(lr_ref, li_ref, ls_ref, br_ref, bi_ref, ar_ref, ai_ref, bbr_ref, bbi_ref):
        ar, ai, bbr, bbi = _s5_disc(lr_ref[...], li_ref[...], ls_ref[...], br_ref[...], bi_ref[...])
        ar_ref[...] = ar
        ai_ref[...] = ai
        bbr_ref[...] = bbr
        bbi_ref[...] = bbi

    sd = jax.ShapeDtypeStruct
    return pl.pallas_call(body, name="s5_disc_fwd",
                          out_shape=[sd(lr.shape, F32), sd(lr.shape, F32), sd(brt.shape, F32), sd(brt.shape, F32)],
                          compiler_params=_params(None, 32 << 20))(lr, li, ls, brt, bit)


def _s5_disc_bwd(lr, li, ls, brt, bit, dar, dai, dbbr, dbbi):
    def body(lr_ref, li_ref, ls_ref, br_ref, bi_ref, dar_ref, dai_ref, dbbr_ref, dbbi_ref,
             dlr_ref, dli_ref, dls_ref, dbr_ref, dbi_ref):
        _, vjp = jax.vjp(_s5_disc, lr_ref[...], li_ref[...], ls_ref[...], br_ref[...], bi_ref[...])
        dlr, dli, dls, dbr, dbi = vjp((dar_ref[...], dai_ref[...], dbbr_ref[...], dbbi_ref[...]))
        dlr_ref[...] = dlr
        dli_ref[...] = dli
        dls_ref[...] = dls
        dbr_ref[...] = dbr
        dbi_ref[...] = dbi

    sd = jax.ShapeDtypeStruct
    return pl.pallas_call(body, name="s5_disc_bwd",
                          out_shape=[sd(lr.shape, F32), sd(lr.shape, F32), sd(ls.shape, F32), sd(brt.shape, F32),
                                     sd(brt.shape, F32)],
                          compiler_params=_params(None, 48 << 20))(lr, li, ls, brt, bit, dar, dai, dbbr, dbbi)


def _cmul(ar, ai, xr, xi):
    return ar * xr - ai * xi, ar * xi + ai * xr


def _s5_chunk_of(step, ncc, nc, rev):
    if not rev:
        return step
    return jnp.where(step < ncc, ncc - 1 - step, nc - 1 - (step - ncc))


def _s5_scan2(asc, desc, row0, nrows, a_r_ref, a_i_ref, cr_ref, ci_ref, *, lane_block, extra=None):
    width = asc[0].shape[1]
    nt = nrows // SUBLANES
    for lb in range(width // lane_block):
        lanes = pl.ds(lb * lane_block, lane_block)
        a1r, a1i = a_r_ref[:, lanes], a_i_ref[:, lanes]
        a2r, a2i = pltpu.roll(a1r, 4, 0), pltpu.roll(a1i, 4, 0)
        lo = lax.broadcasted_iota(jnp.int32, a1r.shape, 0) < 4

        def step(t, carry):
            pr, pi = carry[0], carry[1]
            ra = pl.ds(pl.multiple_of(row0 + t * SUBLANES, SUBLANES), SUBLANES)
            rd = pl.ds(pl.multiple_of(row0 + (nt - 1 - t) * SUBLANES, SUBLANES), SUBLANES)
            ur, ui = asc[0][ra, lanes], asc[1][ra, lanes]
            dr, di = desc[0][rd, lanes], desc[1][rd, lanes]
            mr, mi = _cmul(a1r, a1i, pr, pi)
            y1r, y1i = jnp.where(lo, ur, dr) + mr, jnp.where(lo, ui, di) + mi
            mr, mi = _cmul(a2r, a2i, pltpu.roll(y1r, 4, 0), pltpu.roll(y1i, 4, 0))
            y2r, y2i = jnp.where(lo, dr, ur) + mr, jnp.where(lo, di, ui) + mi
            our, oui = jnp.where(lo, y1r, y2r), jnp.where(lo, y1i, y2i)
            odr, odi = jnp.where(lo, y2r, y1r), jnp.where(lo, y2i, y1i)
            asc[0][ra, lanes] = our
            asc[1][ra, lanes] = oui
            desc[0][rd, lanes] = odr
            desc[1][rd, lanes] = odi
            nxt = (pltpu.roll(y2r, 4, 0), pltpu.roll(y2i, 4, 0))
            if extra is None:
                return nxt
            return nxt + tuple(extra(t, nt - 1 - t, lanes, (our, oui), (odr, odi), carry[2:]))

        init = (cr_ref[:, lanes], ci_ref[:, lanes])
        if extra is not None:
            init = init + tuple(extra.init(lanes))
        out = lax.fori_loop(0, nt, step, init)
        cr_ref[:, lanes] = out[0]
        ci_ref[:, lanes] = out[1]
        if extra is not None:
            extra.done(lanes, out[2:])


S5_SPLIT = 2


def _s5_fwd(u, af_r, af_i, bb, cc, dsk, *, name, dims, xch=None):
    n, dm = u.shape
    nk, swk = bb[0].shape[0], bb[0].shape[2]
    rr, sw = dims["s5_rows"], nk * swk
    nkh, dmh, swh = nk // S5_SPLIT, dm // S5_SPLIT, sw // S5_SPLIT
    nc, ncc = n // rr, dims["nctx"] // rr
    c1 = lambda i: _s5_chunk_of(i, ncc, nc, True)

    def body(u0_ref, u1_ref, afr_ref, afi_ref, b0r, b0i, b1r, b1i, c0r, c0i, c1r, c1i, dsk_ref,
             y0_ref, y1_ref, str_ref, sti_ref, s0r, s0i, s1r, s1i, cr, ci):
        @pl.when(pl.program_id(1) == 0)
        def _():
            cr[...] = jnp.zeros(cr.shape, F32)
            ci[...] = jnp.zeros(ci.shape, F32)

        str_ref[...] = cr[...]
        sti_ref[...] = ci[...]
        ub0, ub1 = u0_ref[...].astype(BF16), u1_ref[...].astype(BF16)
        for k in range(nkh):
            cols, sl = slice(k * LANES, (k + 1) * LANES), slice(k * swk, (k + 1) * swk)
            s0r[:, sl] = _dot(ub0[:, cols], b0r[k])
            s0i[:, sl] = _dot(ub0[:, cols], b0i[k])
            s1r[:, sl] = _dot(ub1[:, cols], b1r[k])
            s1i[:, sl] = _dot(ub1[:, cols], b1i[k])
        _s5_scan2((s0r, s0i), (s1r, s1i), 0, rr, afr_ref, afi_ref, cr, ci, lane_block=dims["s5_lane_block"])
        for k in range(nkh):
            cols, sl = slice(k * LANES, (k + 1) * LANES), slice(k * swk, (k + 1) * swk)
            y0_ref[:, cols] = (_dot(s0r[:, sl].astype(BF16), c0r[k]) - _dot(s0i[:, sl].astype(BF16), c0i[k])
                               + dsk_ref[:, cols] * u0_ref[:, cols].astype(F32)).astype(BF16)
            y1_ref[:, cols] = (_dot(s1r[:, sl].astype(BF16), c1r[k])
                               - _dot(s1i[:, sl].astype(BF16), c1i[k])).astype(BF16)

    row0 = pl.BlockSpec((rr, dmh), lambda h, i: (i, h))
    row1 = pl.BlockSpec((rr, dmh), lambda h, i: (c1(i), h))
    tile = pl.BlockSpec((SUBLANES, swh), lambda h, i: (0, h))
    wspec = lambda a: pl.BlockSpec((nkh,) + a.shape[1:], lambda h, i: (h, 0, 0))
    st_spec = pl.BlockSpec((None, SUBLANES, swh), lambda h, i: (i, 0, h))
    sd = jax.ShapeDtypeStruct
    vmem = 4 * rr * swh * 4 + 12 * rr * dmh * 4 + 16 * nkh * LANES * swk * 2 + (12 << 20)
    return _call(
        body, name=name, xch=xch, args=[u, u, af_r, af_i, *bb, *cc, dsk],
        out_shape=[sd((n, dm), BF16), sd((n, dm), BF16), sd((nc, SUBLANES, sw), F32), sd((nc, SUBLANES, sw), F32)],
        grid=(S5_SPLIT, nc),
        in_specs=[row0, row1, tile, tile] + [wspec(a) for a in (*bb, *cc)] + [pl.BlockSpec((1, dmh), lambda h, i: (0, h))],
        out_specs=[row0, row1, st_spec, st_spec],
        scratch=[pltpu.VMEM((rr, swh), F32)] * 4 + [pltpu.VMEM((SUBLANES, swh), F32)] * 2,
        params=_params(("arbitrary", "arbitrary"), vmem))


class _DaHook2:
    def __init__(self, s0, s1, accs):
        self.s0, self.s1, self.accs = s0, s1, accs

    def init(self, lanes):
        return tuple(a[:, lanes] for a in self.accs)

    def done(self, lanes, acc):
        for a, v in zip(self.accs, acc):
            a[:, lanes] = v

    def __call__(self, t1, t0, lanes, l1, l0, acc):
        row = lax.broadcasted_iota(jnp.int32, l1[0].shape, 0)
        b1 = pl.multiple_of(SUBLANES + t1 * SUBLANES, SUBLANES)
        b0 = pl.multiple_of(SUBLANES + t0 * SUBLANES, SUBLANES)
        cur1, nxt1 = pl.ds(b1, SUBLANES), pl.ds(pl.multiple_of(b1 + SUBLANES, SUBLANES), SUBLANES)
        cur0, prv0 = pl.ds(b0, SUBLANES), pl.ds(pl.multiple_of(b0 - SUBLANES, SUBLANES), SUBLANES)
        p1r = pltpu.roll(jnp.where(row >= 4, self.s1[0][cur1, lanes], self.s1[0][nxt1, lanes]), 4, 0)
        p1i = pltpu.roll(jnp.where(row >= 4, self.s1[1][cur1, lanes], self.s1[1][nxt1, lanes]), 4, 0)
        p0r = pltpu.roll(jnp.where(row >= 4, self.s0[0][prv0, lanes], self.s0[0][cur0, lanes]), 4, 0)
        p0i = pltpu.roll(jnp.where(row >= 4, self.s0[1][prv0, lanes], self.s0[1][cur0, lanes]), 4, 0)
        return (acc[0] + p0r * l0[0] + p0i * l0[1], acc[1] + p0r * l0[1] - p0i * l0[0],
                acc[2] + p1r * l1[0] + p1i * l1[1], acc[3] + p1r * l1[1] - p1i * l1[0])


def _s5_bwd(u, dy, af_r, af_i, ab_r, ab_i, bb, cc, dsk, st_r, st_i, *, name, dims, xch=None):
    n, dm = u.shape
    nk, swk = bb[0].shape[0], bb[0].shape[2]
    rr, sw = dims["s5_rows"], nk * swk
    nkh, dmh, swh = nk // S5_SPLIT, dm // S5_SPLIT, sw // S5_SPLIT
    nc, ncc = n // rr, dims["nctx"] // rr
    f0 = lambda i: nc - 1 - i
    f1 = lambda i: _s5_chunk_of(nc - 1 - i, ncc, nc, True)

    def body(u0_ref, u1_ref, dy0_ref, dy1_ref, afr_ref, afi_ref, abr_ref, abi_ref, b0r, b0i, b1r, b1i, c0r, c0i, c1r, c1i,
             dsk_ref, str_ref, sti_ref,
             du0_ref, du1_ref, db0r, db0i, db1r, db1i, dc0r, dc0i, dc1r, dc1i, da0r_ref, da0i_ref, da1r_ref, da1i_ref, dd_ref,
             s0r, s0i, s1r, s1i, l0r, l0i, l1r, l1i, cr, ci, lcr, lci, a0r, a0i, a1r, a1i, dda):
        i = pl.program_id(1)

        @pl.when(i == 0)
        def _():
            for ref in (lcr, lci, a0r, a0i, a1r, a1i, dda, db0r, db0i, db1r, db1i, dc0r, dc0i, dc1r, dc1i):
                ref[...] = jnp.zeros(ref.shape, F32)

        row = lax.broadcasted_iota(jnp.int32, (SUBLANES, swh), 0)
        for st_ref, car, z0, z1 in ((str_ref, cr, s0r, s1r), (sti_ref, ci, s0i, s1i)):
            st = st_ref[...]
            car[...] = st
            z0[pl.ds(0, SUBLANES), :] = jnp.where(row < 4, st, pltpu.roll(st, 4, 0))
            z1[pl.ds(rr + SUBLANES, SUBLANES), :] = jnp.where(row >= 4, st, pltpu.roll(st, 4, 0))
        body_rows = pl.ds(SUBLANES, rr)
        ub0, ub1 = u0_ref[...].astype(BF16), u1_ref[...].astype(BF16)
        dyb0, dyb1 = dy0_ref[...].astype(BF16), dy1_ref[...].astype(BF16)
        for k in range(nkh):
            cols, sl = slice(k * LANES, (k + 1) * LANES), slice(k * swk, (k + 1) * swk)
            s0r[body_rows, sl] = _dot(ub0[:, cols], b0r[k])
            s0i[body_rows, sl] = _dot(ub0[:, cols], b0i[k])
            s1r[body_rows, sl] = _dot(ub1[:, cols], b1r[k])
            s1i[body_rows, sl] = _dot(ub1[:, cols], b1i[k])
        _s5_scan2((s0r, s0i), (s1r, s1i), SUBLANES, rr, afr_ref, afi_ref, cr, ci, lane_block=dims["s5_lane_block"])
        for k in range(nkh):
            cols, sl = slice(k * LANES, (k + 1) * LANES), slice(k * swk, (k + 1) * swk)
            for dyk, lr, li, sr, si, ccr, cci, dcr, dci in ((dyb0[:, cols], l0r, l0i, s0r, s0i, c0r, c0i, dc0r, dc0i),
                                                           (dyb1[:, cols], l1r, l1i, s1r, s1i, c1r, c1i, dc1r, dc1i)):
                lr[:, sl] = _dot(dyk, ccr[k], NT)
                li[:, sl] = -_dot(dyk, cci[k], NT)
                dcr[k] += _dot(dyk, sr[body_rows, sl].astype(BF16), TN)
                dci[k] -= _dot(dyk, si[body_rows, sl].astype(BF16), TN)
        _s5_scan2((l1r, l1i), (l0r, l0i), 0, rr, abr_ref, abi_ref, lcr, lci, lane_block=dims["s5_lane_block"] // 2,
                  extra=_DaHook2((s0r, s0i), (s1r, s1i), (a0r, a0i, a1r, a1i)))
        for k in range(nkh):
            cols, sl = slice(k * LANES, (k + 1) * LANES), slice(k * swk, (k + 1) * swk)
            for uk, lr, li, br, bi, dbr, dbi, du_ref, first in ((ub0[:, cols], l0r, l0i, b0r, b0i, db0r, db0i, du0_ref, True),
                                                              (ub1[:, cols], l1r, l1i, b1r, b1i, db1r, db1i, du1_ref, False)):
                lrk, lik = lr[:, sl].astype(BF16), li[:, sl].astype(BF16)
                dbr[k] += _dot(uk, lrk, TN)
                dbi[k] += _dot(uk, lik, TN)
                duk = _dot(lrk, br[k], NT) + _dot(lik, bi[k], NT)
                if first:
                    duk = duk + dsk_ref[:, cols] * dy0_ref[:, cols]
                du_ref[:, cols] = duk
        dda[...] += jnp.sum((dy0_ref[...] * u0_ref[...].astype(F32)).reshape(rr // SUBLANES, SUBLANES, dmh), axis=0)

        @pl.when(i == nc - 1)
        def _():
            for o, a in ((da0r_ref, a0r), (da0i_ref, a0i), (da1r_ref, a1r), (da1i_ref, a1i), (dd_ref, dda)):
                o[...] = jnp.sum(a[...], axis=0, keepdims=True)

    row0 = pl.BlockSpec((rr, dmh), lambda h, i: (f0(i), h))
    row1 = pl.BlockSpec((rr, dmh), lambda h, i: (f1(i), h))
    tile = pl.BlockSpec((SUBLANES, swh), lambda h, i: (0, h))
    wspec = lambda a: pl.BlockSpec((nkh,) + a.shape[1:], lambda h, i: (h, 0, 0))
    st_spec = pl.BlockSpec((None, SUBLANES, swh), lambda h, i: (f0(i), 0, h))
    vec = lambda w: pl.BlockSpec((1, w), lambda h, i: (0, h))
    sd = jax.ShapeDtypeStruct
    out_shape = ([sd((n, dm), F32)] * 2 + [sd(a.shape, F32) for a in (*bb, *bb)] + [sd((1, sw), F32)] * 4 + [sd((1, dm), F32)])
    out_specs = [row0, row1] + [wspec(a) for a in (*bb, *bb)] + [vec(swh)] * 4 + [vec(dmh)]
    scratch = ([pltpu.VMEM((rr + 2 * SUBLANES, swh), F32)] * 4 + [pltpu.VMEM((rr, swh), F32)] * 4
               + [pltpu.VMEM((SUBLANES, swh), F32)] * 8 + [pltpu.VMEM((SUBLANES, dmh), F32)])
    vmem = 8 * (rr + 16) * swh * 4 + 16 * rr * dmh * 4 + 48 * nkh * LANES * swk * 4 + (10 << 20)
    return _call(
        body, name=name, xch=xch, args=[u, u, dy, dy, af_r, af_i, ab_r, ab_i, *bb, *cc, dsk, st_r, st_i],
        out_shape=out_shape, grid=(S5_SPLIT, nc),
        in_specs=[row0, row1, row0, row1, tile, tile, tile, tile] + [wspec(a) for a in (*bb, *cc)] + [vec(dmh), st_spec, st_spec],
        out_specs=out_specs, scratch=scratch, params=_params(("arbitrary", "arbitrary"), vmem))


def _hg_mask(kind, rev):
    if kind == "tot":
        r = lax.broadcasted_iota(jnp.int32, (SUBLANES, HG_ROWS), 0)
        c = lax.broadcasted_iota(jnp.int32, (SUBLANES, HG_ROWS), 1)
        return (c & 3) == r
    r = lax.broadcasted_iota(jnp.int32, (HG_ROWS, HG_ROWS), 0)
    c = lax.broadcasted_iota(jnp.int32, (HG_ROWS, HG_ROWS), 1)
    same = (r & 3) == (c & 3)
    before = ((c >> 2) >= (r >> 2)) if rev else ((c >> 2) <= (r >> 2))
    return jnp.logical_and(same, before if kind == "cum" else jnp.logical_not(before))


def _split2(x):
    hi = x.astype(BF16)
    return hi, (x - hi.astype(F32)).astype(BF16)


@functools.partial(jax.custom_vjp, nondiff_argnums=(1, 2))
def _mask_sum(x, kind, rev):
    m = _hg_mask(kind, rev).astype(BF16)
    hi, lo = _split2(x)
    return _dot(m, hi) + _dot(m, lo)


def _mask_sum_fwd(x, kind, rev):
    return _mask_sum(x, kind, rev), None


def _mask_sum_bwd(kind, rev, _, g):
    m = _hg_mask(kind, rev).astype(BF16)
    hi, lo = _split2(g)
    return (_dot(m, hi, TN) + _dot(m, lo, TN),)


_mask_sum.defvjp(_mask_sum_fwd, _mask_sum_bwd)


def _hg_chunk(q, v, fraw, l0, l1, st, *, rev):
    nh = q.shape[1] // HG_HEAD
    lb = _sigmoid(l1 - l0)
    logf = jnp.logaddexp(jnp.log(lb), jnp.log1p(-lb) + jax.nn.log_sigmoid(fraw))
    kk = (1.0 - lb) * _sigmoid(fraw * -1.0)
    tri = _hg_mask("cum", rev)
    bcum = _mask_sum(logf, "cum", rev)
    brem = _mask_sum(logf, "rem", rev)
    bend8 = _mask_sum(logf, "tot", rev)
    r8d = lax.broadcasted_iota(jnp.int32, bend8.shape, 0)
    decs = [jnp.exp(jnp.sum(jnp.where(r8d == b, bend8, 0.0), axis=0, keepdims=True)) for b in range(LOCAL_B)]
    qd = (q * jnp.exp(bcum)).astype(BF16)
    kd = (kk * jnp.exp(-bcum)).astype(BF16)
    ke = (kk * jnp.exp(brem)).astype(BF16)
    wide = (HG_ROWS, LOCAL_B * HG_HEAD)
    mine = (lax.broadcasted_iota(jnp.int32, wide, 1) >> 7) == (lax.broadcasted_iota(jnp.int32, wide, 0) & 3)
    per_example = lambda x: jnp.where(mine, jnp.concatenate([x] * LOCAL_B, axis=1), jnp.zeros(wide, x.dtype))
    outs, new = [], []
    for h in range(nh):
        sl = slice(h * HG_HEAD, (h + 1) * HG_HEAD)
        vh = v[:, sl].astype(BF16)
        att = jnp.where(tri, _dot(qd[:, sl], kd[:, sl], NT), 0.0)
        outs.append(_dot(att.astype(BF16), vh) + _dot(per_example(qd[:, sl]), st[h].astype(BF16), NT))
        dec = jnp.concatenate([d[:, sl] for d in decs], axis=1)
        new.append(st[h] * dec + _dot(vh, per_example(ke[:, sl]), TN))
    return jnp.concatenate(outs, axis=1), tuple(new)


def _hg_chunk_of(step, ncc, nc, rev):
    return _s5_chunk_of(step, ncc, nc, rev)


def _hg_fwd_dir(zz, lb2, *, d, name, dims, xch=None):
    n = zz.shape[0]
    dm = zz.shape[1] // 5
    ns, sw = dm // HG_HEAD, LOCAL_B * HG_HEAD
    nc, ncc = n // HG_ROWS, dims["nctx"] // HG_ROWS
    rev = d == 1
    ch = lambda i: _hg_chunk_of(i, ncc, nc, rev)

    def body(q_ref, v_ref, f_ref, l0_ref, l1_ref, o_ref, st_ref, st):
        @pl.when(pl.program_id(0) == 0)
        def _():
            st[...] = jnp.zeros(st.shape, F32)

        st_ref[...] = st[...]
        o, new = _hg_chunk(q_ref[...].astype(F32), v_ref[...].astype(F32), f_ref[...].astype(F32), l0_ref[...],
                           l1_ref[...], tuple(st[j] for j in range(ns)), rev=rev)
        o_ref[...] = o.astype(BF16)
        for j in range(ns):
            st[j] = new[j]

    blk = lambda off: pl.BlockSpec((HG_ROWS, dm), lambda i, off=off: (ch(i), off))
    lspec = lambda layer: pl.BlockSpec((None, None, 1, dm), lambda i, layer=layer: (d, layer, 0, 0))
    return _call(
        body, name=name, xch=xch, args=[zz, zz, zz, lb2, lb2],
        out_shape=[jax.ShapeDtypeStruct((n, dm), BF16), jax.ShapeDtypeStruct((nc, ns, HG_HEAD, sw), F32)],
        grid=(nc,),
        in_specs=[blk(0), blk(1), blk(2 + d), lspec(0), lspec(1)],
        out_specs=[pl.BlockSpec((HG_ROWS, dm), lambda i: (ch(i), 0)),
                   pl.BlockSpec((None, ns, HG_HEAD, sw), lambda i: (ch(i), 0, 0, 0))],
        scratch=[pltpu.VMEM((ns, HG_HEAD, sw), F32)],
        params=_params(("arbitrary",), 48 << 20))


def _hg_bwd_dir(zz, lb2, do, sts, dqv_prev, *, d, name, dims, xch=None):
    n = zz.shape[0]
    dm = zz.shape[1] // 5
    ns, sw = dm // HG_HEAD, LOCAL_B * HG_HEAD
    nc, ncc = n // HG_ROWS, dims["nctx"] // HG_ROWS
    rev = d == 1
    ch = lambda i: _hg_chunk_of(nc - 1 - i, ncc, nc, rev)
    qv_dtype = F32 if d == 0 else BF16

    def body(*refs):
        q_ref, v_ref, f_ref, l0_ref, l1_ref, do_ref, st_ref = refs[:7]
        pos = 7
        if d == 1:
            dqp_ref, dvp_ref = refs[7:9]
            pos = 9
        dq_ref, dv_ref, df_ref, dl_ref, dst = refs[pos:]
        i = pl.program_id(0)

        @pl.when(i == 0)
        def _():
            dst[...] = jnp.zeros(dst.shape, F32)
            dl_ref[...] = jnp.zeros(dl_ref.shape, F32)

        _, vjp = jax.vjp(functools.partial(_hg_chunk, rev=rev), q_ref[...].astype(F32), v_ref[...].astype(F32),
                         f_ref[...].astype(F32), l0_ref[...], l1_ref[...], tuple(st_ref[j] for j in range(ns)))
        dq, dv, df, dl0, dl1, dstn = vjp((do_ref[...].astype(F32), tuple(dst[j] for j in range(ns))))
        for j in range(ns):
            dst[j] = dstn[j]
        if d == 1:
            dq = dq + dqp_ref[...]
            dv = dv + dvp_ref[...]
        dq_ref[...] = dq.astype(qv_dtype)
        dv_ref[...] = dv.astype(qv_dtype)
        df_ref[...] = df.astype(BF16)
        dl_ref[0] += dl0
        dl_ref[1] += dl1

    blk = lambda off: pl.BlockSpec((HG_ROWS, dm), lambda i, off=off: (ch(i), off))
    oblk = pl.BlockSpec((HG_ROWS, dm), lambda i: (ch(i), 0))
    lspec = lambda layer: pl.BlockSpec((None, None, 1, dm), lambda i, layer=layer: (d, layer, 0, 0))
    ins = [zz, zz, zz, lb2, lb2, do, sts] + (list(dqv_prev) if d == 1 else [])
    in_specs = [blk(0), blk(1), blk(2 + d), lspec(0), lspec(1), oblk,
                pl.BlockSpec((None, ns, HG_HEAD, sw), lambda i: (ch(i), 0, 0, 0))]
    in_specs += [oblk, oblk] if d == 1 else []
    sd = jax.ShapeDtypeStruct
    return _call(
        body, name=name, xch=xch, args=ins,
        out_shape=[sd((n, dm), qv_dtype), sd((n, dm), qv_dtype), sd((n, dm), BF16), sd((2, 1, dm), F32)],
        grid=(nc,), in_specs=in_specs,
        out_specs=[oblk, oblk, oblk, pl.BlockSpec((2, 1, dm), lambda i: (0, 0, 0))],
        scratch=[pltpu.VMEM((ns, HG_HEAD, sw), F32)],
        params=_params(("arbitrary",), 56 << 20))


def _hg_readout(o, g, w):
    outs = []
    for h in range(o.shape[-1] // HG_HEAD):
        sl = slice(h * HG_HEAD, (h + 1) * HG_HEAD)
        oh = o[..., sl]
        outs.append(oh * _rms(oh) * w * _sigmoid(g[..., sl]))
    return jnp.concatenate(outs, axis=-1)


def _silu(x):
    return x * _sigmoid(x)


def _mod_fwd(craw, w, b):
    def body(c_ref, w_ref, b_ref, o_ref):
        s = _silu(c_ref[...]).astype(BF16)
        for layer in range(w.shape[0]):
            o_ref[layer] = _dot(s, w_ref[layer].astype(BF16)) + b_ref[layer]

    return pl.pallas_call(body, name="mod_fwd",
                          out_shape=jax.ShapeDtypeStruct((w.shape[0], craw.shape[0], w.shape[2]), F32),
                          compiler_params=_params(None, 40 << 20))(craw, w, b)


def _mod_bwd(craw, w, dlat_sh, dctx_sh, dlat_full, dctx_full):
    nl, dm, ns = w.shape
    nb = dlat_sh.shape[1]

    def body(c_ref, w_ref, dl_ref, dc_ref, dlf_ref, dcf_ref, dw_ref, db_ref, dcc_ref):
        craw_v = c_ref[...]
        s = _silu(craw_v)
        s_lat = s[:nb].astype(BF16)
        s_ctx = s[nb:].astype(BF16)
        row = lax.broadcasted_iota(jnp.int32, (SUBLANES, ns), 0)
        dsc = jnp.zeros((SUBLANES, dm), F32)
        for layer in range(nl):
            tot = dc_ref[0, pl.ds(layer, 1), :]
            totf = dcf_ref[0, pl.ds(layer, 1), :]
            for i in range(1, NDEV):
                tot = tot + dc_ref[i, pl.ds(layer, 1), :]
                totf = totf + dcf_ref[i, pl.ds(layer, 1), :]
            dc8 = jnp.where(row == 0, jnp.broadcast_to(tot, (SUBLANES, ns)), 0.0).astype(BF16)
            dw_ref[layer] = _dot(s_lat, dl_ref[layer].astype(BF16), TN) + _dot(s_ctx, dc8, TN)
            db_ref[layer] = jnp.sum(dlf_ref[layer], axis=0, keepdims=True) + totf
            dsc = dsc + _dot(dc8, w_ref[layer].astype(BF16), NT)
        cc = craw_v[nb:]
        sg = _sigmoid(cc)
        dcc_ref[...] = dsc * (sg * (1.0 + cc * (1.0 - sg)))

    sd = jax.ShapeDtypeStruct
    return pl.pallas_call(body, name="mod_bwd",
                          out_shape=[sd((nl, dm, ns), F32), sd((nl, 1, dlat_full.shape[2]), F32), sd((SUBLANES, dm), F32)],
                          compiler_params=_params(None, 48 << 20))(craw, w, dlat_sh, dctx_sh, dlat_full, dctx_full)


def _adam_rows(r):
    best = None
    for t in range(2 * SUBLANES, min(r, 128) + 1, 2 * SUBLANES):
        if r % t == 0:
            best = t
    return best if best is not None else r


def _adamw(parts, w, m, v, *, name):
    npart, r, c = parts.shape
    tr = _adam_rows(r)

    def body(p_ref, w_ref, m_ref, v_ref, g_ref, d_ref, nm_ref, nv_ref):
        g = p_ref[0].astype(F32)
        for i in range(1, npart):
            g = g + p_ref[i].astype(F32)
        nm = ADAM_B1 * m_ref[...] + (1.0 - ADAM_B1) * g
        nv = ADAM_B2 * v_ref[...] + (1.0 - ADAM_B2) * (g * g)
        m_hat = nm / (1.0 - ADAM_B1 ** ADAM_STEP)
        v_hat = nv / (1.0 - ADAM_B2 ** ADAM_STEP)
        g_ref[...] = g
        d_ref[...] = -ADAM_LR * (m_hat / (jnp.sqrt(v_hat) + ADAM_EPS) + ADAM_WD * w_ref[...])
        nm_ref[...] = nm
        nv_ref[...] = nv

    spec = pl.BlockSpec((tr, c), lambda i: (i, 0))
    vmem = 2 * (npart + 7) * tr * c * 4 + (8 << 20)
    return pl.pallas_call(
        body, name=name, out_shape=[jax.ShapeDtypeStruct((r, c), F32)] * 4, grid=(r // tr,),
        in_specs=[pl.BlockSpec((npart, tr, c), lambda i: (0, i, 0)), spec, spec, spec], out_specs=[spec] * 4,
        compiler_params=_params(("parallel",), vmem),
    )(parts, w, m, v)


def _to_tm(a):
    return jnp.transpose(a, (1, 0, 2)).reshape(a.shape[1] * a.shape[0], a.shape[2])


def _pattern(mod_lat, mod_ctx, m, dm):
    lat = mod_lat[:, m * dm:(m + 1) * dm]
    ctx = jnp.broadcast_to(mod_ctx[None, m * dm:(m + 1) * dm], (SUBLANES, dm))
    return jnp.stack([ctx, jnp.concatenate([lat, lat], axis=0)])


def _blockdiag_b(bt, nk):
    g, h, p = bt.shape
    t = bt.reshape(nk, 8, h, p)
    return jnp.einsum("kghp,gj->kghjp", t, jnp.eye(8, dtype=bt.dtype)).reshape(nk, 8 * h, 8 * p)


def _blockdiag_c(ct, nk):
    g, h, p = ct.shape
    t = ct.reshape(nk, 8, h, p)
    return jnp.einsum("kghp,gj->kgpjh", t, jnp.eye(8, dtype=ct.dtype)).reshape(nk, 8 * p, 8 * h)


def _diag_b(dbb, h, p):
    nk = dbb.shape[0]
    return jnp.einsum("kghgp->kghp", dbb.reshape(nk, 8, h, 8, p)).reshape(nk * 8, h, p)


def _s5_prep(lam_re, lam_im, log_step, b_re, b_im, c_re, c_im, dm):
    ngrp, nk = dm // S5_GROUP, dm // LANES
    sw = ngrp * S5_STATE
    lr4 = lam_re.reshape(2, ngrp, 1, S5_STATE)
    li4 = lam_im.reshape(2, ngrp, 1, S5_STATE)
    ls4 = log_step.reshape(2, ngrp, 1, 1)
    brt = jnp.transpose(b_re, (0, 1, 3, 2))
    bit = jnp.transpose(b_im, (0, 1, 3, 2))
    abar_r, abar_i, bbar_r, bbar_i = _s5_disc_fwd(lr4, li4, ls4, brt, bit)
    half = lambda a, d: jnp.broadcast_to(a[d].reshape(1, sw), (LOCAL_B, sw))
    tile = lambda a, first: jnp.concatenate([half(a, first), half(a, 1 - first)], axis=0)
    bb = tuple(_blockdiag_b(w[d], nk).astype(BF16) for d in range(2) for w in (bbar_r, bbar_i))
    cc = tuple(_blockdiag_c(w[d], nk).astype(BF16) for d in range(2) for w in (c_re, c_im))
    return dict(disc_in=(lr4, li4, ls4, brt, bit), af=(tile(abar_r, 0), tile(abar_i, 0)),
                ab=(tile(abar_r, 1), -tile(abar_i, 1)), bb=bb, cc=cc)


def _s5_param_grads(prep, dbb, dcc, da):
    lr4 = prep["disc_in"][0]
    dar = jnp.stack([da[0], da[2]]).reshape(lr4.shape)
    dai = jnp.stack([da[1], da[3]]).reshape(lr4.shape)
    dbbr = jnp.stack([_diag_b(dbb[0], S5_GROUP, S5_STATE), _diag_b(dbb[2], S5_GROUP, S5_STATE)])
    dbbi = jnp.stack([_diag_b(dbb[1], S5_GROUP, S5_STATE), _diag_b(dbb[3], S5_GROUP, S5_STATE)])
    dlr, dli, dls, dbrt, dbit = _s5_disc_bwd(*prep["disc_in"], dar, dai, dbbr, dbbi)
    g_c_re = jnp.stack([_diag_b(dcc[0], S5_GROUP, S5_STATE), _diag_b(dcc[2], S5_GROUP, S5_STATE)])
    g_c_im = jnp.stack([_diag_b(dcc[1], S5_GROUP, S5_STATE), _diag_b(dcc[3], S5_GROUP, S5_STATE)])
    return dlr, dli, dls, jnp.transpose(dbrt, (0, 1, 3, 2)), jnp.transpose(dbit, (0, 1, 3, 2)), g_c_re, g_c_im


def kernel(x, c, ctx, c_ctx, w_mod, b_mod, norm1_w, norm2_w, final_norm_w, s5_w_in, s5_lam_re, s5_lam_im, s5_log_step, s5_b_re, s5_b_im, s5_c_re, s5_c_im, s5_d, s5_w_glu, s5_w_out, hg_w_in, hg_lower_bounds, hg_gnorm_w, hg_w_out, ffn_w_up, ffn_conv_w, ffn_conv_b, ffn_w_down, loss_target, m_c_ctx, m_w_mod, m_b_mod, m_norm1_w, m_norm2_w, m_final_norm_w, m_s5_w_in, m_s5_lam_re, m_s5_lam_im, m_s5_log_step, m_s5_b_re, m_s5_b_im, m_s5_c_re, m_s5_c_im, m_s5_d, m_s5_w_glu, m_s5_w_out, m_hg_w_in, m_hg_lower_bounds, m_hg_gnorm_w, m_hg_w_out, m_ffn_w_up, m_ffn_conv_w, m_ffn_conv_b, m_ffn_w_down, v_c_ctx, v_w_mod, v_b_mod, v_norm1_w, v_norm2_w, v_final_norm_w, v_s5_w_in, v_s5_lam_re, v_s5_lam_im, v_s5_log_step, v_s5_b_re, v_s5_b_im, v_s5_c_re, v_s5_c_im, v_s5_d, v_s5_w_glu, v_s5_w_out, v_hg_w_in, v_hg_lower_bounds, v_hg_gnorm_w, v_hg_w_out, v_ffn_w_up, v_ffn_conv_w, v_ffn_conv_b, v_ffn_w_down):
    given = dict(locals())
    bsz, lx, dm = x.shape
    lc = ctx.shape[1]
    assert bsz == LOCAL_B and w_mod.shape[0] == 2 and dm % LANES == 0
    n, nctx = (lc + lx) * bsz, lc * bsz
    ngrp, nstate, hgrp = dm // S5_GROUP, S5_STATE, S5_GROUP
    nk = dm // LANES
    dims = dict(nctx=nctx, tm=min(512, nctx), tm_row=min(512, nctx), s5_rows=min(256, nctx),
                s5_lane_block=min(512, 8 * nstate))
    tm = dims["tm"]
    assert nctx % HG_ROWS == 0 and (lx * bsz) % nctx == 0 and lc % GRID_W == 0 and lc & (lc - 1) == 0
    me = 4 * lax.axis_index("x") + 2 * lax.axis_index("y") + lax.axis_index("c")

    gath = _exchange([given[k].astype(BF16) for k in ("s5_w_in", "s5_w_glu", "s5_w_out")]
                     + [c, hg_lower_bounds, ffn_conv_w], a2a=False, name="gather_weights")
    w_s5in, w_glu, w_s5out = (g.reshape(dm, dm) for g in gath[:3])
    c_all, lb_all, cw_all = gath[3:]
    ns_up = ffn_w_up.shape[2]
    w_up, w_dn = [None, None], [None, None]
    cols = lambda g: jnp.transpose(g, (1, 0, 2)).reshape(g.shape[1], -1)
    shards = lambda w: jnp.transpose(w.reshape(w.shape[0], NDEV, -1), (1, 0, 2))
    tn_up, tn_hg, tkr = 2 * ns_up, 2 * hg_w_in.shape[2], 2304
    assert n % tkr == 0 and n % 1024 == 0
    gather = lambda arrs: _Xchg([a.astype(BF16) for a in arrs], [False] * len(arrs))
    scatter = lambda arrs: _Xchg(arrs, [True] * len(arrs))
    cw = [cols(cw_all[:, layer]) for layer in range(2)]
    cb = [ffn_conv_b[layer].reshape(1, -1) for layer in range(2)]
    lb2 = jnp.transpose(lb_all, (1, 2, 0, 3)).reshape(2, 2, 1, dm)

    nsm = w_mod.shape[2]
    craw = jnp.concatenate([c_all.reshape(NDEV * bsz, dm), c_ctx[None], jnp.zeros((SUBLANES - 1, dm), F32)], axis=0)
    b_sh = lax.dynamic_slice(b_mod, (0, me * nsm), (2, nsm)).reshape(2, 1, nsm)
    mod_sh = _mod_fwd(craw, w_mod, b_sh)
    (mod_g,) = _exchange([mod_sh], a2a=False, name="gather_mod")
    mod_full = jnp.transpose(mod_g, (1, 2, 0, 3)).reshape(2, craw.shape[0], NDEV * nsm)
    pat = []
    for layer in range(2):
        mlat = lax.dynamic_slice(mod_full[layer], (me * bsz, 0), (bsz, N_MOD * dm))
        mctx = mod_full[layer, NDEV * bsz]
        pat.append([_pattern(mlat, mctx, m, dm) for m in range(N_MOD)])

    s5p = _s5_prep(s5_lam_re[0], s5_lam_im[0], s5_log_step[0], s5_b_re[0], s5_b_im[0], s5_c_re[0], s5_c_im[0], dm)
    dsk = s5_d.reshape(1, dm)

    z0 = jnp.concatenate([_to_tm(ctx), _to_tm(x)], axis=0)
    tgt = _to_tm(loss_target)
    n1w = [norm1_w[layer].reshape(1, dm) for layer in range(2)]
    n2w = [norm2_w[layer].reshape(1, dm) for layer in range(2)]

    def ffn_fwd(layer, h2):
        u = _lin(h2, w_up[layer], name=f"ffn_up{layer}", tm=1024, tn=tn_up, o_dtype=BF16)
        hm = _convffn_fwd(u, cw[layer], cb[layer], name=f"convffn_fwd{layer}", dims=dims)
        f = _lin(hm, w_dn[layer], name=f"ffn_down{layer}", tm=tm, tn=dm, o_dtype=BF16)
        return u, hm, f

    _, h0 = _norm_mod_fwd(z0, n1w[0], pat[0][0], pat[0][1], name="norm1_l0", dims=dims)
    u_s5 = _lin(h0, w_s5in, name="s5_in", tm=1024, tn=dm, o_dtype=BF16)
    (y_s5a, y_s5b, st_r, st_i), (g_up0, g_dn0, g_hgin, g_hgout, g_dn1) = _s5_fwd(
        u_s5, *s5p["af"], s5p["bb"], s5p["cc"], dsk, name="s5_fwd", dims=dims,
        xch=gather([ffn_w_up[0], ffn_w_down[0], hg_w_in[0], hg_w_out[0], ffn_w_down[1]]))
    w_up[0], w_dn[0] = cols(g_up0), g_dn0.reshape(-1, dm)
    w_hgin, w_hgout, w_dn[1] = cols(g_hgin), g_hgout.reshape(dm, dm), g_dn1.reshape(-1, dm)
    (zg,) = _rowk(lambda rv, pv, cv, il: ([_gelu(rv[0] + rv[1])], [], []), name="s5_gelu", n=n, tm=dims["tm_row"],
                  nctx=nctx, rows=[(y_s5a, dm, 0, 0), (y_s5b, dm, 0, 0)], out_rows=[(dm, BF16, dm, 0)])
    t_glu = _lin(zg, w_glu, name="s5_glu", tm=1024, tn=dm, o_dtype=BF16)
    (z2g,) = _rowk(lambda rv, pv, cv, il: ([rv[0] * _sigmoid(rv[1])], [], []), name="s5_gate", n=n,
                   tm=dims["tm_row"], nctx=nctx, rows=[(zg, dm, 0, 0), (t_glu, dm, 0, 0)],
                   out_rows=[(dm, BF16, dm, 0)])
    ymix0 = _lin(z2g, w_s5out, name="s5_out", tm=1024, tn=dm, o_dtype=BF16)
    z1_l0, h2_l0 = _norm_mod_fwd(z0, n2w[0], pat[0][3], pat[0][4], name="norm2_l0", dims=dims,
                                 res=(ymix0, pat[0][2]))
    u_l0, hm_l0, f_l0 = ffn_fwd(0, h2_l0)

    z2_l0, h1 = _norm_mod_fwd(z1_l0, n1w[1], pat[1][0], pat[1][1], name="norm1_l1", dims=dims,
                              res=(f_l0, pat[0][5]))
    zz = _lin(h1, w_hgin, name="hg_in", tm=1024, tn=tn_hg, o_dtype=BF16)
    (o_f, sts_f), (g_up1,) = _hg_fwd_dir(zz, lb2, d=0, name="hg_fwd_d0", dims=dims, xch=gather([ffn_w_up[1]]))
    w_up[1] = cols(g_up1)
    (o_b, sts_b), _ = _hg_fwd_dir(zz, lb2, d=1, name="hg_fwd_d1", dims=dims)
    gnw = hg_gnorm_w.reshape(1, HG_HEAD)
    (og,) = _rowk(lambda rv, pv, cv, il: ([_hg_readout(rv[0] + rv[1], rv[2], cv[0])], [], []), name="hg_readout",
                  n=n, tm=dims["tm_row"], nctx=nctx, rows=[(o_f, dm, 0, 0), (o_b, dm, 0, 0), (zz, dm, 4, 0)],
                  consts=[gnw], out_rows=[(dm, BF16, dm, 0)])
    ymix1 = _lin(og, w_hgout, name="hg_out", tm=1024, tn=dm, o_dtype=BF16)
    z1_l1, h2_l1 = _norm_mod_fwd(z2_l0, n2w[1], pat[1][3], pat[1][4], name="norm2_l1", dims=dims,
                                 res=(ymix1, pat[1][2]))
    u_l1, hm_l1, f_l1 = ffn_fwd(1, h2_l1)

    dz, df, dgate2_l1, loss_part, dfinal_w = _loss_bwd(z1_l1, f_l1, pat[1][5], tgt, final_norm_w.reshape(1, dm),
                                                        name="loss_bwd", dims=dims)

    def ffn_bwd(layer, df_, u, hm, h2, xch=None):
        dff = hm.shape[1]
        dhm = _lin(df_, w_dn[layer], name=f"ffn_down_bwd_in{layer}", trans_w=True, tm=1024, tn=dff // 2, o_dtype=BF16)
        dwd = _lin_w(hm, df_, name=f"ffn_down_bwd_w{layer}", ta=dff // 2, tn=dm, tkr=tkr)
        (dua, dug, dcwa, dcwg, dcba, dcbg), got = _convffn_bwd(u, dhm, cw[layer], cb[layer],
                                                               name=f"convffn_bwd{layer}", dims=dims, xch=xch)
        dh2 = _lin_cat((dua, dug), w_up[layer], name=f"ffn_up_bwd_in{layer}", tm=tm, o_dtype=BF16)
        dwu = jnp.concatenate([_lin_w(h2, dua, name=f"ffn_up_bwd_w_a{layer}", ta=dm, tn=tn_up, tkr=tkr),
                               _lin_w(h2, dug, name=f"ffn_up_bwd_w_g{layer}", ta=dm, tn=tn_up, tkr=tkr)], axis=1)
        dcw = shards(jnp.concatenate([dcwa, dcwg], axis=1))
        return dh2, shards(dwu), dwd, dcw, jnp.concatenate([dcba, dcbg], axis=1), got

    dh2, dwu_l1, dwd_l1, dcw_l1, dcb_l1, _ = ffn_bwd(1, df, u_l1, hm_l1, h2_l1)
    dz, dymix, dsh2_l1, dsc2_l1, dgate1_l1, dn2w_l1 = _norm_mod_bwd(dh2, z1_l1, dz, n2w[1], pat[1][4], name="norm2_bwd_l1",
                                                                    dims=dims, res=(ymix1, pat[1][2]))
    dog = _lin(dymix, w_hgout, name="hg_out_bwd_in", trans_w=True, tm=1024, tn=dm, o_dtype=BF16)
    dw_hgout = _lin_w(og, dymix, name="hg_out_bwd_w", ta=dm, tn=dm, tkr=tkr)

    def readout_bwd(rv, pv, cv, il):
        _, vjp = jax.vjp(_hg_readout, rv[0] + rv[1], rv[2], cv[0])
        do, dg, dw = vjp(rv[3])
        return [do, dg], [], [jnp.broadcast_to(dw, (SUBLANES, HG_HEAD)) * (1.0 / SUBLANES)]

    do, dg, dgnw = _rowk(readout_bwd, name="hg_readout_bwd", n=n, tm=dims["tm_row"], nctx=nctx,
                         rows=[(o_f, dm, 0, 0), (o_b, dm, 0, 0), (zz, dm, 4, 0), (dog, dm, 0, 0)], consts=[gnw],
                         out_rows=[(dm, BF16, dm, 0), (dm, BF16, dm, 0)], out_acc=[HG_HEAD])
    (dq0, dv0, dff, dl_f), (p_up1, p_dn1) = _hg_bwd_dir(
        zz, lb2, do, sts_f, None, d=0, name="hg_bwd_d0", dims=dims,
        xch=scatter([dwu_l1, dwd_l1.reshape(NDEV, -1, dm)]))
    (dq, dv, dfb, dl_b), (p_hgout,) = _hg_bwd_dir(
        zz, lb2, do, sts_b, (dq0, dv0), d=1, name="hg_bwd_d1", dims=dims,
        xch=scatter([dw_hgout.reshape(NDEV, -1, dm)]))
    pieces = (dq, dv, dff, dfb, dg)
    dh1 = _lin_cat(pieces, w_hgin, name="hg_in_bwd_in", tm=tm, o_dtype=BF16)
    dw_hgin = shards(jnp.concatenate([_lin_w(h1, piece, name=f"hg_in_bwd_w{p}", ta=dm, tn=dm, tkr=tkr)
                                      for p, piece in enumerate(pieces)], axis=1))
    dz, df0, dsh1_l1, dsc1_l1, dgate2_l0, dn1w_l1 = _norm_mod_bwd(dh1, z2_l0, dz, n1w[1], pat[1][1], name="norm1_bwd_l1",
                                                                  dims=dims, res=(f_l0, pat[0][5]))
    dh2, dwu_l0, dwd_l0, dcw_l0, dcb_l0, _ = ffn_bwd(0, df0, u_l0, hm_l0, h2_l0)
    dz, dymix, dsh2_l0, dsc2_l0, dgate1_l0, dn2w_l0 = _norm_mod_bwd(dh2, z1_l0, dz, n2w[0], pat[0][4], name="norm2_bwd_l0",
                                                                    dims=dims, res=(ymix0, pat[0][2]))
    dz2g = _lin(dymix, w_s5out, name="s5_out_bwd_in", trans_w=True, tm=1024, tn=dm, o_dtype=BF16)
    dw_s5out = _lin_w(z2g, dymix, name="s5_out_bwd_w", ta=dm, tn=dm, tkr=tkr)

    def gate_bwd(rv, pv, cv, il):
        sg = _sigmoid(rv[1])
        return [rv[2] * rv[0] * sg * (1.0 - sg), rv[2] * sg], [], []

    dt_glu, dzg_a = _rowk(gate_bwd, name="s5_gate_bwd", n=n, tm=dims["tm_row"], nctx=nctx,
                          rows=[(zg, dm, 0, 0), (t_glu, dm, 0, 0), (dz2g, dm, 0, 0)],
                          out_rows=[(dm, BF16, dm, 0), (dm, BF16, dm, 0)])
    dzg_b = _lin(dt_glu, w_glu, name="s5_glu_bwd_in", trans_w=True, tm=1024, tn=dm, o_dtype=BF16)
    dw_glu = _lin_w(zg, dt_glu, name="s5_glu_bwd_w", ta=dm, tn=dm, tkr=tkr)

    def gelu_bwd(rv, pv, cv, il):
        _, vjp = jax.vjp(_gelu, rv[0] + rv[1])
        return [vjp(rv[2] + rv[3])[0]], [], []

    (dy_s5,) = _rowk(gelu_bwd, name="s5_gelu_bwd", n=n, tm=dims["tm_row"], nctx=nctx,
                     rows=[(y_s5a, dm, 0, 0), (y_s5b, dm, 0, 0), (dzg_a, dm, 0, 0), (dzg_b, dm, 0, 0)],
                     out_rows=[(dm, F32, dm, 0)])
    dcw_both = jnp.stack([dcw_l0, dcw_l1], axis=1)
    s5g, (p_up0, p_dn0, p_cw, p_s5out, p_glu, p_hgin) = _s5_bwd(
        u_s5, dy_s5, *s5p["af"], *s5p["ab"], s5p["bb"], s5p["cc"], dsk, st_r, st_i, name="s5_bwd", dims=dims,
        xch=scatter([dwu_l0, dwd_l0.reshape(NDEV, -1, dm), dcw_both, dw_s5out.reshape(NDEV, -1, dm),
                     dw_glu.reshape(NDEV, -1, dm), dw_hgin]))
    (du_s5,) = _rowk(lambda rv, pv, cv, il: ([rv[0] + rv[1]], [], []), name="s5_du", n=n, tm=dims["tm_row"], nctx=nctx,
                     rows=[(s5g[0], dm, 0, 0), (s5g[1], dm, 0, 0)], out_rows=[(dm, BF16, dm, 0)])
    ddsk = s5g[14]
    dh0 = _lin(du_s5, w_s5in, name="s5_in_bwd_in", trans_w=True, tm=1024, tn=dm, o_dtype=BF16)
    dw_s5in = _lin_w(h0, du_s5, name="s5_in_bwd_w", ta=dm, tn=dm, tkr=tkr)
    dz0, dsh1_l0, dsc1_l0, dn1w_l0 = _norm_mod_bwd(dh0, z0, dz, n1w[0], pat[0][1], name="norm1_bwd_l0", dims=dims)

    dlr, dli, dls, g_b_re, g_b_im, g_c_re, g_c_im = _s5_param_grads(s5p, s5g[2:6], s5g[6:10], s5g[10:14])

    dmod = jnp.stack([
        jnp.concatenate([dsh1_l0, dsc1_l0, dgate1_l0, dsh2_l0, dsc2_l0, dgate2_l0], axis=1),
        jnp.concatenate([dsh1_l1, dsc1_l1, dgate1_l1, dsh2_l1, dsc2_l1, dgate2_l1], axis=1)])
    dl_hg = jnp.stack([dl_f[:, 0], dl_b[:, 0]])
    wide = lambda g: g.reshape(-1, dm)
    small = [("norm1_w", jnp.concatenate([dn1w_l0, dn1w_l1])),
             ("norm2_w", jnp.concatenate([dn2w_l0, dn2w_l1])), ("final_norm_w", dfinal_w),
             ("s5_lam_re", dlr.reshape(-1, nstate)), ("s5_lam_im", dli.reshape(-1, nstate)),
             ("s5_log_step", dls.reshape(2, ngrp)),
             ("s5_b_re", wide(g_b_re.astype(BF16))), ("s5_b_im", wide(g_b_im.astype(BF16))),
             ("s5_c_re", wide(g_c_re.astype(BF16))), ("s5_c_im", wide(g_c_im.astype(BF16))), ("s5_d", ddsk),
             ("hg_gnorm_w", dgnw), ("ffn_conv_b", jnp.stack([dcb_l0.reshape(-1), dcb_l1.reshape(-1)]))]
    tail = _exchange([dmod, dw_s5in.reshape(NDEV, -1, dm)] + [g for _, g in small] + [wide(dl_hg), loss_part],
                     a2a=[False, True] + [False] * (len(small) + 2), name="gather_tail")
    dmod_g, p_s5in, gathered = tail[0], tail[1], tail[2:]
    dlat_full = jnp.transpose(dmod_g[:, :, :bsz], (1, 0, 2, 3)).reshape(2, NDEV * bsz, N_MOD * dm)
    dctx_full = dmod_g[:, :, bsz]
    dlat_sh = lax.dynamic_slice(dlat_full, (0, 0, me * nsm), (2, NDEV * bsz, nsm))
    dctx_sh = lax.dynamic_slice(dctx_full, (0, 0, me * nsm), (NDEV, 2, nsm))
    g_w_mod, g_b_mod, dcctx8 = _mod_bwd(craw, w_mod, dlat_sh, dctx_sh, dlat_full, dctx_full)

    (g_cctx,) = _exchange([wide(dcctx8[:1])], a2a=False, name="gather_cctx")
    small, gathered = [("c_ctx", wide(dcctx8[:1]))] + small, [g_cctx] + gathered
    res = {}
    for (k, g), parts in zip(small, gathered):
        w2, m2, v2 = (given[p + k].reshape(g.shape) for p in ("", "m_", "v_"))
        res[k] = tuple(o.reshape(given[k].shape) for o in _adamw(parts, w2, m2, v2, name="adamw_" + k))

    def total(parts, name):
        z = jnp.zeros(parts.shape[1:], F32)
        return _adamw(parts, z, z, z, name=name)[0]

    loss = jnp.sum(total(gathered[-1], "sum_loss"))
    dl_tot = total(gathered[-2], "sum_dlb").reshape(dl_hg.shape)
    nlb = hg_lower_bounds.shape[2]
    g_lb = lax.dynamic_slice(dl_tot, (0, 0, me * nlb), (2, 2, nlb))

    def adam_local(name, g, shape2):
        w, m, v = given[name], given["m_" + name], given["v_" + name]
        out = _adamw(g.reshape((1,) + shape2), w.reshape(shape2), m.reshape(shape2), v.reshape(shape2),
                     name="adamw_" + name)
        return tuple(o.reshape(w.shape) for o in out)

    def adam_parts(name, p):
        w, m, v = given[name], given["m_" + name], given["v_" + name]
        shape2 = (p.shape[0], -1, w.shape[-1])
        p3 = p.reshape(shape2)
        s2 = p3.shape[1:]
        out = _adamw(p3, w.reshape(s2), m.reshape(s2), v.reshape(s2), name="adamw_" + name)
        return tuple(o.reshape(w.shape) for o in out)

    res["hg_lower_bounds"] = adam_local("hg_lower_bounds", g_lb, (2 * 2, nlb))
    res["w_mod"] = adam_local("w_mod", g_w_mod, (2 * dm, nsm))
    res["b_mod"] = adam_local("b_mod", g_b_mod, (2, N_MOD * dm))
    res["s5_w_in"] = adam_parts("s5_w_in", p_s5in)
    res["s5_w_glu"] = adam_parts("s5_w_glu", p_glu)
    res["s5_w_out"] = adam_parts("s5_w_out", p_s5out)
    res["hg_w_in"] = adam_parts("hg_w_in", p_hgin)
    res["hg_w_out"] = adam_parts("hg_w_out", p_hgout)
    res["ffn_w_up"] = adam_parts("ffn_w_up", jnp.stack([p_up0, p_up1], axis=1))
    res["ffn_w_down"] = adam_parts("ffn_w_down", jnp.stack([p_dn0, p_dn1], axis=1))
    res["ffn_conv_w"] = adam_parts("ffn_conv_w", p_cw)

    grad_x = jnp.transpose(dz0[nctx:].reshape(lx, bsz, dm), (1, 0, 2))
    order = ["c_ctx", "w_mod", "b_mod", "norm1_w", "norm2_w", "final_norm_w", "s5_w_in", "s5_lam_re", "s5_lam_im",
             "s5_log_step", "s5_b_re", "s5_b_im", "s5_c_re", "s5_c_im", "s5_d", "s5_w_glu", "s5_w_out", "hg_w_in",
             "hg_lower_bounds", "hg_gnorm_w", "hg_w_out", "ffn_w_up", "ffn_conv_w", "ffn_conv_b", "ffn_w_down"]
    outs = [loss, grad_x]
    for j in range(4):
        outs += [res[k][j].reshape(given[k].shape) for k in order]
    return tuple(outs)
```

```python
import functools

import jax
import jax.numpy as jnp
from jax import lax
from jax.experimental import pallas as pl
from jax.experimental.pallas import tpu as pltpu

F32 = jnp.float32
BF16 = jnp.bfloat16
NDEV = 8
LOCAL_B = 4
NORM_EPS = 1e-6
N_MOD = 6
S5_GROUP = 16
S5_STATE = 64
S5_LAM_RE_MAX = -1e-4
HG_HEAD = 128
HG_ROWS = 128
GRID_W = 64
ADAM_LR, ADAM_B1, ADAM_B2, ADAM_EPS, ADAM_WD, ADAM_STEP = 0.001, 0.9, 0.999, 1e-08, 0.01, 10
VMEM_BYTES_V7X = 64 * 1024 * 1024
LANES = 128
SUBLANES = 8

NN = (((1,), (0,)), ((), ()))
NT = (((1,), (1,)), ((), ()))
TN = (((0,), (0,)), ((), ()))
MESH = pl.DeviceIdType.MESH


def _params(sem=None, vmem=None):
    kw = {}
    if sem is not None:
        kw["dimension_semantics"] = sem
    if vmem is not None:
        kw["vmem_limit_bytes"] = int(min(vmem, VMEM_BYTES_V7X - (4 << 20)))
    return pltpu.CompilerParams(**kw)


def _nbytes(shape, dtype):
    n = 1
    for s in shape:
        n *= 1 if s is None else s
    return n * jnp.dtype(dtype).itemsize


def _dot(a, b, dims=NN, precision=None):
    return lax.dot_general(a, b, dims, preferred_element_type=F32, precision=precision)


def _sigmoid(x):
    return 1.0 / (1.0 + jnp.exp(-x))


class _Xchg:
    def __init__(self, arrs, a2a):
        self.arrs, self.a2a, self.n = list(arrs), list(a2a), len(arrs)

    def out_shape(self):
        return [jax.ShapeDtypeStruct(a.shape if f else (NDEV,) + a.shape, a.dtype) for a, f in zip(self.arrs, self.a2a)]

    def scratch(self):
        return [pltpu.SemaphoreType.DMA((self.n * (NDEV - 1),)), pltpu.SemaphoreType.DMA((self.n * (NDEV - 1),)),
                pltpu.SemaphoreType.DMA((self.n,))]

    def _copies(self, ins, outs, sems, with_recvs):
        send_sems, recv_sems, loc_sems = sems
        x, y, c = lax.axis_index("x"), lax.axis_index("y"), lax.axis_index("c")
        me = 4 * x + 2 * y + c
        local, sends, recvs = [], [], []
        for a in range(self.n):
            src = ins[a].at[me] if self.a2a[a] else ins[a]
            local.append(pltpu.make_async_copy(src, outs[a].at[me], loc_sems.at[a]))
            for k in range(1, NDEV):
                px = (1 - x) if (k >> 2) & 1 else x
                py = (1 - y) if (k >> 1) & 1 else y
                pc = (1 - c) if k & 1 else c
                p = 4 * px + 2 * py + pc
                s = a * (NDEV - 1) + k - 1
                src = ins[a].at[p] if self.a2a[a] else ins[a]
                kw = dict(src_ref=src, send_sem=send_sems.at[s], recv_sem=recv_sems.at[s], device_id=(px, py, pc),
                          device_id_type=MESH)
                sends.append(pltpu.make_async_remote_copy(dst_ref=outs[a].at[me], **kw))
                if with_recvs:
                    recvs.append(pltpu.make_async_remote_copy(dst_ref=outs[a].at[p], **kw))
        return local, sends, recvs

    def start(self, ins, outs, sems):
        local, sends, _ = self._copies(ins, outs, sems, False)
        for cp in local + sends:
            cp.start()

    def wait(self, ins, outs, sems):
        local, sends, recvs = self._copies(ins, outs, sems, True)
        for cp in sends:
            cp.wait_send()
        for cp in recvs:
            cp.wait_recv()
        for cp in local:
            cp.wait()


def _exchange(arrs, *, a2a, name):
    xch = _Xchg(arrs, a2a if isinstance(a2a, (list, tuple)) else [a2a] * len(arrs))
    n = xch.n

    def body(*refs):
        xch.start(refs[:n], refs[n:2 * n], refs[2 * n:])
        xch.wait(refs[:n], refs[n:2 * n], refs[2 * n:])

    res = pl.pallas_call(
        body, name=name, out_shape=xch.out_shape(),
        in_specs=[pl.BlockSpec(memory_space=pl.ANY)] * n, out_specs=[pl.BlockSpec(memory_space=pl.ANY)] * n,
        scratch_shapes=xch.scratch(),
    )(*arrs)
    return list(res)


def _call(body, *, name, out_shape, grid, in_specs, out_specs, scratch, params, args, xch=None):
    in_specs, out_specs, out_shape, scratch, args = list(in_specs), list(out_specs), list(out_shape), list(scratch), list(args)
    n_in, n_out, n_scr = len(in_specs), len(out_shape), len(scratch)
    if xch is not None:
        k = xch.n
        inner = body

        def body(*refs):
            ins, xin = refs[:n_in], refs[n_in:n_in + k]
            outs, xout = refs[n_in + k:n_in + k + n_out], refs[n_in + k + n_out:n_in + 2 * k + n_out]
            scr = refs[n_in + 2 * k + n_out:n_in + 2 * k + n_out + n_scr]
            sems = refs[n_in + 2 * k + n_out + n_scr:]
            first = pl.program_id(0) == 0
            last = pl.program_id(0) == grid[0] - 1
            for ax in range(1, len(grid)):
                first = jnp.logical_and(first, pl.program_id(ax) == 0)
                last = jnp.logical_and(last, pl.program_id(ax) == grid[ax] - 1)

            @pl.when(first)
            def _():
                xch.start(xin, xout, sems)

            inner(*ins, *outs, *scr)

            @pl.when(last)
            def _():
                xch.wait(xin, xout, sems)

        anyspec = pl.BlockSpec(memory_space=pl.ANY)
        in_specs += [anyspec] * k
        out_specs += [anyspec] * k
        out_shape += xch.out_shape()
        scratch += xch.scratch()
        args += xch.arrs
    res = pl.pallas_call(body, name=name, out_shape=out_shape, grid=grid, in_specs=in_specs, out_specs=out_specs,
                         scratch_shapes=scratch, compiler_params=params)(*args)
    return list(res[:n_out]), list(res[n_out:])


def _mm(a, b, *, name, grid, a_spec, b_spec, o_spec, o_shape, o_dtype, dims, base=None):
    nk = grid[2]
    o_block = tuple(s for s in o_spec.block_shape if s is not None)

    def body(a_ref, b_ref, *rest):
        base_ref = rest[0] if base is not None else None
        o_ref, scr = rest[1 if base is not None else 0], rest[2 if base is not None else 1:]
        r = _dot(a_ref[...].astype(BF16), b_ref[...].astype(BF16), dims)
        if nk == 1:
            if base is not None:
                r = r + base_ref[...].astype(F32)
            o_ref[...] = r.astype(o_dtype)
        else:
            acc = scr[0]
            k = pl.program_id(2)

            @pl.when(k == 0)
            def _():
                acc[...] = r

            @pl.when(k > 0)
            def _():
                acc[...] += r

            @pl.when(k == nk - 1)
            def _():
                tot = acc[...] if base is None else acc[...] + base_ref[...].astype(F32)
                o_ref[...] = tot.astype(o_dtype)

    blocks = (_nbytes(a_spec.block_shape, a.dtype) + _nbytes(b_spec.block_shape, b.dtype) + _nbytes(o_block, o_dtype)
              + (_nbytes(o_block, base.dtype) if base is not None else 0))
    scratch = [pltpu.VMEM(o_block, F32)] if nk > 1 else []
    vmem = 2 * blocks + 3 * _nbytes(o_block, F32) + (8 << 20)
    return pl.pallas_call(
        body, name=name, out_shape=jax.ShapeDtypeStruct(o_shape, o_dtype), grid=grid,
        in_specs=[a_spec, b_spec] + ([o_spec] if base is not None else []), out_specs=o_spec, scratch_shapes=scratch,
        compiler_params=_params(("parallel", "parallel", "arbitrary"), vmem),
    )(a, b, *([base] if base is not None else []))


def _lin(a, w, *, name, trans_w=False, tm, tn, o_dtype=F32, kblk=0, base=None):
    m, kk = a.shape
    nout = w.shape[0] if trans_w else w.shape[1]
    if trans_w:
        b_spec = pl.BlockSpec((tn, kk), lambda j, i, k: (j, kblk))
    else:
        b_spec = pl.BlockSpec((kk, tn), lambda j, i, k: (0, j))
    return _mm(a, w, name=name, grid=(nout // tn, m // tm, 1), dims=NT if trans_w else NN, o_shape=(m, nout),
               o_dtype=o_dtype, o_spec=pl.BlockSpec((tm, tn), lambda j, i, k: (i, j)),
               a_spec=pl.BlockSpec((tm, kk), lambda j, i, k: (i, 0)), b_spec=b_spec, base=base)


def _lin_cat(pieces, w, *, name, tm, o_dtype=F32):
    m, kk = pieces[0].shape
    nout, npc = w.shape[0], len(pieces)

    def body(*refs):
        w_ref, o_ref = refs[npc], refs[npc + 1]
        acc = _dot(refs[0][...], w_ref[:, 0:kk], NT)
        for p in range(1, npc):
            acc = acc + _dot(refs[p][...], w_ref[:, p * kk:(p + 1) * kk], NT)
        o_ref[...] = acc.astype(o_dtype)

    row = pl.BlockSpec((tm, kk), lambda i: (i, 0))
    vmem = 2 * (npc * _nbytes((tm, kk), pieces[0].dtype) + _nbytes(w.shape, w.dtype)) + 4 * _nbytes((tm, nout), F32) + (8 << 20)
    return pl.pallas_call(
        body, name=name, out_shape=jax.ShapeDtypeStruct((m, nout), o_dtype), grid=(m // tm,),
        in_specs=[row] * npc + [pl.BlockSpec(w.shape, lambda i: (0, 0))], out_specs=pl.BlockSpec((tm, nout), lambda i: (i, 0)),
        compiler_params=_params(("parallel",), vmem),
    )(*pieces, w)


def _lin_w(a, dy, *, name, ta, tn, tkr):
    m, ka = a.shape
    nout = dy.shape[1]
    return _mm(a, dy, name=name, grid=(ka // ta, nout // tn, m // tkr), dims=TN, o_shape=(ka, nout), o_dtype=BF16,
               o_spec=pl.BlockSpec((ta, tn), lambda i, j, k: (i, j)),
               a_spec=pl.BlockSpec((tkr, ta), lambda i, j, k: (k, i)),
               b_spec=pl.BlockSpec((tkr, tn), lambda i, j, k: (k, j)))


def _rowk(fn, *, name, n, tm, nctx, rows=(), pats=(), consts=(), out_rows=(), out_seg=(), out_acc=()):
    nb, ncb = n // tm, nctx // tm
    nr, npat, ncst = len(rows), len(pats), len(consts)
    no, nseg, nacc = len(out_rows), len(out_seg), len(out_acc)
    in_specs, blocks = [], 0
    for arr, w, cb, off in rows:
        in_specs.append(pl.BlockSpec((tm, w), lambda i, cb=cb, off=off: (jnp.maximum(i - off, 0), cb)))
        blocks += _nbytes((tm, w), arr.dtype)
    for p in pats:
        in_specs.append(pl.BlockSpec((None, SUBLANES, p.shape[2]), lambda i: (jnp.where(i >= ncb, 1, 0), 0, 0)))
    for cst in consts:
        in_specs.append(pl.BlockSpec(cst.shape, lambda i: (0, 0)))
    out_shape, out_specs = [], []
    for wt, dt, w, cb in out_rows:
        out_shape.append(jax.ShapeDtypeStruct((n, wt), dt))
        out_specs.append(pl.BlockSpec((tm, w), lambda i, cb=cb: (i, cb)))
        blocks += _nbytes((tm, w), dt)
    for w in out_seg:
        out_shape.append(jax.ShapeDtypeStruct((SUBLANES, w), F32))
        out_specs.append(pl.BlockSpec((SUBLANES, w), lambda i: (0, 0)))
    for w in out_acc:
        out_shape.append(jax.ShapeDtypeStruct((1, w), F32))
        out_specs.append(pl.BlockSpec((1, w), lambda i: (0, 0)))
    scratch = [pltpu.VMEM((2, SUBLANES, w), F32) for w in out_seg] + [pltpu.VMEM((SUBLANES, w), F32) for w in out_acc]

    def body(*refs):
        r_in = refs[:nr]
        p_in = refs[nr:nr + npat]
        c_in = refs[nr + npat:nr + npat + ncst]
        base = nr + npat + ncst
        o_rows = refs[base:base + no]
        o_seg = refs[base + no:base + no + nseg]
        o_acc = refs[base + no + nseg:base + no + nseg + nacc]
        s_seg = refs[base + no + nseg + nacc:base + no + nseg + nacc + nseg]
        s_acc = refs[base + no + nseg + nacc + nseg:]
        i = pl.program_id(0)
        rv = [r[...].astype(F32).reshape(tm // SUBLANES, SUBLANES, r.shape[1]) for r in r_in]
        pv = [p[...] for p in p_in]
        cv = [c[...] for c in c_in]
        is_lat = (i >= ncb).astype(F32)
        ro, so, ao = fn(rv, pv, cv, is_lat)
        for ref, val in zip(o_rows, ro):
            ref[...] = val.reshape(tm, ref.shape[1]).astype(ref.dtype)
        if nseg or nacc:
            @pl.when(i == 0)
            def _():
                for s in list(s_seg) + list(s_acc):
                    s[...] = jnp.zeros(s.shape, F32)

            seg = jnp.where(i >= ncb, 1, 0)
            for s, val in zip(s_seg, so):
                s[seg] = s[seg] + val
            for s, val in zip(s_acc, ao):
                s[...] = s[...] + val

            @pl.when(i == nb - 1)
            def _():
                for o, s in zip(o_seg, s_seg):
                    lat, ctx = s[1], s[0]
                    row = lax.broadcasted_iota(jnp.int32, lat.shape, 0)
                    lat = lat + pltpu.roll(lat, 4, 0)
                    ctx = jnp.broadcast_to(jnp.sum(ctx, axis=0, keepdims=True), lat.shape)
                    o[...] = jnp.where(row < 4, lat, jnp.where(row == 4, ctx, 0.0))
                for o, s in zip(o_acc, s_acc):
                    o[...] = jnp.sum(s[...], axis=0, keepdims=True)

    vmem = 2 * blocks + 8 * tm * 1024 * 4 + (8 << 20)
    res = pl.pallas_call(
        body, name=name, out_shape=out_shape, grid=(nb,), in_specs=in_specs, out_specs=out_specs,
        scratch_shapes=scratch, compiler_params=_params(("arbitrary",), vmem),
    )(*[r[0] for r in rows], *pats, *consts)
    return list(res)


def _rms(z):
    return lax.rsqrt(jnp.mean(z * z, axis=-1, keepdims=True) + NORM_EPS)


def _norm_mod_fwd(z, w, sh, sc, *, name, dims, res=None):
    n, d = z.shape

    def fn(rv, pv, cv, is_lat):
        zz = rv[0]
        if res is not None:
            zz = zz + pv[2][None] * rv[1]
        h = (zz * _rms(zz) * cv[0]) * (1.0 + pv[1][None]) + pv[0][None]
        return ([zz, h] if res is not None else [h]), [], []

    rows = [(z, d, 0, 0)] + ([(res[0], d, 0, 0)] if res is not None else [])
    pats = [sh, sc] + ([res[1]] if res is not None else [])
    outs = ([(d, F32, d, 0)] if res is not None else []) + [(d, BF16, d, 0)]
    out = _rowk(fn, name=name, n=n, tm=dims["tm_row"], nctx=dims["nctx"], rows=rows, pats=pats, consts=[w],
                out_rows=outs)
    return (out[0], out[1]) if res is not None else (None, out[0])


def _norm_core_bwd(zin, dh, w, sc):
    r = _rms(zin)
    xh = zin * r
    dsh = jnp.sum(dh, axis=0)
    dsc = jnp.sum(dh * (xh * w), axis=0)
    dyv = dh * (1.0 + sc[None])
    dw = jnp.sum(dyv * xh, axis=0)
    dxh = dyv * w
    dx = r * (dxh - xh * jnp.mean(dxh * xh, axis=-1, keepdims=True))
    return dx, dsh, dsc, dw


def _norm_mod_bwd(dh, zin, dz_up, w, sc, *, name, dims, res=None):
    n, d = zin.shape

    def fn(rv, pv, cv, is_lat):
        dx, dsh, dsc, dw = _norm_core_bwd(rv[1], rv[0], cv[0], pv[0])
        dz = rv[2] + dx
        if res is None:
            return [dz], [dsh, dsc], [dw]
        return [dz, dz * pv[1][None]], [dsh, dsc, jnp.sum(dz * rv[3], axis=0)], [dw]

    rows = [(dh, d, 0, 0), (zin, d, 0, 0), (dz_up, d, 0, 0)] + ([(res[0], d, 0, 0)] if res is not None else [])
    pats = [sc] + ([res[1]] if res is not None else [])
    outs = [(d, F32, d, 0)] + ([(d, BF16, d, 0)] if res is not None else [])
    return _rowk(fn, name=name, n=n, tm=dims["tm_row"], nctx=dims["nctx"], rows=rows, pats=pats, consts=[w],
                 out_rows=outs, out_seg=[d] * (3 if res is not None else 2), out_acc=[d])


def _loss_bwd(z1, f, gate, tgt, w, *, name, dims):
    n, d = z1.shape

    def fn(rv, pv, cv, is_lat):
        z2 = rv[0] + pv[0][None] * rv[1]
        r = _rms(z2)
        xh = z2 * r
        err = (xh * cv[0] - rv[2]) * is_lat
        dout = err * (1.0 / d)
        dxh = dout * cv[0]
        dz = r * (dxh - xh * jnp.mean(dxh * xh, axis=-1, keepdims=True))
        return ([dz, dz * pv[0][None]], [jnp.sum(dz * rv[1], axis=0)],
                [jnp.sum(0.5 * err * err * (1.0 / d), axis=0), jnp.sum(dout * xh, axis=0)])

    tm = dims["tm_row"]
    rows = [(z1, d, 0, 0), (f, d, 0, 0), (tgt, d, 0, dims["nctx"] // tm)]
    return _rowk(fn, name=name, n=n, tm=tm, nctx=dims["nctx"], rows=rows, pats=[gate], consts=[w],
                 out_rows=[(d, F32, d, 0), (d, BF16, d, 0)], out_seg=[d], out_acc=[d, d])


def _gelu(y):
    return jax.nn.gelu(y, approximate=True)


def _conv_masks(tb, i):
    tok = lax.broadcasted_iota(jnp.int32, (tb, 1), 0) >> 2
    last = jnp.where(i == 0, tb // LOCAL_B - 1, GRID_W - 1)
    wpos = tok & last
    return wpos == 0, wpos == last


CONV_LANES = 2 * LANES


def _conv_taps(u_ref, cw_ref, cb_ref, no_left, no_right, tb):
    uu = u_ref[...].astype(F32)
    ul = jnp.where(no_left, 0.0, pltpu.roll(uu, LOCAL_B, 0))
    ur = jnp.where(no_right, 0.0, pltpu.roll(uu, tb - LOCAL_B, 0))
    val = cb_ref[...] + ul * cw_ref[pl.ds(0, 1), :] + uu * cw_ref[pl.ds(1, 1), :] + ur * cw_ref[pl.ds(2, 1), :]
    return val, ul, uu, ur


def _convffn_specs(tb, nj):
    cl = CONV_LANES
    return [pl.BlockSpec((tb, cl), lambda j, i: (i, j)), pl.BlockSpec((tb, cl), lambda j, i: (i, nj + j)),
            pl.BlockSpec((3, cl), lambda j, i: (0, j)), pl.BlockSpec((3, cl), lambda j, i: (0, nj + j)),
            pl.BlockSpec((1, cl), lambda j, i: (0, j)), pl.BlockSpec((1, cl), lambda j, i: (0, nj + j))]


def _convffn_fwd(u, cw, cb, *, name, dims):
    n, f2 = u.shape
    tb, nj = dims["nctx"], f2 // 2 // CONV_LANES

    def body(ua_ref, ug_ref, cwa_ref, cwg_ref, cba_ref, cbg_ref, o_ref):
        no_left, no_right = _conv_masks(tb, pl.program_id(1))
        a = _conv_taps(ua_ref, cwa_ref, cba_ref, no_left, no_right, tb)[0]
        g = _conv_taps(ug_ref, cwg_ref, cbg_ref, no_left, no_right, tb)[0]
        o_ref[...] = (a * _sigmoid(a) * g).astype(BF16)

    vmem = 16 * tb * CONV_LANES * 4 + (8 << 20)
    return pl.pallas_call(
        body, name=name, out_shape=jax.ShapeDtypeStruct((n, f2 // 2), BF16), grid=(nj, n // tb),
        in_specs=_convffn_specs(tb, nj), out_specs=pl.BlockSpec((tb, CONV_LANES), lambda j, i: (i, j)),
        compiler_params=_params(("parallel", "arbitrary"), vmem),
    )(u, u, cw, cw, cb, cb)


def _convffn_bwd(u, dhm, cw, cb, *, name, dims, xch=None):
    n, f2 = u.shape
    tb, nj = dims["nctx"], f2 // 2 // CONV_LANES

    def body(ua_ref, ug_ref, cwa_ref, cwg_ref, cba_ref, cbg_ref, dh_ref, dua_ref, dug_ref, dcwa_ref, dcwg_ref, dcba_ref,
             dcbg_ref):
        i = pl.program_id(1)
        no_left, no_right = _conv_masks(tb, i)

        @pl.when(i == 0)
        def _():
            for ref in (dcwa_ref, dcwg_ref, dcba_ref, dcbg_ref):
                ref[...] = jnp.zeros(ref.shape, F32)

        a, al, ac, ar = _conv_taps(ua_ref, cwa_ref, cba_ref, no_left, no_right, tb)
        g, gl, gc, gr = _conv_taps(ug_ref, cwg_ref, cbg_ref, no_left, no_right, tb)
        dh = dh_ref[...].astype(F32)
        sa = _sigmoid(a)
        dg = dh * (a * sa)
        da = dh * g * (sa * (1.0 + a * (1.0 - sa)))
        for dc, (tl, tc, tr), cw_ref, du_ref, dcw_ref, dcb_ref in (
                (da, (al, ac, ar), cwa_ref, dua_ref, dcwa_ref, dcba_ref),
                (dg, (gl, gc, gr), cwg_ref, dug_ref, dcwg_ref, dcbg_ref)):
            dcb_ref[...] += jnp.sum(dc, axis=0, keepdims=True)
            dcw_ref[pl.ds(0, 1), :] += jnp.sum(dc * tl, axis=0, keepdims=True)
            dcw_ref[pl.ds(1, 1), :] += jnp.sum(dc * tc, axis=0, keepdims=True)
            dcw_ref[pl.ds(2, 1), :] += jnp.sum(dc * tr, axis=0, keepdims=True)
            du = (dc * cw_ref[pl.ds(1, 1), :]
                  + pltpu.roll(jnp.where(no_left, 0.0, dc) * cw_ref[pl.ds(0, 1), :], tb - LOCAL_B, 0)
                  + pltpu.roll(jnp.where(no_right, 0.0, dc) * cw_ref[pl.ds(2, 1), :], LOCAL_B, 0))
            du_ref[...] = du.astype(BF16)

    cl, f = CONV_LANES, f2 // 2
    sd = jax.ShapeDtypeStruct
    row = pl.BlockSpec((tb, cl), lambda j, i: (i, j))
    vmem = 24 * tb * cl * 4 + (8 << 20)
    return _call(
        body, name=name, xch=xch, args=[u, u, cw, cw, cb, cb, dhm], scratch=[],
        out_shape=[sd((n, f), BF16), sd((n, f), BF16), sd((3, f), F32), sd((3, f), F32), sd((1, f), F32), sd((1, f), F32)],
        grid=(nj, n // tb), in_specs=_convffn_specs(tb, nj) + [row],
        out_specs=[row, row, pl.BlockSpec((3, cl), lambda j, i: (0, j)), pl.BlockSpec((3, cl), lambda j, i: (0, j)),
                   pl.BlockSpec((1, cl), lambda j, i: (0, j)), pl.BlockSpec((1, cl), lambda j, i: (0, j))],
        params=_params(("arbitrary", "arbitrary"), vmem))


def _s5_disc(lr, li, ls, brt, bit):
    lr = jnp.minimum(lr, S5_LAM_RE_MAX)
    dt = jnp.exp(ls)
    mag = jnp.exp(lr * dt)
    ar = mag * jnp.cos(li * dt)
    ai = mag * jnp.sin(li * dt)
    den = lr * lr + li * li
    nr = ar - 1.0
    cr = (nr * lr + ai * li) / den
    ci = (ai * lr - nr * li) / den
    return ar, ai, cr * brt - ci * bit, cr * bit + ci * brt


def _s5_disc_fwd(lr, li, ls, brt, bit):
    def body(lr_ref, li_ref, ls_ref, br_ref, bi_ref, ar_ref, ai_ref, bbr_ref, bbi_ref):
        ar, ai, bbr, bbi = _s5_disc(lr_ref[...], li_ref[...], ls_ref[...], br_ref[...], bi_ref[...])
        ar_ref[...] = ar
        ai_ref[...] = ai
        bbr_ref[...] = bbr
        bbi_ref[...] = bbi

    sd = jax.ShapeDtypeStruct
    return pl.pallas_call(body, name="s5_disc_fwd",
                          out_shape=[sd(lr.shape, F32), sd(lr.shape, F32), sd(brt.shape, F32), sd(brt.shape, F32)],
                          compiler_params=_params(None, 32 << 20))(lr, li, ls, brt, bit)


def _s5_disc_bwd(lr, li, ls, brt, bit, dar, dai, dbbr, dbbi):
    def body(lr_ref, li_ref, ls_ref, br_ref, bi_ref, dar_ref, dai_ref, dbbr_ref, dbbi_ref,
             dlr_ref, dli_ref, dls_ref, dbr_ref, dbi_ref):
        _, vjp = jax.vjp(_s5_disc, lr_ref[...], li_ref[...], ls_ref[...], br_ref[...], bi_ref[...])
        dlr, dli, dls, dbr, dbi = vjp((dar_ref[...], dai_ref[...], dbbr_ref[...], dbbi_ref[...]))
        dlr_ref[...] = dlr
        dli_ref[...] = dli
        dls_ref[...] = dls
        dbr_ref[...] = dbr
        dbi_ref[...] = dbi

    sd = jax.ShapeDtypeStruct
    return pl.pallas_call(body, name="s5_disc_bwd",
                          out_shape=[sd(lr.shape, F32), sd(lr.shape, F32), sd(ls.shape, F32), sd(brt.shape, F32),
                                     sd(brt.shape, F32)],
                          compiler_params=_params(None, 48 << 20))(lr, li, ls, brt, bit, dar, dai, dbbr, dbbi)


def _cmul(ar, ai, xr, xi):
    return ar * xr - ai * xi, ar * xi + ai * xr


def _s5_chunk_of(step, ncc, nc, rev):
    if not rev:
        return step
    return jnp.where(step < ncc, ncc - 1 - step, nc - 1 - (step - ncc))


def _s5_scan2(asc, desc, row0, nrows, a_r_ref, a_i_ref, cr_ref, ci_ref, *, lane_block, extra=None):
    width = asc[0].shape[1]
    nt = nrows // SUBLANES
    for lb in range(width // lane_block):
        lanes = pl.ds(lb * lane_block, lane_block)
        a1r, a1i = a_r_ref[:, lanes], a_i_ref[:, lanes]
        a2r, a2i = pltpu.roll(a1r, 4, 0), pltpu.roll(a1i, 4, 0)
        lo = lax.broadcasted_iota(jnp.int32, a1r.shape, 0) < 4

        def step(t, carry):
            pr, pi = carry[0], carry[1]
            ra = pl.ds(pl.multiple_of(row0 + t * SUBLANES, SUBLANES), SUBLANES)
            rd = pl.ds(pl.multiple_of(row0 + (nt - 1 - t) * SUBLANES, SUBLANES), SUBLANES)
            ur, ui = asc[0][ra, lanes], asc[1][ra, lanes]
            dr, di = desc[0][rd, lanes], desc[1][rd, lanes]
            mr, mi = _cmul(a1r, a1i, pr, pi)
            y1r, y1i = jnp.where(lo, ur, dr) + mr, jnp.where(lo, ui, di) + mi
            mr, mi = _cmul(a2r, a2i, pltpu.roll(y1r, 4, 0), pltpu.roll(y1i, 4, 0))
            y2r, y2i = jnp.where(lo, dr, ur) + mr, jnp.where(lo, di, ui) + mi
            our, oui = jnp.where(lo, y1r, y2r), jnp.where(lo, y1i, y2i)
            odr, odi = jnp.where(lo, y2r, y1r), jnp.where(lo, y2i, y1i)
            asc[0][ra, lanes] = our
            asc[1][ra, lanes] = oui
            desc[0][rd, lanes] = odr
            desc[1][rd, lanes] = odi
            nxt = (pltpu.roll(y2r, 4, 0), pltpu.roll(y2i, 4, 0))
            if extra is None:
                return nxt
            return nxt + tuple(extra(t, nt - 1 - t, lanes, (our, oui), (odr, odi), carry[2:]))

        init = (cr_ref[:, lanes], ci_ref[:, lanes])
        if extra is not None:
            init = init + tuple(extra.init(lanes))
        out = lax.fori_loop(0, nt, step, init)
        cr_ref[:, lanes] = out[0]
        ci_ref[:, lanes] = out[1]
        if extra is not None:
            extra.done(lanes, out[2:])


S5_SPLIT = 2


def _s5_fwd(u, af_r, af_i, bb, cc, dsk, *, name, dims, xch=None):
    n, dm = u.shape
    nk, swk = bb[0].shape[0], bb[0].shape[2]
    rr, sw = dims["s5_rows"], nk * swk
    nkh, dmh, swh = nk // S5_SPLIT, dm // S5_SPLIT, sw // S5_SPLIT
    nc, ncc = n // rr, dims["nctx"] // rr
    c1 = lambda i: _s5_chunk_of(i, ncc, nc, True)

    def body(u0_ref, u1_ref, afr_ref, afi_ref, b0r, b0i, b1r, b1i, c0r, c0i, c1r, c1i, dsk_ref,
             y0_ref, y1_ref, str_ref, sti_ref, s0r, s0i, s1r, s1i, cr, ci):
        @pl.when(pl.program_id(1) == 0)
        def _():
            cr[...] = jnp.zeros(cr.shape, F32)
            ci[...] = jnp.zeros(ci.shape, F32)

        str_ref[...] = cr[...]
        sti_ref[...] = ci[...]
        ub0, ub1 = u0_ref[...].astype(BF16), u1_ref[...].astype(BF16)
        for k in range(nkh):
            cols, sl = slice(k * LANES, (k + 1) * LANES), slice(k * swk, (k + 1) * swk)
            s0r[:, sl] = _dot(ub0[:, cols], b0r[k])
            s0i[:, sl] = _dot(ub0[:, cols], b0i[k])
            s1r[:, sl] = _dot(ub1[:, cols], b1r[k])
            s1i[:, sl] = _dot(ub1[:, cols], b1i[k])
        _s5_scan2((s0r, s0i), (s1r, s1i), 0, rr, afr_ref, afi_ref, cr, ci, lane_block=dims["s5_lane_block"])
        for k in range(nkh):
            cols, sl = slice(k * LANES, (k + 1) * LANES), slice(k * swk, (k + 1) * swk)
            y0_ref[:, cols] = (_dot(s0r[:, sl].astype(BF16), c0r[k]) - _dot(s0i[:, sl].astype(BF16), c0i[k])
                               + dsk_ref[:, cols] * u0_ref[:, cols].astype(F32)).astype(BF16)
            y1_ref[:, cols] = (_dot(s1r[:, sl].astype(BF16), c1r[k])
                               - _dot(s1i[:, sl].astype(BF16), c1i[k])).astype(BF16)

    row0 = pl.BlockSpec((rr, dmh), lambda h, i: (i, h))
    row1 = pl.BlockSpec((rr, dmh), lambda h, i: (c1(i), h))
    tile = pl.BlockSpec((SUBLANES, swh), lambda h, i: (0, h))
    wspec = lambda a: pl.BlockSpec((nkh,) + a.shape[1:], lambda h, i: (h, 0, 0))
    st_spec = pl.BlockSpec((None, SUBLANES, swh), lambda h, i: (i, 0, h))
    sd = jax.ShapeDtypeStruct
    vmem = 4 * rr * swh * 4 + 12 * rr * dmh * 4 + 16 * nkh * LANES * swk * 2 + (12 << 20)
    return _call(
        body, name=name, xch=xch, args=[u, u, af_r, af_i, *bb, *cc, dsk],
        out_shape=[sd((n, dm), BF16), sd((n, dm), BF16), sd((nc, SUBLANES, sw), F32), sd((nc, SUBLANES, sw), F32)],
        grid=(S5_SPLIT, nc),
        in_specs=[row0, row1, tile, tile] + [wspec(a) for a in (*bb, *cc)] + [pl.BlockSpec((1, dmh), lambda h, i: (0, h))],
        out_specs=[row0, row1, st_spec, st_spec],
        scratch=[pltpu.VMEM((rr, swh), F32)] * 4 + [pltpu.VMEM((SUBLANES, swh), F32)] * 2,
        params=_params(("arbitrary", "arbitrary"), vmem))


class _DaHook2:
    def __init__(self, s0, s1, accs):
        self.s0, self.s1, self.accs = s0, s1, accs

    def init(self, lanes):
        return tuple(a[:, lanes] for a in self.accs)

    def done(self, lanes, acc):
        for a, v in zip(self.accs, acc):
            a[:, lanes] = v

    def __call__(self, t1, t0, lanes, l1, l0, acc):
        row = lax.broadcasted_iota(jnp.int32, l1[0].shape, 0)
        b1 = pl.multiple_of(SUBLANES + t1 * SUBLANES, SUBLANES)
        b0 = pl.multiple_of(SUBLANES + t0 * SUBLANES, SUBLANES)
        cur1, nxt1 = pl.ds(b1, SUBLANES), pl.ds(pl.multiple_of(b1 + SUBLANES, SUBLANES), SUBLANES)
        cur0, prv0 = pl.ds(b0, SUBLANES), pl.ds(pl.multiple_of(b0 - SUBLANES, SUBLANES), SUBLANES)
        p1r = pltpu.roll(jnp.where(row >= 4, self.s1[0][cur1, lanes], self.s1[0][nxt1, lanes]), 4, 0)
        p1i = pltpu.roll(jnp.where(row >= 4, self.s1[1][cur1, lanes], self.s1[1][nxt1, lanes]), 4, 0)
        p0r = pltpu.roll(jnp.where(row >= 4, self.s0[0][prv0, lanes], self.s0[0][cur0, lanes]), 4, 0)
        p0i = pltpu.roll(jnp.where(row >= 4, self.s0[1][prv0, lanes], self.s0[1][cur0, lanes]), 4, 0)
        return (acc[0] + p0r * l0[0] + p0i * l0[1], acc[1] + p0r * l0[1] - p0i * l0[0],
                acc[2] + p1r * l1[0] + p1i * l1[1], acc[3] + p1r * l1[1] - p1i * l1[0])


def _s5_bwd(u, dy, af_r, af_i, ab_r, ab_i, bb, cc, dsk, st_r, st_i, *, name, dims, xch=None):
    n, dm = u.shape
    nk, swk = bb[0].shape[0], bb[0].shape[2]
    rr, sw = dims["s5_rows"], nk * swk
    nkh, dmh, swh = nk // S5_SPLIT, dm // S5_SPLIT, sw // S5_SPLIT
    nc, ncc = n // rr, dims["nctx"] // rr
    f0 = lambda i: nc - 1 - i
    f1 = lambda i: _s5_chunk_of(nc - 1 - i, ncc, nc, True)

    def body(u0_ref, u1_ref, dy0_ref, dy1_ref, afr_ref, afi_ref, abr_ref, abi_ref, b0r, b0i, b1r, b1i, c0r, c0i, c1r, c1i,
             dsk_ref, str_ref, sti_ref,
             du0_ref, du1_ref, db0r, db0i, db1r, db1i, dc0r, dc0i, dc1r, dc1i, da0r_ref, da0i_ref, da1r_ref, da1i_ref, dd_ref,
             s0r, s0i, s1r, s1i, l0r, l0i, l1r, l1i, cr, ci, lcr, lci, a0r, a0i, a1r, a1i, dda):
        i = pl.program_id(1)

        @pl.when(i == 0)
        def _():
            for ref in (lcr, lci, a0r, a0i, a1r, a1i, dda, db0r, db0i, db1r, db1i, dc0r, dc0i, dc1r, dc1i):
                ref[...] = jnp.zeros(ref.shape, F32)

        row = lax.broadcasted_iota(jnp.int32, (SUBLANES, swh), 0)
        for st_ref, car, z0, z1 in ((str_ref, cr, s0r, s1r), (sti_ref, ci, s0i, s1i)):
            st = st_ref[...]
            car[...] = st
            z0[pl.ds(0, SUBLANES), :] = jnp.where(row < 4, st, pltpu.roll(st, 4, 0))
            z1[pl.ds(rr + SUBLANES, SUBLANES), :] = jnp.where(row >= 4, st, pltpu.roll(st, 4, 0))
        body_rows = pl.ds(SUBLANES, rr)
        ub0, ub1 = u0_ref[...].astype(BF16), u1_ref[...].astype(BF16)
        dyb0, dyb1 = dy0_ref[...].astype(BF16), dy1_ref[...].astype(BF16)
        for k in range(nkh):
            cols, sl = slice(k * LANES, (k + 1) * LANES), slice(k * swk, (k + 1) * swk)
            s0r[body_rows, sl] = _dot(ub0[:, cols], b0r[k])
            s0i[body_rows, sl] = _dot(ub0[:, cols], b0i[k])
            s1r[body_rows, sl] = _dot(ub1[:, cols], b1r[k])
            s1i[body_rows, sl] = _dot(ub1[:, cols], b1i[k])
        _s5_scan2((s0r, s0i), (s1r, s1i), SUBLANES, rr, afr_ref, afi_ref, cr, ci, lane_block=dims["s5_lane_block"])
        for k in range(nkh):
            cols, sl = slice(k * LANES, (k + 1) * LANES), slice(k * swk, (k + 1) * swk)
            for dyk, lr, li, sr, si, ccr, cci, dcr, dci in ((dyb0[:, cols], l0r, l0i, s0r, s0i, c0r, c0i, dc0r, dc0i),
                                                           (dyb1[:, cols], l1r, l1i, s1r, s1i, c1r, c1i, dc1r, dc1i)):
                lr[:, sl] = _dot(dyk, ccr[k], NT)
                li[:, sl] = -_dot(dyk, cci[k], NT)
                dcr[k] += _dot(dyk, sr[body_rows, sl].astype(BF16), TN)
                dci[k] -= _dot(dyk, si[body_rows, sl].astype(BF16), TN)
        _s5_scan2((l1r, l1i), (l0r, l0i), 0, rr, abr_ref, abi_ref, lcr, lci, lane_block=dims["s5_lane_block"] // 2,
                  extra=_DaHook2((s0r, s0i), (s1r, s1i), (a0r, a0i, a1r, a1i)))
        for k in range(nkh):
            cols, sl = slice(k * LANES, (k + 1) * LANES), slice(k * swk, (k + 1) * swk)
            for uk, lr, li, br, bi, dbr, dbi, du_ref, first in ((ub0[:, cols], l0r, l0i, b0r, b0i, db0r, db0i, du0_ref, True),
                                                              (ub1[:, cols], l1r, l1i, b1r, b1i, db1r, db1i, du1_ref, False)):
                lrk, lik = lr[:, sl].astype(BF16), li[:, sl].astype(BF16)
                dbr[k] += _dot(uk, lrk, TN)
                dbi[k] += _dot(uk, lik, TN)
                duk = _dot(lrk, br[k], NT) + _dot(lik, bi[k], NT)
                if first:
                    duk = duk + dsk_ref[:, cols] * dy0_ref[:, cols]
                du_ref[:, cols] = duk
        dda[...] += jnp.sum((dy0_ref[...] * u0_ref[...].astype(F32)).reshape(rr // SUBLANES, SUBLANES, dmh), axis=0)

        @pl.when(i == nc - 1)
        def _():
            for o, a in ((da0r_ref, a0r), (da0i_ref, a0i), (da1r_ref, a1r), (da1i_ref, a1i), (dd_ref, dda)):
                o[...] = jnp.sum(a[...], axis=0, keepdims=True)

    row0 = pl.BlockSpec((rr, dmh), lambda h, i: (f0(i), h))
    row1 = pl.BlockSpec((rr, dmh), lambda h, i: (f1(i), h))
    tile = pl.BlockSpec((SUBLANES, swh), lambda h, i: (0, h))
    wspec = lambda a: pl.BlockSpec((nkh,) + a.shape[1:], lambda h, i: (h, 0, 0))
    st_spec = pl.BlockSpec((None, SUBLANES, swh), lambda h, i: (f0(i), 0, h))
    vec = lambda w: pl.BlockSpec((1, w), lambda h, i: (0, h))
    sd = jax.ShapeDtypeStruct
    out_shape = ([sd((n, dm), F32)] * 2 + [sd(a.shape, F32) for a in (*bb, *bb)] + [sd((1, sw), F32)] * 4 + [sd((1, dm), F32)])
    out_specs = [row0, row1] + [wspec(a) for a in (*bb, *bb)] + [vec(swh)] * 4 + [vec(dmh)]
    scratch = ([pltpu.VMEM((rr + 2 * SUBLANES, swh), F32)] * 4 + [pltpu.VMEM((rr, swh), F32)] * 4
               + [pltpu.VMEM((SUBLANES, swh), F32)] * 8 + [pltpu.VMEM((SUBLANES, dmh), F32)])
    vmem = 8 * (rr + 16) * swh * 4 + 16 * rr * dmh * 4 + 48 * nkh * LANES * swk * 4 + (10 << 20)
    return _call(
        body, name=name, xch=xch, args=[u, u, dy, dy, af_r, af_i, ab_r, ab_i, *bb, *cc, dsk, st_r, st_i],
        out_shape=out_shape, grid=(S5_SPLIT, nc),
        in_specs=[row0, row1, row0, row1, tile, tile, tile, tile] + [wspec(a) for a in (*bb, *cc)] + [vec(dmh), st_spec, st_spec],
        out_specs=out_specs, scratch=scratch, params=_params(("arbitrary", "arbitrary"), vmem))


def _hg_mask(kind, rev):
    if kind == "tot":
        r = lax.broadcasted_iota(jnp.int32, (SUBLANES, HG_ROWS), 0)
        c = lax.broadcasted_iota(jnp.int32, (SUBLANES, HG_ROWS), 1)
        return (c & 3) == r
    r = lax.broadcasted_iota(jnp.int32, (HG_ROWS, HG_ROWS), 0)
    c = lax.broadcasted_iota(jnp.int32, (HG_ROWS, HG_ROWS), 1)
    same = (r & 3) == (c & 3)
    before = ((c >> 2) >= (r >> 2)) if rev else ((c >> 2) <= (r >> 2))
    return jnp.logical_and(same, before if kind == "cum" else jnp.logical_not(before))


def _split2(x):
    hi = x.astype(BF16)
    return hi, (x - hi.astype(F32)).astype(BF16)


@functools.partial(jax.custom_vjp, nondiff_argnums=(1, 2))
def _mask_sum(x, kind, rev):
    m = _hg_mask(kind, rev).astype(BF16)
    hi, lo = _split2(x)
    return _dot(m, hi) + _dot(m, lo)


def _mask_sum_fwd(x, kind, rev):
    return _mask_sum(x, kind, rev), None


def _mask_sum_bwd(kind, rev, _, g):
    m = _hg_mask(kind, rev).astype(BF16)
    hi, lo = _split2(g)
    return (_dot(m, hi, TN) + _dot(m, lo, TN),)


_mask_sum.defvjp(_mask_sum_fwd, _mask_sum_bwd)


def _hg_chunk(q, v, fraw, l0, l1, st, *, rev):
    nh = q.shape[1] // HG_HEAD
    lb = _sigmoid(l1 - l0)
    logf = jnp.logaddexp(jnp.log(lb), jnp.log1p(-lb) + jax.nn.log_sigmoid(fraw))
    kk = (1.0 - lb) * _sigmoid(fraw * -1.0)
    tri = _hg_mask("cum", rev)
    bcum = _mask_sum(logf, "cum", rev)
    brem = _mask_sum(logf, "rem", rev)
    bend8 = _mask_sum(logf, "tot", rev)
    r8d = lax.broadcasted_iota(jnp.int32, bend8.shape, 0)
    decs = [jnp.exp(jnp.sum(jnp.where(r8d == b, bend8, 0.0), axis=0, keepdims=True)) for b in range(LOCAL_B)]
    qd = (q * jnp.exp(bcum)).astype(BF16)
    kd = (kk * jnp.exp(-bcum)).astype(BF16)
    ke = (kk * jnp.exp(brem)).astype(BF16)
    wide = (HG_ROWS, LOCAL_B * HG_HEAD)
    mine = (lax.broadcasted_iota(jnp.int32, wide, 1) >> 7) == (lax.broadcasted_iota(jnp.int32, wide, 0) & 3)
    per_example = lambda x: jnp.where(mine, jnp.concatenate([x] * LOCAL_B, axis=1), jnp.zeros(wide, x.dtype))
    outs, new = [], []
    for h in range(nh):
        sl = slice(h * HG_HEAD, (h + 1) * HG_HEAD)
        vh = v[:, sl].astype(BF16)
        att = jnp.where(tri, _dot(qd[:, sl], kd[:, sl], NT), 0.0)
        outs.append(_dot(att.astype(BF16), vh) + _dot(per_example(qd[:, sl]), st[h].astype(BF16), NT))
        dec = jnp.concatenate([d[:, sl] for d in decs], axis=1)
        new.append(st[h] * dec + _dot(vh, per_example(ke[:, sl]), TN))
    return jnp.concatenate(outs, axis=1), tuple(new)


def _hg_chunk_of(step, ncc, nc, rev):
    return _s5_chunk_of(step, ncc, nc, rev)


def _hg_fwd_dir(zz, lb2, *, d, name, dims, xch=None):
    n = zz.shape[0]
    dm = zz.shape[1] // 5
    ns, sw = dm // HG_HEAD, LOCAL_B * HG_HEAD
    nc, ncc = n // HG_ROWS, dims["nctx"] // HG_ROWS
    rev = d == 1
    ch = lambda i: _hg_chunk_of(i, ncc, nc, rev)

    def body(q_ref, v_ref, f_ref, l0_ref, l1_ref, o_ref, st_ref, st):
        @pl.when(pl.program_id(0) == 0)
        def _():
            st[...] = jnp.zeros(st.shape, F32)

        st_ref[...] = st[...]
        o, new = _hg_chunk(q_ref[...].astype(F32), v_ref[...].astype(F32), f_ref[...].astype(F32), l0_ref[...],
                           l1_ref[...], tuple(st[j] for j in range(ns)), rev=rev)
        o_ref[...] = o.astype(BF16)
        for j in range(ns):
            st[j] = new[j]

    blk = lambda off: pl.BlockSpec((HG_ROWS, dm), lambda i, off=off: (ch(i), off))
    lspec = lambda layer: pl.BlockSpec((None, None, 1, dm), lambda i, layer=layer: (d, layer, 0, 0))
    return _call(
        body, name=name, xch=xch, args=[zz, zz, zz, lb2, lb2],
        out_shape=[jax.ShapeDtypeStruct((n, dm), BF16), jax.ShapeDtypeStruct((nc, ns, HG_HEAD, sw), F32)],
        grid=(nc,),
        in_specs=[blk(0), blk(1), blk(2 + d), lspec(0), lspec(1)],
        out_specs=[pl.BlockSpec((HG_ROWS, dm), lambda i: (ch(i), 0)),
                   pl.BlockSpec((None, ns, HG_HEAD, sw), lambda i: (ch(i), 0, 0, 0))],
        scratch=[pltpu.VMEM((ns, HG_HEAD, sw), F32)],
        params=_params(("arbitrary",), 48 << 20))


def _hg_bwd_dir(zz, lb2, do, sts, dqv_prev, *, d, name, dims, xch=None):
    n = zz.shape[0]
    dm = zz.shape[1] // 5
    ns, sw = dm // HG_HEAD, LOCAL_B * HG_HEAD
    nc, ncc = n // HG_ROWS, dims["nctx"] // HG_ROWS
    rev = d == 1
    ch = lambda i: _hg_chunk_of(nc - 1 - i, ncc, nc, rev)
    qv_dtype = F32 if d == 0 else BF16

    def body(*refs):
        q_ref, v_ref, f_ref, l0_ref, l1_ref, do_ref, st_ref = refs[:7]
        pos = 7
        if d == 1:
            dqp_ref, dvp_ref = refs[7:9]
            pos = 9
        dq_ref, dv_ref, df_ref, dl_ref, dst = refs[pos:]
        i = pl.program_id(0)

        @pl.when(i == 0)
        def _():
            dst[...] = jnp.zeros(dst.shape, F32)
            dl_ref[...] = jnp.zeros(dl_ref.shape, F32)

        _, vjp = jax.vjp(functools.partial(_hg_chunk, rev=rev), q_ref[...].astype(F32), v_ref[...].astype(F32),
                         f_ref[...].astype(F32), l0_ref[...], l1_ref[...], tuple(st_ref[j] for j in range(ns)))
        dq, dv, df, dl0, dl1, dstn = vjp((do_ref[...].astype(F32), tuple(dst[j] for j in range(ns))))
        for j in range(ns):
            dst[j] = dstn[j]
        if d == 1:
            dq = dq + dqp_ref[...]
            dv = dv + dvp_ref[...]
        dq_ref[...] = dq.astype(qv_dtype)
        dv_ref[...] = dv.astype(qv_dtype)
        df_ref[...] = df.astype(BF16)
        dl_ref[0] += dl0
        dl_ref[1] += dl1

    blk = lambda off: pl.BlockSpec((HG_ROWS, dm), lambda i, off=off: (ch(i), off))
    oblk = pl.BlockSpec((HG_ROWS, dm), lambda i: (ch(i), 0))
    lspec = lambda layer: pl.BlockSpec((None, None, 1, dm), lambda i, layer=layer: (d, layer, 0, 0))
    ins = [zz, zz, zz, lb2, lb2, do, sts] + (list(dqv_prev) if d == 1 else [])
    in_specs = [blk(0), blk(1), blk(2 + d), lspec(0), lspec(1), oblk,
                pl.BlockSpec((None, ns, HG_HEAD, sw), lambda i: (ch(i), 0, 0, 0))]
    in_specs += [oblk, oblk] if d == 1 else []
    sd = jax.ShapeDtypeStruct
    return _call(
        body, name=name, xch=xch, args=ins,
        out_shape=[sd((n, dm), qv_dtype), sd((n, dm), qv_dtype), sd((n, dm), BF16), sd((2, 1, dm), F32)],
        grid=(nc,), in_specs=in_specs,
        out_specs=[oblk, oblk, oblk, pl.BlockSpec((2, 1, dm), lambda i: (0, 0, 0))],
        scratch=[pltpu.VMEM((ns, HG_HEAD, sw), F32)],
        params=_params(("arbitrary",), 56 << 20))


def _hg_readout(o, g, w):
    outs = []
    for h in range(o.shape[-1] // HG_HEAD):
        sl = slice(h * HG_HEAD, (h + 1) * HG_HEAD)
        oh = o[..., sl]
        outs.append(oh * _rms(oh) * w * _sigmoid(g[..., sl]))
    return jnp.concatenate(outs, axis=-1)


def _silu(x):
    return x * _sigmoid(x)


def _mod_fwd(craw, w, b):
    def body(c_ref, w_ref, b_ref, o_ref):
        s = _silu(c_ref[...]).astype(BF16)
        for layer in range(w.shape[0]):
            o_ref[layer] = _dot(s, w_ref[layer].astype(BF16)) + b_ref[layer]

    return pl.pallas_call(body, name="mod_fwd",
                          out_shape=jax.ShapeDtypeStruct((w.shape[0], craw.shape[0], w.shape[2]), F32),
                          compiler_params=_params(None, 40 << 20))(craw, w, b)


def _mod_bwd(craw, w, dlat_sh, dctx_sh, dlat_full, dctx_full):
    nl, dm, ns = w.shape
    nb = dlat_sh.shape[1]

    def body(c_ref, w_ref, dl_ref, dc_ref, dlf_ref, dcf_ref, dw_ref, db_ref, dcc_ref):
        craw_v = c_ref[...]
        s = _silu(craw_v)
        s_lat = s[:nb].astype(BF16)
        s_ctx = s[nb:].astype(BF16)
        row = lax.broadcasted_iota(jnp.int32, (SUBLANES, ns), 0)
        dsc = jnp.zeros((SUBLANES, dm), F32)
        for layer in range(nl):
            tot = dc_ref[0, pl.ds(layer, 1), :]
            totf = dcf_ref[0, pl.ds(layer, 1), :]
            for i in range(1, NDEV):
                tot = tot + dc_ref[i, pl.ds(layer, 1), :]
                totf = totf + dcf_ref[i, pl.ds(layer, 1), :]
            dc8 = jnp.where(row == 0, jnp.broadcast_to(tot, (SUBLANES, ns)), 0.0).astype(BF16)
            dw_ref[layer] = _dot(s_lat, dl_ref[layer].astype(BF16), TN) + _dot(s_ctx, dc8, TN)
            db_ref[layer] = jnp.sum(dlf_ref[layer], axis=0, keepdims=True) + totf
            dsc = dsc + _dot(dc8, w_ref[layer].astype(BF16), NT)
        cc = craw_v[nb:]
        sg = _sigmoid(cc)
        dcc_ref[...] = dsc * (sg * (1.0 + cc * (1.0 - sg)))

    sd = jax.ShapeDtypeStruct
    return pl.pallas_call(body, name="mod_bwd",
                          out_shape=[sd((nl, dm, ns), F32), sd((nl, 1, dlat_full.shape[2]), F32), sd((SUBLANES, dm), F32)],
                          compiler_params=_params(None, 48 << 20))(craw, w, dlat_sh, dctx_sh, dlat_full, dctx_full)


def _adam_rows(r):
    best = None
    for t in range(2 * SUBLANES, min(r, 128) + 1, 2 * SUBLANES):
        if r % t == 0:
            best = t
    return best if best is not None else r


def _adamw(parts, w, m, v, *, name):
    npart, r, c = parts.shape
    tr = _adam_rows(r)

    def body(p_ref, w_ref, m_ref, v_ref, g_ref, d_ref, nm_ref, nv_ref):
        g = p_ref[0].astype(F32)
        for i in range(1, npart):
            g = g + p_ref[i].astype(F32)
        nm = ADAM_B1 * m_ref[...] + (1.0 - ADAM_B1) * g
        nv = ADAM_B2 * v_ref[...] + (1.0 - ADAM_B2) * (g * g)
        m_hat = nm / (1.0 - ADAM_B1 ** ADAM_STEP)
        v_hat = nv / (1.0 - ADAM_B2 ** ADAM_STEP)
        g_ref[...] = g
        d_ref[...] = -ADAM_LR * (m_hat / (jnp.sqrt(v_hat) + ADAM_EPS) + ADAM_WD * w_ref[...])
        nm_ref[...] = nm
        nv_ref[...] = nv

    spec = pl.BlockSpec((tr, c), lambda i: (i, 0))
    vmem = 2 * (npart + 7) * tr * c * 4 + (8 << 20)
    return pl.pallas_call(
        body, name=name, out_shape=[jax.ShapeDtypeStruct((r, c), F32)] * 4, grid=(r // tr,),
        in_specs=[pl.BlockSpec((npart, tr, c), lambda i: (0, i, 0)), spec, spec, spec], out_specs=[spec] * 4,
        compiler_params=_params(("parallel",), vmem),
    )(parts, w, m, v)


def _to_tm(a):
    return jnp.transpose(a, (1, 0, 2)).reshape(a.shape[1] * a.shape[0], a.shape[2])


def _pattern(mod_lat, mod_ctx, m, dm):
    lat = mod_lat[:, m * dm:(m + 1) * dm]
    ctx = jnp.broadcast_to(mod_ctx[None, m * dm:(m + 1) * dm], (SUBLANES, dm))
    return jnp.stack([ctx, jnp.concatenate([lat, lat], axis=0)])


def _blockdiag_b(bt, nk):
    g, h, p = bt.shape
    t = bt.reshape(nk, 8, h, p)
    return jnp.einsum("kghp,gj->kghjp", t, jnp.eye(8, dtype=bt.dtype)).reshape(nk, 8 * h, 8 * p)


def _blockdiag_c(ct, nk):
    g, h, p = ct.shape
    t = ct.reshape(nk, 8, h, p)
    return jnp.einsum("kghp,gj->kgpjh", t, jnp.eye(8, dtype=ct.dtype)).reshape(nk, 8 * p, 8 * h)


def _diag_b(dbb, h, p):
    nk = dbb.shape[0]
    return jnp.einsum("kghgp->kghp", dbb.reshape(nk, 8, h, 8, p)).reshape(nk * 8, h, p)


def _s5_prep(lam_re, lam_im, log_step, b_re, b_im, c_re, c_im, dm):
    ngrp, nk = dm // S5_GROUP, dm // LANES
    sw = ngrp * S5_STATE
    lr4 = lam_re.reshape(2, ngrp, 1, S5_STATE)
    li4 = lam_im.reshape(2, ngrp, 1, S5_STATE)
    ls4 = log_step.reshape(2, ngrp, 1, 1)
    brt = jnp.transpose(b_re, (0, 1, 3, 2))
    bit = jnp.transpose(b_im, (0, 1, 3, 2))
    abar_r, abar_i, bbar_r, bbar_i = _s5_disc_fwd(lr4, li4, ls4, brt, bit)
    half = lambda a, d: jnp.broadcast_to(a[d].reshape(1, sw), (LOCAL_B, sw))
    tile = lambda a, first: jnp.concatenate([half(a, first), half(a, 1 - first)], axis=0)
    bb = tuple(_blockdiag_b(w[d], nk).astype(BF16) for d in range(2) for w in (bbar_r, bbar_i))
    cc = tuple(_blockdiag_c(w[d], nk).astype(BF16) for d in range(2) for w in (c_re, c_im))
    return dict(disc_in=(lr4, li4, ls4, brt, bit), af=(tile(abar_r, 0), tile(abar_i, 0)),
                ab=(tile(abar_r, 1), -tile(abar_i, 1)), bb=bb, cc=cc)


def _s5_param_grads(prep, dbb, dcc, da):
    lr4 = prep["disc_in"][0]
    dar = jnp.stack([da[0], da[2]]).reshape(lr4.shape)
    dai = jnp.stack([da[1], da[3]]).reshape(lr4.shape)
    dbbr = jnp.stack([_diag_b(dbb[0], S5_GROUP, S5_STATE), _diag_b(dbb[2], S5_GROUP, S5_STATE)])
    dbbi = jnp.stack([_diag_b(dbb[1], S5_GROUP, S5_STATE), _diag_b(dbb[3], S5_GROUP, S5_STATE)])
    dlr, dli, dls, dbrt, dbit = _s5_disc_bwd(*prep["disc_in"], dar, dai, dbbr, dbbi)
    g_c_re = jnp.stack([_diag_b(dcc[0], S5_GROUP, S5_STATE), _diag_b(dcc[2], S5_GROUP, S5_STATE)])
    g_c_im = jnp.stack([_diag_b(dcc[1], S5_GROUP, S5_STATE), _diag_b(dcc[3], S5_GROUP, S5_STATE)])
    return dlr, dli, dls, jnp.transpose(dbrt, (0, 1, 3, 2)), jnp.transpose(dbit, (0, 1, 3, 2)), g_c_re, g_c_im


def kernel(x, c, ctx, c_ctx, w_mod, b_mod, norm1_w, norm2_w, final_norm_w, s5_w_in, s5_lam_re, s5_lam_im, s5_log_step, s5_b_re, s5_b_im, s5_c_re, s5_c_im, s5_d, s5_w_glu, s5_w_out, hg_w_in, hg_lower_bounds, hg_gnorm_w, hg_w_out, ffn_w_up, ffn_conv_w, ffn_conv_b, ffn_w_down, loss_target, m_c_ctx, m_w_mod, m_b_mod, m_norm1_w, m_norm2_w, m_final_norm_w, m_s5_w_in, m_s5_lam_re, m_s5_lam_im, m_s5_log_step, m_s5_b_re, m_s5_b_im, m_s5_c_re, m_s5_c_im, m_s5_d, m_s5_w_glu, m_s5_w_out, m_hg_w_in, m_hg_lower_bounds, m_hg_gnorm_w, m_hg_w_out, m_ffn_w_up, m_ffn_conv_w, m_ffn_conv_b, m_ffn_w_down, v_c_ctx, v_w_mod, v_b_mod, v_norm1_w, v_norm2_w, v_final_norm_w, v_s5_w_in, v_s5_lam_re, v_s5_lam_im, v_s5_log_step, v_s5_b_re, v_s5_b_im, v_s5_c_re, v_s5_c_im, v_s5_d, v_s5_w_glu, v_s5_w_out, v_hg_w_in, v_hg_lower_bounds, v_hg_gnorm_w, v_hg_w_out, v_ffn_w_up, v_ffn_conv_w, v_ffn_conv_b, v_ffn_w_down):
    given = dict(locals())
    bsz, lx, dm = x.shape
    lc = ctx.shape[1]
    assert bsz == LOCAL_B and w_mod.shape[0] == 2 and dm % LANES == 0
    n, nctx = (lc + lx) * bsz, lc * bsz
    ngrp, nstate, hgrp = dm // S5_GROUP, S5_STATE, S5_GROUP
    nk = dm // LANES
    dims = dict(nctx=nctx, tm=min(512, nctx), tm_row=min(512, nctx), s5_rows=min(256, nctx),
                s5_lane_block=min(512, 8 * nstate))
    tm = dims["tm"]
    assert nctx % HG_ROWS == 0 and (lx * bsz) % nctx == 0 and lc % GRID_W == 0 and lc & (lc - 1) == 0
    me = 4 * lax.axis_index("x") + 2 * lax.axis_index("y") + lax.axis_index("c")

    gath = _exchange([given[k].astype(BF16) for k in ("s5_w_in", "s5_w_glu", "s5_w_out")]
                     + [c, hg_lower_bounds, ffn_conv_w], a2a=False, name="gather_weights")
    w_s5in, w_glu, w_s5out = (g.reshape(dm, dm) for g in gath[:3])
    c_all, lb_all, cw_all = gath[3:]
    ns_up = ffn_w_up.shape[2]
    w_up, w_dn = [None, None], [None, None]
    cols = lambda g: jnp.transpose(g, (1, 0, 2)).reshape(g.shape[1], -1)
    shards = lambda w: jnp.transpose(w.reshape(w.shape[0], NDEV, -1), (1, 0, 2))
    tn_up, tn_hg, tkr = 2 * ns_up, 2 * hg_w_in.shape[2], 2304
    assert n % tkr == 0 and n % 1024 == 0
    gather = lambda arrs: _Xchg([a.astype(BF16) for a in arrs], [False] * len(arrs))
    scatter = lambda arrs: _Xchg(arrs, [True] * len(arrs))
    cw = [cols(cw_all[:, layer]) for layer in range(2)]
    cb = [ffn_conv_b[layer].reshape(1, -1) for layer in range(2)]
    lb2 = jnp.transpose(lb_all, (1, 2, 0, 3)).reshape(2, 2, 1, dm)

    nsm = w_mod.shape[2]
    craw = jnp.concatenate([c_all.reshape(NDEV * bsz, dm), c_ctx[None], jnp.zeros((SUBLANES - 1, dm), F32)], axis=0)
    b_sh = lax.dynamic_slice(b_mod, (0, me * nsm), (2, nsm)).reshape(2, 1, nsm)
    mod_sh = _mod_fwd(craw, w_mod, b_sh)
    (mod_g,) = _exchange([mod_sh], a2a=False, name="gather_mod")
    mod_full = jnp.transpose(mod_g, (1, 2, 0, 3)).reshape(2, craw.shape[0], NDEV * nsm)
    pat = []
    for layer in range(2):
        mlat = lax.dynamic_slice(mod_full[layer], (me * bsz, 0), (bsz, N_MOD * dm))
        mctx = mod_full[layer, NDEV * bsz]
        pat.append([_pattern(mlat, mctx, m, dm) for m in range(N_MOD)])

    s5p = _s5_prep(s5_lam_re[0], s5_lam_im[0], s5_log_step[0], s5_b_re[0], s5_b_im[0], s5_c_re[0], s5_c_im[0], dm)
    dsk = s5_d.reshape(1, dm)

    z0 = jnp.concatenate([_to_tm(ctx), _to_tm(x)], axis=0)
    tgt = _to_tm(loss_target)
    n1w = [norm1_w[layer].reshape(1, dm) for layer in range(2)]
    n2w = [norm2_w[layer].reshape(1, dm) for layer in range(2)]

    def ffn_fwd(layer, h2):
        u = _lin(h2, w_up[layer], name=f"ffn_up{layer}", tm=1024, tn=tn_up, o_dtype=BF16)
        hm = _convffn_fwd(u, cw[layer], cb[layer], name=f"convffn_fwd{layer}", dims=dims)
        f = _lin(hm, w_dn[layer], name=f"ffn_down{layer}", tm=tm, tn=dm, o_dtype=BF16)
        return u, hm, f

    _, h0 = _norm_mod_fwd(z0, n1w[0], pat[0][0], pat[0][1], name="norm1_l0", dims=dims)
    u_s5 = _lin(h0, w_s5in, name="s5_in", tm=1024, tn=dm, o_dtype=BF16)
    (y_s5a, y_s5b, st_r, st_i), (g_up0, g_dn0, g_hgin, g_hgout, g_dn1) = _s5_fwd(
        u_s5, *s5p["af"], s5p["bb"], s5p["cc"], dsk, name="s5_fwd", dims=dims,
        xch=gather([ffn_w_up[0], ffn_w_down[0], hg_w_in[0], hg_w_out[0], ffn_w_down[1]]))
    w_up[0], w_dn[0] = cols(g_up0), g_dn0.reshape(-1, dm)
    w_hgin, w_hgout, w_dn[1] = cols(g_hgin), g_hgout.reshape(dm, dm), g_dn1.reshape(-1, dm)
    (zg,) = _rowk(lambda rv, pv, cv, il: ([_gelu(rv[0] + rv[1])], [], []), name="s5_gelu", n=n, tm=dims["tm_row"],
                  nctx=nctx, rows=[(y_s5a, dm, 0, 0), (y_s5b, dm, 0, 0)], out_rows=[(dm, BF16, dm, 0)])
    t_glu = _lin(zg, w_glu, name="s5_glu", tm=1024, tn=dm, o_dtype=BF16)
    (z2g,) = _rowk(lambda rv, pv, cv, il: ([rv[0] * _sigmoid(rv[1])], [], []), name="s5_gate", n=n,
                   tm=dims["tm_row"], nctx=nctx, rows=[(zg, dm, 0, 0), (t_glu, dm, 0, 0)],
                   out_rows=[(dm, BF16, dm, 0)])
    ymix0 = _lin(z2g, w_s5out, name="s5_out", tm=1024, tn=dm, o_dtype=BF16)
    z1_l0, h2_l0 = _norm_mod_fwd(z0, n2w[0], pat[0][3], pat[0][4], name="norm2_l0", dims=dims,
                                 res=(ymix0, pat[0][2]))
    u_l0, hm_l0, f_l0 = ffn_fwd(0, h2_l0)

    z2_l0, h1 = _norm_mod_fwd(z1_l0, n1w[1], pat[1][0], pat[1][1], name="norm1_l1", dims=dims,
                              res=(f_l0, pat[0][5]))
    zz = _lin(h1, w_hgin, name="hg_in", tm=1024, tn=tn_hg, o_dtype=BF16)
    (o_f, sts_f), (g_up1,) = _hg_fwd_dir(zz, lb2, d=0, name="hg_fwd_d0", dims=dims, xch=gather([ffn_w_up[1]]))
    w_up[1] = cols(g_up1)
    (o_b, sts_b), _ = _hg_fwd_dir(zz, lb2, d=1, name="hg_fwd_d1", dims=dims)
    gnw = hg_gnorm_w.reshape(1, HG_HEAD)
    (og,) = _rowk(lambda rv, pv, cv, il: ([_hg_readout(rv[0] + rv[1], rv[2], cv[0])], [], []), name="hg_readout",
                  n=n, tm=dims["tm_row"], nctx=nctx, rows=[(o_f, dm, 0, 0), (o_b, dm, 0, 0), (zz, dm, 4, 0)],
                  consts=[gnw], out_rows=[(dm, BF16, dm, 0)])
    ymix1 = _lin(og, w_hgout, name="hg_out", tm=1024, tn=dm, o_dtype=BF16)
    z1_l1, h2_l1 = _norm_mod_fwd(z2_l0, n2w[1], pat[1][3], pat[1][4], name="norm2_l1", dims=dims,
                                 res=(ymix1, pat[1][2]))
    u_l1, hm_l1, f_l1 = ffn_fwd(1, h2_l1)

    dz, df, dgate2_l1, loss_part, dfinal_w = _loss_bwd(z1_l1, f_l1, pat[1][5], tgt, final_norm_w.reshape(1, dm),
                                                        name="loss_bwd", dims=dims)

    def ffn_bwd(layer, df_, u, hm, h2, xch=None):
        dff = hm.shape[1]
        dhm = _lin(df_, w_dn[layer], name=f"ffn_down_bwd_in{layer}", trans_w=True, tm=1024, tn=dff // 2, o_dtype=BF16)
        dwd = _lin_w(hm, df_, name=f"ffn_down_bwd_w{layer}", ta=dff // 2, tn=dm, tkr=tkr)
        (dua, dug, dcwa, dcwg, dcba, dcbg), got = _convffn_bwd(u, dhm, cw[layer], cb[layer],
                                                               name=f"convffn_bwd{layer}", dims=dims, xch=xch)
        dh2 = _lin_cat((dua, dug), w_up[layer], name=f"ffn_up_bwd_in{layer}", tm=tm, o_dtype=BF16)
        dwu = jnp.concatenate([_lin_w(h2, dua, name=f"ffn_up_bwd_w_a{layer}", ta=dm, tn=tn_up, tkr=tkr),
                               _lin_w(h2, dug, name=f"ffn_up_bwd_w_g{layer}", ta=dm, tn=tn_up, tkr=tkr)], axis=1)
        dcw = shards(jnp.concatenate([dcwa, dcwg], axis=1))
        return dh2, shards(dwu), dwd, dcw, jnp.concatenate([dcba, dcbg], axis=1), got

    dh2, dwu_l1, dwd_l1, dcw_l1, dcb_l1, _ = ffn_bwd(1, df, u_l1, hm_l1, h2_l1)
    dz, dymix, dsh2_l1, dsc2_l1, dgate1_l1, dn2w_l1 = _norm_mod_bwd(dh2, z1_l1, dz, n2w[1], pat[1][4], name="norm2_bwd_l1",
                                                                    dims=dims, res=(ymix1, pat[1][2]))
    dog = _lin(dymix, w_hgout, name="hg_out_bwd_in", trans_w=True, tm=1024, tn=dm, o_dtype=BF16)
    dw_hgout = _lin_w(og, dymix, name="hg_out_bwd_w", ta=dm, tn=dm, tkr=tkr)

    def readout_bwd(rv, pv, cv, il):
        _, vjp = jax.vjp(_hg_readout, rv[0] + rv[1], rv[2], cv[0])
        do, dg, dw = vjp(rv[3])
        return [do, dg], [], [jnp.broadcast_to(dw, (SUBLANES, HG_HEAD)) * (1.0 / SUBLANES)]

    do, dg, dgnw = _rowk(readout_bwd, name="hg_readout_bwd", n=n, tm=dims["tm_row"], nctx=nctx,
                         rows=[(o_f, dm, 0, 0), (o_b, dm, 0, 0), (zz, dm, 4, 0), (dog, dm, 0, 0)], consts=[gnw],
                         out_rows=[(dm, BF16, dm, 0), (dm, BF16, dm, 0)], out_acc=[HG_HEAD])
    (dq0, dv0, dff, dl_f), (p_up1, p_dn1) = _hg_bwd_dir(
        zz, lb2, do, sts_f, None, d=0, name="hg_bwd_d0", dims=dims,
        xch=scatter([dwu_l1, dwd_l1.reshape(NDEV, -1, dm)]))
    (dq, dv, dfb, dl_b), (p_hgout,) = _hg_bwd_dir(
        zz, lb2, do, sts_b, (dq0, dv0), d=1, name="hg_bwd_d1", dims=dims,
        xch=scatter([dw_hgout.reshape(NDEV, -1, dm)]))
    pieces = (dq, dv, dff, dfb, dg)
    dh1 = _lin_cat(pieces, w_hgin, name="hg_in_bwd_in", tm=tm, o_dtype=BF16)
    dw_hgin = shards(jnp.concatenate([_lin_w(h1, piece, name=f"hg_in_bwd_w{p}", ta=dm, tn=dm, tkr=tkr)
                                      for p, piece in enumerate(pieces)], axis=1))
    dz, df0, dsh1_l1, dsc1_l1, dgate2_l0, dn1w_l1 = _norm_mod_bwd(dh1, z2_l0, dz, n1w[1], pat[1][1], name="norm1_bwd_l1",
                                                                  dims=dims, res=(f_l0, pat[0][5]))
    dh2, dwu_l0, dwd_l0, dcw_l0, dcb_l0, _ = ffn_bwd(0, df0, u_l0, hm_l0, h2_l0)
    dz, dymix, dsh2_l0, dsc2_l0, dgate1_l0, dn2w_l0 = _norm_mod_bwd(dh2, z1_l0, dz, n2w[0], pat[0][4], name="norm2_bwd_l0",
                                                                    dims=dims, res=(ymix0, pat[0][2]))
    dz2g = _lin(dymix, w_s5out, name="s5_out_bwd_in", trans_w=True, tm=1024, tn=dm, o_dtype=BF16)
    dw_s5out = _lin_w(z2g, dymix, name="s5_out_bwd_w", ta=dm, tn=dm, tkr=tkr)

    def gate_bwd(rv, pv, cv, il):
        sg = _sigmoid(rv[1])
        return [rv[2] * rv[0] * sg * (1.0 - sg), rv[2] * sg], [], []

    dt_glu, dzg_a = _rowk(gate_bwd, name="s5_gate_bwd", n=n, tm=dims["tm_row"], nctx=nctx,
                          rows=[(zg, dm, 0, 0), (t_glu, dm, 0, 0), (dz2g, dm, 0, 0)],
                          out_rows=[(dm, BF16, dm, 0), (dm, BF16, dm, 0)])
    dzg_b = _lin(dt_glu, w_glu, name="s5_glu_bwd_in", trans_w=True, tm=1024, tn=dm, o_dtype=BF16)
    dw_glu = _lin_w(zg, dt_glu, name="s5_glu_bwd_w", ta=dm, tn=dm, tkr=tkr)

    def gelu_bwd(rv, pv, cv, il):
        _, vjp = jax.vjp(_gelu, rv[0] + rv[1])
        return [vjp(rv[2] + rv[3])[0]], [], []

    (dy_s5,) = _rowk(gelu_bwd, name="s5_gelu_bwd", n=n, tm=dims["tm_row"], nctx=nctx,
                     rows=[(y_s5a, dm, 0, 0), (y_s5b, dm, 0, 0), (dzg_a, dm, 0, 0), (dzg_b, dm, 0, 0)],
                     out_rows=[(dm, F32, dm, 0)])
    dcw_both = jnp.stack([dcw_l0, dcw_l1], axis=1)
    s5g, (p_up0, p_dn0, p_cw, p_s5out, p_glu, p_hgin) = _s5_bwd(
        u_s5, dy_s5, *s5p["af"], *s5p["ab"], s5p["bb"], s5p["cc"], dsk, st_r, st_i, name="s5_bwd", dims=dims,
        xch=scatter([dwu_l0, dwd_l0.reshape(NDEV, -1, dm), dcw_both, dw_s5out.reshape(NDEV, -1, dm),
                     dw_glu.reshape(NDEV, -1, dm), dw_hgin]))
    (du_s5,) = _rowk(lambda rv, pv, cv, il: ([rv[0] + rv[1]], [], []), name="s5_du", n=n, tm=dims["tm_row"], nctx=nctx,
                     rows=[(s5g[0], dm, 0, 0), (s5g[1], dm, 0, 0)], out_rows=[(dm, BF16, dm, 0)])
    ddsk = s5g[14]
    dh0 = _lin(du_s5, w_s5in, name="s5_in_bwd_in", trans_w=True, tm=1024, tn=dm, o_dtype=BF16)
    dw_s5in = _lin_w(h0, du_s5, name="s5_in_bwd_w", ta=dm, tn=dm, tkr=tkr)
    dz0, dsh1_l0, dsc1_l0, dn1w_l0 = _norm_mod_bwd(dh0, z0, dz, n1w[0], pat[0][1], name="norm1_bwd_l0", dims=dims)

    dlr, dli, dls, g_b_re, g_b_im, g_c_re, g_c_im = _s5_param_grads(s5p, s5g[2:6], s5g[6:10], s5g[10:14])

    dmod = jnp.stack([
        jnp.concatenate([dsh1_l0, dsc1_l0, dgate1_l0, dsh2_l0, dsc2_l0, dgate2_l0], axis=1),
        jnp.concatenate([dsh1_l1, dsc1_l1, dgate1_l1, dsh2_l1, dsc2_l1, dgate2_l1], axis=1)])
    dl_hg = jnp.stack([dl_f[:, 0], dl_b[:, 0]])
    wide = lambda g: g.reshape(-1, dm)
    small = [("norm1_w", jnp.concatenate([dn1w_l0, dn1w_l1])),
             ("norm2_w", jnp.concatenate([dn2w_l0, dn2w_l1])), ("final_norm_w", dfinal_w),
             ("s5_lam_re", dlr.reshape(-1, nstate)), ("s5_lam_im", dli.reshape(-1, nstate)),
             ("s5_log_step", dls.reshape(2, ngrp)),
             ("s5_b_re", wide(g_b_re.astype(BF16))), ("s5_b_im", wide(g_b_im.astype(BF16))),
             ("s5_c_re", wide(g_c_re.astype(BF16))), ("s5_c_im", wide(g_c_im.astype(BF16))), ("s5_d", ddsk),
             ("hg_gnorm_w", dgnw), ("ffn_conv_b", jnp.stack([dcb_l0.reshape(-1), dcb_l1.reshape(-1)]))]
    big_small = ("s5_b_re", "s5_b_im", "s5_c_re", "s5_c_im")
    tail = _exchange([dmod, dw_s5in.reshape(NDEV, -1, dm)]
                     + [g.reshape(NDEV, -1, dm) if k in big_small else g for k, g in small] + [wide(dl_hg), loss_part],
                     a2a=[False, True] + [k in big_small for k, _ in small] + [False, False], name="gather_tail")
    dmod_g, p_s5in, gathered = tail[0], tail[1], tail[2:]

    def total(parts, name):
        z = jnp.zeros(parts.shape[1:], F32)
        return _adamw(parts, z, z, z, name=name)[0]

    mine_sum = {k: total(parts, "sum_" + k) for (k, _), parts in zip(small, gathered) if k in big_small}
    dlat_full = jnp.transpose(dmod_g[:, :, :bsz], (1, 0, 2, 3)).reshape(2, NDEV * bsz, N_MOD * dm)
    dctx_full = dmod_g[:, :, bsz]
    dlat_sh = lax.dynamic_slice(dlat_full, (0, 0, me * nsm), (2, NDEV * bsz, nsm))
    dctx_sh = lax.dynamic_slice(dctx_full, (0, 0, me * nsm), (NDEV, 2, nsm))
    g_w_mod, g_b_mod, dcctx8 = _mod_bwd(craw, w_mod, dlat_sh, dctx_sh, dlat_full, dctx_full)

    last = _exchange([wide(dcctx8[:1])] + [mine_sum[k] for k in big_small], a2a=False, name="gather_cctx")
    summed = {k: g.reshape(1, -1, dm) for k, g in zip(big_small, last[1:])}
    small, gathered = [("c_ctx", wide(dcctx8[:1]))] + small, [last[0]] + gathered
    res = {}
    for (k, g), parts in zip(small, gathered):
        w2, m2, v2 = (given[p + k].reshape(g.shape) for p in ("", "m_", "v_"))
        res[k] = tuple(o.reshape(given[k].shape)
                       for o in _adamw(summed.get(k, parts), w2, m2, v2, name="adamw_" + k))

    loss = jnp.sum(total(gathered[-1], "sum_loss"))
    dl_tot = total(gathered[-2], "sum_dlb").reshape(dl_hg.shape)
    nlb = hg_lower_bounds.shape[2]
    g_lb = lax.dynamic_slice(dl_tot, (0, 0, me * nlb), (2, 2, nlb))

    def adam_local(name, g, shape2):
        w, m, v = given[name], given["m_" + name], given["v_" + name]
        out = _adamw(g.reshape((1,) + shape2), w.reshape(shape2), m.reshape(shape2), v.reshape(shape2),
                     name="adamw_" + name)
        return tuple(o.reshape(w.shape) for o in out)

    def adam_parts(name, p):
        w, m, v = given[name], given["m_" + name], given["v_" + name]
        shape2 = (p.shape[0], -1, w.shape[-1])
        p3 = p.reshape(shape2)
        s2 = p3.shape[1:]
        out = _adamw(p3, w.reshape(s2), m.reshape(s2), v.reshape(s2), name="adamw_" + name)
        return tuple(o.reshape(w.shape) for o in out)

    res["hg_lower_bounds"] = adam_local("hg_lower_bounds", g_lb, (2 * 2, nlb))
    res["w_mod"] = adam_local("w_mod", g_w_mod, (2 * dm, nsm))
    res["b_mod"] = adam_local("b_mod", g_b_mod, (2, N_MOD * dm))
    res["s5_w_in"] = adam_parts("s5_w_in", p_s5in)
    res["s5_w_glu"] = adam_parts("s5_w_glu", p_glu)
    res["s5_w_out"] = adam_parts("s5_w_out", p_s5out)
    res["hg_w_in"] = adam_parts("hg_w_in", p_hgin)
    res["hg_w_out"] = adam_parts("hg_w_out", p_hgout)
    res["ffn_w_up"] = adam_parts("ffn_w_up", jnp.stack([p_up0, p_up1], axis=1))
    res["ffn_w_down"] = adam_parts("ffn_w_down", jnp.stack([p_dn0, p_dn1], axis=1))
    res["ffn_conv_w"] = adam_parts("ffn_conv_w", p_cw)

    grad_x = jnp.transpose(dz0[nctx:].reshape(lx, bsz, dm), (1, 0, 2))
    order = ["c_ctx", "w_mod", "b_mod", "norm1_w", "norm2_w", "final_norm_w", "s5_w_in", "s5_lam_re", "s5_lam_im",
             "s5_log_step", "s5_b_re", "s5_b_im", "s5_c_re", "s5_c_im", "s5_d", "s5_w_glu", "s5_w_out", "hg_w_in",
             "hg_lower_bounds", "hg_gnorm_w", "hg_w_out", "ffn_w_up", "ffn_conv_w", "ffn_conv_b", "ffn_w_down"]
    outs = [loss, grad_x]
    for j in range(4):
        outs += [res[k][j].reshape(given[k].shape) for k in order]
    return tuple(outs)
```
